```python
import math
import jax, jax.numpy as jnp
from jax import lax
import numpy as np

D_MODEL = 1024
BATCH = 8
SEQ = 4096
DEPTH = 1

CHUNK = 64
N_MEM = 256
D_CONV = D_MODEL
CONV_K = 31
D_SSM = D_MODEL // 2
SSM_GROUP = 16
SSM_GROUPS = D_SSM // SSM_GROUP
SSM_STATE = 64
XA_HEADS = 4
XA_HEAD_DIM = D_MODEL // XA_HEADS
D_FF = 4 * D_MODEL
D_IN = 2 * D_CONV + D_SSM + 2 * D_MODEL
LN_EPS = 1e-5
DEEPNORM_ALPHA = (2.0 * DEPTH) ** 0.25
DEEPNORM_BETA = (8.0 * DEPTH) ** -0.25

kernel_name = "gated_conformer_s5_memxattn_deepnorm"


def layer_norm(x, g, b):
    xf = x.astype(jnp.float32)
    mu = jnp.mean(xf, axis=-1, keepdims=True)
    xc = xf - mu
    var = jnp.mean(xc * xc, axis=-1, keepdims=True)
    y = xc * lax.rsqrt(var + LN_EPS) * g.astype(jnp.float32) + b.astype(jnp.float32)
    return y.astype(x.dtype)


def conformer_conv_branch(val, gate, dw, db, ng, nb, w_out):
    u = val * jax.nn.sigmoid(gate)
    c = lax.conv_general_dilated(
        u, dw[:, None, :].astype(u.dtype),
        window_strides=(1,), padding=[(CONV_K - 1, 0)],
        dimension_numbers=("NWC", "WIO", "NWC"),
        feature_group_count=D_CONV) + db
    c = layer_norm(c, ng, nb)
    c = jax.nn.silu(c)
    return c @ w_out


def _ssm_combine(left, right):
    a1r, a1i, b1r, b1i = left
    a2r, a2i, b2r, b2i = right
    ar = a2r * a1r - a2i * a1i
    ai = a2r * a1i + a2i * a1r
    br = a2r * b1r - a2i * b1i + b2r
    bi = a2r * b1i + a2i * b1r + b2i
    return (ar, ai, br, bi)


def s5_branch(u, log_step, lam_re, lam_im, b_re, b_im, c_re, c_im, d, w_glu):
    bsz, s, _ = u.shape
    uf = u.astype(jnp.float32).reshape(bsz, s, SSM_GROUPS, SSM_GROUP)
    step = jnp.exp(log_step.astype(jnp.float32))[:, None]
    lr = lam_re.astype(jnp.float32)
    li = lam_im.astype(jnp.float32)
    mag = jnp.exp(lr * step)
    ar = mag * jnp.cos(li * step)
    ai = mag * jnp.sin(li * step)
    den = lr * lr + li * li
    nr = ar - 1.0
    cr = (nr * lr + ai * li) / den
    ci = (ai * lr - nr * li) / den
    br = b_re.astype(jnp.float32)
    bi = b_im.astype(jnp.float32)
    bbr = cr[..., None] * br - ci[..., None] * bi
    bbi = cr[..., None] * bi + ci[..., None] * br
    bu_r = jnp.einsum('bsgh,gph->bsgp', uf, bbr)
    bu_i = jnp.einsum('bsgh,gph->bsgp', uf, bbi)
    a_r = jnp.broadcast_to(ar, bu_r.shape)
    a_i = jnp.broadcast_to(ai, bu_i.shape)
    _, _, xr, xi = lax.associative_scan(_ssm_combine, (a_r, a_i, bu_r, bu_i), axis=1)
    y = (jnp.einsum('bsgp,ghp->bsgh', xr, c_re.astype(jnp.float32))
         - jnp.einsum('bsgp,ghp->bsgh', xi, c_im.astype(jnp.float32))
         + d.astype(jnp.float32).reshape(SSM_GROUPS, SSM_GROUP) * uf)
    y = y.reshape(bsz, s, D_SSM).astype(u.dtype)
    z = y @ w_glu
    return z[..., :D_MODEL] * jax.nn.sigmoid(z[..., D_MODEL:])


def hybrid_mixer(h, w_in, conv_dw, conv_db, conv_norm_g, conv_norm_b, w_conv_out,
                 log_step, lam_re, lam_im, b_re, b_im, c_re, c_im, d, w_ssm_glu, w_mix_out):
    p = h @ w_in
    o0 = D_CONV
    o1 = 2 * D_CONV
    o2 = o1 + D_SSM
    o3 = o2 + D_MODEL
    conv_val, conv_gate = p[..., :o0], p[..., o0:o1]
    ssm_in = p[..., o1:o2]
    gate_a, gate_b = p[..., o2:o3], p[..., o3:]
    y_a = conformer_conv_branch(conv_val, conv_gate, conv_dw, conv_db,
                                conv_norm_g, conv_norm_b, w_conv_out)
    y_b = s5_branch(ssm_in, log_step, lam_re, lam_im, b_re, b_im, c_re, c_im, d, w_ssm_glu)
    merged = jax.nn.sigmoid(gate_a) * y_a + jax.nn.sigmoid(gate_b) * y_b
    return merged @ w_mix_out


def memory_cross_attention(h, mem, wq, wkv, wo):
    bsz, s, _ = h.shape
    q = (h @ wq).reshape(bsz, s, XA_HEADS, XA_HEAD_DIM)
    kv = mem @ wkv
    k = kv[..., :D_MODEL].reshape(bsz, N_MEM, XA_HEADS, XA_HEAD_DIM)
    v = kv[..., D_MODEL:].reshape(bsz, N_MEM, XA_HEADS, XA_HEAD_DIM)
    scores = jnp.einsum('bshd,bmhd->bhsm', q.astype(jnp.float32), k.astype(jnp.float32))
    probs = jax.nn.softmax(scores * (XA_HEAD_DIM ** -0.5), axis=-1).astype(h.dtype)
    o = jnp.einsum('bhsm,bmhd->bshd', probs, v).reshape(bsz, s, D_MODEL)
    return o @ wo


def sq_relu_mlp(h, w_up, w_down):
    z = jax.nn.relu(h @ w_up)
    return (z * z) @ w_down


def _fwd_setup_inputs(seed: int = 0) -> dict:
    key = jax.random.key(seed)
    ks = jax.random.split(key, 32)
    f32 = jnp.float32
    L = DEPTH
    nrm = lambda k, shape, scale: jax.random.normal(k, shape, f32) * scale
    gain = lambda k, shape: 1.0 + 0.02 * jax.random.normal(k, shape, f32)
    bias = lambda k, shape: 0.02 * jax.random.normal(k, shape, f32)
    n_idx = jnp.arange(SSM_STATE, dtype=f32)
    lam_re = -0.5 + 0.01 * jax.random.normal(ks[11], (L, SSM_GROUPS, SSM_STATE), f32)
    lam_im = math.pi * n_idx + 0.01 * jax.random.normal(ks[12], (L, SSM_GROUPS, SSM_STATE), f32)
    log_step = jax.random.uniform(ks[10], (L, SSM_GROUPS), f32,
                                  math.log(1e-3), math.log(1e-1))
    return {
        "x": jax.random.normal(ks[0], (BATCH, SEQ, D_MODEL), f32),
        "mem": jax.random.normal(ks[1], (BATCH, N_MEM, D_MODEL), f32),
        "in_norm_g": gain(ks[2], (D_MODEL,)),
        "in_norm_b": bias(ks[3], (D_MODEL,)),
        "w_in": nrm(ks[4], (L, D_MODEL, D_IN), D_MODEL ** -0.5),
        "conv_dw": nrm(ks[5], (L, CONV_K, D_CONV), CONV_K ** -0.5),
        "conv_db": bias(ks[6], (L, D_CONV)),
        "conv_norm_g": gain(ks[7], (L, D_CONV)),
        "conv_norm_b": bias(ks[8], (L, D_CONV)),
        "w_conv_out": nrm(ks[9], (L, D_CONV, D_MODEL), D_CONV ** -0.5),
        "ssm_log_step": log_step,
        "ssm_lambda_re": lam_re,
        "ssm_lambda_im": lam_im,
        "ssm_b_re": nrm(ks[13], (L, SSM_GROUPS, SSM_STATE, SSM_GROUP), (2.0 * SSM_GROUP) ** -0.5),
        "ssm_b_im": nrm(ks[14], (L, SSM_GROUPS, SSM_STATE, SSM_GROUP), (2.0 * SSM_GROUP) ** -0.5),
        "ssm_c_re": nrm(ks[15], (L, SSM_GROUPS, SSM_GROUP, SSM_STATE), (2.0 * SSM_STATE) ** -0.5),
        "ssm_c_im": nrm(ks[16], (L, SSM_GROUPS, SSM_GROUP, SSM_STATE), (2.0 * SSM_STATE) ** -0.5),
        "ssm_d": nrm(ks[17], (L, D_SSM), 1.0),
        "w_ssm_glu": nrm(ks[18], (L, D_SSM, 2 * D_MODEL), D_SSM ** -0.5),
        "w_mix_out": nrm(ks[19], (L, D_MODEL, D_MODEL), DEEPNORM_BETA * D_MODEL ** -0.5),
        "ln1_g": gain(ks[20], (L, D_MODEL)),
        "ln1_b": bias(ks[21], (L, D_MODEL)),
        "xa_wq": nrm(ks[22], (L, D_MODEL, D_MODEL), D_MODEL ** -0.5),
        "xa_wkv": nrm(ks[23], (L, D_MODEL, 2 * D_MODEL), D_MODEL ** -0.5),
        "xa_wo": nrm(ks[24], (L, D_MODEL, D_MODEL), DEEPNORM_BETA * D_MODEL ** -0.5),
        "ln2_g": gain(ks[25], (L, D_MODEL)),
        "ln2_b": bias(ks[26], (L, D_MODEL)),
        "mlp_w_up": nrm(ks[27], (L, D_MODEL, D_FF), D_MODEL ** -0.5),
        "mlp_w_down": nrm(ks[28], (L, D_FF, D_MODEL), DEEPNORM_BETA * D_FF ** -0.5),
        "ln3_g": gain(ks[29], (L, D_MODEL)),
        "ln3_b": bias(ks[30], (L, D_MODEL)),
    }


def _fwd_reference(x, mem, in_norm_g, in_norm_b, w_in, conv_dw, conv_db, conv_norm_g, conv_norm_b,
              w_conv_out, ssm_log_step, ssm_lambda_re, ssm_lambda_im, ssm_b_re, ssm_b_im,
              ssm_c_re, ssm_c_im, ssm_d, w_ssm_glu, w_mix_out, ln1_g, ln1_b,
              xa_wq, xa_wkv, xa_wo, ln2_g, ln2_b, mlp_w_up, mlp_w_down, ln3_g, ln3_b):
    h = layer_norm(x, in_norm_g, in_norm_b)
    for l in range(DEPTH):
        mix = hybrid_mixer(h, w_in[l], conv_dw[l], conv_db[l], conv_norm_g[l], conv_norm_b[l],
                           w_conv_out[l], ssm_log_step[l], ssm_lambda_re[l], ssm_lambda_im[l],
                           ssm_b_re[l], ssm_b_im[l], ssm_c_re[l], ssm_c_im[l], ssm_d[l],
                           w_ssm_glu[l], w_mix_out[l])
        h = layer_norm(DEEPNORM_ALPHA * h + mix, ln1_g[l], ln1_b[l])
        xa = memory_cross_attention(h, mem, xa_wq[l], xa_wkv[l], xa_wo[l])
        h = layer_norm(DEEPNORM_ALPHA * h + xa, ln2_g[l], ln2_b[l])
        ff = sq_relu_mlp(h, mlp_w_up[l], mlp_w_down[l])
        h = layer_norm(DEEPNORM_ALPHA * h + ff, ln3_g[l], ln3_b[l])
    return h


import jax as _jax
import jax.numpy as _jnp

TWIN_FORMAT = 'train_step'
FWD_PARAMS = ['x', 'mem', 'in_norm_g', 'in_norm_b', 'w_in', 'conv_dw', 'conv_db', 'conv_norm_g', 'conv_norm_b', 'w_conv_out', 'ssm_log_step', 'ssm_lambda_re', 'ssm_lambda_im', 'ssm_b_re', 'ssm_b_im', 'ssm_c_re', 'ssm_c_im', 'ssm_d', 'w_ssm_glu', 'w_mix_out', 'ln1_g', 'ln1_b', 'xa_wq', 'xa_wkv', 'xa_wo', 'ln2_g', 'ln2_b', 'mlp_w_up', 'mlp_w_down', 'ln3_g', 'ln3_b']
TWIN_WEIGHTS = ['in_norm_g', 'in_norm_b', 'w_in', 'conv_dw', 'conv_db', 'conv_norm_g', 'conv_norm_b', 'w_conv_out', 'ssm_log_step', 'ssm_lambda_re', 'ssm_lambda_im', 'ssm_b_re', 'ssm_b_im', 'ssm_c_re', 'ssm_c_im', 'ssm_d', 'w_ssm_glu', 'w_mix_out', 'ln1_g', 'ln1_b', 'xa_wq', 'xa_wkv', 'xa_wo', 'ln2_g', 'ln2_b', 'mlp_w_up', 'mlp_w_down', 'ln3_g', 'ln3_b']
TWIN_DIFF_INPUT = 'x'
TWIN_INPUTS = ['x', 'mem', 'in_norm_g', 'in_norm_b', 'w_in', 'conv_dw', 'conv_db', 'conv_norm_g', 'conv_norm_b', 'w_conv_out', 'ssm_log_step', 'ssm_lambda_re', 'ssm_lambda_im', 'ssm_b_re', 'ssm_b_im', 'ssm_c_re', 'ssm_c_im', 'ssm_d', 'w_ssm_glu', 'w_mix_out', 'ln1_g', 'ln1_b', 'xa_wq', 'xa_wkv', 'xa_wo', 'ln2_g', 'ln2_b', 'mlp_w_up', 'mlp_w_down', 'ln3_g', 'ln3_b', 'loss_target', 'm_in_norm_g', 'm_in_norm_b', 'm_w_in', 'm_conv_dw', 'm_conv_db', 'm_conv_norm_g', 'm_conv_norm_b', 'm_w_conv_out', 'm_ssm_log_step', 'm_ssm_lambda_re', 'm_ssm_lambda_im', 'm_ssm_b_re', 'm_ssm_b_im', 'm_ssm_c_re', 'm_ssm_c_im', 'm_ssm_d', 'm_w_ssm_glu', 'm_w_mix_out', 'm_ln1_g', 'm_ln1_b', 'm_xa_wq', 'm_xa_wkv', 'm_xa_wo', 'm_ln2_g', 'm_ln2_b', 'm_mlp_w_up', 'm_mlp_w_down', 'm_ln3_g', 'm_ln3_b', 'v_in_norm_g', 'v_in_norm_b', 'v_w_in', 'v_conv_dw', 'v_conv_db', 'v_conv_norm_g', 'v_conv_norm_b', 'v_w_conv_out', 'v_ssm_log_step', 'v_ssm_lambda_re', 'v_ssm_lambda_im', 'v_ssm_b_re', 'v_ssm_b_im', 'v_ssm_c_re', 'v_ssm_c_im', 'v_ssm_d', 'v_w_ssm_glu', 'v_w_mix_out', 'v_ln1_g', 'v_ln1_b', 'v_xa_wq', 'v_xa_wkv', 'v_xa_wo', 'v_ln2_g', 'v_ln2_b', 'v_mlp_w_up', 'v_mlp_w_down', 'v_ln3_g', 'v_ln3_b']
TWIN_OUTPUTS = ['loss', 'grad_x', 'grad_in_norm_g', 'grad_in_norm_b', 'grad_w_in', 'grad_conv_dw', 'grad_conv_db', 'grad_conv_norm_g', 'grad_conv_norm_b', 'grad_w_conv_out', 'grad_ssm_log_step', 'grad_ssm_lambda_re', 'grad_ssm_lambda_im', 'grad_ssm_b_re', 'grad_ssm_b_im', 'grad_ssm_c_re', 'grad_ssm_c_im', 'grad_ssm_d', 'grad_w_ssm_glu', 'grad_w_mix_out', 'grad_ln1_g', 'grad_ln1_b', 'grad_xa_wq', 'grad_xa_wkv', 'grad_xa_wo', 'grad_ln2_g', 'grad_ln2_b', 'grad_mlp_w_up', 'grad_mlp_w_down', 'grad_ln3_g', 'grad_ln3_b', 'delta_in_norm_g', 'delta_in_norm_b', 'delta_w_in', 'delta_conv_dw', 'delta_conv_db', 'delta_conv_norm_g', 'delta_conv_norm_b', 'delta_w_conv_out', 'delta_ssm_log_step', 'delta_ssm_lambda_re', 'delta_ssm_lambda_im', 'delta_ssm_b_re', 'delta_ssm_b_im', 'delta_ssm_c_re', 'delta_ssm_c_im', 'delta_ssm_d', 'delta_w_ssm_glu', 'delta_w_mix_out', 'delta_ln1_g', 'delta_ln1_b', 'delta_xa_wq', 'delta_xa_wkv', 'delta_xa_wo', 'delta_ln2_g', 'delta_ln2_b', 'delta_mlp_w_up', 'delta_mlp_w_down', 'delta_ln3_g', 'delta_ln3_b', 'new_m_in_norm_g', 'new_m_in_norm_b', 'new_m_w_in', 'new_m_conv_dw', 'new_m_conv_db', 'new_m_conv_norm_g', 'new_m_conv_norm_b', 'new_m_w_conv_out', 'new_m_ssm_log_step', 'new_m_ssm_lambda_re', 'new_m_ssm_lambda_im', 'new_m_ssm_b_re', 'new_m_ssm_b_im', 'new_m_ssm_c_re', 'new_m_ssm_c_im', 'new_m_ssm_d', 'new_m_w_ssm_glu', 'new_m_w_mix_out', 'new_m_ln1_g', 'new_m_ln1_b', 'new_m_xa_wq', 'new_m_xa_wkv', 'new_m_xa_wo', 'new_m_ln2_g', 'new_m_ln2_b', 'new_m_mlp_w_up', 'new_m_mlp_w_down', 'new_m_ln3_g', 'new_m_ln3_b', 'new_v_in_norm_g', 'new_v_in_norm_b', 'new_v_w_in', 'new_v_conv_dw', 'new_v_conv_db', 'new_v_conv_norm_g', 'new_v_conv_norm_b', 'new_v_w_conv_out', 'new_v_ssm_log_step', 'new_v_ssm_lambda_re', 'new_v_ssm_lambda_im', 'new_v_ssm_b_re', 'new_v_ssm_b_im', 'new_v_ssm_c_re', 'new_v_ssm_c_im', 'new_v_ssm_d', 'new_v_w_ssm_glu', 'new_v_w_mix_out', 'new_v_ln1_g', 'new_v_ln1_b', 'new_v_xa_wq', 'new_v_xa_wkv', 'new_v_xa_wo', 'new_v_ln2_g', 'new_v_ln2_b', 'new_v_mlp_w_up', 'new_v_mlp_w_down', 'new_v_ln3_g', 'new_v_ln3_b']
TWIN_LEAF_KINDS = {'loss': 'loss', 'grad_x': 'grad_x', 'grad_in_norm_g': 'grad_w', 'grad_in_norm_b': 'grad_w', 'grad_w_in': 'grad_w', 'grad_conv_dw': 'grad_w', 'grad_conv_db': 'grad_w', 'grad_conv_norm_g': 'grad_w', 'grad_conv_norm_b': 'grad_w', 'grad_w_conv_out': 'grad_w', 'grad_ssm_log_step': 'grad_w', 'grad_ssm_lambda_re': 'grad_w', 'grad_ssm_lambda_im': 'grad_w', 'grad_ssm_b_re': 'grad_w', 'grad_ssm_b_im': 'grad_w', 'grad_ssm_c_re': 'grad_w', 'grad_ssm_c_im': 'grad_w', 'grad_ssm_d': 'grad_w', 'grad_w_ssm_glu': 'grad_w', 'grad_w_mix_out': 'grad_w', 'grad_ln1_g': 'grad_w', 'grad_ln1_b': 'grad_w', 'grad_xa_wq': 'grad_w', 'grad_xa_wkv': 'grad_w', 'grad_xa_wo': 'grad_w', 'grad_ln2_g': 'grad_w', 'grad_ln2_b': 'grad_w', 'grad_mlp_w_up': 'grad_w', 'grad_mlp_w_down': 'grad_w', 'grad_ln3_g': 'grad_w', 'grad_ln3_b': 'grad_w', 'delta_in_norm_g': 'delta_w', 'delta_in_norm_b': 'delta_w', 'delta_w_in': 'delta_w', 'delta_conv_dw': 'delta_w', 'delta_conv_db': 'delta_w', 'delta_conv_norm_g': 'delta_w', 'delta_conv_norm_b': 'delta_w', 'delta_w_conv_out': 'delta_w', 'delta_ssm_log_step': 'delta_w', 'delta_ssm_lambda_re': 'delta_w', 'delta_ssm_lambda_im': 'delta_w', 'delta_ssm_b_re': 'delta_w', 'delta_ssm_b_im': 'delta_w', 'delta_ssm_c_re': 'delta_w', 'delta_ssm_c_im': 'delta_w', 'delta_ssm_d': 'delta_w', 'delta_w_ssm_glu': 'delta_w', 'delta_w_mix_out': 'delta_w', 'delta_ln1_g': 'delta_w', 'delta_ln1_b': 'delta_w', 'delta_xa_wq': 'delta_w', 'delta_xa_wkv': 'delta_w', 'delta_xa_wo': 'delta_w', 'delta_ln2_g': 'delta_w', 'delta_ln2_b': 'delta_w', 'delta_mlp_w_up': 'delta_w', 'delta_mlp_w_down': 'delta_w', 'delta_ln3_g': 'delta_w', 'delta_ln3_b': 'delta_w', 'new_m_in_norm_g': 'new_m', 'new_m_in_norm_b': 'new_m', 'new_m_w_in': 'new_m', 'new_m_conv_dw': 'new_m', 'new_m_conv_db': 'new_m', 'new_m_conv_norm_g': 'new_m', 'new_m_conv_norm_b': 'new_m', 'new_m_w_conv_out': 'new_m', 'new_m_ssm_log_step': 'new_m', 'new_m_ssm_lambda_re': 'new_m', 'new_m_ssm_lambda_im': 'new_m', 'new_m_ssm_b_re': 'new_m', 'new_m_ssm_b_im': 'new_m', 'new_m_ssm_c_re': 'new_m', 'new_m_ssm_c_im': 'new_m', 'new_m_ssm_d': 'new_m', 'new_m_w_ssm_glu': 'new_m', 'new_m_w_mix_out': 'new_m', 'new_m_ln1_g': 'new_m', 'new_m_ln1_b': 'new_m', 'new_m_xa_wq': 'new_m', 'new_m_xa_wkv': 'new_m', 'new_m_xa_wo': 'new_m', 'new_m_ln2_g': 'new_m', 'new_m_ln2_b': 'new_m', 'new_m_mlp_w_up': 'new_m', 'new_m_mlp_w_down': 'new_m', 'new_m_ln3_g': 'new_m', 'new_m_ln3_b': 'new_m', 'new_v_in_norm_g': 'new_v', 'new_v_in_norm_b': 'new_v', 'new_v_w_in': 'new_v', 'new_v_conv_dw': 'new_v', 'new_v_conv_db': 'new_v', 'new_v_conv_norm_g': 'new_v', 'new_v_conv_norm_b': 'new_v', 'new_v_w_conv_out': 'new_v', 'new_v_ssm_log_step': 'new_v', 'new_v_ssm_lambda_re': 'new_v', 'new_v_ssm_lambda_im': 'new_v', 'new_v_ssm_b_re': 'new_v', 'new_v_ssm_b_im': 'new_v', 'new_v_ssm_c_re': 'new_v', 'new_v_ssm_c_im': 'new_v', 'new_v_ssm_d': 'new_v', 'new_v_w_ssm_glu': 'new_v', 'new_v_w_mix_out': 'new_v', 'new_v_ln1_g': 'new_v', 'new_v_ln1_b': 'new_v', 'new_v_xa_wq': 'new_v', 'new_v_xa_wkv': 'new_v', 'new_v_xa_wo': 'new_v', 'new_v_ln2_g': 'new_v', 'new_v_ln2_b': 'new_v', 'new_v_mlp_w_up': 'new_v', 'new_v_mlp_w_down': 'new_v', 'new_v_ln3_g': 'new_v', 'new_v_ln3_b': 'new_v'}


def _forward(args):
    return _fwd_reference(*[args[k] for k in FWD_PARAMS])


def _output_shape():
    def fwd():
        inp = _fwd_setup_inputs(0)
        return _fwd_reference(*[inp[k] for k in FWD_PARAMS])
    out = _jax.eval_shape(fwd)
    return out.shape, out.dtype

N_MICROBATCH = 1
ADAM_LR = 0.001
ADAM_B1 = 0.9
ADAM_B2 = 0.999
ADAM_EPS = 1e-08
ADAM_WD = 0.01
ADAM_STEP = 10
PER_EXAMPLE_BATCH_AXIS = {'x': 0, 'mem': 0, 'loss_target': 0}
SHARED_INPUTS = []
_WEIGHT_DTYPES = {'in_norm_g': _jnp.float32, 'in_norm_b': _jnp.float32, 'w_in': _jnp.float32, 'conv_dw': _jnp.float32, 'conv_db': _jnp.float32, 'conv_norm_g': _jnp.float32, 'conv_norm_b': _jnp.float32, 'w_conv_out': _jnp.float32, 'ssm_log_step': _jnp.float32, 'ssm_lambda_re': _jnp.float32, 'ssm_lambda_im': _jnp.float32, 'ssm_b_re': _jnp.float32, 'ssm_b_im': _jnp.float32, 'ssm_c_re': _jnp.float32, 'ssm_c_im': _jnp.float32, 'ssm_d': _jnp.float32, 'w_ssm_glu': _jnp.float32, 'w_mix_out': _jnp.float32, 'ln1_g': _jnp.float32, 'ln1_b': _jnp.float32, 'xa_wq': _jnp.float32, 'xa_wkv': _jnp.float32, 'xa_wo': _jnp.float32, 'ln2_g': _jnp.float32, 'ln2_b': _jnp.float32, 'mlp_w_up': _jnp.float32, 'mlp_w_down': _jnp.float32, 'ln3_g': _jnp.float32, 'ln3_b': _jnp.float32}
MOMENT_SCALE = {'in_norm_g': 7.929583e-01, 'in_norm_b': 5.914925e-01, 'w_in': 2.108778e-02, 'conv_dw': 3.107201e-02, 'conv_db': 1.206044e-01, 'conv_norm_g': 5.332849e-02, 'conv_norm_b': 7.008868e-02, 'w_conv_out': 3.801373e-02, 'ssm_log_step': 6.999240e-01, 'ssm_lambda_re': 1.750686e-03, 'ssm_lambda_im': 1.586461e-03, 'ssm_b_re': 1.247230e-03, 'ssm_b_im': 1.296848e-03, 'ssm_c_re': 2.650716e-03, 'ssm_c_im': 2.613688e-03, 'ssm_d': 3.862430e-02, 'w_ssm_glu': 1.868370e-02, 'w_mix_out': 7.617145e-02, 'ln1_g': 8.325187e-01, 'ln1_b': 5.941727e-01, 'xa_wq': 9.696017e-03, 'xa_wkv': 1.027933e-02, 'xa_wo': 1.827316e-02, 'ln2_g': 8.358686e-01, 'ln2_b': 5.961088e-01, 'mlp_w_up': 5.398171e-02, 'mlp_w_down': 2.465011e-01, 'ln3_g': 3.210511e+01, 'ln3_b': 7.124451e+00}


def _to_microbatches(a, axis):
    t = _jnp.moveaxis(a, axis, 0)
    t = t.reshape((N_MICROBATCH, t.shape[0] // N_MICROBATCH) + t.shape[1:])
    return _jnp.moveaxis(t, 1, axis + 1)


def setup_inputs(seed: int = 0) -> dict:
    inp = _fwd_setup_inputs(seed)
    key = _jax.random.fold_in(_jax.random.key(seed), 7919)
    shape, _ = _output_shape()
    out = dict(inp)
    out["loss_target"] = _jax.random.normal(_jax.random.fold_in(key, 0), shape, _jnp.float32)
    for i, name in enumerate(TWIN_WEIGHTS):
        w = inp[name].astype(_jnp.float32)
        if MOMENT_SCALE is None:
            s = _jnp.sqrt(_jnp.mean(_jnp.square(w)) + 1e-30)
        else:
            s = MOMENT_SCALE[name]
        km, kv = _jax.random.split(_jax.random.fold_in(key, i + 1))
        out[name] = w
        out["m_" + name] = s * _jax.random.normal(km, w.shape, _jnp.float32)
        out["v_" + name] = (s * s) * _jax.random.uniform(kv, w.shape, _jnp.float32, 0.5, 1.5)
    if N_MICROBATCH > 1:
        for name, axis in PER_EXAMPLE_BATCH_AXIS.items():
            out[name] = _to_microbatches(out[name], axis)
    return {'x': out['x'], 'mem': out['mem'], 'in_norm_g': out['in_norm_g'], 'in_norm_b': out['in_norm_b'], 'w_in': out['w_in'], 'conv_dw': out['conv_dw'], 'conv_db': out['conv_db'], 'conv_norm_g': out['conv_norm_g'], 'conv_norm_b': out['conv_norm_b'], 'w_conv_out': out['w_conv_out'], 'ssm_log_step': out['ssm_log_step'], 'ssm_lambda_re': out['ssm_lambda_re'], 'ssm_lambda_im': out['ssm_lambda_im'], 'ssm_b_re': out['ssm_b_re'], 'ssm_b_im': out['ssm_b_im'], 'ssm_c_re': out['ssm_c_re'], 'ssm_c_im': out['ssm_c_im'], 'ssm_d': out['ssm_d'], 'w_ssm_glu': out['w_ssm_glu'], 'w_mix_out': out['w_mix_out'], 'ln1_g': out['ln1_g'], 'ln1_b': out['ln1_b'], 'xa_wq': out['xa_wq'], 'xa_wkv': out['xa_wkv'], 'xa_wo': out['xa_wo'], 'ln2_g': out['ln2_g'], 'ln2_b': out['ln2_b'], 'mlp_w_up': out['mlp_w_up'], 'mlp_w_down': out['mlp_w_down'], 'ln3_g': out['ln3_g'], 'ln3_b': out['ln3_b'], 'loss_target': out['loss_target'], 'm_in_norm_g': out['m_in_norm_g'], 'm_in_norm_b': out['m_in_norm_b'], 'm_w_in': out['m_w_in'], 'm_conv_dw': out['m_conv_dw'], 'm_conv_db': out['m_conv_db'], 'm_conv_norm_g': out['m_conv_norm_g'], 'm_conv_norm_b': out['m_conv_norm_b'], 'm_w_conv_out': out['m_w_conv_out'], 'm_ssm_log_step': out['m_ssm_log_step'], 'm_ssm_lambda_re': out['m_ssm_lambda_re'], 'm_ssm_lambda_im': out['m_ssm_lambda_im'], 'm_ssm_b_re': out['m_ssm_b_re'], 'm_ssm_b_im': out['m_ssm_b_im'], 'm_ssm_c_re': out['m_ssm_c_re'], 'm_ssm_c_im': out['m_ssm_c_im'], 'm_ssm_d': out['m_ssm_d'], 'm_w_ssm_glu': out['m_w_ssm_glu'], 'm_w_mix_out': out['m_w_mix_out'], 'm_ln1_g': out['m_ln1_g'], 'm_ln1_b': out['m_ln1_b'], 'm_xa_wq': out['m_xa_wq'], 'm_xa_wkv': out['m_xa_wkv'], 'm_xa_wo': out['m_xa_wo'], 'm_ln2_g': out['m_ln2_g'], 'm_ln2_b': out['m_ln2_b'], 'm_mlp_w_up': out['m_mlp_w_up'], 'm_mlp_w_down': out['m_mlp_w_down'], 'm_ln3_g': out['m_ln3_g'], 'm_ln3_b': out['m_ln3_b'], 'v_in_norm_g': out['v_in_norm_g'], 'v_in_norm_b': out['v_in_norm_b'], 'v_w_in': out['v_w_in'], 'v_conv_dw': out['v_conv_dw'], 'v_conv_db': out['v_conv_db'], 'v_conv_norm_g': out['v_conv_norm_g'], 'v_conv_norm_b': out['v_conv_norm_b'], 'v_w_conv_out': out['v_w_conv_out'], 'v_ssm_log_step': out['v_ssm_log_step'], 'v_ssm_lambda_re': out['v_ssm_lambda_re'], 'v_ssm_lambda_im': out['v_ssm_lambda_im'], 'v_ssm_b_re': out['v_ssm_b_re'], 'v_ssm_b_im': out['v_ssm_b_im'], 'v_ssm_c_re': out['v_ssm_c_re'], 'v_ssm_c_im': out['v_ssm_c_im'], 'v_ssm_d': out['v_ssm_d'], 'v_w_ssm_glu': out['v_w_ssm_glu'], 'v_w_mix_out': out['v_w_mix_out'], 'v_ln1_g': out['v_ln1_g'], 'v_ln1_b': out['v_ln1_b'], 'v_xa_wq': out['v_xa_wq'], 'v_xa_wkv': out['v_xa_wkv'], 'v_xa_wo': out['v_xa_wo'], 'v_ln2_g': out['v_ln2_g'], 'v_ln2_b': out['v_ln2_b'], 'v_mlp_w_up': out['v_mlp_w_up'], 'v_mlp_w_down': out['v_mlp_w_down'], 'v_ln3_g': out['v_ln3_g'], 'v_ln3_b': out['v_ln3_b']}


def _loss(weights, diff, rest, loss_target):
    with _jax.named_scope("forward"):
        args = {**rest, TWIN_DIFF_INPUT: diff, **{k: w.astype(_WEIGHT_DTYPES[k]) for k, w in weights.items()}}
        y = _forward(args)
    with _jax.named_scope("loss_head"):
        err = _jnp.square(y.astype(_jnp.float32) - loss_target)
        return 0.5 * _jnp.sum(_jnp.mean(err, axis=-1)) if err.ndim else 0.5 * err


def _adamw(w, g, m, v):
    m = ADAM_B1 * m + (1.0 - ADAM_B1) * g
    v = ADAM_B2 * v + (1.0 - ADAM_B2) * _jnp.square(g)
    m_hat = m / (1.0 - ADAM_B1 ** ADAM_STEP)
    v_hat = v / (1.0 - ADAM_B2 ** ADAM_STEP)
    delta = -ADAM_LR * (m_hat / (_jnp.sqrt(v_hat) + ADAM_EPS) + ADAM_WD * w)
    return delta, m, v


def reference(x, mem, in_norm_g, in_norm_b, w_in, conv_dw, conv_db, conv_norm_g, conv_norm_b, w_conv_out, ssm_log_step, ssm_lambda_re, ssm_lambda_im, ssm_b_re, ssm_b_im, ssm_c_re, ssm_c_im, ssm_d, w_ssm_glu, w_mix_out, ln1_g, ln1_b, xa_wq, xa_wkv, xa_wo, ln2_g, ln2_b, mlp_w_up, mlp_w_down, ln3_g, ln3_b, loss_target, m_in_norm_g, m_in_norm_b, m_w_in, m_conv_dw, m_conv_db, m_conv_norm_g, m_conv_norm_b, m_w_conv_out, m_ssm_log_step, m_ssm_lambda_re, m_ssm_lambda_im, m_ssm_b_re, m_ssm_b_im, m_ssm_c_re, m_ssm_c_im, m_ssm_d, m_w_ssm_glu, m_w_mix_out, m_ln1_g, m_ln1_b, m_xa_wq, m_xa_wkv, m_xa_wo, m_ln2_g, m_ln2_b, m_mlp_w_up, m_mlp_w_down, m_ln3_g, m_ln3_b, v_in_norm_g, v_in_norm_b, v_w_in, v_conv_dw, v_conv_db, v_conv_norm_g, v_conv_norm_b, v_w_conv_out, v_ssm_log_step, v_ssm_lambda_re, v_ssm_lambda_im, v_ssm_b_re, v_ssm_b_im, v_ssm_c_re, v_ssm_c_im, v_ssm_d, v_w_ssm_glu, v_w_mix_out, v_ln1_g, v_ln1_b, v_xa_wq, v_xa_wkv, v_xa_wo, v_ln2_g, v_ln2_b, v_mlp_w_up, v_mlp_w_down, v_ln3_g, v_ln3_b):
    given = dict(x=x, mem=mem, in_norm_g=in_norm_g, in_norm_b=in_norm_b, w_in=w_in, conv_dw=conv_dw, conv_db=conv_db, conv_norm_g=conv_norm_g, conv_norm_b=conv_norm_b, w_conv_out=w_conv_out, ssm_log_step=ssm_log_step, ssm_lambda_re=ssm_lambda_re, ssm_lambda_im=ssm_lambda_im, ssm_b_re=ssm_b_re, ssm_b_im=ssm_b_im, ssm_c_re=ssm_c_re, ssm_c_im=ssm_c_im, ssm_d=ssm_d, w_ssm_glu=w_ssm_glu, w_mix_out=w_mix_out, ln1_g=ln1_g, ln1_b=ln1_b, xa_wq=xa_wq, xa_wkv=xa_wkv, xa_wo=xa_wo, ln2_g=ln2_g, ln2_b=ln2_b, mlp_w_up=mlp_w_up, mlp_w_down=mlp_w_down, ln3_g=ln3_g, ln3_b=ln3_b, loss_target=loss_target, m_in_norm_g=m_in_norm_g, m_in_norm_b=m_in_norm_b, m_w_in=m_w_in, m_conv_dw=m_conv_dw, m_conv_db=m_conv_db, m_conv_norm_g=m_conv_norm_g, m_conv_norm_b=m_conv_norm_b, m_w_conv_out=m_w_conv_out, m_ssm_log_step=m_ssm_log_step, m_ssm_lambda_re=m_ssm_lambda_re, m_ssm_lambda_im=m_ssm_lambda_im, m_ssm_b_re=m_ssm_b_re, m_ssm_b_im=m_ssm_b_im, m_ssm_c_re=m_ssm_c_re, m_ssm_c_im=m_ssm_c_im, m_ssm_d=m_ssm_d, m_w_ssm_glu=m_w_ssm_glu, m_w_mix_out=m_w_mix_out, m_ln1_g=m_ln1_g, m_ln1_b=m_ln1_b, m_xa_wq=m_xa_wq, m_xa_wkv=m_xa_wkv, m_xa_wo=m_xa_wo, m_ln2_g=m_ln2_g, m_ln2_b=m_ln2_b, m_mlp_w_up=m_mlp_w_up, m_mlp_w_down=m_mlp_w_down, m_ln3_g=m_ln3_g, m_ln3_b=m_ln3_b, v_in_norm_g=v_in_norm_g, v_in_norm_b=v_in_norm_b, v_w_in=v_w_in, v_conv_dw=v_conv_dw, v_conv_db=v_conv_db, v_conv_norm_g=v_conv_norm_g, v_conv_norm_b=v_conv_norm_b, v_w_conv_out=v_w_conv_out, v_ssm_log_step=v_ssm_log_step, v_ssm_lambda_re=v_ssm_lambda_re, v_ssm_lambda_im=v_ssm_lambda_im, v_ssm_b_re=v_ssm_b_re, v_ssm_b_im=v_ssm_b_im, v_ssm_c_re=v_ssm_c_re, v_ssm_c_im=v_ssm_c_im, v_ssm_d=v_ssm_d, v_w_ssm_glu=v_w_ssm_glu, v_w_mix_out=v_w_mix_out, v_ln1_g=v_ln1_g, v_ln1_b=v_ln1_b, v_xa_wq=v_xa_wq, v_xa_wkv=v_xa_wkv, v_xa_wo=v_xa_wo, v_ln2_g=v_ln2_g, v_ln2_b=v_ln2_b, v_mlp_w_up=v_mlp_w_up, v_mlp_w_down=v_mlp_w_down, v_ln3_g=v_ln3_g, v_ln3_b=v_ln3_b)
    weights = {n: given[n] for n in TWIN_WEIGHTS}
    shared = {n: given[n] for n in SHARED_INPUTS}
    per_example = {n: given[n] for n in ['x', 'mem']}
    grad_fn = _jax.value_and_grad(_loss, argnums=(0, 1))

    def one_microbatch(ex, loss_target):
        ex = dict(ex)
        diff = ex.pop(TWIN_DIFF_INPUT)
        return grad_fn(weights, diff, {**shared, **ex}, loss_target)

    if N_MICROBATCH == 1:
        loss, (grad_w, grad_x) = one_microbatch(per_example, given["loss_target"])
    else:
        def body(carry, xs):
            loss_sum, grad_sum = carry
            l_k, (gw_k, gx_k) = one_microbatch(xs[0], xs[1])
            with _jax.named_scope("update"):
                return (loss_sum + l_k, _jax.tree.map(_jnp.add, grad_sum, gw_k)), gx_k

        init = (_jnp.zeros((), _jnp.float32), _jax.tree.map(_jnp.zeros_like, weights))
        (loss, grad_w), grad_x = _jax.lax.scan(body, init, (per_example, given["loss_target"]))
    with _jax.named_scope("update"):
        delta_w, new_m, new_v = {}, {}, {}
        for n in TWIN_WEIGHTS:
            delta_w[n], new_m[n], new_v[n] = _adamw(weights[n], grad_w[n], given["m_" + n], given["v_" + n])
    return (loss, grad_x, *[grad_w[n] for n in TWIN_WEIGHTS], *[delta_w[n] for n in TWIN_WEIGHTS],
            *[new_m[n] for n in TWIN_WEIGHTS], *[new_v[n] for n in TWIN_WEIGHTS])
```

```python
import functools
import math

import jax
import jax.numpy as jnp
from jax import lax
from jax.experimental import pallas as pl
from jax.experimental.pallas import tpu as pltpu

F32 = jnp.float32
BF16 = jnp.bfloat16
MESH = pl.DeviceIdType.MESH

D_MODEL = 1024
N_HEADS = 4
HEAD_DIM = D_MODEL // N_HEADS
CONV_K = 31
CONV_HALO = 32
D_SSM = 512
SSM_GROUPS = 32
SSM_GROUP = 16
SSM_STATE = 64
SSM_BLOCKS = 4
SSM_BLOCK_IN = D_SSM // SSM_BLOCKS
SSM_BLOCK_STATE = SSM_GROUPS * SSM_STATE // SSM_BLOCKS
D_FF = 4096
D_IN = 4608
LN_EPS = 1e-5
ALPHA = (2.0 * 1) ** 0.25
N_CHIPS = 4
N_DEV = 8
ADAM_LR, ADAM_B1, ADAM_B2, ADAM_EPS, ADAM_WD, ADAM_STEP = 0.001, 0.9, 0.999, 1e-08, 0.01, 10
VMEM_LIMIT_BYTES = 56 * 1024 * 1024


def _pick(dim, cands):
    for c in cands:
        if dim % c == 0:
            return c
    return dim


def _cparams(sem=None):
    return pltpu.CompilerParams(dimension_semantics=sem, vmem_limit_bytes=VMEM_LIMIT_BYTES)


def _sigmoid(x):
    return 1.0 / (1.0 + jnp.exp(-x))


_DIMS = {"nn": (((1,), (0,)), ((), ())), "nt": (((1,), (1,)), ((), ())), "tn": (((0,), (0,)), ((), ()))}


def matmul(a, b, *, mode, M, N, K, tm, tn, tk, a_spec, b_spec, out_specs, out_shapes, name,
           extras=(), extra_specs=(), epilogue=None, alias_buf=None):
    nk = K // tk
    ne = len(extras)
    no = len(out_shapes)
    na = 0 if alias_buf is None else 1
    dims = _DIMS[mode]

    def body(*refs):
        a_ref, b_ref = refs[0], refs[1]
        e_refs = refs[2:2 + ne]
        o_refs = refs[2 + ne + na:2 + ne + na + no]

        def finish(acc):
            outs = (acc,) if epilogue is None else epilogue(acc, *[r[...] for r in e_refs])
            for o, r in zip(outs, o_refs):
                r[...] = o.astype(r.dtype)

        prod = lax.dot_general(a_ref[...].astype(BF16), b_ref[...].astype(BF16), dims, preferred_element_type=F32)
        if nk == 1:
            finish(prod)
        else:
            acc_ref = refs[-1]
            k = pl.program_id(2)

            @pl.when(k == 0)
            def _():
                acc_ref[...] = prod

            @pl.when(k > 0)
            def _():
                acc_ref[...] += prod

            @pl.when(k == nk - 1)
            def _():
                finish(acc_ref[...])

    in_specs = [pl.BlockSpec(*a_spec), pl.BlockSpec(*b_spec)] + [pl.BlockSpec(*s) for s in extra_specs]
    ins = [a, b, *extras]
    if alias_buf is not None:
        in_specs.append(pl.BlockSpec(memory_space=pl.ANY))
        ins.append(alias_buf)
    res = pl.pallas_call(
        body,
        grid=(M // tm, N // tn, nk),
        in_specs=in_specs,
        out_specs=[pl.BlockSpec(*s) for s in out_specs],
        out_shape=out_shapes,
        scratch_shapes=[] if nk == 1 else [pltpu.VMEM((tm, tn), F32)],
        input_output_aliases={2 + ne: 0} if alias_buf is not None else {},
        compiler_params=_cparams(("parallel", "parallel", "arbitrary")),
        name=name,
    )(*ins)
    return res


def _mn(tm, tn):
    return ((tm, tn), lambda i, j, k: (i, j))


def mm_nn(a, b_arr, b_spec, N, *, name, tm=None, tn, tk, out_dtype=F32, extras=(), epilogue=None, out_dtypes=None):
    M, K = a.shape
    tm = tm or _pick(M, [512, 256, 128])
    dts = out_dtypes or [out_dtype]
    return matmul(a, b_arr, mode="nn", M=M, N=N, K=K, tm=tm, tn=tn, tk=tk,
                  a_spec=((tm, tk), lambda i, j, k: (i, k)), b_spec=b_spec,
                  out_specs=[_mn(tm, tn)] * len(dts), out_shapes=[jax.ShapeDtypeStruct((M, N), d) for d in dts],
                  extras=extras, extra_specs=[_mn(tm, tn)] * len(extras), epilogue=epilogue, name=name)


def mm_nt(a, b_arr, b_spec, N, *, name, tm=None, tn, tk, out_dtype=F32, extras=(), epilogue=None, out_dtypes=None):
    M, K = a.shape
    tm = tm or _pick(M, [512, 256, 128])
    dts = out_dtypes or [out_dtype]
    return matmul(a, b_arr, mode="nt", M=M, N=N, K=K, tm=tm, tn=tn, tk=tk,
                  a_spec=((tm, tk), lambda i, j, k: (i, k)), b_spec=b_spec,
                  out_specs=[_mn(tm, tn)] * len(dts), out_shapes=[jax.ShapeDtypeStruct((M, N), d) for d in dts],
                  extras=extras, extra_specs=[_mn(tm, tn)] * len(extras), epilogue=epilogue, name=name)


def mm_tn(a, b, *, name, tm, tn, tk=None, out_spec, out_shape, out_buf=None):
    K, M = a.shape
    N = b.shape[1]
    tk = tk or _pick(K, [512, 256, 128])
    return matmul(a, b, mode="tn", M=M, N=N, K=K, tm=tm, tn=tn, tk=tk,
                  a_spec=((tk, tm), lambda i, j, k: (k, i)), b_spec=((tk, tn), lambda i, j, k: (k, j)),
                  out_specs=[out_spec], out_shapes=[out_shape], alias_buf=out_buf, name=name)[0]


def _rows(tc, w, cb=0):
    return pl.BlockSpec((tc, w), lambda i: (i, cb))


def _const(shape):
    return pl.BlockSpec(shape, lambda i: tuple([0] * len(shape)))


def _ln_stats(r):
    mu = jnp.mean(r, axis=-1, keepdims=True)
    xc = r - mu
    var = jnp.mean(xc * xc, axis=-1, keepdims=True)
    rstd = lax.rsqrt(var + LN_EPS)
    return xc * rstd, rstd


def _rowsum8(v):
    tc, w = v.shape
    return jnp.sum(v.reshape(tc // 8, 8, w), axis=0)


def ln_fwd(x, g, b, *, name, res=None):
    T, D = x.shape
    tc = _pick(T, [512, 256, 128])
    has_res = res is not None

    def body(*refs):
        if has_res:
            x_ref, res_ref, g_ref, b_ref, r_ref, h_ref, hb_ref = refs
            r = ALPHA * res_ref[...] + x_ref[...]
            r_ref[...] = r
        else:
            x_ref, g_ref, b_ref, h_ref, hb_ref = refs
            r = x_ref[...]
        xhat, _ = _ln_stats(r)
        y = xhat * g_ref[...] + b_ref[...]
        h_ref[...] = y
        hb_ref[...] = y.astype(BF16)

    ins = [x] + ([res] if has_res else []) + [g.reshape(1, D), b.reshape(1, D)]
    in_specs = [_rows(tc, D)] * (2 if has_res else 1) + [_const((1, D))] * 2
    n_out = 3 if has_res else 2
    outs = pl.pallas_call(
        body, grid=(T // tc,), in_specs=in_specs, out_specs=[_rows(tc, D)] * n_out,
        out_shape=[jax.ShapeDtypeStruct((T, D), F32)] * (n_out - 1) + [jax.ShapeDtypeStruct((T, D), BF16)],
        compiler_params=_cparams(("arbitrary",)), name=name)(*ins)
    if has_res:
        return outs
    return (x,) + tuple(outs)


def ln_bwd(r, dy, g, *, name):
    T, D = r.shape
    tc = _pick(T, [512, 256, 128])
    nt = T // tc

    def body(r_ref, dy_ref, g_ref, dr_ref, drb_ref, dg_ref, db_ref, accg, accb):
        i = pl.program_id(0)

        @pl.when(i == 0)
        def _():
            accg[...] = jnp.zeros_like(accg)
            accb[...] = jnp.zeros_like(accb)

        xhat, rstd = _ln_stats(r_ref[...])
        dy = dy_ref[...]
        dxh = dy * g_ref[...]
        m1 = jnp.mean(dxh, axis=-1, keepdims=True)
        m2 = jnp.mean(dxh * xhat, axis=-1, keepdims=True)
        dr = rstd * (dxh - m1 - xhat * m2)
        dr_ref[...] = dr
        drb_ref[...] = dr.astype(BF16)
        accg[...] += _rowsum8(dy * xhat)
        accb[...] += _rowsum8(dy)

        @pl.when(i == nt - 1)
        def _():
            dg_ref[...] = jnp.sum(accg[...], axis=0, keepdims=True)
            db_ref[...] = jnp.sum(accb[...], axis=0, keepdims=True)

    return pl.pallas_call(
        body, grid=(nt,), in_specs=[_rows(tc, D), _rows(tc, D), _const((1, D))],
        out_specs=[_rows(tc, D), _rows(tc, D), _const((1, D)), _const((1, D))],
        out_shape=[jax.ShapeDtypeStruct((T, D), F32), jax.ShapeDtypeStruct((T, D), BF16),
                   jax.ShapeDtypeStruct((1, D), F32), jax.ShapeDtypeStruct((1, D), F32)],
        scratch_shapes=[pltpu.VMEM((8, D), F32), pltpu.VMEM((8, D), F32)],
        compiler_params=_cparams(("arbitrary",)), name=name)(r, dy, g.reshape(1, D))


def loss_head(y, target, *, name):
    T, D = y.shape
    tc = _pick(T, [512, 256, 128])
    nt = T // tc

    def body(y_ref, t_ref, dy_ref, loss_ref, acc):
        i = pl.program_id(0)

        @pl.when(i == 0)
        def _():
            acc[...] = jnp.zeros_like(acc)

        e = y_ref[...] - t_ref[...]
        dy_ref[...] = e * (1.0 / D)
        acc[...] += _rowsum8(e * e)

        @pl.when(i == nt - 1)
        def _():
            s = jnp.sum(jnp.sum(acc[...], axis=0, keepdims=True), axis=1, keepdims=True)
            loss_ref[...] = jnp.broadcast_to(s, (1, 128))

    return pl.pallas_call(
        body, grid=(nt,), in_specs=[_rows(tc, D), _rows(tc, D)],
        out_specs=[_rows(tc, D), _const((1, 128))],
        out_shape=[jax.ShapeDtypeStruct((T, D), F32), jax.ShapeDtypeStruct((1, 128), F32)],
        scratch_shapes=[pltpu.VMEM((8, D), F32)],
        compiler_params=_cparams(("arbitrary",)), name=name)(y, target)


def _halo_prev(tc):
    per = tc // CONV_HALO
    return lambda i: jnp.maximum(i * per - 1, 0)


def conv_fwd(p, dw, db, ng, nb, *, name):
    T = p.shape[0]
    D = D_MODEL
    tc = _pick(T, [256, 128])
    prev = _halo_prev(tc)

    def body(val_ref, gate_ref, valp_ref, gatep_ref, dw_ref, db_ref, ng_ref, nb_ref, c_ref, act_ref, ext):
        i = pl.program_id(0)
        u_prev = valp_ref[...] * _sigmoid(gatep_ref[...])
        ext[0:CONV_HALO, :] = jnp.where(i > 0, u_prev, 0.0)
        ext[CONV_HALO:CONV_HALO + tc, :] = val_ref[...] * _sigmoid(gate_ref[...])
        off = CONV_HALO - (CONV_K - 1)
        acc = jnp.zeros((tc, D), F32)
        for k in range(CONV_K):
            acc = acc + dw_ref[k:k + 1, :] * ext[off + k:off + k + tc, :]
        c = acc + db_ref[...]
        c_ref[...] = c
        xhat, _ = _ln_stats(c)
        cn = xhat * ng_ref[...] + nb_ref[...]
        act_ref[...] = (cn * _sigmoid(cn)).astype(BF16)

    return pl.pallas_call(
        body, grid=(T // tc,),
        in_specs=[_rows(tc, D, 0), _rows(tc, D, 1),
                  pl.BlockSpec((CONV_HALO, D), lambda i: (prev(i), 0)), pl.BlockSpec((CONV_HALO, D), lambda i: (prev(i), 1)),
                  _const((CONV_HALO, D)), _const((1, D)), _const((1, D)), _const((1, D))],
        out_specs=[_rows(tc, D), _rows(tc, D)],
        out_shape=[jax.ShapeDtypeStruct((T, D), F32), jax.ShapeDtypeStruct((T, D), BF16)],
        scratch_shapes=[pltpu.VMEM((CONV_HALO + tc, D), F32)],
        compiler_params=_cparams(("arbitrary",)), name=name)(p, p, p, p, dw, db, ng, nb)


def conv_bwd_norm(dact, c_pre, ng, nb, *, name):
    T, D = c_pre.shape
    tc = _pick(T, [512, 256, 128])
    nt = T // tc

    def body(da_ref, c_ref, ng_ref, nb_ref, dc_ref, dng_ref, dnb_ref, ddb_ref, accg, accb, accd):
        i = pl.program_id(0)

        @pl.when(i == 0)
        def _():
            accg[...] = jnp.zeros_like(accg)
            accb[...] = jnp.zeros_like(accb)
            accd[...] = jnp.zeros_like(accd)

        xhat, rstd = _ln_stats(c_ref[...])
        cn = xhat * ng_ref[...] + nb_ref[...]
        s = _sigmoid(cn)
        dcn = da_ref[...] * (s * (1.0 + cn * (1.0 - s)))
        dxh = dcn * ng_ref[...]
        m1 = jnp.mean(dxh, axis=-1, keepdims=True)
        m2 = jnp.mean(dxh * xhat, axis=-1, keepdims=True)
        dc = rstd * (dxh - m1 - xhat * m2)
        dc_ref[...] = dc
        accg[...] += _rowsum8(dcn * xhat)
        accb[...] += _rowsum8(dcn)
        accd[...] += _rowsum8(dc)

        @pl.when(i == nt - 1)
        def _():
            dng_ref[...] = jnp.sum(accg[...], axis=0, keepdims=True)
            dnb_ref[...] = jnp.sum(accb[...], axis=0, keepdims=True)
            ddb_ref[...] = jnp.sum(accd[...], axis=0, keepdims=True)

    vec = jax.ShapeDtypeStruct((1, D), F32)
    return pl.pallas_call(
        body, grid=(nt,), in_specs=[_rows(tc, D), _rows(tc, D), _const((1, D)), _const((1, D))],
        out_specs=[_rows(tc, D), _const((1, D)), _const((1, D)), _const((1, D))],
        out_shape=[jax.ShapeDtypeStruct((T, D), F32), vec, vec, vec],
        scratch_shapes=[pltpu.VMEM((8, D), F32)] * 3,
        compiler_params=_cparams(("arbitrary",)), name=name)(dact, c_pre, ng, nb)


def conv_bwd_taps(dc, p, dw, *, name):
    T, D = dc.shape
    tc = _pick(T, [256, 128])
    nt = T // tc
    per = tc // CONV_HALO
    prev = _halo_prev(tc)
    last_halo = T // CONV_HALO - 1
    nxt = lambda i: jnp.minimum((i + 1) * per, last_halo)
    off = CONV_HALO - (CONV_K - 1)

    def body(dc_ref, dcn_ref, val_ref, gate_ref, valp_ref, gatep_ref, dw_ref, dvg_ref, ddw_ref, ext_u, ext_d, acc):
        i = pl.program_id(0)

        @pl.when(i == 0)
        def _():
            acc[...] = jnp.zeros_like(acc)

        val = val_ref[...]
        sg = _sigmoid(gate_ref[...])
        u_prev = valp_ref[...] * _sigmoid(gatep_ref[...])
        ext_u[0:CONV_HALO, :] = jnp.where(i > 0, u_prev, 0.0)
        ext_u[CONV_HALO:CONV_HALO + tc, :] = val * sg
        dc = dc_ref[...]
        ext_d[0:tc, :] = dc
        ext_d[tc:tc + CONV_HALO, :] = jnp.where(i < nt - 1, dcn_ref[...], 0.0)
        du = jnp.zeros((tc, D), F32)
        for k in range(CONV_K):
            acc[k] += _rowsum8(dc * ext_u[off + k:off + k + tc, :])
            du = du + dw_ref[k:k + 1, :] * ext_d[CONV_K - 1 - k:CONV_K - 1 - k + tc, :]
        dvg_ref[:, 0:D] = (du * sg).astype(BF16)
        dvg_ref[:, D:2 * D] = (du * val * sg * (1.0 - sg)).astype(BF16)

        @pl.when(i == nt - 1)
        def _():
            ddw_ref[...] = jnp.zeros_like(ddw_ref)
            for k in range(CONV_K):
                ddw_ref[k:k + 1, :] = jnp.sum(acc[k], axis=0, keepdims=True)

    return pl.pallas_call(
        body, grid=(nt,),
        in_specs=[_rows(tc, D), pl.BlockSpec((CONV_HALO, D), lambda i: (nxt(i), 0)),
                  _rows(tc, D, 0), _rows(tc, D, 1),
                  pl.BlockSpec((CONV_HALO, D), lambda i: (prev(i), 0)), pl.BlockSpec((CONV_HALO, D), lambda i: (prev(i), 1)),
                  _const((CONV_HALO, D))],
        out_specs=[_rows(tc, 2 * D), _const((CONV_HALO, D))],
        out_shape=[jax.ShapeDtypeStruct((T, 2 * D), BF16), jax.ShapeDtypeStruct((CONV_HALO, D), F32)],
        scratch_shapes=[pltpu.VMEM((CONV_HALO + tc, D), F32), pltpu.VMEM((CONV_HALO + tc, D), F32),
                        pltpu.VMEM((CONV_K, 8, D), F32)],
        compiler_params=_cparams(("arbitrary",)), name=name)(dc, dc, p, p, p, p, dw)


GATE_A0 = (2 * D_MODEL + D_SSM) // 512
GATE_B0 = GATE_A0 + 2


def merge_fwd(p, ya, z, *, name):
    T = p.shape[0]
    D = D_MODEL
    tc = _pick(T, [512, 256, 128])
    W = 512

    def body(ga_ref, gb_ref, ya_ref, z1_ref, z2_ref, o_ref):
        yb = z1_ref[...] * _sigmoid(z2_ref[...])
        o_ref[...] = (_sigmoid(ga_ref[...]) * ya_ref[...] + _sigmoid(gb_ref[...]) * yb).astype(BF16)

    return pl.pallas_call(
        body, grid=(T // tc, D // W),
        in_specs=[pl.BlockSpec((tc, W), lambda i, j: (i, GATE_A0 + j)), pl.BlockSpec((tc, W), lambda i, j: (i, GATE_B0 + j)),
                  pl.BlockSpec((tc, W), lambda i, j: (i, j)), pl.BlockSpec((tc, W), lambda i, j: (i, j)),
                  pl.BlockSpec((tc, W), lambda i, j: (i, D // W + j))],
        out_specs=pl.BlockSpec((tc, W), lambda i, j: (i, j)),
        out_shape=jax.ShapeDtypeStruct((T, D), BF16),
        compiler_params=_cparams(("arbitrary", "arbitrary")), name=name)(p, p, ya, z, z)


def merge_bwd(dm, p, ya, z, *, name):
    T = p.shape[0]
    D = D_MODEL
    tc = _pick(T, [512, 256, 128])
    W = 512
    nb = D // W

    def body(dm_ref, ga_ref, gb_ref, ya_ref, z1_ref, z2_ref, dya_ref, dga_ref, dgb_ref, dz1_ref, dz2_ref):
        dm = dm_ref[...]
        sa = _sigmoid(ga_ref[...])
        sb = _sigmoid(gb_ref[...])
        s2 = _sigmoid(z2_ref[...])
        z1 = z1_ref[...]
        yb = z1 * s2
        dya_ref[...] = (dm * sa).astype(BF16)
        dga_ref[...] = (dm * ya_ref[...] * sa * (1.0 - sa)).astype(BF16)
        dgb_ref[...] = (dm * yb * sb * (1.0 - sb)).astype(BF16)
        dyb = dm * sb
        dz1_ref[...] = (dyb * s2).astype(BF16)
        dz2_ref[...] = (dyb * z1 * s2 * (1.0 - s2)).astype(BF16)

    blk = lambda off: pl.BlockSpec((tc, W), lambda i, j: (i, off + j))
    dya, dga, dgb, dz1, dz2 = pl.pallas_call(
        body, grid=(T // tc, nb),
        in_specs=[blk(0), blk(GATE_A0), blk(GATE_B0), blk(0), blk(0), blk(nb)],
        out_specs=[blk(0)] * 5,
        out_shape=[jax.ShapeDtypeStruct((T, D), BF16)] * 5,
        compiler_params=_cparams(("arbitrary", "arbitrary")), name=name)(dm, p, p, ya, z, z)
    return dya, dga, dgb, dz1, dz2


def _scan_block(src_r, src_i, dst_r, dst_i, car_r, car_i, pw_r, pw_i, cw_r, cw_i, ntiles, reverse, extra=None):
    W = src_r.shape[1]
    rows = lax.broadcasted_iota(jnp.int32, (8, W), 0)
    steps = []
    for d, pr in ((1, 0), (2, 1), (4, 3)):
        steps.append((d, jnp.broadcast_to(pw_r[pr:pr + 1, :], (8, W)), jnp.broadcast_to(pw_i[pr:pr + 1, :], (8, W))))
    cw_r, cw_i = cw_r[...], cw_i[...]

    def tile(jj, carry):
        j = ntiles - 1 - jj if reverse else jj
        sl = pl.ds(pl.multiple_of(j * 8, 8), 8)
        xr, xi = src_r[sl, :], src_i[sl, :]
        for d, lr, li in steps:
            if reverse:
                sr = jnp.where(rows < 8 - d, pltpu.roll(xr, 8 - d, 0), 0.0)
                si = jnp.where(rows < 8 - d, pltpu.roll(xi, 8 - d, 0), 0.0)
            else:
                sr = jnp.where(rows >= d, pltpu.roll(xr, d, 0), 0.0)
                si = jnp.where(rows >= d, pltpu.roll(xi, d, 0), 0.0)
            xr, xi = xr + lr * sr - li * si, xi + lr * si + li * sr
        cr, ci = car_r[...], car_i[...]
        xr, xi = xr + cw_r * cr - cw_i * ci, xi + cw_r * ci + cw_i * cr
        dst_r[sl, :] = xr
        dst_i[sl, :] = xi
        edge = 0 if reverse else 7
        car_r[...] = jnp.broadcast_to(xr[edge:edge + 1, :], (8, W))
        car_i[...] = jnp.broadcast_to(xi[edge:edge + 1, :], (8, W))
        if extra is not None:
            carry = extra(j, xr, xi, carry)
        return carry

    return tile


def ssm_fwd(p, Br, Bi, Cr, Ci, pw_r, pw_i, dvec, *, name):
    T = p.shape[0]
    tt = _pick(T, [256, 128])
    nt = T // tt
    WI, WS = SSM_BLOCK_IN, SSM_BLOCK_STATE
    u0 = 2 * D_MODEL // WI

    def body(u_ref, br_ref, bi_ref, cr_ref, ci_ref, pwr_ref, pwi_ref, d_ref, xr_ref, xi_ref, y_ref, bur, bui, car_r, car_i):
        i = pl.program_id(1)

        @pl.when(i == 0)
        def _():
            car_r[...] = jnp.zeros_like(car_r)
            car_i[...] = jnp.zeros_like(car_i)

        u = u_ref[...]
        ub = u.astype(BF16)
        bur[...] = jnp.dot(ub, br_ref[...].astype(BF16), preferred_element_type=F32)
        bui[...] = jnp.dot(ub, bi_ref[...].astype(BF16), preferred_element_type=F32)
        tile = _scan_block(bur, bui, xr_ref, xi_ref, car_r, car_i, pwr_ref, pwi_ref, pwr_ref, pwi_ref, tt // 8, False)
        lax.fori_loop(0, tt // 8, tile, 0)
        y = (jnp.dot(xr_ref[...].astype(BF16), cr_ref[...].astype(BF16), preferred_element_type=F32)
             - jnp.dot(xi_ref[...].astype(BF16), ci_ref[...].astype(BF16), preferred_element_type=F32)
             + d_ref[...] * u)
        y_ref[...] = y.astype(BF16)

    return pl.pallas_call(
        body, grid=(SSM_BLOCKS, nt),
        in_specs=[pl.BlockSpec((tt, WI), lambda b, i: (i, u0 + b)),
                  pl.BlockSpec((None, WI, WS), lambda b, i: (b, 0, 0)), pl.BlockSpec((None, WI, WS), lambda b, i: (b, 0, 0)),
                  pl.BlockSpec((None, WS, WI), lambda b, i: (b, 0, 0)), pl.BlockSpec((None, WS, WI), lambda b, i: (b, 0, 0)),
                  pl.BlockSpec((8, WS), lambda b, i: (0, b)), pl.BlockSpec((8, WS), lambda b, i: (0, b)),
                  pl.BlockSpec((1, WI), lambda b, i: (0, b))],
        out_specs=[pl.BlockSpec((tt, WS), lambda b, i: (i, b)), pl.BlockSpec((tt, WS), lambda b, i: (i, b)),
                   pl.BlockSpec((tt, WI), lambda b, i: (i, b))],
        out_shape=[jax.ShapeDtypeStruct((T, SSM_BLOCKS * WS), F32)] * 2 + [jax.ShapeDtypeStruct((T, D_SSM), BF16)],
        scratch_shapes=[pltpu.VMEM((tt, WS), F32), pltpu.VMEM((tt, WS), F32), pltpu.VMEM((8, WS), F32), pltpu.VMEM((8, WS), F32)],
        compiler_params=_cparams(("arbitrary", "arbitrary")), name=name)(p, Br, Bi, Cr, Ci, pw_r, pw_i, dvec)


def ssm_bwd(dy, p, xr, xi, Br, Bi, Cr, Ci, pwc_r, pwc_i, dvec, *, name):
    T = p.shape[0]
    tt = _pick(T, [256, 128])
    nt = T // tt
    WI, WS = SSM_BLOCK_IN, SSM_BLOCK_STATE
    u0 = 2 * D_MODEL // WI
    tb = lambda i: nt - 1 - i
    xprev = lambda i: jnp.maximum(tb(i) * (tt // 8) - 1, 0)
    tn_dims = _DIMS["tn"]
    nt_dims = _DIMS["nt"]

    def body(dy_ref, u_ref, xr_ref, xi_ref, xpr_ref, xpi_ref, br_ref, bi_ref, cr_ref, ci_ref, pwr_ref, pwi_ref,
             cwr_ref, cwi_ref, d_ref,
             du_ref, dbr_ref, dbi_ref, dcr_ref, dci_ref, dar_ref, dai_ref, dd_ref,
             gr, gi, ext_r, ext_i, car_r, car_i):
        i = pl.program_id(1)

        @pl.when(i == 0)
        def _():
            car_r[...] = jnp.zeros_like(car_r)
            car_i[...] = jnp.zeros_like(car_i)
            dbr_ref[...] = jnp.zeros_like(dbr_ref)
            dbi_ref[...] = jnp.zeros_like(dbi_ref)
            dcr_ref[...] = jnp.zeros_like(dcr_ref)
            dci_ref[...] = jnp.zeros_like(dci_ref)
            dar_ref[...] = jnp.zeros_like(dar_ref)
            dai_ref[...] = jnp.zeros_like(dai_ref)
            dd_ref[...] = jnp.zeros_like(dd_ref)

        dy = dy_ref[...]
        dyb = dy.astype(BF16)
        u = u_ref[...]
        ub = u.astype(BF16)
        gr[...] = lax.dot_general(dyb, cr_ref[...].astype(BF16), nt_dims, preferred_element_type=F32)
        gi[...] = -lax.dot_general(dyb, ci_ref[...].astype(BF16), nt_dims, preferred_element_type=F32)
        first = tb(i) == 0
        ext_r[0:8, :] = jnp.where(first, 0.0, xpr_ref[...])
        ext_i[0:8, :] = jnp.where(first, 0.0, xpi_ref[...])
        ext_r[8:8 + tt, :] = xr_ref[...]
        ext_i[8:8 + tt, :] = xi_ref[...]
        rows = lax.broadcasted_iota(jnp.int32, (8, WS), 0)

        def lam_grad(j, g_r, g_i, carry):
            a_r, a_i = carry
            cur = pl.ds(pl.multiple_of(j * 8 + 8, 8), 8)
            prv = pl.ds(pl.multiple_of(j * 8, 8), 8)
            xc_r, xc_i = ext_r[cur, :], ext_i[cur, :]
            xl_r, xl_i = ext_r[prv, :], ext_i[prv, :]
            xp_r = jnp.where(rows == 0, jnp.broadcast_to(xl_r[7:8, :], (8, WS)), pltpu.roll(xc_r, 1, 0))
            xp_i = jnp.where(rows == 0, jnp.broadcast_to(xl_i[7:8, :], (8, WS)), pltpu.roll(xc_i, 1, 0))
            return (a_r + g_r * xp_r + g_i * xp_i, a_i + g_i * xp_r - g_r * xp_i)

        tile = _scan_block(gr, gi, gr, gi, car_r, car_i, pwr_ref, pwi_ref, cwr_ref, cwi_ref, tt // 8, True, extra=lam_grad)
        z8 = jnp.zeros((8, WS), F32)
        a_r, a_i = lax.fori_loop(0, tt // 8, tile, (z8, z8))
        dar_ref[...] += a_r
        dai_ref[...] += a_i
        grb = gr[...].astype(BF16)
        gib = gi[...].astype(BF16)
        dbr_ref[...] += lax.dot_general(ub, grb, tn_dims, preferred_element_type=F32)
        dbi_ref[...] += lax.dot_general(ub, gib, tn_dims, preferred_element_type=F32)
        dcr_ref[...] += lax.dot_general(xr_ref[...].astype(BF16), dyb, tn_dims, preferred_element_type=F32)
        dci_ref[...] -= lax.dot_general(xi_ref[...].astype(BF16), dyb, tn_dims, preferred_element_type=F32)
        du = (lax.dot_general(grb, br_ref[...].astype(BF16), nt_dims, preferred_element_type=F32)
              + lax.dot_general(gib, bi_ref[...].astype(BF16), nt_dims, preferred_element_type=F32)
              + d_ref[...] * dy)
        du_ref[...] = du.astype(BF16)
        dd_ref[...] += _rowsum8(dy * u)

    wspec = lambda shp: pl.BlockSpec((None,) + shp, lambda b, i: (b, 0, 0))
    return pl.pallas_call(
        body, grid=(SSM_BLOCKS, nt),
        in_specs=[pl.BlockSpec((tt, WI), lambda b, i: (tb(i), b)),
                  pl.BlockSpec((tt, WI), lambda b, i: (tb(i), u0 + b)),
                  pl.BlockSpec((tt, WS), lambda b, i: (tb(i), b)), pl.BlockSpec((tt, WS), lambda b, i: (tb(i), b)),
                  pl.BlockSpec((8, WS), lambda b, i: (xprev(i), b)), pl.BlockSpec((8, WS), lambda b, i: (xprev(i), b)),
                  wspec((WI, WS)), wspec((WI, WS)), wspec((WS, WI)), wspec((WS, WI)),
                  pl.BlockSpec((8, WS), lambda b, i: (0, b)), pl.BlockSpec((8, WS), lambda b, i: (0, b)),
                  pl.BlockSpec((8, WS), lambda b, i: (0, b)), pl.BlockSpec((8, WS), lambda b, i: (0, b)),
                  pl.BlockSpec((1, WI), lambda b, i: (0, b))],
        out_specs=[pl.BlockSpec((tt, WI), lambda b, i: (tb(i), b)),
                   wspec((WI, WS)), wspec((WI, WS)), wspec((WS, WI)), wspec((WS, WI)),
                   pl.BlockSpec((8, WS), lambda b, i: (0, b)), pl.BlockSpec((8, WS), lambda b, i: (0, b)),
                   pl.BlockSpec((8, WI), lambda b, i: (0, b))],
        out_shape=[jax.ShapeDtypeStruct((T, D_SSM), BF16),
                   jax.ShapeDtypeStruct((SSM_BLOCKS, WI, WS), F32), jax.ShapeDtypeStruct((SSM_BLOCKS, WI, WS), F32),
                   jax.ShapeDtypeStruct((SSM_BLOCKS, WS, WI), F32), jax.ShapeDtypeStruct((SSM_BLOCKS, WS, WI), F32),
                   jax.ShapeDtypeStruct((8, SSM_BLOCKS * WS), F32), jax.ShapeDtypeStruct((8, SSM_BLOCKS * WS), F32),
                   jax.ShapeDtypeStruct((8, D_SSM), F32)],
        scratch_shapes=[pltpu.VMEM((tt, WS), F32), pltpu.VMEM((tt, WS), F32),
                        pltpu.VMEM((tt + 8, WS), F32), pltpu.VMEM((tt + 8, WS), F32),
                        pltpu.VMEM((8, WS), F32), pltpu.VMEM((8, WS), F32)],
        compiler_params=_cparams(("arbitrary", "arbitrary")), name=name,
    )(dy, p, xr, xi, xr, xi, Br, Bi, Cr, Ci, pwc_r, pwc_i, pwc_r[::-1], pwc_i[::-1], dvec)


def _ssm_discretise(log_step, lam_re, lam_im, b_re, b_im):
    step = jnp.exp(log_step)[:, None]
    mag = jnp.exp(lam_re * step)
    ar = mag * jnp.cos(lam_im * step)
    ai = mag * jnp.sin(lam_im * step)
    den = lam_re * lam_re + lam_im * lam_im
    nr = ar - 1.0
    cr = (nr * lam_re + ai * lam_im) / den
    ci = (ai * lam_re - nr * lam_im) / den
    bbr = cr[..., None] * b_re - ci[..., None] * b_im
    bbi = cr[..., None] * b_im + ci[..., None] * b_re
    return ar, ai, bbr, bbi


def _blockdiag_in(bb):
    t = jnp.transpose(bb, (0, 2, 1)).reshape(SSM_BLOCKS, 8, SSM_GROUP, SSM_STATE)
    eye = jnp.eye(8, dtype=bb.dtype)
    return (t[:, :, :, None, :] * eye[None, :, None, :, None]).reshape(SSM_BLOCKS, SSM_BLOCK_IN, SSM_BLOCK_STATE)


def _blockdiag_out(cc):
    t = jnp.transpose(cc, (0, 2, 1)).reshape(SSM_BLOCKS, 8, SSM_STATE, SSM_GROUP)
    eye = jnp.eye(8, dtype=cc.dtype)
    return (t[:, :, :, None, :] * eye[None, :, None, :, None]).reshape(SSM_BLOCKS, SSM_BLOCK_STATE, SSM_BLOCK_IN)


def _diag_in(d):
    t = d.reshape(SSM_BLOCKS, 8, SSM_GROUP, 8, SSM_STATE)
    t = jnp.einsum("bghgp->bghp", t).reshape(SSM_GROUPS, SSM_GROUP, SSM_STATE)
    return jnp.transpose(t, (0, 2, 1))


def _diag_out(d):
    t = d.reshape(SSM_BLOCKS, 8, SSM_STATE, 8, SSM_GROUP)
    t = jnp.einsum("bgpgh->bgph", t).reshape(SSM_GROUPS, SSM_STATE, SSM_GROUP)
    return jnp.transpose(t, (0, 2, 1))


def _powers(ar, ai):
    rs, is_ = [ar], [ai]
    for _ in range(7):
        r, i = rs[-1], is_[-1]
        rs.append(r * ar - i * ai)
        is_.append(r * ai + i * ar)
    return jnp.stack(rs), jnp.stack(is_)


def attn_fwd(q, kv, *, name):
    T, D = q.shape
    nm = kv.shape[0]
    tq = _pick(T, [512, 256, 128])
    scale = HEAD_DIM ** -0.5

    def body(q_ref, k_ref, v_ref, o_ref):
        for h in range(N_HEADS):
            sl = slice(h * HEAD_DIM, (h + 1) * HEAD_DIM)
            s = lax.dot_general(q_ref[:, sl], k_ref[:, sl].astype(BF16), _DIMS["nt"], preferred_element_type=F32) * scale
            e = jnp.exp(s - jnp.max(s, axis=-1, keepdims=True))
            pr = e / jnp.sum(e, axis=-1, keepdims=True)
            o_ref[:, sl] = jnp.dot(pr.astype(BF16), v_ref[:, sl].astype(BF16), preferred_element_type=F32).astype(BF16)

    return pl.pallas_call(
        body, grid=(T // tq,),
        in_specs=[_rows(tq, D), pl.BlockSpec((nm, D), lambda i: (0, 0)), pl.BlockSpec((nm, D), lambda i: (0, 1))],
        out_specs=_rows(tq, D), out_shape=jax.ShapeDtypeStruct((T, D), BF16),
        compiler_params=_cparams(("arbitrary",)), name=name)(q, kv, kv)


def attn_bwd(q, kv, do, *, name):
    T, D = q.shape
    nm = kv.shape[0]
    tq = _pick(T, [512, 256, 128])
    nt = T // tq
    scale = HEAD_DIM ** -0.5

    def body(q_ref, k_ref, v_ref, do_ref, dq_ref, dkv_ref):
        i = pl.program_id(0)

        @pl.when(i == 0)
        def _():
            dkv_ref[...] = jnp.zeros_like(dkv_ref)

        for h in range(N_HEADS):
            sl = slice(h * HEAD_DIM, (h + 1) * HEAD_DIM)
            slv = slice(D + h * HEAD_DIM, D + (h + 1) * HEAD_DIM)
            qh = q_ref[:, sl]
            kh = k_ref[:, sl].astype(BF16)
            vh = v_ref[:, sl].astype(BF16)
            doh = do_ref[:, sl].astype(BF16)
            s = lax.dot_general(qh, kh, _DIMS["nt"], preferred_element_type=F32) * scale
            e = jnp.exp(s - jnp.max(s, axis=-1, keepdims=True))
            pr = e / jnp.sum(e, axis=-1, keepdims=True)
            dp = lax.dot_general(doh, vh, _DIMS["nt"], preferred_element_type=F32)
            ds = (pr * (dp - jnp.sum(pr * dp, axis=-1, keepdims=True)) * scale).astype(BF16)
            dq_ref[:, sl] = jnp.dot(ds, kh, preferred_element_type=F32).astype(BF16)
            dkv_ref[:, sl] += lax.dot_general(ds, qh, _DIMS["tn"], preferred_element_type=F32)
            dkv_ref[:, slv] += lax.dot_general(pr.astype(BF16), doh, _DIMS["tn"], preferred_element_type=F32)

    return pl.pallas_call(
        body, grid=(nt,),
        in_specs=[_rows(tq, D), pl.BlockSpec((nm, D), lambda i: (0, 0)), pl.BlockSpec((nm, D), lambda i: (0, 1)), _rows(tq, D)],
        out_specs=[_rows(tq, D), _const((nm, 2 * D))],
        out_shape=[jax.ShapeDtypeStruct((T, D), BF16), jax.ShapeDtypeStruct((nm, 2 * D), F32)],
        compiler_params=_cparams(("arbitrary",)), name=name)(q, kv, kv, do)


def _adam_math(w, g, m, v):
    m = ADAM_B1 * m + (1.0 - ADAM_B1) * g
    v = ADAM_B2 * v + (1.0 - ADAM_B2) * (g * g)
    m_hat = m / (1.0 - ADAM_B1 ** ADAM_STEP)
    v_hat = v / (1.0 - ADAM_B2 ** ADAM_STEP)
    delta = -ADAM_LR * (m_hat / (jnp.sqrt(v_hat) + ADAM_EPS) + ADAM_WD * w)
    return delta, m, v


def adamw(w, m, v, g_arr, g_row0, *, name):
    R, C = w.shape
    tr = _pick(R, [256, 128, 64, 32, 16, 8])
    assert g_row0 % tr == 0
    g0 = g_row0 // tr

    def body(w_ref, m_ref, v_ref, g_ref, go_ref, d_ref, mo_ref, vo_ref):
        g = g_ref[...]
        d, mn, vn = _adam_math(w_ref[...], g, m_ref[...], v_ref[...])
        go_ref[...] = g
        d_ref[...] = d
        mo_ref[...] = mn
        vo_ref[...] = vn

    sp = pl.BlockSpec((tr, C), lambda i: (i, 0))
    return pl.pallas_call(
        body, grid=(R // tr,), in_specs=[sp, sp, sp, pl.BlockSpec((tr, C), lambda i: (g0 + i, 0))],
        out_specs=[sp] * 4, out_shape=[jax.ShapeDtypeStruct((R, C), F32)] * 4,
        compiler_params=_cparams(("arbitrary",)), name=name)(w, m, v, g_arr)


def _place():
    x, y, c = lax.axis_index("x"), lax.axis_index("y"), lax.axis_index("c")
    chips = [(1 - x, y), (x, 1 - y), (1 - x, 1 - y)]
    return x, y, c, chips


ANY = pl.BlockSpec(memory_space=pl.ANY)


def allgather_weights(shards, *, name):
    n = len(shards)

    def body(*refs):
        s_refs, o_refs = refs[:n], refs[n:2 * n]
        send_sems, recv_sems, fsend_sems, frecv_sems, local_sems = refs[2 * n:]
        x, y, c, chips = _place()
        k_me = 2 * x + y
        sib = (x, y, 1 - c)
        halves = [s.shape[0] // 2 for s in shards]

        def half(a, cc):
            return pl.ds(pl.multiple_of(cc * halves[a], 16), halves[a])

        own = [pltpu.make_async_copy(s_refs[a], o_refs[a].at[k_me], local_sems.at[a]) for a in range(n)]
        for cp in own:
            cp.start()
        sends = []
        for a in range(n):
            for r, (px, py) in enumerate(chips):
                cp = pltpu.make_async_remote_copy(
                    src_ref=s_refs[a].at[half(a, c)], dst_ref=o_refs[a].at[k_me, half(a, c)],
                    send_sem=send_sems.at[3 * a + r], recv_sem=recv_sems.at[3 * a + r],
                    device_id=(px, py, c), device_id_type=MESH)
                cp.start()
                sends.append(cp)
        passed = []
        for a in range(n):
            for r, (px, py) in enumerate(chips):
                win = o_refs[a].at[2 * px + py, half(a, c)]
                pltpu.make_async_remote_copy(
                    src_ref=win, dst_ref=win, send_sem=send_sems.at[3 * a + r], recv_sem=recv_sems.at[3 * a + r],
                    device_id=(px, py, c), device_id_type=MESH).wait_recv()
                cp = pltpu.make_async_remote_copy(
                    src_ref=win, dst_ref=win, send_sem=fsend_sems.at[3 * a + r], recv_sem=frecv_sems.at[3 * a + r],
                    device_id=sib, device_id_type=MESH)
                cp.start()
                passed.append(cp)
        for a in range(n):
            for r, (px, py) in enumerate(chips):
                win = o_refs[a].at[2 * px + py, half(a, 1 - c)]
                pltpu.make_async_remote_copy(
                    src_ref=win, dst_ref=win, send_sem=fsend_sems.at[3 * a + r], recv_sem=frecv_sems.at[3 * a + r],
                    device_id=sib, device_id_type=MESH).wait_recv()
        for cp in sends + passed:
            cp.wait_send()
        for cp in own:
            cp.wait()

    return pl.pallas_call(
        body, in_specs=[ANY] * n, out_specs=[ANY] * n,
        out_shape=[jax.ShapeDtypeStruct((N_CHIPS,) + s.shape, s.dtype) for s in shards],
        scratch_shapes=[pltpu.SemaphoreType.DMA((3 * n,))] * 4 + [pltpu.SemaphoreType.DMA((n,))],
        name=name)(*shards)


def exchange_halves(grads, *, name):
    n = len(grads)

    def body(*refs):
        g_refs, l_refs = refs[:n], refs[n:2 * n]
        send_sems, recv_sems = refs[2 * n:]
        x, y, c, _ = _place()
        cps = []
        for a in range(n):
            h = grads[a].shape[1] // 2
            cp = pltpu.make_async_remote_copy(
                src_ref=g_refs[a].at[:, pl.ds(pl.multiple_of((1 - c) * h, 8), h)], dst_ref=l_refs[a],
                send_sem=send_sems.at[a], recv_sem=recv_sems.at[a], device_id=(x, y, 1 - c), device_id_type=MESH)
            cp.start()
            cps.append(cp)
        for cp in cps:
            cp.wait()

    return pl.pallas_call(
        body, in_specs=[ANY] * n, out_specs=[ANY] * n,
        out_shape=[jax.ShapeDtypeStruct((g.shape[0], g.shape[1] // 2, g.shape[2]), g.dtype) for g in grads],
        scratch_shapes=[pltpu.SemaphoreType.DMA((n,))] * 2,
        name=name)(*grads)


def scatter_to_owners(parts, *, name):
    n = len(parts)

    def body(*refs):
        p_refs, l_refs = refs[:n], refs[n:2 * n]
        send_sems, recv_sems = refs[2 * n:]
        x, y, c, chips = _place()
        cps = []
        for a in range(n):
            for r, (px, py) in enumerate(chips):
                cp = pltpu.make_async_remote_copy(
                    src_ref=p_refs[a].at[2 * px + py], dst_ref=l_refs[a].at[r],
                    send_sem=send_sems.at[3 * a + r], recv_sem=recv_sems.at[3 * a + r],
                    device_id=(px, py, c), device_id_type=MESH)
                cp.start()
                cps.append(cp)
        for cp in cps:
            cp.wait()

    return pl.pallas_call(
        body, in_specs=[ANY] * n, out_specs=[ANY] * n,
        out_shape=[jax.ShapeDtypeStruct((3,) + p.shape[1:], p.dtype) for p in parts],
        scratch_shapes=[pltpu.SemaphoreType.DMA((3 * n,))] * 2,
        name=name)(*parts)


def join_halves(halves, *, name):
    n = len(halves)

    def body(*refs):
        h_refs, o_refs = refs[:n], refs[n:2 * n]
        send_sems, recv_sems, local_sems = refs[2 * n:]
        x, y, c, _ = _place()
        cps, own = [], []
        for a in range(n):
            h = halves[a].shape[0]
            win = o_refs[a].at[pl.ds(pl.multiple_of(c * h, 8), h)]
            lc = pltpu.make_async_copy(h_refs[a], win, local_sems.at[a])
            lc.start()
            own.append(lc)
            cp = pltpu.make_async_remote_copy(
                src_ref=h_refs[a], dst_ref=win, send_sem=send_sems.at[a], recv_sem=recv_sems.at[a],
                device_id=(x, y, 1 - c), device_id_type=MESH)
            cp.start()
            cps.append(cp)
        for a in range(n):
            h = halves[a].shape[0]
            other = o_refs[a].at[pl.ds(pl.multiple_of((1 - c) * h, 8), h)]
            pltpu.make_async_remote_copy(
                src_ref=h_refs[a], dst_ref=other, send_sem=send_sems.at[a], recv_sem=recv_sems.at[a],
                device_id=(x, y, 1 - c), device_id_type=MESH).wait_recv()
        for cp in cps:
            cp.wait_send()
        for lc in own:
            lc.wait()

    return pl.pallas_call(
        body, in_specs=[ANY] * n, out_specs=[ANY] * n,
        out_shape=[jax.ShapeDtypeStruct((2 * h.shape[0], h.shape[1]), h.dtype) for h in halves],
        scratch_shapes=[pltpu.SemaphoreType.DMA((n,))] * 3,
        name=name)(*halves)


def add_sibling(g, l, c, *, name):
    nb, R, C = g.shape
    h = R // 2
    tr = _pick(h, [256, 128, 64, 32, 16, 8])
    per = h // tr

    def body(c_ref, g_ref, l_ref, o_ref):
        o_ref[...] = g_ref[...] + l_ref[...]

    return pl.pallas_call(
        body,
        grid_spec=pltpu.PrefetchScalarGridSpec(
            num_scalar_prefetch=1, grid=(nb, per),
            in_specs=[pl.BlockSpec((None, tr, C), lambda k, i, c_ref: (k, c_ref[0] * per + i, 0)),
                      pl.BlockSpec((None, tr, C), lambda k, i, c_ref: (k, i, 0))],
            out_specs=pl.BlockSpec((None, tr, C), lambda k, i, c_ref: (k, i, 0))),
        out_shape=jax.ShapeDtypeStruct((nb, h, C), F32),
        compiler_params=_cparams(("arbitrary", "arbitrary")), name=name)(c, g, l)


def add_chips(part, land, k_me, *, name):
    _, H, C = part.shape
    tr = _pick(H, [256, 128, 64, 32, 16, 8])

    def body(k_ref, p_ref, l_ref, o_ref):
        o_ref[...] = ((p_ref[...] + l_ref[0]) + l_ref[1]) + l_ref[2]

    return pl.pallas_call(
        body,
        grid_spec=pltpu.PrefetchScalarGridSpec(
            num_scalar_prefetch=1, grid=(H // tr,),
            in_specs=[pl.BlockSpec((None, tr, C), lambda i, k_ref: (k_ref[0], i, 0)),
                      pl.BlockSpec((3, tr, C), lambda i, k_ref: (0, i, 0))],
            out_specs=pl.BlockSpec((tr, C), lambda i, k_ref: (i, 0))),
        out_shape=jax.ShapeDtypeStruct((H, C), F32),
        compiler_params=_cparams(("arbitrary",)), name=name)(k_me, part, land)


def allgather_sum(v, *, name):
    m_per, n = v.shape

    def body(x_ref, out_ref, sum_ref, send_sems, recv_sems, local_sem):
        x, y, c, chips = _place()
        me, sibling = (x, y, c), (x, y, 1 - c)

        def rows(px, py, pc):
            return out_ref.at[pl.ds(pl.multiple_of((4 * px + 2 * py + pc) * m_per, 8), m_per), :]

        def copy(k, block, to, src=None):
            return pltpu.make_async_remote_copy(
                src_ref=rows(*block) if src is None else src, dst_ref=rows(*block),
                send_sem=send_sems.at[k], recv_sem=recv_sems.at[k], device_id=to, device_id_type=MESH)

        mine = pltpu.make_async_copy(x_ref, rows(*me), local_sem)
        mine.start()
        first = [copy(0, me, sibling, src=x_ref)]
        first += [copy(1 + j, me, (*chip, c), src=x_ref) for j, chip in enumerate(chips)]
        for cp in first:
            cp.start()
        passed = [copy(4 + j, (*chip, c), sibling) for j, chip in enumerate(chips)]
        for j, chip in enumerate(chips):
            copy(1 + j, (*chip, c), me).wait_recv()
            passed[j].start()
        copy(0, sibling, me).wait_recv()
        for j, chip in enumerate(chips):
            copy(4 + j, (*chip, 1 - c), me).wait_recv()
        for cp in first + passed:
            cp.wait_send()
        mine.wait()
        acc = out_ref[0:m_per, :]
        for d in range(1, N_DEV):
            acc = acc + out_ref[d * m_per:(d + 1) * m_per, :]
        sum_ref[...] = acc

    vm = pl.BlockSpec(memory_space=pltpu.VMEM)
    return pl.pallas_call(
        body, in_specs=[vm], out_specs=[vm, vm],
        out_shape=[jax.ShapeDtypeStruct((N_DEV * m_per, n), v.dtype), jax.ShapeDtypeStruct((m_per, n), v.dtype)],
        scratch_shapes=[pltpu.SemaphoreType.DMA((7,)), pltpu.SemaphoreType.DMA((7,)), pltpu.SemaphoreType.DMA],
        compiler_params=pltpu.CompilerParams(vmem_limit_bytes=VMEM_LIMIT_BYTES), name=name)(v)


def _pack(arrs):
    cols = []
    for a in arrs:
        f = a.reshape(-1)
        pad = (-f.shape[0]) % 128
        cols.append(jnp.pad(f, (0, pad)).reshape(-1, 128))
    out = jnp.concatenate(cols, axis=0)
    pad = (-out.shape[0]) % 8
    return jnp.pad(out, ((0, pad), (0, 0)))


def _unpack(buf, shapes):
    outs, r = [], 0
    for s in shapes:
        nel = math.prod(s)
        nr = -(-nel // 128)
        outs.append(buf[r:r + nr].reshape(-1)[:nel].reshape(s))
        r += nr
    return outs


GA_CONV_OUT, GA_MIX_OUT, GA_WQ, GA_WO, GA_DOWN, GA_UP, GA_ROWS = 0, 256, 512, 768, 1024, 2048, 3072


def kernel(x, mem, in_norm_g, in_norm_b, w_in, conv_dw, conv_db, conv_norm_g, conv_norm_b, w_conv_out, ssm_log_step, ssm_lambda_re, ssm_lambda_im, ssm_b_re, ssm_b_im, ssm_c_re, ssm_c_im, ssm_d, w_ssm_glu, w_mix_out, ln1_g, ln1_b, xa_wq, xa_wkv, xa_wo, ln2_g, ln2_b, mlp_w_up, mlp_w_down, ln3_g, ln3_b, loss_target, m_in_norm_g, m_in_norm_b, m_w_in, m_conv_dw, m_conv_db, m_conv_norm_g, m_conv_norm_b, m_w_conv_out, m_ssm_log_step, m_ssm_lambda_re, m_ssm_lambda_im, m_ssm_b_re, m_ssm_b_im, m_ssm_c_re, m_ssm_c_im, m_ssm_d, m_w_ssm_glu, m_w_mix_out, m_ln1_g, m_ln1_b, m_xa_wq, m_xa_wkv, m_xa_wo, m_ln2_g, m_ln2_b, m_mlp_w_up, m_mlp_w_down, m_ln3_g, m_ln3_b, v_in_norm_g, v_in_norm_b, v_w_in, v_conv_dw, v_conv_db, v_conv_norm_g, v_conv_norm_b, v_w_conv_out, v_ssm_log_step, v_ssm_lambda_re, v_ssm_lambda_im, v_ssm_b_re, v_ssm_b_im, v_ssm_c_re, v_ssm_c_im, v_ssm_d, v_w_ssm_glu, v_w_mix_out, v_ln1_g, v_ln1_b, v_xa_wq, v_xa_wkv, v_xa_wo, v_ln2_g, v_ln2_b, v_mlp_w_up, v_mlp_w_down, v_ln3_g, v_ln3_b):
    D = D_MODEL
    xs = x[0]
    T = xs.shape[0]
    mems = mem[0]
    NM = mems.shape[0]
    tgt = loss_target[0]
    my_c = lax.axis_index("c")
    k_me = 2 * lax.axis_index("x") + lax.axis_index("y")
    c_arr = jnp.reshape(my_c, (1,)).astype(jnp.int32)
    k_arr = jnp.reshape(k_me, (1,)).astype(jnp.int32)

    sh_a = jnp.concatenate([w_conv_out[0], w_mix_out[0], xa_wq[0], xa_wo[0], mlp_w_down[0], mlp_w_up[0]], axis=0).astype(BF16)
    GA, GIN, GKV, GGLU = allgather_weights(
        [sh_a, w_in[0].astype(BF16), xa_wkv[0].astype(BF16), w_ssm_glu[0].astype(BF16)], name="allgather_weights")
    dw_pad = jnp.pad(conv_dw[0], ((0, CONV_HALO - CONV_K), (0, 0)))
    dw_all, _ = allgather_sum(dw_pad, name="allgather_conv_dw")
    dw_full = jnp.transpose(dw_all.reshape(N_DEV, CONV_HALO, D // N_CHIPS)[::2], (1, 0, 2)).reshape(CONV_HALO, D)

    def ga_rows(row0, tk):
        per = 256 // tk
        return lambda kk: (kk // per, row0 // tk + kk % per)

    def b_rowshard(row0, tn, tk=256):
        f = ga_rows(row0, tk)
        return ((None, tk, tn), lambda i, j, k: (*f(k), j))

    _, h0, h0b = ln_fwd(xs, in_norm_g, in_norm_b, name="ln0_fwd")
    p = mm_nn(h0b, GIN, ((None, D, 1152), lambda i, j, k: (j, 0, 0)), D_IN, tn=1152, tk=D, name="mm_w_in")[0]
    c_pre, actb = conv_fwd(p, dw_full, conv_db, conv_norm_g[0].reshape(1, D), conv_norm_b[0].reshape(1, D), name="conv_fwd")
    ya = mm_nn(actb, GA, b_rowshard(GA_CONV_OUT, D), D, tn=D, tk=256, name="mm_conv_out")[0]

    lstep, lre, lim = ssm_log_step[0], ssm_lambda_re[0], ssm_lambda_im[0]
    bre, bim, cre, cim = ssm_b_re[0], ssm_b_im[0], ssm_c_re[0], ssm_c_im[0]
    (ar, ai, bbr, bbi), disc_vjp = jax.vjp(_ssm_discretise, lstep, lre, lim, bre, bim)
    Br, Bi = _blockdiag_in(bbr), _blockdiag_in(bbi)
    Cr, Ci = _blockdiag_out(cre), _blockdiag_out(cim)
    pw_r, pw_i = _powers(ar.reshape(-1), ai.reshape(-1))
    dvec = ssm_d[0].reshape(1, D_SSM)
    xr, xi, yssm = ssm_fwd(p, Br, Bi, Cr, Ci, pw_r, pw_i, dvec, name="ssm_fwd")
    z = mm_nn(yssm, GGLU, ((None, D_SSM, 512), lambda i, j, k: (j, 0, 0)), 2 * D, tn=512, tk=D_SSM, name="mm_ssm_glu")[0]
    mergedb = merge_fwd(p, ya, z, name="merge_fwd")
    mix = mm_nn(mergedb, GA, b_rowshard(GA_MIX_OUT, D), D, tn=D, tk=256, name="mm_mix_out")[0]
    r1, h1, h1b = ln_fwd(mix, ln1_g[0], ln1_b[0], res=h0, name="ln1_fwd")

    qb = mm_nn(h1b, GA, b_rowshard(GA_WQ, D), D, tn=D, tk=256, out_dtype=BF16, name="mm_wq")[0]
    kv = mm_nn(mems, GKV, ((None, D, 512), lambda i, j, k: (j, 0, 0)), 2 * D, tn=512, tk=D, name="mm_wkv")[0]
    ob = attn_fwd(qb, kv, name="attn_fwd")
    xa = mm_nn(ob, GA, b_rowshard(GA_WO, D), D, tn=D, tk=256, name="mm_wo")[0]
    r2, h2, h2b = ln_fwd(xa, ln2_g[0], ln2_b[0], res=h1, name="ln2_fwd")

    def relu2(acc):
        zr = jnp.maximum(acc, 0.0)
        return acc, zr * zr

    zpre, zzb = mm_nn(h2b, GA, ((None, D, D), lambda i, j, k: (j, GA_UP // D, 0)), D_FF, tn=D, tk=D,
                      out_dtypes=[F32, BF16], epilogue=relu2, name="mm_up")
    ff = mm_nn(zzb, GA, ((None, D, D), lambda i, j, k: (k, GA_DOWN // D, 0)), D, tn=D, tk=D, name="mm_down")[0]
    r3, h3, _ = ln_fwd(ff, ln3_g[0], ln3_b[0], res=h2, name="ln3_fwd")
    dh3, sq = loss_head(h3, tgt, name="loss_head")
    loss = lax.psum(0.5 * sq[0, 0] / D, ("x", "y", "c"))

    ga_shape = jax.ShapeDtypeStruct((N_CHIPS, GA_ROWS, D), F32)
    dr3, dr3b, dg3, db3 = ln_bwd(r3, dh3, ln3_g[0], name="ln3_bwd")
    dzpreb = mm_nt(dr3b, GA, ((None, D, D), lambda i, j, k: (j, GA_DOWN // D, 0)), D_FF, tn=D, tk=D, out_dtype=BF16,
                   extras=(zpre,), epilogue=lambda acc, zp: (acc * (2.0 * jnp.maximum(zp, 0.0)),), name="mm_down_t")[0]
    GAg = mm_tn(zzb, dr3b, tm=D, tn=D, out_spec=((None, D, D), lambda i, j, k: (i, GA_DOWN // D, 0)), out_shape=ga_shape,
                name="mm_down_g")
    GAg = mm_tn(h2b, dzpreb, tm=D, tn=D, out_spec=((None, D, D), lambda i, j, k: (j, GA_UP // D, 0)), out_shape=ga_shape,
                out_buf=GAg, name="mm_up_g")
    dh2 = mm_nt(dzpreb, GA, ((None, D, D), lambda i, j, k: (k, GA_UP // D, 0)), D, tn=D, tk=D,
                extras=(dr3,), epilogue=lambda acc, d: (acc + ALPHA * d,), name="mm_up_t")[0]
    dr2, dr2b, dg2, db2 = ln_bwd(r2, dh2, ln2_g[0], name="ln2_bwd")

    def b_rowshard_t(row0):
        return ((None, 256, D), lambda i, j, k: (j, row0 // 256, 0))

    def g_rowshard(row0):
        return ((None, 256, D), lambda i, j, k: (i, row0 // 256, 0))

    dob = mm_nt(dr2b, GA, b_rowshard_t(GA_WO), D, tm=1024 if T % 1024 == 0 else None, tn=256, tk=D, out_dtype=BF16, name="mm_wo_t")[0]
    GAg = mm_tn(ob, dr2b, tm=256, tn=D, out_spec=g_rowshard(GA_WO), out_shape=ga_shape, out_buf=GAg, name="mm_wo_g")
    dqb, dkv = attn_bwd(qb, kv, dob, name="attn_bwd")
    GAg = mm_tn(h1b, dqb, tm=256, tn=D, out_spec=g_rowshard(GA_WQ), out_shape=ga_shape, out_buf=GAg, name="mm_wq_g")
    GKVg = mm_tn(mems, dkv, tm=D, tn=512, tk=NM, out_spec=((None, D, 512), lambda i, j, k: (j, 0, 0)),
                 out_shape=jax.ShapeDtypeStruct((N_CHIPS, D, 512), F32), name="mm_wkv_g")
    dh1 = mm_nt(dqb, GA, b_rowshard_t(GA_WQ), D, tm=1024 if T % 1024 == 0 else None, tn=256, tk=D,
                extras=(dr2,), epilogue=lambda acc, d: (acc + ALPHA * d,), name="mm_wq_t")[0]
    dr1, dr1b, dg1, db1 = ln_bwd(r1, dh1, ln1_g[0], name="ln1_bwd")

    dmerged = mm_nt(dr1b, GA, b_rowshard_t(GA_MIX_OUT), D, tm=1024 if T % 1024 == 0 else None, tn=256, tk=D, name="mm_mix_t")[0]
    GAg = mm_tn(mergedb, dr1b, tm=256, tn=D, out_spec=g_rowshard(GA_MIX_OUT), out_shape=ga_shape, out_buf=GAg, name="mm_mix_g")
    dyab, dgab, dgbb, dz1b, dz2b = merge_bwd(dmerged, p, ya, z, name="merge_bwd")
    dzb = jnp.concatenate([dz1b, dz2b], axis=1)
    GGLUg = mm_tn(yssm, dzb, tm=D_SSM, tn=512, out_spec=((None, D_SSM, 512), lambda i, j, k: (j, 0, 0)),
                  out_shape=jax.ShapeDtypeStruct((N_CHIPS, D_SSM, 512), F32), name="mm_glu_g")
    dyssm = mm_nt(dzb, GGLU, ((None, D_SSM, 512), lambda i, j, k: (k, 0, 0)), D_SSM, tn=D_SSM, tk=512, name="mm_glu_t")[0]
    pwc_r, pwc_i = pw_r, -pw_i
    dub, dBr, dBi, dCr, dCi, dar8, dai8, dd8 = ssm_bwd(dyssm, p, xr, xi, Br, Bi, Cr, Ci, pwc_r, pwc_i, dvec, name="ssm_bwd")
    dar = jnp.sum(dar8, axis=0).reshape(SSM_GROUPS, SSM_STATE)
    dai = jnp.sum(dai8, axis=0).reshape(SSM_GROUPS, SSM_STATE)
    g_lstep, g_lre, g_lim, g_bre, g_bim = disc_vjp((dar, dai, _diag_in(dBr), _diag_in(dBi)))
    g_cre, g_cim = _diag_out(dCr), _diag_out(dCi)
    g_d = jnp.sum(dd8, axis=0).reshape(1, D_SSM)

    dact = mm_nt(dyab, GA, b_rowshard_t(GA_CONV_OUT), D, tm=1024 if T % 1024 == 0 else None, tn=256, tk=D, name="mm_conv_out_t")[0]
    GAg = mm_tn(actb, dyab, tm=256, tn=D, out_spec=g_rowshard(GA_CONV_OUT), out_shape=ga_shape, out_buf=GAg, name="mm_conv_out_g")
    dc, dng, dnb, ddb = conv_bwd_norm(dact, c_pre, conv_norm_g[0].reshape(1, D), conv_norm_b[0].reshape(1, D), name="conv_bwd_norm")
    dvgb, ddw = conv_bwd_taps(dc, p, dw_full, name="conv_bwd_taps")
    dpb = jnp.concatenate([dvgb, dub, dgab, dgbb], axis=1)
    GINg = mm_tn(h0b, dpb, tm=D, tn=1152, out_spec=((None, D, 1152), lambda i, j, k: (j, 0, 0)),
                 out_shape=jax.ShapeDtypeStruct((N_CHIPS, D, 1152), F32), name="mm_w_in_g")
    dh0 = mm_nt(dpb, GIN, ((None, 512, 1152), lambda i, j, k: (k, j, 0)), D, tn=512, tk=1152,
                extras=(dr1,), epilogue=lambda acc, d: (acc + ALPHA * d,), name="mm_w_in_t")[0]
    gx, _, dg0, db0 = ln_bwd(xs, dh0, in_norm_g, name="ln0_bwd")

    big = [GAg, GINg, GKVg, GGLUg]
    tags = ["a", "in", "kv", "glu"]
    lands = exchange_halves(big, name="rs_exchange_halves")
    parts = [add_sibling(g, l, c_arr, name="rs_add_sibling_" + t) for g, l, t in zip(big, lands, tags)]
    lands2 = scatter_to_owners(parts, name="rs_scatter_to_owners")
    halves = [add_chips(pt, l2, k_arr, name="rs_add_chips_" + t) for pt, l2, t in zip(parts, lands2, tags)]
    gA, gIN, gKV, gGLU = join_halves(halves, name="rs_join_halves")

    small_names = ["in_norm_g", "in_norm_b", "conv_db", "conv_norm_g", "conv_norm_b", "ssm_log_step", "ssm_lambda_re",
                   "ssm_lambda_im", "ssm_b_re", "ssm_b_im", "ssm_c_re", "ssm_c_im", "ssm_d", "ln1_g", "ln1_b",
                   "ln2_g", "ln2_b", "ln3_g", "ln3_b"]
    small_w = [in_norm_g, in_norm_b, conv_db, conv_norm_g, conv_norm_b, ssm_log_step, ssm_lambda_re, ssm_lambda_im,
               ssm_b_re, ssm_b_im, ssm_c_re, ssm_c_im, ssm_d, ln1_g, ln1_b, ln2_g, ln2_b, ln3_g, ln3_b]
    small_m = [m_in_norm_g, m_in_norm_b, m_conv_db, m_conv_norm_g, m_conv_norm_b, m_ssm_log_step, m_ssm_lambda_re,
               m_ssm_lambda_im, m_ssm_b_re, m_ssm_b_im, m_ssm_c_re, m_ssm_c_im, m_ssm_d, m_ln1_g, m_ln1_b, m_ln2_g,
               m_ln2_b, m_ln3_g, m_ln3_b]
    small_v = [v_in_norm_g, v_in_norm_b, v_conv_db, v_conv_norm_g, v_conv_norm_b, v_ssm_log_step, v_ssm_lambda_re,
               v_ssm_lambda_im, v_ssm_b_re, v_ssm_b_im, v_ssm_c_re, v_ssm_c_im, v_ssm_d, v_ln1_g, v_ln1_b, v_ln2_g,
               v_ln2_b, v_ln3_g, v_ln3_b]
    small_g = [dg0, db0, ddb, dng, dnb, g_lstep, g_lre, g_lim, g_bre, g_bim, g_cre, g_cim, g_d, dg1, db1, dg2, db2, dg3, db3]
    small_shapes = [w.shape for w in small_w]
    n_small_rows = _pack(small_w).shape[0]
    packed_g = _pack(small_g + [ddw])
    _, summed = allgather_sum(packed_g, name="allreduce_small")
    small_rows = sum(-(-math.prod(s) // 128) for s in small_shapes)
    ddw_full = summed[small_rows:small_rows + CONV_HALO * D // 128].reshape(CONV_HALO, D)
    g_dw = lax.dynamic_slice_in_dim(ddw_full, k_me * (D // N_CHIPS), D // N_CHIPS, axis=1)
    gs_packed = jnp.pad(summed[:small_rows], ((0, n_small_rows - small_rows), (0, 0)))

    res = {}

    def upd(nm, w, m, v, g_arr, row0=0):
        shp = w.shape
        w2, m2, v2 = (a.reshape(-1, shp[-1]) for a in (w, m, v))
        outs = adamw(w2, m2, v2, g_arr, row0, name="adamw_" + nm)
        res[nm] = tuple(o.reshape(shp) for o in outs)

    upd("w_conv_out", w_conv_out, m_w_conv_out, v_w_conv_out, gA, GA_CONV_OUT)
    upd("w_mix_out", w_mix_out, m_w_mix_out, v_w_mix_out, gA, GA_MIX_OUT)
    upd("xa_wq", xa_wq, m_xa_wq, v_xa_wq, gA, GA_WQ)
    upd("xa_wo", xa_wo, m_xa_wo, v_xa_wo, gA, GA_WO)
    upd("mlp_w_down", mlp_w_down, m_mlp_w_down, v_mlp_w_down, gA, GA_DOWN)
    upd("mlp_w_up", mlp_w_up, m_mlp_w_up, v_mlp_w_up, gA, GA_UP)
    upd("w_in", w_in, m_w_in, v_w_in, gIN)
    upd("xa_wkv", xa_wkv, m_xa_wkv, v_xa_wkv, gKV)
    upd("w_ssm_glu", w_ssm_glu, m_w_ssm_glu, v_w_ssm_glu, gGLU)
    pad_dw = lambda a: jnp.pad(a[0], ((0, CONV_HALO - CONV_K), (0, 0)))
    dw_outs = adamw(pad_dw(conv_dw), pad_dw(m_conv_dw), pad_dw(v_conv_dw), g_dw, 0, name="adamw_conv_dw")
    res["conv_dw"] = tuple(o[:CONV_K][None] for o in dw_outs)
    sm_outs = adamw(_pack(small_w), _pack(small_m), _pack(small_v), gs_packed, 0, name="adamw_small")
    sm_un = [_unpack(o, small_shapes) for o in sm_outs]
    for idx, nm in enumerate(small_names):
        res[nm] = tuple(sm_un[q][idx] for q in range(4))

    order = ["in_norm_g", "in_norm_b", "w_in", "conv_dw", "conv_db", "conv_norm_g", "conv_norm_b", "w_conv_out",
             "ssm_log_step", "ssm_lambda_re", "ssm_lambda_im", "ssm_b_re", "ssm_b_im", "ssm_c_re", "ssm_c_im", "ssm_d",
             "w_ssm_glu", "w_mix_out", "ln1_g", "ln1_b", "xa_wq", "xa_wkv", "xa_wo", "ln2_g", "ln2_b", "mlp_w_up",
             "mlp_w_down", "ln3_g", "ln3_b"]
    return (loss, gx[None], *[res[n][0] for n in order], *[res[n][1] for n in order],
            *[res[n][2] for n in order], *[res[n][3] for n in order])
```

```python
import functools
import math

import jax
import jax.numpy as jnp
from jax import lax
from jax.experimental import pallas as pl
from jax.experimental.pallas import tpu as pltpu

F32 = jnp.float32
BF16 = jnp.bfloat16
MESH = pl.DeviceIdType.MESH

D_MODEL = 1024
N_HEADS = 4
HEAD_DIM = D_MODEL // N_HEADS
CONV_K = 31
CONV_HALO = 32
D_SSM = 512
SSM_GROUPS = 32
SSM_GROUP = 16
SSM_STATE = 64
SSM_BLOCKS = 4
SSM_BLOCK_IN = D_SSM // SSM_BLOCKS
SSM_BLOCK_STATE = SSM_GROUPS * SSM_STATE // SSM_BLOCKS
D_FF = 4096
D_IN = 4608
LN_EPS = 1e-5
ALPHA = (2.0 * 1) ** 0.25
N_CHIPS = 4
N_DEV = 8
ADAM_LR, ADAM_B1, ADAM_B2, ADAM_EPS, ADAM_WD, ADAM_STEP = 0.001, 0.9, 0.999, 1e-08, 0.01, 10
VMEM_LIMIT_BYTES = 56 * 1024 * 1024


def _pick(dim, cands):
    for c in cands:
        if dim % c == 0:
            return c
    return dim


def _cparams(sem=None):
    return pltpu.CompilerParams(dimension_semantics=sem, vmem_limit_bytes=VMEM_LIMIT_BYTES)


def _sigmoid(x):
    return 1.0 / (1.0 + jnp.exp(-x))


_DIMS = {"nn": (((1,), (0,)), ((), ())), "nt": (((1,), (1,)), ((), ())), "tn": (((0,), (0,)), ((), ()))}


def matmul(a, b, *, mode, M, N, K, tm, tn, tk, a_spec, b_spec, out_specs, out_shapes, name,
           extras=(), extra_specs=(), epilogue=None, alias_buf=None, b_view=None):
    nk = K // tk
    ne = len(extras)
    no = len(out_shapes)
    na = 0 if alias_buf is None else 1
    dims = _DIMS[mode]

    def body(*refs):
        a_ref, b_ref = refs[0], refs[1]
        e_refs = refs[2:2 + ne]
        o_refs = refs[2 + ne + na:2 + ne + na + no]

        def finish(acc):
            outs = (acc,) if epilogue is None else epilogue(acc, *[r[...] for r in e_refs])
            for o, r in zip(outs, o_refs):
                r[...] = o.astype(r.dtype).reshape(r.shape)

        b_blk = b_ref[...] if b_view is None else b_ref[...].reshape(b_view)
        prod = lax.dot_general(a_ref[...].astype(BF16), b_blk.astype(BF16), dims, preferred_element_type=F32)
        if nk == 1:
            finish(prod)
        else:
            acc_ref = refs[-1]
            k = pl.program_id(2)

            @pl.when(k == 0)
            def _():
                acc_ref[...] = prod

            @pl.when(k > 0)
            def _():
                acc_ref[...] += prod

            @pl.when(k == nk - 1)
            def _():
                finish(acc_ref[...])

    in_specs = [pl.BlockSpec(*a_spec), pl.BlockSpec(*b_spec)] + [pl.BlockSpec(*s) for s in extra_specs]
    ins = [a, b, *extras]
    if alias_buf is not None:
        in_specs.append(pl.BlockSpec(memory_space=pl.ANY))
        ins.append(alias_buf)
    res = pl.pallas_call(
        body,
        grid=(M // tm, N // tn, nk),
        in_specs=in_specs,
        out_specs=[pl.BlockSpec(*s) for s in out_specs],
        out_shape=out_shapes,
        scratch_shapes=[] if nk == 1 else [pltpu.VMEM((tm, tn), F32)],
        input_output_aliases={2 + ne: 0} if alias_buf is not None else {},
        compiler_params=_cparams(("parallel", "parallel", "arbitrary")),
        name=name,
    )(*ins)
    return res


def _mn(tm, tn):
    return ((tm, tn), lambda i, j, k: (i, j))


def mm_nn(a, b_arr, b_spec, N, *, name, tm=None, tn, tk, out_dtype=F32, extras=(), epilogue=None, out_dtypes=None,
          b_view=None):
    M, K = a.shape
    tm = tm or _pick(M, [1024, 512, 256, 128])
    dts = out_dtypes or [out_dtype]
    return matmul(a, b_arr, mode="nn", M=M, N=N, K=K, tm=tm, tn=tn, tk=tk,
                  a_spec=((tm, tk), lambda i, j, k: (i, k)), b_spec=b_spec, b_view=b_view,
                  out_specs=[_mn(tm, tn)] * len(dts), out_shapes=[jax.ShapeDtypeStruct((M, N), d) for d in dts],
                  extras=extras, extra_specs=[_mn(tm, tn)] * len(extras), epilogue=epilogue, name=name)


def mm_nt(a, b_arr, b_spec, N, *, name, tm=None, tn, tk, out_dtype=F32, extras=(), epilogue=None, out_dtypes=None,
          b_view=None):
    M, K = a.shape
    tm = tm or _pick(M, [1024, 512, 256, 128])
    dts = out_dtypes or [out_dtype]
    return matmul(a, b_arr, mode="nt", M=M, N=N, K=K, tm=tm, tn=tn, tk=tk,
                  a_spec=((tm, tk), lambda i, j, k: (i, k)), b_spec=b_spec, b_view=b_view,
                  out_specs=[_mn(tm, tn)] * len(dts), out_shapes=[jax.ShapeDtypeStruct((M, N), d) for d in dts],
                  extras=extras, extra_specs=[_mn(tm, tn)] * len(extras), epilogue=epilogue, name=name)


def mm_tn(a, b, *, name, tm, tn, tk=None, out_spec, out_shape, out_buf=None):
    K, M = a.shape
    N = b.shape[1]
    tk = tk or _pick(K, [1024, 512, 256, 128])
    return matmul(a, b, mode="tn", M=M, N=N, K=K, tm=tm, tn=tn, tk=tk,
                  a_spec=((tk, tm), lambda i, j, k: (k, i)), b_spec=((tk, tn), lambda i, j, k: (k, j)),
                  out_specs=[out_spec], out_shapes=[out_shape], alias_buf=out_buf, name=name)[0]


def _rows(tc, w, cb=0):
    return pl.BlockSpec((tc, w), lambda i: (i, cb))


def _const(shape):
    return pl.BlockSpec(shape, lambda i: tuple([0] * len(shape)))


def _ln_stats(r):
    mu = jnp.mean(r, axis=-1, keepdims=True)
    xc = r - mu
    var = jnp.mean(xc * xc, axis=-1, keepdims=True)
    rstd = lax.rsqrt(var + LN_EPS)
    return xc * rstd, rstd


def _rowsum8(v):
    tc, w = v.shape
    return jnp.sum(v.reshape(tc // 8, 8, w), axis=0)


def ln_fwd(x, g, b, *, name, res=None):
    T, D = x.shape
    tc = _pick(T, [512, 256, 128])
    has_res = res is not None

    def body(*refs):
        if has_res:
            x_ref, res_ref, g_ref, b_ref, r_ref, h_ref, hb_ref = refs
            r = ALPHA * res_ref[...] + x_ref[...]
            r_ref[...] = r
        else:
            x_ref, g_ref, b_ref, h_ref, hb_ref = refs
            r = x_ref[...]
        xhat, _ = _ln_stats(r)
        y = xhat * g_ref[...] + b_ref[...]
        h_ref[...] = y
        hb_ref[...] = y.astype(BF16)

    ins = [x] + ([res] if has_res else []) + [g.reshape(1, D), b.reshape(1, D)]
    in_specs = [_rows(tc, D)] * (2 if has_res else 1) + [_const((1, D))] * 2
    n_out = 3 if has_res else 2
    outs = pl.pallas_call(
        body, grid=(T // tc,), in_specs=in_specs, out_specs=[_rows(tc, D)] * n_out,
        out_shape=[jax.ShapeDtypeStruct((T, D), F32)] * (n_out - 1) + [jax.ShapeDtypeStruct((T, D), BF16)],
        compiler_params=_cparams(("arbitrary",)), name=name)(*ins)
    if has_res:
        return outs
    return (x,) + tuple(outs)


def ln_bwd(r, dy, g, *, name):
    T, D = r.shape
    tc = _pick(T, [512, 256, 128])
    nt = T // tc

    def body(r_ref, dy_ref, g_ref, dr_ref, drb_ref, dg_ref, db_ref, accg, accb):
        i = pl.program_id(0)

        @pl.when(i == 0)
        def _():
            accg[...] = jnp.zeros_like(accg)
            accb[...] = jnp.zeros_like(accb)

        xhat, rstd = _ln_stats(r_ref[...])
        dy = dy_ref[...]
        dxh = dy * g_ref[...]
        m1 = jnp.mean(dxh, axis=-1, keepdims=True)
        m2 = jnp.mean(dxh * xhat, axis=-1, keepdims=True)
        dr = rstd * (dxh - m1 - xhat * m2)
        dr_ref[...] = dr
        drb_ref[...] = dr.astype(BF16)
        accg[...] += _rowsum8(dy * xhat)
        accb[...] += _rowsum8(dy)

        @pl.when(i == nt - 1)
        def _():
            dg_ref[...] = jnp.sum(accg[...], axis=0, keepdims=True)
            db_ref[...] = jnp.sum(accb[...], axis=0, keepdims=True)

    return pl.pallas_call(
        body, grid=(nt,), in_specs=[_rows(tc, D), _rows(tc, D), _const((1, D))],
        out_specs=[_rows(tc, D), _rows(tc, D), _const((1, D)), _const((1, D))],
        out_shape=[jax.ShapeDtypeStruct((T, D), F32), jax.ShapeDtypeStruct((T, D), BF16),
                   jax.ShapeDtypeStruct((1, D), F32), jax.ShapeDtypeStruct((1, D), F32)],
        scratch_shapes=[pltpu.VMEM((8, D), F32), pltpu.VMEM((8, D), F32)],
        compiler_params=_cparams(("arbitrary",)), name=name)(r, dy, g.reshape(1, D))


def loss_head(y, target, *, name):
    T, D = y.shape
    tc = _pick(T, [512, 256, 128])
    nt = T // tc

    def body(y_ref, t_ref, dy_ref, loss_ref, acc):
        i = pl.program_id(0)

        @pl.when(i == 0)
        def _():
            acc[...] = jnp.zeros_like(acc)

        e = y_ref[...] - t_ref[...]
        dy_ref[...] = e * (1.0 / D)
        acc[...] += _rowsum8(e * e)

        @pl.when(i == nt - 1)
        def _():
            s = jnp.sum(jnp.sum(acc[...], axis=0, keepdims=True), axis=1, keepdims=True)
            loss_ref[...] = jnp.broadcast_to(s, (1, 128))

    return pl.pallas_call(
        body, grid=(nt,), in_specs=[_rows(tc, D), _rows(tc, D)],
        out_specs=[_rows(tc, D), _const((1, 128))],
        out_shape=[jax.ShapeDtypeStruct((T, D), F32), jax.ShapeDtypeStruct((1, 128), F32)],
        scratch_shapes=[pltpu.VMEM((8, D), F32)],
        compiler_params=_cparams(("arbitrary",)), name=name)(y, target)


def _halo_prev(tc):
    per = tc // CONV_HALO
    return lambda i: jnp.maximum(i * per - 1, 0)


def conv_fwd(p, dw, db, ng, nb, *, name):
    T = p.shape[0]
    D = D_MODEL
    tc = _pick(T, [256, 128])
    prev = _halo_prev(tc)

    def body(val_ref, gate_ref, valp_ref, gatep_ref, dw_ref, db_ref, ng_ref, nb_ref, c_ref, act_ref, ext):
        i = pl.program_id(0)
        u_prev = valp_ref[...] * _sigmoid(gatep_ref[...])
        ext[0:CONV_HALO, :] = jnp.where(i > 0, u_prev, 0.0)
        ext[CONV_HALO:CONV_HALO + tc, :] = val_ref[...] * _sigmoid(gate_ref[...])
        off = CONV_HALO - (CONV_K - 1)
        acc = jnp.zeros((tc, D), F32)
        for k in range(CONV_K):
            acc = acc + dw_ref[k:k + 1, :] * ext[off + k:off + k + tc, :]
        c = acc + db_ref[...]
        c_ref[...] = c
        xhat, _ = _ln_stats(c)
        cn = xhat * ng_ref[...] + nb_ref[...]
        act_ref[...] = (cn * _sigmoid(cn)).astype(BF16)

    return pl.pallas_call(
        body, grid=(T // tc,),
        in_specs=[_rows(tc, D, 0), _rows(tc, D, 1),
                  pl.BlockSpec((CONV_HALO, D), lambda i: (prev(i), 0)), pl.BlockSpec((CONV_HALO, D), lambda i: (prev(i), 1)),
                  _const((CONV_HALO, D)), _const((1, D)), _const((1, D)), _const((1, D))],
        out_specs=[_rows(tc, D), _rows(tc, D)],
        out_shape=[jax.ShapeDtypeStruct((T, D), F32), jax.ShapeDtypeStruct((T, D), BF16)],
        scratch_shapes=[pltpu.VMEM((CONV_HALO + tc, D), F32)],
        compiler_params=_cparams(("arbitrary",)), name=name)(p, p, p, p, dw, db, ng, nb)


def conv_bwd_norm(dact, c_pre, ng, nb, *, name):
    T, D = c_pre.shape
    tc = _pick(T, [512, 256, 128])
    nt = T // tc

    def body(da_ref, c_ref, ng_ref, nb_ref, dc_ref, dng_ref, dnb_ref, ddb_ref, accg, accb, accd):
        i = pl.program_id(0)

        @pl.when(i == 0)
        def _():
            accg[...] = jnp.zeros_like(accg)
            accb[...] = jnp.zeros_like(accb)
            accd[...] = jnp.zeros_like(accd)

        xhat, rstd = _ln_stats(c_ref[...])
        cn = xhat * ng_ref[...] + nb_ref[...]
        s = _sigmoid(cn)
        dcn = da_ref[...] * (s * (1.0 + cn * (1.0 - s)))
        dxh = dcn * ng_ref[...]
        m1 = jnp.mean(dxh, axis=-1, keepdims=True)
        m2 = jnp.mean(dxh * xhat, axis=-1, keepdims=True)
        dc = rstd * (dxh - m1 - xhat * m2)
        dc_ref[...] = dc
        accg[...] += _rowsum8(dcn * xhat)
        accb[...] += _rowsum8(dcn)
        accd[...] += _rowsum8(dc)

        @pl.when(i == nt - 1)
        def _():
            dng_ref[...] = jnp.sum(accg[...], axis=0, keepdims=True)
            dnb_ref[...] = jnp.sum(accb[...], axis=0, keepdims=True)
            ddb_ref[...] = jnp.sum(accd[...], axis=0, keepdims=True)

    vec = jax.ShapeDtypeStruct((1, D), F32)
    return pl.pallas_call(
        body, grid=(nt,), in_specs=[_rows(tc, D), _rows(tc, D), _const((1, D)), _const((1, D))],
        out_specs=[_rows(tc, D), _const((1, D)), _const((1, D)), _const((1, D))],
        out_shape=[jax.ShapeDtypeStruct((T, D), F32), vec, vec, vec],
        scratch_shapes=[pltpu.VMEM((8, D), F32)] * 3,
        compiler_params=_cparams(("arbitrary",)), name=name)(dact, c_pre, ng, nb)


def conv_bwd_taps(dc, p, dw, *, name):
    T, D = dc.shape
    tc = _pick(T, [256, 128])
    nt = T // tc
    per = tc // CONV_HALO
    prev = _halo_prev(tc)
    last_halo = T // CONV_HALO - 1
    nxt = lambda i: jnp.minimum((i + 1) * per, last_halo)
    off = CONV_HALO - (CONV_K - 1)

    def body(dc_ref, dcn_ref, val_ref, gate_ref, valp_ref, gatep_ref, dw_ref, dvg_ref, ddw_ref, ext_u, ext_d, acc):
        i = pl.program_id(0)

        @pl.when(i == 0)
        def _():
            acc[...] = jnp.zeros_like(acc)

        val = val_ref[...]
        sg = _sigmoid(gate_ref[...])
        u_prev = valp_ref[...] * _sigmoid(gatep_ref[...])
        ext_u[0:CONV_HALO, :] = jnp.where(i > 0, u_prev, 0.0)
        ext_u[CONV_HALO:CONV_HALO + tc, :] = val * sg
        dc = dc_ref[...]
        ext_d[0:tc, :] = dc
        ext_d[tc:tc + CONV_HALO, :] = jnp.where(i < nt - 1, dcn_ref[...], 0.0)
        du = jnp.zeros((tc, D), F32)
        for k in range(CONV_K):
            acc[k] += _rowsum8(dc * ext_u[off + k:off + k + tc, :])
            du = du + dw_ref[k:k + 1, :] * ext_d[CONV_K - 1 - k:CONV_K - 1 - k + tc, :]
        dvg_ref[:, 0:D] = (du * sg).astype(BF16)
        dvg_ref[:, D:2 * D] = (du * val * sg * (1.0 - sg)).astype(BF16)

        @pl.when(i == nt - 1)
        def _():
            ddw_ref[...] = jnp.zeros_like(ddw_ref)
            for k in range(CONV_K):
                ddw_ref[k:k + 1, :] = jnp.sum(acc[k], axis=0, keepdims=True)

    return pl.pallas_call(
        body, grid=(nt,),
        in_specs=[_rows(tc, D), pl.BlockSpec((CONV_HALO, D), lambda i: (nxt(i), 0)),
                  _rows(tc, D, 0), _rows(tc, D, 1),
                  pl.BlockSpec((CONV_HALO, D), lambda i: (prev(i), 0)), pl.BlockSpec((CONV_HALO, D), lambda i: (prev(i), 1)),
                  _const((CONV_HALO, D))],
        out_specs=[_rows(tc, 2 * D), _const((CONV_HALO, D))],
        out_shape=[jax.ShapeDtypeStruct((T, 2 * D), BF16), jax.ShapeDtypeStruct((CONV_HALO, D), F32)],
        scratch_shapes=[pltpu.VMEM((CONV_HALO + tc, D), F32), pltpu.VMEM((CONV_HALO + tc, D), F32),
                        pltpu.VMEM((CONV_K, 8, D), F32)],
        compiler_params=_cparams(("arbitrary",)), name=name)(dc, dc, p, p, p, p, dw)


GATE_A0 = (2 * D_MODEL + D_SSM) // 512
GATE_B0 = GATE_A0 + 2


def merge_fwd(p, ya, z, *, name):
    T = p.shape[0]
    D = D_MODEL
    tc = _pick(T, [512, 256, 128])
    W = 512

    def body(ga_ref, gb_ref, ya_ref, z1_ref, z2_ref, o_ref):
        yb = z1_ref[...] * _sigmoid(z2_ref[...])
        o_ref[...] = (_sigmoid(ga_ref[...]) * ya_ref[...] + _sigmoid(gb_ref[...]) * yb).astype(BF16)

    return pl.pallas_call(
        body, grid=(T // tc, D // W),
        in_specs=[pl.BlockSpec((tc, W), lambda i, j: (i, GATE_A0 + j)), pl.BlockSpec((tc, W), lambda i, j: (i, GATE_B0 + j)),
                  pl.BlockSpec((tc, W), lambda i, j: (i, j)), pl.BlockSpec((tc, W), lambda i, j: (i, j)),
                  pl.BlockSpec((tc, W), lambda i, j: (i, D // W + j))],
        out_specs=pl.BlockSpec((tc, W), lambda i, j: (i, j)),
        out_shape=jax.ShapeDtypeStruct((T, D), BF16),
        compiler_params=_cparams(("arbitrary", "arbitrary")), name=name)(p, p, ya, z, z)


def merge_bwd(dm, p, ya, z, *, name):
    T = p.shape[0]
    D = D_MODEL
    tc = _pick(T, [512, 256, 128])
    W = 512
    nb = D // W

    def body(dm_ref, ga_ref, gb_ref, ya_ref, z1_ref, z2_ref, dya_ref, dga_ref, dgb_ref, dz1_ref, dz2_ref):
        dm = dm_ref[...]
        sa = _sigmoid(ga_ref[...])
        sb = _sigmoid(gb_ref[...])
        s2 = _sigmoid(z2_ref[...])
        z1 = z1_ref[...]
        yb = z1 * s2
        dya_ref[...] = (dm * sa).astype(BF16)
        dga_ref[...] = (dm * ya_ref[...] * sa * (1.0 - sa)).astype(BF16)
        dgb_ref[...] = (dm * yb * sb * (1.0 - sb)).astype(BF16)
        dyb = dm * sb
        dz1_ref[...] = (dyb * s2).astype(BF16)
        dz2_ref[...] = (dyb * z1 * s2 * (1.0 - s2)).astype(BF16)

    blk = lambda off: pl.BlockSpec((tc, W), lambda i, j: (i, off + j))
    dya, dga, dgb, dz1, dz2 = pl.pallas_call(
        body, grid=(T // tc, nb),
        in_specs=[blk(0), blk(GATE_A0), blk(GATE_B0), blk(0), blk(0), blk(nb)],
        out_specs=[blk(0)] * 5,
        out_shape=[jax.ShapeDtypeStruct((T, D), BF16)] * 5,
        compiler_params=_cparams(("arbitrary", "arbitrary")), name=name)(dm, p, p, ya, z, z)
    return dya, dga, dgb, dz1, dz2


def _scan_block(src_r, src_i, dst_r, dst_i, car_r, car_i, pw_r, pw_i, cw_r, cw_i, ntiles, reverse, extra=None):
    W = src_r.shape[1]
    rows = lax.broadcasted_iota(jnp.int32, (8, W), 0)
    steps = []
    for d, pr in ((1, 0), (2, 1), (4, 3)):
        steps.append((d, jnp.broadcast_to(pw_r[pr:pr + 1, :], (8, W)), jnp.broadcast_to(pw_i[pr:pr + 1, :], (8, W))))
    cw_r, cw_i = cw_r[...], cw_i[...]

    def tile(jj, carry):
        j = ntiles - 1 - jj if reverse else jj
        sl = pl.ds(pl.multiple_of(j * 8, 8), 8)
        xr, xi = src_r[sl, :], src_i[sl, :]
        for d, lr, li in steps:
            if reverse:
                sr = jnp.where(rows < 8 - d, pltpu.roll(xr, 8 - d, 0), 0.0)
                si = jnp.where(rows < 8 - d, pltpu.roll(xi, 8 - d, 0), 0.0)
            else:
                sr = jnp.where(rows >= d, pltpu.roll(xr, d, 0), 0.0)
                si = jnp.where(rows >= d, pltpu.roll(xi, d, 0), 0.0)
            xr, xi = xr + lr * sr - li * si, xi + lr * si + li * sr
        cr, ci = car_r[...], car_i[...]
        xr, xi = xr + cw_r * cr - cw_i * ci, xi + cw_r * ci + cw_i * cr
        dst_r[sl, :] = xr
        dst_i[sl, :] = xi
        edge = 0 if reverse else 7
        car_r[...] = jnp.broadcast_to(xr[edge:edge + 1, :], (8, W))
        car_i[...] = jnp.broadcast_to(xi[edge:edge + 1, :], (8, W))
        if extra is not None:
            carry = extra(j, xr, xi, carry)
        return carry

    return tile


def ssm_fwd(p, Br, Bi, Cr, Ci, pw_r, pw_i, dvec, *, name):
    T = p.shape[0]
    tt = _pick(T, [256, 128])
    nt = T // tt
    WI, WS = SSM_BLOCK_IN, SSM_BLOCK_STATE
    u0 = 2 * D_MODEL // WI

    def body(u_ref, br_ref, bi_ref, cr_ref, ci_ref, pwr_ref, pwi_ref, d_ref, xr_ref, xi_ref, y_ref, bur, bui, car_r, car_i):
        i = pl.program_id(1)

        @pl.when(i == 0)
        def _():
            car_r[...] = jnp.zeros_like(car_r)
            car_i[...] = jnp.zeros_like(car_i)

        u = u_ref[...]
        ub = u.astype(BF16)
        bur[...] = jnp.dot(ub, br_ref[...].astype(BF16), preferred_element_type=F32)
        bui[...] = jnp.dot(ub, bi_ref[...].astype(BF16), preferred_element_type=F32)
        tile = _scan_block(bur, bui, xr_ref, xi_ref, car_r, car_i, pwr_ref, pwi_ref, pwr_ref, pwi_ref, tt // 8, False)
        lax.fori_loop(0, tt // 8, tile, 0)
        y = (jnp.dot(xr_ref[...].astype(BF16), cr_ref[...].astype(BF16), preferred_element_type=F32)
             - jnp.dot(xi_ref[...].astype(BF16), ci_ref[...].astype(BF16), preferred_element_type=F32)
             + d_ref[...] * u)
        y_ref[...] = y.astype(BF16)

    return pl.pallas_call(
        body, grid=(SSM_BLOCKS, nt),
        in_specs=[pl.BlockSpec((tt, WI), lambda b, i: (i, u0 + b)),
                  pl.BlockSpec((None, WI, WS), lambda b, i: (b, 0, 0)), pl.BlockSpec((None, WI, WS), lambda b, i: (b, 0, 0)),
                  pl.BlockSpec((None, WS, WI), lambda b, i: (b, 0, 0)), pl.BlockSpec((None, WS, WI), lambda b, i: (b, 0, 0)),
                  pl.BlockSpec((8, WS), lambda b, i: (0, b)), pl.BlockSpec((8, WS), lambda b, i: (0, b)),
                  pl.BlockSpec((1, WI), lambda b, i: (0, b))],
        out_specs=[pl.BlockSpec((tt, WS), lambda b, i: (i, b)), pl.BlockSpec((tt, WS), lambda b, i: (i, b)),
                   pl.BlockSpec((tt, WI), lambda b, i: (i, b))],
        out_shape=[jax.ShapeDtypeStruct((T, SSM_BLOCKS * WS), F32)] * 2 + [jax.ShapeDtypeStruct((T, D_SSM), BF16)],
        scratch_shapes=[pltpu.VMEM((tt, WS), F32), pltpu.VMEM((tt, WS), F32), pltpu.VMEM((8, WS), F32), pltpu.VMEM((8, WS), F32)],
        compiler_params=_cparams(("arbitrary", "arbitrary")), name=name)(p, Br, Bi, Cr, Ci, pw_r, pw_i, dvec)


def ssm_bwd(dy, p, xr, xi, Br, Bi, Cr, Ci, pwc_r, pwc_i, dvec, *, name):
    T = p.shape[0]
    tt = _pick(T, [256, 128])
    nt = T // tt
    WI, WS = SSM_BLOCK_IN, SSM_BLOCK_STATE
    u0 = 2 * D_MODEL // WI
    tb = lambda i: nt - 1 - i
    xprev = lambda i: jnp.maximum(tb(i) * (tt // 8) - 1, 0)
    tn_dims = _DIMS["tn"]
    nt_dims = _DIMS["nt"]

    def body(dy_ref, u_ref, xr_ref, xi_ref, xpr_ref, xpi_ref, br_ref, bi_ref, cr_ref, ci_ref, pwr_ref, pwi_ref,
             cwr_ref, cwi_ref, d_ref,
             du_ref, dbr_ref, dbi_ref, dcr_ref, dci_ref, dar_ref, dai_ref, dd_ref,
             gr, gi, ext_r, ext_i, car_r, car_i):
        i = pl.program_id(1)

        @pl.when(i == 0)
        def _():
            car_r[...] = jnp.zeros_like(car_r)
            car_i[...] = jnp.zeros_like(car_i)
            dbr_ref[...] = jnp.zeros_like(dbr_ref)
            dbi_ref[...] = jnp.zeros_like(dbi_ref)
            dcr_ref[...] = jnp.zeros_like(dcr_ref)
            dci_ref[...] = jnp.zeros_like(dci_ref)
            dar_ref[...] = jnp.zeros_like(dar_ref)
            dai_ref[...] = jnp.zeros_like(dai_ref)
            dd_ref[...] = jnp.zeros_like(dd_ref)

        dy = dy_ref[...]
        dyb = dy.astype(BF16)
        u = u_ref[...]
        ub = u.astype(BF16)
        gr[...] = lax.dot_general(dyb, cr_ref[...].astype(BF16), nt_dims, preferred_element_type=F32)
        gi[...] = -lax.dot_general(dyb, ci_ref[...].astype(BF16), nt_dims, preferred_element_type=F32)
        first = tb(i) == 0
        ext_r[0:8, :] = jnp.where(first, 0.0, xpr_ref[...])
        ext_i[0:8, :] = jnp.where(first, 0.0, xpi_ref[...])
        ext_r[8:8 + tt, :] = xr_ref[...]
        ext_i[8:8 + tt, :] = xi_ref[...]
        rows = lax.broadcasted_iota(jnp.int32, (8, WS), 0)

        def lam_grad(j, g_r, g_i, carry):
            a_r, a_i = carry
            cur = pl.ds(pl.multiple_of(j * 8 + 8, 8), 8)
            prv = pl.ds(pl.multiple_of(j * 8, 8), 8)
            xc_r, xc_i = ext_r[cur, :], ext_i[cur, :]
            xl_r, xl_i = ext_r[prv, :], ext_i[prv, :]
            xp_r = jnp.where(rows == 0, jnp.broadcast_to(xl_r[7:8, :], (8, WS)), pltpu.roll(xc_r, 1, 0))
            xp_i = jnp.where(rows == 0, jnp.broadcast_to(xl_i[7:8, :], (8, WS)), pltpu.roll(xc_i, 1, 0))
            return (a_r + g_r * xp_r + g_i * xp_i, a_i + g_i * xp_r - g_r * xp_i)

        tile = _scan_block(gr, gi, gr, gi, car_r, car_i, pwr_ref, pwi_ref, cwr_ref, cwi_ref, tt // 8, True, extra=lam_grad)
        z8 = jnp.zeros((8, WS), F32)
        a_r, a_i = lax.fori_loop(0, tt // 8, tile, (z8, z8))
        dar_ref[...] += a_r
        dai_ref[...] += a_i
        grb = gr[...].astype(BF16)
        gib = gi[...].astype(BF16)
        dbr_ref[...] += lax.dot_general(ub, grb, tn_dims, preferred_element_type=F32)
        dbi_ref[...] += lax.dot_general(ub, gib, tn_dims, preferred_element_type=F32)
        dcr_ref[...] += lax.dot_general(xr_ref[...].astype(BF16), dyb, tn_dims, preferred_element_type=F32)
        dci_ref[...] -= lax.dot_general(xi_ref[...].astype(BF16), dyb, tn_dims, preferred_element_type=F32)
        du = (lax.dot_general(grb, br_ref[...].astype(BF16), nt_dims, preferred_element_type=F32)
              + lax.dot_general(gib, bi_ref[...].astype(BF16), nt_dims, preferred_element_type=F32)
              + d_ref[...] * dy)
        du_ref[...] = du.astype(BF16)
        dd_ref[...] += _rowsum8(dy * u)

    wspec = lambda shp: pl.BlockSpec((None,) + shp, lambda b, i: (b, 0, 0))
    return pl.pallas_call(
        body, grid=(SSM_BLOCKS, nt),
        in_specs=[pl.BlockSpec((tt, WI), lambda b, i: (tb(i), b)),
                  pl.BlockSpec((tt, WI), lambda b, i: (tb(i), u0 + b)),
                  pl.BlockSpec((tt, WS), lambda b, i: (tb(i), b)), pl.BlockSpec((tt, WS), lambda b, i: (tb(i), b)),
                  pl.BlockSpec((8, WS), lambda b, i: (xprev(i), b)), pl.BlockSpec((8, WS), lambda b, i: (xprev(i), b)),
                  wspec((WI, WS)), wspec((WI, WS)), wspec((WS, WI)), wspec((WS, WI)),
                  pl.BlockSpec((8, WS), lambda b, i: (0, b)), pl.BlockSpec((8, WS), lambda b, i: (0, b)),
                  pl.BlockSpec((8, WS), lambda b, i: (0, b)), pl.BlockSpec((8, WS), lambda b, i: (0, b)),
                  pl.BlockSpec((1, WI), lambda b, i: (0, b))],
        out_specs=[pl.BlockSpec((tt, WI), lambda b, i: (tb(i), b)),
                   wspec((WI, WS)), wspec((WI, WS)), wspec((WS, WI)), wspec((WS, WI)),
                   pl.BlockSpec((8, WS), lambda b, i: (0, b)), pl.BlockSpec((8, WS), lambda b, i: (0, b)),
                   pl.BlockSpec((8, WI), lambda b, i: (0, b))],
        out_shape=[jax.ShapeDtypeStruct((T, D_SSM), BF16),
                   jax.ShapeDtypeStruct((SSM_BLOCKS, WI, WS), F32), jax.ShapeDtypeStruct((SSM_BLOCKS, WI, WS), F32),
                   jax.ShapeDtypeStruct((SSM_BLOCKS, WS, WI), F32), jax.ShapeDtypeStruct((SSM_BLOCKS, WS, WI), F32),
                   jax.ShapeDtypeStruct((8, SSM_BLOCKS * WS), F32), jax.ShapeDtypeStruct((8, SSM_BLOCKS * WS), F32),
                   jax.ShapeDtypeStruct((8, D_SSM), F32)],
        scratch_shapes=[pltpu.VMEM((tt, WS), F32), pltpu.VMEM((tt, WS), F32),
                        pltpu.VMEM((tt + 8, WS), F32), pltpu.VMEM((tt + 8, WS), F32),
                        pltpu.VMEM((8, WS), F32), pltpu.VMEM((8, WS), F32)],
        compiler_params=_cparams(("arbitrary", "arbitrary")), name=name,
    )(dy, p, xr, xi, xr, xi, Br, Bi, Cr, Ci, pwc_r, pwc_i, pwc_r[::-1], pwc_i[::-1], dvec)


def _ssm_discretise(log_step, lam_re, lam_im, b_re, b_im):
    step = jnp.exp(log_step)[:, None]
    mag = jnp.exp(lam_re * step)
    ar = mag * jnp.cos(lam_im * step)
    ai = mag * jnp.sin(lam_im * step)
    den = lam_re * lam_re + lam_im * lam_im
    nr = ar - 1.0
    cr = (nr * lam_re + ai * lam_im) / den
    ci = (ai * lam_re - nr * lam_im) / den
    bbr = cr[..., None] * b_re - ci[..., None] * b_im
    bbi = cr[..., None] * b_im + ci[..., None] * b_re
    return ar, ai, bbr, bbi


def _blockdiag_in(bb):
    t = jnp.transpose(bb, (0, 2, 1)).reshape(SSM_BLOCKS, 8, SSM_GROUP, SSM_STATE)
    eye = jnp.eye(8, dtype=bb.dtype)
    return (t[:, :, :, None, :] * eye[None, :, None, :, None]).reshape(SSM_BLOCKS, SSM_BLOCK_IN, SSM_BLOCK_STATE)


def _blockdiag_out(cc):
    t = jnp.transpose(cc, (0, 2, 1)).reshape(SSM_BLOCKS, 8, SSM_STATE, SSM_GROUP)
    eye = jnp.eye(8, dtype=cc.dtype)
    return (t[:, :, :, None, :] * eye[None, :, None, :, None]).reshape(SSM_BLOCKS, SSM_BLOCK_STATE, SSM_BLOCK_IN)


def _diag_in(d):
    t = d.reshape(SSM_BLOCKS, 8, SSM_GROUP, 8, SSM_STATE)
    t = jnp.einsum("bghgp->bghp", t).reshape(SSM_GROUPS, SSM_GROUP, SSM_STATE)
    return jnp.transpose(t, (0, 2, 1))


def _diag_out(d):
    t = d.reshape(SSM_BLOCKS, 8, SSM_STATE, 8, SSM_GROUP)
    t = jnp.einsum("bgpgh->bgph", t).reshape(SSM_GROUPS, SSM_STATE, SSM_GROUP)
    return jnp.transpose(t, (0, 2, 1))


def _powers(ar, ai):
    rs, is_ = [ar], [ai]
    for _ in range(7):
        r, i = rs[-1], is_[-1]
        rs.append(r * ar - i * ai)
        is_.append(r * ai + i * ar)
    return jnp.stack(rs), jnp.stack(is_)


def attn_fwd(q, kv, *, name):
    T, D = q.shape
    nm = kv.shape[0]
    tq = _pick(T, [512, 256, 128])
    scale = HEAD_DIM ** -0.5

    def body(q_ref, k_ref, v_ref, o_ref):
        for h in range(N_HEADS):
            sl = slice(h * HEAD_DIM, (h + 1) * HEAD_DIM)
            s = lax.dot_general(q_ref[:, sl], k_ref[:, sl].astype(BF16), _DIMS["nt"], preferred_element_type=F32) * scale
            e = jnp.exp(s - jnp.max(s, axis=-1, keepdims=True))
            pr = e / jnp.sum(e, axis=-1, keepdims=True)
            o_ref[:, sl] = jnp.dot(pr.astype(BF16), v_ref[:, sl].astype(BF16), preferred_element_type=F32).astype(BF16)

    return pl.pallas_call(
        body, grid=(T // tq,),
        in_specs=[_rows(tq, D), pl.BlockSpec((nm, D), lambda i: (0, 0)), pl.BlockSpec((nm, D), lambda i: (0, 1))],
        out_specs=_rows(tq, D), out_shape=jax.ShapeDtypeStruct((T, D), BF16),
        compiler_params=_cparams(("arbitrary",)), name=name)(q, kv, kv)


def attn_bwd(q, kv, do, *, name):
    T, D = q.shape
    nm = kv.shape[0]
    tq = _pick(T, [512, 256, 128])
    nt = T // tq
    scale = HEAD_DIM ** -0.5

    def body(q_ref, k_ref, v_ref, do_ref, dq_ref, dkv_ref):
        i = pl.program_id(0)

        @pl.when(i == 0)
        def _():
            dkv_ref[...] = jnp.zeros_like(dkv_ref)

        for h in range(N_HEADS):
            sl = slice(h * HEAD_DIM, (h + 1) * HEAD_DIM)
            slv = slice(D + h * HEAD_DIM, D + (h + 1) * HEAD_DIM)
            qh = q_ref[:, sl]
            kh = k_ref[:, sl].astype(BF16)
            vh = v_ref[:, sl].astype(BF16)
            doh = do_ref[:, sl].astype(BF16)
            s = lax.dot_general(qh, kh, _DIMS["nt"], preferred_element_type=F32) * scale
            e = jnp.exp(s - jnp.max(s, axis=-1, keepdims=True))
            pr = e / jnp.sum(e, axis=-1, keepdims=True)
            dp = lax.dot_general(doh, vh, _DIMS["nt"], preferred_element_type=F32)
            ds = (pr * (dp - jnp.sum(pr * dp, axis=-1, keepdims=True)) * scale).astype(BF16)
            dq_ref[:, sl] = jnp.dot(ds, kh, preferred_element_type=F32).astype(BF16)
            dkv_ref[:, sl] += lax.dot_general(ds, qh, _DIMS["tn"], preferred_element_type=F32)
            dkv_ref[:, slv] += lax.dot_general(pr.astype(BF16), doh, _DIMS["tn"], preferred_element_type=F32)

    return pl.pallas_call(
        body, grid=(nt,),
        in_specs=[_rows(tq, D), pl.BlockSpec((nm, D), lambda i: (0, 0)), pl.BlockSpec((nm, D), lambda i: (0, 1)), _rows(tq, D)],
        out_specs=[_rows(tq, D), _const((nm, 2 * D))],
        out_shape=[jax.ShapeDtypeStruct((T, D), BF16), jax.ShapeDtypeStruct((nm, 2 * D), F32)],
        compiler_params=_cparams(("arbitrary",)), name=name)(q, kv, kv, do)


def _adam_math(w, g, m, v):
    m = ADAM_B1 * m + (1.0 - ADAM_B1) * g
    v = ADAM_B2 * v + (1.0 - ADAM_B2) * (g * g)
    m_hat = m / (1.0 - ADAM_B1 ** ADAM_STEP)
    v_hat = v / (1.0 - ADAM_B2 ** ADAM_STEP)
    delta = -ADAM_LR * (m_hat / (jnp.sqrt(v_hat) + ADAM_EPS) + ADAM_WD * w)
    return delta, m, v


def adamw(w, m, v, g_arr, g_row0, *, name):
    R, C = w.shape
    tr = _pick(R, [256, 128, 64, 32, 16, 8])
    assert g_row0 % tr == 0
    g0 = g_row0 // tr

    def body(w_ref, m_ref, v_ref, g_ref, go_ref, d_ref, mo_ref, vo_ref):
        g = g_ref[...]
        d, mn, vn = _adam_math(w_ref[...], g, m_ref[...], v_ref[...])
        go_ref[...] = g
        d_ref[...] = d
        mo_ref[...] = mn
        vo_ref[...] = vn

    sp = pl.BlockSpec((tr, C), lambda i: (i, 0))
    return pl.pallas_call(
        body, grid=(R // tr,), in_specs=[sp, sp, sp, pl.BlockSpec((tr, C), lambda i: (g0 + i, 0))],
        out_specs=[sp] * 4, out_shape=[jax.ShapeDtypeStruct((R, C), F32)] * 4,
        compiler_params=_cparams(("arbitrary",)), name=name)(w, m, v, g_arr)


def _place():
    x, y, c = lax.axis_index("x"), lax.axis_index("y"), lax.axis_index("c")
    chips = [(1 - x, y), (x, 1 - y), (1 - x, 1 - y)]
    return x, y, c, chips


ANY = pl.BlockSpec(memory_space=pl.ANY)


def allgather_weights(bufs, *, name):
    n = len(bufs)

    def body(*refs):
        o_refs = refs[n:2 * n]
        send_sems, recv_sems, fsend_sems, frecv_sems = refs[2 * n:]
        x, y, c, chips = _place()
        k_me = 2 * x + y
        sib = (x, y, 1 - c)
        halves = [b.shape[1] // 2 for b in bufs]

        def half(a, cc):
            return pl.ds(pl.multiple_of(cc * halves[a], 16), halves[a])

        sends = []
        for a in range(n):
            for r, (px, py) in enumerate(chips):
                cp = pltpu.make_async_remote_copy(
                    src_ref=o_refs[a].at[k_me, half(a, c)], dst_ref=o_refs[a].at[k_me, half(a, c)],
                    send_sem=send_sems.at[3 * a + r], recv_sem=recv_sems.at[3 * a + r],
                    device_id=(px, py, c), device_id_type=MESH)
                cp.start()
                sends.append(cp)
        passed = []
        for a in range(n):
            for r, (px, py) in enumerate(chips):
                win = o_refs[a].at[2 * px + py, half(a, c)]
                pltpu.make_async_remote_copy(
                    src_ref=win, dst_ref=win, send_sem=send_sems.at[3 * a + r], recv_sem=recv_sems.at[3 * a + r],
                    device_id=(px, py, c), device_id_type=MESH).wait_recv()
                cp = pltpu.make_async_remote_copy(
                    src_ref=win, dst_ref=win, send_sem=fsend_sems.at[3 * a + r], recv_sem=frecv_sems.at[3 * a + r],
                    device_id=sib, device_id_type=MESH)
                cp.start()
                passed.append(cp)
        for a in range(n):
            for r, (px, py) in enumerate(chips):
                win = o_refs[a].at[2 * px + py, half(a, 1 - c)]
                pltpu.make_async_remote_copy(
                    src_ref=win, dst_ref=win, send_sem=fsend_sems.at[3 * a + r], recv_sem=frecv_sems.at[3 * a + r],
                    device_id=sib, device_id_type=MESH).wait_recv()
        for cp in sends + passed:
            cp.wait_send()

    return pl.pallas_call(
        body, in_specs=[ANY] * n, out_specs=[ANY] * n,
        out_shape=[jax.ShapeDtypeStruct(b.shape, b.dtype) for b in bufs],
        scratch_shapes=[pltpu.SemaphoreType.DMA((3 * n,))] * 4,
        input_output_aliases={a: a for a in range(n)},
        name=name)(*bufs)


def exchange_halves(grads, *, name):
    n = len(grads)

    def body(*refs):
        g_refs, l_refs = refs[:n], refs[n:2 * n]
        send_sems, recv_sems = refs[2 * n:]
        x, y, c, _ = _place()
        cps = []
        for a in range(n):
            h = grads[a].shape[1] // 2
            cp = pltpu.make_async_remote_copy(
                src_ref=g_refs[a].at[:, pl.ds(pl.multiple_of((1 - c) * h, 8), h)], dst_ref=l_refs[a],
                send_sem=send_sems.at[a], recv_sem=recv_sems.at[a], device_id=(x, y, 1 - c), device_id_type=MESH)
            cp.start()
            cps.append(cp)
        for cp in cps:
            cp.wait()

    return pl.pallas_call(
        body, in_specs=[ANY] * n, out_specs=[ANY] * n,
        out_shape=[jax.ShapeDtypeStruct((g.shape[0], g.shape[1] // 2, g.shape[2]), g.dtype) for g in grads],
        scratch_shapes=[pltpu.SemaphoreType.DMA((n,))] * 2,
        name=name)(*grads)


def scatter_to_owners(parts, *, name):
    n = len(parts)

    def body(*refs):
        p_refs, l_refs = refs[:n], refs[n:2 * n]
        send_sems, recv_sems = refs[2 * n:]
        x, y, c, chips = _place()
        cps = []
        for a in range(n):
            for r, (px, py) in enumerate(chips):
                cp = pltpu.make_async_remote_copy(
                    src_ref=p_refs[a].at[2 * px + py], dst_ref=l_refs[a].at[r],
                    send_sem=send_sems.at[3 * a + r], recv_sem=recv_sems.at[3 * a + r],
                    device_id=(px, py, c), device_id_type=MESH)
                cp.start()
                cps.append(cp)
        for cp in cps:
            cp.wait()

    return pl.pallas_call(
        body, in_specs=[ANY] * n, out_specs=[ANY] * n,
        out_shape=[jax.ShapeDtypeStruct((3,) + p.shape[1:], p.dtype) for p in parts],
        scratch_shapes=[pltpu.SemaphoreType.DMA((3 * n,))] * 2,
        name=name)(*parts)


def join_halves(fulls, *, name):
    n = len(fulls)

    def body(*refs):
        o_refs = refs[n:2 * n]
        send_sems, recv_sems = refs[2 * n:]
        x, y, c, _ = _place()
        cps = []
        for a in range(n):
            h = fulls[a].shape[0] // 2
            win = o_refs[a].at[pl.ds(pl.multiple_of(c * h, 8), h)]
            cp = pltpu.make_async_remote_copy(
                src_ref=win, dst_ref=win, send_sem=send_sems.at[a], recv_sem=recv_sems.at[a],
                device_id=(x, y, 1 - c), device_id_type=MESH)
            cp.start()
            cps.append(cp)
        for a in range(n):
            h = fulls[a].shape[0] // 2
            other = o_refs[a].at[pl.ds(pl.multiple_of((1 - c) * h, 8), h)]
            pltpu.make_async_remote_copy(
                src_ref=other, dst_ref=other, send_sem=send_sems.at[a], recv_sem=recv_sems.at[a],
                device_id=(x, y, 1 - c), device_id_type=MESH).wait_recv()
        for cp in cps:
            cp.wait_send()

    return pl.pallas_call(
        body, in_specs=[ANY] * n, out_specs=[ANY] * n,
        out_shape=[jax.ShapeDtypeStruct(f.shape, f.dtype) for f in fulls],
        scratch_shapes=[pltpu.SemaphoreType.DMA((n,))] * 2,
        input_output_aliases={a: a for a in range(n)},
        name=name)(*fulls)


def add_sibling(g, l, c, *, name):
    nb, R, C = g.shape
    h = R // 2
    tr = _pick(h, [256, 128, 64, 32, 16])
    per = h // tr

    def body(c_ref, g_ref, l_ref, o_ref):
        o_ref[...] = (g_ref[...] + l_ref[...]).astype(BF16)

    return pl.pallas_call(
        body,
        grid_spec=pltpu.PrefetchScalarGridSpec(
            num_scalar_prefetch=1, grid=(nb, per),
            in_specs=[pl.BlockSpec((None, tr, C), lambda k, i, c_ref: (k, c_ref[0] * per + i, 0)),
                      pl.BlockSpec((None, tr, C), lambda k, i, c_ref: (k, i, 0))],
            out_specs=pl.BlockSpec((None, tr, C), lambda k, i, c_ref: (k, i, 0))),
        out_shape=jax.ShapeDtypeStruct((nb, h, C), BF16),
        compiler_params=_cparams(("arbitrary", "arbitrary")), name=name)(c, g, l)


def add_chips(part, land, kc, *, name):
    _, H, C = part.shape
    tr = _pick(H, [256, 128, 64, 32, 16])
    per = H // tr

    def body(kc_ref, p_ref, l_ref, o_ref):
        o_ref[...] = ((p_ref[...].astype(F32) + l_ref[0].astype(F32)) + l_ref[1].astype(F32)) + l_ref[2].astype(F32)

    return pl.pallas_call(
        body,
        grid_spec=pltpu.PrefetchScalarGridSpec(
            num_scalar_prefetch=1, grid=(per,),
            in_specs=[pl.BlockSpec((None, tr, C), lambda i, kc_ref: (kc_ref[0], i, 0)),
                      pl.BlockSpec((3, tr, C), lambda i, kc_ref: (0, i, 0))],
            out_specs=pl.BlockSpec((tr, C), lambda i, kc_ref: (kc_ref[1] * per + i, 0))),
        out_shape=jax.ShapeDtypeStruct((2 * H, C), F32),
        compiler_params=_cparams(("arbitrary",)), name=name)(kc, part, land)


def allgather_sum(v, *, name):
    m_per, n = v.shape

    def body(x_ref, out_ref, sum_ref, send_sems, recv_sems, local_sem):
        x, y, c, chips = _place()
        me, sibling = (x, y, c), (x, y, 1 - c)

        def rows(px, py, pc):
            return out_ref.at[pl.ds(pl.multiple_of((4 * px + 2 * py + pc) * m_per, 8), m_per), :]

        def copy(k, block, to, src=None):
            return pltpu.make_async_remote_copy(
                src_ref=rows(*block) if src is None else src, dst_ref=rows(*block),
                send_sem=send_sems.at[k], recv_sem=recv_sems.at[k], device_id=to, device_id_type=MESH)

        mine = pltpu.make_async_copy(x_ref, rows(*me), local_sem)
        mine.start()
        first = [copy(0, me, sibling, src=x_ref)]
        first += [copy(1 + j, me, (*chip, c), src=x_ref) for j, chip in enumerate(chips)]
        for cp in first:
            cp.start()
        passed = [copy(4 + j, (*chip, c), sibling) for j, chip in enumerate(chips)]
        for j, chip in enumerate(chips):
            copy(1 + j, (*chip, c), me).wait_recv()
            passed[j].start()
        copy(0, sibling, me).wait_recv()
        for j, chip in enumerate(chips):
            copy(4 + j, (*chip, 1 - c), me).wait_recv()
        for cp in first + passed:
            cp.wait_send()
        mine.wait()
        acc = out_ref[0:m_per, :]
        for d in range(1, N_DEV):
            acc = acc + out_ref[d * m_per:(d + 1) * m_per, :]
        sum_ref[...] = acc

    vm = pl.BlockSpec(memory_space=pltpu.VMEM)
    return pl.pallas_call(
        body, in_specs=[vm], out_specs=[vm, vm],
        out_shape=[jax.ShapeDtypeStruct((N_DEV * m_per, n), v.dtype), jax.ShapeDtypeStruct((m_per, n), v.dtype)],
        scratch_shapes=[pltpu.SemaphoreType.DMA((7,)), pltpu.SemaphoreType.DMA((7,)), pltpu.SemaphoreType.DMA],
        compiler_params=pltpu.CompilerParams(vmem_limit_bytes=VMEM_LIMIT_BYTES), name=name)(v)


def _pack(arrs):
    cols = []
    for a in arrs:
        f = a.reshape(-1)
        pad = (-f.shape[0]) % 128
        cols.append(jnp.pad(f, (0, pad)).reshape(-1, 128))
    out = jnp.concatenate(cols, axis=0)
    pad = (-out.shape[0]) % 8
    return jnp.pad(out, ((0, pad), (0, 0)))


def _unpack(buf, shapes):
    outs, r = [], 0
    for s in shapes:
        nel = math.prod(s)
        nr = -(-nel // 128)
        outs.append(buf[r:r + nr].reshape(-1)[:nel].reshape(s))
        r += nr
    return outs


GA_CONV_OUT, GA_MIX_OUT, GA_WQ, GA_WO, GA_DOWN, GA_UP, GA_ROWS = 0, 256, 512, 768, 1024, 2048, 3072


def kernel(x, mem, in_norm_g, in_norm_b, w_in, conv_dw, conv_db, conv_norm_g, conv_norm_b, w_conv_out, ssm_log_step, ssm_lambda_re, ssm_lambda_im, ssm_b_re, ssm_b_im, ssm_c_re, ssm_c_im, ssm_d, w_ssm_glu, w_mix_out, ln1_g, ln1_b, xa_wq, xa_wkv, xa_wo, ln2_g, ln2_b, mlp_w_up, mlp_w_down, ln3_g, ln3_b, loss_target, m_in_norm_g, m_in_norm_b, m_w_in, m_conv_dw, m_conv_db, m_conv_norm_g, m_conv_norm_b, m_w_conv_out, m_ssm_log_step, m_ssm_lambda_re, m_ssm_lambda_im, m_ssm_b_re, m_ssm_b_im, m_ssm_c_re, m_ssm_c_im, m_ssm_d, m_w_ssm_glu, m_w_mix_out, m_ln1_g, m_ln1_b, m_xa_wq, m_xa_wkv, m_xa_wo, m_ln2_g, m_ln2_b, m_mlp_w_up, m_mlp_w_down, m_ln3_g, m_ln3_b, v_in_norm_g, v_in_norm_b, v_w_in, v_conv_dw, v_conv_db, v_conv_norm_g, v_conv_norm_b, v_w_conv_out, v_ssm_log_step, v_ssm_lambda_re, v_ssm_lambda_im, v_ssm_b_re, v_ssm_b_im, v_ssm_c_re, v_ssm_c_im, v_ssm_d, v_w_ssm_glu, v_w_mix_out, v_ln1_g, v_ln1_b, v_xa_wq, v_xa_wkv, v_xa_wo, v_ln2_g, v_ln2_b, v_mlp_w_up, v_mlp_w_down, v_ln3_g, v_ln3_b):
    D = D_MODEL
    xs = x[0]
    T = xs.shape[0]
    mems = mem[0]
    NM = mems.shape[0]
    tgt = loss_target[0]
    my_c = lax.axis_index("c")
    k_me = 2 * lax.axis_index("x") + lax.axis_index("y")
    c_arr = jnp.reshape(my_c, (1,)).astype(jnp.int32)
    k_arr = jnp.reshape(k_me, (1,)).astype(jnp.int32)

    sh_a = jnp.concatenate([w_conv_out[0], w_mix_out[0], xa_wq[0], xa_wo[0], mlp_w_down[0], mlp_w_up[0]], axis=0).astype(BF16)
    def own_block(shard):
        buf = jnp.zeros((N_CHIPS,) + shard.shape, shard.dtype)
        return lax.dynamic_update_slice(buf, shard[None], (k_me, 0, 0))

    GA, GIN, GKV, GGLU = allgather_weights(
        [own_block(s) for s in (sh_a, w_in[0].astype(BF16), xa_wkv[0].astype(BF16), w_ssm_glu[0].astype(BF16))],
        name="allgather_weights")
    dw_pad = jnp.pad(conv_dw[0], ((0, CONV_HALO - CONV_K), (0, 0)))
    dw_all, _ = allgather_sum(dw_pad, name="allgather_conv_dw")
    dw_full = jnp.transpose(dw_all.reshape(N_DEV, CONV_HALO, D // N_CHIPS)[::2], (1, 0, 2)).reshape(CONV_HALO, D)

    def w_rowshard(row0):
        return dict(b_spec=((N_CHIPS, 256, D), lambda i, j, k: (0, row0 // 256, 0)), b_view=(D, D), tn=D, tk=D)

    _, h0, h0b = ln_fwd(xs, in_norm_g, in_norm_b, name="ln0_fwd")
    p = mm_nn(h0b, GIN, ((None, D, 1152), lambda i, j, k: (j, 0, 0)), D_IN, tn=1152, tk=D, name="mm_w_in")[0]
    c_pre, actb = conv_fwd(p, dw_full, conv_db, conv_norm_g[0].reshape(1, D), conv_norm_b[0].reshape(1, D), name="conv_fwd")
    ya = mm_nn(actb, GA, N=D, name="mm_conv_out", **w_rowshard(GA_CONV_OUT))[0]

    lstep, lre, lim = ssm_log_step[0], ssm_lambda_re[0], ssm_lambda_im[0]
    bre, bim, cre, cim = ssm_b_re[0], ssm_b_im[0], ssm_c_re[0], ssm_c_im[0]
    (ar, ai, bbr, bbi), disc_vjp = jax.vjp(_ssm_discretise, lstep, lre, lim, bre, bim)
    Br, Bi = _blockdiag_in(bbr), _blockdiag_in(bbi)
    Cr, Ci = _blockdiag_out(cre), _blockdiag_out(cim)
    pw_r, pw_i = _powers(ar.reshape(-1), ai.reshape(-1))
    dvec = ssm_d[0].reshape(1, D_SSM)
    xr, xi, yssm = ssm_fwd(p, Br, Bi, Cr, Ci, pw_r, pw_i, dvec, name="ssm_fwd")
    z = mm_nn(yssm, GGLU, ((None, D_SSM, 512), lambda i, j, k: (j, 0, 0)), 2 * D, tn=512, tk=D_SSM, name="mm_ssm_glu")[0]
    mergedb = merge_fwd(p, ya, z, name="merge_fwd")
    mix = mm_nn(mergedb, GA, N=D, name="mm_mix_out", **w_rowshard(GA_MIX_OUT))[0]
    r1, h1, h1b = ln_fwd(mix, ln1_g[0], ln1_b[0], res=h0, name="ln1_fwd")

    qb = mm_nn(h1b, GA, N=D, out_dtype=BF16, name="mm_wq", **w_rowshard(GA_WQ))[0]
    kv = mm_nn(mems, GKV, ((None, D, 512), lambda i, j, k: (j, 0, 0)), 2 * D, tn=512, tk=D, name="mm_wkv")[0]
    ob = attn_fwd(qb, kv, name="attn_fwd")
    xa = mm_nn(ob, GA, N=D, name="mm_wo", **w_rowshard(GA_WO))[0]
    r2, h2, h2b = ln_fwd(xa, ln2_g[0], ln2_b[0], res=h1, name="ln2_fwd")

    def relu2(acc):
        zr = jnp.maximum(acc, 0.0)
        return acc, zr * zr

    zpre, zzb = mm_nn(h2b, GA, ((None, D, D), lambda i, j, k: (j, GA_UP // D, 0)), D_FF, tn=D, tk=D,
                      out_dtypes=[F32, BF16], epilogue=relu2, name="mm_up")
    ff = mm_nn(zzb, GA, ((None, D, D), lambda i, j, k: (k, GA_DOWN // D, 0)), D, tn=D, tk=D, name="mm_down")[0]
    r3, h3, _ = ln_fwd(ff, ln3_g[0], ln3_b[0], res=h2, name="ln3_fwd")
    dh3, sq = loss_head(h3, tgt, name="loss_head")
    loss = lax.psum(0.5 * sq[0, 0] / D, ("x", "y", "c"))

    ga_shape = jax.ShapeDtypeStruct((N_CHIPS, GA_ROWS, D), F32)
    dr3, dr3b, dg3, db3 = ln_bwd(r3, dh3, ln3_g[0], name="ln3_bwd")
    dzpreb = mm_nt(dr3b, GA, ((None, D, D), lambda i, j, k: (j, GA_DOWN // D, 0)), D_FF, tn=D, tk=D, out_dtype=BF16,
                   extras=(zpre,), epilogue=lambda acc, zp: (acc * (2.0 * jnp.maximum(zp, 0.0)),), name="mm_down_t")[0]
    GAg = mm_tn(zzb, dr3b, tm=D, tn=D, out_spec=((None, D, D), lambda i, j, k: (i, GA_DOWN // D, 0)), out_shape=ga_shape,
                name="mm_down_g")
    GAg = mm_tn(h2b, dzpreb, tm=D, tn=D, out_spec=((None, D, D), lambda i, j, k: (j, GA_UP // D, 0)), out_shape=ga_shape,
                out_buf=GAg, name="mm_up_g")
    dh2 = mm_nt(dzpreb, GA, ((None, D, D), lambda i, j, k: (k, GA_UP // D, 0)), D, tn=D, tk=D,
                extras=(dr3,), epilogue=lambda acc, d: (acc + ALPHA * d,), name="mm_up_t")[0]
    dr2, dr2b, dg2, db2 = ln_bwd(r2, dh2, ln2_g[0], name="ln2_bwd")

    def g_rowshard(row0):
        return dict(tm=D, tn=D, out_spec=((N_CHIPS, 256, D), lambda i, j, k: (0, row0 // 256, 0)), out_shape=ga_shape)

    dob = mm_nt(dr2b, GA, N=D, out_dtype=BF16, name="mm_wo_t", **w_rowshard(GA_WO))[0]
    GAg = mm_tn(ob, dr2b, out_buf=GAg, name="mm_wo_g", **g_rowshard(GA_WO))
    dqb, dkv = attn_bwd(qb, kv, dob, name="attn_bwd")
    GAg = mm_tn(h1b, dqb, out_buf=GAg, name="mm_wq_g", **g_rowshard(GA_WQ))
    GKVg = mm_tn(mems, dkv, tm=D, tn=512, tk=NM, out_spec=((None, D, 512), lambda i, j, k: (j, 0, 0)),
                 out_shape=jax.ShapeDtypeStruct((N_CHIPS, D, 512), F32), name="mm_wkv_g")
    dh1 = mm_nt(dqb, GA, N=D, extras=(dr2,), epilogue=lambda acc, d: (acc + ALPHA * d,), name="mm_wq_t",
                **w_rowshard(GA_WQ))[0]
    dr1, dr1b, dg1, db1 = ln_bwd(r1, dh1, ln1_g[0], name="ln1_bwd")

    dmerged = mm_nt(dr1b, GA, N=D, name="mm_mix_t", **w_rowshard(GA_MIX_OUT))[0]
    GAg = mm_tn(mergedb, dr1b, out_buf=GAg, name="mm_mix_g", **g_rowshard(GA_MIX_OUT))
    dyab, dgab, dgbb, dz1b, dz2b = merge_bwd(dmerged, p, ya, z, name="merge_bwd")
    dzb = jnp.concatenate([dz1b, dz2b], axis=1)
    GGLUg = mm_tn(yssm, dzb, tm=D_SSM, tn=512, out_spec=((None, D_SSM, 512), lambda i, j, k: (j, 0, 0)),
                  out_shape=jax.ShapeDtypeStruct((N_CHIPS, D_SSM, 512), F32), name="mm_glu_g")
    dyssm = mm_nt(dzb, GGLU, ((None, D_SSM, 512), lambda i, j, k: (k, 0, 0)), D_SSM, tn=D_SSM, tk=512, name="mm_glu_t")[0]
    pwc_r, pwc_i = pw_r, -pw_i
    dub, dBr, dBi, dCr, dCi, dar8, dai8, dd8 = ssm_bwd(dyssm, p, xr, xi, Br, Bi, Cr, Ci, pwc_r, pwc_i, dvec, name="ssm_bwd")
    dar = jnp.sum(dar8, axis=0).reshape(SSM_GROUPS, SSM_STATE)
    dai = jnp.sum(dai8, axis=0).reshape(SSM_GROUPS, SSM_STATE)
    g_lstep, g_lre, g_lim, g_bre, g_bim = disc_vjp((dar, dai, _diag_in(dBr), _diag_in(dBi)))
    g_cre, g_cim = _diag_out(dCr), _diag_out(dCi)
    g_d = jnp.sum(dd8, axis=0).reshape(1, D_SSM)

    dact = mm_nt(dyab, GA, N=D, name="mm_conv_out_t", **w_rowshard(GA_CONV_OUT))[0]
    GAg = mm_tn(actb, dyab, out_buf=GAg, name="mm_conv_out_g", **g_rowshard(GA_CONV_OUT))
    dc, dng, dnb, ddb = conv_bwd_norm(dact, c_pre, conv_norm_g[0].reshape(1, D), conv_norm_b[0].reshape(1, D), name="conv_bwd_norm")
    dvgb, ddw = conv_bwd_taps(dc, p, dw_full, name="conv_bwd_taps")
    dpb = jnp.concatenate([dvgb, dub, dgab, dgbb], axis=1)
    GINg = mm_tn(h0b, dpb, tm=D, tn=1152, out_spec=((None, D, 1152), lambda i, j, k: (j, 0, 0)),
                 out_shape=jax.ShapeDtypeStruct((N_CHIPS, D, 1152), F32), name="mm_w_in_g")
    dh0 = mm_nt(dpb, GIN, ((None, 512, 1152), lambda i, j, k: (k, j, 0)), D, tn=512, tk=1152,
                extras=(dr1,), epilogue=lambda acc, d: (acc + ALPHA * d,), name="mm_w_in_t")[0]
    gx, _, dg0, db0 = ln_bwd(xs, dh0, in_norm_g, name="ln0_bwd")

    big = [GAg, GINg, GKVg, GGLUg]
    tags = ["a", "in", "kv", "glu"]
    lands = exchange_halves(big, name="rs_exchange_halves")
    parts = [add_sibling(g, l, c_arr, name="rs_add_sibling_" + t) for g, l, t in zip(big, lands, tags)]
    lands2 = scatter_to_owners(parts, name="rs_scatter_to_owners")
    kc_arr = jnp.concatenate([k_arr, c_arr])
    halves = [add_chips(pt, l2, kc_arr, name="rs_add_chips_" + t) for pt, l2, t in zip(parts, lands2, tags)]
    gA, gIN, gKV, gGLU = join_halves(halves, name="rs_join_halves")

    small_names = ["in_norm_g", "in_norm_b", "conv_db", "conv_norm_g", "conv_norm_b", "ssm_log_step", "ssm_lambda_re",
                   "ssm_lambda_im", "ssm_b_re", "ssm_b_im", "ssm_c_re", "ssm_c_im", "ssm_d", "ln1_g", "ln1_b",
                   "ln2_g", "ln2_b", "ln3_g", "ln3_b"]
    small_w = [in_norm_g, in_norm_b, conv_db, conv_norm_g, conv_norm_b, ssm_log_step, ssm_lambda_re, ssm_lambda_im,
               ssm_b_re, ssm_b_im, ssm_c_re, ssm_c_im, ssm_d, ln1_g, ln1_b, ln2_g, ln2_b, ln3_g, ln3_b]
    small_m = [m_in_norm_g, m_in_norm_b, m_conv_db, m_conv_norm_g, m_conv_norm_b, m_ssm_log_step, m_ssm_lambda_re,
               m_ssm_lambda_im, m_ssm_b_re, m_ssm_b_im, m_ssm_c_re, m_ssm_c_im, m_ssm_d, m_ln1_g, m_ln1_b, m_ln2_g,
               m_ln2_b, m_ln3_g, m_ln3_b]
    small_v = [v_in_norm_g, v_in_norm_b, v_conv_db, v_conv_norm_g, v_conv_norm_b, v_ssm_log_step, v_ssm_lambda_re,
               v_ssm_lambda_im, v_ssm_b_re, v_ssm_b_im, v_ssm_c_re, v_ssm_c_im, v_ssm_d, v_ln1_g, v_ln1_b, v_ln2_g,
               v_ln2_b, v_ln3_g, v_ln3_b]
    small_g = [dg0, db0, ddb, dng, dnb, g_lstep, g_lre, g_lim, g_bre, g_bim, g_cre, g_cim, g_d, dg1, db1, dg2, db2, dg3, db3]
    small_shapes = [w.shape for w in small_w]
    n_small_rows = _pack(small_w).shape[0]
    packed_g = _pack(small_g + [ddw])
    _, summed = allgather_sum(packed_g, name="allreduce_small")
    small_rows = sum(-(-math.prod(s) // 128) for s in small_shapes)
    ddw_full = summed[small_rows:small_rows + CONV_HALO * D // 128].reshape(CONV_HALO, D)
    g_dw = lax.dynamic_slice_in_dim(ddw_full, k_me * (D // N_CHIPS), D // N_CHIPS, axis=1)
    gs_packed = jnp.pad(summed[:small_rows], ((0, n_small_rows - small_rows), (0, 0)))

    res = {}

    def upd(nm, w, m, v, g_arr, row0=0):
        shp = w.shape
        w2, m2, v2 = (a.reshape(-1, shp[-1]) for a in (w, m, v))
        outs = adamw(w2, m2, v2, g_arr, row0, name="adamw_" + nm)
        res[nm] = tuple(o.reshape(shp) for o in outs)

    upd("w_conv_out", w_conv_out, m_w_conv_out, v_w_conv_out, gA, GA_CONV_OUT)
    upd("w_mix_out", w_mix_out, m_w_mix_out, v_w_mix_out, gA, GA_MIX_OUT)
    upd("xa_wq", xa_wq, m_xa_wq, v_xa_wq, gA, GA_WQ)
    upd("xa_wo", xa_wo, m_xa_wo, v_xa_wo, gA, GA_WO)
    upd("mlp_w_down", mlp_w_down, m_mlp_w_down, v_mlp_w_down, gA, GA_DOWN)
    upd("mlp_w_up", mlp_w_up, m_mlp_w_up, v_mlp_w_up, gA, GA_UP)
    upd("w_in", w_in, m_w_in, v_w_in, gIN)
    upd("xa_wkv", xa_wkv, m_xa_wkv, v_xa_wkv, gKV)
    upd("w_ssm_glu", w_ssm_glu, m_w_ssm_glu, v_w_ssm_glu, gGLU)
    pad_dw = lambda a: jnp.pad(a[0], ((0, CONV_HALO - CONV_K), (0, 0)))
    dw_outs = adamw(pad_dw(conv_dw), pad_dw(m_conv_dw), pad_dw(v_conv_dw), g_dw, 0, name="adamw_conv_dw")
    res["conv_dw"] = tuple(o[:CONV_K][None] for o in dw_outs)
    sm_outs = adamw(_pack(small_w), _pack(small_m), _pack(small_v), gs_packed, 0, name="adamw_small")
    sm_un = [_unpack(o, small_shapes) for o in sm_outs]
    for idx, nm in enumerate(small_names):
        res[nm] = tuple(sm_un[q][idx] for q in range(4))

    order = ["in_norm_g", "in_norm_b", "w_in", "conv_dw", "conv_db", "conv_norm_g", "conv_norm_b", "w_conv_out",
             "ssm_log_step", "ssm_lambda_re", "ssm_lambda_im", "ssm_b_re", "ssm_b_im", "ssm_c_re", "ssm_c_im", "ssm_d",
             "w_ssm_glu", "w_mix_out", "ln1_g", "ln1_b", "xa_wq", "xa_wkv", "xa_wo", "ln2_g", "ln2_b", "mlp_w_up",
             "mlp_w_down", "ln3_g", "ln3_b"]
    return (loss, gx[None], *[res[n][0] for n in order], *[res[n][1] for n in order],
            *[res[n][2] for n in order], *[res[n][3] for n in order])
```

```python
import functools
import math

import jax
import jax.numpy as jnp
from jax import lax
from jax.experimental import pallas as pl
from jax.experimental.pallas import tpu as pltpu

F32 = jnp.float32
BF16 = jnp.bfloat16
MESH = pl.DeviceIdType.MESH

D_MODEL = 1024
N_HEADS = 4
HEAD_DIM = D_MODEL // N_HEADS
CONV_K = 31
CONV_HALO = 32
D_SSM = 512
SSM_GROUPS = 32
SSM_GROUP = 16
SSM_STATE = 64
SSM_BLOCKS = 4
SSM_BLOCK_IN = D_SSM // SSM_BLOCKS
SSM_BLOCK_STATE = SSM_GROUPS * SSM_STATE // SSM_BLOCKS
D_FF = 4096
D_IN = 4608
LN_EPS = 1e-5
ALPHA = (2.0 * 1) ** 0.25
N_CHIPS = 4
N_DEV = 8
ADAM_LR, ADAM_B1, ADAM_B2, ADAM_EPS, ADAM_WD, ADAM_STEP = 0.001, 0.9, 0.999, 1e-08, 0.01, 10
VMEM_LIMIT_BYTES = 56 * 1024 * 1024


def _pick(dim, cands):
    for c in cands:
        if dim % c == 0:
            return c
    return dim


def _cparams(sem=None):
    return pltpu.CompilerParams(dimension_semantics=sem, vmem_limit_bytes=VMEM_LIMIT_BYTES)


def _sigmoid(x):
    return 1.0 / (1.0 + jnp.exp(-x))


_DIMS = {"nn": (((1,), (0,)), ((), ())), "nt": (((1,), (1,)), ((), ())), "tn": (((0,), (0,)), ((), ()))}


def matmul(a, b, *, mode, M, N, K, tm, tn, tk, a_spec, b_spec, out_specs, out_shapes, name,
           extras=(), extra_specs=(), epilogue=None, alias_buf=None, b_view=None):
    nk = K // tk
    ne = len(extras)
    no = len(out_shapes)
    na = 0 if alias_buf is None else 1
    dims = _DIMS[mode]

    def body(*refs):
        a_ref, b_ref = refs[0], refs[1]
        e_refs = refs[2:2 + ne]
        o_refs = refs[2 + ne + na:2 + ne + na + no]

        def finish(acc):
            outs = (acc,) if epilogue is None else epilogue(acc, *[r[...] for r in e_refs])
            for o, r in zip(outs, o_refs):
                r[...] = o.astype(r.dtype).reshape(r.shape)

        b_blk = b_ref[...] if b_view is None else b_ref[...].reshape(b_view)
        prod = lax.dot_general(a_ref[...].astype(BF16), b_blk.astype(BF16), dims, preferred_element_type=F32)
        if nk == 1:
            finish(prod)
        else:
            acc_ref = refs[-1]
            k = pl.program_id(2)

            @pl.when(k == 0)
            def _():
                acc_ref[...] = prod

            @pl.when(k > 0)
            def _():
                acc_ref[...] += prod

            @pl.when(k == nk - 1)
            def _():
                finish(acc_ref[...])

    in_specs = [pl.BlockSpec(*a_spec), pl.BlockSpec(*b_spec)] + [pl.BlockSpec(*s) for s in extra_specs]
    ins = [a, b, *extras]
    if alias_buf is not None:
        in_specs.append(pl.BlockSpec(memory_space=pl.ANY))
        ins.append(alias_buf)
    res = pl.pallas_call(
        body,
        grid=(M // tm, N // tn, nk),
        in_specs=in_specs,
        out_specs=[pl.BlockSpec(*s) for s in out_specs],
        out_shape=out_shapes,
        scratch_shapes=[] if nk == 1 else [pltpu.VMEM((tm, tn), F32)],
        input_output_aliases={2 + ne: 0} if alias_buf is not None else {},
        compiler_params=_cparams(("parallel", "parallel", "arbitrary")),
        name=name,
    )(*ins)
    return res


def _mn(tm, tn):
    return ((tm, tn), lambda i, j, k: (i, j))


def mm_nn(a, b_arr, b_spec, N, *, name, tm=None, tn, tk, out_dtype=F32, extras=(), epilogue=None, out_dtypes=None,
          b_view=None):
    M, K = a.shape
    tm = tm or _pick(M, [1024, 512, 256, 128])
    dts = out_dtypes or [out_dtype]
    return matmul(a, b_arr, mode="nn", M=M, N=N, K=K, tm=tm, tn=tn, tk=tk,
                  a_spec=((tm, tk), lambda i, j, k: (i, k)), b_spec=b_spec, b_view=b_view,
                  out_specs=[_mn(tm, tn)] * len(dts), out_shapes=[jax.ShapeDtypeStruct((M, N), d) for d in dts],
                  extras=extras, extra_specs=[_mn(tm, tn)] * len(extras), epilogue=epilogue, name=name)


def mm_nt(a, b_arr, b_spec, N, *, name, tm=None, tn, tk, out_dtype=F32, extras=(), epilogue=None, out_dtypes=None,
          b_view=None):
    M, K = a.shape
    tm = tm or _pick(M, [1024, 512, 256, 128])
    dts = out_dtypes or [out_dtype]
    return matmul(a, b_arr, mode="nt", M=M, N=N, K=K, tm=tm, tn=tn, tk=tk,
                  a_spec=((tm, tk), lambda i, j, k: (i, k)), b_spec=b_spec, b_view=b_view,
                  out_specs=[_mn(tm, tn)] * len(dts), out_shapes=[jax.ShapeDtypeStruct((M, N), d) for d in dts],
                  extras=extras, extra_specs=[_mn(tm, tn)] * len(extras), epilogue=epilogue, name=name)


def mm_tn(a, b, *, name, tm, tn, tk=None, out_spec, out_shape, out_buf=None):
    K, M = a.shape
    N = b.shape[1]
    tk = tk or _pick(K, [1024, 512, 256, 128])
    return matmul(a, b, mode="tn", M=M, N=N, K=K, tm=tm, tn=tn, tk=tk,
                  a_spec=((tk, tm), lambda i, j, k: (k, i)), b_spec=((tk, tn), lambda i, j, k: (k, j)),
                  out_specs=[out_spec], out_shapes=[out_shape], alias_buf=out_buf, name=name)[0]


def _rows(tc, w, cb=0):
    return pl.BlockSpec((tc, w), lambda i: (i, cb))


def _const(shape):
    return pl.BlockSpec(shape, lambda i: tuple([0] * len(shape)))


def _ln_stats(r):
    mu = jnp.mean(r, axis=-1, keepdims=True)
    xc = r - mu
    var = jnp.mean(xc * xc, axis=-1, keepdims=True)
    rstd = lax.rsqrt(var + LN_EPS)
    return xc * rstd, rstd


def _rowsum8(v):
    tc, w = v.shape
    return jnp.sum(v.reshape(tc // 8, 8, w), axis=0)


def ln_fwd(x, g, b, *, name, res=None):
    T, D = x.shape
    tc = _pick(T, [512, 256, 128])
    has_res = res is not None

    def body(*refs):
        if has_res:
            x_ref, res_ref, g_ref, b_ref, r_ref, h_ref, hb_ref = refs
            r = ALPHA * res_ref[...] + x_ref[...]
            r_ref[...] = r
        else:
            x_ref, g_ref, b_ref, h_ref, hb_ref = refs
            r = x_ref[...]
        xhat, _ = _ln_stats(r)
        y = xhat * g_ref[...] + b_ref[...]
        h_ref[...] = y
        hb_ref[...] = y.astype(BF16)

    ins = [x] + ([res] if has_res else []) + [g.reshape(1, D), b.reshape(1, D)]
    in_specs = [_rows(tc, D)] * (2 if has_res else 1) + [_const((1, D))] * 2
    n_out = 3 if has_res else 2
    outs = pl.pallas_call(
        body, grid=(T // tc,), in_specs=in_specs, out_specs=[_rows(tc, D)] * n_out,
        out_shape=[jax.ShapeDtypeStruct((T, D), F32)] * (n_out - 1) + [jax.ShapeDtypeStruct((T, D), BF16)],
        compiler_params=_cparams(("arbitrary",)), name=name)(*ins)
    if has_res:
        return outs
    return (x,) + tuple(outs)


def ln_bwd(r, dy, g, *, name):
    T, D = r.shape
    tc = _pick(T, [512, 256, 128])
    nt = T // tc

    def body(r_ref, dy_ref, g_ref, dr_ref, drb_ref, dg_ref, db_ref, accg, accb):
        i = pl.program_id(0)

        @pl.when(i == 0)
        def _():
            accg[...] = jnp.zeros_like(accg)
            accb[...] = jnp.zeros_like(accb)

        xhat, rstd = _ln_stats(r_ref[...])
        dy = dy_ref[...]
        dxh = dy * g_ref[...]
        m1 = jnp.mean(dxh, axis=-1, keepdims=True)
        m2 = jnp.mean(dxh * xhat, axis=-1, keepdims=True)
        dr = rstd * (dxh - m1 - xhat * m2)
        dr_ref[...] = dr
        drb_ref[...] = dr.astype(BF16)
        accg[...] += _rowsum8(dy * xhat)
        accb[...] += _rowsum8(dy)

        @pl.when(i == nt - 1)
        def _():
            dg_ref[...] = jnp.sum(accg[...], axis=0, keepdims=True)
            db_ref[...] = jnp.sum(accb[...], axis=0, keepdims=True)

    return pl.pallas_call(
        body, grid=(nt,), in_specs=[_rows(tc, D), _rows(tc, D), _const((1, D))],
        out_specs=[_rows(tc, D), _rows(tc, D), _const((1, D)), _const((1, D))],
        out_shape=[jax.ShapeDtypeStruct((T, D), F32), jax.ShapeDtypeStruct((T, D), BF16),
                   jax.ShapeDtypeStruct((1, D), F32), jax.ShapeDtypeStruct((1, D), F32)],
        scratch_shapes=[pltpu.VMEM((8, D), F32), pltpu.VMEM((8, D), F32)],
        compiler_params=_cparams(("arbitrary",)), name=name)(r, dy, g.reshape(1, D))


def loss_head(y, target, *, name):
    T, D = y.shape
    tc = _pick(T, [512, 256, 128])
    nt = T // tc

    def body(y_ref, t_ref, dy_ref, loss_ref, acc):
        i = pl.program_id(0)

        @pl.when(i == 0)
        def _():
            acc[...] = jnp.zeros_like(acc)

        e = y_ref[...] - t_ref[...]
        dy_ref[...] = e * (1.0 / D)
        acc[...] += _rowsum8(e * e)

        @pl.when(i == nt - 1)
        def _():
            s = jnp.sum(jnp.sum(acc[...], axis=0, keepdims=True), axis=1, keepdims=True)
            loss_ref[...] = jnp.broadcast_to(s, (1, 128))

    return pl.pallas_call(
        body, grid=(nt,), in_specs=[_rows(tc, D), _rows(tc, D)],
        out_specs=[_rows(tc, D), _const((1, 128))],
        out_shape=[jax.ShapeDtypeStruct((T, D), F32), jax.ShapeDtypeStruct((1, 128), F32)],
        scratch_shapes=[pltpu.VMEM((8, D), F32)],
        compiler_params=_cparams(("arbitrary",)), name=name)(y, target)


def _halo_prev(tc):
    per = tc // CONV_HALO
    return lambda i: jnp.maximum(i * per - 1, 0)


CONV_ROWS = 32
CONV_TAP_GROUP = 4


def _fill_shifts(S, nrows):
    for b in range(1, 8):
        S[b, 0:nrows - 8, :] = S[0, b:b + nrows - 8, :]


def _tap_sum(S, w_ref, offs, r0, nrows):
    acc = None
    for k, o in enumerate(offs):
        a, b = divmod(o, 8)
        term = w_ref[k:k + 1, :] * S[b, pl.ds(pl.multiple_of(r0 + 8 * a, 8), nrows), :]
        acc = term if acc is None else acc + term
    return acc


def conv_fwd(p, dw, db, ng, nb, *, name):
    T = p.shape[0]
    D = D_MODEL
    tc = _pick(T, [256, 128])
    prev = _halo_prev(tc)
    off = CONV_HALO - (CONV_K - 1)
    offs = [off + k for k in range(CONV_K)]

    def body(val_ref, gate_ref, valp_ref, gatep_ref, dw_ref, db_ref, ng_ref, nb_ref, c_ref, act_ref, S):
        i = pl.program_id(0)
        u_prev = valp_ref[...] * _sigmoid(gatep_ref[...])
        S[0, 0:CONV_HALO, :] = jnp.where(i > 0, u_prev, 0.0)
        S[0, CONV_HALO:CONV_HALO + tc, :] = val_ref[...] * _sigmoid(gate_ref[...])
        _fill_shifts(S, CONV_HALO + tc)

        def rows(j, carry):
            r0 = pl.multiple_of(j * CONV_ROWS, CONV_ROWS)
            c_ref[pl.ds(r0, CONV_ROWS), :] = _tap_sum(S, dw_ref, offs, r0, CONV_ROWS) + db_ref[...]
            return carry

        lax.fori_loop(0, tc // CONV_ROWS, rows, 0)
        c = c_ref[...]
        xhat, _ = _ln_stats(c)
        cn = xhat * ng_ref[...] + nb_ref[...]
        act_ref[...] = (cn * _sigmoid(cn)).astype(BF16)

    return pl.pallas_call(
        body, grid=(T // tc,),
        in_specs=[_rows(tc, D, 0), _rows(tc, D, 1),
                  pl.BlockSpec((CONV_HALO, D), lambda i: (prev(i), 0)), pl.BlockSpec((CONV_HALO, D), lambda i: (prev(i), 1)),
                  _const((CONV_HALO, D)), _const((1, D)), _const((1, D)), _const((1, D))],
        out_specs=[_rows(tc, D), _rows(tc, D)],
        out_shape=[jax.ShapeDtypeStruct((T, D), F32), jax.ShapeDtypeStruct((T, D), BF16)],
        scratch_shapes=[pltpu.VMEM((8, CONV_HALO + tc, D), F32)],
        compiler_params=_cparams(("arbitrary",)), name=name)(p, p, p, p, dw, db, ng, nb)


def conv_bwd_norm(dact, c_pre, ng, nb, *, name):
    T, D = c_pre.shape
    tc = _pick(T, [512, 256, 128])
    nt = T // tc

    def body(da_ref, c_ref, ng_ref, nb_ref, dc_ref, dng_ref, dnb_ref, ddb_ref, accg, accb, accd):
        i = pl.program_id(0)

        @pl.when(i == 0)
        def _():
            accg[...] = jnp.zeros_like(accg)
            accb[...] = jnp.zeros_like(accb)
            accd[...] = jnp.zeros_like(accd)

        xhat, rstd = _ln_stats(c_ref[...])
        cn = xhat * ng_ref[...] + nb_ref[...]
        s = _sigmoid(cn)
        dcn = da_ref[...] * (s * (1.0 + cn * (1.0 - s)))
        dxh = dcn * ng_ref[...]
        m1 = jnp.mean(dxh, axis=-1, keepdims=True)
        m2 = jnp.mean(dxh * xhat, axis=-1, keepdims=True)
        dc = rstd * (dxh - m1 - xhat * m2)
        dc_ref[...] = dc
        accg[...] += _rowsum8(dcn * xhat)
        accb[...] += _rowsum8(dcn)
        accd[...] += _rowsum8(dc)

        @pl.when(i == nt - 1)
        def _():
            dng_ref[...] = jnp.sum(accg[...], axis=0, keepdims=True)
            dnb_ref[...] = jnp.sum(accb[...], axis=0, keepdims=True)
            ddb_ref[...] = jnp.sum(accd[...], axis=0, keepdims=True)

    vec = jax.ShapeDtypeStruct((1, D), F32)
    return pl.pallas_call(
        body, grid=(nt,), in_specs=[_rows(tc, D), _rows(tc, D), _const((1, D)), _const((1, D))],
        out_specs=[_rows(tc, D), _const((1, D)), _const((1, D)), _const((1, D))],
        out_shape=[jax.ShapeDtypeStruct((T, D), F32), vec, vec, vec],
        scratch_shapes=[pltpu.VMEM((8, D), F32)] * 3,
        compiler_params=_cparams(("arbitrary",)), name=name)(dact, c_pre, ng, nb)


def conv_bwd_taps(dc, p, dw, *, name):
    T, D = dc.shape
    tc = _pick(T, [256, 128])
    nt = T // tc
    per = tc // CONV_HALO
    prev = _halo_prev(tc)
    last_halo = T // CONV_HALO - 1
    nxt = lambda i: jnp.minimum((i + 1) * per, last_halo)
    off = CONV_HALO - (CONV_K - 1)

    def body(dc_ref, dcn_ref, val_ref, gate_ref, valp_ref, gatep_ref, dw_ref, dvg_ref, ddw_ref, ext_u, ext_d, acc):
        i = pl.program_id(0)

        @pl.when(i == 0)
        def _():
            acc[...] = jnp.zeros_like(acc)

        u_prev = valp_ref[...] * _sigmoid(gatep_ref[...])
        ext_u[0, 0:CONV_HALO, :] = jnp.where(i > 0, u_prev, 0.0)
        ext_u[0, CONV_HALO:CONV_HALO + tc, :] = val_ref[...] * _sigmoid(gate_ref[...])
        ext_d[0, 0:tc, :] = dc_ref[...]
        ext_d[0, tc:tc + CONV_HALO, :] = jnp.where(i < nt - 1, dcn_ref[...], 0.0)
        _fill_shifts(ext_u, CONV_HALO + tc)
        _fill_shifts(ext_d, CONV_HALO + tc)

        def rows(j, carry):
            r0 = pl.multiple_of(j * CONV_ROWS, CONV_ROWS)
            sl = pl.ds(r0, CONV_ROWS)
            du = _tap_sum(ext_d, dw_ref, [CONV_K - 1 - k for k in range(CONV_K)], r0, CONV_ROWS)
            sg = _sigmoid(gate_ref[sl, :])
            dvg_ref[sl, 0:D] = (du * sg).astype(BF16)
            dvg_ref[sl, D:2 * D] = (du * val_ref[sl, :] * sg * (1.0 - sg)).astype(BF16)
            return carry

        lax.fori_loop(0, tc // CONV_ROWS, rows, 0)

        for k0 in range(0, CONV_K, CONV_TAP_GROUP):
            ks = list(range(k0, min(k0 + CONV_TAP_GROUP, CONV_K)))

            def taps(j, accs, ks=ks):
                r0 = pl.multiple_of(j * 8, 8)
                dct = dc_ref[pl.ds(r0, 8), :]
                out = []
                for k, a_k in zip(ks, accs):
                    a, b = divmod(off + k, 8)
                    out.append(a_k + dct * ext_u[b, pl.ds(pl.multiple_of(r0 + 8 * a, 8), 8), :])
                return tuple(out)

            accs = lax.fori_loop(0, tc // 8, taps, tuple(jnp.zeros((8, D), F32) for _ in ks))
            for k, a_k in zip(ks, accs):
                acc[k] += a_k

        @pl.when(i == nt - 1)
        def _():
            ddw_ref[...] = jnp.zeros_like(ddw_ref)
            for k in range(CONV_K):
                ddw_ref[k:k + 1, :] = jnp.sum(acc[k], axis=0, keepdims=True)

    return pl.pallas_call(
        body, grid=(nt,),
        in_specs=[_rows(tc, D), pl.BlockSpec((CONV_HALO, D), lambda i: (nxt(i), 0)),
                  _rows(tc, D, 0), _rows(tc, D, 1),
                  pl.BlockSpec((CONV_HALO, D), lambda i: (prev(i), 0)), pl.BlockSpec((CONV_HALO, D), lambda i: (prev(i), 1)),
                  _const((CONV_HALO, D))],
        out_specs=[_rows(tc, 2 * D), _const((CONV_HALO, D))],
        out_shape=[jax.ShapeDtypeStruct((T, 2 * D), BF16), jax.ShapeDtypeStruct((CONV_HALO, D), F32)],
        scratch_shapes=[pltpu.VMEM((8, CONV_HALO + tc, D), F32), pltpu.VMEM((8, CONV_HALO + tc, D), F32),
                        pltpu.VMEM((CONV_K, 8, D), F32)],
        compiler_params=_cparams(("arbitrary",)), name=name)(dc, dc, p, p, p, p, dw)


GATE_A0 = (2 * D_MODEL + D_SSM) // 512
GATE_B0 = GATE_A0 + 2


def merge_fwd(p, ya, z, *, name):
    T = p.shape[0]
    D = D_MODEL
    tc = _pick(T, [512, 256, 128])
    W = 512

    def body(ga_ref, gb_ref, ya_ref, z1_ref, z2_ref, o_ref):
        yb = z1_ref[...] * _sigmoid(z2_ref[...])
        o_ref[...] = (_sigmoid(ga_ref[...]) * ya_ref[...] + _sigmoid(gb_ref[...]) * yb).astype(BF16)

    return pl.pallas_call(
        body, grid=(T // tc, D // W),
        in_specs=[pl.BlockSpec((tc, W), lambda i, j: (i, GATE_A0 + j)), pl.BlockSpec((tc, W), lambda i, j: (i, GATE_B0 + j)),
                  pl.BlockSpec((tc, W), lambda i, j: (i, j)), pl.BlockSpec((tc, W), lambda i, j: (i, j)),
                  pl.BlockSpec((tc, W), lambda i, j: (i, D // W + j))],
        out_specs=pl.BlockSpec((tc, W), lambda i, j: (i, j)),
        out_shape=jax.ShapeDtypeStruct((T, D), BF16),
        compiler_params=_cparams(("arbitrary", "arbitrary")), name=name)(p, p, ya, z, z)


def merge_bwd(dm, p, ya, z, *, name):
    T = p.shape[0]
    D = D_MODEL
    tc = _pick(T, [512, 256, 128])
    W = 512
    nb = D // W

    def body(dm_ref, ga_ref, gb_ref, ya_ref, z1_ref, z2_ref, dya_ref, dga_ref, dgb_ref, dz1_ref, dz2_ref):
        dm = dm_ref[...]
        sa = _sigmoid(ga_ref[...])
        sb = _sigmoid(gb_ref[...])
        s2 = _sigmoid(z2_ref[...])
        z1 = z1_ref[...]
        yb = z1 * s2
        dya_ref[...] = (dm * sa).astype(BF16)
        dga_ref[...] = (dm * ya_ref[...] * sa * (1.0 - sa)).astype(BF16)
        dgb_ref[...] = (dm * yb * sb * (1.0 - sb)).astype(BF16)
        dyb = dm * sb
        dz1_ref[...] = (dyb * s2).astype(BF16)
        dz2_ref[...] = (dyb * z1 * s2 * (1.0 - s2)).astype(BF16)

    blk = lambda off: pl.BlockSpec((tc, W), lambda i, j: (i, off + j))
    dya, dga, dgb, dz1, dz2 = pl.pallas_call(
        body, grid=(T // tc, nb),
        in_specs=[blk(0), blk(GATE_A0), blk(GATE_B0), blk(0), blk(0), blk(nb)],
        out_specs=[blk(0)] * 5,
        out_shape=[jax.ShapeDtypeStruct((T, D), BF16)] * 5,
        compiler_params=_cparams(("arbitrary", "arbitrary")), name=name)(dm, p, p, ya, z, z)
    return dya, dga, dgb, dz1, dz2


def _scan_block(src_r, src_i, dst_r, dst_i, car_r, car_i, pw_r, pw_i, cw_r, cw_i, ntiles, reverse, extra=None):
    W = src_r.shape[1]
    rows = lax.broadcasted_iota(jnp.int32, (8, W), 0)
    steps = []
    for d, pr in ((1, 0), (2, 1), (4, 3)):
        steps.append((d, jnp.broadcast_to(pw_r[pr:pr + 1, :], (8, W)), jnp.broadcast_to(pw_i[pr:pr + 1, :], (8, W))))
    cw_r, cw_i = cw_r[...], cw_i[...]

    def tile(jj, carry):
        j = ntiles - 1 - jj if reverse else jj
        sl = pl.ds(pl.multiple_of(j * 8, 8), 8)
        xr, xi = src_r[sl, :], src_i[sl, :]
        for d, lr, li in steps:
            if reverse:
                sr = jnp.where(rows < 8 - d, pltpu.roll(xr, 8 - d, 0), 0.0)
                si = jnp.where(rows < 8 - d, pltpu.roll(xi, 8 - d, 0), 0.0)
            else:
                sr = jnp.where(rows >= d, pltpu.roll(xr, d, 0), 0.0)
                si = jnp.where(rows >= d, pltpu.roll(xi, d, 0), 0.0)
            xr, xi = xr + lr * sr - li * si, xi + lr * si + li * sr
        cr, ci = car_r[...], car_i[...]
        xr, xi = xr + cw_r * cr - cw_i * ci, xi + cw_r * ci + cw_i * cr
        dst_r[sl, :] = xr
        dst_i[sl, :] = xi
        edge = 0 if reverse else 7
        car_r[...] = jnp.broadcast_to(xr[edge:edge + 1, :], (8, W))
        car_i[...] = jnp.broadcast_to(xi[edge:edge + 1, :], (8, W))
        if extra is not None:
            carry = extra(j, xr, xi, carry)
        return carry

    return tile


def ssm_fwd(p, Br, Bi, Cr, Ci, pw_r, pw_i, dvec, *, name):
    T = p.shape[0]
    tt = _pick(T, [256, 128])
    nt = T // tt
    WI, WS = SSM_BLOCK_IN, SSM_BLOCK_STATE
    u0 = 2 * D_MODEL // WI

    def body(u_ref, br_ref, bi_ref, cr_ref, ci_ref, pwr_ref, pwi_ref, d_ref, xr_ref, xi_ref, y_ref, bur, bui, car_r, car_i):
        i = pl.program_id(1)

        @pl.when(i == 0)
        def _():
            car_r[...] = jnp.zeros_like(car_r)
            car_i[...] = jnp.zeros_like(car_i)

        u = u_ref[...]
        ub = u.astype(BF16)
        bur[...] = jnp.dot(ub, br_ref[...].astype(BF16), preferred_element_type=F32)
        bui[...] = jnp.dot(ub, bi_ref[...].astype(BF16), preferred_element_type=F32)
        tile = _scan_block(bur, bui, xr_ref, xi_ref, car_r, car_i, pwr_ref, pwi_ref, pwr_ref, pwi_ref, tt // 8, False)
        lax.fori_loop(0, tt // 8, tile, 0)
        y = (jnp.dot(xr_ref[...].astype(BF16), cr_ref[...].astype(BF16), preferred_element_type=F32)
             - jnp.dot(xi_ref[...].astype(BF16), ci_ref[...].astype(BF16), preferred_element_type=F32)
             + d_ref[...] * u)
        y_ref[...] = y.astype(BF16)

    return pl.pallas_call(
        body, grid=(SSM_BLOCKS, nt),
        in_specs=[pl.BlockSpec((tt, WI), lambda b, i: (i, u0 + b)),
                  pl.BlockSpec((None, WI, WS), lambda b, i: (b, 0, 0)), pl.BlockSpec((None, WI, WS), lambda b, i: (b, 0, 0)),
                  pl.BlockSpec((None, WS, WI), lambda b, i: (b, 0, 0)), pl.BlockSpec((None, WS, WI), lambda b, i: (b, 0, 0)),
                  pl.BlockSpec((8, WS), lambda b, i: (0, b)), pl.BlockSpec((8, WS), lambda b, i: (0, b)),
                  pl.BlockSpec((1, WI), lambda b, i: (0, b))],
        out_specs=[pl.BlockSpec((tt, WS), lambda b, i: (i, b)), pl.BlockSpec((tt, WS), lambda b, i: (i, b)),
                   pl.BlockSpec((tt, WI), lambda b, i: (i, b))],
        out_shape=[jax.ShapeDtypeStruct((T, SSM_BLOCKS * WS), F32)] * 2 + [jax.ShapeDtypeStruct((T, D_SSM), BF16)],
        scratch_shapes=[pltpu.VMEM((tt, WS), F32), pltpu.VMEM((tt, WS), F32), pltpu.VMEM((8, WS), F32), pltpu.VMEM((8, WS), F32)],
        compiler_params=_cparams(("arbitrary", "arbitrary")), name=name)(p, Br, Bi, Cr, Ci, pw_r, pw_i, dvec)


def ssm_bwd(dy, p, xr, xi, Br, Bi, Cr, Ci, pwc_r, pwc_i, dvec, *, name):
    T = p.shape[0]
    tt = _pick(T, [256, 128])
    nt = T // tt
    WI, WS = SSM_BLOCK_IN, SSM_BLOCK_STATE
    u0 = 2 * D_MODEL // WI
    tb = lambda i: nt - 1 - i
    xprev = lambda i: jnp.maximum(tb(i) * (tt // 8) - 1, 0)
    tn_dims = _DIMS["tn"]
    nt_dims = _DIMS["nt"]

    def body(dy_ref, u_ref, xr_ref, xi_ref, xpr_ref, xpi_ref, br_ref, bi_ref, cr_ref, ci_ref, pwr_ref, pwi_ref,
             cwr_ref, cwi_ref, d_ref,
             du_ref, dbr_ref, dbi_ref, dcr_ref, dci_ref, dar_ref, dai_ref, dd_ref,
             gr, gi, ext_r, ext_i, car_r, car_i):
        i = pl.program_id(1)

        @pl.when(i == 0)
        def _():
            car_r[...] = jnp.zeros_like(car_r)
            car_i[...] = jnp.zeros_like(car_i)
            dbr_ref[...] = jnp.zeros_like(dbr_ref)
            dbi_ref[...] = jnp.zeros_like(dbi_ref)
            dcr_ref[...] = jnp.zeros_like(dcr_ref)
            dci_ref[...] = jnp.zeros_like(dci_ref)
            dar_ref[...] = jnp.zeros_like(dar_ref)
            dai_ref[...] = jnp.zeros_like(dai_ref)
            dd_ref[...] = jnp.zeros_like(dd_ref)

        dy = dy_ref[...]
        dyb = dy.astype(BF16)
        u = u_ref[...]
        ub = u.astype(BF16)
        gr[...] = lax.dot_general(dyb, cr_ref[...].astype(BF16), nt_dims, preferred_element_type=F32)
        gi[...] = -lax.dot_general(dyb, ci_ref[...].astype(BF16), nt_dims, preferred_element_type=F32)
        first = tb(i) == 0
        ext_r[0:8, :] = jnp.where(first, 0.0, xpr_ref[...])
        ext_i[0:8, :] = jnp.where(first, 0.0, xpi_ref[...])
        ext_r[8:8 + tt, :] = xr_ref[...]
        ext_i[8:8 + tt, :] = xi_ref[...]
        rows = lax.broadcasted_iota(jnp.int32, (8, WS), 0)

        def lam_grad(j, g_r, g_i, carry):
            a_r, a_i = carry
            cur = pl.ds(pl.multiple_of(j * 8 + 8, 8), 8)
            prv = pl.ds(pl.multiple_of(j * 8, 8), 8)
            xc_r, xc_i = ext_r[cur, :], ext_i[cur, :]
            xl_r, xl_i = ext_r[prv, :], ext_i[prv, :]
            xp_r = jnp.where(rows == 0, jnp.broadcast_to(xl_r[7:8, :], (8, WS)), pltpu.roll(xc_r, 1, 0))
            xp_i = jnp.where(rows == 0, jnp.broadcast_to(xl_i[7:8, :], (8, WS)), pltpu.roll(xc_i, 1, 0))
            return (a_r + g_r * xp_r + g_i * xp_i, a_i + g_i * xp_r - g_r * xp_i)

        tile = _scan_block(gr, gi, gr, gi, car_r, car_i, pwr_ref, pwi_ref, cwr_ref, cwi_ref, tt // 8, True, extra=lam_grad)
        z8 = jnp.zeros((8, WS), F32)
        a_r, a_i = lax.fori_loop(0, tt // 8, tile, (z8, z8))
        dar_ref[...] += a_r
        dai_ref[...] += a_i
        grb = gr[...].astype(BF16)
        gib = gi[...].astype(BF16)
        dbr_ref[...] += lax.dot_general(ub, grb, tn_dims, preferred_element_type=F32)
        dbi_ref[...] += lax.dot_general(ub, gib, tn_dims, preferred_element_type=F32)
        dcr_ref[...] += lax.dot_general(xr_ref[...].astype(BF16), dyb, tn_dims, preferred_element_type=F32)
        dci_ref[...] -= lax.dot_general(xi_ref[...].astype(BF16), dyb, tn_dims, preferred_element_type=F32)
        du = (lax.dot_general(grb, br_ref[...].astype(BF16), nt_dims, preferred_element_type=F32)
              + lax.dot_general(gib, bi_ref[...].astype(BF16), nt_dims, preferred_element_type=F32)
              + d_ref[...] * dy)
        du_ref[...] = du.astype(BF16)
        dd_ref[...] += _rowsum8(dy * u)

    wspec = lambda shp: pl.BlockSpec((None,) + shp, lambda b, i: (b, 0, 0))
    return pl.pallas_call(
        body, grid=(SSM_BLOCKS, nt),
        in_specs=[pl.BlockSpec((tt, WI), lambda b, i: (tb(i), b)),
                  pl.BlockSpec((tt, WI), lambda b, i: (tb(i), u0 + b)),
                  pl.BlockSpec((tt, WS), lambda b, i: (tb(i), b)), pl.BlockSpec((tt, WS), lambda b, i: (tb(i), b)),
                  pl.BlockSpec((8, WS), lambda b, i: (xprev(i), b)), pl.BlockSpec((8, WS), lambda b, i: (xprev(i), b)),
                  wspec((WI, WS)), wspec((WI, WS)), wspec((WS, WI)), wspec((WS, WI)),
                  pl.BlockSpec((8, WS), lambda b, i: (0, b)), pl.BlockSpec((8, WS), lambda b, i: (0, b)),
                  pl.BlockSpec((8, WS), lambda b, i: (0, b)), pl.BlockSpec((8, WS), lambda b, i: (0, b)),
                  pl.BlockSpec((1, WI), lambda b, i: (0, b))],
        out_specs=[pl.BlockSpec((tt, WI), lambda b, i: (tb(i), b)),
                   wspec((WI, WS)), wspec((WI, WS)), wspec((WS, WI)), wspec((WS, WI)),
                   pl.BlockSpec((8, WS), lambda b, i: (0, b)), pl.BlockSpec((8, WS), lambda b, i: (0, b)),
                   pl.BlockSpec((8, WI), lambda b, i: (0, b))],
        out_shape=[jax.ShapeDtypeStruct((T, D_SSM), BF16),
                   jax.ShapeDtypeStruct((SSM_BLOCKS, WI, WS), F32), jax.ShapeDtypeStruct((SSM_BLOCKS, WI, WS), F32),
                   jax.ShapeDtypeStruct((SSM_BLOCKS, WS, WI), F32), jax.ShapeDtypeStruct((SSM_BLOCKS, WS, WI), F32),
                   jax.ShapeDtypeStruct((8, SSM_BLOCKS * WS), F32), jax.ShapeDtypeStruct((8, SSM_BLOCKS * WS), F32),
                   jax.ShapeDtypeStruct((8, D_SSM), F32)],
        scratch_shapes=[pltpu.VMEM((tt, WS), F32), pltpu.VMEM((tt, WS), F32),
                        pltpu.VMEM((tt + 8, WS), F32), pltpu.VMEM((tt + 8, WS), F32),
                        pltpu.VMEM((8, WS), F32), pltpu.VMEM((8, WS), F32)],
        compiler_params=_cparams(("arbitrary", "arbitrary")), name=name,
    )(dy, p, xr, xi, xr, xi, Br, Bi, Cr, Ci, pwc_r, pwc_i, pwc_r[::-1], pwc_i[::-1], dvec)


def _ssm_discretise(log_step, lam_re, lam_im, b_re, b_im):
    step = jnp.exp(log_step)[:, None]
    mag = jnp.exp(lam_re * step)
    ar = mag * jnp.cos(lam_im * step)
    ai = mag * jnp.sin(lam_im * step)
    den = lam_re * lam_re + lam_im * lam_im
    nr = ar - 1.0
    cr = (nr * lam_re + ai * lam_im) / den
    ci = (ai * lam_re - nr * lam_im) / den
    bbr = cr[..., None] * b_re - ci[..., None] * b_im
    bbi = cr[..., None] * b_im + ci[..., None] * b_re
    return ar, ai, bbr, bbi


def _blockdiag_in(bb):
    t = jnp.transpose(bb, (0, 2, 1)).reshape(SSM_BLOCKS, 8, SSM_GROUP, SSM_STATE)
    eye = jnp.eye(8, dtype=bb.dtype)
    return (t[:, :, :, None, :] * eye[None, :, None, :, None]).reshape(SSM_BLOCKS, SSM_BLOCK_IN, SSM_BLOCK_STATE)


def _blockdiag_out(cc):
    t = jnp.transpose(cc, (0, 2, 1)).reshape(SSM_BLOCKS, 8, SSM_STATE, SSM_GROUP)
    eye = jnp.eye(8, dtype=cc.dtype)
    return (t[:, :, :, None, :] * eye[None, :, None, :, None]).reshape(SSM_BLOCKS, SSM_BLOCK_STATE, SSM_BLOCK_IN)


def _diag_in(d):
    t = d.reshape(SSM_BLOCKS, 8, SSM_GROUP, 8, SSM_STATE)
    t = jnp.einsum("bghgp->bghp", t).reshape(SSM_GROUPS, SSM_GROUP, SSM_STATE)
    return jnp.transpose(t, (0, 2, 1))


def _diag_out(d):
    t = d.reshape(SSM_BLOCKS, 8, SSM_STATE, 8, SSM_GROUP)
    t = jnp.einsum("bgpgh->bgph", t).reshape(SSM_GROUPS, SSM_STATE, SSM_GROUP)
    return jnp.transpose(t, (0, 2, 1))


def _powers(ar, ai):
    rs, is_ = [ar], [ai]
    for _ in range(7):
        r, i = rs[-1], is_[-1]
        rs.append(r * ar - i * ai)
        is_.append(r * ai + i * ar)
    return jnp.stack(rs), jnp.stack(is_)


def attn_fwd(q, kv, *, name):
    T, D = q.shape
    nm = kv.shape[0]
    tq = _pick(T, [512, 256, 128])
    scale = HEAD_DIM ** -0.5

    def body(q_ref, k_ref, v_ref, o_ref):
        for h in range(N_HEADS):
            sl = slice(h * HEAD_DIM, (h + 1) * HEAD_DIM)
            s = lax.dot_general(q_ref[:, sl], k_ref[:, sl].astype(BF16), _DIMS["nt"], preferred_element_type=F32) * scale
            e = jnp.exp(s - jnp.max(s, axis=-1, keepdims=True))
            pr = e / jnp.sum(e, axis=-1, keepdims=True)
            o_ref[:, sl] = jnp.dot(pr.astype(BF16), v_ref[:, sl].astype(BF16), preferred_element_type=F32).astype(BF16)

    return pl.pallas_call(
        body, grid=(T // tq,),
        in_specs=[_rows(tq, D), pl.BlockSpec((nm, D), lambda i: (0, 0)), pl.BlockSpec((nm, D), lambda i: (0, 1))],
        out_specs=_rows(tq, D), out_shape=jax.ShapeDtypeStruct((T, D), BF16),
        compiler_params=_cparams(("arbitrary",)), name=name)(q, kv, kv)


def attn_bwd(q, kv, do, *, name):
    T, D = q.shape
    nm = kv.shape[0]
    tq = _pick(T, [512, 256, 128])
    nt = T // tq
    scale = HEAD_DIM ** -0.5

    def body(q_ref, k_ref, v_ref, do_ref, dq_ref, dkv_ref):
        i = pl.program_id(0)

        @pl.when(i == 0)
        def _():
            dkv_ref[...] = jnp.zeros_like(dkv_ref)

        for h in range(N_HEADS):
            sl = slice(h * HEAD_DIM, (h + 1) * HEAD_DIM)
            slv = slice(D + h * HEAD_DIM, D + (h + 1) * HEAD_DIM)
            qh = q_ref[:, sl]
            kh = k_ref[:, sl].astype(BF16)
            vh = v_ref[:, sl].astype(BF16)
            doh = do_ref[:, sl].astype(BF16)
            s = lax.dot_general(qh, kh, _DIMS["nt"], preferred_element_type=F32) * scale
            e = jnp.exp(s - jnp.max(s, axis=-1, keepdims=True))
            pr = e / jnp.sum(e, axis=-1, keepdims=True)
            dp = lax.dot_general(doh, vh, _DIMS["nt"], preferred_element_type=F32)
            ds = (pr * (dp - jnp.sum(pr * dp, axis=-1, keepdims=True)) * scale).astype(BF16)
            dq_ref[:, sl] = jnp.dot(ds, kh, preferred_element_type=F32).astype(BF16)
            dkv_ref[:, sl] += lax.dot_general(ds, qh, _DIMS["tn"], preferred_element_type=F32)
            dkv_ref[:, slv] += lax.dot_general(pr.astype(BF16), doh, _DIMS["tn"], preferred_element_type=F32)

    return pl.pallas_call(
        body, grid=(nt,),
        in_specs=[_rows(tq, D), pl.BlockSpec((nm, D), lambda i: (0, 0)), pl.BlockSpec((nm, D), lambda i: (0, 1)), _rows(tq, D)],
        out_specs=[_rows(tq, D), _const((nm, 2 * D))],
        out_shape=[jax.ShapeDtypeStruct((T, D), BF16), jax.ShapeDtypeStruct((nm, 2 * D), F32)],
        compiler_params=_cparams(("arbitrary",)), name=name)(q, kv, kv, do)


def _adam_math(w, g, m, v):
    m = ADAM_B1 * m + (1.0 - ADAM_B1) * g
    v = ADAM_B2 * v + (1.0 - ADAM_B2) * (g * g)
    m_hat = m / (1.0 - ADAM_B1 ** ADAM_STEP)
    v_hat = v / (1.0 - ADAM_B2 ** ADAM_STEP)
    delta = -ADAM_LR * (m_hat / (jnp.sqrt(v_hat) + ADAM_EPS) + ADAM_WD * w)
    return delta, m, v


def adamw(w, m, v, g_arr, g_row0, *, name):
    R, C = w.shape
    tr = _pick(R, [256, 128, 64, 32, 16, 8])
    assert g_row0 % tr == 0
    g0 = g_row0 // tr

    def body(w_ref, m_ref, v_ref, g_ref, go_ref, d_ref, mo_ref, vo_ref):
        g = g_ref[...]
        d, mn, vn = _adam_math(w_ref[...], g, m_ref[...], v_ref[...])
        go_ref[...] = g
        d_ref[...] = d
        mo_ref[...] = mn
        vo_ref[...] = vn

    sp = pl.BlockSpec((tr, C), lambda i: (i, 0))
    return pl.pallas_call(
        body, grid=(R // tr,), in_specs=[sp, sp, sp, pl.BlockSpec((tr, C), lambda i: (g0 + i, 0))],
        out_specs=[sp] * 4, out_shape=[jax.ShapeDtypeStruct((R, C), F32)] * 4,
        compiler_params=_cparams(("arbitrary",)), name=name)(w, m, v, g_arr)


def _place():
    x, y, c = lax.axis_index("x"), lax.axis_index("y"), lax.axis_index("c")
    chips = [(1 - x, y), (x, 1 - y), (1 - x, 1 - y)]
    return x, y, c, chips


ANY = pl.BlockSpec(memory_space=pl.ANY)


def allgather_weights(bufs, *, name):
    n = len(bufs)

    def body(*refs):
        o_refs = refs[n:2 * n]
        send_sems, recv_sems, fsend_sems, frecv_sems = refs[2 * n:]
        x, y, c, chips = _place()
        k_me = 2 * x + y
        sib = (x, y, 1 - c)
        halves = [b.shape[1] // 2 for b in bufs]

        def half(a, cc):
            return pl.ds(pl.multiple_of(cc * halves[a], 16), halves[a])

        sends = []
        for a in range(n):
            for r, (px, py) in enumerate(chips):
                cp = pltpu.make_async_remote_copy(
                    src_ref=o_refs[a].at[k_me, half(a, c)], dst_ref=o_refs[a].at[k_me, half(a, c)],
                    send_sem=send_sems.at[3 * a + r], recv_sem=recv_sems.at[3 * a + r],
                    device_id=(px, py, c), device_id_type=MESH)
                cp.start()
                sends.append(cp)
        passed = []
        for a in range(n):
            for r, (px, py) in enumerate(chips):
                win = o_refs[a].at[2 * px + py, half(a, c)]
                pltpu.make_async_remote_copy(
                    src_ref=win, dst_ref=win, send_sem=send_sems.at[3 * a + r], recv_sem=recv_sems.at[3 * a + r],
                    device_id=(px, py, c), device_id_type=MESH).wait_recv()
                cp = pltpu.make_async_remote_copy(
                    src_ref=win, dst_ref=win, send_sem=fsend_sems.at[3 * a + r], recv_sem=frecv_sems.at[3 * a + r],
                    device_id=sib, device_id_type=MESH)
                cp.start()
                passed.append(cp)
        for a in range(n):
            for r, (px, py) in enumerate(chips):
                win = o_refs[a].at[2 * px + py, half(a, 1 - c)]
                pltpu.make_async_remote_copy(
                    src_ref=win, dst_ref=win, send_sem=fsend_sems.at[3 * a + r], recv_sem=frecv_sems.at[3 * a + r],
                    device_id=sib, device_id_type=MESH).wait_recv()
        for cp in sends + passed:
            cp.wait_send()

    return pl.pallas_call(
        body, in_specs=[ANY] * n, out_specs=[ANY] * n,
        out_shape=[jax.ShapeDtypeStruct(b.shape, b.dtype) for b in bufs],
        scratch_shapes=[pltpu.SemaphoreType.DMA((3 * n,))] * 4,
        input_output_aliases={a: a for a in range(n)},
        name=name)(*bufs)


def exchange_halves(grads, *, name):
    n = len(grads)

    def body(*refs):
        g_refs, l_refs = refs[:n], refs[n:2 * n]
        send_sems, recv_sems = refs[2 * n:]
        x, y, c, _ = _place()
        cps = []
        for a in range(n):
            h = grads[a].shape[1] // 2
            cp = pltpu.make_async_remote_copy(
                src_ref=g_refs[a].at[:, pl.ds(pl.multiple_of((1 - c) * h, 8), h)], dst_ref=l_refs[a],
                send_sem=send_sems.at[a], recv_sem=recv_sems.at[a], device_id=(x, y, 1 - c), device_id_type=MESH)
            cp.start()
            cps.append(cp)
        for cp in cps:
            cp.wait()

    return pl.pallas_call(
        body, in_specs=[ANY] * n, out_specs=[ANY] * n,
        out_shape=[jax.ShapeDtypeStruct((g.shape[0], g.shape[1] // 2, g.shape[2]), g.dtype) for g in grads],
        scratch_shapes=[pltpu.SemaphoreType.DMA((n,))] * 2,
        name=name)(*grads)


def scatter_to_owners(parts, *, name):
    n = len(parts)

    def body(*refs):
        p_refs, l_refs = refs[:n], refs[n:2 * n]
        send_sems, recv_sems = refs[2 * n:]
        x, y, c, chips = _place()
        cps = []
        for a in range(n):
            for r, (px, py) in enumerate(chips):
                cp = pltpu.make_async_remote_copy(
                    src_ref=p_refs[a].at[2 * px + py], dst_ref=l_refs[a].at[r],
                    send_sem=send_sems.at[3 * a + r], recv_sem=recv_sems.at[3 * a + r],
                    device_id=(px, py, c), device_id_type=MESH)
                cp.start()
                cps.append(cp)
        for cp in cps:
            cp.wait()

    return pl.pallas_call(
        body, in_specs=[ANY] * n, out_specs=[ANY] * n,
        out_shape=[jax.ShapeDtypeStruct((3,) + p.shape[1:], p.dtype) for p in parts],
        scratch_shapes=[pltpu.SemaphoreType.DMA((3 * n,))] * 2,
        name=name)(*parts)


def join_halves(fulls, *, name):
    n = len(fulls)

    def body(*refs):
        o_refs = refs[n:2 * n]
        send_sems, recv_sems = refs[2 * n:]
        x, y, c, _ = _place()
        cps = []
        for a in range(n):
            h = fulls[a].shape[0] // 2
            win = o_refs[a].at[pl.ds(pl.multiple_of(c * h, 8), h)]
            cp = pltpu.make_async_remote_copy(
                src_ref=win, dst_ref=win, send_sem=send_sems.at[a], recv_sem=recv_sems.at[a],
                device_id=(x, y, 1 - c), device_id_type=MESH)
            cp.start()
            cps.append(cp)
        for a in range(n):
            h = fulls[a].shape[0] // 2
            other = o_refs[a].at[pl.ds(pl.multiple_of((1 - c) * h, 8), h)]
            pltpu.make_async_remote_copy(
                src_ref=other, dst_ref=other, send_sem=send_sems.at[a], recv_sem=recv_sems.at[a],
                device_id=(x, y, 1 - c), device_id_type=MESH).wait_recv()
        for cp in cps:
            cp.wait_send()

    return pl.pallas_call(
        body, in_specs=[ANY] * n, out_specs=[ANY] * n,
        out_shape=[jax.ShapeDtypeStruct(f.shape, f.dtype) for f in fulls],
        scratch_shapes=[pltpu.SemaphoreType.DMA((n,))] * 2,
        input_output_aliases={a: a for a in range(n)},
        name=name)(*fulls)


def add_sibling(g, l, c, *, name):
    nb, R, C = g.shape
    h = R // 2
    tr = _pick(h, [256, 128, 64, 32, 16])
    per = h // tr

    def body(c_ref, g_ref, l_ref, o_ref):
        o_ref[...] = (g_ref[...] + l_ref[...]).astype(BF16)

    return pl.pallas_call(
        body,
        grid_spec=pltpu.PrefetchScalarGridSpec(
            num_scalar_prefetch=1, grid=(nb, per),
            in_specs=[pl.BlockSpec((None, tr, C), lambda k, i, c_ref: (k, c_ref[0] * per + i, 0)),
                      pl.BlockSpec((None, tr, C), lambda k, i, c_ref: (k, i, 0))],
            out_specs=pl.BlockSpec((None, tr, C), lambda k, i, c_ref: (k, i, 0))),
        out_shape=jax.ShapeDtypeStruct((nb, h, C), BF16),
        compiler_params=_cparams(("arbitrary", "arbitrary")), name=name)(c, g, l)


def add_chips(part, land, kc, *, name):
    _, H, C = part.shape
    tr = _pick(H, [256, 128, 64, 32, 16])
    per = H // tr

    def body(kc_ref, p_ref, l_ref, o_ref):
        o_ref[...] = ((p_ref[...].astype(F32) + l_ref[0].astype(F32)) + l_ref[1].astype(F32)) + l_ref[2].astype(F32)

    return pl.pallas_call(
        body,
        grid_spec=pltpu.PrefetchScalarGridSpec(
            num_scalar_prefetch=1, grid=(per,),
            in_specs=[pl.BlockSpec((None, tr, C), lambda i, kc_ref: (kc_ref[0], i, 0)),
                      pl.BlockSpec((3, tr, C), lambda i, kc_ref: (0, i, 0))],
            out_specs=pl.BlockSpec((tr, C), lambda i, kc_ref: (kc_ref[1] * per + i, 0))),
        out_shape=jax.ShapeDtypeStruct((2 * H, C), F32),
        compiler_params=_cparams(("arbitrary",)), name=name)(kc, part, land)


def allgather_sum(v, *, name):
    m_per, n = v.shape

    def body(x_ref, out_ref, sum_ref, send_sems, recv_sems, local_sem):
        x, y, c, chips = _place()
        me, sibling = (x, y, c), (x, y, 1 - c)

        def rows(px, py, pc):
            return out_ref.at[pl.ds(pl.multiple_of((4 * px + 2 * py + pc) * m_per, 8), m_per), :]

        def copy(k, block, to, src=None):
            return pltpu.make_async_remote_copy(
                src_ref=rows(*block) if src is None else src, dst_ref=rows(*block),
                send_sem=send_sems.at[k], recv_sem=recv_sems.at[k], device_id=to, device_id_type=MESH)

        mine = pltpu.make_async_copy(x_ref, rows(*me), local_sem)
        mine.start()
        first = [copy(0, me, sibling, src=x_ref)]
        first += [copy(1 + j, me, (*chip, c), src=x_ref) for j, chip in enumerate(chips)]
        for cp in first:
            cp.start()
        passed = [copy(4 + j, (*chip, c), sibling) for j, chip in enumerate(chips)]
        for j, chip in enumerate(chips):
            copy(1 + j, (*chip, c), me).wait_recv()
            passed[j].start()
        copy(0, sibling, me).wait_recv()
        for j, chip in enumerate(chips):
            copy(4 + j, (*chip, 1 - c), me).wait_recv()
        for cp in first + passed:
            cp.wait_send()
        mine.wait()
        acc = out_ref[0:m_per, :]
        for d in range(1, N_DEV):
            acc = acc + out_ref[d * m_per:(d + 1) * m_per, :]
        sum_ref[...] = acc

    vm = pl.BlockSpec(memory_space=pltpu.VMEM)
    return pl.pallas_call(
        body, in_specs=[vm], out_specs=[vm, vm],
        out_shape=[jax.ShapeDtypeStruct((N_DEV * m_per, n), v.dtype), jax.ShapeDtypeStruct((m_per, n), v.dtype)],
        scratch_shapes=[pltpu.SemaphoreType.DMA((7,)), pltpu.SemaphoreType.DMA((7,)), pltpu.SemaphoreType.DMA],
        compiler_params=pltpu.CompilerParams(vmem_limit_bytes=VMEM_LIMIT_BYTES), name=name)(v)


def _pack(arrs):
    cols = []
    for a in arrs:
        f = a.reshape(-1)
        pad = (-f.shape[0]) % 128
        cols.append(jnp.pad(f, (0, pad)).reshape(-1, 128))
    out = jnp.concatenate(cols, axis=0)
    pad = (-out.shape[0]) % 8
    return jnp.pad(out, ((0, pad), (0, 0)))


def _unpack(buf, shapes):
    outs, r = [], 0
    for s in shapes:
        nel = math.prod(s)
        nr = -(-nel // 128)
        outs.append(buf[r:r + nr].reshape(-1)[:nel].reshape(s))
        r += nr
    return outs


GA_CONV_OUT, GA_MIX_OUT, GA_WQ, GA_WO, GA_DOWN, GA_UP, GA_ROWS = 0, 256, 512, 768, 1024, 2048, 3072


def kernel(x, mem, in_norm_g, in_norm_b, w_in, conv_dw, conv_db, conv_norm_g, conv_norm_b, w_conv_out, ssm_log_step, ssm_lambda_re, ssm_lambda_im, ssm_b_re, ssm_b_im, ssm_c_re, ssm_c_im, ssm_d, w_ssm_glu, w_mix_out, ln1_g, ln1_b, xa_wq, xa_wkv, xa_wo, ln2_g, ln2_b, mlp_w_up, mlp_w_down, ln3_g, ln3_b, loss_target, m_in_norm_g, m_in_norm_b, m_w_in, m_conv_dw, m_conv_db, m_conv_norm_g, m_conv_norm_b, m_w_conv_out, m_ssm_log_step, m_ssm_lambda_re, m_ssm_lambda_im, m_ssm_b_re, m_ssm_b_im, m_ssm_c_re, m_ssm_c_im, m_ssm_d, m_w_ssm_glu, m_w_mix_out, m_ln1_g, m_ln1_b, m_xa_wq, m_xa_wkv, m_xa_wo, m_ln2_g, m_ln2_b, m_mlp_w_up, m_mlp_w_down, m_ln3_g, m_ln3_b, v_in_norm_g, v_in_norm_b, v_w_in, v_conv_dw, v_conv_db, v_conv_norm_g, v_conv_norm_b, v_w_conv_out, v_ssm_log_step, v_ssm_lambda_re, v_ssm_lambda_im, v_ssm_b_re, v_ssm_b_im, v_ssm_c_re, v_ssm_c_im, v_ssm_d, v_w_ssm_glu, v_w_mix_out, v_ln1_g, v_ln1_b, v_xa_wq, v_xa_wkv, v_xa_wo, v_ln2_g, v_ln2_b, v_mlp_w_up, v_mlp_w_down, v_ln3_g, v_ln3_b):
    D = D_MODEL
    xs = x[0]
    T = xs.shape[0]
    mems = mem[0]
    NM = mems.shape[0]
    tgt = loss_target[0]
    my_c = lax.axis_index("c")
    k_me = 2 * lax.axis_index("x") + lax.axis_index("y")
    c_arr = jnp.reshape(my_c, (1,)).astype(jnp.int32)
    k_arr = jnp.reshape(k_me, (1,)).astype(jnp.int32)

    sh_a = jnp.concatenate([w_conv_out[0], w_mix_out[0], xa_wq[0], xa_wo[0], mlp_w_down[0], mlp_w_up[0]], axis=0).astype(BF16)
    def own_block(shard):
        buf = jnp.zeros((N_CHIPS,) + shard.shape, shard.dtype)
        return lax.dynamic_update_slice(buf, shard[None], (k_me, 0, 0))

    GA, GIN, GKV, GGLU = allgather_weights(
        [own_block(s) for s in (sh_a, w_in[0].astype(BF16), xa_wkv[0].astype(BF16), w_ssm_glu[0].astype(BF16))],
        name="allgather_weights")
    dw_pad = jnp.pad(conv_dw[0], ((0, CONV_HALO - CONV_K), (0, 0)))
    dw_all, _ = allgather_sum(dw_pad, name="allgather_conv_dw")
    dw_full = jnp.transpose(dw_all.reshape(N_DEV, CONV_HALO, D // N_CHIPS)[::2], (1, 0, 2)).reshape(CONV_HALO, D)

    def w_rowshard(row0):
        return dict(b_spec=((N_CHIPS, 256, D), lambda i, j, k: (0, row0 // 256, 0)), b_view=(D, D), tn=D, tk=D)

    _, h0, h0b = ln_fwd(xs, in_norm_g, in_norm_b, name="ln0_fwd")
    p = mm_nn(h0b, GIN, ((None, D, 1152), lambda i, j, k: (j, 0, 0)), D_IN, tn=1152, tk=D, name="mm_w_in")[0]
    c_pre, actb = conv_fwd(p, dw_full, conv_db, conv_norm_g[0].reshape(1, D), conv_norm_b[0].reshape(1, D), name="conv_fwd")
    ya = mm_nn(actb, GA, N=D, name="mm_conv_out", **w_rowshard(GA_CONV_OUT))[0]

    lstep, lre, lim = ssm_log_step[0], ssm_lambda_re[0], ssm_lambda_im[0]
    bre, bim, cre, cim = ssm_b_re[0], ssm_b_im[0], ssm_c_re[0], ssm_c_im[0]
    (ar, ai, bbr, bbi), disc_vjp = jax.vjp(_ssm_discretise, lstep, lre, lim, bre, bim)
    Br, Bi = _blockdiag_in(bbr), _blockdiag_in(bbi)
    Cr, Ci = _blockdiag_out(cre), _blockdiag_out(cim)
    pw_r, pw_i = _powers(ar.reshape(-1), ai.reshape(-1))
    dvec = ssm_d[0].reshape(1, D_SSM)
    xr, xi, yssm = ssm_fwd(p, Br, Bi, Cr, Ci, pw_r, pw_i, dvec, name="ssm_fwd")
    z = mm_nn(yssm, GGLU, ((None, D_SSM, 512), lambda i, j, k: (j, 0, 0)), 2 * D, tn=512, tk=D_SSM, name="mm_ssm_glu")[0]
    mergedb = merge_fwd(p, ya, z, name="merge_fwd")
    mix = mm_nn(mergedb, GA, N=D, name="mm_mix_out", **w_rowshard(GA_MIX_OUT))[0]
    r1, h1, h1b = ln_fwd(mix, ln1_g[0], ln1_b[0], res=h0, name="ln1_fwd")

    qb = mm_nn(h1b, GA, N=D, out_dtype=BF16, name="mm_wq", **w_rowshard(GA_WQ))[0]
    kv = mm_nn(mems, GKV, ((None, D, 512), lambda i, j, k: (j, 0, 0)), 2 * D, tn=512, tk=D, name="mm_wkv")[0]
    ob = attn_fwd(qb, kv, name="attn_fwd")
    xa = mm_nn(ob, GA, N=D, name="mm_wo", **w_rowshard(GA_WO))[0]
    r2, h2, h2b = ln_fwd(xa, ln2_g[0], ln2_b[0], res=h1, name="ln2_fwd")

    def relu2(acc):
        zr = jnp.maximum(acc, 0.0)
        return acc, zr * zr

    zpre, zzb = mm_nn(h2b, GA, ((None, D, D), lambda i, j, k: (j, GA_UP // D, 0)), D_FF, tn=D, tk=D,
                      out_dtypes=[F32, BF16], epilogue=relu2, name="mm_up")
    ff = mm_nn(zzb, GA, ((None, D, D), lambda i, j, k: (k, GA_DOWN // D, 0)), D, tn=D, tk=D, name="mm_down")[0]
    r3, h3, _ = ln_fwd(ff, ln3_g[0], ln3_b[0], res=h2, name="ln3_fwd")
    dh3, sq = loss_head(h3, tgt, name="loss_head")
    loss = lax.psum(0.5 * sq[0, 0] / D, ("x", "y", "c"))

    ga_shape = jax.ShapeDtypeStruct((N_CHIPS, GA_ROWS, D), F32)
    dr3, dr3b, dg3, db3 = ln_bwd(r3, dh3, ln3_g[0], name="ln3_bwd")
    dzpreb = mm_nt(dr3b, GA, ((None, D, D), lambda i, j, k: (j, GA_DOWN // D, 0)), D_FF, tn=D, tk=D, out_dtype=BF16,
                   extras=(zpre,), epilogue=lambda acc, zp: (acc * (2.0 * jnp.maximum(zp, 0.0)),), name="mm_down_t")[0]
    GAg = mm_tn(zzb, dr3b, tm=D, tn=D, out_spec=((None, D, D), lambda i, j, k: (i, GA_DOWN // D, 0)), out_shape=ga_shape,
                name="mm_down_g")
    GAg = mm_tn(h2b, dzpreb, tm=D, tn=D, out_spec=((None, D, D), lambda i, j, k: (j, GA_UP // D, 0)), out_shape=ga_shape,
                out_buf=GAg, name="mm_up_g")
    dh2 = mm_nt(dzpreb, GA, ((None, D, D), lambda i, j, k: (k, GA_UP // D, 0)), D, tn=D, tk=D,
                extras=(dr3,), epilogue=lambda acc, d: (acc + ALPHA * d,), name="mm_up_t")[0]
    dr2, dr2b, dg2, db2 = ln_bwd(r2, dh2, ln2_g[0], name="ln2_bwd")

    def g_rowshard(row0):
        return dict(tm=D, tn=D, out_spec=((N_CHIPS, 256, D), lambda i, j, k: (0, row0 // 256, 0)), out_shape=ga_shape)

    dob = mm_nt(dr2b, GA, N=D, out_dtype=BF16, name="mm_wo_t", **w_rowshard(GA_WO))[0]
    GAg = mm_tn(ob, dr2b, out_buf=GAg, name="mm_wo_g", **g_rowshard(GA_WO))
    dqb, dkv = attn_bwd(qb, kv, dob, name="attn_bwd")
    GAg = mm_tn(h1b, dqb, out_buf=GAg, name="mm_wq_g", **g_rowshard(GA_WQ))
    GKVg = mm_tn(mems, dkv, tm=D, tn=512, tk=NM, out_spec=((None, D, 512), lambda i, j, k: (j, 0, 0)),
                 out_shape=jax.ShapeDtypeStruct((N_CHIPS, D, 512), F32), name="mm_wkv_g")
    dh1 = mm_nt(dqb, GA, N=D, extras=(dr2,), epilogue=lambda acc, d: (acc + ALPHA * d,), name="mm_wq_t",
                **w_rowshard(GA_WQ))[0]
    dr1, dr1b, dg1, db1 = ln_bwd(r1, dh1, ln1_g[0], name="ln1_bwd")

    dmerged = mm_nt(dr1b, GA, N=D, name="mm_mix_t", **w_rowshard(GA_MIX_OUT))[0]
    GAg = mm_tn(mergedb, dr1b, out_buf=GAg, name="mm_mix_g", **g_rowshard(GA_MIX_OUT))
    dyab, dgab, dgbb, dz1b, dz2b = merge_bwd(dmerged, p, ya, z, name="merge_bwd")
    dzb = jnp.concatenate([dz1b, dz2b], axis=1)
    GGLUg = mm_tn(yssm, dzb, tm=D_SSM, tn=512, out_spec=((None, D_SSM, 512), lambda i, j, k: (j, 0, 0)),
                  out_shape=jax.ShapeDtypeStruct((N_CHIPS, D_SSM, 512), F32), name="mm_glu_g")
    dyssm = mm_nt(dzb, GGLU, ((None, D_SSM, 512), lambda i, j, k: (k, 0, 0)), D_SSM, tn=D_SSM, tk=512, name="mm_glu_t")[0]
    pwc_r, pwc_i = pw_r, -pw_i
    dub, dBr, dBi, dCr, dCi, dar8, dai8, dd8 = ssm_bwd(dyssm, p, xr, xi, Br, Bi, Cr, Ci, pwc_r, pwc_i, dvec, name="ssm_bwd")
    dar = jnp.sum(dar8, axis=0).reshape(SSM_GROUPS, SSM_STATE)
    dai = jnp.sum(dai8, axis=0).reshape(SSM_GROUPS, SSM_STATE)
    g_lstep, g_lre, g_lim, g_bre, g_bim = disc_vjp((dar, dai, _diag_in(dBr), _diag_in(dBi)))
    g_cre, g_cim = _diag_out(dCr), _diag_out(dCi)
    g_d = jnp.sum(dd8, axis=0).reshape(1, D_SSM)

    dact = mm_nt(dyab, GA, N=D, name="mm_conv_out_t", **w_rowshard(GA_CONV_OUT))[0]
    GAg = mm_tn(actb, dyab, out_buf=GAg, name="mm_conv_out_g", **g_rowshard(GA_CONV_OUT))
    dc, dng, dnb, ddb = conv_bwd_norm(dact, c_pre, conv_norm_g[0].reshape(1, D), conv_norm_b[0].reshape(1, D), name="conv_bwd_norm")
    dvgb, ddw = conv_bwd_taps(dc, p, dw_full, name="conv_bwd_taps")
    dpb = jnp.concatenate([dvgb, dub, dgab, dgbb], axis=1)
    GINg = mm_tn(h0b, dpb, tm=D, tn=1152, out_spec=((None, D, 1152), lambda i, j, k: (j, 0, 0)),
                 out_shape=jax.ShapeDtypeStruct((N_CHIPS, D, 1152), F32), name="mm_w_in_g")
    dh0 = mm_nt(dpb, GIN, ((None, 512, 1152), lambda i, j, k: (k, j, 0)), D, tn=512, tk=1152,
                extras=(dr1,), epilogue=lambda acc, d: (acc + ALPHA * d,), name="mm_w_in_t")[0]
    gx, _, dg0, db0 = ln_bwd(xs, dh0, in_norm_g, name="ln0_bwd")

    big = [GAg, GINg, GKVg, GGLUg]
    tags = ["a", "in", "kv", "glu"]
    lands = exchange_halves(big, name="rs_exchange_halves")
    parts = [add_sibling(g, l, c_arr, name="rs_add_sibling_" + t) for g, l, t in zip(big, lands, tags)]
    lands2 = scatter_to_owners(parts, name="rs_scatter_to_owners")
    kc_arr = jnp.concatenate([k_arr, c_arr])
    halves = [add_chips(pt, l2, kc_arr, name="rs_add_chips_" + t) for pt, l2, t in zip(parts, lands2, tags)]
    gA, gIN, gKV, gGLU = join_halves(halves, name="rs_join_halves")

    small_names = ["in_norm_g", "in_norm_b", "conv_db", "conv_norm_g", "conv_norm_b", "ssm_log_step", "ssm_lambda_re",
                   "ssm_lambda_im", "ssm_b_re", "ssm_b_im", "ssm_c_re", "ssm_c_im", "ssm_d", "ln1_g", "ln1_b",
                   "ln2_g", "ln2_b", "ln3_g", "ln3_b"]
    small_w = [in_norm_g, in_norm_b, conv_db, conv_norm_g, conv_norm_b, ssm_log_step, ssm_lambda_re, ssm_lambda_im,
               ssm_b_re, ssm_b_im, ssm_c_re, ssm_c_im, ssm_d, ln1_g, ln1_b, ln2_g, ln2_b, ln3_g, ln3_b]
    small_m = [m_in_norm_g, m_in_norm_b, m_conv_db, m_conv_norm_g, m_conv_norm_b, m_ssm_log_step, m_ssm_lambda_re,
               m_ssm_lambda_im, m_ssm_b_re, m_ssm_b_im, m_ssm_c_re, m_ssm_c_im, m_ssm_d, m_ln1_g, m_ln1_b, m_ln2_g,
               m_ln2_b, m_ln3_g, m_ln3_b]
    small_v = [v_in_norm_g, v_in_norm_b, v_conv_db, v_conv_norm_g, v_conv_norm_b, v_ssm_log_step, v_ssm_lambda_re,
               v_ssm_lambda_im, v_ssm_b_re, v_ssm_b_im, v_ssm_c_re, v_ssm_c_im, v_ssm_d, v_ln1_g, v_ln1_b, v_ln2_g,
               v_ln2_b, v_ln3_g, v_ln3_b]
    small_g = [dg0, db0, ddb, dng, dnb, g_lstep, g_lre, g_lim, g_bre, g_bim, g_cre, g_cim, g_d, dg1, db1, dg2, db2, dg3, db3]
    small_shapes = [w.shape for w in small_w]
    n_small_rows = _pack(small_w).shape[0]
    packed_g = _pack(small_g + [ddw])
    _, summed = allgather_sum(packed_g, name="allreduce_small")
    small_rows = sum(-(-math.prod(s) // 128) for s in small_shapes)
    ddw_full = summed[small_rows:small_rows + CONV_HALO * D // 128].reshape(CONV_HALO, D)
    g_dw = lax.dynamic_slice_in_dim(ddw_full, k_me * (D // N_CHIPS), D // N_CHIPS, axis=1)
    gs_packed = jnp.pad(summed[:small_rows], ((0, n_small_rows - small_rows), (0, 0)))

    res = {}

    def upd(nm, w, m, v, g_arr, row0=0):
        shp = w.shape
        w2, m2, v2 = (a.reshape(-1, shp[-1]) for a in (w, m, v))
        outs = adamw(w2, m2, v2, g_arr, row0, name="adamw_" + nm)
        res[nm] = tuple(o.reshape(shp) for o in outs)

    upd("w_conv_out", w_conv_out, m_w_conv_out, v_w_conv_out, gA, GA_CONV_OUT)
    upd("w_mix_out", w_mix_out, m_w_mix_out, v_w_mix_out, gA, GA_MIX_OUT)
    upd("xa_wq", xa_wq, m_xa_wq, v_xa_wq, gA, GA_WQ)
    upd("xa_wo", xa_wo, m_xa_wo, v_xa_wo, gA, GA_WO)
    upd("mlp_w_down", mlp_w_down, m_mlp_w_down, v_mlp_w_down, gA, GA_DOWN)
    upd("mlp_w_up", mlp_w_up, m_mlp_w_up, v_mlp_w_up, gA, GA_UP)
    upd("w_in", w_in, m_w_in, v_w_in, gIN)
    upd("xa_wkv", xa_wkv, m_xa_wkv, v_xa_wkv, gKV)
    upd("w_ssm_glu", w_ssm_glu, m_w_ssm_glu, v_w_ssm_glu, gGLU)
    pad_dw = lambda a: jnp.pad(a[0], ((0, CONV_HALO - CONV_K), (0, 0)))
    dw_outs = adamw(pad_dw(conv_dw), pad_dw(m_conv_dw), pad_dw(v_conv_dw), g_dw, 0, name="adamw_conv_dw")
    res["conv_dw"] = tuple(o[:CONV_K][None] for o in dw_outs)
    sm_outs = adamw(_pack(small_w), _pack(small_m), _pack(small_v), gs_packed, 0, name="adamw_small")
    sm_un = [_unpack(o, small_shapes) for o in sm_outs]
    for idx, nm in enumerate(small_names):
        res[nm] = tuple(sm_un[q][idx] for q in range(4))

    order = ["in_norm_g", "in_norm_b", "w_in", "conv_dw", "conv_db", "conv_norm_g", "conv_norm_b", "w_conv_out",
             "ssm_log_step", "ssm_lambda_re", "ssm_lambda_im", "ssm_b_re", "ssm_b_im", "ssm_c_re", "ssm_c_im", "ssm_d",
             "w_ssm_glu", "w_mix_out", "ln1_g", "ln1_b", "xa_wq", "xa_wkv", "xa_wo", "ln2_g", "ln2_b", "mlp_w_up",
             "mlp_w_down", "ln3_g", "ln3_b"]
    return (loss, gx[None], *[res[n][0] for n in order], *[res[n][1] for n in order],
            *[res[n][2] for n in order], *[res[n][3] for n in order])
```

```python
import functools
import math

import jax
import jax.numpy as jnp
from jax import lax
from jax.experimental import pallas as pl
from jax.experimental.pallas import tpu as pltpu

F32 = jnp.float32
BF16 = jnp.bfloat16
MESH = pl.DeviceIdType.MESH

D_MODEL = 1024
N_HEADS = 4
HEAD_DIM = D_MODEL // N_HEADS
CONV_K = 31
CONV_HALO = 32
D_SSM = 512
SSM_GROUPS = 32
SSM_GROUP = 16
SSM_STATE = 64
SSM_BLOCKS = 4
SSM_BLOCK_IN = D_SSM // SSM_BLOCKS
SSM_BLOCK_STATE = SSM_GROUPS * SSM_STATE // SSM_BLOCKS
D_FF = 4096
D_IN = 4608
LN_EPS = 1e-5
ALPHA = (2.0 * 1) ** 0.25
N_CHIPS = 4
N_DEV = 8
ADAM_LR, ADAM_B1, ADAM_B2, ADAM_EPS, ADAM_WD, ADAM_STEP = 0.001, 0.9, 0.999, 1e-08, 0.01, 10
VMEM_LIMIT_BYTES = 56 * 1024 * 1024


def _pick(dim, cands):
    for c in cands:
        if dim % c == 0:
            return c
    return dim


def _cparams(sem=None):
    return pltpu.CompilerParams(dimension_semantics=sem, vmem_limit_bytes=VMEM_LIMIT_BYTES)


def _sigmoid(x):
    return 1.0 / (1.0 + jnp.exp(-x))


_DIMS = {"nn": (((1,), (0,)), ((), ())), "nt": (((1,), (1,)), ((), ())), "tn": (((0,), (0,)), ((), ()))}


def matmul(a, b, *, mode, M, N, K, tm, tn, tk, a_spec, b_spec, out_specs, out_shapes, name,
           extras=(), extra_specs=(), epilogue=None, alias_buf=None, b_view=None):
    nk = K // tk
    ne = len(extras)
    no = len(out_shapes)
    na = 0 if alias_buf is None else 1
    dims = _DIMS[mode]

    def body(*refs):
        a_ref, b_ref = refs[0], refs[1]
        e_refs = refs[2:2 + ne]
        o_refs = refs[2 + ne + na:2 + ne + na + no]

        def finish(acc):
            outs = (acc,) if epilogue is None else epilogue(acc, *[r[...] for r in e_refs])
            for o, r in zip(outs, o_refs):
                r[...] = o.astype(r.dtype).reshape(r.shape)

        b_blk = b_ref[...] if b_view is None else b_ref[...].reshape(b_view)
        prod = lax.dot_general(a_ref[...].astype(BF16), b_blk.astype(BF16), dims, preferred_element_type=F32)
        if nk == 1:
            finish(prod)
        else:
            acc_ref = refs[-1]
            k = pl.program_id(2)

            @pl.when(k == 0)
            def _():
                acc_ref[...] = prod

            @pl.when(k > 0)
            def _():
                acc_ref[...] += prod

            @pl.when(k == nk - 1)
            def _():
                finish(acc_ref[...])

    in_specs = [pl.BlockSpec(*a_spec), pl.BlockSpec(*b_spec)] + [pl.BlockSpec(*s) for s in extra_specs]
    ins = [a, b, *extras]
    if alias_buf is not None:
        in_specs.append(pl.BlockSpec(memory_space=pl.ANY))
        ins.append(alias_buf)
    res = pl.pallas_call(
        body,
        grid=(M // tm, N // tn, nk),
        in_specs=in_specs,
        out_specs=[pl.BlockSpec(*s) for s in out_specs],
        out_shape=out_shapes,
        scratch_shapes=[] if nk == 1 else [pltpu.VMEM((tm, tn), F32)],
        input_output_aliases={2 + ne: 0} if alias_buf is not None else {},
        compiler_params=_cparams(("parallel", "parallel", "arbitrary")),
        name=name,
    )(*ins)
    return res


def _mn(tm, tn):
    return ((tm, tn), lambda i, j, k: (i, j))


def mm_nn(a, b_arr, b_spec, N, *, name, tm=None, tn, tk, out_dtype=F32, extras=(), epilogue=None, out_dtypes=None,
          b_view=None):
    M, K = a.shape
    tm = tm or _pick(M, [1024, 512, 256, 128])
    dts = out_dtypes or [out_dtype]
    return matmul(a, b_arr, mode="nn", M=M, N=N, K=K, tm=tm, tn=tn, tk=tk,
                  a_spec=((tm, tk), lambda i, j, k: (i, k)), b_spec=b_spec, b_view=b_view,
                  out_specs=[_mn(tm, tn)] * len(dts), out_shapes=[jax.ShapeDtypeStruct((M, N), d) for d in dts],
                  extras=extras, extra_specs=[_mn(tm, tn)] * len(extras), epilogue=epilogue, name=name)


def mm_nt(a, b_arr, b_spec, N, *, name, tm=None, tn, tk, out_dtype=F32, extras=(), epilogue=None, out_dtypes=None,
          b_view=None):
    M, K = a.shape
    tm = tm or _pick(M, [1024, 512, 256, 128])
    dts = out_dtypes or [out_dtype]
    return matmul(a, b_arr, mode="nt", M=M, N=N, K=K, tm=tm, tn=tn, tk=tk,
                  a_spec=((tm, tk), lambda i, j, k: (i, k)), b_spec=b_spec, b_view=b_view,
                  out_specs=[_mn(tm, tn)] * len(dts), out_shapes=[jax.ShapeDtypeStruct((M, N), d) for d in dts],
                  extras=extras, extra_specs=[_mn(tm, tn)] * len(extras), epilogue=epilogue, name=name)


def mm_tn(a, b, *, name, tm, tn, tk=None, out_spec, out_shape, out_buf=None):
    K, M = a.shape
    N = b.shape[1]
    tk = tk or _pick(K, [1024, 512, 256, 128])
    return matmul(a, b, mode="tn", M=M, N=N, K=K, tm=tm, tn=tn, tk=tk,
                  a_spec=((tk, tm), lambda i, j, k: (k, i)), b_spec=((tk, tn), lambda i, j, k: (k, j)),
                  out_specs=[out_spec], out_shapes=[out_shape], alias_buf=out_buf, name=name)[0]


def _rows(tc, w, cb=0):
    return pl.BlockSpec((tc, w), lambda i: (i, cb))


def _const(shape):
    return pl.BlockSpec(shape, lambda i: tuple([0] * len(shape)))


def _ln_stats(r):
    mu = jnp.mean(r, axis=-1, keepdims=True)
    xc = r - mu
    var = jnp.mean(xc * xc, axis=-1, keepdims=True)
    rstd = lax.rsqrt(var + LN_EPS)
    return xc * rstd, rstd


def _rowsum8(v):
    tc, w = v.shape
    return jnp.sum(v.reshape(tc // 8, 8, w), axis=0)


def ln_fwd(x, g, b, *, name, res=None):
    T, D = x.shape
    tc = _pick(T, [512, 256, 128])
    has_res = res is not None

    def body(*refs):
        if has_res:
            x_ref, res_ref, g_ref, b_ref, r_ref, h_ref, hb_ref = refs
            r = ALPHA * res_ref[...] + x_ref[...]
            r_ref[...] = r
        else:
            x_ref, g_ref, b_ref, h_ref, hb_ref = refs
            r = x_ref[...]
        xhat, _ = _ln_stats(r)
        y = xhat * g_ref[...] + b_ref[...]
        h_ref[...] = y
        hb_ref[...] = y.astype(BF16)

    ins = [x] + ([res] if has_res else []) + [g.reshape(1, D), b.reshape(1, D)]
    in_specs = [_rows(tc, D)] * (2 if has_res else 1) + [_const((1, D))] * 2
    n_out = 3 if has_res else 2
    outs = pl.pallas_call(
        body, grid=(T // tc,), in_specs=in_specs, out_specs=[_rows(tc, D)] * n_out,
        out_shape=[jax.ShapeDtypeStruct((T, D), F32)] * (n_out - 1) + [jax.ShapeDtypeStruct((T, D), BF16)],
        compiler_params=_cparams(("arbitrary",)), name=name)(*ins)
    if has_res:
        return outs
    return (x,) + tuple(outs)


def ln_bwd(r, dy, g, *, name):
    T, D = r.shape
    tc = _pick(T, [512, 256, 128])
    nt = T // tc

    def body(r_ref, dy_ref, g_ref, dr_ref, drb_ref, dg_ref, db_ref, accg, accb):
        i = pl.program_id(0)

        @pl.when(i == 0)
        def _():
            accg[...] = jnp.zeros_like(accg)
            accb[...] = jnp.zeros_like(accb)

        xhat, rstd = _ln_stats(r_ref[...])
        dy = dy_ref[...]
        dxh = dy * g_ref[...]
        m1 = jnp.mean(dxh, axis=-1, keepdims=True)
        m2 = jnp.mean(dxh * xhat, axis=-1, keepdims=True)
        dr = rstd * (dxh - m1 - xhat * m2)
        dr_ref[...] = dr
        drb_ref[...] = dr.astype(BF16)
        accg[...] += _rowsum8(dy * xhat)
        accb[...] += _rowsum8(dy)

        @pl.when(i == nt - 1)
        def _():
            dg_ref[...] = jnp.sum(accg[...], axis=0, keepdims=True)
            db_ref[...] = jnp.sum(accb[...], axis=0, keepdims=True)

    return pl.pallas_call(
        body, grid=(nt,), in_specs=[_rows(tc, D), _rows(tc, D), _const((1, D))],
        out_specs=[_rows(tc, D), _rows(tc, D), _const((1, D)), _const((1, D))],
        out_shape=[jax.ShapeDtypeStruct((T, D), F32), jax.ShapeDtypeStruct((T, D), BF16),
                   jax.ShapeDtypeStruct((1, D), F32), jax.ShapeDtypeStruct((1, D), F32)],
        scratch_shapes=[pltpu.VMEM((8, D), F32), pltpu.VMEM((8, D), F32)],
        compiler_params=_cparams(("arbitrary",)), name=name)(r, dy, g.reshape(1, D))


def loss_head(y, target, *, name):
    T, D = y.shape
    tc = _pick(T, [512, 256, 128])
    nt = T // tc

    def body(y_ref, t_ref, dy_ref, loss_ref, acc):
        i = pl.program_id(0)

        @pl.when(i == 0)
        def _():
            acc[...] = jnp.zeros_like(acc)

        e = y_ref[...] - t_ref[...]
        dy_ref[...] = e * (1.0 / D)
        acc[...] += _rowsum8(e * e)

        @pl.when(i == nt - 1)
        def _():
            s = jnp.sum(jnp.sum(acc[...], axis=0, keepdims=True), axis=1, keepdims=True)
            loss_ref[...] = jnp.broadcast_to(s, (1, 128))

    return pl.pallas_call(
        body, grid=(nt,), in_specs=[_rows(tc, D), _rows(tc, D)],
        out_specs=[_rows(tc, D), _const((1, 128))],
        out_shape=[jax.ShapeDtypeStruct((T, D), F32), jax.ShapeDtypeStruct((1, 128), F32)],
        scratch_shapes=[pltpu.VMEM((8, D), F32)],
        compiler_params=_cparams(("arbitrary",)), name=name)(y, target)


def _halo_prev(tc):
    per = tc // CONV_HALO
    return lambda i: jnp.maximum(i * per - 1, 0)


CONV_ROWS = 32
CONV_TAP_GROUP = 4


def _fill_shifts(S, nrows):
    for b in range(1, 8):
        S[b, 0:nrows - 8, :] = S[0, b:b + nrows - 8, :]


def _tap_sum(S, w_ref, offs, r0, nrows):
    acc = None
    for k, o in enumerate(offs):
        a, b = divmod(o, 8)
        term = w_ref[k:k + 1, :] * S[b, pl.ds(pl.multiple_of(r0 + 8 * a, 8), nrows), :]
        acc = term if acc is None else acc + term
    return acc


def conv_fwd(p, dw, db, ng, nb, *, name):
    T = p.shape[0]
    D = D_MODEL
    tc = _pick(T, [256, 128])
    prev = _halo_prev(tc)
    off = CONV_HALO - (CONV_K - 1)
    offs = [off + k for k in range(CONV_K)]

    def body(val_ref, gate_ref, valp_ref, gatep_ref, dw_ref, db_ref, ng_ref, nb_ref, c_ref, act_ref, S):
        i = pl.program_id(0)
        u_prev = valp_ref[...] * _sigmoid(gatep_ref[...])
        S[0, 0:CONV_HALO, :] = jnp.where(i > 0, u_prev, 0.0)
        S[0, CONV_HALO:CONV_HALO + tc, :] = val_ref[...] * _sigmoid(gate_ref[...])
        _fill_shifts(S, CONV_HALO + tc)

        def rows(j, carry):
            r0 = pl.multiple_of(j * CONV_ROWS, CONV_ROWS)
            c_ref[pl.ds(r0, CONV_ROWS), :] = _tap_sum(S, dw_ref, offs, r0, CONV_ROWS) + db_ref[...]
            return carry

        lax.fori_loop(0, tc // CONV_ROWS, rows, 0)
        c = c_ref[...]
        xhat, _ = _ln_stats(c)
        cn = xhat * ng_ref[...] + nb_ref[...]
        act_ref[...] = (cn * _sigmoid(cn)).astype(BF16)

    return pl.pallas_call(
        body, grid=(T // tc,),
        in_specs=[_rows(tc, D, 0), _rows(tc, D, 1),
                  pl.BlockSpec((CONV_HALO, D), lambda i: (prev(i), 0)), pl.BlockSpec((CONV_HALO, D), lambda i: (prev(i), 1)),
                  _const((CONV_HALO, D)), _const((1, D)), _const((1, D)), _const((1, D))],
        out_specs=[_rows(tc, D), _rows(tc, D)],
        out_shape=[jax.ShapeDtypeStruct((T, D), F32), jax.ShapeDtypeStruct((T, D), BF16)],
        scratch_shapes=[pltpu.VMEM((8, CONV_HALO + tc, D), F32)],
        compiler_params=_cparams(("arbitrary",)), name=name)(p, p, p, p, dw, db, ng, nb)


def conv_bwd_norm(dact, c_pre, ng, nb, *, name):
    T, D = c_pre.shape
    tc = _pick(T, [512, 256, 128])
    nt = T // tc

    def body(da_ref, c_ref, ng_ref, nb_ref, dc_ref, dng_ref, dnb_ref, ddb_ref, accg, accb, accd):
        i = pl.program_id(0)

        @pl.when(i == 0)
        def _():
            accg[...] = jnp.zeros_like(accg)
            accb[...] = jnp.zeros_like(accb)
            accd[...] = jnp.zeros_like(accd)

        xhat, rstd = _ln_stats(c_ref[...])
        cn = xhat * ng_ref[...] + nb_ref[...]
        s = _sigmoid(cn)
        dcn = da_ref[...] * (s * (1.0 + cn * (1.0 - s)))
        dxh = dcn * ng_ref[...]
        m1 = jnp.mean(dxh, axis=-1, keepdims=True)
        m2 = jnp.mean(dxh * xhat, axis=-1, keepdims=True)
        dc = rstd * (dxh - m1 - xhat * m2)
        dc_ref[...] = dc
        accg[...] += _rowsum8(dcn * xhat)
        accb[...] += _rowsum8(dcn)
        accd[...] += _rowsum8(dc)

        @pl.when(i == nt - 1)
        def _():
            dng_ref[...] = jnp.sum(accg[...], axis=0, keepdims=True)
            dnb_ref[...] = jnp.sum(accb[...], axis=0, keepdims=True)
            ddb_ref[...] = jnp.sum(accd[...], axis=0, keepdims=True)

    vec = jax.ShapeDtypeStruct((1, D), F32)
    return pl.pallas_call(
        body, grid=(nt,), in_specs=[_rows(tc, D), _rows(tc, D), _const((1, D)), _const((1, D))],
        out_specs=[_rows(tc, D), _const((1, D)), _const((1, D)), _const((1, D))],
        out_shape=[jax.ShapeDtypeStruct((T, D), F32), vec, vec, vec],
        scratch_shapes=[pltpu.VMEM((8, D), F32)] * 3,
        compiler_params=_cparams(("arbitrary",)), name=name)(dact, c_pre, ng, nb)


def conv_bwd_taps(dc, p, dw, *, name):
    T, D = dc.shape
    tc = _pick(T, [256, 128])
    nt = T // tc
    per = tc // CONV_HALO
    prev = _halo_prev(tc)
    last_halo = T // CONV_HALO - 1
    nxt = lambda i: jnp.minimum((i + 1) * per, last_halo)
    off = CONV_HALO - (CONV_K - 1)

    def body(dc_ref, dcn_ref, val_ref, gate_ref, valp_ref, gatep_ref, dw_ref, dvg_ref, ddw_ref, ext_u, ext_d, acc):
        i = pl.program_id(0)

        @pl.when(i == 0)
        def _():
            acc[...] = jnp.zeros_like(acc)

        u_prev = valp_ref[...] * _sigmoid(gatep_ref[...])
        ext_u[0, 0:CONV_HALO, :] = jnp.where(i > 0, u_prev, 0.0)
        ext_u[0, CONV_HALO:CONV_HALO + tc, :] = val_ref[...] * _sigmoid(gate_ref[...])
        ext_d[0, 0:tc, :] = dc_ref[...]
        ext_d[0, tc:tc + CONV_HALO, :] = jnp.where(i < nt - 1, dcn_ref[...], 0.0)
        _fill_shifts(ext_u, CONV_HALO + tc)
        _fill_shifts(ext_d, CONV_HALO + tc)

        def rows(j, carry):
            r0 = pl.multiple_of(j * CONV_ROWS, CONV_ROWS)
            sl = pl.ds(r0, CONV_ROWS)
            du = _tap_sum(ext_d, dw_ref, [CONV_K - 1 - k for k in range(CONV_K)], r0, CONV_ROWS)
            sg = _sigmoid(gate_ref[sl, :])
            dvg_ref[sl, 0:D] = (du * sg).astype(BF16)
            dvg_ref[sl, D:2 * D] = (du * val_ref[sl, :] * sg * (1.0 - sg)).astype(BF16)
            return carry

        lax.fori_loop(0, tc // CONV_ROWS, rows, 0)

        for k0 in range(0, CONV_K, CONV_TAP_GROUP):
            ks = list(range(k0, min(k0 + CONV_TAP_GROUP, CONV_K)))

            def taps(j, accs, ks=ks):
                r0 = pl.multiple_of(j * 8, 8)
                dct = dc_ref[pl.ds(r0, 8), :]
                out = []
                for k, a_k in zip(ks, accs):
                    a, b = divmod(off + k, 8)
                    out.append(a_k + dct * ext_u[b, pl.ds(pl.multiple_of(r0 + 8 * a, 8), 8), :])
                return tuple(out)

            accs = lax.fori_loop(0, tc // 8, taps, tuple(jnp.zeros((8, D), F32) for _ in ks))
            for k, a_k in zip(ks, accs):
                acc[k] += a_k

        @pl.when(i == nt - 1)
        def _():
            ddw_ref[...] = jnp.zeros_like(ddw_ref)
            for k in range(CONV_K):
                ddw_ref[k:k + 1, :] = jnp.sum(acc[k], axis=0, keepdims=True)

    return pl.pallas_call(
        body, grid=(nt,),
        in_specs=[_rows(tc, D), pl.BlockSpec((CONV_HALO, D), lambda i: (nxt(i), 0)),
                  _rows(tc, D, 0), _rows(tc, D, 1),
                  pl.BlockSpec((CONV_HALO, D), lambda i: (prev(i), 0)), pl.BlockSpec((CONV_HALO, D), lambda i: (prev(i), 1)),
                  _const((CONV_HALO, D))],
        out_specs=[_rows(tc, 2 * D), _const((CONV_HALO, D))],
        out_shape=[jax.ShapeDtypeStruct((T, 2 * D), BF16), jax.ShapeDtypeStruct((CONV_HALO, D), F32)],
        scratch_shapes=[pltpu.VMEM((8, CONV_HALO + tc, D), F32), pltpu.VMEM((8, CONV_HALO + tc, D), F32),
                        pltpu.VMEM((CONV_K, 8, D), F32)],
        compiler_params=_cparams(("arbitrary",)), name=name)(dc, dc, p, p, p, p, dw)


GATE_A0 = (2 * D_MODEL + D_SSM) // 512
GATE_B0 = GATE_A0 + 2


def merge_fwd(p, ya, z, *, name):
    T = p.shape[0]
    D = D_MODEL
    tc = _pick(T, [512, 256, 128])
    W = 512

    def body(ga_ref, gb_ref, ya_ref, z1_ref, z2_ref, o_ref):
        yb = z1_ref[...] * _sigmoid(z2_ref[...])
        o_ref[...] = (_sigmoid(ga_ref[...]) * ya_ref[...] + _sigmoid(gb_ref[...]) * yb).astype(BF16)

    return pl.pallas_call(
        body, grid=(T // tc, D // W),
        in_specs=[pl.BlockSpec((tc, W), lambda i, j: (i, GATE_A0 + j)), pl.BlockSpec((tc, W), lambda i, j: (i, GATE_B0 + j)),
                  pl.BlockSpec((tc, W), lambda i, j: (i, j)), pl.BlockSpec((tc, W), lambda i, j: (i, j)),
                  pl.BlockSpec((tc, W), lambda i, j: (i, D // W + j))],
        out_specs=pl.BlockSpec((tc, W), lambda i, j: (i, j)),
        out_shape=jax.ShapeDtypeStruct((T, D), BF16),
        compiler_params=_cparams(("arbitrary", "arbitrary")), name=name)(p, p, ya, z, z)


def merge_bwd(dm, p, ya, z, *, name):
    T = p.shape[0]
    D = D_MODEL
    tc = _pick(T, [512, 256, 128])
    W = 512
    nb = D // W

    def body(dm_ref, ga_ref, gb_ref, ya_ref, z1_ref, z2_ref, dya_ref, dga_ref, dgb_ref, dz1_ref, dz2_ref):
        dm = dm_ref[...]
        sa = _sigmoid(ga_ref[...])
        sb = _sigmoid(gb_ref[...])
        s2 = _sigmoid(z2_ref[...])
        z1 = z1_ref[...]
        yb = z1 * s2
        dya_ref[...] = (dm * sa).astype(BF16)
        dga_ref[...] = (dm * ya_ref[...] * sa * (1.0 - sa)).astype(BF16)
        dgb_ref[...] = (dm * yb * sb * (1.0 - sb)).astype(BF16)
        dyb = dm * sb
        dz1_ref[...] = (dyb * s2).astype(BF16)
        dz2_ref[...] = (dyb * z1 * s2 * (1.0 - s2)).astype(BF16)

    blk = lambda off: pl.BlockSpec((tc, W), lambda i, j: (i, off + j))
    dya, dga, dgb, dz1, dz2 = pl.pallas_call(
        body, grid=(T // tc, nb),
        in_specs=[blk(0), blk(GATE_A0), blk(GATE_B0), blk(0), blk(0), blk(nb)],
        out_specs=[blk(0)] * 5,
        out_shape=[jax.ShapeDtypeStruct((T, D), BF16)] * 5,
        compiler_params=_cparams(("arbitrary", "arbitrary")), name=name)(dm, p, p, ya, z, z)
    return dya, dga, dgb, dz1, dz2


def _scan_block(src_r, src_i, dst_r, dst_i, car_r, car_i, pw_r, pw_i, cw_r, cw_i, ntiles, reverse, extra=None):
    W = src_r.shape[1]
    rows = lax.broadcasted_iota(jnp.int32, (8, W), 0)
    steps = []
    for d, pr in ((1, 0), (2, 1), (4, 3)):
        steps.append((d, jnp.broadcast_to(pw_r[pr:pr + 1, :], (8, W)), jnp.broadcast_to(pw_i[pr:pr + 1, :], (8, W))))
    cw_r, cw_i = cw_r[...], cw_i[...]

    def tile(jj, carry):
        j = ntiles - 1 - jj if reverse else jj
        sl = pl.ds(pl.multiple_of(j * 8, 8), 8)
        xr, xi = src_r[sl, :], src_i[sl, :]
        for d, lr, li in steps:
            if reverse:
                sr = jnp.where(rows < 8 - d, pltpu.roll(xr, 8 - d, 0), 0.0)
                si = jnp.where(rows < 8 - d, pltpu.roll(xi, 8 - d, 0), 0.0)
            else:
                sr = jnp.where(rows >= d, pltpu.roll(xr, d, 0), 0.0)
                si = jnp.where(rows >= d, pltpu.roll(xi, d, 0), 0.0)
            xr, xi = xr + lr * sr - li * si, xi + lr * si + li * sr
        cr, ci = car_r[...], car_i[...]
        xr, xi = xr + cw_r * cr - cw_i * ci, xi + cw_r * ci + cw_i * cr
        dst_r[sl, :] = xr
        dst_i[sl, :] = xi
        edge = 0 if reverse else 7
        car_r[...] = jnp.broadcast_to(xr[edge:edge + 1, :], (8, W))
        car_i[...] = jnp.broadcast_to(xi[edge:edge + 1, :], (8, W))
        if extra is not None:
            carry = extra(j, xr, xi, carry)
        return carry

    return tile


def ssm_fwd(p, Br, Bi, Cr, Ci, pw_r, pw_i, dvec, *, name):
    T = p.shape[0]
    tt = _pick(T, [256, 128])
    nt = T // tt
    WI, WS = SSM_BLOCK_IN, SSM_BLOCK_STATE
    u0 = 2 * D_MODEL // WI

    def body(u_ref, br_ref, bi_ref, cr_ref, ci_ref, pwr_ref, pwi_ref, d_ref, xr_ref, xi_ref, y_ref, bur, bui, car_r, car_i):
        i = pl.program_id(1)

        @pl.when(i == 0)
        def _():
            car_r[...] = jnp.zeros_like(car_r)
            car_i[...] = jnp.zeros_like(car_i)

        u = u_ref[...]
        ub = u.astype(BF16)
        bur[...] = jnp.dot(ub, br_ref[...].astype(BF16), preferred_element_type=F32)
        bui[...] = jnp.dot(ub, bi_ref[...].astype(BF16), preferred_element_type=F32)
        tile = _scan_block(bur, bui, xr_ref, xi_ref, car_r, car_i, pwr_ref, pwi_ref, pwr_ref, pwi_ref, tt // 8, False)
        lax.fori_loop(0, tt // 8, tile, 0)
        y = (jnp.dot(xr_ref[...].astype(BF16), cr_ref[...].astype(BF16), preferred_element_type=F32)
             - jnp.dot(xi_ref[...].astype(BF16), ci_ref[...].astype(BF16), preferred_element_type=F32)
             + d_ref[...] * u)
        y_ref[...] = y.astype(BF16)

    return pl.pallas_call(
        body, grid=(SSM_BLOCKS, nt),
        in_specs=[pl.BlockSpec((tt, WI), lambda b, i: (i, u0 + b)),
                  pl.BlockSpec((None, WI, WS), lambda b, i: (b, 0, 0)), pl.BlockSpec((None, WI, WS), lambda b, i: (b, 0, 0)),
                  pl.BlockSpec((None, WS, WI), lambda b, i: (b, 0, 0)), pl.BlockSpec((None, WS, WI), lambda b, i: (b, 0, 0)),
                  pl.BlockSpec((8, WS), lambda b, i: (0, b)), pl.BlockSpec((8, WS), lambda b, i: (0, b)),
                  pl.BlockSpec((1, WI), lambda b, i: (0, b))],
        out_specs=[pl.BlockSpec((tt, WS), lambda b, i: (i, b)), pl.BlockSpec((tt, WS), lambda b, i: (i, b)),
                   pl.BlockSpec((tt, WI), lambda b, i: (i, b))],
        out_shape=[jax.ShapeDtypeStruct((T, SSM_BLOCKS * WS), F32)] * 2 + [jax.ShapeDtypeStruct((T, D_SSM), BF16)],
        scratch_shapes=[pltpu.VMEM((tt, WS), F32), pltpu.VMEM((tt, WS), F32), pltpu.VMEM((8, WS), F32), pltpu.VMEM((8, WS), F32)],
        compiler_params=_cparams(("arbitrary", "arbitrary")), name=name)(p, Br, Bi, Cr, Ci, pw_r, pw_i, dvec)


def ssm_bwd(dy, p, xr, xi, Br, Bi, Cr, Ci, pwc_r, pwc_i, dvec, *, name):
    T = p.shape[0]
    tt = _pick(T, [256, 128])
    nt = T // tt
    WI, WS = SSM_BLOCK_IN, SSM_BLOCK_STATE
    u0 = 2 * D_MODEL // WI
    tb = lambda i: nt - 1 - i
    xprev = lambda i: jnp.maximum(tb(i) * (tt // 8) - 1, 0)
    tn_dims = _DIMS["tn"]
    nt_dims = _DIMS["nt"]

    def body(dy_ref, u_ref, xr_ref, xi_ref, xpr_ref, xpi_ref, br_ref, bi_ref, cr_ref, ci_ref, pwr_ref, pwi_ref,
             cwr_ref, cwi_ref, d_ref,
             du_ref, dbr_ref, dbi_ref, dcr_ref, dci_ref, dar_ref, dai_ref, dd_ref,
             gr, gi, ext_r, ext_i, car_r, car_i):
        i = pl.program_id(1)

        @pl.when(i == 0)
        def _():
            car_r[...] = jnp.zeros_like(car_r)
            car_i[...] = jnp.zeros_like(car_i)
            dbr_ref[...] = jnp.zeros_like(dbr_ref)
            dbi_ref[...] = jnp.zeros_like(dbi_ref)
            dcr_ref[...] = jnp.zeros_like(dcr_ref)
            dci_ref[...] = jnp.zeros_like(dci_ref)
            dar_ref[...] = jnp.zeros_like(dar_ref)
            dai_ref[...] = jnp.zeros_like(dai_ref)
            dd_ref[...] = jnp.zeros_like(dd_ref)

        dy = dy_ref[...]
        dyb = dy.astype(BF16)
        u = u_ref[...]
        ub = u.astype(BF16)
        gr[...] = lax.dot_general(dyb, cr_ref[...].astype(BF16), nt_dims, preferred_element_type=F32)
        gi[...] = -lax.dot_general(dyb, ci_ref[...].astype(BF16), nt_dims, preferred_element_type=F32)
        first = tb(i) == 0
        ext_r[0:8, :] = jnp.where(first, 0.0, xpr_ref[...])
        ext_i[0:8, :] = jnp.where(first, 0.0, xpi_ref[...])
        ext_r[8:8 + tt, :] = xr_ref[...]
        ext_i[8:8 + tt, :] = xi_ref[...]
        rows = lax.broadcasted_iota(jnp.int32, (8, WS), 0)

        def lam_grad(j, g_r, g_i, carry):
            a_r, a_i = carry
            cur = pl.ds(pl.multiple_of(j * 8 + 8, 8), 8)
            prv = pl.ds(pl.multiple_of(j * 8, 8), 8)
            xc_r, xc_i = ext_r[cur, :], ext_i[cur, :]
            xl_r, xl_i = ext_r[prv, :], ext_i[prv, :]
            xp_r = jnp.where(rows == 0, jnp.broadcast_to(xl_r[7:8, :], (8, WS)), pltpu.roll(xc_r, 1, 0))
            xp_i = jnp.where(rows == 0, jnp.broadcast_to(xl_i[7:8, :], (8, WS)), pltpu.roll(xc_i, 1, 0))
            return (a_r + g_r * xp_r + g_i * xp_i, a_i + g_i * xp_r - g_r * xp_i)

        tile = _scan_block(gr, gi, gr, gi, car_r, car_i, pwr_ref, pwi_ref, cwr_ref, cwi_ref, tt // 8, True, extra=lam_grad)
        z8 = jnp.zeros((8, WS), F32)
        a_r, a_i = lax.fori_loop(0, tt // 8, tile, (z8, z8))
        dar_ref[...] += a_r
        dai_ref[...] += a_i
        grb = gr[...].astype(BF16)
        gib = gi[...].astype(BF16)
        dbr_ref[...] += lax.dot_general(ub, grb, tn_dims, preferred_element_type=F32)
        dbi_ref[...] += lax.dot_general(ub, gib, tn_dims, preferred_element_type=F32)
        dcr_ref[...] += lax.dot_general(xr_ref[...].astype(BF16), dyb, tn_dims, preferred_element_type=F32)
        dci_ref[...] -= lax.dot_general(xi_ref[...].astype(BF16), dyb, tn_dims, preferred_element_type=F32)
        du = (lax.dot_general(grb, br_ref[...].astype(BF16), nt_dims, preferred_element_type=F32)
              + lax.dot_general(gib, bi_ref[...].astype(BF16), nt_dims, preferred_element_type=F32)
              + d_ref[...] * dy)
        du_ref[...] = du.astype(BF16)
        dd_ref[...] += _rowsum8(dy * u)

    wspec = lambda shp: pl.BlockSpec((None,) + shp, lambda b, i: (b, 0, 0))
    return pl.pallas_call(
        body, grid=(SSM_BLOCKS, nt),
        in_specs=[pl.BlockSpec((tt, WI), lambda b, i: (tb(i), b)),
                  pl.BlockSpec((tt, WI), lambda b, i: (tb(i), u0 + b)),
                  pl.BlockSpec((tt, WS), lambda b, i: (tb(i), b)), pl.BlockSpec((tt, WS), lambda b, i: (tb(i), b)),
                  pl.BlockSpec((8, WS), lambda b, i: (xprev(i), b)), pl.BlockSpec((8, WS), lambda b, i: (xprev(i), b)),
                  wspec((WI, WS)), wspec((WI, WS)), wspec((WS, WI)), wspec((WS, WI)),
                  pl.BlockSpec((8, WS), lambda b, i: (0, b)), pl.BlockSpec((8, WS), lambda b, i: (0, b)),
                  pl.BlockSpec((8, WS), lambda b, i: (0, b)), pl.BlockSpec((8, WS), lambda b, i: (0, b)),
                  pl.BlockSpec((1, WI), lambda b, i: (0, b))],
        out_specs=[pl.BlockSpec((tt, WI), lambda b, i: (tb(i), b)),
                   wspec((WI, WS)), wspec((WI, WS)), wspec((WS, WI)), wspec((WS, WI)),
                   pl.BlockSpec((8, WS), lambda b, i: (0, b)), pl.BlockSpec((8, WS), lambda b, i: (0, b)),
                   pl.BlockSpec((8, WI), lambda b, i: (0, b))],
        out_shape=[jax.ShapeDtypeStruct((T, D_SSM), BF16),
                   jax.ShapeDtypeStruct((SSM_BLOCKS, WI, WS), F32), jax.ShapeDtypeStruct((SSM_BLOCKS, WI, WS), F32),
                   jax.ShapeDtypeStruct((SSM_BLOCKS, WS, WI), F32), jax.ShapeDtypeStruct((SSM_BLOCKS, WS, WI), F32),
                   jax.ShapeDtypeStruct((8, SSM_BLOCKS * WS), F32), jax.ShapeDtypeStruct((8, SSM_BLOCKS * WS), F32),
                   jax.ShapeDtypeStruct((8, D_SSM), F32)],
        scratch_shapes=[pltpu.VMEM((tt, WS), F32), pltpu.VMEM((tt, WS), F32),
                        pltpu.VMEM((tt + 8, WS), F32), pltpu.VMEM((tt + 8, WS), F32),
                        pltpu.VMEM((8, WS), F32), pltpu.VMEM((8, WS), F32)],
        compiler_params=_cparams(("arbitrary", "arbitrary")), name=name,
    )(dy, p, xr, xi, xr, xi, Br, Bi, Cr, Ci, pwc_r, pwc_i, pwc_r[::-1], pwc_i[::-1], dvec)


def _ssm_discretise(log_step, lam_re, lam_im, b_re, b_im):
    step = jnp.exp(log_step)[:, None]
    mag = jnp.exp(lam_re * step)
    ar = mag * jnp.cos(lam_im * step)
    ai = mag * jnp.sin(lam_im * step)
    den = lam_re * lam_re + lam_im * lam_im
    nr = ar - 1.0
    cr = (nr * lam_re + ai * lam_im) / den
    ci = (ai * lam_re - nr * lam_im) / den
    bbr = cr[..., None] * b_re - ci[..., None] * b_im
    bbi = cr[..., None] * b_im + ci[..., None] * b_re
    return ar, ai, bbr, bbi


def _blockdiag_in(bb):
    t = jnp.transpose(bb, (0, 2, 1)).reshape(SSM_BLOCKS, 8, SSM_GROUP, SSM_STATE)
    eye = jnp.eye(8, dtype=bb.dtype)
    return (t[:, :, :, None, :] * eye[None, :, None, :, None]).reshape(SSM_BLOCKS, SSM_BLOCK_IN, SSM_BLOCK_STATE)


def _blockdiag_out(cc):
    t = jnp.transpose(cc, (0, 2, 1)).reshape(SSM_BLOCKS, 8, SSM_STATE, SSM_GROUP)
    eye = jnp.eye(8, dtype=cc.dtype)
    return (t[:, :, :, None, :] * eye[None, :, None, :, None]).reshape(SSM_BLOCKS, SSM_BLOCK_STATE, SSM_BLOCK_IN)


def _diag_in(d):
    t = d.reshape(SSM_BLOCKS, 8, SSM_GROUP, 8, SSM_STATE)
    t = jnp.einsum("bghgp->bghp", t).reshape(SSM_GROUPS, SSM_GROUP, SSM_STATE)
    return jnp.transpose(t, (0, 2, 1))


def _diag_out(d):
    t = d.reshape(SSM_BLOCKS, 8, SSM_STATE, 8, SSM_GROUP)
    t = jnp.einsum("bgpgh->bgph", t).reshape(SSM_GROUPS, SSM_STATE, SSM_GROUP)
    return jnp.transpose(t, (0, 2, 1))


def _powers(ar, ai):
    rs, is_ = [ar], [ai]
    for _ in range(7):
        r, i = rs[-1], is_[-1]
        rs.append(r * ar - i * ai)
        is_.append(r * ai + i * ar)
    return jnp.stack(rs), jnp.stack(is_)


def attn_fwd(q, kv, *, name):
    T, D = q.shape
    nm = kv.shape[0]
    tq = _pick(T, [512, 256, 128])
    scale = HEAD_DIM ** -0.5

    def body(q_ref, k_ref, v_ref, o_ref):
        for h in range(N_HEADS):
            sl = slice(h * HEAD_DIM, (h + 1) * HEAD_DIM)
            s = lax.dot_general(q_ref[:, sl], k_ref[:, sl].astype(BF16), _DIMS["nt"], preferred_element_type=F32) * scale
            e = jnp.exp(s - jnp.max(s, axis=-1, keepdims=True))
            pr = e / jnp.sum(e, axis=-1, keepdims=True)
            o_ref[:, sl] = jnp.dot(pr.astype(BF16), v_ref[:, sl].astype(BF16), preferred_element_type=F32).astype(BF16)

    return pl.pallas_call(
        body, grid=(T // tq,),
        in_specs=[_rows(tq, D), pl.BlockSpec((nm, D), lambda i: (0, 0)), pl.BlockSpec((nm, D), lambda i: (0, 1))],
        out_specs=_rows(tq, D), out_shape=jax.ShapeDtypeStruct((T, D), BF16),
        compiler_params=_cparams(("arbitrary",)), name=name)(q, kv, kv)


def attn_bwd(q, kv, do, *, name):
    T, D = q.shape
    nm = kv.shape[0]
    tq = _pick(T, [512, 256, 128])
    nt = T // tq
    scale = HEAD_DIM ** -0.5

    def body(q_ref, k_ref, v_ref, do_ref, dq_ref, dkv_ref):
        i = pl.program_id(0)

        @pl.when(i == 0)
        def _():
            dkv_ref[...] = jnp.zeros_like(dkv_ref)

        for h in range(N_HEADS):
            sl = slice(h * HEAD_DIM, (h + 1) * HEAD_DIM)
            slv = slice(D + h * HEAD_DIM, D + (h + 1) * HEAD_DIM)
            qh = q_ref[:, sl]
            kh = k_ref[:, sl].astype(BF16)
            vh = v_ref[:, sl].astype(BF16)
            doh = do_ref[:, sl].astype(BF16)
            s = lax.dot_general(qh, kh, _DIMS["nt"], preferred_element_type=F32) * scale
            e = jnp.exp(s - jnp.max(s, axis=-1, keepdims=True))
            pr = e / jnp.sum(e, axis=-1, keepdims=True)
            dp = lax.dot_general(doh, vh, _DIMS["nt"], preferred_element_type=F32)
            ds = (pr * (dp - jnp.sum(pr * dp, axis=-1, keepdims=True)) * scale).astype(BF16)
            dq_ref[:, sl] = jnp.dot(ds, kh, preferred_element_type=F32).astype(BF16)
            dkv_ref[:, sl] += lax.dot_general(ds, qh, _DIMS["tn"], preferred_element_type=F32)
            dkv_ref[:, slv] += lax.dot_general(pr.astype(BF16), doh, _DIMS["tn"], preferred_element_type=F32)

    return pl.pallas_call(
        body, grid=(nt,),
        in_specs=[_rows(tq, D), pl.BlockSpec((nm, D), lambda i: (0, 0)), pl.BlockSpec((nm, D), lambda i: (0, 1)), _rows(tq, D)],
        out_specs=[_rows(tq, D), _const((nm, 2 * D))],
        out_shape=[jax.ShapeDtypeStruct((T, D), BF16), jax.ShapeDtypeStruct((nm, 2 * D), F32)],
        compiler_params=_cparams(("arbitrary",)), name=name)(q, kv, kv, do)


def _adam_math(w, g, m, v):
    m = ADAM_B1 * m + (1.0 - ADAM_B1) * g
    v = ADAM_B2 * v + (1.0 - ADAM_B2) * (g * g)
    m_hat = m / (1.0 - ADAM_B1 ** ADAM_STEP)
    v_hat = v / (1.0 - ADAM_B2 ** ADAM_STEP)
    delta = -ADAM_LR * (m_hat / (jnp.sqrt(v_hat) + ADAM_EPS) + ADAM_WD * w)
    return delta, m, v


def adamw(w, m, v, g_arr, g_row0, *, name):
    R, C = w.shape
    tr = _pick(R, [256, 128, 64, 32, 16, 8])
    assert g_row0 % tr == 0
    g0 = g_row0 // tr

    def body(w_ref, m_ref, v_ref, g_ref, go_ref, d_ref, mo_ref, vo_ref):
        g = g_ref[...]
        d, mn, vn = _adam_math(w_ref[...], g, m_ref[...], v_ref[...])
        go_ref[...] = g
        d_ref[...] = d
        mo_ref[...] = mn
        vo_ref[...] = vn

    sp = pl.BlockSpec((tr, C), lambda i: (i, 0))
    return pl.pallas_call(
        body, grid=(R // tr,), in_specs=[sp, sp, sp, pl.BlockSpec((tr, C), lambda i: (g0 + i, 0))],
        out_specs=[sp] * 4, out_shape=[jax.ShapeDtypeStruct((R, C), F32)] * 4,
        compiler_params=_cparams(("arbitrary",)), name=name)(w, m, v, g_arr)


def _place():
    x, y, c = lax.axis_index("x"), lax.axis_index("y"), lax.axis_index("c")
    chips = [(1 - x, y), (x, 1 - y), (1 - x, 1 - y)]
    return x, y, c, chips


ANY = pl.BlockSpec(memory_space=pl.ANY)


def allgather_weights(bufs, *, name):
    n = len(bufs)

    def body(*refs):
        o_refs = refs[n:2 * n]
        send_sems, recv_sems, fsend_sems, frecv_sems = refs[2 * n:]
        x, y, c, chips = _place()
        k_me = 2 * x + y
        sib = (x, y, 1 - c)
        halves = [b.shape[1] // 2 for b in bufs]

        def half(a, cc):
            return pl.ds(pl.multiple_of(cc * halves[a], 16), halves[a])

        sends = []
        for a in range(n):
            for r, (px, py) in enumerate(chips):
                cp = pltpu.make_async_remote_copy(
                    src_ref=o_refs[a].at[k_me, half(a, c)], dst_ref=o_refs[a].at[k_me, half(a, c)],
                    send_sem=send_sems.at[3 * a + r], recv_sem=recv_sems.at[3 * a + r],
                    device_id=(px, py, c), device_id_type=MESH)
                cp.start()
                sends.append(cp)
        passed = []
        for a in range(n):
            for r, (px, py) in enumerate(chips):
                win = o_refs[a].at[2 * px + py, half(a, c)]
                pltpu.make_async_remote_copy(
                    src_ref=win, dst_ref=win, send_sem=send_sems.at[3 * a + r], recv_sem=recv_sems.at[3 * a + r],
                    device_id=(px, py, c), device_id_type=MESH).wait_recv()
                cp = pltpu.make_async_remote_copy(
                    src_ref=win, dst_ref=win, send_sem=fsend_sems.at[3 * a + r], recv_sem=frecv_sems.at[3 * a + r],
                    device_id=sib, device_id_type=MESH)
                cp.start()
                passed.append(cp)
        for a in range(n):
            for r, (px, py) in enumerate(chips):
                win = o_refs[a].at[2 * px + py, half(a, 1 - c)]
                pltpu.make_async_remote_copy(
                    src_ref=win, dst_ref=win, send_sem=fsend_sems.at[3 * a + r], recv_sem=frecv_sems.at[3 * a + r],
                    device_id=sib, device_id_type=MESH).wait_recv()
        for cp in sends + passed:
            cp.wait_send()

    return pl.pallas_call(
        body, in_specs=[ANY] * n, out_specs=[ANY] * n,
        out_shape=[jax.ShapeDtypeStruct(b.shape, b.dtype) for b in bufs],
        scratch_shapes=[pltpu.SemaphoreType.DMA((3 * n,))] * 4,
        input_output_aliases={a: a for a in range(n)},
        name=name)(*bufs)


HBM_SPEC = pl.BlockSpec(memory_space=pltpu.HBM)
SEM_SPEC = pl.BlockSpec(memory_space=pltpu.SEMAPHORE)


def _hbm(a):
    return pltpu.with_memory_space_constraint(a, pltpu.HBM)


def gather_start(bufs, pieces, *, name):
    n = len(bufs)
    npc = len(pieces)

    def body(*refs):
        b_refs = refs[:n]
        send_sems, recv_sems = refs[n], refs[n + 1]
        x, y, c, chips = _place()
        k_me = 2 * x + y
        for q, (a, row0, rows) in enumerate(pieces):
            win = b_refs[a].at[k_me, pl.ds(row0, rows)]
            for r, (px, py) in enumerate(chips):
                pltpu.make_async_remote_copy(
                    src_ref=win, dst_ref=win, send_sem=send_sems.at[3 * q + r], recv_sem=recv_sems.at[3 * q + r],
                    device_id=(px, py, c), device_id_type=MESH).start()

    return pl.pallas_call(
        body, in_specs=[HBM_SPEC] * n, out_specs=[SEM_SPEC, SEM_SPEC] + [HBM_SPEC] * n,
        out_shape=[pltpu.SemaphoreType.DMA((3 * npc,)), pltpu.SemaphoreType.DMA((3 * npc,))]
        + [pltpu.HBM(b.shape, b.dtype) for b in bufs],
        input_output_aliases={a: 2 + a for a in range(n)},
        compiler_params=pltpu.CompilerParams(has_side_effects=pltpu.SideEffectType.DATAFLOW_SIDE_EFFECTING),
        name=name)(*[_hbm(b) for b in bufs])


def gather_wait(send_sems, recv_sems, bufs, which, after, *, name):
    n = len(bufs)

    def body(*refs):
        b_refs = refs[:n]
        send_sems, recv_sems = refs[n], refs[n + 1]
        x, y, c, chips = _place()
        k_me = 2 * x + y
        for a, row0, rows, q in which:
            for r, (px, py) in enumerate(chips):
                cp = pltpu.make_async_remote_copy(
                    src_ref=b_refs[a].at[k_me, pl.ds(row0, rows)], dst_ref=b_refs[a].at[2 * px + py, pl.ds(row0, rows)],
                    send_sem=send_sems.at[3 * q + r], recv_sem=recv_sems.at[3 * q + r],
                    device_id=(px, py, c), device_id_type=MESH)
                cp.wait_send()
                cp.wait_recv()

    return pl.pallas_call(
        body, in_specs=[HBM_SPEC] * n + [SEM_SPEC, SEM_SPEC, ANY], out_specs=[HBM_SPEC] * n,
        out_shape=[pltpu.HBM(b.shape, b.dtype) for b in bufs],
        input_output_aliases={a: a for a in range(n)},
        compiler_params=pltpu.CompilerParams(has_side_effects=pltpu.SideEffectType.DATAFLOW_SIDE_EFFECTING),
        name=name)(*bufs, send_sems, recv_sems, after)


def exchange_halves(grads, *, name):
    n = len(grads)

    def body(*refs):
        g_refs, l_refs = refs[:n], refs[n:2 * n]
        send_sems, recv_sems = refs[2 * n:]
        x, y, c, _ = _place()
        cps = []
        for a in range(n):
            h = grads[a].shape[1] // 2
            cp = pltpu.make_async_remote_copy(
                src_ref=g_refs[a].at[:, pl.ds(pl.multiple_of((1 - c) * h, 8), h)], dst_ref=l_refs[a],
                send_sem=send_sems.at[a], recv_sem=recv_sems.at[a], device_id=(x, y, 1 - c), device_id_type=MESH)
            cp.start()
            cps.append(cp)
        for cp in cps:
            cp.wait()

    return pl.pallas_call(
        body, in_specs=[ANY] * n, out_specs=[ANY] * n,
        out_shape=[jax.ShapeDtypeStruct((g.shape[0], g.shape[1] // 2, g.shape[2]), g.dtype) for g in grads],
        scratch_shapes=[pltpu.SemaphoreType.DMA((n,))] * 2,
        name=name)(*grads)


def scatter_to_owners(parts, *, name):
    n = len(parts)

    def body(*refs):
        p_refs, l_refs = refs[:n], refs[n:2 * n]
        send_sems, recv_sems = refs[2 * n:]
        x, y, c, chips = _place()
        cps = []
        for a in range(n):
            for r, (px, py) in enumerate(chips):
                cp = pltpu.make_async_remote_copy(
                    src_ref=p_refs[a].at[2 * px + py], dst_ref=l_refs[a].at[r],
                    send_sem=send_sems.at[3 * a + r], recv_sem=recv_sems.at[3 * a + r],
                    device_id=(px, py, c), device_id_type=MESH)
                cp.start()
                cps.append(cp)
        for cp in cps:
            cp.wait()

    return pl.pallas_call(
        body, in_specs=[ANY] * n, out_specs=[ANY] * n,
        out_shape=[jax.ShapeDtypeStruct((3,) + p.shape[1:], p.dtype) for p in parts],
        scratch_shapes=[pltpu.SemaphoreType.DMA((3 * n,))] * 2,
        name=name)(*parts)


def join_halves(fulls, *, name):
    n = len(fulls)

    def body(*refs):
        o_refs = refs[n:2 * n]
        send_sems, recv_sems = refs[2 * n:]
        x, y, c, _ = _place()
        cps = []
        for a in range(n):
            h = fulls[a].shape[0] // 2
            win = o_refs[a].at[pl.ds(pl.multiple_of(c * h, 8), h)]
            cp = pltpu.make_async_remote_copy(
                src_ref=win, dst_ref=win, send_sem=send_sems.at[a], recv_sem=recv_sems.at[a],
                device_id=(x, y, 1 - c), device_id_type=MESH)
            cp.start()
            cps.append(cp)
        for a in range(n):
            h = fulls[a].shape[0] // 2
            other = o_refs[a].at[pl.ds(pl.multiple_of((1 - c) * h, 8), h)]
            pltpu.make_async_remote_copy(
                src_ref=other, dst_ref=other, send_sem=send_sems.at[a], recv_sem=recv_sems.at[a],
                device_id=(x, y, 1 - c), device_id_type=MESH).wait_recv()
        for cp in cps:
            cp.wait_send()

    return pl.pallas_call(
        body, in_specs=[ANY] * n, out_specs=[ANY] * n,
        out_shape=[jax.ShapeDtypeStruct(f.shape, f.dtype) for f in fulls],
        scratch_shapes=[pltpu.SemaphoreType.DMA((n,))] * 2,
        input_output_aliases={a: a for a in range(n)},
        name=name)(*fulls)


def add_sibling(g, l, c, *, name):
    nb, R, C = g.shape
    h = R // 2
    tr = _pick(h, [256, 128, 64, 32, 16])
    per = h // tr

    def body(c_ref, g_ref, l_ref, o_ref):
        o_ref[...] = (g_ref[...] + l_ref[...]).astype(BF16)

    return pl.pallas_call(
        body,
        grid_spec=pltpu.PrefetchScalarGridSpec(
            num_scalar_prefetch=1, grid=(nb, per),
            in_specs=[pl.BlockSpec((None, tr, C), lambda k, i, c_ref: (k, c_ref[0] * per + i, 0)),
                      pl.BlockSpec((None, tr, C), lambda k, i, c_ref: (k, i, 0))],
            out_specs=pl.BlockSpec((None, tr, C), lambda k, i, c_ref: (k, i, 0))),
        out_shape=jax.ShapeDtypeStruct((nb, h, C), BF16),
        compiler_params=_cparams(("arbitrary", "arbitrary")), name=name)(c, g, l)


def add_chips(part, land, kc, *, name):
    _, H, C = part.shape
    tr = _pick(H, [256, 128, 64, 32, 16])
    per = H // tr

    def body(kc_ref, p_ref, l_ref, o_ref):
        o_ref[...] = ((p_ref[...].astype(F32) + l_ref[0].astype(F32)) + l_ref[1].astype(F32)) + l_ref[2].astype(F32)

    return pl.pallas_call(
        body,
        grid_spec=pltpu.PrefetchScalarGridSpec(
            num_scalar_prefetch=1, grid=(per,),
            in_specs=[pl.BlockSpec((None, tr, C), lambda i, kc_ref: (kc_ref[0], i, 0)),
                      pl.BlockSpec((3, tr, C), lambda i, kc_ref: (0, i, 0))],
            out_specs=pl.BlockSpec((tr, C), lambda i, kc_ref: (kc_ref[1] * per + i, 0))),
        out_shape=jax.ShapeDtypeStruct((2 * H, C), F32),
        compiler_params=_cparams(("arbitrary",)), name=name)(kc, part, land)


def allgather_sum(v, *, name):
    m_per, n = v.shape

    def body(x_ref, out_ref, sum_ref, send_sems, recv_sems, local_sem):
        x, y, c, chips = _place()
        me, sibling = (x, y, c), (x, y, 1 - c)

        def rows(px, py, pc):
            return out_ref.at[pl.ds(pl.multiple_of((4 * px + 2 * py + pc) * m_per, 8), m_per), :]

        def copy(k, block, to, src=None):
            return pltpu.make_async_remote_copy(
                src_ref=rows(*block) if src is None else src, dst_ref=rows(*block),
                send_sem=send_sems.at[k], recv_sem=recv_sems.at[k], device_id=to, device_id_type=MESH)

        mine = pltpu.make_async_copy(x_ref, rows(*me), local_sem)
        mine.start()
        first = [copy(0, me, sibling, src=x_ref)]
        first += [copy(1 + j, me, (*chip, c), src=x_ref) for j, chip in enumerate(chips)]
        for cp in first:
            cp.start()
        passed = [copy(4 + j, (*chip, c), sibling) for j, chip in enumerate(chips)]
        for j, chip in enumerate(chips):
            copy(1 + j, (*chip, c), me).wait_recv()
            passed[j].start()
        copy(0, sibling, me).wait_recv()
        for j, chip in enumerate(chips):
            copy(4 + j, (*chip, 1 - c), me).wait_recv()
        for cp in first + passed:
            cp.wait_send()
        mine.wait()
        acc = out_ref[0:m_per, :]
        for d in range(1, N_DEV):
            acc = acc + out_ref[d * m_per:(d + 1) * m_per, :]
        sum_ref[...] = acc

    vm = pl.BlockSpec(memory_space=pltpu.VMEM)
    return pl.pallas_call(
        body, in_specs=[vm], out_specs=[vm, vm],
        out_shape=[jax.ShapeDtypeStruct((N_DEV * m_per, n), v.dtype), jax.ShapeDtypeStruct((m_per, n), v.dtype)],
        scratch_shapes=[pltpu.SemaphoreType.DMA((7,)), pltpu.SemaphoreType.DMA((7,)), pltpu.SemaphoreType.DMA],
        compiler_params=pltpu.CompilerParams(vmem_limit_bytes=VMEM_LIMIT_BYTES), name=name)(v)


def _pack(arrs):
    cols = []
    for a in arrs:
        f = a.reshape(-1)
        pad = (-f.shape[0]) % 128
        cols.append(jnp.pad(f, (0, pad)).reshape(-1, 128))
    out = jnp.concatenate(cols, axis=0)
    pad = (-out.shape[0]) % 8
    return jnp.pad(out, ((0, pad), (0, 0)))


def _unpack(buf, shapes):
    outs, r = [], 0
    for s in shapes:
        nel = math.prod(s)
        nr = -(-nel // 128)
        outs.append(buf[r:r + nr].reshape(-1)[:nel].reshape(s))
        r += nr
    return outs


GA_CONV_OUT, GA_MIX_OUT, GA_WQ, GA_WO, GA_DOWN, GA_UP, GA_ROWS = 0, 256, 512, 768, 1024, 2048, 3072


def kernel(x, mem, in_norm_g, in_norm_b, w_in, conv_dw, conv_db, conv_norm_g, conv_norm_b, w_conv_out, ssm_log_step, ssm_lambda_re, ssm_lambda_im, ssm_b_re, ssm_b_im, ssm_c_re, ssm_c_im, ssm_d, w_ssm_glu, w_mix_out, ln1_g, ln1_b, xa_wq, xa_wkv, xa_wo, ln2_g, ln2_b, mlp_w_up, mlp_w_down, ln3_g, ln3_b, loss_target, m_in_norm_g, m_in_norm_b, m_w_in, m_conv_dw, m_conv_db, m_conv_norm_g, m_conv_norm_b, m_w_conv_out, m_ssm_log_step, m_ssm_lambda_re, m_ssm_lambda_im, m_ssm_b_re, m_ssm_b_im, m_ssm_c_re, m_ssm_c_im, m_ssm_d, m_w_ssm_glu, m_w_mix_out, m_ln1_g, m_ln1_b, m_xa_wq, m_xa_wkv, m_xa_wo, m_ln2_g, m_ln2_b, m_mlp_w_up, m_mlp_w_down, m_ln3_g, m_ln3_b, v_in_norm_g, v_in_norm_b, v_w_in, v_conv_dw, v_conv_db, v_conv_norm_g, v_conv_norm_b, v_w_conv_out, v_ssm_log_step, v_ssm_lambda_re, v_ssm_lambda_im, v_ssm_b_re, v_ssm_b_im, v_ssm_c_re, v_ssm_c_im, v_ssm_d, v_w_ssm_glu, v_w_mix_out, v_ln1_g, v_ln1_b, v_xa_wq, v_xa_wkv, v_xa_wo, v_ln2_g, v_ln2_b, v_mlp_w_up, v_mlp_w_down, v_ln3_g, v_ln3_b):
    D = D_MODEL
    xs = x[0]
    T = xs.shape[0]
    mems = mem[0]
    NM = mems.shape[0]
    tgt = loss_target[0]
    my_c = lax.axis_index("c")
    k_me = 2 * lax.axis_index("x") + lax.axis_index("y")
    c_arr = jnp.reshape(my_c, (1,)).astype(jnp.int32)
    k_arr = jnp.reshape(k_me, (1,)).astype(jnp.int32)

    sh_a = jnp.concatenate([w_conv_out[0], w_mix_out[0], xa_wq[0], xa_wo[0], mlp_w_down[0], mlp_w_up[0]], axis=0).astype(BF16)
    def own_block(shard):
        buf = jnp.zeros((N_CHIPS,) + shard.shape, shard.dtype)
        return lax.dynamic_update_slice(buf, shard[None], (k_me, 0, 0))

    ag_bufs = [own_block(s) for s in (sh_a, w_in[0].astype(BF16), xa_wkv[0].astype(BF16), w_ssm_glu[0].astype(BF16))]
    ag_pieces = [(1, 0, D), (0, GA_CONV_OUT, 256), (3, 0, D_SSM), (0, GA_MIX_OUT, 256), (0, GA_WQ, 256), (2, 0, D),
                 (0, GA_WO, 256), (0, GA_UP, D), (0, GA_DOWN, D)]
    ag_send, ag_recv, GA, GIN, GKV, GGLU = gather_start(ag_bufs, ag_pieces, name="gather_start")
    dw_pad = jnp.pad(conv_dw[0], ((0, CONV_HALO - CONV_K), (0, 0)))
    dw_all, _ = allgather_sum(dw_pad, name="allgather_conv_dw")
    dw_full = jnp.transpose(dw_all.reshape(N_DEV, CONV_HALO, D // N_CHIPS)[::2], (1, 0, 2)).reshape(CONV_HALO, D)

    def w_rowshard(row0):
        return dict(b_spec=((N_CHIPS, 256, D), lambda i, j, k: (0, row0 // 256, 0)), b_view=(D, D), tn=D, tk=D)

    _, h0, h0b = ln_fwd(xs, in_norm_g, in_norm_b, name="ln0_fwd")
    (GIN,) = gather_wait(ag_send, ag_recv, [GIN], [(0, 0, D, 0)], h0b, name="gather_wait_in")
    p = mm_nn(h0b, GIN, ((None, D, 1152), lambda i, j, k: (j, 0, 0)), D_IN, tn=1152, tk=D, name="mm_w_in")[0]
    GA, GGLU = gather_wait(ag_send, ag_recv, [GA, GGLU],
                           [(0, GA_CONV_OUT, 256, 1), (1, 0, D_SSM, 2), (0, GA_MIX_OUT, 256, 3)], p, name="gather_wait_mixer")
    c_pre, actb = conv_fwd(p, dw_full, conv_db, conv_norm_g[0].reshape(1, D), conv_norm_b[0].reshape(1, D), name="conv_fwd")
    ya = mm_nn(actb, GA, N=D, name="mm_conv_out", **w_rowshard(GA_CONV_OUT))[0]

    lstep, lre, lim = ssm_log_step[0], ssm_lambda_re[0], ssm_lambda_im[0]
    bre, bim, cre, cim = ssm_b_re[0], ssm_b_im[0], ssm_c_re[0], ssm_c_im[0]
    (ar, ai, bbr, bbi), disc_vjp = jax.vjp(_ssm_discretise, lstep, lre, lim, bre, bim)
    Br, Bi = _blockdiag_in(bbr), _blockdiag_in(bbi)
    Cr, Ci = _blockdiag_out(cre), _blockdiag_out(cim)
    pw_r, pw_i = _powers(ar.reshape(-1), ai.reshape(-1))
    dvec = ssm_d[0].reshape(1, D_SSM)
    xr, xi, yssm = ssm_fwd(p, Br, Bi, Cr, Ci, pw_r, pw_i, dvec, name="ssm_fwd")
    z = mm_nn(yssm, GGLU, ((None, D_SSM, 512), lambda i, j, k: (j, 0, 0)), 2 * D, tn=512, tk=D_SSM, name="mm_ssm_glu")[0]
    mergedb = merge_fwd(p, ya, z, name="merge_fwd")
    mix = mm_nn(mergedb, GA, N=D, name="mm_mix_out", **w_rowshard(GA_MIX_OUT))[0]
    r1, h1, h1b = ln_fwd(mix, ln1_g[0], ln1_b[0], res=h0, name="ln1_fwd")
    GA, GKV = gather_wait(ag_send, ag_recv, [GA, GKV], [(0, GA_WQ, 256, 4), (1, 0, D, 5), (0, GA_WO, 256, 6)], mix,
                          name="gather_wait_attn")

    qb = mm_nn(h1b, GA, N=D, out_dtype=BF16, name="mm_wq", **w_rowshard(GA_WQ))[0]
    kv = mm_nn(mems, GKV, ((None, D, 512), lambda i, j, k: (j, 0, 0)), 2 * D, tn=512, tk=D, name="mm_wkv")[0]
    ob = attn_fwd(qb, kv, name="attn_fwd")
    xa = mm_nn(ob, GA, N=D, name="mm_wo", **w_rowshard(GA_WO))[0]
    r2, h2, h2b = ln_fwd(xa, ln2_g[0], ln2_b[0], res=h1, name="ln2_fwd")
    (GA,) = gather_wait(ag_send, ag_recv, [GA], [(0, GA_UP, D, 7), (0, GA_DOWN, D, 8)], xa, name="gather_wait_mlp")

    def relu2(acc):
        zr = jnp.maximum(acc, 0.0)
        return acc, zr * zr

    zpre, zzb = mm_nn(h2b, GA, ((None, D, D), lambda i, j, k: (j, GA_UP // D, 0)), D_FF, tn=D, tk=D,
                      out_dtypes=[F32, BF16], epilogue=relu2, name="mm_up")
    ff = mm_nn(zzb, GA, ((None, D, D), lambda i, j, k: (k, GA_DOWN // D, 0)), D, tn=D, tk=D, name="mm_down")[0]
    r3, h3, _ = ln_fwd(ff, ln3_g[0], ln3_b[0], res=h2, name="ln3_fwd")
    dh3, sq = loss_head(h3, tgt, name="loss_head")
    loss = lax.psum(0.5 * sq[0, 0] / D, ("x", "y", "c"))

    ga_shape = jax.ShapeDtypeStruct((N_CHIPS, GA_ROWS, D), F32)
    dr3, dr3b, dg3, db3 = ln_bwd(r3, dh3, ln3_g[0], name="ln3_bwd")
    dzpreb = mm_nt(dr3b, GA, ((None, D, D), lambda i, j, k: (j, GA_DOWN // D, 0)), D_FF, tn=D, tk=D, out_dtype=BF16,
                   extras=(zpre,), epilogue=lambda acc, zp: (acc * (2.0 * jnp.maximum(zp, 0.0)),), name="mm_down_t")[0]
    GAg = mm_tn(zzb, dr3b, tm=D, tn=D, out_spec=((None, D, D), lambda i, j, k: (i, GA_DOWN // D, 0)), out_shape=ga_shape,
                name="mm_down_g")
    GAg = mm_tn(h2b, dzpreb, tm=D, tn=D, out_spec=((None, D, D), lambda i, j, k: (j, GA_UP // D, 0)), out_shape=ga_shape,
                out_buf=GAg, name="mm_up_g")
    dh2 = mm_nt(dzpreb, GA, ((None, D, D), lambda i, j, k: (k, GA_UP // D, 0)), D, tn=D, tk=D,
                extras=(dr3,), epilogue=lambda acc, d: (acc + ALPHA * d,), name="mm_up_t")[0]
    dr2, dr2b, dg2, db2 = ln_bwd(r2, dh2, ln2_g[0], name="ln2_bwd")

    def g_rowshard(row0):
        return dict(tm=D, tn=D, out_spec=((N_CHIPS, 256, D), lambda i, j, k: (0, row0 // 256, 0)), out_shape=ga_shape)

    dob = mm_nt(dr2b, GA, N=D, out_dtype=BF16, name="mm_wo_t", **w_rowshard(GA_WO))[0]
    GAg = mm_tn(ob, dr2b, out_buf=GAg, name="mm_wo_g", **g_rowshard(GA_WO))
    dqb, dkv = attn_bwd(qb, kv, dob, name="attn_bwd")
    GAg = mm_tn(h1b, dqb, out_buf=GAg, name="mm_wq_g", **g_rowshard(GA_WQ))
    GKVg = mm_tn(mems, dkv, tm=D, tn=512, tk=NM, out_spec=((None, D, 512), lambda i, j, k: (j, 0, 0)),
                 out_shape=jax.ShapeDtypeStruct((N_CHIPS, D, 512), F32), name="mm_wkv_g")
    dh1 = mm_nt(dqb, GA, N=D, extras=(dr2,), epilogue=lambda acc, d: (acc + ALPHA * d,), name="mm_wq_t",
                **w_rowshard(GA_WQ))[0]
    dr1, dr1b, dg1, db1 = ln_bwd(r1, dh1, ln1_g[0], name="ln1_bwd")

    dmerged = mm_nt(dr1b, GA, N=D, name="mm_mix_t", **w_rowshard(GA_MIX_OUT))[0]
    GAg = mm_tn(mergedb, dr1b, out_buf=GAg, name="mm_mix_g", **g_rowshard(GA_MIX_OUT))
    dyab, dgab, dgbb, dz1b, dz2b = merge_bwd(dmerged, p, ya, z, name="merge_bwd")
    dzb = jnp.concatenate([dz1b, dz2b], axis=1)
    GGLUg = mm_tn(yssm, dzb, tm=D_SSM, tn=512, out_spec=((None, D_SSM, 512), lambda i, j, k: (j, 0, 0)),
                  out_shape=jax.ShapeDtypeStruct((N_CHIPS, D_SSM, 512), F32), name="mm_glu_g")
    dyssm = mm_nt(dzb, GGLU, ((None, D_SSM, 512), lambda i, j, k: (k, 0, 0)), D_SSM, tn=D_SSM, tk=512, name="mm_glu_t")[0]
    pwc_r, pwc_i = pw_r, -pw_i
    dub, dBr, dBi, dCr, dCi, dar8, dai8, dd8 = ssm_bwd(dyssm, p, xr, xi, Br, Bi, Cr, Ci, pwc_r, pwc_i, dvec, name="ssm_bwd")
    dar = jnp.sum(dar8, axis=0).reshape(SSM_GROUPS, SSM_STATE)
    dai = jnp.sum(dai8, axis=0).reshape(SSM_GROUPS, SSM_STATE)
    g_lstep, g_lre, g_lim, g_bre, g_bim = disc_vjp((dar, dai, _diag_in(dBr), _diag_in(dBi)))
    g_cre, g_cim = _diag_out(dCr), _diag_out(dCi)
    g_d = jnp.sum(dd8, axis=0).reshape(1, D_SSM)

    dact = mm_nt(dyab, GA, N=D, name="mm_conv_out_t", **w_rowshard(GA_CONV_OUT))[0]
    GAg = mm_tn(actb, dyab, out_buf=GAg, name="mm_conv_out_g", **g_rowshard(GA_CONV_OUT))
    dc, dng, dnb, ddb = conv_bwd_norm(dact, c_pre, conv_norm_g[0].reshape(1, D), conv_norm_b[0].reshape(1, D), name="conv_bwd_norm")
    dvgb, ddw = conv_bwd_taps(dc, p, dw_full, name="conv_bwd_taps")
    dpb = jnp.concatenate([dvgb, dub, dgab, dgbb], axis=1)
    GINg = mm_tn(h0b, dpb, tm=D, tn=1152, out_spec=((None, D, 1152), lambda i, j, k: (j, 0, 0)),
                 out_shape=jax.ShapeDtypeStruct((N_CHIPS, D, 1152), F32), name="mm_w_in_g")
    dh0 = mm_nt(dpb, GIN, ((None, 512, 1152), lambda i, j, k: (k, j, 0)), D, tn=512, tk=1152,
                extras=(dr1,), epilogue=lambda acc, d: (acc + ALPHA * d,), name="mm_w_in_t")[0]
    gx, _, dg0, db0 = ln_bwd(xs, dh0, in_norm_g, name="ln0_bwd")

    big = [GAg, GINg, GKVg, GGLUg]
    tags = ["a", "in", "kv", "glu"]
    lands = exchange_halves(big, name="rs_exchange_halves")
    parts = [add_sibling(g, l, c_arr, name="rs_add_sibling_" + t) for g, l, t in zip(big, lands, tags)]
    lands2 = scatter_to_owners(parts, name="rs_scatter_to_owners")
    kc_arr = jnp.concatenate([k_arr, c_arr])
    halves = [add_chips(pt, l2, kc_arr, name="rs_add_chips_" + t) for pt, l2, t in zip(parts, lands2, tags)]
    gA, gIN, gKV, gGLU = join_halves(halves, name="rs_join_halves")

    small_names = ["in_norm_g", "in_norm_b", "conv_db", "conv_norm_g", "conv_norm_b", "ssm_log_step", "ssm_lambda_re",
                   "ssm_lambda_im", "ssm_b_re", "ssm_b_im", "ssm_c_re", "ssm_c_im", "ssm_d", "ln1_g", "ln1_b",
                   "ln2_g", "ln2_b", "ln3_g", "ln3_b"]
    small_w = [in_norm_g, in_norm_b, conv_db, conv_norm_g, conv_norm_b, ssm_log_step, ssm_lambda_re, ssm_lambda_im,
               ssm_b_re, ssm_b_im, ssm_c_re, ssm_c_im, ssm_d, ln1_g, ln1_b, ln2_g, ln2_b, ln3_g, ln3_b]
    small_m = [m_in_norm_g, m_in_norm_b, m_conv_db, m_conv_norm_g, m_conv_norm_b, m_ssm_log_step, m_ssm_lambda_re,
               m_ssm_lambda_im, m_ssm_b_re, m_ssm_b_im, m_ssm_c_re, m_ssm_c_im, m_ssm_d, m_ln1_g, m_ln1_b, m_ln2_g,
               m_ln2_b, m_ln3_g, m_ln3_b]
    small_v = [v_in_norm_g, v_in_norm_b, v_conv_db, v_conv_norm_g, v_conv_norm_b, v_ssm_log_step, v_ssm_lambda_re,
               v_ssm_lambda_im, v_ssm_b_re, v_ssm_b_im, v_ssm_c_re, v_ssm_c_im, v_ssm_d, v_ln1_g, v_ln1_b, v_ln2_g,
               v_ln2_b, v_ln3_g, v_ln3_b]
    small_g = [dg0, db0, ddb, dng, dnb, g_lstep, g_lre, g_lim, g_bre, g_bim, g_cre, g_cim, g_d, dg1, db1, dg2, db2, dg3, db3]
    small_shapes = [w.shape for w in small_w]
    n_small_rows = _pack(small_w).shape[0]
    packed_g = _pack(small_g + [ddw])
    _, summed = allgather_sum(packed_g, name="allreduce_small")
    small_rows = sum(-(-math.prod(s) // 128) for s in small_shapes)
    ddw_full = summed[small_rows:small_rows + CONV_HALO * D // 128].reshape(CONV_HALO, D)
    g_dw = lax.dynamic_slice_in_dim(ddw_full, k_me * (D // N_CHIPS), D // N_CHIPS, axis=1)
    gs_packed = jnp.pad(summed[:small_rows], ((0, n_small_rows - small_rows), (0, 0)))

    res = {}

    def upd(nm, w, m, v, g_arr, row0=0):
        shp = w.shape
        w2, m2, v2 = (a.reshape(-1, shp[-1]) for a in (w, m, v))
        outs = adamw(w2, m2, v2, g_arr, row0, name="adamw_" + nm)
        res[nm] = tuple(o.reshape(shp) for o in outs)

    upd("w_conv_out", w_conv_out, m_w_conv_out, v_w_conv_out, gA, GA_CONV_OUT)
    upd("w_mix_out", w_mix_out, m_w_mix_out, v_w_mix_out, gA, GA_MIX_OUT)
    upd("xa_wq", xa_wq, m_xa_wq, v_xa_wq, gA, GA_WQ)
    upd("xa_wo", xa_wo, m_xa_wo, v_xa_wo, gA, GA_WO)
    upd("mlp_w_down", mlp_w_down, m_mlp_w_down, v_mlp_w_down, gA, GA_DOWN)
    upd("mlp_w_up", mlp_w_up, m_mlp_w_up, v_mlp_w_up, gA, GA_UP)
    upd("w_in", w_in, m_w_in, v_w_in, gIN)
    upd("xa_wkv", xa_wkv, m_xa_wkv, v_xa_wkv, gKV)
    upd("w_ssm_glu", w_ssm_glu, m_w_ssm_glu, v_w_ssm_glu, gGLU)
    pad_dw = lambda a: jnp.pad(a[0], ((0, CONV_HALO - CONV_K), (0, 0)))
    dw_outs = adamw(pad_dw(conv_dw), pad_dw(m_conv_dw), pad_dw(v_conv_dw), g_dw, 0, name="adamw_conv_dw")
    res["conv_dw"] = tuple(o[:CONV_K][None] for o in dw_outs)
    sm_outs = adamw(_pack(small_w), _pack(small_m), _pack(small_v), gs_packed, 0, name="adamw_small")
    sm_un = [_unpack(o, small_shapes) for o in sm_outs]
    for idx, nm in enumerate(small_names):
        res[nm] = tuple(sm_un[q][idx] for q in range(4))

    order = ["in_norm_g", "in_norm_b", "w_in", "conv_dw", "conv_db", "conv_norm_g", "conv_norm_b", "w_conv_out",
             "ssm_log_step", "ssm_lambda_re", "ssm_lambda_im", "ssm_b_re", "ssm_b_im", "ssm_c_re", "ssm_c_im", "ssm_d",
             "w_ssm_glu", "w_mix_out", "ln1_g", "ln1_b", "xa_wq", "xa_wkv", "xa_wo", "ln2_g", "ln2_b", "mlp_w_up",
             "mlp_w_down", "ln3_g", "ln3_b"]
    return (loss, gx[None], *[res[n][0] for n in order], *[res[n][1] for n in order],
            *[res[n][2] for n in order], *[res[n][3] for n in order])
```

```python
import functools
import math

import jax
import jax.numpy as jnp
from jax import lax
from jax.experimental import pallas as pl
from jax.experimental.pallas import tpu as pltpu

F32 = jnp.float32
BF16 = jnp.bfloat16
MESH = pl.DeviceIdType.MESH

D_MODEL = 1024
N_HEADS = 4
HEAD_DIM = D_MODEL // N_HEADS
CONV_K = 31
CONV_HALO = 32
D_SSM = 512
SSM_GROUPS = 32
SSM_GROUP = 16
SSM_STATE = 64
SSM_BLOCKS = 4
SSM_BLOCK_IN = D_SSM // SSM_BLOCKS
SSM_BLOCK_STATE = SSM_GROUPS * SSM_STATE // SSM_BLOCKS
D_FF = 4096
D_IN = 4608
LN_EPS = 1e-5
ALPHA = (2.0 * 1) ** 0.25
N_CHIPS = 4
N_DEV = 8
ADAM_LR, ADAM_B1, ADAM_B2, ADAM_EPS, ADAM_WD, ADAM_STEP = 0.001, 0.9, 0.999, 1e-08, 0.01, 10
VMEM_LIMIT_BYTES = 56 * 1024 * 1024


def _pick(dim, cands):
    for c in cands:
        if dim % c == 0:
            return c
    return dim


def _cparams(sem=None):
    return pltpu.CompilerParams(dimension_semantics=sem, vmem_limit_bytes=VMEM_LIMIT_BYTES)


def _sigmoid(x):
    return 1.0 / (1.0 + jnp.exp(-x))


_DIMS = {"nn": (((1,), (0,)), ((), ())), "nt": (((1,), (1,)), ((), ())), "tn": (((0,), (0,)), ((), ()))}


def matmul(a, b, *, mode, M, N, K, tm, tn, tk, a_spec, b_spec, out_specs, out_shapes, name,
           extras=(), extra_specs=(), epilogue=None, alias_buf=None, b_view=None):
    nk = K // tk
    ne = len(extras)
    no = len(out_shapes)
    na = 0 if alias_buf is None else 1
    dims = _DIMS[mode]

    def body(*refs):
        a_ref, b_ref = refs[0], refs[1]
        e_refs = refs[2:2 + ne]
        o_refs = refs[2 + ne + na:2 + ne + na + no]

        def finish(acc):
            outs = (acc,) if epilogue is None else epilogue(acc, *[r[...] for r in e_refs])
            for o, r in zip(outs, o_refs):
                r[...] = o.astype(r.dtype).reshape(r.shape)

        b_blk = b_ref[...] if b_view is None else b_ref[...].reshape(b_view)
        prod = lax.dot_general(a_ref[...].astype(BF16), b_blk.astype(BF16), dims, preferred_element_type=F32)
        if nk == 1:
            finish(prod)
        else:
            acc_ref = refs[-1]
            k = pl.program_id(2)

            @pl.when(k == 0)
            def _():
                acc_ref[...] = prod

            @pl.when(k > 0)
            def _():
                acc_ref[...] += prod

            @pl.when(k == nk - 1)
            def _():
                finish(acc_ref[...])

    in_specs = [pl.BlockSpec(*a_spec), pl.BlockSpec(*b_spec)] + [pl.BlockSpec(*s) for s in extra_specs]
    ins = [a, b, *extras]
    if alias_buf is not None:
        in_specs.append(pl.BlockSpec(memory_space=pl.ANY))
        ins.append(alias_buf)
    res = pl.pallas_call(
        body,
        grid=(M // tm, N // tn, nk),
        in_specs=in_specs,
        out_specs=[pl.BlockSpec(*s) for s in out_specs],
        out_shape=out_shapes,
        scratch_shapes=[] if nk == 1 else [pltpu.VMEM((tm, tn), F32)],
        input_output_aliases={2 + ne: 0} if alias_buf is not None else {},
        compiler_params=_cparams(("parallel", "parallel", "arbitrary")),
        name=name,
    )(*ins)
    return res


def _mn(tm, tn):
    return ((tm, tn), lambda i, j, k: (i, j))


def mm_nn(a, b_arr, b_spec, N, *, name, tm=None, tn, tk, out_dtype=F32, extras=(), epilogue=None, out_dtypes=None,
          b_view=None):
    M, K = a.shape
    tm = tm or _pick(M, [1024, 512, 256, 128])
    dts = out_dtypes or [out_dtype]
    return matmul(a, b_arr, mode="nn", M=M, N=N, K=K, tm=tm, tn=tn, tk=tk,
                  a_spec=((tm, tk), lambda i, j, k: (i, k)), b_spec=b_spec, b_view=b_view,
                  out_specs=[_mn(tm, tn)] * len(dts), out_shapes=[jax.ShapeDtypeStruct((M, N), d) for d in dts],
                  extras=extras, extra_specs=[_mn(tm, tn)] * len(extras), epilogue=epilogue, name=name)


def mm_nt(a, b_arr, b_spec, N, *, name, tm=None, tn, tk, out_dtype=F32, extras=(), epilogue=None, out_dtypes=None,
          b_view=None):
    M, K = a.shape
    tm = tm or _pick(M, [1024, 512, 256, 128])
    dts = out_dtypes or [out_dtype]
    return matmul(a, b_arr, mode="nt", M=M, N=N, K=K, tm=tm, tn=tn, tk=tk,
                  a_spec=((tm, tk), lambda i, j, k: (i, k)), b_spec=b_spec, b_view=b_view,
                  out_specs=[_mn(tm, tn)] * len(dts), out_shapes=[jax.ShapeDtypeStruct((M, N), d) for d in dts],
                  extras=extras, extra_specs=[_mn(tm, tn)] * len(extras), epilogue=epilogue, name=name)


def mm_tn(a, b, *, name, tm, tn, tk=None, out_spec, out_shape, out_buf=None):
    K, M = a.shape
    N = b.shape[1]
    tk = tk or _pick(K, [1024, 512, 256, 128])
    return matmul(a, b, mode="tn", M=M, N=N, K=K, tm=tm, tn=tn, tk=tk,
                  a_spec=((tk, tm), lambda i, j, k: (k, i)), b_spec=((tk, tn), lambda i, j, k: (k, j)),
                  out_specs=[out_spec], out_shapes=[out_shape], alias_buf=out_buf, name=name)[0]


def _rows(tc, w, cb=0):
    return pl.BlockSpec((tc, w), lambda i: (i, cb))


def _const(shape):
    return pl.BlockSpec(shape, lambda i: tuple([0] * len(shape)))


def _ln_stats(r):
    mu = jnp.mean(r, axis=-1, keepdims=True)
    xc = r - mu
    var = jnp.mean(xc * xc, axis=-1, keepdims=True)
    rstd = lax.rsqrt(var + LN_EPS)
    return xc * rstd, rstd


def _rowsum8(v):
    tc, w = v.shape
    return jnp.sum(v.reshape(tc // 8, 8, w), axis=0)


def ln_fwd(x, g, b, *, name, res=None):
    T, D = x.shape
    tc = _pick(T, [512, 256, 128])
    has_res = res is not None

    def body(*refs):
        if has_res:
            x_ref, res_ref, g_ref, b_ref, r_ref, h_ref, hb_ref = refs
            r = ALPHA * res_ref[...] + x_ref[...]
            r_ref[...] = r
        else:
            x_ref, g_ref, b_ref, h_ref, hb_ref = refs
            r = x_ref[...]
        xhat, _ = _ln_stats(r)
        y = xhat * g_ref[...] + b_ref[...]
        h_ref[...] = y
        hb_ref[...] = y.astype(BF16)

    ins = [x] + ([res] if has_res else []) + [g.reshape(1, D), b.reshape(1, D)]
    in_specs = [_rows(tc, D)] * (2 if has_res else 1) + [_const((1, D))] * 2
    n_out = 3 if has_res else 2
    outs = pl.pallas_call(
        body, grid=(T // tc,), in_specs=in_specs, out_specs=[_rows(tc, D)] * n_out,
        out_shape=[jax.ShapeDtypeStruct((T, D), F32)] * (n_out - 1) + [jax.ShapeDtypeStruct((T, D), BF16)],
        compiler_params=_cparams(("arbitrary",)), name=name)(*ins)
    if has_res:
        return outs
    return (x,) + tuple(outs)


def ln_bwd(r, dy, g, *, name):
    T, D = r.shape
    tc = _pick(T, [512, 256, 128])
    nt = T // tc

    def body(r_ref, dy_ref, g_ref, dr_ref, drb_ref, dg_ref, db_ref, accg, accb):
        i = pl.program_id(0)

        @pl.when(i == 0)
        def _():
            accg[...] = jnp.zeros_like(accg)
            accb[...] = jnp.zeros_like(accb)

        xhat, rstd = _ln_stats(r_ref[...])
        dy = dy_ref[...]
        dxh = dy * g_ref[...]
        m1 = jnp.mean(dxh, axis=-1, keepdims=True)
        m2 = jnp.mean(dxh * xhat, axis=-1, keepdims=True)
        dr = rstd * (dxh - m1 - xhat * m2)
        dr_ref[...] = dr
        drb_ref[...] = dr.astype(BF16)
        accg[...] += _rowsum8(dy * xhat)
        accb[...] += _rowsum8(dy)

        @pl.when(i == nt - 1)
        def _():
            dg_ref[...] = jnp.sum(accg[...], axis=0, keepdims=True)
            db_ref[...] = jnp.sum(accb[...], axis=0, keepdims=True)

    return pl.pallas_call(
        body, grid=(nt,), in_specs=[_rows(tc, D), _rows(tc, D), _const((1, D))],
        out_specs=[_rows(tc, D), _rows(tc, D), _const((1, D)), _const((1, D))],
        out_shape=[jax.ShapeDtypeStruct((T, D), F32), jax.ShapeDtypeStruct((T, D), BF16),
                   jax.ShapeDtypeStruct((1, D), F32), jax.ShapeDtypeStruct((1, D), F32)],
        scratch_shapes=[pltpu.VMEM((8, D), F32), pltpu.VMEM((8, D), F32)],
        compiler_params=_cparams(("arbitrary",)), name=name)(r, dy, g.reshape(1, D))


def loss_head(y, target, *, name):
    T, D = y.shape
    tc = _pick(T, [512, 256, 128])
    nt = T // tc

    def body(y_ref, t_ref, dy_ref, loss_ref, acc):
        i = pl.program_id(0)

        @pl.when(i == 0)
        def _():
            acc[...] = jnp.zeros_like(acc)

        e = y_ref[...] - t_ref[...]
        dy_ref[...] = e * (1.0 / D)
        acc[...] += _rowsum8(e * e)

        @pl.when(i == nt - 1)
        def _():
            s = jnp.sum(jnp.sum(acc[...], axis=0, keepdims=True), axis=1, keepdims=True)
            loss_ref[...] = jnp.broadcast_to(s, (1, 128))

    return pl.pallas_call(
        body, grid=(nt,), in_specs=[_rows(tc, D), _rows(tc, D)],
        out_specs=[_rows(tc, D), _const((1, 128))],
        out_shape=[jax.ShapeDtypeStruct((T, D), F32), jax.ShapeDtypeStruct((1, 128), F32)],
        scratch_shapes=[pltpu.VMEM((8, D), F32)],
        compiler_params=_cparams(("arbitrary",)), name=name)(y, target)


def _halo_prev(tc):
    per = tc // CONV_HALO
    return lambda i: jnp.maximum(i * per - 1, 0)


CONV_ROWS = 32
CONV_TAP_GROUP = 4


def _fill_shifts(S, nrows):
    for b in range(1, 8):
        S[b, 0:nrows - 8, :] = S[0, b:b + nrows - 8, :]


def _tap_sum(S, w_ref, offs, r0, nrows):
    acc = None
    for k, o in enumerate(offs):
        a, b = divmod(o, 8)
        term = w_ref[k:k + 1, :] * S[b, pl.ds(pl.multiple_of(r0 + 8 * a, 8), nrows), :]
        acc = term if acc is None else acc + term
    return acc


def conv_fwd(p, dw, db, ng, nb, *, name):
    T = p.shape[0]
    D = D_MODEL
    tc = _pick(T, [256, 128])
    prev = _halo_prev(tc)
    off = CONV_HALO - (CONV_K - 1)
    offs = [off + k for k in range(CONV_K)]

    def body(val_ref, gate_ref, valp_ref, gatep_ref, dw_ref, db_ref, ng_ref, nb_ref, c_ref, act_ref, S):
        i = pl.program_id(0)
        u_prev = valp_ref[...] * _sigmoid(gatep_ref[...])
        S[0, 0:CONV_HALO, :] = jnp.where(i > 0, u_prev, 0.0)
        S[0, CONV_HALO:CONV_HALO + tc, :] = val_ref[...] * _sigmoid(gate_ref[...])
        _fill_shifts(S, CONV_HALO + tc)

        def rows(j, carry):
            r0 = pl.multiple_of(j * CONV_ROWS, CONV_ROWS)
            c_ref[pl.ds(r0, CONV_ROWS), :] = _tap_sum(S, dw_ref, offs, r0, CONV_ROWS) + db_ref[...]
            return carry

        lax.fori_loop(0, tc // CONV_ROWS, rows, 0)
        c = c_ref[...]
        xhat, _ = _ln_stats(c)
        cn = xhat * ng_ref[...] + nb_ref[...]
        act_ref[...] = (cn * _sigmoid(cn)).astype(BF16)

    return pl.pallas_call(
        body, grid=(T // tc,),
        in_specs=[_rows(tc, D, 0), _rows(tc, D, 1),
                  pl.BlockSpec((CONV_HALO, D), lambda i: (prev(i), 0)), pl.BlockSpec((CONV_HALO, D), lambda i: (prev(i), 1)),
                  _const((CONV_HALO, D)), _const((1, D)), _const((1, D)), _const((1, D))],
        out_specs=[_rows(tc, D), _rows(tc, D)],
        out_shape=[jax.ShapeDtypeStruct((T, D), F32), jax.ShapeDtypeStruct((T, D), BF16)],
        scratch_shapes=[pltpu.VMEM((8, CONV_HALO + tc, D), F32)],
        compiler_params=_cparams(("arbitrary",)), name=name)(p, p, p, p, dw, db, ng, nb)


def conv_bwd_norm(dact, c_pre, ng, nb, *, name):
    T, D = c_pre.shape
    tc = _pick(T, [512, 256, 128])
    nt = T // tc

    def body(da_ref, c_ref, ng_ref, nb_ref, dc_ref, dng_ref, dnb_ref, ddb_ref, accg, accb, accd):
        i = pl.program_id(0)

        @pl.when(i == 0)
        def _():
            accg[...] = jnp.zeros_like(accg)
            accb[...] = jnp.zeros_like(accb)
            accd[...] = jnp.zeros_like(accd)

        xhat, rstd = _ln_stats(c_ref[...])
        cn = xhat * ng_ref[...] + nb_ref[...]
        s = _sigmoid(cn)
        dcn = da_ref[...] * (s * (1.0 + cn * (1.0 - s)))
        dxh = dcn * ng_ref[...]
        m1 = jnp.mean(dxh, axis=-1, keepdims=True)
        m2 = jnp.mean(dxh * xhat, axis=-1, keepdims=True)
        dc = rstd * (dxh - m1 - xhat * m2)
        dc_ref[...] = dc
        accg[...] += _rowsum8(dcn * xhat)
        accb[...] += _rowsum8(dcn)
        accd[...] += _rowsum8(dc)

        @pl.when(i == nt - 1)
        def _():
            dng_ref[...] = jnp.sum(accg[...], axis=0, keepdims=True)
            dnb_ref[...] = jnp.sum(accb[...], axis=0, keepdims=True)
            ddb_ref[...] = jnp.sum(accd[...], axis=0, keepdims=True)

    vec = jax.ShapeDtypeStruct((1, D), F32)
    return pl.pallas_call(
        body, grid=(nt,), in_specs=[_rows(tc, D), _rows(tc, D), _const((1, D)), _const((1, D))],
        out_specs=[_rows(tc, D), _const((1, D)), _const((1, D)), _const((1, D))],
        out_shape=[jax.ShapeDtypeStruct((T, D), F32), vec, vec, vec],
        scratch_shapes=[pltpu.VMEM((8, D), F32)] * 3,
        compiler_params=_cparams(("arbitrary",)), name=name)(dact, c_pre, ng, nb)


def conv_bwd_taps(dc, p, dw, *, name):
    T, D = dc.shape
    tc = _pick(T, [256, 128])
    nt = T // tc
    per = tc // CONV_HALO
    prev = _halo_prev(tc)
    last_halo = T // CONV_HALO - 1
    nxt = lambda i: jnp.minimum((i + 1) * per, last_halo)
    off = CONV_HALO - (CONV_K - 1)

    def body(dc_ref, dcn_ref, val_ref, gate_ref, valp_ref, gatep_ref, dw_ref, dvg_ref, ddw_ref, ext_u, ext_d, acc):
        i = pl.program_id(0)

        @pl.when(i == 0)
        def _():
            acc[...] = jnp.zeros_like(acc)

        u_prev = valp_ref[...] * _sigmoid(gatep_ref[...])
        ext_u[0, 0:CONV_HALO, :] = jnp.where(i > 0, u_prev, 0.0)
        ext_u[0, CONV_HALO:CONV_HALO + tc, :] = val_ref[...] * _sigmoid(gate_ref[...])
        ext_d[0, 0:tc, :] = dc_ref[...]
        ext_d[0, tc:tc + CONV_HALO, :] = jnp.where(i < nt - 1, dcn_ref[...], 0.0)
        _fill_shifts(ext_u, CONV_HALO + tc)
        _fill_shifts(ext_d, CONV_HALO + tc)

        def rows(j, carry):
            r0 = pl.multiple_of(j * CONV_ROWS, CONV_ROWS)
            sl = pl.ds(r0, CONV_ROWS)
            du = _tap_sum(ext_d, dw_ref, [CONV_K - 1 - k for k in range(CONV_K)], r0, CONV_ROWS)
            sg = _sigmoid(gate_ref[sl, :])
            dvg_ref[sl, 0:D] = (du * sg).astype(BF16)
            dvg_ref[sl, D:2 * D] = (du * val_ref[sl, :] * sg * (1.0 - sg)).astype(BF16)
            return carry

        lax.fori_loop(0, tc // CONV_ROWS, rows, 0)

        for k0 in range(0, CONV_K, CONV_TAP_GROUP):
            ks = list(range(k0, min(k0 + CONV_TAP_GROUP, CONV_K)))

            def taps(j, accs, ks=ks):
                r0 = pl.multiple_of(j * 8, 8)
                dct = dc_ref[pl.ds(r0, 8), :]
                out = []
                for k, a_k in zip(ks, accs):
                    a, b = divmod(off + k, 8)
                    out.append(a_k + dct * ext_u[b, pl.ds(pl.multiple_of(r0 + 8 * a, 8), 8), :])
                return tuple(out)

            accs = lax.fori_loop(0, tc // 8, taps, tuple(jnp.zeros((8, D), F32) for _ in ks))
            for k, a_k in zip(ks, accs):
                acc[k] += a_k

        @pl.when(i == nt - 1)
        def _():
            ddw_ref[...] = jnp.zeros_like(ddw_ref)
            for k in range(CONV_K):
                ddw_ref[k:k + 1, :] = jnp.sum(acc[k], axis=0, keepdims=True)

    return pl.pallas_call(
        body, grid=(nt,),
        in_specs=[_rows(tc, D), pl.BlockSpec((CONV_HALO, D), lambda i: (nxt(i), 0)),
                  _rows(tc, D, 0), _rows(tc, D, 1),
                  pl.BlockSpec((CONV_HALO, D), lambda i: (prev(i), 0)), pl.BlockSpec((CONV_HALO, D), lambda i: (prev(i), 1)),
                  _const((CONV_HALO, D))],
        out_specs=[_rows(tc, 2 * D), _const((CONV_HALO, D))],
        out_shape=[jax.ShapeDtypeStruct((T, 2 * D), BF16), jax.ShapeDtypeStruct((CONV_HALO, D), F32)],
        scratch_shapes=[pltpu.VMEM((8, CONV_HALO + tc, D), F32), pltpu.VMEM((8, CONV_HALO + tc, D), F32),
                        pltpu.VMEM((CONV_K, 8, D), F32)],
        compiler_params=_cparams(("arbitrary",)), name=name)(dc, dc, p, p, p, p, dw)


GATE_A0 = (2 * D_MODEL + D_SSM) // 512
GATE_B0 = GATE_A0 + 2


def merge_fwd(p, ya, z, *, name):
    T = p.shape[0]
    D = D_MODEL
    tc = _pick(T, [512, 256, 128])
    W = 512

    def body(ga_ref, gb_ref, ya_ref, z1_ref, z2_ref, o_ref):
        yb = z1_ref[...] * _sigmoid(z2_ref[...])
        o_ref[...] = (_sigmoid(ga_ref[...]) * ya_ref[...] + _sigmoid(gb_ref[...]) * yb).astype(BF16)

    return pl.pallas_call(
        body, grid=(T // tc, D // W),
        in_specs=[pl.BlockSpec((tc, W), lambda i, j: (i, GATE_A0 + j)), pl.BlockSpec((tc, W), lambda i, j: (i, GATE_B0 + j)),
                  pl.BlockSpec((tc, W), lambda i, j: (i, j)), pl.BlockSpec((tc, W), lambda i, j: (i, j)),
                  pl.BlockSpec((tc, W), lambda i, j: (i, D // W + j))],
        out_specs=pl.BlockSpec((tc, W), lambda i, j: (i, j)),
        out_shape=jax.ShapeDtypeStruct((T, D), BF16),
        compiler_params=_cparams(("arbitrary", "arbitrary")), name=name)(p, p, ya, z, z)


def merge_bwd(dm, p, ya, z, *, name):
    T = p.shape[0]
    D = D_MODEL
    tc = _pick(T, [512, 256, 128])
    W = 512
    nb = D // W

    def body(dm_ref, ga_ref, gb_ref, ya_ref, z1_ref, z2_ref, dya_ref, dga_ref, dgb_ref, dz1_ref, dz2_ref):
        dm = dm_ref[...]
        sa = _sigmoid(ga_ref[...])
        sb = _sigmoid(gb_ref[...])
        s2 = _sigmoid(z2_ref[...])
        z1 = z1_ref[...]
        yb = z1 * s2
        dya_ref[...] = (dm * sa).astype(BF16)
        dga_ref[...] = (dm * ya_ref[...] * sa * (1.0 - sa)).astype(BF16)
        dgb_ref[...] = (dm * yb * sb * (1.0 - sb)).astype(BF16)
        dyb = dm * sb
        dz1_ref[...] = (dyb * s2).astype(BF16)
        dz2_ref[...] = (dyb * z1 * s2 * (1.0 - s2)).astype(BF16)

    blk = lambda off: pl.BlockSpec((tc, W), lambda i, j: (i, off + j))
    dya, dga, dgb, dz1, dz2 = pl.pallas_call(
        body, grid=(T // tc, nb),
        in_specs=[blk(0), blk(GATE_A0), blk(GATE_B0), blk(0), blk(0), blk(nb)],
        out_specs=[blk(0)] * 5,
        out_shape=[jax.ShapeDtypeStruct((T, D), BF16)] * 5,
        compiler_params=_cparams(("arbitrary", "arbitrary")), name=name)(dm, p, p, ya, z, z)
    return dya, dga, dgb, dz1, dz2


def _scan_block(src_r, src_i, dst_r, dst_i, car_r, car_i, pw_r, pw_i, cw_r, cw_i, ntiles, reverse, extra=None):
    W = src_r.shape[1]
    rows = lax.broadcasted_iota(jnp.int32, (8, W), 0)
    steps = []
    for d, pr in ((1, 0), (2, 1), (4, 3)):
        steps.append((d, jnp.broadcast_to(pw_r[pr:pr + 1, :], (8, W)), jnp.broadcast_to(pw_i[pr:pr + 1, :], (8, W))))
    cw_r, cw_i = cw_r[...], cw_i[...]

    def tile(jj, carry):
        j = ntiles - 1 - jj if reverse else jj
        sl = pl.ds(pl.multiple_of(j * 8, 8), 8)
        xr, xi = src_r[sl, :], src_i[sl, :]
        for d, lr, li in steps:
            if reverse:
                sr = jnp.where(rows < 8 - d, pltpu.roll(xr, 8 - d, 0), 0.0)
                si = jnp.where(rows < 8 - d, pltpu.roll(xi, 8 - d, 0), 0.0)
            else:
                sr = jnp.where(rows >= d, pltpu.roll(xr, d, 0), 0.0)
                si = jnp.where(rows >= d, pltpu.roll(xi, d, 0), 0.0)
            xr, xi = xr + lr * sr - li * si, xi + lr * si + li * sr
        cr, ci = car_r[...], car_i[...]
        xr, xi = xr + cw_r * cr - cw_i * ci, xi + cw_r * ci + cw_i * cr
        dst_r[sl, :] = xr
        dst_i[sl, :] = xi
        edge = 0 if reverse else 7
        car_r[...] = jnp.broadcast_to(xr[edge:edge + 1, :], (8, W))
        car_i[...] = jnp.broadcast_to(xi[edge:edge + 1, :], (8, W))
        if extra is not None:
            carry = extra(j, xr, xi, carry)
        return carry

    return tile


def ssm_fwd(p, Br, Bi, Cr, Ci, pw_r, pw_i, dvec, *, name):
    T = p.shape[0]
    tt = _pick(T, [256, 128])
    nt = T // tt
    WI, WS = SSM_BLOCK_IN, SSM_BLOCK_STATE
    u0 = 2 * D_MODEL // WI

    def body(u_ref, br_ref, bi_ref, cr_ref, ci_ref, pwr_ref, pwi_ref, d_ref, xr_ref, xi_ref, y_ref, bur, bui, car_r, car_i):
        i = pl.program_id(1)

        @pl.when(i == 0)
        def _():
            car_r[...] = jnp.zeros_like(car_r)
            car_i[...] = jnp.zeros_like(car_i)

        u = u_ref[...]
        ub = u.astype(BF16)
        bur[...] = jnp.dot(ub, br_ref[...].astype(BF16), preferred_element_type=F32)
        bui[...] = jnp.dot(ub, bi_ref[...].astype(BF16), preferred_element_type=F32)
        tile = _scan_block(bur, bui, xr_ref, xi_ref, car_r, car_i, pwr_ref, pwi_ref, pwr_ref, pwi_ref, tt // 8, False)
        lax.fori_loop(0, tt // 8, tile, 0)
        y = (jnp.dot(xr_ref[...].astype(BF16), cr_ref[...].astype(BF16), preferred_element_type=F32)
             - jnp.dot(xi_ref[...].astype(BF16), ci_ref[...].astype(BF16), preferred_element_type=F32)
             + d_ref[...] * u)
        y_ref[...] = y.astype(BF16)

    return pl.pallas_call(
        body, grid=(SSM_BLOCKS, nt),
        in_specs=[pl.BlockSpec((tt, WI), lambda b, i: (i, u0 + b)),
                  pl.BlockSpec((None, WI, WS), lambda b, i: (b, 0, 0)), pl.BlockSpec((None, WI, WS), lambda b, i: (b, 0, 0)),
                  pl.BlockSpec((None, WS, WI), lambda b, i: (b, 0, 0)), pl.BlockSpec((None, WS, WI), lambda b, i: (b, 0, 0)),
                  pl.BlockSpec((8, WS), lambda b, i: (0, b)), pl.BlockSpec((8, WS), lambda b, i: (0, b)),
                  pl.BlockSpec((1, WI), lambda b, i: (0, b))],
        out_specs=[pl.BlockSpec((tt, WS), lambda b, i: (i, b)), pl.BlockSpec((tt, WS), lambda b, i: (i, b)),
                   pl.BlockSpec((tt, WI), lambda b, i: (i, b))],
        out_shape=[jax.ShapeDtypeStruct((T, SSM_BLOCKS * WS), F32)] * 2 + [jax.ShapeDtypeStruct((T, D_SSM), BF16)],
        scratch_shapes=[pltpu.VMEM((tt, WS), F32), pltpu.VMEM((tt, WS), F32), pltpu.VMEM((8, WS), F32), pltpu.VMEM((8, WS), F32)],
        compiler_params=_cparams(("arbitrary", "arbitrary")), name=name)(p, Br, Bi, Cr, Ci, pw_r, pw_i, dvec)


def ssm_bwd(dy, p, xr, xi, Br, Bi, Cr, Ci, pwc_r, pwc_i, dvec, *, name):
    T = p.shape[0]
    tt = _pick(T, [256, 128])
    nt = T // tt
    WI, WS = SSM_BLOCK_IN, SSM_BLOCK_STATE
    u0 = 2 * D_MODEL // WI
    tb = lambda i: nt - 1 - i
    xprev = lambda i: jnp.maximum(tb(i) * (tt // 8) - 1, 0)
    tn_dims = _DIMS["tn"]
    nt_dims = _DIMS["nt"]

    def body(dy_ref, u_ref, xr_ref, xi_ref, xpr_ref, xpi_ref, br_ref, bi_ref, cr_ref, ci_ref, pwr_ref, pwi_ref,
             cwr_ref, cwi_ref, d_ref,
             du_ref, dbr_ref, dbi_ref, dcr_ref, dci_ref, dar_ref, dai_ref, dd_ref,
             gr, gi, ext_r, ext_i, car_r, car_i):
        i = pl.program_id(1)

        @pl.when(i == 0)
        def _():
            car_r[...] = jnp.zeros_like(car_r)
            car_i[...] = jnp.zeros_like(car_i)
            dbr_ref[...] = jnp.zeros_like(dbr_ref)
            dbi_ref[...] = jnp.zeros_like(dbi_ref)
            dcr_ref[...] = jnp.zeros_like(dcr_ref)
            dci_ref[...] = jnp.zeros_like(dci_ref)
            dar_ref[...] = jnp.zeros_like(dar_ref)
            dai_ref[...] = jnp.zeros_like(dai_ref)
            dd_ref[...] = jnp.zeros_like(dd_ref)

        dy = dy_ref[...]
        dyb = dy.astype(BF16)
        u = u_ref[...]
        ub = u.astype(BF16)
        gr[...] = lax.dot_general(dyb, cr_ref[...].astype(BF16), nt_dims, preferred_element_type=F32)
        gi[...] = -lax.dot_general(dyb, ci_ref[...].astype(BF16), nt_dims, preferred_element_type=F32)
        first = tb(i) == 0
        ext_r[0:8, :] = jnp.where(first, 0.0, xpr_ref[...])
        ext_i[0:8, :] = jnp.where(first, 0.0, xpi_ref[...])
        ext_r[8:8 + tt, :] = xr_ref[...]
        ext_i[8:8 + tt, :] = xi_ref[...]
        rows = lax.broadcasted_iota(jnp.int32, (8, WS), 0)

        def lam_grad(j, g_r, g_i, carry):
            a_r, a_i = carry
            cur = pl.ds(pl.multiple_of(j * 8 + 8, 8), 8)
            prv = pl.ds(pl.multiple_of(j * 8, 8), 8)
            xc_r, xc_i = ext_r[cur, :], ext_i[cur, :]
            xl_r, xl_i = ext_r[prv, :], ext_i[prv, :]
            xp_r = jnp.where(rows == 0, jnp.broadcast_to(xl_r[7:8, :], (8, WS)), pltpu.roll(xc_r, 1, 0))
            xp_i = jnp.where(rows == 0, jnp.broadcast_to(xl_i[7:8, :], (8, WS)), pltpu.roll(xc_i, 1, 0))
            return (a_r + g_r * xp_r + g_i * xp_i, a_i + g_i * xp_r - g_r * xp_i)

        tile = _scan_block(gr, gi, gr, gi, car_r, car_i, pwr_ref, pwi_ref, cwr_ref, cwi_ref, tt // 8, True, extra=lam_grad)
        z8 = jnp.zeros((8, WS), F32)
        a_r, a_i = lax.fori_loop(0, tt // 8, tile, (z8, z8))
        dar_ref[...] += a_r
        dai_ref[...] += a_i
        grb = gr[...].astype(BF16)
        gib = gi[...].astype(BF16)
        dbr_ref[...] += lax.dot_general(ub, grb, tn_dims, preferred_element_type=F32)
        dbi_ref[...] += lax.dot_general(ub, gib, tn_dims, preferred_element_type=F32)
        dcr_ref[...] += lax.dot_general(xr_ref[...].astype(BF16), dyb, tn_dims, preferred_element_type=F32)
        dci_ref[...] -= lax.dot_general(xi_ref[...].astype(BF16), dyb, tn_dims, preferred_element_type=F32)
        du = (lax.dot_general(grb, br_ref[...].astype(BF16), nt_dims, preferred_element_type=F32)
              + lax.dot_general(gib, bi_ref[...].astype(BF16), nt_dims, preferred_element_type=F32)
              + d_ref[...] * dy)
        du_ref[...] = du.astype(BF16)
        dd_ref[...] += _rowsum8(dy * u)

    wspec = lambda shp: pl.BlockSpec((None,) + shp, lambda b, i: (b, 0, 0))
    return pl.pallas_call(
        body, grid=(SSM_BLOCKS, nt),
        in_specs=[pl.BlockSpec((tt, WI), lambda b, i: (tb(i), b)),
                  pl.BlockSpec((tt, WI), lambda b, i: (tb(i), u0 + b)),
                  pl.BlockSpec((tt, WS), lambda b, i: (tb(i), b)), pl.BlockSpec((tt, WS), lambda b, i: (tb(i), b)),
                  pl.BlockSpec((8, WS), lambda b, i: (xprev(i), b)), pl.BlockSpec((8, WS), lambda b, i: (xprev(i), b)),
                  wspec((WI, WS)), wspec((WI, WS)), wspec((WS, WI)), wspec((WS, WI)),
                  pl.BlockSpec((8, WS), lambda b, i: (0, b)), pl.BlockSpec((8, WS), lambda b, i: (0, b)),
                  pl.BlockSpec((8, WS), lambda b, i: (0, b)), pl.BlockSpec((8, WS), lambda b, i: (0, b)),
                  pl.BlockSpec((1, WI), lambda b, i: (0, b))],
        out_specs=[pl.BlockSpec((tt, WI), lambda b, i: (tb(i), b)),
                   wspec((WI, WS)), wspec((WI, WS)), wspec((WS, WI)), wspec((WS, WI)),
                   pl.BlockSpec((8, WS), lambda b, i: (0, b)), pl.BlockSpec((8, WS), lambda b, i: (0, b)),
                   pl.BlockSpec((8, WI), lambda b, i: (0, b))],
        out_shape=[jax.ShapeDtypeStruct((T, D_SSM), BF16),
                   jax.ShapeDtypeStruct((SSM_BLOCKS, WI, WS), F32), jax.ShapeDtypeStruct((SSM_BLOCKS, WI, WS), F32),
                   jax.ShapeDtypeStruct((SSM_BLOCKS, WS, WI), F32), jax.ShapeDtypeStruct((SSM_BLOCKS, WS, WI), F32),
                   jax.ShapeDtypeStruct((8, SSM_BLOCKS * WS), F32), jax.ShapeDtypeStruct((8, SSM_BLOCKS * WS), F32),
                   jax.ShapeDtypeStruct((8, D_SSM), F32)],
        scratch_shapes=[pltpu.VMEM((tt, WS), F32), pltpu.VMEM((tt, WS), F32),
                        pltpu.VMEM((tt + 8, WS), F32), pltpu.VMEM((tt + 8, WS), F32),
                        pltpu.VMEM((8, WS), F32), pltpu.VMEM((8, WS), F32)],
        compiler_params=_cparams(("arbitrary", "arbitrary")), name=name,
    )(dy, p, xr, xi, xr, xi, Br, Bi, Cr, Ci, pwc_r, pwc_i, pwc_r[::-1], pwc_i[::-1], dvec)


def _ssm_discretise(log_step, lam_re, lam_im, b_re, b_im):
    step = jnp.exp(log_step)[:, None]
    mag = jnp.exp(lam_re * step)
    ar = mag * jnp.cos(lam_im * step)
    ai = mag * jnp.sin(lam_im * step)
    den = lam_re * lam_re + lam_im * lam_im
    nr = ar - 1.0
    cr = (nr * lam_re + ai * lam_im) / den
    ci = (ai * lam_re - nr * lam_im) / den
    bbr = cr[..., None] * b_re - ci[..., None] * b_im
    bbi = cr[..., None] * b_im + ci[..., None] * b_re
    return ar, ai, bbr, bbi


def _blockdiag_in(bb):
    t = jnp.transpose(bb, (0, 2, 1)).reshape(SSM_BLOCKS, 8, SSM_GROUP, SSM_STATE)
    eye = jnp.eye(8, dtype=bb.dtype)
    return (t[:, :, :, None, :] * eye[None, :, None, :, None]).reshape(SSM_BLOCKS, SSM_BLOCK_IN, SSM_BLOCK_STATE)


def _blockdiag_out(cc):
    t = jnp.transpose(cc, (0, 2, 1)).reshape(SSM_BLOCKS, 8, SSM_STATE, SSM_GROUP)
    eye = jnp.eye(8, dtype=cc.dtype)
    return (t[:, :, :, None, :] * eye[None, :, None, :, None]).reshape(SSM_BLOCKS, SSM_BLOCK_STATE, SSM_BLOCK_IN)


def _diag_in(d):
    t = d.reshape(SSM_BLOCKS, 8, SSM_GROUP, 8, SSM_STATE)
    t = jnp.einsum("bghgp->bghp", t).reshape(SSM_GROUPS, SSM_GROUP, SSM_STATE)
    return jnp.transpose(t, (0, 2, 1))


def _diag_out(d):
    t = d.reshape(SSM_BLOCKS, 8, SSM_STATE, 8, SSM_GROUP)
    t = jnp.einsum("bgpgh->bgph", t).reshape(SSM_GROUPS, SSM_STATE, SSM_GROUP)
    return jnp.transpose(t, (0, 2, 1))


def _powers(ar, ai):
    rs, is_ = [ar], [ai]
    for _ in range(7):
        r, i = rs[-1], is_[-1]
        rs.append(r * ar - i * ai)
        is_.append(r * ai + i * ar)
    return jnp.stack(rs), jnp.stack(is_)


def attn_fwd(q, kv, *, name):
    T, D = q.shape
    nm = kv.shape[0]
    tq = _pick(T, [512, 256, 128])
    scale = HEAD_DIM ** -0.5

    def body(q_ref, k_ref, v_ref, o_ref):
        for h in range(N_HEADS):
            sl = slice(h * HEAD_DIM, (h + 1) * HEAD_DIM)
            s = lax.dot_general(q_ref[:, sl], k_ref[:, sl].astype(BF16), _DIMS["nt"], preferred_element_type=F32) * scale
            e = jnp.exp(s - jnp.max(s, axis=-1, keepdims=True))
            pr = e / jnp.sum(e, axis=-1, keepdims=True)
            o_ref[:, sl] = jnp.dot(pr.astype(BF16), v_ref[:, sl].astype(BF16), preferred_element_type=F32).astype(BF16)

    return pl.pallas_call(
        body, grid=(T // tq,),
        in_specs=[_rows(tq, D), pl.BlockSpec((nm, D), lambda i: (0, 0)), pl.BlockSpec((nm, D), lambda i: (0, 1))],
        out_specs=_rows(tq, D), out_shape=jax.ShapeDtypeStruct((T, D), BF16),
        compiler_params=_cparams(("arbitrary",)), name=name)(q, kv, kv)


def attn_bwd(q, kv, do, *, name):
    T, D = q.shape
    nm = kv.shape[0]
    tq = _pick(T, [512, 256, 128])
    nt = T // tq
    scale = HEAD_DIM ** -0.5

    def body(q_ref, k_ref, v_ref, do_ref, dq_ref, dkv_ref):
        i = pl.program_id(0)

        @pl.when(i == 0)
        def _():
            dkv_ref[...] = jnp.zeros_like(dkv_ref)

        for h in range(N_HEADS):
            sl = slice(h * HEAD_DIM, (h + 1) * HEAD_DIM)
            slv = slice(D + h * HEAD_DIM, D + (h + 1) * HEAD_DIM)
            qh = q_ref[:, sl]
            kh = k_ref[:, sl].astype(BF16)
            vh = v_ref[:, sl].astype(BF16)
            doh = do_ref[:, sl].astype(BF16)
            s = lax.dot_general(qh, kh, _DIMS["nt"], preferred_element_type=F32) * scale
            e = jnp.exp(s - jnp.max(s, axis=-1, keepdims=True))
            pr = e / jnp.sum(e, axis=-1, keepdims=True)
            dp = lax.dot_general(doh, vh, _DIMS["nt"], preferred_element_type=F32)
            ds = (pr * (dp - jnp.sum(pr * dp, axis=-1, keepdims=True)) * scale).astype(BF16)
            dq_ref[:, sl] = jnp.dot(ds, kh, preferred_element_type=F32).astype(BF16)
            dkv_ref[:, sl] += lax.dot_general(ds, qh, _DIMS["tn"], preferred_element_type=F32)
            dkv_ref[:, slv] += lax.dot_general(pr.astype(BF16), doh, _DIMS["tn"], preferred_element_type=F32)

    return pl.pallas_call(
        body, grid=(nt,),
        in_specs=[_rows(tq, D), pl.BlockSpec((nm, D), lambda i: (0, 0)), pl.BlockSpec((nm, D), lambda i: (0, 1)), _rows(tq, D)],
        out_specs=[_rows(tq, D), _const((nm, 2 * D))],
        out_shape=[jax.ShapeDtypeStruct((T, D), BF16), jax.ShapeDtypeStruct((nm, 2 * D), F32)],
        compiler_params=_cparams(("arbitrary",)), name=name)(q, kv, kv, do)


def _adam_math(w, g, m, v):
    m = ADAM_B1 * m + (1.0 - ADAM_B1) * g
    v = ADAM_B2 * v + (1.0 - ADAM_B2) * (g * g)
    m_hat = m / (1.0 - ADAM_B1 ** ADAM_STEP)
    v_hat = v / (1.0 - ADAM_B2 ** ADAM_STEP)
    delta = -ADAM_LR * (m_hat / (jnp.sqrt(v_hat) + ADAM_EPS) + ADAM_WD * w)
    return delta, m, v


def adamw(w, m, v, g_arr, g_row0, *, name):
    R, C = w.shape
    tr = _pick(R, [256, 128, 64, 32, 16, 8])
    assert g_row0 % tr == 0
    g0 = g_row0 // tr

    def body(w_ref, m_ref, v_ref, g_ref, go_ref, d_ref, mo_ref, vo_ref):
        g = g_ref[...]
        d, mn, vn = _adam_math(w_ref[...], g, m_ref[...], v_ref[...])
        go_ref[...] = g
        d_ref[...] = d
        mo_ref[...] = mn
        vo_ref[...] = vn

    sp = pl.BlockSpec((tr, C), lambda i: (i, 0))
    return pl.pallas_call(
        body, grid=(R // tr,), in_specs=[sp, sp, sp, pl.BlockSpec((tr, C), lambda i: (g0 + i, 0))],
        out_specs=[sp] * 4, out_shape=[jax.ShapeDtypeStruct((R, C), F32)] * 4,
        compiler_params=_cparams(("arbitrary",)), name=name)(w, m, v, g_arr)


def _place():
    x, y, c = lax.axis_index("x"), lax.axis_index("y"), lax.axis_index("c")
    chips = [(1 - x, y), (x, 1 - y), (1 - x, 1 - y)]
    return x, y, c, chips


ANY = pl.BlockSpec(memory_space=pl.ANY)


def allgather_weights(bufs, *, name):
    n = len(bufs)

    def body(*refs):
        o_refs = refs[n:2 * n]
        send_sems, recv_sems, fsend_sems, frecv_sems = refs[2 * n:]
        x, y, c, chips = _place()
        k_me = 2 * x + y
        sib = (x, y, 1 - c)
        halves = [b.shape[1] // 2 for b in bufs]

        def half(a, cc):
            return pl.ds(pl.multiple_of(cc * halves[a], 16), halves[a])

        sends = []
        for a in range(n):
            for r, (px, py) in enumerate(chips):
                cp = pltpu.make_async_remote_copy(
                    src_ref=o_refs[a].at[k_me, half(a, c)], dst_ref=o_refs[a].at[k_me, half(a, c)],
                    send_sem=send_sems.at[3 * a + r], recv_sem=recv_sems.at[3 * a + r],
                    device_id=(px, py, c), device_id_type=MESH)
                cp.start()
                sends.append(cp)
        passed = []
        for a in range(n):
            for r, (px, py) in enumerate(chips):
                win = o_refs[a].at[2 * px + py, half(a, c)]
                pltpu.make_async_remote_copy(
                    src_ref=win, dst_ref=win, send_sem=send_sems.at[3 * a + r], recv_sem=recv_sems.at[3 * a + r],
                    device_id=(px, py, c), device_id_type=MESH).wait_recv()
                cp = pltpu.make_async_remote_copy(
                    src_ref=win, dst_ref=win, send_sem=fsend_sems.at[3 * a + r], recv_sem=frecv_sems.at[3 * a + r],
                    device_id=sib, device_id_type=MESH)
                cp.start()
                passed.append(cp)
        for a in range(n):
            for r, (px, py) in enumerate(chips):
                win = o_refs[a].at[2 * px + py, half(a, 1 - c)]
                pltpu.make_async_remote_copy(
                    src_ref=win, dst_ref=win, send_sem=fsend_sems.at[3 * a + r], recv_sem=frecv_sems.at[3 * a + r],
                    device_id=sib, device_id_type=MESH).wait_recv()
        for cp in sends + passed:
            cp.wait_send()

    return pl.pallas_call(
        body, in_specs=[ANY] * n, out_specs=[ANY] * n,
        out_shape=[jax.ShapeDtypeStruct(b.shape, b.dtype) for b in bufs],
        scratch_shapes=[pltpu.SemaphoreType.DMA((3 * n,))] * 4,
        input_output_aliases={a: a for a in range(n)},
        name=name)(*bufs)


HBM_SPEC = pl.BlockSpec(memory_space=pltpu.HBM)
SEM_SPEC = pl.BlockSpec(memory_space=pltpu.SEMAPHORE)


def _hbm(a):
    return pltpu.with_memory_space_constraint(a, pltpu.HBM)


def gather_start(bufs, pieces, *, name):
    n = len(bufs)
    npc = len(pieces)

    def body(*refs):
        b_refs = refs[:n]
        send_sems, recv_sems = refs[n], refs[n + 1]
        x, y, c, chips = _place()
        k_me = 2 * x + y
        for q, (a, row0, rows) in enumerate(pieces):
            win = b_refs[a].at[k_me, pl.ds(row0, rows)]
            for r, (px, py) in enumerate(chips):
                pltpu.make_async_remote_copy(
                    src_ref=win, dst_ref=win, send_sem=send_sems.at[3 * q + r], recv_sem=recv_sems.at[3 * q + r],
                    device_id=(px, py, c), device_id_type=MESH).start()

    return pl.pallas_call(
        body, in_specs=[HBM_SPEC] * n, out_specs=[SEM_SPEC, SEM_SPEC] + [HBM_SPEC] * n,
        out_shape=[pltpu.SemaphoreType.DMA((3 * npc,)), pltpu.SemaphoreType.DMA((3 * npc,))]
        + [pltpu.HBM(b.shape, b.dtype) for b in bufs],
        input_output_aliases={a: 2 + a for a in range(n)},
        compiler_params=pltpu.CompilerParams(has_side_effects=pltpu.SideEffectType.DATAFLOW_SIDE_EFFECTING),
        name=name)(*[_hbm(b) for b in bufs])


def gather_wait(send_sems, recv_sems, bufs, which, after, *, name):
    n = len(bufs)

    def body(*refs):
        b_refs = refs[:n]
        send_sems, recv_sems = refs[n], refs[n + 1]
        x, y, c, chips = _place()
        k_me = 2 * x + y
        for a, row0, rows, q in which:
            for r, (px, py) in enumerate(chips):
                cp = pltpu.make_async_remote_copy(
                    src_ref=b_refs[a].at[k_me, pl.ds(row0, rows)], dst_ref=b_refs[a].at[2 * px + py, pl.ds(row0, rows)],
                    send_sem=send_sems.at[3 * q + r], recv_sem=recv_sems.at[3 * q + r],
                    device_id=(px, py, c), device_id_type=MESH)
                cp.wait_send()
                cp.wait_recv()

    return pl.pallas_call(
        body, in_specs=[HBM_SPEC] * n + [SEM_SPEC, SEM_SPEC, ANY], out_specs=[HBM_SPEC] * n,
        out_shape=[pltpu.HBM(b.shape, b.dtype) for b in bufs],
        input_output_aliases={a: a for a in range(n)},
        compiler_params=pltpu.CompilerParams(has_side_effects=pltpu.SideEffectType.DATAFLOW_SIDE_EFFECTING),
        name=name)(*bufs, send_sems, recv_sems, after)


def exchange_halves(grads, *, name):
    n = len(grads)

    def body(*refs):
        g_refs, l_refs = refs[:n], refs[n:2 * n]
        send_sems, recv_sems = refs[2 * n:]
        x, y, c, _ = _place()
        cps = []
        for a in range(n):
            h = grads[a].shape[1] // 2
            cp = pltpu.make_async_remote_copy(
                src_ref=g_refs[a].at[:, pl.ds(pl.multiple_of((1 - c) * h, 8), h)], dst_ref=l_refs[a],
                send_sem=send_sems.at[a], recv_sem=recv_sems.at[a], device_id=(x, y, 1 - c), device_id_type=MESH)
            cp.start()
            cps.append(cp)
        for cp in cps:
            cp.wait()

    return pl.pallas_call(
        body, in_specs=[ANY] * n, out_specs=[ANY] * n,
        out_shape=[jax.ShapeDtypeStruct((g.shape[0], g.shape[1] // 2, g.shape[2]), g.dtype) for g in grads],
        scratch_shapes=[pltpu.SemaphoreType.DMA((n,))] * 2,
        name=name)(*grads)


def scatter_start(parts, *, name):
    n = len(parts)
    lands = [lax.empty((3,) + p.shape[1:], p.dtype) for p in parts]

    def body(*refs):
        p_refs, l_refs = refs[:n], refs[n:2 * n]
        send_sems, recv_sems = refs[2 * n], refs[2 * n + 1]
        x, y, c, chips = _place()
        for a in range(n):
            for r, (px, py) in enumerate(chips):
                pltpu.make_async_remote_copy(
                    src_ref=p_refs[a].at[2 * px + py], dst_ref=l_refs[a].at[r],
                    send_sem=send_sems.at[3 * a + r], recv_sem=recv_sems.at[3 * a + r],
                    device_id=(px, py, c), device_id_type=MESH).start()

    outs = pl.pallas_call(
        body, in_specs=[HBM_SPEC] * (2 * n), out_specs=[SEM_SPEC, SEM_SPEC] + [HBM_SPEC] * (2 * n),
        out_shape=[pltpu.SemaphoreType.DMA((3 * n,)), pltpu.SemaphoreType.DMA((3 * n,))]
        + [pltpu.HBM(a.shape, a.dtype) for a in parts + lands],
        input_output_aliases={a: 2 + a for a in range(2 * n)},
        compiler_params=pltpu.CompilerParams(has_side_effects=pltpu.SideEffectType.DATAFLOW_SIDE_EFFECTING),
        name=name)(*[_hbm(a) for a in parts + lands])
    return outs[0], outs[1], list(outs[2:2 + n]), list(outs[2 + n:])


def scatter_wait(rounds, after, *, name):
    sizes = [len(r[2]) for r in rounds]
    flat = [a for r in rounds for a in r[2] + r[3]]
    sems = [s for r in rounds for s in (r[0], r[1])]
    nflat = len(flat)

    def body(*refs):
        x, y, c, chips = _place()
        pos = 0
        for ri, n in enumerate(sizes):
            p_refs, l_refs = refs[pos:pos + n], refs[pos + n:pos + 2 * n]
            send_sems, recv_sems = refs[nflat + 2 * ri], refs[nflat + 2 * ri + 1]
            for a in range(n):
                for r, (px, py) in enumerate(chips):
                    cp = pltpu.make_async_remote_copy(
                        src_ref=p_refs[a].at[2 * px + py], dst_ref=l_refs[a].at[r],
                        send_sem=send_sems.at[3 * a + r], recv_sem=recv_sems.at[3 * a + r],
                        device_id=(px, py, c), device_id_type=MESH)
                    cp.wait_send()
                    cp.wait_recv()
            pos += 2 * n

    outs = pl.pallas_call(
        body, in_specs=[HBM_SPEC] * nflat + [SEM_SPEC] * len(sems) + [ANY], out_specs=[HBM_SPEC] * nflat,
        out_shape=[pltpu.HBM(a.shape, a.dtype) for a in flat],
        input_output_aliases={a: a for a in range(nflat)},
        compiler_params=pltpu.CompilerParams(has_side_effects=pltpu.SideEffectType.DATAFLOW_SIDE_EFFECTING),
        name=name)(*flat, *sems, after)
    res, pos = [], 0
    for n in sizes:
        res.append((list(outs[pos:pos + n]), list(outs[pos + n:pos + 2 * n])))
        pos += 2 * n
    return res


def join_halves(fulls, *, name):
    n = len(fulls)

    def body(*refs):
        o_refs = refs[n:2 * n]
        send_sems, recv_sems = refs[2 * n:]
        x, y, c, _ = _place()
        cps = []
        for a in range(n):
            h = fulls[a].shape[0] // 2
            win = o_refs[a].at[pl.ds(pl.multiple_of(c * h, 8), h)]
            cp = pltpu.make_async_remote_copy(
                src_ref=win, dst_ref=win, send_sem=send_sems.at[a], recv_sem=recv_sems.at[a],
                device_id=(x, y, 1 - c), device_id_type=MESH)
            cp.start()
            cps.append(cp)
        for a in range(n):
            h = fulls[a].shape[0] // 2
            other = o_refs[a].at[pl.ds(pl.multiple_of((1 - c) * h, 8), h)]
            pltpu.make_async_remote_copy(
                src_ref=other, dst_ref=other, send_sem=send_sems.at[a], recv_sem=recv_sems.at[a],
                device_id=(x, y, 1 - c), device_id_type=MESH).wait_recv()
        for cp in cps:
            cp.wait_send()

    return pl.pallas_call(
        body, in_specs=[ANY] * n, out_specs=[ANY] * n,
        out_shape=[jax.ShapeDtypeStruct(f.shape, f.dtype) for f in fulls],
        scratch_shapes=[pltpu.SemaphoreType.DMA((n,))] * 2,
        input_output_aliases={a: a for a in range(n)},
        name=name)(*fulls)


def add_sibling(g, l, c, *, name):
    nb, R, C = g.shape
    h = R // 2
    tr = _pick(h, [256, 128, 64, 32, 16])
    per = h // tr

    def body(c_ref, g_ref, l_ref, o_ref):
        o_ref[...] = (g_ref[...] + l_ref[...]).astype(BF16)

    return pl.pallas_call(
        body,
        grid_spec=pltpu.PrefetchScalarGridSpec(
            num_scalar_prefetch=1, grid=(nb, per),
            in_specs=[pl.BlockSpec((None, tr, C), lambda k, i, c_ref: (k, c_ref[0] * per + i, 0)),
                      pl.BlockSpec((None, tr, C), lambda k, i, c_ref: (k, i, 0))],
            out_specs=pl.BlockSpec((None, tr, C), lambda k, i, c_ref: (k, i, 0))),
        out_shape=jax.ShapeDtypeStruct((nb, h, C), BF16),
        compiler_params=_cparams(("arbitrary", "arbitrary")), name=name)(c, g, l)


def add_chips(part, land, kc, *, name):
    _, H, C = part.shape
    tr = _pick(H, [256, 128, 64, 32, 16])
    per = H // tr

    def body(kc_ref, p_ref, l_ref, o_ref):
        o_ref[...] = ((p_ref[...].astype(F32) + l_ref[0].astype(F32)) + l_ref[1].astype(F32)) + l_ref[2].astype(F32)

    return pl.pallas_call(
        body,
        grid_spec=pltpu.PrefetchScalarGridSpec(
            num_scalar_prefetch=1, grid=(per,),
            in_specs=[pl.BlockSpec((None, tr, C), lambda i, kc_ref: (kc_ref[0], i, 0)),
                      pl.BlockSpec((3, tr, C), lambda i, kc_ref: (0, i, 0))],
            out_specs=pl.BlockSpec((tr, C), lambda i, kc_ref: (kc_ref[1] * per + i, 0))),
        out_shape=jax.ShapeDtypeStruct((2 * H, C), F32),
        compiler_params=_cparams(("arbitrary",)), name=name)(kc, part, land)


def allgather_sum(v, *, name):
    m_per, n = v.shape

    def body(x_ref, out_ref, sum_ref, send_sems, recv_sems, local_sem):
        x, y, c, chips = _place()
        me, sibling = (x, y, c), (x, y, 1 - c)

        def rows(px, py, pc):
            return out_ref.at[pl.ds(pl.multiple_of((4 * px + 2 * py + pc) * m_per, 8), m_per), :]

        def copy(k, block, to, src=None):
            return pltpu.make_async_remote_copy(
                src_ref=rows(*block) if src is None else src, dst_ref=rows(*block),
                send_sem=send_sems.at[k], recv_sem=recv_sems.at[k], device_id=to, device_id_type=MESH)

        mine = pltpu.make_async_copy(x_ref, rows(*me), local_sem)
        mine.start()
        first = [copy(0, me, sibling, src=x_ref)]
        first += [copy(1 + j, me, (*chip, c), src=x_ref) for j, chip in enumerate(chips)]
        for cp in first:
            cp.start()
        passed = [copy(4 + j, (*chip, c), sibling) for j, chip in enumerate(chips)]
        for j, chip in enumerate(chips):
            copy(1 + j, (*chip, c), me).wait_recv()
            passed[j].start()
        copy(0, sibling, me).wait_recv()
        for j, chip in enumerate(chips):
            copy(4 + j, (*chip, 1 - c), me).wait_recv()
        for cp in first + passed:
            cp.wait_send()
        mine.wait()
        acc = out_ref[0:m_per, :]
        for d in range(1, N_DEV):
            acc = acc + out_ref[d * m_per:(d + 1) * m_per, :]
        sum_ref[...] = acc

    vm = pl.BlockSpec(memory_space=pltpu.VMEM)
    return pl.pallas_call(
        body, in_specs=[vm], out_specs=[vm, vm],
        out_shape=[jax.ShapeDtypeStruct((N_DEV * m_per, n), v.dtype), jax.ShapeDtypeStruct((m_per, n), v.dtype)],
        scratch_shapes=[pltpu.SemaphoreType.DMA((7,)), pltpu.SemaphoreType.DMA((7,)), pltpu.SemaphoreType.DMA],
        compiler_params=pltpu.CompilerParams(vmem_limit_bytes=VMEM_LIMIT_BYTES), name=name)(v)


def _pack(arrs):
    cols = []
    for a in arrs:
        f = a.reshape(-1)
        pad = (-f.shape[0]) % 128
        cols.append(jnp.pad(f, (0, pad)).reshape(-1, 128))
    out = jnp.concatenate(cols, axis=0)
    pad = (-out.shape[0]) % 8
    return jnp.pad(out, ((0, pad), (0, 0)))


def _unpack(buf, shapes):
    outs, r = [], 0
    for s in shapes:
        nel = math.prod(s)
        nr = -(-nel // 128)
        outs.append(buf[r:r + nr].reshape(-1)[:nel].reshape(s))
        r += nr
    return outs


GA_CONV_OUT, GA_MIX_OUT, GA_WQ, GA_WO, GA_DOWN, GA_UP, GA_ROWS = 0, 256, 512, 768, 1024, 2048, 3072
G1_DOWN, G1_UP, G1_ROWS = 0, 1024, 2048
G2_CONV_OUT, G2_MIX_OUT, G2_WQ, G2_WO, G2_ROWS = 0, 256, 512, 768, 1024


def kernel(x, mem, in_norm_g, in_norm_b, w_in, conv_dw, conv_db, conv_norm_g, conv_norm_b, w_conv_out, ssm_log_step, ssm_lambda_re, ssm_lambda_im, ssm_b_re, ssm_b_im, ssm_c_re, ssm_c_im, ssm_d, w_ssm_glu, w_mix_out, ln1_g, ln1_b, xa_wq, xa_wkv, xa_wo, ln2_g, ln2_b, mlp_w_up, mlp_w_down, ln3_g, ln3_b, loss_target, m_in_norm_g, m_in_norm_b, m_w_in, m_conv_dw, m_conv_db, m_conv_norm_g, m_conv_norm_b, m_w_conv_out, m_ssm_log_step, m_ssm_lambda_re, m_ssm_lambda_im, m_ssm_b_re, m_ssm_b_im, m_ssm_c_re, m_ssm_c_im, m_ssm_d, m_w_ssm_glu, m_w_mix_out, m_ln1_g, m_ln1_b, m_xa_wq, m_xa_wkv, m_xa_wo, m_ln2_g, m_ln2_b, m_mlp_w_up, m_mlp_w_down, m_ln3_g, m_ln3_b, v_in_norm_g, v_in_norm_b, v_w_in, v_conv_dw, v_conv_db, v_conv_norm_g, v_conv_norm_b, v_w_conv_out, v_ssm_log_step, v_ssm_lambda_re, v_ssm_lambda_im, v_ssm_b_re, v_ssm_b_im, v_ssm_c_re, v_ssm_c_im, v_ssm_d, v_w_ssm_glu, v_w_mix_out, v_ln1_g, v_ln1_b, v_xa_wq, v_xa_wkv, v_xa_wo, v_ln2_g, v_ln2_b, v_mlp_w_up, v_mlp_w_down, v_ln3_g, v_ln3_b):
    D = D_MODEL
    xs = x[0]
    T = xs.shape[0]
    mems = mem[0]
    NM = mems.shape[0]
    tgt = loss_target[0]
    my_c = lax.axis_index("c")
    k_me = 2 * lax.axis_index("x") + lax.axis_index("y")
    c_arr = jnp.reshape(my_c, (1,)).astype(jnp.int32)
    k_arr = jnp.reshape(k_me, (1,)).astype(jnp.int32)

    sh_a = jnp.concatenate([w_conv_out[0], w_mix_out[0], xa_wq[0], xa_wo[0], mlp_w_down[0], mlp_w_up[0]], axis=0).astype(BF16)
    def own_block(shard):
        buf = jnp.zeros((N_CHIPS,) + shard.shape, shard.dtype)
        return lax.dynamic_update_slice(buf, shard[None], (k_me, 0, 0))

    dw_pad = jnp.pad(conv_dw[0], ((0, CONV_HALO - CONV_K), (0, 0)))
    ag_bufs = [own_block(s) for s in (sh_a, w_in[0].astype(BF16), xa_wkv[0].astype(BF16), w_ssm_glu[0].astype(BF16), dw_pad)]
    ag_pieces = [(1, 0, D), (4, 0, CONV_HALO), (0, GA_CONV_OUT, 256), (3, 0, D_SSM), (0, GA_MIX_OUT, 256), (0, GA_WQ, 256),
                 (2, 0, D), (0, GA_WO, 256), (0, GA_UP, D), (0, GA_DOWN, D)]
    ag_send, ag_recv, GA, GIN, GKV, GGLU, GDW = gather_start(ag_bufs, ag_pieces, name="gather_start")

    def w_rowshard(row0):
        return dict(b_spec=((N_CHIPS, 256, D), lambda i, j, k: (0, row0 // 256, 0)), b_view=(D, D), tn=D, tk=D)

    _, h0, h0b = ln_fwd(xs, in_norm_g, in_norm_b, name="ln0_fwd")
    (GIN,) = gather_wait(ag_send, ag_recv, [GIN], [(0, 0, D, 0)], h0b, name="gather_wait_in")
    p = mm_nn(h0b, GIN, ((None, D, 1152), lambda i, j, k: (j, 0, 0)), D_IN, tn=1152, tk=D, name="mm_w_in")[0]
    GA, GGLU, GDW = gather_wait(
        ag_send, ag_recv, [GA, GGLU, GDW],
        [(2, 0, CONV_HALO, 1), (0, GA_CONV_OUT, 256, 2), (1, 0, D_SSM, 3), (0, GA_MIX_OUT, 256, 4)], p, name="gather_wait_mixer")
    dw_full = jnp.transpose(GDW, (1, 0, 2)).reshape(CONV_HALO, D)
    c_pre, actb = conv_fwd(p, dw_full, conv_db, conv_norm_g[0].reshape(1, D), conv_norm_b[0].reshape(1, D), name="conv_fwd")
    ya = mm_nn(actb, GA, N=D, name="mm_conv_out", **w_rowshard(GA_CONV_OUT))[0]

    lstep, lre, lim = ssm_log_step[0], ssm_lambda_re[0], ssm_lambda_im[0]
    bre, bim, cre, cim = ssm_b_re[0], ssm_b_im[0], ssm_c_re[0], ssm_c_im[0]
    (ar, ai, bbr, bbi), disc_vjp = jax.vjp(_ssm_discretise, lstep, lre, lim, bre, bim)
    Br, Bi = _blockdiag_in(bbr), _blockdiag_in(bbi)
    Cr, Ci = _blockdiag_out(cre), _blockdiag_out(cim)
    pw_r, pw_i = _powers(ar.reshape(-1), ai.reshape(-1))
    dvec = ssm_d[0].reshape(1, D_SSM)
    xr, xi, yssm = ssm_fwd(p, Br, Bi, Cr, Ci, pw_r, pw_i, dvec, name="ssm_fwd")
    z = mm_nn(yssm, GGLU, ((None, D_SSM, 512), lambda i, j, k: (j, 0, 0)), 2 * D, tn=512, tk=D_SSM, name="mm_ssm_glu")[0]
    mergedb = merge_fwd(p, ya, z, name="merge_fwd")
    mix = mm_nn(mergedb, GA, N=D, name="mm_mix_out", **w_rowshard(GA_MIX_OUT))[0]
    r1, h1, h1b = ln_fwd(mix, ln1_g[0], ln1_b[0], res=h0, name="ln1_fwd")
    GA, GKV = gather_wait(ag_send, ag_recv, [GA, GKV], [(0, GA_WQ, 256, 5), (1, 0, D, 6), (0, GA_WO, 256, 7)], mix,
                          name="gather_wait_attn")

    qb = mm_nn(h1b, GA, N=D, out_dtype=BF16, name="mm_wq", **w_rowshard(GA_WQ))[0]
    kv = mm_nn(mems, GKV, ((None, D, 512), lambda i, j, k: (j, 0, 0)), 2 * D, tn=512, tk=D, name="mm_wkv")[0]
    ob = attn_fwd(qb, kv, name="attn_fwd")
    xa = mm_nn(ob, GA, N=D, name="mm_wo", **w_rowshard(GA_WO))[0]
    r2, h2, h2b = ln_fwd(xa, ln2_g[0], ln2_b[0], res=h1, name="ln2_fwd")
    (GA,) = gather_wait(ag_send, ag_recv, [GA], [(0, GA_UP, D, 8), (0, GA_DOWN, D, 9)], xa, name="gather_wait_mlp")

    def relu2(acc):
        zr = jnp.maximum(acc, 0.0)
        return acc, zr * zr

    zpre, zzb = mm_nn(h2b, GA, ((None, D, D), lambda i, j, k: (j, GA_UP // D, 0)), D_FF, tn=D, tk=D,
                      out_dtypes=[F32, BF16], epilogue=relu2, name="mm_up")
    ff = mm_nn(zzb, GA, ((None, D, D), lambda i, j, k: (k, GA_DOWN // D, 0)), D, tn=D, tk=D, name="mm_down")[0]
    r3, h3, _ = ln_fwd(ff, ln3_g[0], ln3_b[0], res=h2, name="ln3_fwd")
    dh3, sq = loss_head(h3, tgt, name="loss_head")
    loss = lax.psum(0.5 * sq[0, 0] / D, ("x", "y", "c"))

    def rs_begin(grads, tags, rnd):
        lands = exchange_halves(grads, name=f"rs{rnd}_exchange_halves")
        parts = [add_sibling(g, l, c_arr, name=f"rs{rnd}_add_sibling_{t}") for g, l, t in zip(grads, lands, tags)]
        return scatter_start(parts, name=f"rs{rnd}_scatter_start")

    g1_shape = jax.ShapeDtypeStruct((N_CHIPS, G1_ROWS, D), F32)
    g2_shape = jax.ShapeDtypeStruct((N_CHIPS, G2_ROWS, D), F32)
    dr3, dr3b, dg3, db3 = ln_bwd(r3, dh3, ln3_g[0], name="ln3_bwd")
    dzpreb = mm_nt(dr3b, GA, ((None, D, D), lambda i, j, k: (j, GA_DOWN // D, 0)), D_FF, tn=D, tk=D, out_dtype=BF16,
                   extras=(zpre,), epilogue=lambda acc, zp: (acc * (2.0 * jnp.maximum(zp, 0.0)),), name="mm_down_t")[0]
    G1g = mm_tn(zzb, dr3b, tm=D, tn=D, out_spec=((None, D, D), lambda i, j, k: (i, G1_DOWN // D, 0)), out_shape=g1_shape,
                name="mm_down_g")
    G1g = mm_tn(h2b, dzpreb, tm=D, tn=D, out_spec=((None, D, D), lambda i, j, k: (j, G1_UP // D, 0)), out_shape=g1_shape,
                out_buf=G1g, name="mm_up_g")
    round1 = rs_begin([G1g], ["mlp"], 1)
    dh2 = mm_nt(dzpreb, GA, ((None, D, D), lambda i, j, k: (k, GA_UP // D, 0)), D, tn=D, tk=D,
                extras=(dr3,), epilogue=lambda acc, d: (acc + ALPHA * d,), name="mm_up_t")[0]
    dr2, dr2b, dg2, db2 = ln_bwd(r2, dh2, ln2_g[0], name="ln2_bwd")

    def g_rowshard(row0, out_buf):
        return dict(tm=D, tn=D, out_spec=((N_CHIPS, 256, D), lambda i, j, k: (0, row0 // 256, 0)), out_shape=g2_shape,
                    out_buf=out_buf)

    dob = mm_nt(dr2b, GA, N=D, out_dtype=BF16, name="mm_wo_t", **w_rowshard(GA_WO))[0]
    G2g = mm_tn(ob, dr2b, name="mm_wo_g", **g_rowshard(G2_WO, None))
    dqb, dkv = attn_bwd(qb, kv, dob, name="attn_bwd")
    G2g = mm_tn(h1b, dqb, name="mm_wq_g", **g_rowshard(G2_WQ, G2g))
    GKVg = mm_tn(mems, dkv, tm=D, tn=512, tk=NM, out_spec=((None, D, 512), lambda i, j, k: (j, 0, 0)),
                 out_shape=jax.ShapeDtypeStruct((N_CHIPS, D, 512), F32), name="mm_wkv_g")
    dh1 = mm_nt(dqb, GA, N=D, extras=(dr2,), epilogue=lambda acc, d: (acc + ALPHA * d,), name="mm_wq_t",
                **w_rowshard(GA_WQ))[0]
    dr1, dr1b, dg1, db1 = ln_bwd(r1, dh1, ln1_g[0], name="ln1_bwd")

    dmerged = mm_nt(dr1b, GA, N=D, name="mm_mix_t", **w_rowshard(GA_MIX_OUT))[0]
    G2g = mm_tn(mergedb, dr1b, name="mm_mix_g", **g_rowshard(G2_MIX_OUT, G2g))
    dyab, dgab, dgbb, dz1b, dz2b = merge_bwd(dmerged, p, ya, z, name="merge_bwd")
    dzb = jnp.concatenate([dz1b, dz2b], axis=1)
    GGLUg = mm_tn(yssm, dzb, tm=D_SSM, tn=512, out_spec=((None, D_SSM, 512), lambda i, j, k: (j, 0, 0)),
                  out_shape=jax.ShapeDtypeStruct((N_CHIPS, D_SSM, 512), F32), name="mm_glu_g")
    dyssm = mm_nt(dzb, GGLU, ((None, D_SSM, 512), lambda i, j, k: (k, 0, 0)), D_SSM, tn=D_SSM, tk=512, name="mm_glu_t")[0]
    pwc_r, pwc_i = pw_r, -pw_i
    dub, dBr, dBi, dCr, dCi, dar8, dai8, dd8 = ssm_bwd(dyssm, p, xr, xi, Br, Bi, Cr, Ci, pwc_r, pwc_i, dvec, name="ssm_bwd")
    dar = jnp.sum(dar8, axis=0).reshape(SSM_GROUPS, SSM_STATE)
    dai = jnp.sum(dai8, axis=0).reshape(SSM_GROUPS, SSM_STATE)
    g_lstep, g_lre, g_lim, g_bre, g_bim = disc_vjp((dar, dai, _diag_in(dBr), _diag_in(dBi)))
    g_cre, g_cim = _diag_out(dCr), _diag_out(dCi)
    g_d = jnp.sum(dd8, axis=0).reshape(1, D_SSM)

    dact = mm_nt(dyab, GA, N=D, name="mm_conv_out_t", **w_rowshard(GA_CONV_OUT))[0]
    G2g = mm_tn(actb, dyab, name="mm_conv_out_g", **g_rowshard(G2_CONV_OUT, G2g))
    round2 = rs_begin([G2g, GKVg, GGLUg], ["sq", "kv", "glu"], 2)
    dc, dng, dnb, ddb = conv_bwd_norm(dact, c_pre, conv_norm_g[0].reshape(1, D), conv_norm_b[0].reshape(1, D), name="conv_bwd_norm")
    dvgb, ddw = conv_bwd_taps(dc, p, dw_full, name="conv_bwd_taps")
    dpb = jnp.concatenate([dvgb, dub, dgab, dgbb], axis=1)
    GINg = mm_tn(h0b, dpb, tm=D, tn=1152, out_spec=((None, D, 1152), lambda i, j, k: (j, 0, 0)),
                 out_shape=jax.ShapeDtypeStruct((N_CHIPS, D, 1152), F32), name="mm_w_in_g")
    round3 = rs_begin([GINg], ["in"], 3)
    dh0 = mm_nt(dpb, GIN, ((None, 512, 1152), lambda i, j, k: (k, j, 0)), D, tn=512, tk=1152,
                extras=(dr1,), epilogue=lambda acc, d: (acc + ALPHA * d,), name="mm_w_in_t")[0]
    gx, _, dg0, db0 = ln_bwd(xs, dh0, in_norm_g, name="ln0_bwd")

    kc_arr = jnp.concatenate([k_arr, c_arr])
    landed = scatter_wait([round1, round2, round3], gx, name="rs_scatter_wait")
    tags = ["mlp", "sq", "kv", "glu", "in"]
    pairs = [(pt, l2) for parts, lands2 in landed for pt, l2 in zip(parts, lands2)]
    halves = [add_chips(pt, l2, kc_arr, name="rs_add_chips_" + t) for (pt, l2), t in zip(pairs, tags)]
    g1, g2, gKV, gGLU, gIN = join_halves(halves, name="rs_join_halves")

    small_names = ["in_norm_g", "in_norm_b", "conv_db", "conv_norm_g", "conv_norm_b", "ssm_log_step", "ssm_lambda_re",
                   "ssm_lambda_im", "ssm_b_re", "ssm_b_im", "ssm_c_re", "ssm_c_im", "ssm_d", "ln1_g", "ln1_b",
                   "ln2_g", "ln2_b", "ln3_g", "ln3_b"]
    small_w = [in_norm_g, in_norm_b, conv_db, conv_norm_g, conv_norm_b, ssm_log_step, ssm_lambda_re, ssm_lambda_im,
               ssm_b_re, ssm_b_im, ssm_c_re, ssm_c_im, ssm_d, ln1_g, ln1_b, ln2_g, ln2_b, ln3_g, ln3_b]
    small_m = [m_in_norm_g, m_in_norm_b, m_conv_db, m_conv_norm_g, m_conv_norm_b, m_ssm_log_step, m_ssm_lambda_re,
               m_ssm_lambda_im, m_ssm_b_re, m_ssm_b_im, m_ssm_c_re, m_ssm_c_im, m_ssm_d, m_ln1_g, m_ln1_b, m_ln2_g,
               m_ln2_b, m_ln3_g, m_ln3_b]
    small_v = [v_in_norm_g, v_in_norm_b, v_conv_db, v_conv_norm_g, v_conv_norm_b, v_ssm_log_step, v_ssm_lambda_re,
               v_ssm_lambda_im, v_ssm_b_re, v_ssm_b_im, v_ssm_c_re, v_ssm_c_im, v_ssm_d, v_ln1_g, v_ln1_b, v_ln2_g,
               v_ln2_b, v_ln3_g, v_ln3_b]
    small_g = [dg0, db0, ddb, dng, dnb, g_lstep, g_lre, g_lim, g_bre, g_bim, g_cre, g_cim, g_d, dg1, db1, dg2, db2, dg3, db3]
    small_shapes = [w.shape for w in small_w]
    n_small_rows = _pack(small_w).shape[0]
    packed_g = _pack(small_g + [ddw])
    _, summed = allgather_sum(packed_g, name="allreduce_small")
    small_rows = sum(-(-math.prod(s) // 128) for s in small_shapes)
    ddw_full = summed[small_rows:small_rows + CONV_HALO * D // 128].reshape(CONV_HALO, D)
    g_dw = lax.dynamic_slice_in_dim(ddw_full, k_me * (D // N_CHIPS), D // N_CHIPS, axis=1)
    gs_packed = jnp.pad(summed[:small_rows], ((0, n_small_rows - small_rows), (0, 0)))

    res = {}

    def upd(nm, w, m, v, g_arr, row0=0):
        shp = w.shape
        w2, m2, v2 = (a.reshape(-1, shp[-1]) for a in (w, m, v))
        outs = adamw(w2, m2, v2, g_arr, row0, name="adamw_" + nm)
        res[nm] = tuple(o.reshape(shp) for o in outs)

    upd("w_conv_out", w_conv_out, m_w_conv_out, v_w_conv_out, g2, G2_CONV_OUT)
    upd("w_mix_out", w_mix_out, m_w_mix_out, v_w_mix_out, g2, G2_MIX_OUT)
    upd("xa_wq", xa_wq, m_xa_wq, v_xa_wq, g2, G2_WQ)
    upd("xa_wo", xa_wo, m_xa_wo, v_xa_wo, g2, G2_WO)
    upd("mlp_w_down", mlp_w_down, m_mlp_w_down, v_mlp_w_down, g1, G1_DOWN)
    upd("mlp_w_up", mlp_w_up, m_mlp_w_up, v_mlp_w_up, g1, G1_UP)
    upd("w_in", w_in, m_w_in, v_w_in, gIN)
    upd("xa_wkv", xa_wkv, m_xa_wkv, v_xa_wkv, gKV)
    upd("w_ssm_glu", w_ssm_glu, m_w_ssm_glu, v_w_ssm_glu, gGLU)
    pad_dw = lambda a: jnp.pad(a[0], ((0, CONV_HALO - CONV_K), (0, 0)))
    dw_outs = adamw(pad_dw(conv_dw), pad_dw(m_conv_dw), pad_dw(v_conv_dw), g_dw, 0, name="adamw_conv_dw")
    res["conv_dw"] = tuple(o[:CONV_K][None] for o in dw_outs)
    sm_outs = adamw(_pack(small_w), _pack(small_m), _pack(small_v), gs_packed, 0, name="adamw_small")
    sm_un = [_unpack(o, small_shapes) for o in sm_outs]
    for idx, nm in enumerate(small_names):
        res[nm] = tuple(sm_un[q][idx] for q in range(4))

    order = ["in_norm_g", "in_norm_b", "w_in", "conv_dw", "conv_db", "conv_norm_g", "conv_norm_b", "w_conv_out",
             "ssm_log_step", "ssm_lambda_re", "ssm_lambda_im", "ssm_b_re", "ssm_b_im", "ssm_c_re", "ssm_c_im", "ssm_d",
             "w_ssm_glu", "w_mix_out", "ln1_g", "ln1_b", "xa_wq", "xa_wkv", "xa_wo", "ln2_g", "ln2_b", "mlp_w_up",
             "mlp_w_down", "ln3_g", "ln3_b"]
    return (loss, gx[None], *[res[n][0] for n in order], *[res[n][1] for n in order],
            *[res[n][2] for n in order], *[res[n][3] for n in order])
```

```python
import functools
import math

import jax
import jax.numpy as jnp
from jax import lax
from jax.experimental import pallas as pl
from jax.experimental.pallas import tpu as pltpu

F32 = jnp.float32
BF16 = jnp.bfloat16
MESH = pl.DeviceIdType.MESH

D_MODEL = 1024
N_HEADS = 4
HEAD_DIM = D_MODEL // N_HEADS
CONV_K = 31
CONV_HALO = 32
D_SSM = 512
SSM_GROUPS = 32
SSM_GROUP = 16
SSM_STATE = 64
SSM_BLOCKS = 4
SSM_BLOCK_IN = D_SSM // SSM_BLOCKS
SSM_BLOCK_STATE = SSM_GROUPS * SSM_STATE // SSM_BLOCKS
D_FF = 4096
D_IN = 4608
LN_EPS = 1e-5
ALPHA = (2.0 * 1) ** 0.25
N_CHIPS = 4
N_DEV = 8
ADAM_LR, ADAM_B1, ADAM_B2, ADAM_EPS, ADAM_WD, ADAM_STEP = 0.001, 0.9, 0.999, 1e-08, 0.01, 10
VMEM_LIMIT_BYTES = 56 * 1024 * 1024


def _pick(dim, cands):
    for c in cands:
        if dim % c == 0:
            return c
    return dim


def _cparams(sem=None):
    return pltpu.CompilerParams(dimension_semantics=sem, vmem_limit_bytes=VMEM_LIMIT_BYTES)


def _sigmoid(x):
    return 1.0 / (1.0 + jnp.exp(-x))


_DIMS = {"nn": (((1,), (0,)), ((), ())), "nt": (((1,), (1,)), ((), ())), "tn": (((0,), (0,)), ((), ()))}


def matmul(a, b, *, mode, M, N, K, tm, tn, tk, a_spec, b_spec, out_specs, out_shapes, name,
           extras=(), extra_specs=(), epilogue=None, alias_buf=None, b_view=None, after=()):
    nk = K // tk
    ne = len(extras)
    no = len(out_shapes)
    na = (0 if alias_buf is None else 1) + len(after)
    dims = _DIMS[mode]

    def body(*refs):
        a_ref, b_ref = refs[0], refs[1]
        e_refs = refs[2:2 + ne]
        o_refs = refs[2 + ne + na:2 + ne + na + no]

        def finish(acc):
            outs = (acc,) if epilogue is None else epilogue(acc, *[r[...] for r in e_refs])
            for o, r in zip(outs, o_refs):
                r[...] = o.astype(r.dtype).reshape(r.shape)

        b_blk = b_ref[...] if b_view is None else b_ref[...].reshape(b_view)
        prod = lax.dot_general(a_ref[...].astype(BF16), b_blk.astype(BF16), dims, preferred_element_type=F32)
        if nk == 1:
            finish(prod)
        else:
            acc_ref = refs[-1]
            k = pl.program_id(2)

            @pl.when(k == 0)
            def _():
                acc_ref[...] = prod

            @pl.when(k > 0)
            def _():
                acc_ref[...] += prod

            @pl.when(k == nk - 1)
            def _():
                finish(acc_ref[...])

    in_specs = [pl.BlockSpec(*a_spec), pl.BlockSpec(*b_spec)] + [pl.BlockSpec(*s) for s in extra_specs]
    ins = [a, b, *extras]
    if alias_buf is not None:
        in_specs.append(pl.BlockSpec(memory_space=pl.ANY))
        ins.append(alias_buf)
    for dep in after:
        in_specs.append(pl.BlockSpec(memory_space=pl.ANY))
        ins.append(dep)
    res = pl.pallas_call(
        body,
        grid=(M // tm, N // tn, nk),
        in_specs=in_specs,
        out_specs=[pl.BlockSpec(*s) for s in out_specs],
        out_shape=out_shapes,
        scratch_shapes=[] if nk == 1 else [pltpu.VMEM((tm, tn), F32)],
        input_output_aliases={2 + ne: 0} if alias_buf is not None else {},
        compiler_params=_cparams(("parallel", "parallel", "arbitrary")),
        name=name,
    )(*ins)
    return res


def _mn(tm, tn):
    return ((tm, tn), lambda i, j, k: (i, j))


def mm_nn(a, b_arr, b_spec, N, *, name, tm=None, tn, tk, out_dtype=F32, extras=(), epilogue=None, out_dtypes=None,
          b_view=None):
    M, K = a.shape
    tm = tm or _pick(M, [1024, 512, 256, 128])
    dts = out_dtypes or [out_dtype]
    return matmul(a, b_arr, mode="nn", M=M, N=N, K=K, tm=tm, tn=tn, tk=tk,
                  a_spec=((tm, tk), lambda i, j, k: (i, k)), b_spec=b_spec, b_view=b_view,
                  out_specs=[_mn(tm, tn)] * len(dts), out_shapes=[jax.ShapeDtypeStruct((M, N), d) for d in dts],
                  extras=extras, extra_specs=[_mn(tm, tn)] * len(extras), epilogue=epilogue, name=name)


def mm_nt(a, b_arr, b_spec, N, *, name, tm=None, tn, tk, out_dtype=F32, extras=(), epilogue=None, out_dtypes=None,
          b_view=None, after=()):
    M, K = a.shape
    tm = tm or _pick(M, [1024, 512, 256, 128])
    dts = out_dtypes or [out_dtype]
    return matmul(a, b_arr, mode="nt", M=M, N=N, K=K, tm=tm, tn=tn, tk=tk, after=after,
                  a_spec=((tm, tk), lambda i, j, k: (i, k)), b_spec=b_spec, b_view=b_view,
                  out_specs=[_mn(tm, tn)] * len(dts), out_shapes=[jax.ShapeDtypeStruct((M, N), d) for d in dts],
                  extras=extras, extra_specs=[_mn(tm, tn)] * len(extras), epilogue=epilogue, name=name)


def mm_tn(a, b, *, name, tm, tn, tk=None, out_spec, out_shape, out_buf=None):
    K, M = a.shape
    N = b.shape[1]
    tk = tk or _pick(K, [2048, 1024, 512, 256, 128])
    return matmul(a, b, mode="tn", M=M, N=N, K=K, tm=tm, tn=tn, tk=tk,
                  a_spec=((tk, tm), lambda i, j, k: (k, i)), b_spec=((tk, tn), lambda i, j, k: (k, j)),
                  out_specs=[out_spec], out_shapes=[out_shape], alias_buf=out_buf, name=name)[0]


def _rows(tc, w, cb=0):
    return pl.BlockSpec((tc, w), lambda i: (i, cb))


def _const(shape):
    return pl.BlockSpec(shape, lambda i: tuple([0] * len(shape)))


def _ln_stats(r):
    mu = jnp.mean(r, axis=-1, keepdims=True)
    xc = r - mu
    var = jnp.mean(xc * xc, axis=-1, keepdims=True)
    rstd = lax.rsqrt(var + LN_EPS)
    return xc * rstd, rstd


def _rowsum8(v):
    tc, w = v.shape
    return jnp.sum(v.reshape(tc // 8, 8, w), axis=0)


def ln_fwd(x, g, b, *, name, res=None):
    T, D = x.shape
    tc = _pick(T, [512, 256, 128])
    has_res = res is not None

    def body(*refs):
        if has_res:
            x_ref, res_ref, g_ref, b_ref, r_ref, h_ref, hb_ref = refs
            r = ALPHA * res_ref[...] + x_ref[...]
            r_ref[...] = r
        else:
            x_ref, g_ref, b_ref, h_ref, hb_ref = refs
            r = x_ref[...]
        xhat, _ = _ln_stats(r)
        y = xhat * g_ref[...] + b_ref[...]
        h_ref[...] = y
        hb_ref[...] = y.astype(BF16)

    ins = [x] + ([res] if has_res else []) + [g.reshape(1, D), b.reshape(1, D)]
    in_specs = [_rows(tc, D)] * (2 if has_res else 1) + [_const((1, D))] * 2
    n_out = 3 if has_res else 2
    outs = pl.pallas_call(
        body, grid=(T // tc,), in_specs=in_specs, out_specs=[_rows(tc, D)] * n_out,
        out_shape=[jax.ShapeDtypeStruct((T, D), F32)] * (n_out - 1) + [jax.ShapeDtypeStruct((T, D), BF16)],
        compiler_params=_cparams(("arbitrary",)), name=name)(*ins)
    if has_res:
        return outs
    return (x,) + tuple(outs)


def ln_bwd(r, dy, g, *, name):
    T, D = r.shape
    tc = _pick(T, [512, 256, 128])
    nt = T // tc

    def body(r_ref, dy_ref, g_ref, dr_ref, drb_ref, dg_ref, db_ref, accg, accb):
        i = pl.program_id(0)

        @pl.when(i == 0)
        def _():
            accg[...] = jnp.zeros_like(accg)
            accb[...] = jnp.zeros_like(accb)

        xhat, rstd = _ln_stats(r_ref[...])
        dy = dy_ref[...]
        dxh = dy * g_ref[...]
        m1 = jnp.mean(dxh, axis=-1, keepdims=True)
        m2 = jnp.mean(dxh * xhat, axis=-1, keepdims=True)
        dr = rstd * (dxh - m1 - xhat * m2)
        dr_ref[...] = dr
        drb_ref[...] = dr.astype(BF16)
        accg[...] += _rowsum8(dy * xhat)
        accb[...] += _rowsum8(dy)

        @pl.when(i == nt - 1)
        def _():
            dg_ref[...] = jnp.sum(accg[...], axis=0, keepdims=True)
            db_ref[...] = jnp.sum(accb[...], axis=0, keepdims=True)

    return pl.pallas_call(
        body, grid=(nt,), in_specs=[_rows(tc, D), _rows(tc, D), _const((1, D))],
        out_specs=[_rows(tc, D), _rows(tc, D), _const((1, D)), _const((1, D))],
        out_shape=[jax.ShapeDtypeStruct((T, D), F32), jax.ShapeDtypeStruct((T, D), BF16),
                   jax.ShapeDtypeStruct((1, D), F32), jax.ShapeDtypeStruct((1, D), F32)],
        scratch_shapes=[pltpu.VMEM((8, D), F32), pltpu.VMEM((8, D), F32)],
        compiler_params=_cparams(("arbitrary",)), name=name)(r, dy, g.reshape(1, D))


def loss_head(y, target, *, name):
    T, D = y.shape
    tc = _pick(T, [512, 256, 128])
    nt = T // tc

    def body(y_ref, t_ref, dy_ref, loss_ref, acc):
        i = pl.program_id(0)

        @pl.when(i == 0)
        def _():
            acc[...] = jnp.zeros_like(acc)

        e = y_ref[...] - t_ref[...]
        dy_ref[...] = e * (1.0 / D)
        acc[...] += _rowsum8(e * e)

        @pl.when(i == nt - 1)
        def _():
            s = jnp.sum(jnp.sum(acc[...], axis=0, keepdims=True), axis=1, keepdims=True)
            loss_ref[...] = jnp.broadcast_to(s, (1, 128))

    return pl.pallas_call(
        body, grid=(nt,), in_specs=[_rows(tc, D), _rows(tc, D)],
        out_specs=[_rows(tc, D), _const((1, 128))],
        out_shape=[jax.ShapeDtypeStruct((T, D), F32), jax.ShapeDtypeStruct((1, 128), F32)],
        scratch_shapes=[pltpu.VMEM((8, D), F32)],
        compiler_params=_cparams(("arbitrary",)), name=name)(y, target)


def _halo_prev(tc):
    per = tc // CONV_HALO
    return lambda i: jnp.maximum(i * per - 1, 0)


CONV_ROWS = 32
CONV_TAP_GROUP = 4
CONV_TILE_UNROLL = 4


def _fill_shifts(S, nrows):
    for b in range(1, 8):
        S[b, 0:nrows - 8, :] = S[0, b:b + nrows - 8, :]


def _tap_sum(S, w_ref, offs, r0, nrows):
    acc = None
    for k, o in enumerate(offs):
        a, b = divmod(o, 8)
        term = w_ref[k:k + 1, :] * S[b, pl.ds(pl.multiple_of(r0 + 8 * a, 8), nrows), :]
        acc = term if acc is None else acc + term
    return acc


def conv_fwd(p, dw, db, ng, nb, *, name):
    T = p.shape[0]
    D = D_MODEL
    tc = _pick(T, [256, 128])
    prev = _halo_prev(tc)
    off = CONV_HALO - (CONV_K - 1)
    offs = [off + k for k in range(CONV_K)]

    def body(val_ref, gate_ref, valp_ref, gatep_ref, dw_ref, db_ref, ng_ref, nb_ref, c_ref, act_ref, S):
        i = pl.program_id(0)
        u_prev = valp_ref[...] * _sigmoid(gatep_ref[...])
        S[0, 0:CONV_HALO, :] = jnp.where(i > 0, u_prev, 0.0)
        S[0, CONV_HALO:CONV_HALO + tc, :] = val_ref[...] * _sigmoid(gate_ref[...])
        _fill_shifts(S, CONV_HALO + tc)

        def rows(j, carry):
            r0 = pl.multiple_of(j * CONV_ROWS, CONV_ROWS)
            c_ref[pl.ds(r0, CONV_ROWS), :] = _tap_sum(S, dw_ref, offs, r0, CONV_ROWS) + db_ref[...]
            return carry

        lax.fori_loop(0, tc // CONV_ROWS, rows, 0)
        c = c_ref[...]
        xhat, _ = _ln_stats(c)
        cn = xhat * ng_ref[...] + nb_ref[...]
        act_ref[...] = (cn * _sigmoid(cn)).astype(BF16)

    return pl.pallas_call(
        body, grid=(T // tc,),
        in_specs=[_rows(tc, D, 0), _rows(tc, D, 1),
                  pl.BlockSpec((CONV_HALO, D), lambda i: (prev(i), 0)), pl.BlockSpec((CONV_HALO, D), lambda i: (prev(i), 1)),
                  _const((CONV_HALO, D)), _const((1, D)), _const((1, D)), _const((1, D))],
        out_specs=[_rows(tc, D), _rows(tc, D)],
        out_shape=[jax.ShapeDtypeStruct((T, D), F32), jax.ShapeDtypeStruct((T, D), BF16)],
        scratch_shapes=[pltpu.VMEM((8, CONV_HALO + tc, D), F32)],
        compiler_params=_cparams(("arbitrary",)), name=name)(p, p, p, p, dw, db, ng, nb)


def conv_bwd_norm(dact, c_pre, ng, nb, after, *, name):
    T, D = c_pre.shape
    tc = _pick(T, [512, 256, 128])
    nt = T // tc

    def body(da_ref, c_ref, ng_ref, nb_ref, after_ref, dc_ref, dng_ref, dnb_ref, ddb_ref, accg, accb, accd):
        i = pl.program_id(0)

        @pl.when(i == 0)
        def _():
            accg[...] = jnp.zeros_like(accg)
            accb[...] = jnp.zeros_like(accb)
            accd[...] = jnp.zeros_like(accd)

        xhat, rstd = _ln_stats(c_ref[...])
        cn = xhat * ng_ref[...] + nb_ref[...]
        s = _sigmoid(cn)
        dcn = da_ref[...] * (s * (1.0 + cn * (1.0 - s)))
        dxh = dcn * ng_ref[...]
        m1 = jnp.mean(dxh, axis=-1, keepdims=True)
        m2 = jnp.mean(dxh * xhat, axis=-1, keepdims=True)
        dc = rstd * (dxh - m1 - xhat * m2)
        dc_ref[...] = dc
        accg[...] += _rowsum8(dcn * xhat)
        accb[...] += _rowsum8(dcn)
        accd[...] += _rowsum8(dc)

        @pl.when(i == nt - 1)
        def _():
            dng_ref[...] = jnp.sum(accg[...], axis=0, keepdims=True)
            dnb_ref[...] = jnp.sum(accb[...], axis=0, keepdims=True)
            ddb_ref[...] = jnp.sum(accd[...], axis=0, keepdims=True)

    vec = jax.ShapeDtypeStruct((1, D), F32)
    return pl.pallas_call(
        body, grid=(nt,), in_specs=[_rows(tc, D), _rows(tc, D), _const((1, D)), _const((1, D)), ANY],
        out_specs=[_rows(tc, D), _const((1, D)), _const((1, D)), _const((1, D))],
        out_shape=[jax.ShapeDtypeStruct((T, D), F32), vec, vec, vec],
        scratch_shapes=[pltpu.VMEM((8, D), F32)] * 3,
        compiler_params=_cparams(("arbitrary",)), name=name)(dact, c_pre, ng, nb, after)


def conv_bwd_taps(dc, p, dw, *, name):
    T, D = dc.shape
    tc = _pick(T, [256, 128])
    nt = T // tc
    per = tc // CONV_HALO
    prev = _halo_prev(tc)
    last_halo = T // CONV_HALO - 1
    nxt = lambda i: jnp.minimum((i + 1) * per, last_halo)
    off = CONV_HALO - (CONV_K - 1)

    def body(dc_ref, dcn_ref, val_ref, gate_ref, valp_ref, gatep_ref, dw_ref, dvg_ref, ddw_ref, ext_u, ext_d, acc):
        i = pl.program_id(0)

        @pl.when(i == 0)
        def _():
            acc[...] = jnp.zeros_like(acc)

        u_prev = valp_ref[...] * _sigmoid(gatep_ref[...])
        ext_u[0, 0:CONV_HALO, :] = jnp.where(i > 0, u_prev, 0.0)
        ext_u[0, CONV_HALO:CONV_HALO + tc, :] = val_ref[...] * _sigmoid(gate_ref[...])
        ext_d[0, 0:tc, :] = dc_ref[...]
        ext_d[0, tc:tc + CONV_HALO, :] = jnp.where(i < nt - 1, dcn_ref[...], 0.0)
        _fill_shifts(ext_u, CONV_HALO + tc)
        _fill_shifts(ext_d, CONV_HALO + tc)

        def rows(j, carry):
            r0 = pl.multiple_of(j * CONV_ROWS, CONV_ROWS)
            sl = pl.ds(r0, CONV_ROWS)
            du = _tap_sum(ext_d, dw_ref, [CONV_K - 1 - k for k in range(CONV_K)], r0, CONV_ROWS)
            sg = _sigmoid(gate_ref[sl, :])
            dvg_ref[sl, 0:D] = (du * sg).astype(BF16)
            dvg_ref[sl, D:2 * D] = (du * val_ref[sl, :] * sg * (1.0 - sg)).astype(BF16)
            return carry

        lax.fori_loop(0, tc // CONV_ROWS, rows, 0)

        for k0 in range(0, CONV_K, CONV_TAP_GROUP):
            ks = list(range(k0, min(k0 + CONV_TAP_GROUP, CONV_K)))

            def taps(j, accs, ks=ks):
                out = list(accs)
                for t in range(CONV_TILE_UNROLL):
                    r0 = pl.multiple_of((j * CONV_TILE_UNROLL + t) * 8, 8)
                    dct = dc_ref[pl.ds(r0, 8), :]
                    for q, k in enumerate(ks):
                        a, b = divmod(off + k, 8)
                        out[q] = out[q] + dct * ext_u[b, pl.ds(pl.multiple_of(r0 + 8 * a, 8), 8), :]
                return tuple(out)

            accs = lax.fori_loop(0, tc // (8 * CONV_TILE_UNROLL), taps, tuple(jnp.zeros((8, D), F32) for _ in ks))
            for k, a_k in zip(ks, accs):
                acc[k] += a_k

        @pl.when(i == nt - 1)
        def _():
            ddw_ref[...] = jnp.zeros_like(ddw_ref)
            for k in range(CONV_K):
                ddw_ref[k:k + 1, :] = jnp.sum(acc[k], axis=0, keepdims=True)

    return pl.pallas_call(
        body, grid=(nt,),
        in_specs=[_rows(tc, D), pl.BlockSpec((CONV_HALO, D), lambda i: (nxt(i), 0)),
                  _rows(tc, D, 0), _rows(tc, D, 1),
                  pl.BlockSpec((CONV_HALO, D), lambda i: (prev(i), 0)), pl.BlockSpec((CONV_HALO, D), lambda i: (prev(i), 1)),
                  _const((CONV_HALO, D))],
        out_specs=[_rows(tc, 2 * D), _const((CONV_HALO, D))],
        out_shape=[jax.ShapeDtypeStruct((T, 2 * D), BF16), jax.ShapeDtypeStruct((CONV_HALO, D), F32)],
        scratch_shapes=[pltpu.VMEM((8, CONV_HALO + tc, D), F32), pltpu.VMEM((8, CONV_HALO + tc, D), F32),
                        pltpu.VMEM((CONV_K, 8, D), F32)],
        compiler_params=_cparams(("arbitrary",)), name=name)(dc, dc, p, p, p, p, dw)


GATE_A0 = (2 * D_MODEL + D_SSM) // 512
GATE_B0 = GATE_A0 + 2


def merge_fwd(p, ya, z, *, name):
    T = p.shape[0]
    D = D_MODEL
    tc = _pick(T, [512, 256, 128])
    W = 512

    def body(ga_ref, gb_ref, ya_ref, z1_ref, z2_ref, o_ref):
        yb = z1_ref[...] * _sigmoid(z2_ref[...])
        o_ref[...] = (_sigmoid(ga_ref[...]) * ya_ref[...] + _sigmoid(gb_ref[...]) * yb).astype(BF16)

    return pl.pallas_call(
        body, grid=(T // tc, D // W),
        in_specs=[pl.BlockSpec((tc, W), lambda i, j: (i, GATE_A0 + j)), pl.BlockSpec((tc, W), lambda i, j: (i, GATE_B0 + j)),
                  pl.BlockSpec((tc, W), lambda i, j: (i, j)), pl.BlockSpec((tc, W), lambda i, j: (i, j)),
                  pl.BlockSpec((tc, W), lambda i, j: (i, D // W + j))],
        out_specs=pl.BlockSpec((tc, W), lambda i, j: (i, j)),
        out_shape=jax.ShapeDtypeStruct((T, D), BF16),
        compiler_params=_cparams(("arbitrary", "arbitrary")), name=name)(p, p, ya, z, z)


def merge_bwd(dm, p, ya, z, *, name):
    T = p.shape[0]
    D = D_MODEL
    tc = _pick(T, [512, 256, 128])
    W = 512
    nb = D // W

    def body(dm_ref, ga_ref, gb_ref, ya_ref, z1_ref, z2_ref, dya_ref, dga_ref, dgb_ref, dz1_ref, dz2_ref):
        dm = dm_ref[...]
        sa = _sigmoid(ga_ref[...])
        sb = _sigmoid(gb_ref[...])
        s2 = _sigmoid(z2_ref[...])
        z1 = z1_ref[...]
        yb = z1 * s2
        dya_ref[...] = (dm * sa).astype(BF16)
        dga_ref[...] = (dm * ya_ref[...] * sa * (1.0 - sa)).astype(BF16)
        dgb_ref[...] = (dm * yb * sb * (1.0 - sb)).astype(BF16)
        dyb = dm * sb
        dz1_ref[...] = (dyb * s2).astype(BF16)
        dz2_ref[...] = (dyb * z1 * s2 * (1.0 - s2)).astype(BF16)

    blk = lambda off: pl.BlockSpec((tc, W), lambda i, j: (i, off + j))
    dya, dga, dgb, dz1, dz2 = pl.pallas_call(
        body, grid=(T // tc, nb),
        in_specs=[blk(0), blk(GATE_A0), blk(GATE_B0), blk(0), blk(0), blk(nb)],
        out_specs=[blk(0)] * 5,
        out_shape=[jax.ShapeDtypeStruct((T, D), BF16)] * 5,
        compiler_params=_cparams(("arbitrary", "arbitrary")), name=name)(dm, p, p, ya, z, z)
    return dya, dga, dgb, dz1, dz2


def _scan_block(src_r, src_i, dst_r, dst_i, car_r, car_i, pw_r, pw_i, cw_r, cw_i, ntiles, reverse, extra=None):
    W = src_r.shape[1]
    rows = lax.broadcasted_iota(jnp.int32, (8, W), 0)
    steps = []
    for d, pr in ((1, 0), (2, 1), (4, 3)):
        valid = rows < 8 - d if reverse else rows >= d
        steps.append((d, jnp.where(valid, jnp.broadcast_to(pw_r[pr:pr + 1, :], (8, W)), 0.0),
                      jnp.where(valid, jnp.broadcast_to(pw_i[pr:pr + 1, :], (8, W)), 0.0)))
    cw_r, cw_i = cw_r[...], cw_i[...]

    def tile(jj, carry):
        j = ntiles - 1 - jj if reverse else jj
        sl = pl.ds(pl.multiple_of(j * 8, 8), 8)
        xr, xi = src_r[sl, :], src_i[sl, :]
        for d, lr, li in steps:
            sr = pltpu.roll(xr, 8 - d if reverse else d, 0)
            si = pltpu.roll(xi, 8 - d if reverse else d, 0)
            xr, xi = xr + lr * sr - li * si, xi + lr * si + li * sr
        cr, ci = car_r[...], car_i[...]
        xr, xi = xr + cw_r * cr - cw_i * ci, xi + cw_r * ci + cw_i * cr
        dst_r[sl, :] = xr
        dst_i[sl, :] = xi
        edge = 0 if reverse else 7
        car_r[...] = jnp.broadcast_to(xr[edge:edge + 1, :], (8, W))
        car_i[...] = jnp.broadcast_to(xi[edge:edge + 1, :], (8, W))
        if extra is not None:
            carry = extra(j, xr, xi, carry)
        return carry

    return tile


def ssm_fwd(p, Br, Bi, Cr, Ci, pw_r, pw_i, dvec, *, name):
    T = p.shape[0]
    tt = _pick(T, [256, 128])
    nt = T // tt
    WI, WS = SSM_BLOCK_IN, SSM_BLOCK_STATE
    u0 = 2 * D_MODEL // WI

    def body(u_ref, br_ref, bi_ref, cr_ref, ci_ref, pwr_ref, pwi_ref, d_ref, xr_ref, xi_ref, y_ref, bur, bui, car_r, car_i):
        i = pl.program_id(1)

        @pl.when(i == 0)
        def _():
            car_r[...] = jnp.zeros_like(car_r)
            car_i[...] = jnp.zeros_like(car_i)

        u = u_ref[...]
        ub = u.astype(BF16)
        bur[...] = jnp.dot(ub, br_ref[...].astype(BF16), preferred_element_type=F32)
        bui[...] = jnp.dot(ub, bi_ref[...].astype(BF16), preferred_element_type=F32)
        tile = _scan_block(bur, bui, xr_ref, xi_ref, car_r, car_i, pwr_ref, pwi_ref, pwr_ref, pwi_ref, tt // 8, False)
        lax.fori_loop(0, tt // 8, tile, 0)
        y = (jnp.dot(xr_ref[...].astype(BF16), cr_ref[...].astype(BF16), preferred_element_type=F32)
             - jnp.dot(xi_ref[...].astype(BF16), ci_ref[...].astype(BF16), preferred_element_type=F32)
             + d_ref[...] * u)
        y_ref[...] = y.astype(BF16)

    return pl.pallas_call(
        body, grid=(SSM_BLOCKS, nt),
        in_specs=[pl.BlockSpec((tt, WI), lambda b, i: (i, u0 + b)),
                  pl.BlockSpec((None, WI, WS), lambda b, i: (b, 0, 0)), pl.BlockSpec((None, WI, WS), lambda b, i: (b, 0, 0)),
                  pl.BlockSpec((None, WS, WI), lambda b, i: (b, 0, 0)), pl.BlockSpec((None, WS, WI), lambda b, i: (b, 0, 0)),
                  pl.BlockSpec((8, WS), lambda b, i: (0, b)), pl.BlockSpec((8, WS), lambda b, i: (0, b)),
                  pl.BlockSpec((1, WI), lambda b, i: (0, b))],
        out_specs=[pl.BlockSpec((tt, WS), lambda b, i: (i, b)), pl.BlockSpec((tt, WS), lambda b, i: (i, b)),
                   pl.BlockSpec((tt, WI), lambda b, i: (i, b))],
        out_shape=[jax.ShapeDtypeStruct((T, SSM_BLOCKS * WS), F32)] * 2 + [jax.ShapeDtypeStruct((T, D_SSM), BF16)],
        scratch_shapes=[pltpu.VMEM((tt, WS), F32), pltpu.VMEM((tt, WS), F32), pltpu.VMEM((8, WS), F32), pltpu.VMEM((8, WS), F32)],
        compiler_params=_cparams(("arbitrary", "arbitrary")), name=name)(p, Br, Bi, Cr, Ci, pw_r, pw_i, dvec)


def ssm_bwd(dy, p, xr, xi, Br, Bi, Cr, Ci, pwc_r, pwc_i, dvec, *, name):
    T = p.shape[0]
    tt = _pick(T, [256, 128])
    nt = T // tt
    WI, WS = SSM_BLOCK_IN, SSM_BLOCK_STATE
    u0 = 2 * D_MODEL // WI
    tb = lambda i: nt - 1 - i
    xprev = lambda i: jnp.maximum(tb(i) * (tt // 8) - 1, 0)
    tn_dims = _DIMS["tn"]
    nt_dims = _DIMS["nt"]

    def body(dy_ref, u_ref, xr_ref, xi_ref, xpr_ref, xpi_ref, br_ref, bi_ref, cr_ref, ci_ref, pwr_ref, pwi_ref,
             cwr_ref, cwi_ref, d_ref,
             du_ref, dbr_ref, dbi_ref, dcr_ref, dci_ref, dar_ref, dai_ref, dd_ref,
             gr, gi, ext_r, ext_i, car_r, car_i):
        i = pl.program_id(1)

        @pl.when(i == 0)
        def _():
            car_r[...] = jnp.zeros_like(car_r)
            car_i[...] = jnp.zeros_like(car_i)
            dbr_ref[...] = jnp.zeros_like(dbr_ref)
            dbi_ref[...] = jnp.zeros_like(dbi_ref)
            dcr_ref[...] = jnp.zeros_like(dcr_ref)
            dci_ref[...] = jnp.zeros_like(dci_ref)
            dar_ref[...] = jnp.zeros_like(dar_ref)
            dai_ref[...] = jnp.zeros_like(dai_ref)
            dd_ref[...] = jnp.zeros_like(dd_ref)

        dy = dy_ref[...]
        dyb = dy.astype(BF16)
        u = u_ref[...]
        ub = u.astype(BF16)
        gr[...] = lax.dot_general(dyb, cr_ref[...].astype(BF16), nt_dims, preferred_element_type=F32)
        gi[...] = -lax.dot_general(dyb, ci_ref[...].astype(BF16), nt_dims, preferred_element_type=F32)
        first = tb(i) == 0
        ext_r[0:8, :] = jnp.where(first, 0.0, xpr_ref[...])
        ext_i[0:8, :] = jnp.where(first, 0.0, xpi_ref[...])
        ext_r[8:8 + tt, :] = xr_ref[...]
        ext_i[8:8 + tt, :] = xi_ref[...]
        rows = lax.broadcasted_iota(jnp.int32, (8, WS), 0)

        def lam_grad(j, g_r, g_i, carry):
            a_r, a_i = carry
            cur = pl.ds(pl.multiple_of(j * 8 + 8, 8), 8)
            prv = pl.ds(pl.multiple_of(j * 8, 8), 8)
            xc_r, xc_i = ext_r[cur, :], ext_i[cur, :]
            xl_r, xl_i = ext_r[prv, :], ext_i[prv, :]
            xp_r = jnp.where(rows == 0, jnp.broadcast_to(xl_r[7:8, :], (8, WS)), pltpu.roll(xc_r, 1, 0))
            xp_i = jnp.where(rows == 0, jnp.broadcast_to(xl_i[7:8, :], (8, WS)), pltpu.roll(xc_i, 1, 0))
            return (a_r + g_r * xp_r + g_i * xp_i, a_i + g_i * xp_r - g_r * xp_i)

        tile = _scan_block(gr, gi, gr, gi, car_r, car_i, pwr_ref, pwi_ref, cwr_ref, cwi_ref, tt // 8, True, extra=lam_grad)
        z8 = jnp.zeros((8, WS), F32)
        a_r, a_i = lax.fori_loop(0, tt // 8, tile, (z8, z8))
        dar_ref[...] += a_r
        dai_ref[...] += a_i
        grb = gr[...].astype(BF16)
        gib = gi[...].astype(BF16)
        dbr_ref[...] += lax.dot_general(ub, grb, tn_dims, preferred_element_type=F32)
        dbi_ref[...] += lax.dot_general(ub, gib, tn_dims, preferred_element_type=F32)
        dcr_ref[...] += lax.dot_general(xr_ref[...].astype(BF16), dyb, tn_dims, preferred_element_type=F32)
        dci_ref[...] -= lax.dot_general(xi_ref[...].astype(BF16), dyb, tn_dims, preferred_element_type=F32)
        du = (lax.dot_general(grb, br_ref[...].astype(BF16), nt_dims, preferred_element_type=F32)
              + lax.dot_general(gib, bi_ref[...].astype(BF16), nt_dims, preferred_element_type=F32)
              + d_ref[...] * dy)
        du_ref[...] = du.astype(BF16)
        dd_ref[...] += _rowsum8(dy * u)

    wspec = lambda shp: pl.BlockSpec((None,) + shp, lambda b, i: (b, 0, 0))
    return pl.pallas_call(
        body, grid=(SSM_BLOCKS, nt),
        in_specs=[pl.BlockSpec((tt, WI), lambda b, i: (tb(i), b)),
                  pl.BlockSpec((tt, WI), lambda b, i: (tb(i), u0 + b)),
                  pl.BlockSpec((tt, WS), lambda b, i: (tb(i), b)), pl.BlockSpec((tt, WS), lambda b, i: (tb(i), b)),
                  pl.BlockSpec((8, WS), lambda b, i: (xprev(i), b)), pl.BlockSpec((8, WS), lambda b, i: (xprev(i), b)),
                  wspec((WI, WS)), wspec((WI, WS)), wspec((WS, WI)), wspec((WS, WI)),
                  pl.BlockSpec((8, WS), lambda b, i: (0, b)), pl.BlockSpec((8, WS), lambda b, i: (0, b)),
                  pl.BlockSpec((8, WS), lambda b, i: (0, b)), pl.BlockSpec((8, WS), lambda b, i: (0, b)),
                  pl.BlockSpec((1, WI), lambda b, i: (0, b))],
        out_specs=[pl.BlockSpec((tt, WI), lambda b, i: (tb(i), b)),
                   wspec((WI, WS)), wspec((WI, WS)), wspec((WS, WI)), wspec((WS, WI)),
                   pl.BlockSpec((8, WS), lambda b, i: (0, b)), pl.BlockSpec((8, WS), lambda b, i: (0, b)),
                   pl.BlockSpec((8, WI), lambda b, i: (0, b))],
        out_shape=[jax.ShapeDtypeStruct((T, D_SSM), BF16),
                   jax.ShapeDtypeStruct((SSM_BLOCKS, WI, WS), F32), jax.ShapeDtypeStruct((SSM_BLOCKS, WI, WS), F32),
                   jax.ShapeDtypeStruct((SSM_BLOCKS, WS, WI), F32), jax.ShapeDtypeStruct((SSM_BLOCKS, WS, WI), F32),
                   jax.ShapeDtypeStruct((8, SSM_BLOCKS * WS), F32), jax.ShapeDtypeStruct((8, SSM_BLOCKS * WS), F32),
                   jax.ShapeDtypeStruct((8, D_SSM), F32)],
        scratch_shapes=[pltpu.VMEM((tt, WS), F32), pltpu.VMEM((tt, WS), F32),
                        pltpu.VMEM((tt + 8, WS), F32), pltpu.VMEM((tt + 8, WS), F32),
                        pltpu.VMEM((8, WS), F32), pltpu.VMEM((8, WS), F32)],
        compiler_params=_cparams(("arbitrary", "arbitrary")), name=name,
    )(dy, p, xr, xi, xr, xi, Br, Bi, Cr, Ci, pwc_r, pwc_i, pwc_r[::-1], pwc_i[::-1], dvec)


def _ssm_discretise(log_step, lam_re, lam_im, b_re, b_im):
    step = jnp.exp(log_step)[:, None]
    mag = jnp.exp(lam_re * step)
    ar = mag * jnp.cos(lam_im * step)
    ai = mag * jnp.sin(lam_im * step)
    den = lam_re * lam_re + lam_im * lam_im
    nr = ar - 1.0
    cr = (nr * lam_re + ai * lam_im) / den
    ci = (ai * lam_re - nr * lam_im) / den
    bbr = cr[..., None] * b_re - ci[..., None] * b_im
    bbi = cr[..., None] * b_im + ci[..., None] * b_re
    return ar, ai, bbr, bbi


def _blockdiag_in(bb):
    t = jnp.transpose(bb, (0, 2, 1)).reshape(SSM_BLOCKS, 8, SSM_GROUP, SSM_STATE)
    eye = jnp.eye(8, dtype=bb.dtype)
    return (t[:, :, :, None, :] * eye[None, :, None, :, None]).reshape(SSM_BLOCKS, SSM_BLOCK_IN, SSM_BLOCK_STATE)


def _blockdiag_out(cc):
    t = jnp.transpose(cc, (0, 2, 1)).reshape(SSM_BLOCKS, 8, SSM_STATE, SSM_GROUP)
    eye = jnp.eye(8, dtype=cc.dtype)
    return (t[:, :, :, None, :] * eye[None, :, None, :, None]).reshape(SSM_BLOCKS, SSM_BLOCK_STATE, SSM_BLOCK_IN)


def _diag_in(d):
    t = d.reshape(SSM_BLOCKS, 8, SSM_GROUP, 8, SSM_STATE)
    t = jnp.einsum("bghgp->bghp", t).reshape(SSM_GROUPS, SSM_GROUP, SSM_STATE)
    return jnp.transpose(t, (0, 2, 1))


def _diag_out(d):
    t = d.reshape(SSM_BLOCKS, 8, SSM_STATE, 8, SSM_GROUP)
    t = jnp.einsum("bgpgh->bgph", t).reshape(SSM_GROUPS, SSM_STATE, SSM_GROUP)
    return jnp.transpose(t, (0, 2, 1))


def _powers(ar, ai):
    rs, is_ = [ar], [ai]
    for _ in range(7):
        r, i = rs[-1], is_[-1]
        rs.append(r * ar - i * ai)
        is_.append(r * ai + i * ar)
    return jnp.stack(rs), jnp.stack(is_)


def attn_fwd(q, kv, *, name):
    T, D = q.shape
    nm = kv.shape[0]
    tq = _pick(T, [512, 256, 128])
    scale = HEAD_DIM ** -0.5

    def body(q_ref, k_ref, v_ref, o_ref):
        for h in range(N_HEADS):
            sl = slice(h * HEAD_DIM, (h + 1) * HEAD_DIM)
            s = lax.dot_general(q_ref[:, sl], k_ref[:, sl].astype(BF16), _DIMS["nt"], preferred_element_type=F32) * scale
            e = jnp.exp(s - jnp.max(s, axis=-1, keepdims=True))
            pr = e / jnp.sum(e, axis=-1, keepdims=True)
            o_ref[:, sl] = jnp.dot(pr.astype(BF16), v_ref[:, sl].astype(BF16), preferred_element_type=F32).astype(BF16)

    return pl.pallas_call(
        body, grid=(T // tq,),
        in_specs=[_rows(tq, D), pl.BlockSpec((nm, D), lambda i: (0, 0)), pl.BlockSpec((nm, D), lambda i: (0, 1))],
        out_specs=_rows(tq, D), out_shape=jax.ShapeDtypeStruct((T, D), BF16),
        compiler_params=_cparams(("arbitrary",)), name=name)(q, kv, kv)


def attn_bwd(q, kv, do, *, name):
    T, D = q.shape
    nm = kv.shape[0]
    tq = _pick(T, [512, 256, 128])
    nt = T // tq
    scale = HEAD_DIM ** -0.5

    def body(q_ref, k_ref, v_ref, do_ref, dq_ref, dkv_ref):
        i = pl.program_id(0)

        @pl.when(i == 0)
        def _():
            dkv_ref[...] = jnp.zeros_like(dkv_ref)

        for h in range(N_HEADS):
            sl = slice(h * HEAD_DIM, (h + 1) * HEAD_DIM)
            slv = slice(D + h * HEAD_DIM, D + (h + 1) * HEAD_DIM)
            qh = q_ref[:, sl]
            kh = k_ref[:, sl].astype(BF16)
            vh = v_ref[:, sl].astype(BF16)
            doh = do_ref[:, sl].astype(BF16)
            s = lax.dot_general(qh, kh, _DIMS["nt"], preferred_element_type=F32) * scale
            e = jnp.exp(s - jnp.max(s, axis=-1, keepdims=True))
            pr = e / jnp.sum(e, axis=-1, keepdims=True)
            dp = lax.dot_general(doh, vh, _DIMS["nt"], preferred_element_type=F32)
            ds = (pr * (dp - jnp.sum(pr * dp, axis=-1, keepdims=True)) * scale).astype(BF16)
            dq_ref[:, sl] = jnp.dot(ds, kh, preferred_element_type=F32).astype(BF16)
            dkv_ref[:, sl] += lax.dot_general(ds, qh, _DIMS["tn"], preferred_element_type=F32)
            dkv_ref[:, slv] += lax.dot_general(pr.astype(BF16), doh, _DIMS["tn"], preferred_element_type=F32)

    return pl.pallas_call(
        body, grid=(nt,),
        in_specs=[_rows(tq, D), pl.BlockSpec((nm, D), lambda i: (0, 0)), pl.BlockSpec((nm, D), lambda i: (0, 1)), _rows(tq, D)],
        out_specs=[_rows(tq, D), _const((nm, 2 * D))],
        out_shape=[jax.ShapeDtypeStruct((T, D), BF16), jax.ShapeDtypeStruct((nm, 2 * D), F32)],
        compiler_params=_cparams(("arbitrary",)), name=name)(q, kv, kv, do)


def _adam_math(w, g, m, v):
    m = ADAM_B1 * m + (1.0 - ADAM_B1) * g
    v = ADAM_B2 * v + (1.0 - ADAM_B2) * (g * g)
    m_hat = m / (1.0 - ADAM_B1 ** ADAM_STEP)
    v_hat = v / (1.0 - ADAM_B2 ** ADAM_STEP)
    delta = -ADAM_LR * (m_hat / (jnp.sqrt(v_hat) + ADAM_EPS) + ADAM_WD * w)
    return delta, m, v


def adamw(w, m, v, g_arr, g_row0, *, name):
    R, C = w.shape
    tr = _pick(R, [256, 128, 64, 32, 16, 8])
    assert g_row0 % tr == 0
    g0 = g_row0 // tr

    def body(w_ref, m_ref, v_ref, g_ref, go_ref, d_ref, mo_ref, vo_ref):
        g = g_ref[...]
        d, mn, vn = _adam_math(w_ref[...], g, m_ref[...], v_ref[...])
        go_ref[...] = g
        d_ref[...] = d
        mo_ref[...] = mn
        vo_ref[...] = vn

    sp = pl.BlockSpec((tr, C), lambda i: (i, 0))
    return pl.pallas_call(
        body, grid=(R // tr,), in_specs=[sp, sp, sp, pl.BlockSpec((tr, C), lambda i: (g0 + i, 0))],
        out_specs=[sp] * 4, out_shape=[jax.ShapeDtypeStruct((R, C), F32)] * 4,
        compiler_params=_cparams(("arbitrary",)), name=name)(w, m, v, g_arr)


def _place():
    x, y, c = lax.axis_index("x"), lax.axis_index("y"), lax.axis_index("c")
    chips = [(1 - x, y), (x, 1 - y), (1 - x, 1 - y)]
    return x, y, c, chips


ANY = pl.BlockSpec(memory_space=pl.ANY)


def allgather_weights(bufs, *, name):
    n = len(bufs)

    def body(*refs):
        o_refs = refs[n:2 * n]
        send_sems, recv_sems, fsend_sems, frecv_sems = refs[2 * n:]
        x, y, c, chips = _place()
        k_me = 2 * x + y
        sib = (x, y, 1 - c)
        halves = [b.shape[1] // 2 for b in bufs]

        def half(a, cc):
            return pl.ds(pl.multiple_of(cc * halves[a], 16), halves[a])

        sends = []
        for a in range(n):
            for r, (px, py) in enumerate(chips):
                cp = pltpu.make_async_remote_copy(
                    src_ref=o_refs[a].at[k_me, half(a, c)], dst_ref=o_refs[a].at[k_me, half(a, c)],
                    send_sem=send_sems.at[3 * a + r], recv_sem=recv_sems.at[3 * a + r],
                    device_id=(px, py, c), device_id_type=MESH)
                cp.start()
                sends.append(cp)
        passed = []
        for a in range(n):
            for r, (px, py) in enumerate(chips):
                win = o_refs[a].at[2 * px + py, half(a, c)]
                pltpu.make_async_remote_copy(
                    src_ref=win, dst_ref=win, send_sem=send_sems.at[3 * a + r], recv_sem=recv_sems.at[3 * a + r],
                    device_id=(px, py, c), device_id_type=MESH).wait_recv()
                cp = pltpu.make_async_remote_copy(
                    src_ref=win, dst_ref=win, send_sem=fsend_sems.at[3 * a + r], recv_sem=frecv_sems.at[3 * a + r],
                    device_id=sib, device_id_type=MESH)
                cp.start()
                passed.append(cp)
        for a in range(n):
            for r, (px, py) in enumerate(chips):
                win = o_refs[a].at[2 * px + py, half(a, 1 - c)]
                pltpu.make_async_remote_copy(
                    src_ref=win, dst_ref=win, send_sem=fsend_sems.at[3 * a + r], recv_sem=frecv_sems.at[3 * a + r],
                    device_id=sib, device_id_type=MESH).wait_recv()
        for cp in sends + passed:
            cp.wait_send()

    return pl.pallas_call(
        body, in_specs=[ANY] * n, out_specs=[ANY] * n,
        out_shape=[jax.ShapeDtypeStruct(b.shape, b.dtype) for b in bufs],
        scratch_shapes=[pltpu.SemaphoreType.DMA((3 * n,))] * 4,
        input_output_aliases={a: a for a in range(n)},
        name=name)(*bufs)


HBM_SPEC = pl.BlockSpec(memory_space=pltpu.HBM)
SEM_SPEC = pl.BlockSpec(memory_space=pltpu.SEMAPHORE)


def _hbm(a):
    return pltpu.with_memory_space_constraint(a, pltpu.HBM)


def gather_start(bufs, pieces, *, name):
    n = len(bufs)
    npc = len(pieces)

    def body(*refs):
        b_refs = refs[:n]
        send_sems, recv_sems = refs[n], refs[n + 1]
        x, y, c, chips = _place()
        k_me = 2 * x + y
        for q, (a, row0, rows) in enumerate(pieces):
            win = b_refs[a].at[k_me, pl.ds(row0, rows)]
            for r, (px, py) in enumerate(chips):
                pltpu.make_async_remote_copy(
                    src_ref=win, dst_ref=win, send_sem=send_sems.at[3 * q + r], recv_sem=recv_sems.at[3 * q + r],
                    device_id=(px, py, c), device_id_type=MESH).start()

    return pl.pallas_call(
        body, in_specs=[HBM_SPEC] * n, out_specs=[SEM_SPEC, SEM_SPEC] + [HBM_SPEC] * n,
        out_shape=[pltpu.SemaphoreType.DMA((3 * npc,)), pltpu.SemaphoreType.DMA((3 * npc,))]
        + [pltpu.HBM(b.shape, b.dtype) for b in bufs],
        input_output_aliases={a: 2 + a for a in range(n)},
        compiler_params=pltpu.CompilerParams(has_side_effects=pltpu.SideEffectType.DATAFLOW_SIDE_EFFECTING),
        name=name)(*[_hbm(b) for b in bufs])


def gather_wait(send_sems, recv_sems, bufs, which, after, *, name):
    n = len(bufs)

    def body(*refs):
        b_refs = refs[:n]
        send_sems, recv_sems = refs[n], refs[n + 1]
        x, y, c, chips = _place()
        k_me = 2 * x + y
        for a, row0, rows, q in which:
            for r, (px, py) in enumerate(chips):
                cp = pltpu.make_async_remote_copy(
                    src_ref=b_refs[a].at[k_me, pl.ds(row0, rows)], dst_ref=b_refs[a].at[2 * px + py, pl.ds(row0, rows)],
                    send_sem=send_sems.at[3 * q + r], recv_sem=recv_sems.at[3 * q + r],
                    device_id=(px, py, c), device_id_type=MESH)
                cp.wait_send()
                cp.wait_recv()

    return pl.pallas_call(
        body, in_specs=[HBM_SPEC] * n + [SEM_SPEC, SEM_SPEC, ANY], out_specs=[HBM_SPEC] * n,
        out_shape=[pltpu.HBM(b.shape, b.dtype) for b in bufs],
        input_output_aliases={a: a for a in range(n)},
        compiler_params=pltpu.CompilerParams(has_side_effects=pltpu.SideEffectType.DATAFLOW_SIDE_EFFECTING),
        name=name)(*bufs, send_sems, recv_sems, after)


def exchange_halves(grads, *, name):
    n = len(grads)

    def body(*refs):
        g_refs, l_refs = refs[:n], refs[n:2 * n]
        send_sems, recv_sems = refs[2 * n:]
        x, y, c, _ = _place()
        cps = []
        for a in range(n):
            h = grads[a].shape[1] // 2
            cp = pltpu.make_async_remote_copy(
                src_ref=g_refs[a].at[:, pl.ds(pl.multiple_of((1 - c) * h, 8), h)], dst_ref=l_refs[a],
                send_sem=send_sems.at[a], recv_sem=recv_sems.at[a], device_id=(x, y, 1 - c), device_id_type=MESH)
            cp.start()
            cps.append(cp)
        for cp in cps:
            cp.wait()

    return pl.pallas_call(
        body, in_specs=[ANY] * n, out_specs=[ANY] * n,
        out_shape=[jax.ShapeDtypeStruct((g.shape[0], g.shape[1] // 2, g.shape[2]), g.dtype) for g in grads],
        scratch_shapes=[pltpu.SemaphoreType.DMA((n,))] * 2,
        name=name)(*grads)


def scatter_start(parts, *, name):
    n = len(parts)
    lands = [lax.empty((3,) + p.shape[1:], p.dtype) for p in parts]

    def body(*refs):
        p_refs, l_refs = refs[:n], refs[n:2 * n]
        send_sems, recv_sems = refs[2 * n], refs[2 * n + 1]
        x, y, c, chips = _place()
        for a in range(n):
            for r, (px, py) in enumerate(chips):
                pltpu.make_async_remote_copy(
                    src_ref=p_refs[a].at[2 * px + py], dst_ref=l_refs[a].at[r],
                    send_sem=send_sems.at[3 * a + r], recv_sem=recv_sems.at[3 * a + r],
                    device_id=(px, py, c), device_id_type=MESH).start()

    outs = pl.pallas_call(
        body, in_specs=[HBM_SPEC] * (2 * n), out_specs=[SEM_SPEC, SEM_SPEC] + [HBM_SPEC] * (2 * n),
        out_shape=[pltpu.SemaphoreType.DMA((3 * n,)), pltpu.SemaphoreType.DMA((3 * n,))]
        + [pltpu.HBM(a.shape, a.dtype) for a in parts + lands],
        input_output_aliases={a: 2 + a for a in range(2 * n)},
        compiler_params=pltpu.CompilerParams(has_side_effects=pltpu.SideEffectType.DATAFLOW_SIDE_EFFECTING),
        name=name)(*[_hbm(a) for a in parts + lands])
    return outs[0], outs[1], list(outs[2:2 + n]), list(outs[2 + n:])


def scatter_wait(rounds, after, *, name):
    sizes = [len(r[2]) for r in rounds]
    flat = [a for r in rounds for a in r[2] + r[3]]
    sems = [s for r in rounds for s in (r[0], r[1])]
    nflat = len(flat)

    def body(*refs):
        x, y, c, chips = _place()
        pos = 0
        for ri, n in enumerate(sizes):
            p_refs, l_refs = refs[pos:pos + n], refs[pos + n:pos + 2 * n]
            send_sems, recv_sems = refs[nflat + 2 * ri], refs[nflat + 2 * ri + 1]
            for a in range(n):
                for r, (px, py) in enumerate(chips):
                    cp = pltpu.make_async_remote_copy(
                        src_ref=p_refs[a].at[2 * px + py], dst_ref=l_refs[a].at[r],
                        send_sem=send_sems.at[3 * a + r], recv_sem=recv_sems.at[3 * a + r],
                        device_id=(px, py, c), device_id_type=MESH)
                    cp.wait_send()
                    cp.wait_recv()
            pos += 2 * n

    outs = pl.pallas_call(
        body, in_specs=[HBM_SPEC] * nflat + [SEM_SPEC] * len(sems) + [ANY], out_specs=[HBM_SPEC] * nflat,
        out_shape=[pltpu.HBM(a.shape, a.dtype) for a in flat],
        input_output_aliases={a: a for a in range(nflat)},
        compiler_params=pltpu.CompilerParams(has_side_effects=pltpu.SideEffectType.DATAFLOW_SIDE_EFFECTING),
        name=name)(*flat, *sems, after)
    res, pos = [], 0
    for n in sizes:
        res.append((list(outs[pos:pos + n]), list(outs[pos + n:pos + 2 * n])))
        pos += 2 * n
    return res


def join_halves(fulls, *, name):
    n = len(fulls)

    def body(*refs):
        o_refs = refs[n:2 * n]
        send_sems, recv_sems = refs[2 * n:]
        x, y, c, _ = _place()
        cps = []
        for a in range(n):
            h = fulls[a].shape[0] // 2
            win = o_refs[a].at[pl.ds(pl.multiple_of(c * h, 8), h)]
            cp = pltpu.make_async_remote_copy(
                src_ref=win, dst_ref=win, send_sem=send_sems.at[a], recv_sem=recv_sems.at[a],
                device_id=(x, y, 1 - c), device_id_type=MESH)
            cp.start()
            cps.append(cp)
        for a in range(n):
            h = fulls[a].shape[0] // 2
            other = o_refs[a].at[pl.ds(pl.multiple_of((1 - c) * h, 8), h)]
            pltpu.make_async_remote_copy(
                src_ref=other, dst_ref=other, send_sem=send_sems.at[a], recv_sem=recv_sems.at[a],
                device_id=(x, y, 1 - c), device_id_type=MESH).wait_recv()
        for cp in cps:
            cp.wait_send()

    return pl.pallas_call(
        body, in_specs=[ANY] * n, out_specs=[ANY] * n,
        out_shape=[jax.ShapeDtypeStruct(f.shape, f.dtype) for f in fulls],
        scratch_shapes=[pltpu.SemaphoreType.DMA((n,))] * 2,
        input_output_aliases={a: a for a in range(n)},
        name=name)(*fulls)


def add_sibling(g, l, c, *, name):
    nb, R, C = g.shape
    h = R // 2
    tr = _pick(h, [256, 128, 64, 32, 16])
    per = h // tr

    def body(c_ref, g_ref, l_ref, o_ref):
        o_ref[...] = (g_ref[...] + l_ref[...]).astype(BF16)

    return pl.pallas_call(
        body,
        grid_spec=pltpu.PrefetchScalarGridSpec(
            num_scalar_prefetch=1, grid=(nb, per),
            in_specs=[pl.BlockSpec((None, tr, C), lambda k, i, c_ref: (k, c_ref[0] * per + i, 0)),
                      pl.BlockSpec((None, tr, C), lambda k, i, c_ref: (k, i, 0))],
            out_specs=pl.BlockSpec((None, tr, C), lambda k, i, c_ref: (k, i, 0))),
        out_shape=jax.ShapeDtypeStruct((nb, h, C), BF16),
        compiler_params=_cparams(("arbitrary", "arbitrary")), name=name)(c, g, l)


def add_chips(part, land, kc, *, name):
    _, H, C = part.shape
    tr = _pick(H, [256, 128, 64, 32, 16])
    per = H // tr

    def body(kc_ref, p_ref, l_ref, o_ref):
        o_ref[...] = ((p_ref[...].astype(F32) + l_ref[0].astype(F32)) + l_ref[1].astype(F32)) + l_ref[2].astype(F32)

    return pl.pallas_call(
        body,
        grid_spec=pltpu.PrefetchScalarGridSpec(
            num_scalar_prefetch=1, grid=(per,),
            in_specs=[pl.BlockSpec((None, tr, C), lambda i, kc_ref: (kc_ref[0], i, 0)),
                      pl.BlockSpec((3, tr, C), lambda i, kc_ref: (0, i, 0))],
            out_specs=pl.BlockSpec((tr, C), lambda i, kc_ref: (kc_ref[1] * per + i, 0))),
        out_shape=jax.ShapeDtypeStruct((2 * H, C), F32),
        compiler_params=_cparams(("arbitrary",)), name=name)(kc, part, land)


def allgather_sum(v, *, name):
    m_per, n = v.shape

    def body(x_ref, out_ref, sum_ref, send_sems, recv_sems, local_sem):
        x, y, c, chips = _place()
        me, sibling = (x, y, c), (x, y, 1 - c)

        def rows(px, py, pc):
            return out_ref.at[pl.ds(pl.multiple_of((4 * px + 2 * py + pc) * m_per, 8), m_per), :]

        def copy(k, block, to, src=None):
            return pltpu.make_async_remote_copy(
                src_ref=rows(*block) if src is None else src, dst_ref=rows(*block),
                send_sem=send_sems.at[k], recv_sem=recv_sems.at[k], device_id=to, device_id_type=MESH)

        mine = pltpu.make_async_copy(x_ref, rows(*me), local_sem)
        mine.start()
        first = [copy(0, me, sibling, src=x_ref)]
        first += [copy(1 + j, me, (*chip, c), src=x_ref) for j, chip in enumerate(chips)]
        for cp in first:
            cp.start()
        passed = [copy(4 + j, (*chip, c), sibling) for j, chip in enumerate(chips)]
        for j, chip in enumerate(chips):
            copy(1 + j, (*chip, c), me).wait_recv()
            passed[j].start()
        copy(0, sibling, me).wait_recv()
        for j, chip in enumerate(chips):
            copy(4 + j, (*chip, 1 - c), me).wait_recv()
        for cp in first + passed:
            cp.wait_send()
        mine.wait()
        acc = out_ref[0:m_per, :]
        for d in range(1, N_DEV):
            acc = acc + out_ref[d * m_per:(d + 1) * m_per, :]
        sum_ref[...] = acc

    vm = pl.BlockSpec(memory_space=pltpu.VMEM)
    return pl.pallas_call(
        body, in_specs=[vm], out_specs=[vm, vm],
        out_shape=[jax.ShapeDtypeStruct((N_DEV * m_per, n), v.dtype), jax.ShapeDtypeStruct((m_per, n), v.dtype)],
        scratch_shapes=[pltpu.SemaphoreType.DMA((7,)), pltpu.SemaphoreType.DMA((7,)), pltpu.SemaphoreType.DMA],
        compiler_params=pltpu.CompilerParams(vmem_limit_bytes=VMEM_LIMIT_BYTES), name=name)(v)


def _pack(arrs):
    cols = []
    for a in arrs:
        f = a.reshape(-1)
        pad = (-f.shape[0]) % 128
        cols.append(jnp.pad(f, (0, pad)).reshape(-1, 128))
    out = jnp.concatenate(cols, axis=0)
    pad = (-out.shape[0]) % 8
    return jnp.pad(out, ((0, pad), (0, 0)))


def _unpack(buf, shapes):
    outs, r = [], 0
    for s in shapes:
        nel = math.prod(s)
        nr = -(-nel // 128)
        outs.append(buf[r:r + nr].reshape(-1)[:nel].reshape(s))
        r += nr
    return outs


GA_CONV_OUT, GA_MIX_OUT, GA_WQ, GA_WO, GA_DOWN, GA_UP, GA_ROWS = 0, 256, 512, 768, 1024, 2048, 3072
G1_DOWN, G1_UP, G1_ROWS = 0, 1024, 2048
G2_CONV_OUT, G2_MIX_OUT, G2_WQ, G2_WO, G2_ROWS = 0, 256, 512, 768, 1024


def kernel(x, mem, in_norm_g, in_norm_b, w_in, conv_dw, conv_db, conv_norm_g, conv_norm_b, w_conv_out, ssm_log_step, ssm_lambda_re, ssm_lambda_im, ssm_b_re, ssm_b_im, ssm_c_re, ssm_c_im, ssm_d, w_ssm_glu, w_mix_out, ln1_g, ln1_b, xa_wq, xa_wkv, xa_wo, ln2_g, ln2_b, mlp_w_up, mlp_w_down, ln3_g, ln3_b, loss_target, m_in_norm_g, m_in_norm_b, m_w_in, m_conv_dw, m_conv_db, m_conv_norm_g, m_conv_norm_b, m_w_conv_out, m_ssm_log_step, m_ssm_lambda_re, m_ssm_lambda_im, m_ssm_b_re, m_ssm_b_im, m_ssm_c_re, m_ssm_c_im, m_ssm_d, m_w_ssm_glu, m_w_mix_out, m_ln1_g, m_ln1_b, m_xa_wq, m_xa_wkv, m_xa_wo, m_ln2_g, m_ln2_b, m_mlp_w_up, m_mlp_w_down, m_ln3_g, m_ln3_b, v_in_norm_g, v_in_norm_b, v_w_in, v_conv_dw, v_conv_db, v_conv_norm_g, v_conv_norm_b, v_w_conv_out, v_ssm_log_step, v_ssm_lambda_re, v_ssm_lambda_im, v_ssm_b_re, v_ssm_b_im, v_ssm_c_re, v_ssm_c_im, v_ssm_d, v_w_ssm_glu, v_w_mix_out, v_ln1_g, v_ln1_b, v_xa_wq, v_xa_wkv, v_xa_wo, v_ln2_g, v_ln2_b, v_mlp_w_up, v_mlp_w_down, v_ln3_g, v_ln3_b):
    D = D_MODEL
    xs = x[0]
    T = xs.shape[0]
    mems = mem[0]
    NM = mems.shape[0]
    tgt = loss_target[0]
    my_c = lax.axis_index("c")
    k_me = 2 * lax.axis_index("x") + lax.axis_index("y")
    c_arr = jnp.reshape(my_c, (1,)).astype(jnp.int32)
    k_arr = jnp.reshape(k_me, (1,)).astype(jnp.int32)

    sh_a = jnp.concatenate([w_conv_out[0], w_mix_out[0], xa_wq[0], xa_wo[0], mlp_w_down[0], mlp_w_up[0]], axis=0).astype(BF16)
    def own_block(shard):
        buf = jnp.zeros((N_CHIPS,) + shard.shape, shard.dtype)
        return lax.dynamic_update_slice(buf, shard[None], (k_me, 0, 0))

    dw_pad = jnp.pad(conv_dw[0], ((0, CONV_HALO - CONV_K), (0, 0)))
    ag_bufs = [own_block(s) for s in (sh_a, w_in[0].astype(BF16), xa_wkv[0].astype(BF16), w_ssm_glu[0].astype(BF16), dw_pad)]
    ag_pieces = [(1, 0, D), (4, 0, CONV_HALO), (0, GA_CONV_OUT, 256), (3, 0, D_SSM), (0, GA_MIX_OUT, 256), (0, GA_WQ, 256),
                 (2, 0, D), (0, GA_WO, 256), (0, GA_UP, D), (0, GA_DOWN, D)]
    ag_send, ag_recv, GA, GIN, GKV, GGLU, GDW = gather_start(ag_bufs, ag_pieces, name="gather_start")

    def w_rowshard(row0):
        return dict(b_spec=((N_CHIPS, 256, D), lambda i, j, k: (0, row0 // 256, 0)), b_view=(D, D), tn=D, tk=D)

    _, h0, h0b = ln_fwd(xs, in_norm_g, in_norm_b, name="ln0_fwd")
    (GIN,) = gather_wait(ag_send, ag_recv, [GIN], [(0, 0, D, 0)], h0b, name="gather_wait_in")
    p = mm_nn(h0b, GIN, ((None, D, 1152), lambda i, j, k: (j, 0, 0)), D_IN, tn=1152, tk=D, name="mm_w_in")[0]
    GA, GGLU, GDW = gather_wait(
        ag_send, ag_recv, [GA, GGLU, GDW],
        [(2, 0, CONV_HALO, 1), (0, GA_CONV_OUT, 256, 2), (1, 0, D_SSM, 3), (0, GA_MIX_OUT, 256, 4)], p, name="gather_wait_mixer")
    dw_taps = jnp.transpose(GDW, (1, 0, 2)).reshape(CONV_HALO, D)
    c_pre, actb = conv_fwd(p, dw_taps, conv_db, conv_norm_g[0].reshape(1, D), conv_norm_b[0].reshape(1, D), name="conv_fwd")
    ya = mm_nn(actb, GA, N=D, name="mm_conv_out", **w_rowshard(GA_CONV_OUT))[0]

    lstep, lre, lim = ssm_log_step[0], ssm_lambda_re[0], ssm_lambda_im[0]
    bre, bim, cre, cim = ssm_b_re[0], ssm_b_im[0], ssm_c_re[0], ssm_c_im[0]
    (ar, ai, bbr, bbi), disc_vjp = jax.vjp(_ssm_discretise, lstep, lre, lim, bre, bim)
    Br, Bi = _blockdiag_in(bbr), _blockdiag_in(bbi)
    Cr, Ci = _blockdiag_out(cre), _blockdiag_out(cim)
    pw_r, pw_i = _powers(ar.reshape(-1), ai.reshape(-1))
    dvec = ssm_d[0].reshape(1, D_SSM)
    xr, xi, yssm = ssm_fwd(p, Br, Bi, Cr, Ci, pw_r, pw_i, dvec, name="ssm_fwd")
    z = mm_nn(yssm, GGLU, ((None, D_SSM, 512), lambda i, j, k: (j, 0, 0)), 2 * D, tn=512, tk=D_SSM, name="mm_ssm_glu")[0]
    mergedb = merge_fwd(p, ya, z, name="merge_fwd")
    mix = mm_nn(mergedb, GA, N=D, name="mm_mix_out", **w_rowshard(GA_MIX_OUT))[0]
    r1, h1, h1b = ln_fwd(mix, ln1_g[0], ln1_b[0], res=h0, name="ln1_fwd")
    GA, GKV = gather_wait(ag_send, ag_recv, [GA, GKV], [(0, GA_WQ, 256, 5), (1, 0, D, 6), (0, GA_WO, 256, 7)], mix,
                          name="gather_wait_attn")

    qb = mm_nn(h1b, GA, N=D, out_dtype=BF16, name="mm_wq", **w_rowshard(GA_WQ))[0]
    kv = mm_nn(mems, GKV, ((None, D, 512), lambda i, j, k: (j, 0, 0)), 2 * D, tn=512, tk=D, name="mm_wkv")[0]
    ob = attn_fwd(qb, kv, name="attn_fwd")
    xa = mm_nn(ob, GA, N=D, name="mm_wo", **w_rowshard(GA_WO))[0]
    r2, h2, h2b = ln_fwd(xa, ln2_g[0], ln2_b[0], res=h1, name="ln2_fwd")
    (GA,) = gather_wait(ag_send, ag_recv, [GA], [(0, GA_UP, D, 8), (0, GA_DOWN, D, 9)], xa, name="gather_wait_mlp")

    def relu2(acc):
        zr = jnp.maximum(acc, 0.0)
        return acc, zr * zr

    zpre, zzb = mm_nn(h2b, GA, ((None, D, D), lambda i, j, k: (j, GA_UP // D, 0)), D_FF, tn=D, tk=D,
                      out_dtypes=[F32, BF16], epilogue=relu2, name="mm_up")
    ff = mm_nn(zzb, GA, ((N_CHIPS, D, D), lambda i, j, k: (0, GA_DOWN // D, 0)), D, tm=_pick(T, [512, 256, 128]), tn=D, tk=D_FF,
               b_view=(D_FF, D), name="mm_down")[0]
    r3, h3, _ = ln_fwd(ff, ln3_g[0], ln3_b[0], res=h2, name="ln3_fwd")
    dh3, sq = loss_head(h3, tgt, name="loss_head")
    loss = lax.psum(0.5 * sq[0, 0] / D, ("x", "y", "c"))

    def rs_begin(grads, tags, rnd):
        lands = exchange_halves(grads, name=f"rs{rnd}_exchange_halves")
        parts = [add_sibling(g, l, c_arr, name=f"rs{rnd}_add_sibling_{t}") for g, l, t in zip(grads, lands, tags)]
        return scatter_start(parts, name=f"rs{rnd}_scatter_start")

    g1_shape = jax.ShapeDtypeStruct((N_CHIPS, G1_ROWS, D), F32)
    g2_shape = jax.ShapeDtypeStruct((N_CHIPS, G2_ROWS, D), F32)
    dr3, dr3b, dg3, db3 = ln_bwd(r3, dh3, ln3_g[0], name="ln3_bwd")
    dzpreb = mm_nt(dr3b, GA, ((None, D, D), lambda i, j, k: (j, GA_DOWN // D, 0)), D_FF, tn=D, tk=D, out_dtype=BF16,
                   extras=(zpre,), epilogue=lambda acc, zp: (acc * (2.0 * jnp.maximum(zp, 0.0)),), name="mm_down_t")[0]
    G1g = mm_tn(zzb, dr3b, tm=D, tn=D, out_spec=((None, D, D), lambda i, j, k: (i, G1_DOWN // D, 0)), out_shape=g1_shape,
                name="mm_down_g")
    G1g = mm_tn(h2b, dzpreb, tm=D, tn=D, out_spec=((None, D, D), lambda i, j, k: (j, G1_UP // D, 0)), out_shape=g1_shape,
                out_buf=G1g, name="mm_up_g")
    round1 = rs_begin([G1g], ["mlp"], 1)
    dh2 = mm_nt(dzpreb, GA, ((None, D, D), lambda i, j, k: (k, GA_UP // D, 0)), D, tn=D, tk=D,
                extras=(dr3,), epilogue=lambda acc, d: (acc + ALPHA * d,), after=(round1[2][0],), name="mm_up_t")[0]
    dr2, dr2b, dg2, db2 = ln_bwd(r2, dh2, ln2_g[0], name="ln2_bwd")

    def g_rowshard(row0, out_buf):
        return dict(tm=D, tn=D, out_spec=((N_CHIPS, 256, D), lambda i, j, k: (0, row0 // 256, 0)), out_shape=g2_shape,
                    out_buf=out_buf)

    dob = mm_nt(dr2b, GA, N=D, out_dtype=BF16, name="mm_wo_t", **w_rowshard(GA_WO))[0]
    G2g = mm_tn(ob, dr2b, name="mm_wo_g", **g_rowshard(G2_WO, None))
    dqb, dkv = attn_bwd(qb, kv, dob, name="attn_bwd")
    G2g = mm_tn(h1b, dqb, name="mm_wq_g", **g_rowshard(G2_WQ, G2g))
    GKVg = mm_tn(mems, dkv, tm=D, tn=512, tk=NM, out_spec=((None, D, 512), lambda i, j, k: (j, 0, 0)),
                 out_shape=jax.ShapeDtypeStruct((N_CHIPS, D, 512), F32), name="mm_wkv_g")
    dh1 = mm_nt(dqb, GA, N=D, extras=(dr2,), epilogue=lambda acc, d: (acc + ALPHA * d,), name="mm_wq_t",
                **w_rowshard(GA_WQ))[0]
    dr1, dr1b, dg1, db1 = ln_bwd(r1, dh1, ln1_g[0], name="ln1_bwd")

    dmerged = mm_nt(dr1b, GA, N=D, name="mm_mix_t", **w_rowshard(GA_MIX_OUT))[0]
    G2g = mm_tn(mergedb, dr1b, name="mm_mix_g", **g_rowshard(G2_MIX_OUT, G2g))
    dyab, dgab, dgbb, dz1b, dz2b = merge_bwd(dmerged, p, ya, z, name="merge_bwd")
    dzb = jnp.concatenate([dz1b, dz2b], axis=1)
    GGLUg = mm_tn(yssm, dzb, tm=D_SSM, tn=512, out_spec=((None, D_SSM, 512), lambda i, j, k: (j, 0, 0)),
                  out_shape=jax.ShapeDtypeStruct((N_CHIPS, D_SSM, 512), F32), name="mm_glu_g")
    dyssm = mm_nt(dzb, GGLU, ((None, D_SSM, 512), lambda i, j, k: (k, 0, 0)), D_SSM, tn=D_SSM, tk=512, name="mm_glu_t")[0]
    pwc_r, pwc_i = pw_r, -pw_i
    dub, dBr, dBi, dCr, dCi, dar8, dai8, dd8 = ssm_bwd(dyssm, p, xr, xi, Br, Bi, Cr, Ci, pwc_r, pwc_i, dvec, name="ssm_bwd")
    dar = jnp.sum(dar8, axis=0).reshape(SSM_GROUPS, SSM_STATE)
    dai = jnp.sum(dai8, axis=0).reshape(SSM_GROUPS, SSM_STATE)
    g_lstep, g_lre, g_lim, g_bre, g_bim = disc_vjp((dar, dai, _diag_in(dBr), _diag_in(dBi)))
    g_cre, g_cim = _diag_out(dCr), _diag_out(dCi)
    g_d = jnp.sum(dd8, axis=0).reshape(1, D_SSM)

    dact = mm_nt(dyab, GA, N=D, name="mm_conv_out_t", **w_rowshard(GA_CONV_OUT))[0]
    G2g = mm_tn(actb, dyab, name="mm_conv_out_g", **g_rowshard(G2_CONV_OUT, G2g))
    round2 = rs_begin([G2g, GKVg, GGLUg], ["sq", "kv", "glu"], 2)
    dc, dng, dnb, ddb = conv_bwd_norm(dact, c_pre, conv_norm_g[0].reshape(1, D), conv_norm_b[0].reshape(1, D),
                                      round2[2][0], name="conv_bwd_norm")
    dvgb, ddw = conv_bwd_taps(dc, p, dw_taps, name="conv_bwd_taps")
    dpb = jnp.concatenate([dvgb, dub, dgab, dgbb], axis=1)
    GINg = mm_tn(h0b, dpb, tm=D, tn=1152, out_spec=((None, D, 1152), lambda i, j, k: (j, 0, 0)),
                 out_shape=jax.ShapeDtypeStruct((N_CHIPS, D, 1152), F32), name="mm_w_in_g")
    round3 = rs_begin([GINg], ["in"], 3)
    dh0 = mm_nt(dpb, GIN, ((None, 512, 1152), lambda i, j, k: (k, j, 0)), D, tn=512, tk=1152,
                extras=(dr1,), epilogue=lambda acc, d: (acc + ALPHA * d,), after=(round3[2][0],), name="mm_w_in_t")[0]
    gx, _, dg0, db0 = ln_bwd(xs, dh0, in_norm_g, name="ln0_bwd")

    kc_arr = jnp.concatenate([k_arr, c_arr])
    landed = scatter_wait([round1, round2, round3], gx, name="rs_scatter_wait")
    tags = ["mlp", "sq", "kv", "glu", "in"]
    pairs = [(pt, l2) for parts, lands2 in landed for pt, l2 in zip(parts, lands2)]
    halves = [add_chips(pt, l2, kc_arr, name="rs_add_chips_" + t) for (pt, l2), t in zip(pairs, tags)]
    g1, g2, gKV, gGLU, gIN = join_halves(halves, name="rs_join_halves")

    small_names = ["in_norm_g", "in_norm_b", "conv_db", "conv_norm_g", "conv_norm_b", "ssm_log_step", "ssm_lambda_re",
                   "ssm_lambda_im", "ssm_b_re", "ssm_b_im", "ssm_c_re", "ssm_c_im", "ssm_d", "ln1_g", "ln1_b",
                   "ln2_g", "ln2_b", "ln3_g", "ln3_b"]
    small_w = [in_norm_g, in_norm_b, conv_db, conv_norm_g, conv_norm_b, ssm_log_step, ssm_lambda_re, ssm_lambda_im,
               ssm_b_re, ssm_b_im, ssm_c_re, ssm_c_im, ssm_d, ln1_g, ln1_b, ln2_g, ln2_b, ln3_g, ln3_b]
    small_m = [m_in_norm_g, m_in_norm_b, m_conv_db, m_conv_norm_g, m_conv_norm_b, m_ssm_log_step, m_ssm_lambda_re,
               m_ssm_lambda_im, m_ssm_b_re, m_ssm_b_im, m_ssm_c_re, m_ssm_c_im, m_ssm_d, m_ln1_g, m_ln1_b, m_ln2_g,
               m_ln2_b, m_ln3_g, m_ln3_b]
    small_v = [v_in_norm_g, v_in_norm_b, v_conv_db, v_conv_norm_g, v_conv_norm_b, v_ssm_log_step, v_ssm_lambda_re,
               v_ssm_lambda_im, v_ssm_b_re, v_ssm_b_im, v_ssm_c_re, v_ssm_c_im, v_ssm_d, v_ln1_g, v_ln1_b, v_ln2_g,
               v_ln2_b, v_ln3_g, v_ln3_b]
    small_g = [dg0, db0, ddb, dng, dnb, g_lstep, g_lre, g_lim, g_bre, g_bim, g_cre, g_cim, g_d, dg1, db1, dg2, db2, dg3, db3]
    small_shapes = [w.shape for w in small_w]
    n_small_rows = _pack(small_w).shape[0]
    packed_g = _pack(small_g + [ddw])
    _, summed = allgather_sum(packed_g, name="allreduce_small")
    small_rows = sum(-(-math.prod(s) // 128) for s in small_shapes)
    ddw_full = summed[small_rows:small_rows + CONV_HALO * D // 128].reshape(CONV_HALO, D)
    g_dw = lax.dynamic_slice_in_dim(ddw_full, k_me * (D // N_CHIPS), D // N_CHIPS, axis=1)
    gs_packed = jnp.pad(summed[:small_rows], ((0, n_small_rows - small_rows), (0, 0)))

    res = {}

    def upd(nm, w, m, v, g_arr, row0=0):
        shp = w.shape
        w2, m2, v2 = (a.reshape(-1, shp[-1]) for a in (w, m, v))
        outs = adamw(w2, m2, v2, g_arr, row0, name="adamw_" + nm)
        res[nm] = tuple(o.reshape(shp) for o in outs)

    upd("w_conv_out", w_conv_out, m_w_conv_out, v_w_conv_out, g2, G2_CONV_OUT)
    upd("w_mix_out", w_mix_out, m_w_mix_out, v_w_mix_out, g2, G2_MIX_OUT)
    upd("xa_wq", xa_wq, m_xa_wq, v_xa_wq, g2, G2_WQ)
    upd("xa_wo", xa_wo, m_xa_wo, v_xa_wo, g2, G2_WO)
    upd("mlp_w_down", mlp_w_down, m_mlp_w_down, v_mlp_w_down, g1, G1_DOWN)
    upd("mlp_w_up", mlp_w_up, m_mlp_w_up, v_mlp_w_up, g1, G1_UP)
    upd("w_in", w_in, m_w_in, v_w_in, gIN)
    upd("xa_wkv", xa_wkv, m_xa_wkv, v_xa_wkv, gKV)
    upd("w_ssm_glu", w_ssm_glu, m_w_ssm_glu, v_w_ssm_glu, gGLU)
    pad_dw = lambda a: jnp.pad(a[0], ((0, CONV_HALO - CONV_K), (0, 0)))
    dw_outs = adamw(pad_dw(conv_dw), pad_dw(m_conv_dw), pad_dw(v_conv_dw), g_dw, 0, name="adamw_conv_dw")
    res["conv_dw"] = tuple(o[:CONV_K][None] for o in dw_outs)
    sm_outs = adamw(_pack(small_w), _pack(small_m), _pack(small_v), gs_packed, 0, name="adamw_small")
    sm_un = [_unpack(o, small_shapes) for o in sm_outs]
    for idx, nm in enumerate(small_names):
        res[nm] = tuple(sm_un[q][idx] for q in range(4))

    order = ["in_norm_g", "in_norm_b", "w_in", "conv_dw", "conv_db", "conv_norm_g", "conv_norm_b", "w_conv_out",
             "ssm_log_step", "ssm_lambda_re", "ssm_lambda_im", "ssm_b_re", "ssm_b_im", "ssm_c_re", "ssm_c_im", "ssm_d",
             "w_ssm_glu", "w_mix_out", "ln1_g", "ln1_b", "xa_wq", "xa_wkv", "xa_wo", "ln2_g", "ln2_b", "mlp_w_up",
             "mlp_w_down", "ln3_g", "ln3_b"]
    return (loss, gx[None], *[res[n][0] for n in order], *[res[n][1] for n in order],
            *[res[n][2] for n in order], *[res[n][3] for n in order])
```

```python
import functools
import math

import jax
import jax.numpy as jnp
from jax import lax
from jax.experimental import pallas as pl
from jax.experimental.pallas import tpu as pltpu

F32 = jnp.float32
BF16 = jnp.bfloat16
MESH = pl.DeviceIdType.MESH

D_MODEL = 1024
N_HEADS = 4
HEAD_DIM = D_MODEL // N_HEADS
CONV_K = 31
CONV_HALO = 32
D_SSM = 512
SSM_GROUPS = 32
SSM_GROUP = 16
SSM_STATE = 64
SSM_BLOCKS = 4
SSM_BLOCK_IN = D_SSM // SSM_BLOCKS
SSM_BLOCK_STATE = SSM_GROUPS * SSM_STATE // SSM_BLOCKS
D_FF = 4096
D_IN = 4608
LN_EPS = 1e-5
ALPHA = (2.0 * 1) ** 0.25
N_CHIPS = 4
N_DEV = 8
ADAM_LR, ADAM_B1, ADAM_B2, ADAM_EPS, ADAM_WD, ADAM_STEP = 0.001, 0.9, 0.999, 1e-08, 0.01, 10
VMEM_LIMIT_BYTES = 56 * 1024 * 1024


def _pick(dim, cands):
    for c in cands:
        if dim % c == 0:
            return c
    return dim


def _cparams(sem=None):
    return pltpu.CompilerParams(dimension_semantics=sem, vmem_limit_bytes=VMEM_LIMIT_BYTES)


def _sigmoid(x):
    return 1.0 / (1.0 + jnp.exp(-x))


_DIMS = {"nn": (((1,), (0,)), ((), ())), "nt": (((1,), (1,)), ((), ())), "tn": (((0,), (0,)), ((), ()))}


def matmul(a, b, *, mode, M, N, K, tm, tn, tk, a_spec, b_spec, out_specs, out_shapes, name,
           extras=(), extra_specs=(), epilogue=None, alias_buf=None, b_view=None, after=()):
    nk = K // tk
    ne = len(extras)
    no = len(out_shapes)
    na = (0 if alias_buf is None else 1) + len(after)
    dims = _DIMS[mode]

    def body(*refs):
        a_ref, b_ref = refs[0], refs[1]
        e_refs = refs[2:2 + ne]
        o_refs = refs[2 + ne + na:2 + ne + na + no]

        def finish(acc):
            outs = (acc,) if epilogue is None else epilogue(acc, *[r[...] for r in e_refs])
            for o, r in zip(outs, o_refs):
                r[...] = o.astype(r.dtype).reshape(r.shape)

        b_blk = b_ref[...] if b_view is None else b_ref[...].reshape(b_view)
        prod = lax.dot_general(a_ref[...].astype(BF16), b_blk.astype(BF16), dims, preferred_element_type=F32)
        if nk == 1:
            finish(prod)
        else:
            acc_ref = refs[-1]
            k = pl.program_id(2)

            @pl.when(k == 0)
            def _():
                acc_ref[...] = prod

            @pl.when(k > 0)
            def _():
                acc_ref[...] += prod

            @pl.when(k == nk - 1)
            def _():
                finish(acc_ref[...])

    in_specs = [pl.BlockSpec(*a_spec), pl.BlockSpec(*b_spec)] + [pl.BlockSpec(*s) for s in extra_specs]
    ins = [a, b, *extras]
    if alias_buf is not None:
        in_specs.append(pl.BlockSpec(memory_space=pl.ANY))
        ins.append(alias_buf)
    for dep in after:
        in_specs.append(pl.BlockSpec(memory_space=pl.ANY))
        ins.append(dep)
    res = pl.pallas_call(
        body,
        grid=(M // tm, N // tn, nk),
        in_specs=in_specs,
        out_specs=[pl.BlockSpec(*s) for s in out_specs],
        out_shape=out_shapes,
        scratch_shapes=[] if nk == 1 else [pltpu.VMEM((tm, tn), F32)],
        input_output_aliases={2 + ne: 0} if alias_buf is not None else {},
        compiler_params=_cparams(("parallel", "parallel", "arbitrary")),
        name=name,
    )(*ins)
    return res


def _mn(tm, tn):
    return ((tm, tn), lambda i, j, k: (i, j))


def mm_nn(a, b_arr, b_spec, N, *, name, tm=None, tn, tk, out_dtype=F32, extras=(), epilogue=None, out_dtypes=None,
          b_view=None):
    M, K = a.shape
    tm = tm or _pick(M, [1024, 512, 256, 128])
    dts = out_dtypes or [out_dtype]
    return matmul(a, b_arr, mode="nn", M=M, N=N, K=K, tm=tm, tn=tn, tk=tk,
                  a_spec=((tm, tk), lambda i, j, k: (i, k)), b_spec=b_spec, b_view=b_view,
                  out_specs=[_mn(tm, tn)] * len(dts), out_shapes=[jax.ShapeDtypeStruct((M, N), d) for d in dts],
                  extras=extras, extra_specs=[_mn(tm, tn)] * len(extras), epilogue=epilogue, name=name)


def mm_nt(a, b_arr, b_spec, N, *, name, tm=None, tn, tk, out_dtype=F32, extras=(), epilogue=None, out_dtypes=None,
          b_view=None, after=()):
    M, K = a.shape
    tm = tm or _pick(M, [1024, 512, 256, 128])
    dts = out_dtypes or [out_dtype]
    return matmul(a, b_arr, mode="nt", M=M, N=N, K=K, tm=tm, tn=tn, tk=tk, after=after,
                  a_spec=((tm, tk), lambda i, j, k: (i, k)), b_spec=b_spec, b_view=b_view,
                  out_specs=[_mn(tm, tn)] * len(dts), out_shapes=[jax.ShapeDtypeStruct((M, N), d) for d in dts],
                  extras=extras, extra_specs=[_mn(tm, tn)] * len(extras), epilogue=epilogue, name=name)


def mm_tn(a, b, *, name, tm, tn, tk=None, out_spec, out_shape, out_buf=None):
    K, M = a.shape
    N = b.shape[1]
    tk = tk or _pick(K, [2048, 1024, 512, 256, 128])
    return matmul(a, b, mode="tn", M=M, N=N, K=K, tm=tm, tn=tn, tk=tk,
                  a_spec=((tk, tm), lambda i, j, k: (k, i)), b_spec=((tk, tn), lambda i, j, k: (k, j)),
                  out_specs=[out_spec], out_shapes=[out_shape], alias_buf=out_buf, name=name)[0]


def _rows(tc, w, cb=0):
    return pl.BlockSpec((tc, w), lambda i: (i, cb))


def _const(shape):
    return pl.BlockSpec(shape, lambda i: tuple([0] * len(shape)))


def _ln_stats(r):
    mu = jnp.mean(r, axis=-1, keepdims=True)
    xc = r - mu
    var = jnp.mean(xc * xc, axis=-1, keepdims=True)
    rstd = lax.rsqrt(var + LN_EPS)
    return xc * rstd, rstd


def _rowsum8(v):
    tc, w = v.shape
    return jnp.sum(v.reshape(tc // 8, 8, w), axis=0)


def ln_fwd(x, g, b, *, name, res=None):
    T, D = x.shape
    tc = _pick(T, [512, 256, 128])
    has_res = res is not None

    def body(*refs):
        if has_res:
            x_ref, res_ref, g_ref, b_ref, r_ref, h_ref, hb_ref = refs
            r = ALPHA * res_ref[...] + x_ref[...]
            r_ref[...] = r
        else:
            x_ref, g_ref, b_ref, h_ref, hb_ref = refs
            r = x_ref[...]
        xhat, _ = _ln_stats(r)
        y = xhat * g_ref[...] + b_ref[...]
        h_ref[...] = y
        hb_ref[...] = y.astype(BF16)

    ins = [x] + ([res] if has_res else []) + [g.reshape(1, D), b.reshape(1, D)]
    in_specs = [_rows(tc, D)] * (2 if has_res else 1) + [_const((1, D))] * 2
    n_out = 3 if has_res else 2
    outs = pl.pallas_call(
        body, grid=(T // tc,), in_specs=in_specs, out_specs=[_rows(tc, D)] * n_out,
        out_shape=[jax.ShapeDtypeStruct((T, D), F32)] * (n_out - 1) + [jax.ShapeDtypeStruct((T, D), BF16)],
        compiler_params=_cparams(("arbitrary",)), name=name)(*ins)
    if has_res:
        return outs
    return (x,) + tuple(outs)


def ln_bwd(r, dy, g, *, name):
    T, D = r.shape
    tc = _pick(T, [512, 256, 128])
    nt = T // tc

    def body(r_ref, dy_ref, g_ref, dr_ref, drb_ref, dg_ref, db_ref, accg, accb):
        i = pl.program_id(0)

        @pl.when(i == 0)
        def _():
            accg[...] = jnp.zeros_like(accg)
            accb[...] = jnp.zeros_like(accb)

        xhat, rstd = _ln_stats(r_ref[...])
        dy = dy_ref[...]
        dxh = dy * g_ref[...]
        m1 = jnp.mean(dxh, axis=-1, keepdims=True)
        m2 = jnp.mean(dxh * xhat, axis=-1, keepdims=True)
        dr = rstd * (dxh - m1 - xhat * m2)
        dr_ref[...] = dr
        drb_ref[...] = dr.astype(BF16)
        accg[...] += _rowsum8(dy * xhat)
        accb[...] += _rowsum8(dy)

        @pl.when(i == nt - 1)
        def _():
            dg_ref[...] = jnp.sum(accg[...], axis=0, keepdims=True)
            db_ref[...] = jnp.sum(accb[...], axis=0, keepdims=True)

    return pl.pallas_call(
        body, grid=(nt,), in_specs=[_rows(tc, D), _rows(tc, D), _const((1, D))],
        out_specs=[_rows(tc, D), _rows(tc, D), _const((1, D)), _const((1, D))],
        out_shape=[jax.ShapeDtypeStruct((T, D), F32), jax.ShapeDtypeStruct((T, D), BF16),
                   jax.ShapeDtypeStruct((1, D), F32), jax.ShapeDtypeStruct((1, D), F32)],
        scratch_shapes=[pltpu.VMEM((8, D), F32), pltpu.VMEM((8, D), F32)],
        compiler_params=_cparams(("arbitrary",)), name=name)(r, dy, g.reshape(1, D))


def loss_head(y, target, *, name):
    T, D = y.shape
    tc = _pick(T, [512, 256, 128])
    nt = T // tc

    def body(y_ref, t_ref, dy_ref, loss_ref, acc):
        i = pl.program_id(0)

        @pl.when(i == 0)
        def _():
            acc[...] = jnp.zeros_like(acc)

        e = y_ref[...] - t_ref[...]
        dy_ref[...] = e * (1.0 / D)
        acc[...] += _rowsum8(e * e)

        @pl.when(i == nt - 1)
        def _():
            s = jnp.sum(jnp.sum(acc[...], axis=0, keepdims=True), axis=1, keepdims=True)
            loss_ref[...] = jnp.broadcast_to(s, (1, 128))

    return pl.pallas_call(
        body, grid=(nt,), in_specs=[_rows(tc, D), _rows(tc, D)],
        out_specs=[_rows(tc, D), _const((1, 128))],
        out_shape=[jax.ShapeDtypeStruct((T, D), F32), jax.ShapeDtypeStruct((1, 128), F32)],
        scratch_shapes=[pltpu.VMEM((8, D), F32)],
        compiler_params=_cparams(("arbitrary",)), name=name)(y, target)


def _halo_prev(tc):
    per = tc // CONV_HALO
    return lambda i: jnp.maximum(i * per - 1, 0)


CONV_ROWS = 32
CONV_TAP_GROUP = 4
CONV_TILE_UNROLL = 4


def _fill_shifts(S, nrows):
    for b in range(1, 8):
        S[b, 0:nrows - 8, :] = S[0, b:b + nrows - 8, :]


def _tap_sum(S, w_ref, offs, r0, nrows):
    acc = None
    for k, o in enumerate(offs):
        a, b = divmod(o, 8)
        term = w_ref[k:k + 1, :] * S[b, pl.ds(pl.multiple_of(r0 + 8 * a, 8), nrows), :]
        acc = term if acc is None else acc + term
    return acc


def conv_fwd(p, dw, db, ng, nb, *, name):
    T = p.shape[0]
    D = D_MODEL
    tc = _pick(T, [256, 128])
    prev = _halo_prev(tc)
    off = CONV_HALO - (CONV_K - 1)
    offs = [off + k for k in range(CONV_K)]

    def body(val_ref, gate_ref, valp_ref, gatep_ref, dw_ref, db_ref, ng_ref, nb_ref, c_ref, act_ref, S):
        i = pl.program_id(0)
        u_prev = valp_ref[...] * _sigmoid(gatep_ref[...])
        S[0, 0:CONV_HALO, :] = jnp.where(i > 0, u_prev, 0.0)
        S[0, CONV_HALO:CONV_HALO + tc, :] = val_ref[...] * _sigmoid(gate_ref[...])
        _fill_shifts(S, CONV_HALO + tc)

        def rows(j, carry):
            r0 = pl.multiple_of(j * CONV_ROWS, CONV_ROWS)
            c_ref[pl.ds(r0, CONV_ROWS), :] = _tap_sum(S, dw_ref, offs, r0, CONV_ROWS) + db_ref[...]
            return carry

        lax.fori_loop(0, tc // CONV_ROWS, rows, 0)
        c = c_ref[...]
        xhat, _ = _ln_stats(c)
        cn = xhat * ng_ref[...] + nb_ref[...]
        act_ref[...] = (cn * _sigmoid(cn)).astype(BF16)

    return pl.pallas_call(
        body, grid=(T // tc,),
        in_specs=[_rows(tc, D, 0), _rows(tc, D, 1),
                  pl.BlockSpec((CONV_HALO, D), lambda i: (prev(i), 0)), pl.BlockSpec((CONV_HALO, D), lambda i: (prev(i), 1)),
                  _const((CONV_HALO, D)), _const((1, D)), _const((1, D)), _const((1, D))],
        out_specs=[_rows(tc, D), _rows(tc, D)],
        out_shape=[jax.ShapeDtypeStruct((T, D), F32), jax.ShapeDtypeStruct((T, D), BF16)],
        scratch_shapes=[pltpu.VMEM((8, CONV_HALO + tc, D), F32)],
        compiler_params=_cparams(("arbitrary",)), name=name)(p, p, p, p, dw, db, ng, nb)


def conv_bwd_norm(dact, c_pre, ng, nb, after, *, name):
    T, D = c_pre.shape
    tc = _pick(T, [512, 256, 128])
    nt = T // tc

    def body(da_ref, c_ref, ng_ref, nb_ref, after_ref, dc_ref, dng_ref, dnb_ref, ddb_ref, accg, accb, accd):
        i = pl.program_id(0)

        @pl.when(i == 0)
        def _():
            accg[...] = jnp.zeros_like(accg)
            accb[...] = jnp.zeros_like(accb)
            accd[...] = jnp.zeros_like(accd)

        xhat, rstd = _ln_stats(c_ref[...])
        cn = xhat * ng_ref[...] + nb_ref[...]
        s = _sigmoid(cn)
        dcn = da_ref[...] * (s * (1.0 + cn * (1.0 - s)))
        dxh = dcn * ng_ref[...]
        m1 = jnp.mean(dxh, axis=-1, keepdims=True)
        m2 = jnp.mean(dxh * xhat, axis=-1, keepdims=True)
        dc = rstd * (dxh - m1 - xhat * m2)
        dc_ref[...] = dc
        accg[...] += _rowsum8(dcn * xhat)
        accb[...] += _rowsum8(dcn)
        accd[...] += _rowsum8(dc)

        @pl.when(i == nt - 1)
        def _():
            dng_ref[...] = jnp.sum(accg[...], axis=0, keepdims=True)
            dnb_ref[...] = jnp.sum(accb[...], axis=0, keepdims=True)
            ddb_ref[...] = jnp.sum(accd[...], axis=0, keepdims=True)

    vec = jax.ShapeDtypeStruct((1, D), F32)
    return pl.pallas_call(
        body, grid=(nt,), in_specs=[_rows(tc, D), _rows(tc, D), _const((1, D)), _const((1, D)), ANY],
        out_specs=[_rows(tc, D), _const((1, D)), _const((1, D)), _const((1, D))],
        out_shape=[jax.ShapeDtypeStruct((T, D), F32), vec, vec, vec],
        scratch_shapes=[pltpu.VMEM((8, D), F32)] * 3,
        compiler_params=_cparams(("arbitrary",)), name=name)(dact, c_pre, ng, nb, after)


def conv_bwd_taps(dc, p, dw, *, name):
    T, D = dc.shape
    tc = _pick(T, [256, 128])
    nt = T // tc
    per = tc // CONV_HALO
    prev = _halo_prev(tc)
    last_halo = T // CONV_HALO - 1
    nxt = lambda i: jnp.minimum((i + 1) * per, last_halo)
    off = CONV_HALO - (CONV_K - 1)

    def body(dc_ref, dcn_ref, val_ref, gate_ref, valp_ref, gatep_ref, dw_ref, dvg_ref, ddw_ref, ext_u, ext_d, acc):
        i = pl.program_id(0)

        @pl.when(i == 0)
        def _():
            acc[...] = jnp.zeros_like(acc)

        u_prev = valp_ref[...] * _sigmoid(gatep_ref[...])
        ext_u[0, 0:CONV_HALO, :] = jnp.where(i > 0, u_prev, 0.0)
        ext_u[0, CONV_HALO:CONV_HALO + tc, :] = val_ref[...] * _sigmoid(gate_ref[...])
        ext_d[0, 0:tc, :] = dc_ref[...]
        ext_d[0, tc:tc + CONV_HALO, :] = jnp.where(i < nt - 1, dcn_ref[...], 0.0)
        _fill_shifts(ext_u, CONV_HALO + tc)
        _fill_shifts(ext_d, CONV_HALO + tc)

        def rows(j, carry):
            r0 = pl.multiple_of(j * CONV_ROWS, CONV_ROWS)
            sl = pl.ds(r0, CONV_ROWS)
            du = _tap_sum(ext_d, dw_ref, [CONV_K - 1 - k for k in range(CONV_K)], r0, CONV_ROWS)
            sg = _sigmoid(gate_ref[sl, :])
            dvg_ref[sl, 0:D] = (du * sg).astype(BF16)
            dvg_ref[sl, D:2 * D] = (du * val_ref[sl, :] * sg * (1.0 - sg)).astype(BF16)
            return carry

        lax.fori_loop(0, tc // CONV_ROWS, rows, 0)

        for k0 in range(0, CONV_K, CONV_TAP_GROUP):
            ks = list(range(k0, min(k0 + CONV_TAP_GROUP, CONV_K)))

            def taps(j, accs, ks=ks):
                out = list(accs)
                for t in range(CONV_TILE_UNROLL):
                    r0 = pl.multiple_of((j * CONV_TILE_UNROLL + t) * 8, 8)
                    dct = dc_ref[pl.ds(r0, 8), :]
                    for q, k in enumerate(ks):
                        a, b = divmod(off + k, 8)
                        out[q] = out[q] + dct * ext_u[b, pl.ds(pl.multiple_of(r0 + 8 * a, 8), 8), :]
                return tuple(out)

            accs = lax.fori_loop(0, tc // (8 * CONV_TILE_UNROLL), taps, tuple(jnp.zeros((8, D), F32) for _ in ks))
            for k, a_k in zip(ks, accs):
                acc[k] += a_k

        @pl.when(i == nt - 1)
        def _():
            ddw_ref[...] = jnp.zeros_like(ddw_ref)
            for k in range(CONV_K):
                ddw_ref[k:k + 1, :] = jnp.sum(acc[k], axis=0, keepdims=True)

    return pl.pallas_call(
        body, grid=(nt,),
        in_specs=[_rows(tc, D), pl.BlockSpec((CONV_HALO, D), lambda i: (nxt(i), 0)),
                  _rows(tc, D, 0), _rows(tc, D, 1),
                  pl.BlockSpec((CONV_HALO, D), lambda i: (prev(i), 0)), pl.BlockSpec((CONV_HALO, D), lambda i: (prev(i), 1)),
                  _const((CONV_HALO, D))],
        out_specs=[_rows(tc, 2 * D), _const((CONV_HALO, D))],
        out_shape=[jax.ShapeDtypeStruct((T, 2 * D), BF16), jax.ShapeDtypeStruct((CONV_HALO, D), F32)],
        scratch_shapes=[pltpu.VMEM((8, CONV_HALO + tc, D), F32), pltpu.VMEM((8, CONV_HALO + tc, D), F32),
                        pltpu.VMEM((CONV_K, 8, D), F32)],
        compiler_params=_cparams(("arbitrary",)), name=name)(dc, dc, p, p, p, p, dw)


GATE_A0 = (2 * D_MODEL + D_SSM) // 512
GATE_B0 = GATE_A0 + 2


def merge_fwd(p, ya, z, *, name):
    T = p.shape[0]
    D = D_MODEL
    tc = _pick(T, [512, 256, 128])
    W = 512

    def body(ga_ref, gb_ref, ya_ref, z1_ref, z2_ref, o_ref):
        yb = z1_ref[...] * _sigmoid(z2_ref[...])
        o_ref[...] = (_sigmoid(ga_ref[...]) * ya_ref[...] + _sigmoid(gb_ref[...]) * yb).astype(BF16)

    return pl.pallas_call(
        body, grid=(T // tc, D // W),
        in_specs=[pl.BlockSpec((tc, W), lambda i, j: (i, GATE_A0 + j)), pl.BlockSpec((tc, W), lambda i, j: (i, GATE_B0 + j)),
                  pl.BlockSpec((tc, W), lambda i, j: (i, j)), pl.BlockSpec((tc, W), lambda i, j: (i, j)),
                  pl.BlockSpec((tc, W), lambda i, j: (i, D // W + j))],
        out_specs=pl.BlockSpec((tc, W), lambda i, j: (i, j)),
        out_shape=jax.ShapeDtypeStruct((T, D), BF16),
        compiler_params=_cparams(("arbitrary", "arbitrary")), name=name)(p, p, ya, z, z)


def merge_bwd(dm, p, ya, z, *, name):
    T = p.shape[0]
    D = D_MODEL
    tc = _pick(T, [512, 256, 128])
    W = 512
    nb = D // W

    def body(dm_ref, ga_ref, gb_ref, ya_ref, z1_ref, z2_ref, dya_ref, dga_ref, dgb_ref, dz1_ref, dz2_ref):
        dm = dm_ref[...]
        sa = _sigmoid(ga_ref[...])
        sb = _sigmoid(gb_ref[...])
        s2 = _sigmoid(z2_ref[...])
        z1 = z1_ref[...]
        yb = z1 * s2
        dya_ref[...] = (dm * sa).astype(BF16)
        dga_ref[...] = (dm * ya_ref[...] * sa * (1.0 - sa)).astype(BF16)
        dgb_ref[...] = (dm * yb * sb * (1.0 - sb)).astype(BF16)
        dyb = dm * sb
        dz1_ref[...] = (dyb * s2).astype(BF16)
        dz2_ref[...] = (dyb * z1 * s2 * (1.0 - s2)).astype(BF16)

    blk = lambda off: pl.BlockSpec((tc, W), lambda i, j: (i, off + j))
    dya, dga, dgb, dz1, dz2 = pl.pallas_call(
        body, grid=(T // tc, nb),
        in_specs=[blk(0), blk(GATE_A0), blk(GATE_B0), blk(0), blk(0), blk(nb)],
        out_specs=[blk(0)] * 5,
        out_shape=[jax.ShapeDtypeStruct((T, D), BF16)] * 5,
        compiler_params=_cparams(("arbitrary", "arbitrary")), name=name)(dm, p, p, ya, z, z)
    return dya, dga, dgb, dz1, dz2


def _scan_block(src_r, src_i, dst_r, dst_i, car_r, car_i, pw_r, pw_i, cw_r, cw_i, ntiles, reverse, extra=None):
    W = src_r.shape[1]
    rows = lax.broadcasted_iota(jnp.int32, (8, W), 0)
    steps = []
    for d, pr in ((1, 0), (2, 1), (4, 3)):
        valid = rows < 8 - d if reverse else rows >= d
        steps.append((d, jnp.where(valid, jnp.broadcast_to(pw_r[pr:pr + 1, :], (8, W)), 0.0),
                      jnp.where(valid, jnp.broadcast_to(pw_i[pr:pr + 1, :], (8, W)), 0.0)))
    cw_r, cw_i = cw_r[...], cw_i[...]

    def tile(jj, carry):
        j = ntiles - 1 - jj if reverse else jj
        sl = pl.ds(pl.multiple_of(j * 8, 8), 8)
        xr, xi = src_r[sl, :], src_i[sl, :]
        for d, lr, li in steps:
            sr = pltpu.roll(xr, 8 - d if reverse else d, 0)
            si = pltpu.roll(xi, 8 - d if reverse else d, 0)
            xr, xi = xr + lr * sr - li * si, xi + lr * si + li * sr
        cr, ci = car_r[...], car_i[...]
        xr, xi = xr + cw_r * cr - cw_i * ci, xi + cw_r * ci + cw_i * cr
        dst_r[sl, :] = xr
        dst_i[sl, :] = xi
        edge = 0 if reverse else 7
        car_r[...] = jnp.broadcast_to(xr[edge:edge + 1, :], (8, W))
        car_i[...] = jnp.broadcast_to(xi[edge:edge + 1, :], (8, W))
        if extra is not None:
            carry = extra(j, xr, xi, carry)
        return carry

    return tile


def ssm_fwd(p, Br, Bi, Cr, Ci, pw_r, pw_i, dvec, *, name):
    T = p.shape[0]
    tt = _pick(T, [256, 128])
    nt = T // tt
    WI, WS = SSM_BLOCK_IN, SSM_BLOCK_STATE
    u0 = 2 * D_MODEL // WI

    def body(u_ref, br_ref, bi_ref, cr_ref, ci_ref, pwr_ref, pwi_ref, d_ref, xr_ref, xi_ref, y_ref, bur, bui, car_r, car_i):
        i = pl.program_id(1)

        @pl.when(i == 0)
        def _():
            car_r[...] = jnp.zeros_like(car_r)
            car_i[...] = jnp.zeros_like(car_i)

        u = u_ref[...]
        ub = u.astype(BF16)
        bur[...] = jnp.dot(ub, br_ref[...].astype(BF16), preferred_element_type=F32)
        bui[...] = jnp.dot(ub, bi_ref[...].astype(BF16), preferred_element_type=F32)
        tile = _scan_block(bur, bui, xr_ref, xi_ref, car_r, car_i, pwr_ref, pwi_ref, pwr_ref, pwi_ref, tt // 8, False)
        lax.fori_loop(0, tt // 8, tile, 0)
        y = (jnp.dot(xr_ref[...].astype(BF16), cr_ref[...].astype(BF16), preferred_element_type=F32)
             - jnp.dot(xi_ref[...].astype(BF16), ci_ref[...].astype(BF16), preferred_element_type=F32)
             + d_ref[...] * u)
        y_ref[...] = y.astype(BF16)

    return pl.pallas_call(
        body, grid=(SSM_BLOCKS, nt),
        in_specs=[pl.BlockSpec((tt, WI), lambda b, i: (i, u0 + b)),
                  pl.BlockSpec((None, WI, WS), lambda b, i: (b, 0, 0)), pl.BlockSpec((None, WI, WS), lambda b, i: (b, 0, 0)),
                  pl.BlockSpec((None, WS, WI), lambda b, i: (b, 0, 0)), pl.BlockSpec((None, WS, WI), lambda b, i: (b, 0, 0)),
                  pl.BlockSpec((8, WS), lambda b, i: (0, b)), pl.BlockSpec((8, WS), lambda b, i: (0, b)),
                  pl.BlockSpec((1, WI), lambda b, i: (0, b))],
        out_specs=[pl.BlockSpec((tt, WS), lambda b, i: (i, b)), pl.BlockSpec((tt, WS), lambda b, i: (i, b)),
                   pl.BlockSpec((tt, WI), lambda b, i: (i, b))],
        out_shape=[jax.ShapeDtypeStruct((T, SSM_BLOCKS * WS), F32)] * 2 + [jax.ShapeDtypeStruct((T, D_SSM), BF16)],
        scratch_shapes=[pltpu.VMEM((tt, WS), F32), pltpu.VMEM((tt, WS), F32), pltpu.VMEM((8, WS), F32), pltpu.VMEM((8, WS), F32)],
        compiler_params=_cparams(("arbitrary", "arbitrary")), name=name)(p, Br, Bi, Cr, Ci, pw_r, pw_i, dvec)


def ssm_bwd(dy, p, xr, xi, Br, Bi, Cr, Ci, pwc_r, pwc_i, dvec, *, name):
    T = p.shape[0]
    tt = _pick(T, [256, 128])
    nt = T // tt
    WI, WS = SSM_BLOCK_IN, SSM_BLOCK_STATE
    u0 = 2 * D_MODEL // WI
    tb = lambda i: nt - 1 - i
    xprev = lambda i: jnp.maximum(tb(i) * (tt // 8) - 1, 0)
    tn_dims = _DIMS["tn"]
    nt_dims = _DIMS["nt"]

    def body(dy_ref, u_ref, xr_ref, xi_ref, xpr_ref, xpi_ref, br_ref, bi_ref, cr_ref, ci_ref, pwr_ref, pwi_ref,
             cwr_ref, cwi_ref, d_ref,
             du_ref, dbr_ref, dbi_ref, dcr_ref, dci_ref, dar_ref, dai_ref, dd_ref,
             gr, gi, ext_r, ext_i, car_r, car_i):
        i = pl.program_id(1)

        @pl.when(i == 0)
        def _():
            car_r[...] = jnp.zeros_like(car_r)
            car_i[...] = jnp.zeros_like(car_i)
            dbr_ref[...] = jnp.zeros_like(dbr_ref)
            dbi_ref[...] = jnp.zeros_like(dbi_ref)
            dcr_ref[...] = jnp.zeros_like(dcr_ref)
            dci_ref[...] = jnp.zeros_like(dci_ref)
            dar_ref[...] = jnp.zeros_like(dar_ref)
            dai_ref[...] = jnp.zeros_like(dai_ref)
            dd_ref[...] = jnp.zeros_like(dd_ref)

        dy = dy_ref[...]
        dyb = dy.astype(BF16)
        u = u_ref[...]
        ub = u.astype(BF16)
        gr[...] = lax.dot_general(dyb, cr_ref[...].astype(BF16), nt_dims, preferred_element_type=F32)
        gi[...] = -lax.dot_general(dyb, ci_ref[...].astype(BF16), nt_dims, preferred_element_type=F32)
        first = tb(i) == 0
        ext_r[0:8, :] = jnp.where(first, 0.0, xpr_ref[...])
        ext_i[0:8, :] = jnp.where(first, 0.0, xpi_ref[...])
        ext_r[8:8 + tt, :] = xr_ref[...]
        ext_i[8:8 + tt, :] = xi_ref[...]
        rows = lax.broadcasted_iota(jnp.int32, (8, WS), 0)

        def lam_grad(j, g_r, g_i, carry):
            a_r, a_i = carry
            cur = pl.ds(pl.multiple_of(j * 8 + 8, 8), 8)
            prv = pl.ds(pl.multiple_of(j * 8, 8), 8)
            xc_r, xc_i = ext_r[cur, :], ext_i[cur, :]
            xl_r, xl_i = ext_r[prv, :], ext_i[prv, :]
            xp_r = jnp.where(rows == 0, jnp.broadcast_to(xl_r[7:8, :], (8, WS)), pltpu.roll(xc_r, 1, 0))
            xp_i = jnp.where(rows == 0, jnp.broadcast_to(xl_i[7:8, :], (8, WS)), pltpu.roll(xc_i, 1, 0))
            return (a_r + g_r * xp_r + g_i * xp_i, a_i + g_i * xp_r - g_r * xp_i)

        tile = _scan_block(gr, gi, gr, gi, car_r, car_i, pwr_ref, pwi_ref, cwr_ref, cwi_ref, tt // 8, True, extra=lam_grad)
        z8 = jnp.zeros((8, WS), F32)
        a_r, a_i = lax.fori_loop(0, tt // 8, tile, (z8, z8))
        dar_ref[...] += a_r
        dai_ref[...] += a_i
        grb = gr[...].astype(BF16)
        gib = gi[...].astype(BF16)
        dbr_ref[...] += lax.dot_general(ub, grb, tn_dims, preferred_element_type=F32)
        dbi_ref[...] += lax.dot_general(ub, gib, tn_dims, preferred_element_type=F32)
        dcr_ref[...] += lax.dot_general(xr_ref[...].astype(BF16), dyb, tn_dims, preferred_element_type=F32)
        dci_ref[...] -= lax.dot_general(xi_ref[...].astype(BF16), dyb, tn_dims, preferred_element_type=F32)
        du = (lax.dot_general(grb, br_ref[...].astype(BF16), nt_dims, preferred_element_type=F32)
              + lax.dot_general(gib, bi_ref[...].astype(BF16), nt_dims, preferred_element_type=F32)
              + d_ref[...] * dy)
        du_ref[...] = du.astype(BF16)
        dd_ref[...] += _rowsum8(dy * u)

    wspec = lambda shp: pl.BlockSpec((None,) + shp, lambda b, i: (b, 0, 0))
    return pl.pallas_call(
        body, grid=(SSM_BLOCKS, nt),
        in_specs=[pl.BlockSpec((tt, WI), lambda b, i: (tb(i), b)),
                  pl.BlockSpec((tt, WI), lambda b, i: (tb(i), u0 + b)),
                  pl.BlockSpec((tt, WS), lambda b, i: (tb(i), b)), pl.BlockSpec((tt, WS), lambda b, i: (tb(i), b)),
                  pl.BlockSpec((8, WS), lambda b, i: (xprev(i), b)), pl.BlockSpec((8, WS), lambda b, i: (xprev(i), b)),
                  wspec((WI, WS)), wspec((WI, WS)), wspec((WS, WI)), wspec((WS, WI)),
                  pl.BlockSpec((8, WS), lambda b, i: (0, b)), pl.BlockSpec((8, WS), lambda b, i: (0, b)),
                  pl.BlockSpec((8, WS), lambda b, i: (0, b)), pl.BlockSpec((8, WS), lambda b, i: (0, b)),
                  pl.BlockSpec((1, WI), lambda b, i: (0, b))],
        out_specs=[pl.BlockSpec((tt, WI), lambda b, i: (tb(i), b)),
                   wspec((WI, WS)), wspec((WI, WS)), wspec((WS, WI)), wspec((WS, WI)),
                   pl.BlockSpec((8, WS), lambda b, i: (0, b)), pl.BlockSpec((8, WS), lambda b, i: (0, b)),
                   pl.BlockSpec((8, WI), lambda b, i: (0, b))],
        out_shape=[jax.ShapeDtypeStruct((T, D_SSM), BF16),
                   jax.ShapeDtypeStruct((SSM_BLOCKS, WI, WS), F32), jax.ShapeDtypeStruct((SSM_BLOCKS, WI, WS), F32),
                   jax.ShapeDtypeStruct((SSM_BLOCKS, WS, WI), F32), jax.ShapeDtypeStruct((SSM_BLOCKS, WS, WI), F32),
                   jax.ShapeDtypeStruct((8, SSM_BLOCKS * WS), F32), jax.ShapeDtypeStruct((8, SSM_BLOCKS * WS), F32),
                   jax.ShapeDtypeStruct((8, D_SSM), F32)],
        scratch_shapes=[pltpu.VMEM((tt, WS), F32), pltpu.VMEM((tt, WS), F32),
                        pltpu.VMEM((tt + 8, WS), F32), pltpu.VMEM((tt + 8, WS), F32),
                        pltpu.VMEM((8, WS), F32), pltpu.VMEM((8, WS), F32)],
        compiler_params=_cparams(("arbitrary", "arbitrary")), name=name,
    )(dy, p, xr, xi, xr, xi, Br, Bi, Cr, Ci, pwc_r, pwc_i, pwc_r[::-1], pwc_i[::-1], dvec)


def _ssm_discretise(log_step, lam_re, lam_im, b_re, b_im):
    step = jnp.exp(log_step)[:, None]
    mag = jnp.exp(lam_re * step)
    ar = mag * jnp.cos(lam_im * step)
    ai = mag * jnp.sin(lam_im * step)
    den = lam_re * lam_re + lam_im * lam_im
    nr = ar - 1.0
    cr = (nr * lam_re + ai * lam_im) / den
    ci = (ai * lam_re - nr * lam_im) / den
    bbr = cr[..., None] * b_re - ci[..., None] * b_im
    bbi = cr[..., None] * b_im + ci[..., None] * b_re
    return ar, ai, bbr, bbi


def _blockdiag_in(bb):
    t = jnp.transpose(bb, (0, 2, 1)).reshape(SSM_BLOCKS, 8, SSM_GROUP, SSM_STATE)
    eye = jnp.eye(8, dtype=bb.dtype)
    return (t[:, :, :, None, :] * eye[None, :, None, :, None]).reshape(SSM_BLOCKS, SSM_BLOCK_IN, SSM_BLOCK_STATE)


def _blockdiag_out(cc):
    t = jnp.transpose(cc, (0, 2, 1)).reshape(SSM_BLOCKS, 8, SSM_STATE, SSM_GROUP)
    eye = jnp.eye(8, dtype=cc.dtype)
    return (t[:, :, :, None, :] * eye[None, :, None, :, None]).reshape(SSM_BLOCKS, SSM_BLOCK_STATE, SSM_BLOCK_IN)


def _diag_in(d):
    t = d.reshape(SSM_BLOCKS, 8, SSM_GROUP, 8, SSM_STATE)
    t = jnp.einsum("bghgp->bghp", t).reshape(SSM_GROUPS, SSM_GROUP, SSM_STATE)
    return jnp.transpose(t, (0, 2, 1))


def _diag_out(d):
    t = d.reshape(SSM_BLOCKS, 8, SSM_STATE, 8, SSM_GROUP)
    t = jnp.einsum("bgpgh->bgph", t).reshape(SSM_GROUPS, SSM_STATE, SSM_GROUP)
    return jnp.transpose(t, (0, 2, 1))


def _powers(ar, ai):
    rs, is_ = [ar], [ai]
    for _ in range(7):
        r, i = rs[-1], is_[-1]
        rs.append(r * ar - i * ai)
        is_.append(r * ai + i * ar)
    return jnp.stack(rs), jnp.stack(is_)


def attn_fwd(q, kv, *, name):
    T, D = q.shape
    nm = kv.shape[0]
    tq = _pick(T, [512, 256, 128])
    scale = HEAD_DIM ** -0.5

    def body(q_ref, k_ref, v_ref, o_ref):
        for h in range(N_HEADS):
            sl = slice(h * HEAD_DIM, (h + 1) * HEAD_DIM)
            s = lax.dot_general(q_ref[:, sl], k_ref[:, sl].astype(BF16), _DIMS["nt"], preferred_element_type=F32) * scale
            e = jnp.exp(s - jnp.max(s, axis=-1, keepdims=True))
            pr = e / jnp.sum(e, axis=-1, keepdims=True)
            o_ref[:, sl] = jnp.dot(pr.astype(BF16), v_ref[:, sl].astype(BF16), preferred_element_type=F32).astype(BF16)

    return pl.pallas_call(
        body, grid=(T // tq,),
        in_specs=[_rows(tq, D), pl.BlockSpec((nm, D), lambda i: (0, 0)), pl.BlockSpec((nm, D), lambda i: (0, 1))],
        out_specs=_rows(tq, D), out_shape=jax.ShapeDtypeStruct((T, D), BF16),
        compiler_params=_cparams(("arbitrary",)), name=name)(q, kv, kv)


def attn_bwd(q, kv, do, *, name):
    T, D = q.shape
    nm = kv.shape[0]
    tq = _pick(T, [512, 256, 128])
    nt = T // tq
    scale = HEAD_DIM ** -0.5

    def body(q_ref, k_ref, v_ref, do_ref, dq_ref, dkv_ref):
        i = pl.program_id(0)

        @pl.when(i == 0)
        def _():
            dkv_ref[...] = jnp.zeros_like(dkv_ref)

        for h in range(N_HEADS):
            sl = slice(h * HEAD_DIM, (h + 1) * HEAD_DIM)
            slv = slice(D + h * HEAD_DIM, D + (h + 1) * HEAD_DIM)
            qh = q_ref[:, sl]
            kh = k_ref[:, sl].astype(BF16)
            vh = v_ref[:, sl].astype(BF16)
            doh = do_ref[:, sl].astype(BF16)
            s = lax.dot_general(qh, kh, _DIMS["nt"], preferred_element_type=F32) * scale
            e = jnp.exp(s - jnp.max(s, axis=-1, keepdims=True))
            pr = e / jnp.sum(e, axis=-1, keepdims=True)
            dp = lax.dot_general(doh, vh, _DIMS["nt"], preferred_element_type=F32)
            ds = (pr * (dp - jnp.sum(pr * dp, axis=-1, keepdims=True)) * scale).astype(BF16)
            dq_ref[:, sl] = jnp.dot(ds, kh, preferred_element_type=F32).astype(BF16)
            dkv_ref[:, sl] += lax.dot_general(ds, qh, _DIMS["tn"], preferred_element_type=F32)
            dkv_ref[:, slv] += lax.dot_general(pr.astype(BF16), doh, _DIMS["tn"], preferred_element_type=F32)

    return pl.pallas_call(
        body, grid=(nt,),
        in_specs=[_rows(tq, D), pl.BlockSpec((nm, D), lambda i: (0, 0)), pl.BlockSpec((nm, D), lambda i: (0, 1)), _rows(tq, D)],
        out_specs=[_rows(tq, D), _const((nm, 2 * D))],
        out_shape=[jax.ShapeDtypeStruct((T, D), BF16), jax.ShapeDtypeStruct((nm, 2 * D), F32)],
        compiler_params=_cparams(("arbitrary",)), name=name)(q, kv, kv, do)


def _adam_math(w, g, m, v):
    m = ADAM_B1 * m + (1.0 - ADAM_B1) * g
    v = ADAM_B2 * v + (1.0 - ADAM_B2) * (g * g)
    m_hat = m / (1.0 - ADAM_B1 ** ADAM_STEP)
    v_hat = v / (1.0 - ADAM_B2 ** ADAM_STEP)
    delta = -ADAM_LR * (m_hat / (jnp.sqrt(v_hat) + ADAM_EPS) + ADAM_WD * w)
    return delta, m, v


def adamw(w, m, v, g_arr, g_row0, *, name):
    R, C = w.shape
    tr = _pick(R, [256, 128, 64, 32, 16, 8])
    assert g_row0 % tr == 0
    g0 = g_row0 // tr

    def body(w_ref, m_ref, v_ref, g_ref, go_ref, d_ref, mo_ref, vo_ref):
        g = g_ref[...]
        d, mn, vn = _adam_math(w_ref[...], g, m_ref[...], v_ref[...])
        go_ref[...] = g
        d_ref[...] = d
        mo_ref[...] = mn
        vo_ref[...] = vn

    sp = pl.BlockSpec((tr, C), lambda i: (i, 0))
    return pl.pallas_call(
        body, grid=(R // tr,), in_specs=[sp, sp, sp, pl.BlockSpec((tr, C), lambda i: (g0 + i, 0))],
        out_specs=[sp] * 4, out_shape=[jax.ShapeDtypeStruct((R, C), F32)] * 4,
        compiler_params=_cparams(("arbitrary",)), name=name)(w, m, v, g_arr)


def _place():
    x, y, c = lax.axis_index("x"), lax.axis_index("y"), lax.axis_index("c")
    chips = [(1 - x, y), (x, 1 - y), (1 - x, 1 - y)]
    return x, y, c, chips


ANY = pl.BlockSpec(memory_space=pl.ANY)


def allgather_weights(bufs, *, name):
    n = len(bufs)

    def body(*refs):
        o_refs = refs[n:2 * n]
        send_sems, recv_sems, fsend_sems, frecv_sems = refs[2 * n:]
        x, y, c, chips = _place()
        k_me = 2 * x + y
        sib = (x, y, 1 - c)
        halves = [b.shape[1] // 2 for b in bufs]

        def half(a, cc):
            return pl.ds(pl.multiple_of(cc * halves[a], 16), halves[a])

        sends = []
        for a in range(n):
            for r, (px, py) in enumerate(chips):
                cp = pltpu.make_async_remote_copy(
                    src_ref=o_refs[a].at[k_me, half(a, c)], dst_ref=o_refs[a].at[k_me, half(a, c)],
                    send_sem=send_sems.at[3 * a + r], recv_sem=recv_sems.at[3 * a + r],
                    device_id=(px, py, c), device_id_type=MESH)
                cp.start()
                sends.append(cp)
        passed = []
        for a in range(n):
            for r, (px, py) in enumerate(chips):
                win = o_refs[a].at[2 * px + py, half(a, c)]
                pltpu.make_async_remote_copy(
                    src_ref=win, dst_ref=win, send_sem=send_sems.at[3 * a + r], recv_sem=recv_sems.at[3 * a + r],
                    device_id=(px, py, c), device_id_type=MESH).wait_recv()
                cp = pltpu.make_async_remote_copy(
                    src_ref=win, dst_ref=win, send_sem=fsend_sems.at[3 * a + r], recv_sem=frecv_sems.at[3 * a + r],
                    device_id=sib, device_id_type=MESH)
                cp.start()
                passed.append(cp)
        for a in range(n):
            for r, (px, py) in enumerate(chips):
                win = o_refs[a].at[2 * px + py, half(a, 1 - c)]
                pltpu.make_async_remote_copy(
                    src_ref=win, dst_ref=win, send_sem=fsend_sems.at[3 * a + r], recv_sem=frecv_sems.at[3 * a + r],
                    device_id=sib, device_id_type=MESH).wait_recv()
        for cp in sends + passed:
            cp.wait_send()

    return pl.pallas_call(
        body, in_specs=[ANY] * n, out_specs=[ANY] * n,
        out_shape=[jax.ShapeDtypeStruct(b.shape, b.dtype) for b in bufs],
        scratch_shapes=[pltpu.SemaphoreType.DMA((3 * n,))] * 4,
        input_output_aliases={a: a for a in range(n)},
        name=name)(*bufs)


HBM_SPEC = pl.BlockSpec(memory_space=pltpu.HBM)
SEM_SPEC = pl.BlockSpec(memory_space=pltpu.SEMAPHORE)


def _hbm(a):
    return pltpu.with_memory_space_constraint(a, pltpu.HBM)


def gather_start(bufs, pieces, *, name):
    n = len(bufs)
    npc = len(pieces)

    def body(*refs):
        b_refs = refs[:n]
        send_sems, recv_sems = refs[n], refs[n + 1]
        x, y, c, chips = _place()
        k_me = 2 * x + y
        for q, (a, row0, rows) in enumerate(pieces):
            win = b_refs[a].at[k_me, pl.ds(row0, rows)]
            for r, (px, py) in enumerate(chips):
                pltpu.make_async_remote_copy(
                    src_ref=win, dst_ref=win, send_sem=send_sems.at[3 * q + r], recv_sem=recv_sems.at[3 * q + r],
                    device_id=(px, py, c), device_id_type=MESH).start()

    return pl.pallas_call(
        body, in_specs=[HBM_SPEC] * n, out_specs=[SEM_SPEC, SEM_SPEC] + [HBM_SPEC] * n,
        out_shape=[pltpu.SemaphoreType.DMA((3 * npc,)), pltpu.SemaphoreType.DMA((3 * npc,))]
        + [pltpu.HBM(b.shape, b.dtype) for b in bufs],
        input_output_aliases={a: 2 + a for a in range(n)},
        compiler_params=pltpu.CompilerParams(has_side_effects=pltpu.SideEffectType.DATAFLOW_SIDE_EFFECTING),
        name=name)(*[_hbm(b) for b in bufs])


def gather_wait(send_sems, recv_sems, bufs, which, after, *, name):
    n = len(bufs)

    def body(*refs):
        b_refs = refs[:n]
        send_sems, recv_sems = refs[n], refs[n + 1]
        x, y, c, chips = _place()
        k_me = 2 * x + y
        for a, row0, rows, q in which:
            for r, (px, py) in enumerate(chips):
                cp = pltpu.make_async_remote_copy(
                    src_ref=b_refs[a].at[k_me, pl.ds(row0, rows)], dst_ref=b_refs[a].at[2 * px + py, pl.ds(row0, rows)],
                    send_sem=send_sems.at[3 * q + r], recv_sem=recv_sems.at[3 * q + r],
                    device_id=(px, py, c), device_id_type=MESH)
                cp.wait_send()
                cp.wait_recv()

    return pl.pallas_call(
        body, in_specs=[HBM_SPEC] * n + [SEM_SPEC, SEM_SPEC, ANY], out_specs=[HBM_SPEC] * n,
        out_shape=[pltpu.HBM(b.shape, b.dtype) for b in bufs],
        input_output_aliases={a: a for a in range(n)},
        compiler_params=pltpu.CompilerParams(has_side_effects=pltpu.SideEffectType.DATAFLOW_SIDE_EFFECTING),
        name=name)(*bufs, send_sems, recv_sems, after)


def exchange_halves(grads, *, name):
    n = len(grads)

    def body(*refs):
        g_refs, l_refs = refs[:n], refs[n:2 * n]
        send_sems, recv_sems = refs[2 * n:]
        x, y, c, _ = _place()
        cps = []
        for a in range(n):
            h = grads[a].shape[1] // 2
            cp = pltpu.make_async_remote_copy(
                src_ref=g_refs[a].at[:, pl.ds(pl.multiple_of((1 - c) * h, 8), h)], dst_ref=l_refs[a],
                send_sem=send_sems.at[a], recv_sem=recv_sems.at[a], device_id=(x, y, 1 - c), device_id_type=MESH)
            cp.start()
            cps.append(cp)
        for cp in cps:
            cp.wait()

    return pl.pallas_call(
        body, in_specs=[ANY] * n, out_specs=[ANY] * n,
        out_shape=[jax.ShapeDtypeStruct((g.shape[0], g.shape[1] // 2, g.shape[2]), g.dtype) for g in grads],
        scratch_shapes=[pltpu.SemaphoreType.DMA((n,))] * 2,
        name=name)(*grads)


N_PEERS = N_DEV - 1


def _scatter_copies(p_refs, l_refs, send_sems, recv_sems):
    x, y, c, _ = _place()
    cps = []
    for a in range(len(p_refs)):
        h = p_refs[a].shape[1] // 2
        for fx, fy in ((0, 0), (1, 0), (0, 1), (1, 1)):
            for fc in (0, 1):
                if (fx, fy, fc) == (0, 0, 0):
                    continue
                slot = 2 * (fx + 2 * fy) + fc - 1
                px, py, pc = (1 - x if fx else x), (1 - y if fy else y), (1 - c if fc else c)
                cps.append(pltpu.make_async_remote_copy(
                    src_ref=p_refs[a].at[2 * px + py, pl.ds(pl.multiple_of(pc * h, 16), h)], dst_ref=l_refs[a].at[slot],
                    send_sem=send_sems.at[N_PEERS * a + slot], recv_sem=recv_sems.at[N_PEERS * a + slot],
                    device_id=(px, py, pc), device_id_type=MESH))
    return cps


def scatter_start(parts, *, name):
    n = len(parts)
    lands = [lax.empty((N_PEERS, p.shape[1] // 2, p.shape[2]), p.dtype) for p in parts]

    def body(*refs):
        for cp in _scatter_copies(refs[:n], refs[n:2 * n], refs[2 * n], refs[2 * n + 1]):
            cp.start()

    outs = pl.pallas_call(
        body, in_specs=[HBM_SPEC] * (2 * n), out_specs=[SEM_SPEC, SEM_SPEC] + [HBM_SPEC] * (2 * n),
        out_shape=[pltpu.SemaphoreType.DMA((N_PEERS * n,)), pltpu.SemaphoreType.DMA((N_PEERS * n,))]
        + [pltpu.HBM(a.shape, a.dtype) for a in parts + lands],
        input_output_aliases={a: 2 + a for a in range(2 * n)},
        compiler_params=pltpu.CompilerParams(has_side_effects=pltpu.SideEffectType.DATAFLOW_SIDE_EFFECTING),
        name=name)(*[_hbm(a) for a in parts + lands])
    return outs[0], outs[1], list(outs[2:2 + n]), list(outs[2 + n:])


def scatter_wait(rounds, after, *, name):
    sizes = [len(r[2]) for r in rounds]
    flat = [a for r in rounds for a in r[2] + r[3]]
    sems = [s for r in rounds for s in (r[0], r[1])]
    nflat = len(flat)

    def body(*refs):
        pos = 0
        for ri, n in enumerate(sizes):
            for cp in _scatter_copies(refs[pos:pos + n], refs[pos + n:pos + 2 * n], refs[nflat + 2 * ri], refs[nflat + 2 * ri + 1]):
                cp.wait_send()
                cp.wait_recv()
            pos += 2 * n

    outs = pl.pallas_call(
        body, in_specs=[HBM_SPEC] * nflat + [SEM_SPEC] * len(sems) + [ANY], out_specs=[HBM_SPEC] * nflat,
        out_shape=[pltpu.HBM(a.shape, a.dtype) for a in flat],
        input_output_aliases={a: a for a in range(nflat)},
        compiler_params=pltpu.CompilerParams(has_side_effects=pltpu.SideEffectType.DATAFLOW_SIDE_EFFECTING),
        name=name)(*flat, *sems, after)
    res, pos = [], 0
    for n in sizes:
        res.append((list(outs[pos:pos + n]), list(outs[pos + n:pos + 2 * n])))
        pos += 2 * n
    return res


def join_halves(fulls, *, name):
    n = len(fulls)

    def body(*refs):
        o_refs = refs[n:2 * n]
        send_sems, recv_sems = refs[2 * n:]
        x, y, c, _ = _place()
        cps = []
        for a in range(n):
            h = fulls[a].shape[0] // 2
            win = o_refs[a].at[pl.ds(pl.multiple_of(c * h, 8), h)]
            cp = pltpu.make_async_remote_copy(
                src_ref=win, dst_ref=win, send_sem=send_sems.at[a], recv_sem=recv_sems.at[a],
                device_id=(x, y, 1 - c), device_id_type=MESH)
            cp.start()
            cps.append(cp)
        for a in range(n):
            h = fulls[a].shape[0] // 2
            other = o_refs[a].at[pl.ds(pl.multiple_of((1 - c) * h, 8), h)]
            pltpu.make_async_remote_copy(
                src_ref=other, dst_ref=other, send_sem=send_sems.at[a], recv_sem=recv_sems.at[a],
                device_id=(x, y, 1 - c), device_id_type=MESH).wait_recv()
        for cp in cps:
            cp.wait_send()

    return pl.pallas_call(
        body, in_specs=[ANY] * n, out_specs=[ANY] * n,
        out_shape=[jax.ShapeDtypeStruct(f.shape, f.dtype) for f in fulls],
        scratch_shapes=[pltpu.SemaphoreType.DMA((n,))] * 2,
        input_output_aliases={a: a for a in range(n)},
        name=name)(*fulls)


def add_partials(part, land, kc, *, name):
    _, R, C = part.shape
    H = R // 2
    tr = _pick(H, [256, 128, 64, 32, 16])
    per = H // tr

    def body(kc_ref, p_ref, l_ref, o_ref):
        acc = p_ref[...].astype(F32)
        for s in range(N_PEERS):
            acc = acc + l_ref[s].astype(F32)
        o_ref[...] = acc

    return pl.pallas_call(
        body,
        grid_spec=pltpu.PrefetchScalarGridSpec(
            num_scalar_prefetch=1, grid=(per,),
            in_specs=[pl.BlockSpec((None, tr, C), lambda i, kc_ref: (kc_ref[0], kc_ref[1] * per + i, 0)),
                      pl.BlockSpec((N_PEERS, tr, C), lambda i, kc_ref: (0, i, 0))],
            out_specs=pl.BlockSpec((tr, C), lambda i, kc_ref: (kc_ref[1] * per + i, 0))),
        out_shape=jax.ShapeDtypeStruct((R, C), F32),
        compiler_params=_cparams(("arbitrary",)), name=name)(kc, part, land)


def allgather_sum(v, *, name):
    m_per, n = v.shape

    def body(x_ref, out_ref, sum_ref, send_sems, recv_sems, local_sem):
        x, y, c, chips = _place()
        me, sibling = (x, y, c), (x, y, 1 - c)

        def rows(px, py, pc):
            return out_ref.at[pl.ds(pl.multiple_of((4 * px + 2 * py + pc) * m_per, 8), m_per), :]

        def copy(k, block, to, src=None):
            return pltpu.make_async_remote_copy(
                src_ref=rows(*block) if src is None else src, dst_ref=rows(*block),
                send_sem=send_sems.at[k], recv_sem=recv_sems.at[k], device_id=to, device_id_type=MESH)

        mine = pltpu.make_async_copy(x_ref, rows(*me), local_sem)
        mine.start()
        first = [copy(0, me, sibling, src=x_ref)]
        first += [copy(1 + j, me, (*chip, c), src=x_ref) for j, chip in enumerate(chips)]
        for cp in first:
            cp.start()
        passed = [copy(4 + j, (*chip, c), sibling) for j, chip in enumerate(chips)]
        for j, chip in enumerate(chips):
            copy(1 + j, (*chip, c), me).wait_recv()
            passed[j].start()
        copy(0, sibling, me).wait_recv()
        for j, chip in enumerate(chips):
            copy(4 + j, (*chip, 1 - c), me).wait_recv()
        for cp in first + passed:
            cp.wait_send()
        mine.wait()
        acc = out_ref[0:m_per, :]
        for d in range(1, N_DEV):
            acc = acc + out_ref[d * m_per:(d + 1) * m_per, :]
        sum_ref[...] = acc

    vm = pl.BlockSpec(memory_space=pltpu.VMEM)
    return pl.pallas_call(
        body, in_specs=[vm], out_specs=[vm, vm],
        out_shape=[jax.ShapeDtypeStruct((N_DEV * m_per, n), v.dtype), jax.ShapeDtypeStruct((m_per, n), v.dtype)],
        scratch_shapes=[pltpu.SemaphoreType.DMA((7,)), pltpu.SemaphoreType.DMA((7,)), pltpu.SemaphoreType.DMA],
        compiler_params=pltpu.CompilerParams(vmem_limit_bytes=VMEM_LIMIT_BYTES), name=name)(v)


def _pack(arrs):
    cols = []
    for a in arrs:
        f = a.reshape(-1)
        pad = (-f.shape[0]) % 128
        cols.append(jnp.pad(f, (0, pad)).reshape(-1, 128))
    out = jnp.concatenate(cols, axis=0)
    pad = (-out.shape[0]) % 8
    return jnp.pad(out, ((0, pad), (0, 0)))


def _unpack(buf, shapes):
    outs, r = [], 0
    for s in shapes:
        nel = math.prod(s)
        nr = -(-nel // 128)
        outs.append(buf[r:r + nr].reshape(-1)[:nel].reshape(s))
        r += nr
    return outs


GA_CONV_OUT, GA_MIX_OUT, GA_WQ, GA_WO, GA_DOWN, GA_UP, GA_ROWS = 0, 256, 512, 768, 1024, 2048, 3072
G1_DOWN, G1_UP, G1_ROWS = 0, 1024, 2048
G2_CONV_OUT, G2_MIX_OUT, G2_WQ, G2_WO, G2_ROWS = 0, 256, 512, 768, 1024


def kernel(x, mem, in_norm_g, in_norm_b, w_in, conv_dw, conv_db, conv_norm_g, conv_norm_b, w_conv_out, ssm_log_step, ssm_lambda_re, ssm_lambda_im, ssm_b_re, ssm_b_im, ssm_c_re, ssm_c_im, ssm_d, w_ssm_glu, w_mix_out, ln1_g, ln1_b, xa_wq, xa_wkv, xa_wo, ln2_g, ln2_b, mlp_w_up, mlp_w_down, ln3_g, ln3_b, loss_target, m_in_norm_g, m_in_norm_b, m_w_in, m_conv_dw, m_conv_db, m_conv_norm_g, m_conv_norm_b, m_w_conv_out, m_ssm_log_step, m_ssm_lambda_re, m_ssm_lambda_im, m_ssm_b_re, m_ssm_b_im, m_ssm_c_re, m_ssm_c_im, m_ssm_d, m_w_ssm_glu, m_w_mix_out, m_ln1_g, m_ln1_b, m_xa_wq, m_xa_wkv, m_xa_wo, m_ln2_g, m_ln2_b, m_mlp_w_up, m_mlp_w_down, m_ln3_g, m_ln3_b, v_in_norm_g, v_in_norm_b, v_w_in, v_conv_dw, v_conv_db, v_conv_norm_g, v_conv_norm_b, v_w_conv_out, v_ssm_log_step, v_ssm_lambda_re, v_ssm_lambda_im, v_ssm_b_re, v_ssm_b_im, v_ssm_c_re, v_ssm_c_im, v_ssm_d, v_w_ssm_glu, v_w_mix_out, v_ln1_g, v_ln1_b, v_xa_wq, v_xa_wkv, v_xa_wo, v_ln2_g, v_ln2_b, v_mlp_w_up, v_mlp_w_down, v_ln3_g, v_ln3_b):
    D = D_MODEL
    xs = x[0]
    T = xs.shape[0]
    mems = mem[0]
    NM = mems.shape[0]
    tgt = loss_target[0]
    my_c = lax.axis_index("c")
    k_me = 2 * lax.axis_index("x") + lax.axis_index("y")
    c_arr = jnp.reshape(my_c, (1,)).astype(jnp.int32)
    k_arr = jnp.reshape(k_me, (1,)).astype(jnp.int32)

    sh_a = jnp.concatenate([w_conv_out[0], w_mix_out[0], xa_wq[0], xa_wo[0], mlp_w_down[0], mlp_w_up[0]], axis=0).astype(BF16)
    def own_block(shard):
        buf = jnp.zeros((N_CHIPS,) + shard.shape, shard.dtype)
        return lax.dynamic_update_slice(buf, shard[None], (k_me, 0, 0))

    dw_pad = jnp.pad(conv_dw[0], ((0, CONV_HALO - CONV_K), (0, 0)))
    ag_bufs = [own_block(s) for s in (sh_a, w_in[0].astype(BF16), xa_wkv[0].astype(BF16), w_ssm_glu[0].astype(BF16), dw_pad)]
    ag_pieces = [(1, 0, D), (4, 0, CONV_HALO), (0, GA_CONV_OUT, 256), (3, 0, D_SSM), (0, GA_MIX_OUT, 256), (0, GA_WQ, 256),
                 (2, 0, D), (0, GA_WO, 256), (0, GA_UP, D), (0, GA_DOWN, D)]
    ag_send, ag_recv, GA, GIN, GKV, GGLU, GDW = gather_start(ag_bufs, ag_pieces, name="gather_start")

    def w_rowshard(row0):
        return dict(b_spec=((N_CHIPS, 256, D), lambda i, j, k: (0, row0 // 256, 0)), b_view=(D, D), tn=D, tk=D)

    _, h0, h0b = ln_fwd(xs, in_norm_g, in_norm_b, name="ln0_fwd")
    (GIN,) = gather_wait(ag_send, ag_recv, [GIN], [(0, 0, D, 0)], h0b, name="gather_wait_in")
    p = mm_nn(h0b, GIN, ((None, D, 1152), lambda i, j, k: (j, 0, 0)), D_IN, tn=1152, tk=D, name="mm_w_in")[0]
    GA, GGLU, GDW = gather_wait(
        ag_send, ag_recv, [GA, GGLU, GDW],
        [(2, 0, CONV_HALO, 1), (0, GA_CONV_OUT, 256, 2), (1, 0, D_SSM, 3), (0, GA_MIX_OUT, 256, 4)], p, name="gather_wait_mixer")
    dw_taps = jnp.transpose(GDW, (1, 0, 2)).reshape(CONV_HALO, D)
    c_pre, actb = conv_fwd(p, dw_taps, conv_db, conv_norm_g[0].reshape(1, D), conv_norm_b[0].reshape(1, D), name="conv_fwd")
    ya = mm_nn(actb, GA, N=D, name="mm_conv_out", **w_rowshard(GA_CONV_OUT))[0]

    lstep, lre, lim = ssm_log_step[0], ssm_lambda_re[0], ssm_lambda_im[0]
    bre, bim, cre, cim = ssm_b_re[0], ssm_b_im[0], ssm_c_re[0], ssm_c_im[0]
    (ar, ai, bbr, bbi), disc_vjp = jax.vjp(_ssm_discretise, lstep, lre, lim, bre, bim)
    Br, Bi = _blockdiag_in(bbr), _blockdiag_in(bbi)
    Cr, Ci = _blockdiag_out(cre), _blockdiag_out(cim)
    pw_r, pw_i = _powers(ar.reshape(-1), ai.reshape(-1))
    dvec = ssm_d[0].reshape(1, D_SSM)
    xr, xi, yssm = ssm_fwd(p, Br, Bi, Cr, Ci, pw_r, pw_i, dvec, name="ssm_fwd")
    z = mm_nn(yssm, GGLU, ((None, D_SSM, 512), lambda i, j, k: (j, 0, 0)), 2 * D, tn=512, tk=D_SSM, name="mm_ssm_glu")[0]
    mergedb = merge_fwd(p, ya, z, name="merge_fwd")
    mix = mm_nn(mergedb, GA, N=D, name="mm_mix_out", **w_rowshard(GA_MIX_OUT))[0]
    r1, h1, h1b = ln_fwd(mix, ln1_g[0], ln1_b[0], res=h0, name="ln1_fwd")
    GA, GKV = gather_wait(ag_send, ag_recv, [GA, GKV], [(0, GA_WQ, 256, 5), (1, 0, D, 6), (0, GA_WO, 256, 7)], mix,
                          name="gather_wait_attn")

    qb = mm_nn(h1b, GA, N=D, out_dtype=BF16, name="mm_wq", **w_rowshard(GA_WQ))[0]
    kv = mm_nn(mems, GKV, ((None, D, 512), lambda i, j, k: (j, 0, 0)), 2 * D, tn=512, tk=D, name="mm_wkv")[0]
    ob = attn_fwd(qb, kv, name="attn_fwd")
    xa = mm_nn(ob, GA, N=D, name="mm_wo", **w_rowshard(GA_WO))[0]
    r2, h2, h2b = ln_fwd(xa, ln2_g[0], ln2_b[0], res=h1, name="ln2_fwd")
    (GA,) = gather_wait(ag_send, ag_recv, [GA], [(0, GA_UP, D, 8), (0, GA_DOWN, D, 9)], xa, name="gather_wait_mlp")

    def relu2(acc):
        zr = jnp.maximum(acc, 0.0)
        return acc, zr * zr

    zpre, zzb = mm_nn(h2b, GA, ((None, D, D), lambda i, j, k: (j, GA_UP // D, 0)), D_FF, tn=D, tk=D,
                      out_dtypes=[F32, BF16], epilogue=relu2, name="mm_up")
    ff = mm_nn(zzb, GA, ((N_CHIPS, D, D), lambda i, j, k: (0, GA_DOWN // D, 0)), D, tm=_pick(T, [512, 256, 128]), tn=D, tk=D_FF,
               b_view=(D_FF, D), name="mm_down")[0]
    r3, h3, _ = ln_fwd(ff, ln3_g[0], ln3_b[0], res=h2, name="ln3_fwd")
    dh3, sq = loss_head(h3, tgt, name="loss_head")
    loss = lax.psum(0.5 * sq[0, 0] / D, ("x", "y", "c"))

    def rs_begin(grads, rnd):
        return scatter_start(grads, name=f"rs{rnd}_scatter_start")

    g1_shape = jax.ShapeDtypeStruct((N_CHIPS, G1_ROWS, D), BF16)
    g2_shape = jax.ShapeDtypeStruct((N_CHIPS, G2_ROWS, D), BF16)
    dr3, dr3b, dg3, db3 = ln_bwd(r3, dh3, ln3_g[0], name="ln3_bwd")
    dzpreb = mm_nt(dr3b, GA, ((None, D, D), lambda i, j, k: (j, GA_DOWN // D, 0)), D_FF, tn=D, tk=D, out_dtype=BF16,
                   extras=(zpre,), epilogue=lambda acc, zp: (acc * (2.0 * jnp.maximum(zp, 0.0)),), name="mm_down_t")[0]
    G1g = mm_tn(zzb, dr3b, tm=D, tn=D, out_spec=((None, D, D), lambda i, j, k: (i, G1_DOWN // D, 0)), out_shape=g1_shape,
                name="mm_down_g")
    G1g = mm_tn(h2b, dzpreb, tm=D, tn=D, out_spec=((None, D, D), lambda i, j, k: (j, G1_UP // D, 0)), out_shape=g1_shape,
                out_buf=G1g, name="mm_up_g")
    round1 = rs_begin([G1g], 1)
    dh2 = mm_nt(dzpreb, GA, ((None, D, D), lambda i, j, k: (k, GA_UP // D, 0)), D, tn=D, tk=D,
                extras=(dr3,), epilogue=lambda acc, d: (acc + ALPHA * d,), after=(round1[2][0],), name="mm_up_t")[0]
    dr2, dr2b, dg2, db2 = ln_bwd(r2, dh2, ln2_g[0], name="ln2_bwd")

    def g_rowshard(row0, out_buf):
        return dict(tm=D, tn=D, out_spec=((N_CHIPS, 256, D), lambda i, j, k: (0, row0 // 256, 0)), out_shape=g2_shape,
                    out_buf=out_buf)

    dob = mm_nt(dr2b, GA, N=D, out_dtype=BF16, name="mm_wo_t", **w_rowshard(GA_WO))[0]
    G2g = mm_tn(ob, dr2b, name="mm_wo_g", **g_rowshard(G2_WO, None))
    dqb, dkv = attn_bwd(qb, kv, dob, name="attn_bwd")
    G2g = mm_tn(h1b, dqb, name="mm_wq_g", **g_rowshard(G2_WQ, G2g))
    GKVg = mm_tn(mems, dkv, tm=D, tn=512, tk=NM, out_spec=((None, D, 512), lambda i, j, k: (j, 0, 0)),
                 out_shape=jax.ShapeDtypeStruct((N_CHIPS, D, 512), BF16), name="mm_wkv_g")
    dh1 = mm_nt(dqb, GA, N=D, extras=(dr2,), epilogue=lambda acc, d: (acc + ALPHA * d,), name="mm_wq_t",
                **w_rowshard(GA_WQ))[0]
    dr1, dr1b, dg1, db1 = ln_bwd(r1, dh1, ln1_g[0], name="ln1_bwd")

    dmerged = mm_nt(dr1b, GA, N=D, name="mm_mix_t", **w_rowshard(GA_MIX_OUT))[0]
    G2g = mm_tn(mergedb, dr1b, name="mm_mix_g", **g_rowshard(G2_MIX_OUT, G2g))
    dyab, dgab, dgbb, dz1b, dz2b = merge_bwd(dmerged, p, ya, z, name="merge_bwd")
    dzb = jnp.concatenate([dz1b, dz2b], axis=1)
    GGLUg = mm_tn(yssm, dzb, tm=D_SSM, tn=512, out_spec=((None, D_SSM, 512), lambda i, j, k: (j, 0, 0)),
                  out_shape=jax.ShapeDtypeStruct((N_CHIPS, D_SSM, 512), BF16), name="mm_glu_g")
    dyssm = mm_nt(dzb, GGLU, ((None, D_SSM, 512), lambda i, j, k: (k, 0, 0)), D_SSM, tn=D_SSM, tk=512, name="mm_glu_t")[0]
    pwc_r, pwc_i = pw_r, -pw_i
    dub, dBr, dBi, dCr, dCi, dar8, dai8, dd8 = ssm_bwd(dyssm, p, xr, xi, Br, Bi, Cr, Ci, pwc_r, pwc_i, dvec, name="ssm_bwd")
    dar = jnp.sum(dar8, axis=0).reshape(SSM_GROUPS, SSM_STATE)
    dai = jnp.sum(dai8, axis=0).reshape(SSM_GROUPS, SSM_STATE)
    g_lstep, g_lre, g_lim, g_bre, g_bim = disc_vjp((dar, dai, _diag_in(dBr), _diag_in(dBi)))
    g_cre, g_cim = _diag_out(dCr), _diag_out(dCi)
    g_d = jnp.sum(dd8, axis=0).reshape(1, D_SSM)

    dact = mm_nt(dyab, GA, N=D, name="mm_conv_out_t", **w_rowshard(GA_CONV_OUT))[0]
    G2g = mm_tn(actb, dyab, name="mm_conv_out_g", **g_rowshard(G2_CONV_OUT, G2g))
    round2 = rs_begin([G2g, GKVg, GGLUg], 2)
    dc, dng, dnb, ddb = conv_bwd_norm(dact, c_pre, conv_norm_g[0].reshape(1, D), conv_norm_b[0].reshape(1, D),
                                      round2[2][0], name="conv_bwd_norm")
    dvgb, ddw = conv_bwd_taps(dc, p, dw_taps, name="conv_bwd_taps")
    dpb = jnp.concatenate([dvgb, dub, dgab, dgbb], axis=1)
    GINg = mm_tn(h0b, dpb, tm=D, tn=1152, out_spec=((None, D, 1152), lambda i, j, k: (j, 0, 0)),
                 out_shape=jax.ShapeDtypeStruct((N_CHIPS, D, 1152), BF16), name="mm_w_in_g")
    round3 = rs_begin([GINg], 3)
    dh0 = mm_nt(dpb, GIN, ((None, 512, 1152), lambda i, j, k: (k, j, 0)), D, tn=512, tk=1152,
                extras=(dr1,), epilogue=lambda acc, d: (acc + ALPHA * d,), after=(round3[2][0],), name="mm_w_in_t")[0]
    gx, _, dg0, db0 = ln_bwd(xs, dh0, in_norm_g, name="ln0_bwd")

    kc_arr = jnp.concatenate([k_arr, c_arr])
    landed = scatter_wait([round1, round2, round3], gx, name="rs_scatter_wait")
    tags = ["mlp", "sq", "kv", "glu", "in"]
    pairs = [(pt, l2) for parts, lands2 in landed for pt, l2 in zip(parts, lands2)]
    halves = [add_partials(pt, l2, kc_arr, name="rs_add_partials_" + t) for (pt, l2), t in zip(pairs, tags)]
    g1, g2, gKV, gGLU, gIN = join_halves(halves, name="rs_join_halves")

    small_names = ["in_norm_g", "in_norm_b", "conv_db", "conv_norm_g", "conv_norm_b", "ssm_log_step", "ssm_lambda_re",
                   "ssm_lambda_im", "ssm_b_re", "ssm_b_im", "ssm_c_re", "ssm_c_im", "ssm_d", "ln1_g", "ln1_b",
                   "ln2_g", "ln2_b", "ln3_g", "ln3_b"]
    small_w = [in_norm_g, in_norm_b, conv_db, conv_norm_g, conv_norm_b, ssm_log_step, ssm_lambda_re, ssm_lambda_im,
               ssm_b_re, ssm_b_im, ssm_c_re, ssm_c_im, ssm_d, ln1_g, ln1_b, ln2_g, ln2_b, ln3_g, ln3_b]
    small_m = [m_in_norm_g, m_in_norm_b, m_conv_db, m_conv_norm_g, m_conv_norm_b, m_ssm_log_step, m_ssm_lambda_re,
               m_ssm_lambda_im, m_ssm_b_re, m_ssm_b_im, m_ssm_c_re, m_ssm_c_im, m_ssm_d, m_ln1_g, m_ln1_b, m_ln2_g,
               m_ln2_b, m_ln3_g, m_ln3_b]
    small_v = [v_in_norm_g, v_in_norm_b, v_conv_db, v_conv_norm_g, v_conv_norm_b, v_ssm_log_step, v_ssm_lambda_re,
               v_ssm_lambda_im, v_ssm_b_re, v_ssm_b_im, v_ssm_c_re, v_ssm_c_im, v_ssm_d, v_ln1_g, v_ln1_b, v_ln2_g,
               v_ln2_b, v_ln3_g, v_ln3_b]
    small_g = [dg0, db0, ddb, dng, dnb, g_lstep, g_lre, g_lim, g_bre, g_bim, g_cre, g_cim, g_d, dg1, db1, dg2, db2, dg3, db3]
    small_shapes = [w.shape for w in small_w]
    n_small_rows = _pack(small_w).shape[0]
    packed_g = _pack(small_g + [ddw])
    _, summed = allgather_sum(packed_g, name="allreduce_small")
    small_rows = sum(-(-math.prod(s) // 128) for s in small_shapes)
    ddw_full = summed[small_rows:small_rows + CONV_HALO * D // 128].reshape(CONV_HALO, D)
    g_dw = lax.dynamic_slice_in_dim(ddw_full, k_me * (D // N_CHIPS), D // N_CHIPS, axis=1)
    gs_packed = jnp.pad(summed[:small_rows], ((0, n_small_rows - small_rows), (0, 0)))

    res = {}

    def upd(nm, w, m, v, g_arr, row0=0):
        shp = w.shape
        w2, m2, v2 = (a.reshape(-1, shp[-1]) for a in (w, m, v))
        outs = adamw(w2, m2, v2, g_arr, row0, name="adamw_" + nm)
        res[nm] = tuple(o.reshape(shp) for o in outs)

    upd("w_conv_out", w_conv_out, m_w_conv_out, v_w_conv_out, g2, G2_CONV_OUT)
    upd("w_mix_out", w_mix_out, m_w_mix_out, v_w_mix_out, g2, G2_MIX_OUT)
    upd("xa_wq", xa_wq, m_xa_wq, v_xa_wq, g2, G2_WQ)
    upd("xa_wo", xa_wo, m_xa_wo, v_xa_wo, g2, G2_WO)
    upd("mlp_w_down", mlp_w_down, m_mlp_w_down, v_mlp_w_down, g1, G1_DOWN)
    upd("mlp_w_up", mlp_w_up, m_mlp_w_up, v_mlp_w_up, g1, G1_UP)
    upd("w_in", w_in, m_w_in, v_w_in, gIN)
    upd("xa_wkv", xa_wkv, m_xa_wkv, v_xa_wkv, gKV)
    upd("w_ssm_glu", w_ssm_glu, m_w_ssm_glu, v_w_ssm_glu, gGLU)
    pad_dw = lambda a: jnp.pad(a[0], ((0, CONV_HALO - CONV_K), (0, 0)))
    dw_outs = adamw(pad_dw(conv_dw), pad_dw(m_conv_dw), pad_dw(v_conv_dw), g_dw, 0, name="adamw_conv_dw")
    res["conv_dw"] = tuple(o[:CONV_K][None] for o in dw_outs)
    sm_outs = adamw(_pack(small_w), _pack(small_m), _pack(small_v), gs_packed, 0, name="adamw_small")
    sm_un = [_unpack(o, small_shapes) for o in sm_outs]
    for idx, nm in enumerate(small_names):
        res[nm] = tuple(sm_un[q][idx] for q in range(4))

    order = ["in_norm_g", "in_norm_b", "w_in", "conv_dw", "conv_db", "conv_norm_g", "conv_norm_b", "w_conv_out",
             "ssm_log_step", "ssm_lambda_re", "ssm_lambda_im", "ssm_b_re", "ssm_b_im", "ssm_c_re", "ssm_c_im", "ssm_d",
             "w_ssm_glu", "w_mix_out", "ln1_g", "ln1_b", "xa_wq", "xa_wkv", "xa_wo", "ln2_g", "ln2_b", "mlp_w_up",
             "mlp_w_down", "ln3_g", "ln3_b"]
    return (loss, gx[None], *[res[n][0] for n in order], *[res[n][1] for n in order],
            *[res[n][2] for n in order], *[res[n][3] for n in order])
```

```python
import functools
import math

import jax
import jax.numpy as jnp
from jax import lax
from jax.experimental import pallas as pl
from jax.experimental.pallas import tpu as pltpu

F32 = jnp.float32
BF16 = jnp.bfloat16
MESH = pl.DeviceIdType.MESH

D_MODEL = 1024
N_HEADS = 4
HEAD_DIM = D_MODEL // N_HEADS
CONV_K = 31
CONV_HALO = 32
D_SSM = 512
SSM_GROUPS = 32
SSM_GROUP = 16
SSM_STATE = 64
SSM_BLOCKS = 4
SSM_BLOCK_IN = D_SSM // SSM_BLOCKS
SSM_BLOCK_STATE = SSM_GROUPS * SSM_STATE // SSM_BLOCKS
D_FF = 4096
D_IN = 4608
LN_EPS = 1e-5
ALPHA = (2.0 * 1) ** 0.25
N_CHIPS = 4
N_DEV = 8
ADAM_LR, ADAM_B1, ADAM_B2, ADAM_EPS, ADAM_WD, ADAM_STEP = 0.001, 0.9, 0.999, 1e-08, 0.01, 10
VMEM_LIMIT_BYTES = 56 * 1024 * 1024


def _pick(dim, cands):
    for c in cands:
        if dim % c == 0:
            return c
    return dim


def _cparams(sem=None):
    return pltpu.CompilerParams(dimension_semantics=sem, vmem_limit_bytes=VMEM_LIMIT_BYTES)


def _sigmoid(x):
    return 1.0 / (1.0 + jnp.exp(-x))


_DIMS = {"nn": (((1,), (0,)), ((), ())), "nt": (((1,), (1,)), ((), ())), "tn": (((0,), (0,)), ((), ()))}


def matmul(a, b, *, mode, M, N, K, tm, tn, tk, a_spec, b_spec, out_specs, out_shapes, name,
           extras=(), extra_specs=(), epilogue=None, alias_buf=None, b_view=None, after=()):
    nk = K // tk
    ne = len(extras)
    no = len(out_shapes)
    na = (0 if alias_buf is None else 1) + len(after)
    dims = _DIMS[mode]

    def body(*refs):
        a_ref, b_ref = refs[0], refs[1]
        e_refs = refs[2:2 + ne]
        o_refs = refs[2 + ne + na:2 + ne + na + no]

        def finish(acc):
            outs = (acc,) if epilogue is None else epilogue(acc, *[r[...] for r in e_refs])
            for o, r in zip(outs, o_refs):
                r[...] = o.astype(r.dtype).reshape(r.shape)

        b_blk = b_ref[...] if b_view is None else b_ref[...].reshape(b_view)
        prod = lax.dot_general(a_ref[...].astype(BF16), b_blk.astype(BF16), dims, preferred_element_type=F32)
        if nk == 1:
            finish(prod)
        else:
            acc_ref = refs[-1]
            k = pl.program_id(2)

            @pl.when(k == 0)
            def _():
                acc_ref[...] = prod

            @pl.when(k > 0)
            def _():
                acc_ref[...] += prod

            @pl.when(k == nk - 1)
            def _():
                finish(acc_ref[...])

    in_specs = [pl.BlockSpec(*a_spec), pl.BlockSpec(*b_spec)] + [pl.BlockSpec(*s) for s in extra_specs]
    ins = [a, b, *extras]
    if alias_buf is not None:
        in_specs.append(pl.BlockSpec(memory_space=pl.ANY))
        ins.append(alias_buf)
    for dep in after:
        in_specs.append(pl.BlockSpec(memory_space=pl.ANY))
        ins.append(dep)
    res = pl.pallas_call(
        body,
        grid=(M // tm, N // tn, nk),
        in_specs=in_specs,
        out_specs=[pl.BlockSpec(*s) for s in out_specs],
        out_shape=out_shapes,
        scratch_shapes=[] if nk == 1 else [pltpu.VMEM((tm, tn), F32)],
        input_output_aliases={2 + ne: 0} if alias_buf is not None else {},
        compiler_params=_cparams(("parallel", "parallel", "arbitrary")),
        name=name,
    )(*ins)
    return res


def _mn(tm, tn):
    return ((tm, tn), lambda i, j, k: (i, j))


def mm_nn(a, b_arr, b_spec, N, *, name, tm=None, tn, tk, out_dtype=F32, extras=(), epilogue=None, out_dtypes=None,
          b_view=None, extra_specs=None):
    M, K = a.shape
    tm = tm or _pick(M, [1024, 512, 256, 128])
    dts = out_dtypes or [out_dtype]
    return matmul(a, b_arr, mode="nn", M=M, N=N, K=K, tm=tm, tn=tn, tk=tk,
                  a_spec=((tm, tk), lambda i, j, k: (i, k)), b_spec=b_spec, b_view=b_view,
                  out_specs=[_mn(tm, tn)] * len(dts), out_shapes=[jax.ShapeDtypeStruct((M, N), d) for d in dts],
                  extras=extras, extra_specs=extra_specs or [_mn(tm, tn)] * len(extras), epilogue=epilogue, name=name)


def mm_nt(a, b_arr, b_spec, N, *, name, tm=None, tn, tk, out_dtype=F32, extras=(), epilogue=None, out_dtypes=None,
          b_view=None, after=()):
    M, K = a.shape
    tm = tm or _pick(M, [1024, 512, 256, 128])
    dts = out_dtypes or [out_dtype]
    return matmul(a, b_arr, mode="nt", M=M, N=N, K=K, tm=tm, tn=tn, tk=tk, after=after,
                  a_spec=((tm, tk), lambda i, j, k: (i, k)), b_spec=b_spec, b_view=b_view,
                  out_specs=[_mn(tm, tn)] * len(dts), out_shapes=[jax.ShapeDtypeStruct((M, N), d) for d in dts],
                  extras=extras, extra_specs=[_mn(tm, tn)] * len(extras), epilogue=epilogue, name=name)


def mm_tn(a, b, *, name, tm, tn, tk=None, out_spec, out_shape, out_buf=None):
    K, M = a.shape
    N = b.shape[1]
    tk = tk or _pick(K, [2048, 1024, 512, 256, 128])
    return matmul(a, b, mode="tn", M=M, N=N, K=K, tm=tm, tn=tn, tk=tk,
                  a_spec=((tk, tm), lambda i, j, k: (k, i)), b_spec=((tk, tn), lambda i, j, k: (k, j)),
                  out_specs=[out_spec], out_shapes=[out_shape], alias_buf=out_buf, name=name)[0]


def _rows(tc, w, cb=0):
    return pl.BlockSpec((tc, w), lambda i: (i, cb))


def _const(shape):
    return pl.BlockSpec(shape, lambda i: tuple([0] * len(shape)))


def _ln_stats(r):
    mu = jnp.mean(r, axis=-1, keepdims=True)
    xc = r - mu
    var = jnp.mean(xc * xc, axis=-1, keepdims=True)
    rstd = lax.rsqrt(var + LN_EPS)
    return xc * rstd, rstd


def _rowsum8(v):
    tc, w = v.shape
    return jnp.sum(v.reshape(tc // 8, 8, w), axis=0)


def ln_fwd(x, g, b, *, name, res=None):
    T, D = x.shape
    tc = _pick(T, [512, 256, 128])
    has_res = res is not None

    def body(*refs):
        if has_res:
            x_ref, res_ref, g_ref, b_ref, r_ref, h_ref, hb_ref = refs
            r = ALPHA * res_ref[...] + x_ref[...]
            r_ref[...] = r
        else:
            x_ref, g_ref, b_ref, h_ref, hb_ref = refs
            r = x_ref[...]
        xhat, _ = _ln_stats(r)
        y = xhat * g_ref[...] + b_ref[...]
        h_ref[...] = y
        hb_ref[...] = y.astype(BF16)

    ins = [x] + ([res] if has_res else []) + [g.reshape(1, D), b.reshape(1, D)]
    in_specs = [_rows(tc, D)] * (2 if has_res else 1) + [_const((1, D))] * 2
    n_out = 3 if has_res else 2
    outs = pl.pallas_call(
        body, grid=(T // tc,), in_specs=in_specs, out_specs=[_rows(tc, D)] * n_out,
        out_shape=[jax.ShapeDtypeStruct((T, D), F32)] * (n_out - 1) + [jax.ShapeDtypeStruct((T, D), BF16)],
        compiler_params=_cparams(("arbitrary",)), name=name)(*ins)
    if has_res:
        return outs
    return (x,) + tuple(outs)


def ln_bwd(r, dy, g, *, name):
    T, D = r.shape
    tc = _pick(T, [512, 256, 128])
    nt = T // tc

    def body(r_ref, dy_ref, g_ref, dr_ref, drb_ref, dg_ref, db_ref, accg, accb):
        i = pl.program_id(0)

        @pl.when(i == 0)
        def _():
            accg[...] = jnp.zeros_like(accg)
            accb[...] = jnp.zeros_like(accb)

        xhat, rstd = _ln_stats(r_ref[...])
        dy = dy_ref[...]
        dxh = dy * g_ref[...]
        m1 = jnp.mean(dxh, axis=-1, keepdims=True)
        m2 = jnp.mean(dxh * xhat, axis=-1, keepdims=True)
        dr = rstd * (dxh - m1 - xhat * m2)
        dr_ref[...] = dr
        drb_ref[...] = dr.astype(BF16)
        accg[...] += _rowsum8(dy * xhat)
        accb[...] += _rowsum8(dy)

        @pl.when(i == nt - 1)
        def _():
            dg_ref[...] = jnp.sum(accg[...], axis=0, keepdims=True)
            db_ref[...] = jnp.sum(accb[...], axis=0, keepdims=True)

    return pl.pallas_call(
        body, grid=(nt,), in_specs=[_rows(tc, D), _rows(tc, D), _const((1, D))],
        out_specs=[_rows(tc, D), _rows(tc, D), _const((1, D)), _const((1, D))],
        out_shape=[jax.ShapeDtypeStruct((T, D), F32), jax.ShapeDtypeStruct((T, D), BF16),
                   jax.ShapeDtypeStruct((1, D), F32), jax.ShapeDtypeStruct((1, D), F32)],
        scratch_shapes=[pltpu.VMEM((8, D), F32), pltpu.VMEM((8, D), F32)],
        compiler_params=_cparams(("arbitrary",)), name=name)(r, dy, g.reshape(1, D))


def ln_loss_bwd(x, res, g, b, target, *, name):
    T, D = x.shape
    tc = _pick(T, [512, 256, 128])
    nt = T // tc

    def body(x_ref, res_ref, g_ref, b_ref, t_ref, dr_ref, drb_ref, dg_ref, db_ref, loss_ref, accg, accb, accl):
        i = pl.program_id(0)

        @pl.when(i == 0)
        def _():
            accg[...] = jnp.zeros_like(accg)
            accb[...] = jnp.zeros_like(accb)
            accl[...] = jnp.zeros_like(accl)

        r = ALPHA * res_ref[...] + x_ref[...]
        xhat, rstd = _ln_stats(r)
        e = xhat * g_ref[...] + b_ref[...] - t_ref[...]
        dy = e * (1.0 / D)
        dxh = dy * g_ref[...]
        m1 = jnp.mean(dxh, axis=-1, keepdims=True)
        m2 = jnp.mean(dxh * xhat, axis=-1, keepdims=True)
        dr = rstd * (dxh - m1 - xhat * m2)
        dr_ref[...] = dr
        drb_ref[...] = dr.astype(BF16)
        accg[...] += _rowsum8(dy * xhat)
        accb[...] += _rowsum8(dy)
        accl[...] += _rowsum8(e * e)

        @pl.when(i == nt - 1)
        def _():
            dg_ref[...] = jnp.sum(accg[...], axis=0, keepdims=True)
            db_ref[...] = jnp.sum(accb[...], axis=0, keepdims=True)
            s = jnp.sum(jnp.sum(accl[...], axis=0, keepdims=True), axis=1, keepdims=True)
            loss_ref[...] = jnp.broadcast_to(s, (1, 128))

    return pl.pallas_call(
        body, grid=(nt,), in_specs=[_rows(tc, D), _rows(tc, D), _const((1, D)), _const((1, D)), _rows(tc, D)],
        out_specs=[_rows(tc, D), _rows(tc, D), _const((1, D)), _const((1, D)), _const((1, 128))],
        out_shape=[jax.ShapeDtypeStruct((T, D), F32), jax.ShapeDtypeStruct((T, D), BF16),
                   jax.ShapeDtypeStruct((1, D), F32), jax.ShapeDtypeStruct((1, D), F32), jax.ShapeDtypeStruct((1, 128), F32)],
        scratch_shapes=[pltpu.VMEM((8, D), F32)] * 3,
        compiler_params=_cparams(("arbitrary",)), name=name)(x, res, g.reshape(1, D), b.reshape(1, D), target)


def _halo_prev(tc):
    per = tc // CONV_HALO
    return lambda i: jnp.maximum(i * per - 1, 0)


CONV_ROWS = 32
CONV_TAP_GROUP = 4
CONV_TILE_UNROLL = 4


def _fill_shifts(S, nrows):
    for b in range(1, 8):
        S[b, 0:nrows - 8, :] = S[0, b:b + nrows - 8, :]


def _tap_sum(S, w_ref, offs, r0, nrows):
    acc = None
    for k, o in enumerate(offs):
        a, b = divmod(o, 8)
        term = w_ref[k:k + 1, :] * S[b, pl.ds(pl.multiple_of(r0 + 8 * a, 8), nrows), :]
        acc = term if acc is None else acc + term
    return acc


def conv_fwd(p, dw, db, ng, nb, *, name):
    T = p.shape[0]
    D = D_MODEL
    tc = _pick(T, [256, 128])
    prev = _halo_prev(tc)
    off = CONV_HALO - (CONV_K - 1)
    offs = [off + k for k in range(CONV_K)]

    def body(val_ref, gate_ref, valp_ref, gatep_ref, dw_ref, db_ref, ng_ref, nb_ref, c_ref, act_ref, S):
        i = pl.program_id(0)
        u_prev = valp_ref[...] * _sigmoid(gatep_ref[...])
        S[0, 0:CONV_HALO, :] = jnp.where(i > 0, u_prev, 0.0)
        S[0, CONV_HALO:CONV_HALO + tc, :] = val_ref[...] * _sigmoid(gate_ref[...])
        _fill_shifts(S, CONV_HALO + tc)

        def rows(j, carry):
            r0 = pl.multiple_of(j * CONV_ROWS, CONV_ROWS)
            c_ref[pl.ds(r0, CONV_ROWS), :] = _tap_sum(S, dw_ref, offs, r0, CONV_ROWS) + db_ref[...]
            return carry

        lax.fori_loop(0, tc // CONV_ROWS, rows, 0)
        c = c_ref[...]
        xhat, _ = _ln_stats(c)
        cn = xhat * ng_ref[...] + nb_ref[...]
        act_ref[...] = (cn * _sigmoid(cn)).astype(BF16)

    return pl.pallas_call(
        body, grid=(T // tc,),
        in_specs=[_rows(tc, D, 0), _rows(tc, D, 1),
                  pl.BlockSpec((CONV_HALO, D), lambda i: (prev(i), 0)), pl.BlockSpec((CONV_HALO, D), lambda i: (prev(i), 1)),
                  _const((CONV_HALO, D)), _const((1, D)), _const((1, D)), _const((1, D))],
        out_specs=[_rows(tc, D), _rows(tc, D)],
        out_shape=[jax.ShapeDtypeStruct((T, D), F32), jax.ShapeDtypeStruct((T, D), BF16)],
        scratch_shapes=[pltpu.VMEM((8, CONV_HALO + tc, D), F32)],
        compiler_params=_cparams(("arbitrary",)), name=name)(p, p, p, p, dw, db, ng, nb)


def conv_bwd_norm(dact, c_pre, ng, nb, after, *, name):
    T, D = c_pre.shape
    tc = _pick(T, [512, 256, 128])
    nt = T // tc

    def body(da_ref, c_ref, ng_ref, nb_ref, after_ref, dc_ref, dng_ref, dnb_ref, ddb_ref, accg, accb, accd):
        i = pl.program_id(0)

        @pl.when(i == 0)
        def _():
            accg[...] = jnp.zeros_like(accg)
            accb[...] = jnp.zeros_like(accb)
            accd[...] = jnp.zeros_like(accd)

        xhat, rstd = _ln_stats(c_ref[...])
        cn = xhat * ng_ref[...] + nb_ref[...]
        s = _sigmoid(cn)
        dcn = da_ref[...] * (s * (1.0 + cn * (1.0 - s)))
        dxh = dcn * ng_ref[...]
        m1 = jnp.mean(dxh, axis=-1, keepdims=True)
        m2 = jnp.mean(dxh * xhat, axis=-1, keepdims=True)
        dc = rstd * (dxh - m1 - xhat * m2)
        dc_ref[...] = dc
        accg[...] += _rowsum8(dcn * xhat)
        accb[...] += _rowsum8(dcn)
        accd[...] += _rowsum8(dc)

        @pl.when(i == nt - 1)
        def _():
            dng_ref[...] = jnp.sum(accg[...], axis=0, keepdims=True)
            dnb_ref[...] = jnp.sum(accb[...], axis=0, keepdims=True)
            ddb_ref[...] = jnp.sum(accd[...], axis=0, keepdims=True)

    vec = jax.ShapeDtypeStruct((1, D), F32)
    return pl.pallas_call(
        body, grid=(nt,), in_specs=[_rows(tc, D), _rows(tc, D), _const((1, D)), _const((1, D)), ANY],
        out_specs=[_rows(tc, D), _const((1, D)), _const((1, D)), _const((1, D))],
        out_shape=[jax.ShapeDtypeStruct((T, D), F32), vec, vec, vec],
        scratch_shapes=[pltpu.VMEM((8, D), F32)] * 3,
        compiler_params=_cparams(("arbitrary",)), name=name)(dact, c_pre, ng, nb, after)


def conv_bwd_taps(dc, p, dw, *, name):
    T, D = dc.shape
    tc = _pick(T, [256, 128])
    nt = T // tc
    per = tc // CONV_HALO
    prev = _halo_prev(tc)
    last_halo = T // CONV_HALO - 1
    nxt = lambda i: jnp.minimum((i + 1) * per, last_halo)
    off = CONV_HALO - (CONV_K - 1)

    def body(dc_ref, dcn_ref, val_ref, gate_ref, valp_ref, gatep_ref, dw_ref, dvg_ref, ddw_ref, ext_u, ext_d, acc):
        i = pl.program_id(0)

        @pl.when(i == 0)
        def _():
            acc[...] = jnp.zeros_like(acc)

        u_prev = valp_ref[...] * _sigmoid(gatep_ref[...])
        ext_u[0, 0:CONV_HALO, :] = jnp.where(i > 0, u_prev, 0.0)
        ext_u[0, CONV_HALO:CONV_HALO + tc, :] = val_ref[...] * _sigmoid(gate_ref[...])
        ext_d[0, 0:tc, :] = dc_ref[...]
        ext_d[0, tc:tc + CONV_HALO, :] = jnp.where(i < nt - 1, dcn_ref[...], 0.0)
        _fill_shifts(ext_u, CONV_HALO + tc)
        _fill_shifts(ext_d, CONV_HALO + tc)

        def rows(j, carry):
            r0 = pl.multiple_of(j * CONV_ROWS, CONV_ROWS)
            sl = pl.ds(r0, CONV_ROWS)
            du = _tap_sum(ext_d, dw_ref, [CONV_K - 1 - k for k in range(CONV_K)], r0, CONV_ROWS)
            sg = _sigmoid(gate_ref[sl, :])
            dvg_ref[sl, 0:D] = (du * sg).astype(BF16)
            dvg_ref[sl, D:2 * D] = (du * val_ref[sl, :] * sg * (1.0 - sg)).astype(BF16)
            return carry

        lax.fori_loop(0, tc // CONV_ROWS, rows, 0)

        for k0 in range(0, CONV_K, CONV_TAP_GROUP):
            ks = list(range(k0, min(k0 + CONV_TAP_GROUP, CONV_K)))

            def taps(j, accs, ks=ks):
                out = list(accs)
                for t in range(CONV_TILE_UNROLL):
                    r0 = pl.multiple_of((j * CONV_TILE_UNROLL + t) * 8, 8)
                    dct = dc_ref[pl.ds(r0, 8), :]
                    for q, k in enumerate(ks):
                        a, b = divmod(off + k, 8)
                        out[q] = out[q] + dct * ext_u[b, pl.ds(pl.multiple_of(r0 + 8 * a, 8), 8), :]
                return tuple(out)

            accs = lax.fori_loop(0, tc // (8 * CONV_TILE_UNROLL), taps, tuple(jnp.zeros((8, D), F32) for _ in ks))
            for k, a_k in zip(ks, accs):
                acc[k] += a_k

        @pl.when(i == nt - 1)
        def _():
            ddw_ref[...] = jnp.zeros_like(ddw_ref)
            for k in range(CONV_K):
                ddw_ref[k:k + 1, :] = jnp.sum(acc[k], axis=0, keepdims=True)

    return pl.pallas_call(
        body, grid=(nt,),
        in_specs=[_rows(tc, D), pl.BlockSpec((CONV_HALO, D), lambda i: (nxt(i), 0)),
                  _rows(tc, D, 0), _rows(tc, D, 1),
                  pl.BlockSpec((CONV_HALO, D), lambda i: (prev(i), 0)), pl.BlockSpec((CONV_HALO, D), lambda i: (prev(i), 1)),
                  _const((CONV_HALO, D))],
        out_specs=[_rows(tc, 2 * D), _const((CONV_HALO, D))],
        out_shape=[jax.ShapeDtypeStruct((T, 2 * D), BF16), jax.ShapeDtypeStruct((CONV_HALO, D), F32)],
        scratch_shapes=[pltpu.VMEM((8, CONV_HALO + tc, D), F32), pltpu.VMEM((8, CONV_HALO + tc, D), F32),
                        pltpu.VMEM((CONV_K, 8, D), F32)],
        compiler_params=_cparams(("arbitrary",)), name=name)(dc, dc, p, p, p, p, dw)


GATE_A0 = (2 * D_MODEL + D_SSM) // 512
GATE_B0 = GATE_A0 + 2


def merge_fwd(p, ya, z, *, name):
    T = p.shape[0]
    D = D_MODEL
    tc = _pick(T, [512, 256, 128])
    W = 512

    def body(ga_ref, gb_ref, ya_ref, z1_ref, z2_ref, o_ref):
        yb = z1_ref[...] * _sigmoid(z2_ref[...])
        o_ref[...] = (_sigmoid(ga_ref[...]) * ya_ref[...] + _sigmoid(gb_ref[...]) * yb).astype(BF16)

    return pl.pallas_call(
        body, grid=(T // tc, D // W),
        in_specs=[pl.BlockSpec((tc, W), lambda i, j: (i, GATE_A0 + j)), pl.BlockSpec((tc, W), lambda i, j: (i, GATE_B0 + j)),
                  pl.BlockSpec((tc, W), lambda i, j: (i, j)), pl.BlockSpec((tc, W), lambda i, j: (i, j)),
                  pl.BlockSpec((tc, W), lambda i, j: (i, D // W + j))],
        out_specs=pl.BlockSpec((tc, W), lambda i, j: (i, j)),
        out_shape=jax.ShapeDtypeStruct((T, D), BF16),
        compiler_params=_cparams(("arbitrary", "arbitrary")), name=name)(p, p, ya, z, z)


def merge_bwd(dm, p, ya, z, *, name):
    T = p.shape[0]
    D = D_MODEL
    tc = _pick(T, [512, 256, 128])
    W = 512
    nb = D // W

    def body(dm_ref, ga_ref, gb_ref, ya_ref, z1_ref, z2_ref, dya_ref, dga_ref, dgb_ref, dz1_ref, dz2_ref):
        dm = dm_ref[...]
        sa = _sigmoid(ga_ref[...])
        sb = _sigmoid(gb_ref[...])
        s2 = _sigmoid(z2_ref[...])
        z1 = z1_ref[...]
        yb = z1 * s2
        dya_ref[...] = (dm * sa).astype(BF16)
        dga_ref[...] = (dm * ya_ref[...] * sa * (1.0 - sa)).astype(BF16)
        dgb_ref[...] = (dm * yb * sb * (1.0 - sb)).astype(BF16)
        dyb = dm * sb
        dz1_ref[...] = (dyb * s2).astype(BF16)
        dz2_ref[...] = (dyb * z1 * s2 * (1.0 - s2)).astype(BF16)

    blk = lambda off: pl.BlockSpec((tc, W), lambda i, j: (i, off + j))
    dya, dga, dgb, dz1, dz2 = pl.pallas_call(
        body, grid=(T // tc, nb),
        in_specs=[blk(0), blk(GATE_A0), blk(GATE_B0), blk(0), blk(0), blk(nb)],
        out_specs=[blk(0)] * 5,
        out_shape=[jax.ShapeDtypeStruct((T, D), BF16)] * 5,
        compiler_params=_cparams(("arbitrary", "arbitrary")), name=name)(dm, p, p, ya, z, z)
    return dya, dga, dgb, dz1, dz2


def _scan_block(src_r, src_i, dst_r, dst_i, car_r, car_i, pw_r, pw_i, cw_r, cw_i, ntiles, reverse, extra=None):
    W = src_r.shape[1]
    rows = lax.broadcasted_iota(jnp.int32, (8, W), 0)
    steps = []
    for d, pr in ((1, 0), (2, 1), (4, 3)):
        valid = rows < 8 - d if reverse else rows >= d
        steps.append((d, jnp.where(valid, jnp.broadcast_to(pw_r[pr:pr + 1, :], (8, W)), 0.0),
                      jnp.where(valid, jnp.broadcast_to(pw_i[pr:pr + 1, :], (8, W)), 0.0)))
    cw_r, cw_i = cw_r[...], cw_i[...]

    def tile(jj, carry):
        j = ntiles - 1 - jj if reverse else jj
        sl = pl.ds(pl.multiple_of(j * 8, 8), 8)
        xr, xi = src_r[sl, :], src_i[sl, :]
        for d, lr, li in steps:
            sr = pltpu.roll(xr, 8 - d if reverse else d, 0)
            si = pltpu.roll(xi, 8 - d if reverse else d, 0)
            xr, xi = xr + lr * sr - li * si, xi + lr * si + li * sr
        cr, ci = car_r[...], car_i[...]
        xr, xi = xr + cw_r * cr - cw_i * ci, xi + cw_r * ci + cw_i * cr
        dst_r[sl, :] = xr
        dst_i[sl, :] = xi
        edge = 0 if reverse else 7
        car_r[...] = jnp.broadcast_to(xr[edge:edge + 1, :], (8, W))
        car_i[...] = jnp.broadcast_to(xi[edge:edge + 1, :], (8, W))
        if extra is not None:
            carry = extra(j, xr, xi, carry)
        return carry

    return tile


def ssm_fwd(p, Br, Bi, Cr, Ci, pw_r, pw_i, dvec, *, name):
    T = p.shape[0]
    tt = _pick(T, [512, 256, 128])
    nt = T // tt
    WI, WS = SSM_BLOCK_IN, SSM_BLOCK_STATE
    u0 = 2 * D_MODEL // WI

    def body(u_ref, br_ref, bi_ref, cr_ref, ci_ref, pwr_ref, pwi_ref, d_ref, xr_ref, xi_ref, y_ref, bur, bui, car_r, car_i):
        i = pl.program_id(1)

        @pl.when(i == 0)
        def _():
            car_r[...] = jnp.zeros_like(car_r)
            car_i[...] = jnp.zeros_like(car_i)

        u = u_ref[...]
        ub = u.astype(BF16)
        bur[...] = jnp.dot(ub, br_ref[...].astype(BF16), preferred_element_type=F32)
        bui[...] = jnp.dot(ub, bi_ref[...].astype(BF16), preferred_element_type=F32)
        tile = _scan_block(bur, bui, xr_ref, xi_ref, car_r, car_i, pwr_ref, pwi_ref, pwr_ref, pwi_ref, tt // 8, False)
        lax.fori_loop(0, tt // 8, tile, 0)
        y = (jnp.dot(xr_ref[...].astype(BF16), cr_ref[...].astype(BF16), preferred_element_type=F32)
             - jnp.dot(xi_ref[...].astype(BF16), ci_ref[...].astype(BF16), preferred_element_type=F32)
             + d_ref[...] * u)
        y_ref[...] = y.astype(BF16)

    return pl.pallas_call(
        body, grid=(SSM_BLOCKS, nt),
        in_specs=[pl.BlockSpec((tt, WI), lambda b, i: (i, u0 + b)),
                  pl.BlockSpec((None, WI, WS), lambda b, i: (b, 0, 0)), pl.BlockSpec((None, WI, WS), lambda b, i: (b, 0, 0)),
                  pl.BlockSpec((None, WS, WI), lambda b, i: (b, 0, 0)), pl.BlockSpec((None, WS, WI), lambda b, i: (b, 0, 0)),
                  pl.BlockSpec((8, WS), lambda b, i: (0, b)), pl.BlockSpec((8, WS), lambda b, i: (0, b)),
                  pl.BlockSpec((1, WI), lambda b, i: (0, b))],
        out_specs=[pl.BlockSpec((tt, WS), lambda b, i: (i, b)), pl.BlockSpec((tt, WS), lambda b, i: (i, b)),
                   pl.BlockSpec((tt, WI), lambda b, i: (i, b))],
        out_shape=[jax.ShapeDtypeStruct((T, SSM_BLOCKS * WS), F32)] * 2 + [jax.ShapeDtypeStruct((T, D_SSM), BF16)],
        scratch_shapes=[pltpu.VMEM((tt, WS), F32), pltpu.VMEM((tt, WS), F32), pltpu.VMEM((8, WS), F32), pltpu.VMEM((8, WS), F32)],
        compiler_params=_cparams(("arbitrary", "arbitrary")), name=name)(p, Br, Bi, Cr, Ci, pw_r, pw_i, dvec)


def ssm_bwd(dy, p, xr, xi, Br, Bi, Cr, Ci, pwc_r, pwc_i, cwc_r, cwc_i, dvec, *, name):
    T = p.shape[0]
    tt = _pick(T, [512, 256, 128])
    nt = T // tt
    WI, WS = SSM_BLOCK_IN, SSM_BLOCK_STATE
    u0 = 2 * D_MODEL // WI
    tb = lambda i: nt - 1 - i
    xprev = lambda i: jnp.maximum(tb(i) * (tt // 8) - 1, 0)
    tn_dims = _DIMS["tn"]
    nt_dims = _DIMS["nt"]

    def body(dy_ref, u_ref, xr_ref, xi_ref, xpr_ref, xpi_ref, br_ref, bi_ref, cr_ref, ci_ref, pwr_ref, pwi_ref,
             cwr_ref, cwi_ref, d_ref,
             du_ref, dbr_ref, dbi_ref, dcr_ref, dci_ref, dar_ref, dai_ref, dd_ref,
             gr, gi, ext_r, ext_i, car_r, car_i):
        i = pl.program_id(1)

        @pl.when(i == 0)
        def _():
            car_r[...] = jnp.zeros_like(car_r)
            car_i[...] = jnp.zeros_like(car_i)
            dbr_ref[...] = jnp.zeros_like(dbr_ref)
            dbi_ref[...] = jnp.zeros_like(dbi_ref)
            dcr_ref[...] = jnp.zeros_like(dcr_ref)
            dci_ref[...] = jnp.zeros_like(dci_ref)
            dar_ref[...] = jnp.zeros_like(dar_ref)
            dai_ref[...] = jnp.zeros_like(dai_ref)
            dd_ref[...] = jnp.zeros_like(dd_ref)

        dy = dy_ref[...]
        dyb = dy.astype(BF16)
        u = u_ref[...]
        ub = u.astype(BF16)
        gr[...] = lax.dot_general(dyb, cr_ref[...].astype(BF16), nt_dims, preferred_element_type=F32)
        gi[...] = -lax.dot_general(dyb, ci_ref[...].astype(BF16), nt_dims, preferred_element_type=F32)
        first = tb(i) == 0
        ext_r[0:8, :] = jnp.where(first, 0.0, xpr_ref[...])
        ext_i[0:8, :] = jnp.where(first, 0.0, xpi_ref[...])
        ext_r[8:8 + tt, :] = xr_ref[...]
        ext_i[8:8 + tt, :] = xi_ref[...]
        rows = lax.broadcasted_iota(jnp.int32, (8, WS), 0)

        def lam_grad(j, g_r, g_i, carry):
            a_r, a_i = carry
            cur = pl.ds(pl.multiple_of(j * 8 + 8, 8), 8)
            prv = pl.ds(pl.multiple_of(j * 8, 8), 8)
            xc_r, xc_i = ext_r[cur, :], ext_i[cur, :]
            xl_r, xl_i = ext_r[prv, :], ext_i[prv, :]
            xp_r = jnp.where(rows == 0, jnp.broadcast_to(xl_r[7:8, :], (8, WS)), pltpu.roll(xc_r, 1, 0))
            xp_i = jnp.where(rows == 0, jnp.broadcast_to(xl_i[7:8, :], (8, WS)), pltpu.roll(xc_i, 1, 0))
            return (a_r + g_r * xp_r + g_i * xp_i, a_i + g_i * xp_r - g_r * xp_i)

        tile = _scan_block(gr, gi, gr, gi, car_r, car_i, pwr_ref, pwi_ref, cwr_ref, cwi_ref, tt // 8, True, extra=lam_grad)
        z8 = jnp.zeros((8, WS), F32)
        a_r, a_i = lax.fori_loop(0, tt // 8, tile, (z8, z8))
        dar_ref[...] += a_r
        dai_ref[...] += a_i
        grb = gr[...].astype(BF16)
        gib = gi[...].astype(BF16)
        dbr_ref[...] += lax.dot_general(ub, grb, tn_dims, preferred_element_type=F32)
        dbi_ref[...] += lax.dot_general(ub, gib, tn_dims, preferred_element_type=F32)
        dcr_ref[...] += lax.dot_general(xr_ref[...].astype(BF16), dyb, tn_dims, preferred_element_type=F32)
        dci_ref[...] -= lax.dot_general(xi_ref[...].astype(BF16), dyb, tn_dims, preferred_element_type=F32)
        du = (lax.dot_general(grb, br_ref[...].astype(BF16), nt_dims, preferred_element_type=F32)
              + lax.dot_general(gib, bi_ref[...].astype(BF16), nt_dims, preferred_element_type=F32)
              + d_ref[...] * dy)
        du_ref[...] = du.astype(BF16)
        dd_ref[...] += _rowsum8(dy * u)

    wspec = lambda shp: pl.BlockSpec((None,) + shp, lambda b, i: (b, 0, 0))
    return pl.pallas_call(
        body, grid=(SSM_BLOCKS, nt),
        in_specs=[pl.BlockSpec((tt, WI), lambda b, i: (tb(i), b)),
                  pl.BlockSpec((tt, WI), lambda b, i: (tb(i), u0 + b)),
                  pl.BlockSpec((tt, WS), lambda b, i: (tb(i), b)), pl.BlockSpec((tt, WS), lambda b, i: (tb(i), b)),
                  pl.BlockSpec((8, WS), lambda b, i: (xprev(i), b)), pl.BlockSpec((8, WS), lambda b, i: (xprev(i), b)),
                  wspec((WI, WS)), wspec((WI, WS)), wspec((WS, WI)), wspec((WS, WI)),
                  pl.BlockSpec((8, WS), lambda b, i: (0, b)), pl.BlockSpec((8, WS), lambda b, i: (0, b)),
                  pl.BlockSpec((8, WS), lambda b, i: (0, b)), pl.BlockSpec((8, WS), lambda b, i: (0, b)),
                  pl.BlockSpec((1, WI), lambda b, i: (0, b))],
        out_specs=[pl.BlockSpec((tt, WI), lambda b, i: (tb(i), b)),
                   wspec((WI, WS)), wspec((WI, WS)), wspec((WS, WI)), wspec((WS, WI)),
                   pl.BlockSpec((8, WS), lambda b, i: (0, b)), pl.BlockSpec((8, WS), lambda b, i: (0, b)),
                   pl.BlockSpec((8, WI), lambda b, i: (0, b))],
        out_shape=[jax.ShapeDtypeStruct((T, D_SSM), BF16),
                   jax.ShapeDtypeStruct((SSM_BLOCKS, WI, WS), F32), jax.ShapeDtypeStruct((SSM_BLOCKS, WI, WS), F32),
                   jax.ShapeDtypeStruct((SSM_BLOCKS, WS, WI), F32), jax.ShapeDtypeStruct((SSM_BLOCKS, WS, WI), F32),
                   jax.ShapeDtypeStruct((8, SSM_BLOCKS * WS), F32), jax.ShapeDtypeStruct((8, SSM_BLOCKS * WS), F32),
                   jax.ShapeDtypeStruct((8, D_SSM), F32)],
        scratch_shapes=[pltpu.VMEM((tt, WS), F32), pltpu.VMEM((tt, WS), F32),
                        pltpu.VMEM((tt + 8, WS), F32), pltpu.VMEM((tt + 8, WS), F32),
                        pltpu.VMEM((8, WS), F32), pltpu.VMEM((8, WS), F32)],
        compiler_params=_cparams(("arbitrary", "arbitrary")), name=name,
    )(dy, p, xr, xi, xr, xi, Br, Bi, Cr, Ci, pwc_r, pwc_i, cwc_r, cwc_i, dvec)


def _ssm_discretise(log_step, lam_re, lam_im, b_re, b_im):
    step = jnp.exp(log_step)[:, None]
    mag = jnp.exp(lam_re * step)
    ar = mag * jnp.cos(lam_im * step)
    ai = mag * jnp.sin(lam_im * step)
    den = lam_re * lam_re + lam_im * lam_im
    nr = ar - 1.0
    cr = (nr * lam_re + ai * lam_im) / den
    ci = (ai * lam_re - nr * lam_im) / den
    bbr = cr[..., None] * b_re - ci[..., None] * b_im
    bbi = cr[..., None] * b_im + ci[..., None] * b_re
    return ar, ai, bbr, bbi


def _blockdiag_in(bb):
    t = jnp.transpose(bb, (0, 2, 1)).reshape(SSM_BLOCKS, 8, SSM_GROUP, SSM_STATE)
    eye = jnp.eye(8, dtype=bb.dtype)
    return (t[:, :, :, None, :] * eye[None, :, None, :, None]).reshape(SSM_BLOCKS, SSM_BLOCK_IN, SSM_BLOCK_STATE)


def _blockdiag_out(cc):
    t = jnp.transpose(cc, (0, 2, 1)).reshape(SSM_BLOCKS, 8, SSM_STATE, SSM_GROUP)
    eye = jnp.eye(8, dtype=cc.dtype)
    return (t[:, :, :, None, :] * eye[None, :, None, :, None]).reshape(SSM_BLOCKS, SSM_BLOCK_STATE, SSM_BLOCK_IN)


def _diag_in(d):
    t = d.reshape(SSM_BLOCKS, 8, SSM_GROUP, 8, SSM_STATE)
    t = jnp.einsum("bghgp->bghp", t).reshape(SSM_GROUPS, SSM_GROUP, SSM_STATE)
    return jnp.transpose(t, (0, 2, 1))


def _diag_out(d):
    t = d.reshape(SSM_BLOCKS, 8, SSM_STATE, 8, SSM_GROUP)
    t = jnp.einsum("bgpgh->bgph", t).reshape(SSM_GROUPS, SSM_STATE, SSM_GROUP)
    return jnp.transpose(t, (0, 2, 1))


def _powers(ar, ai):
    rs, is_ = [ar], [ai]
    for _ in range(7):
        r, i = rs[-1], is_[-1]
        rs.append(r * ar - i * ai)
        is_.append(r * ai + i * ar)
    return jnp.stack(rs), jnp.stack(is_), jnp.stack(rs[::-1]), jnp.stack(is_[::-1])


def attn_fwd(q, kv, *, name):
    T, D = q.shape
    nm = kv.shape[0]
    tq = _pick(T, [512, 256, 128])
    scale = HEAD_DIM ** -0.5

    def body(q_ref, k_ref, v_ref, o_ref):
        for h in range(N_HEADS):
            sl = slice(h * HEAD_DIM, (h + 1) * HEAD_DIM)
            s = lax.dot_general(q_ref[:, sl], k_ref[:, sl].astype(BF16), _DIMS["nt"], preferred_element_type=F32) * scale
            e = jnp.exp(s - jnp.max(s, axis=-1, keepdims=True))
            pr = e / jnp.sum(e, axis=-1, keepdims=True)
            o_ref[:, sl] = jnp.dot(pr.astype(BF16), v_ref[:, sl].astype(BF16), preferred_element_type=F32).astype(BF16)

    return pl.pallas_call(
        body, grid=(T // tq,),
        in_specs=[_rows(tq, D), pl.BlockSpec((nm, D), lambda i: (0, 0)), pl.BlockSpec((nm, D), lambda i: (0, 1))],
        out_specs=_rows(tq, D), out_shape=jax.ShapeDtypeStruct((T, D), BF16),
        compiler_params=_cparams(("arbitrary",)), name=name)(q, kv, kv)


def attn_bwd(q, kv, do, *, name):
    T, D = q.shape
    nm = kv.shape[0]
    tq = _pick(T, [512, 256, 128])
    nt = T // tq
    scale = HEAD_DIM ** -0.5

    def body(q_ref, k_ref, v_ref, do_ref, dq_ref, dkv_ref):
        i = pl.program_id(0)

        @pl.when(i == 0)
        def _():
            dkv_ref[...] = jnp.zeros_like(dkv_ref)

        for h in range(N_HEADS):
            sl = slice(h * HEAD_DIM, (h + 1) * HEAD_DIM)
            slv = slice(D + h * HEAD_DIM, D + (h + 1) * HEAD_DIM)
            qh = q_ref[:, sl]
            kh = k_ref[:, sl].astype(BF16)
            vh = v_ref[:, sl].astype(BF16)
            doh = do_ref[:, sl].astype(BF16)
            s = lax.dot_general(qh, kh, _DIMS["nt"], preferred_element_type=F32) * scale
            e = jnp.exp(s - jnp.max(s, axis=-1, keepdims=True))
            pr = e / jnp.sum(e, axis=-1, keepdims=True)
            dp = lax.dot_general(doh, vh, _DIMS["nt"], preferred_element_type=F32)
            ds = (pr * (dp - jnp.sum(pr * dp, axis=-1, keepdims=True)) * scale).astype(BF16)
            dq_ref[:, sl] = jnp.dot(ds, kh, preferred_element_type=F32).astype(BF16)
            dkv_ref[:, sl] += lax.dot_general(ds, qh, _DIMS["tn"], preferred_element_type=F32)
            dkv_ref[:, slv] += lax.dot_general(pr.astype(BF16), doh, _DIMS["tn"], preferred_element_type=F32)

    return pl.pallas_call(
        body, grid=(nt,),
        in_specs=[_rows(tq, D), pl.BlockSpec((nm, D), lambda i: (0, 0)), pl.BlockSpec((nm, D), lambda i: (0, 1)), _rows(tq, D)],
        out_specs=[_rows(tq, D), _const((nm, 2 * D))],
        out_shape=[jax.ShapeDtypeStruct((T, D), BF16), jax.ShapeDtypeStruct((nm, 2 * D), F32)],
        compiler_params=_cparams(("arbitrary",)), name=name)(q, kv, kv, do)


def _adam_math(w, g, m, v):
    m = ADAM_B1 * m + (1.0 - ADAM_B1) * g
    v = ADAM_B2 * v + (1.0 - ADAM_B2) * (g * g)
    m_hat = m / (1.0 - ADAM_B1 ** ADAM_STEP)
    v_hat = v / (1.0 - ADAM_B2 ** ADAM_STEP)
    delta = -ADAM_LR * (m_hat / (jnp.sqrt(v_hat) + ADAM_EPS) + ADAM_WD * w)
    return delta, m, v


def adamw(w, m, v, g_arr, g_row0, *, name):
    R, C = w.shape
    tr = _pick(R, [256, 128, 64, 32, 16, 8])
    assert g_row0 % tr == 0
    g0 = g_row0 // tr

    def body(w_ref, m_ref, v_ref, g_ref, go_ref, d_ref, mo_ref, vo_ref):
        g = g_ref[...]
        d, mn, vn = _adam_math(w_ref[...], g, m_ref[...], v_ref[...])
        go_ref[...] = g
        d_ref[...] = d
        mo_ref[...] = mn
        vo_ref[...] = vn

    sp = pl.BlockSpec((tr, C), lambda i: (i, 0))
    return pl.pallas_call(
        body, grid=(R // tr,), in_specs=[sp, sp, sp, pl.BlockSpec((tr, C), lambda i: (g0 + i, 0))],
        out_specs=[sp] * 4, out_shape=[jax.ShapeDtypeStruct((R, C), F32)] * 4,
        compiler_params=_cparams(("arbitrary",)), name=name)(w, m, v, g_arr)


def _place():
    x, y, c = lax.axis_index("x"), lax.axis_index("y"), lax.axis_index("c")
    chips = [(1 - x, y), (x, 1 - y), (1 - x, 1 - y)]
    return x, y, c, chips


ANY = pl.BlockSpec(memory_space=pl.ANY)


def allgather_weights(bufs, *, name):
    n = len(bufs)

    def body(*refs):
        o_refs = refs[n:2 * n]
        send_sems, recv_sems, fsend_sems, frecv_sems = refs[2 * n:]
        x, y, c, chips = _place()
        k_me = 2 * x + y
        sib = (x, y, 1 - c)
        halves = [b.shape[1] // 2 for b in bufs]

        def half(a, cc):
            return pl.ds(pl.multiple_of(cc * halves[a], 16), halves[a])

        sends = []
        for a in range(n):
            for r, (px, py) in enumerate(chips):
                cp = pltpu.make_async_remote_copy(
                    src_ref=o_refs[a].at[k_me, half(a, c)], dst_ref=o_refs[a].at[k_me, half(a, c)],
                    send_sem=send_sems.at[3 * a + r], recv_sem=recv_sems.at[3 * a + r],
                    device_id=(px, py, c), device_id_type=MESH)
                cp.start()
                sends.append(cp)
        passed = []
        for a in range(n):
            for r, (px, py) in enumerate(chips):
                win = o_refs[a].at[2 * px + py, half(a, c)]
                pltpu.make_async_remote_copy(
                    src_ref=win, dst_ref=win, send_sem=send_sems.at[3 * a + r], recv_sem=recv_sems.at[3 * a + r],
                    device_id=(px, py, c), device_id_type=MESH).wait_recv()
                cp = pltpu.make_async_remote_copy(
                    src_ref=win, dst_ref=win, send_sem=fsend_sems.at[3 * a + r], recv_sem=frecv_sems.at[3 * a + r],
                    device_id=sib, device_id_type=MESH)
                cp.start()
                passed.append(cp)
        for a in range(n):
            for r, (px, py) in enumerate(chips):
                win = o_refs[a].at[2 * px + py, half(a, 1 - c)]
                pltpu.make_async_remote_copy(
                    src_ref=win, dst_ref=win, send_sem=fsend_sems.at[3 * a + r], recv_sem=frecv_sems.at[3 * a + r],
                    device_id=sib, device_id_type=MESH).wait_recv()
        for cp in sends + passed:
            cp.wait_send()

    return pl.pallas_call(
        body, in_specs=[ANY] * n, out_specs=[ANY] * n,
        out_shape=[jax.ShapeDtypeStruct(b.shape, b.dtype) for b in bufs],
        scratch_shapes=[pltpu.SemaphoreType.DMA((3 * n,))] * 4,
        input_output_aliases={a: a for a in range(n)},
        name=name)(*bufs)


HBM_SPEC = pl.BlockSpec(memory_space=pltpu.HBM)
SEM_SPEC = pl.BlockSpec(memory_space=pltpu.SEMAPHORE)


def _hbm(a):
    return pltpu.with_memory_space_constraint(a, pltpu.HBM)


def gather_start(bufs, pieces, *, name):
    n = len(bufs)
    npc = len(pieces)

    def body(*refs):
        b_refs = refs[:n]
        send_sems, recv_sems = refs[n], refs[n + 1]
        x, y, c, chips = _place()
        k_me = 2 * x + y
        for q, (a, row0, rows) in enumerate(pieces):
            win = b_refs[a].at[k_me, pl.ds(row0, rows)]
            for r, (px, py) in enumerate(chips):
                pltpu.make_async_remote_copy(
                    src_ref=win, dst_ref=win, send_sem=send_sems.at[3 * q + r], recv_sem=recv_sems.at[3 * q + r],
                    device_id=(px, py, c), device_id_type=MESH).start()

    return pl.pallas_call(
        body, in_specs=[HBM_SPEC] * n, out_specs=[SEM_SPEC, SEM_SPEC] + [HBM_SPEC] * n,
        out_shape=[pltpu.SemaphoreType.DMA((3 * npc,)), pltpu.SemaphoreType.DMA((3 * npc,))]
        + [pltpu.HBM(b.shape, b.dtype) for b in bufs],
        input_output_aliases={a: 2 + a for a in range(n)},
        compiler_params=pltpu.CompilerParams(has_side_effects=pltpu.SideEffectType.DATAFLOW_SIDE_EFFECTING),
        name=name)(*[_hbm(b) for b in bufs])


def gather_wait(send_sems, recv_sems, bufs, which, after, *, name):
    n = len(bufs)

    def body(*refs):
        b_refs = refs[:n]
        send_sems, recv_sems = refs[n], refs[n + 1]
        x, y, c, chips = _place()
        k_me = 2 * x + y
        for a, row0, rows, q in which:
            for r, (px, py) in enumerate(chips):
                cp = pltpu.make_async_remote_copy(
                    src_ref=b_refs[a].at[k_me, pl.ds(row0, rows)], dst_ref=b_refs[a].at[2 * px + py, pl.ds(row0, rows)],
                    send_sem=send_sems.at[3 * q + r], recv_sem=recv_sems.at[3 * q + r],
                    device_id=(px, py, c), device_id_type=MESH)
                cp.wait_send()
                cp.wait_recv()

    return pl.pallas_call(
        body, in_specs=[HBM_SPEC] * n + [SEM_SPEC, SEM_SPEC, ANY], out_specs=[HBM_SPEC] * n,
        out_shape=[pltpu.HBM(b.shape, b.dtype) for b in bufs],
        input_output_aliases={a: a for a in range(n)},
        compiler_params=pltpu.CompilerParams(has_side_effects=pltpu.SideEffectType.DATAFLOW_SIDE_EFFECTING),
        name=name)(*bufs, send_sems, recv_sems, after)


def exchange_halves(grads, *, name):
    n = len(grads)

    def body(*refs):
        g_refs, l_refs = refs[:n], refs[n:2 * n]
        send_sems, recv_sems = refs[2 * n:]
        x, y, c, _ = _place()
        cps = []
        for a in range(n):
            h = grads[a].shape[1] // 2
            cp = pltpu.make_async_remote_copy(
                src_ref=g_refs[a].at[:, pl.ds(pl.multiple_of((1 - c) * h, 8), h)], dst_ref=l_refs[a],
                send_sem=send_sems.at[a], recv_sem=recv_sems.at[a], device_id=(x, y, 1 - c), device_id_type=MESH)
            cp.start()
            cps.append(cp)
        for cp in cps:
            cp.wait()

    return pl.pallas_call(
        body, in_specs=[ANY] * n, out_specs=[ANY] * n,
        out_shape=[jax.ShapeDtypeStruct((g.shape[0], g.shape[1] // 2, g.shape[2]), g.dtype) for g in grads],
        scratch_shapes=[pltpu.SemaphoreType.DMA((n,))] * 2,
        name=name)(*grads)


N_PEERS = N_DEV - 1


def _scatter_copies(p_refs, l_refs, send_sems, recv_sems):
    x, y, c, _ = _place()
    cps = []
    for a in range(len(p_refs)):
        h = p_refs[a].shape[1] // 2
        for fx, fy in ((0, 0), (1, 0), (0, 1), (1, 1)):
            for fc in (0, 1):
                if (fx, fy, fc) == (0, 0, 0):
                    continue
                slot = 2 * (fx + 2 * fy) + fc - 1
                px, py, pc = (1 - x if fx else x), (1 - y if fy else y), (1 - c if fc else c)
                cps.append(pltpu.make_async_remote_copy(
                    src_ref=p_refs[a].at[2 * px + py, pl.ds(pl.multiple_of(pc * h, 16), h)], dst_ref=l_refs[a].at[slot],
                    send_sem=send_sems.at[N_PEERS * a + slot], recv_sem=recv_sems.at[N_PEERS * a + slot],
                    device_id=(px, py, pc), device_id_type=MESH))
    return cps


def scatter_start(parts, *, name):
    n = len(parts)
    lands = [lax.empty((N_PEERS, p.shape[1] // 2, p.shape[2]), p.dtype) for p in parts]

    def body(*refs):
        for cp in _scatter_copies(refs[:n], refs[n:2 * n], refs[2 * n], refs[2 * n + 1]):
            cp.start()

    outs = pl.pallas_call(
        body, in_specs=[HBM_SPEC] * (2 * n), out_specs=[SEM_SPEC, SEM_SPEC] + [HBM_SPEC] * (2 * n),
        out_shape=[pltpu.SemaphoreType.DMA((N_PEERS * n,)), pltpu.SemaphoreType.DMA((N_PEERS * n,))]
        + [pltpu.HBM(a.shape, a.dtype) for a in parts + lands],
        input_output_aliases={a: 2 + a for a in range(2 * n)},
        compiler_params=pltpu.CompilerParams(has_side_effects=pltpu.SideEffectType.DATAFLOW_SIDE_EFFECTING),
        name=name)(*[_hbm(a) for a in parts + lands])
    return outs[0], outs[1], list(outs[2:2 + n]), list(outs[2 + n:])


def scatter_wait(rounds, after, *, name):
    sizes = [len(r[2]) for r in rounds]
    flat = [a for r in rounds for a in r[2] + r[3]]
    sems = [s for r in rounds for s in (r[0], r[1])]
    nflat = len(flat)

    def body(*refs):
        pos = 0
        for ri, n in enumerate(sizes):
            for cp in _scatter_copies(refs[pos:pos + n], refs[pos + n:pos + 2 * n], refs[nflat + 2 * ri], refs[nflat + 2 * ri + 1]):
                cp.wait_send()
                cp.wait_recv()
            pos += 2 * n

    outs = pl.pallas_call(
        body, in_specs=[HBM_SPEC] * nflat + [SEM_SPEC] * len(sems) + [ANY], out_specs=[HBM_SPEC] * nflat,
        out_shape=[pltpu.HBM(a.shape, a.dtype) for a in flat],
        input_output_aliases={a: a for a in range(nflat)},
        compiler_params=pltpu.CompilerParams(has_side_effects=pltpu.SideEffectType.DATAFLOW_SIDE_EFFECTING),
        name=name)(*flat, *sems, after)
    res, pos = [], 0
    for n in sizes:
        res.append((list(outs[pos:pos + n]), list(outs[pos + n:pos + 2 * n])))
        pos += 2 * n
    return res


def join_halves(fulls, *, name):
    n = len(fulls)

    def body(*refs):
        o_refs = refs[n:2 * n]
        send_sems, recv_sems = refs[2 * n:]
        x, y, c, _ = _place()
        cps = []
        for a in range(n):
            h = fulls[a].shape[0] // 2
            win = o_refs[a].at[pl.ds(pl.multiple_of(c * h, 8), h)]
            cp = pltpu.make_async_remote_copy(
                src_ref=win, dst_ref=win, send_sem=send_sems.at[a], recv_sem=recv_sems.at[a],
                device_id=(x, y, 1 - c), device_id_type=MESH)
            cp.start()
            cps.append(cp)
        for a in range(n):
            h = fulls[a].shape[0] // 2
            other = o_refs[a].at[pl.ds(pl.multiple_of((1 - c) * h, 8), h)]
            pltpu.make_async_remote_copy(
                src_ref=other, dst_ref=other, send_sem=send_sems.at[a], recv_sem=recv_sems.at[a],
                device_id=(x, y, 1 - c), device_id_type=MESH).wait_recv()
        for cp in cps:
            cp.wait_send()

    return pl.pallas_call(
        body, in_specs=[ANY] * n, out_specs=[ANY] * n,
        out_shape=[jax.ShapeDtypeStruct(f.shape, f.dtype) for f in fulls],
        scratch_shapes=[pltpu.SemaphoreType.DMA((n,))] * 2,
        input_output_aliases={a: a for a in range(n)},
        name=name)(*fulls)


def add_partials(part, land, kc, *, name):
    _, R, C = part.shape
    H = R // 2
    tr = _pick(H, [256, 128, 64, 32, 16])
    per = H // tr

    def body(kc_ref, p_ref, l_ref, o_ref):
        acc = p_ref[...].astype(F32)
        for s in range(N_PEERS):
            acc = acc + l_ref[s].astype(F32)
        o_ref[...] = acc

    return pl.pallas_call(
        body,
        grid_spec=pltpu.PrefetchScalarGridSpec(
            num_scalar_prefetch=1, grid=(per,),
            in_specs=[pl.BlockSpec((None, tr, C), lambda i, kc_ref: (kc_ref[0], kc_ref[1] * per + i, 0)),
                      pl.BlockSpec((N_PEERS, tr, C), lambda i, kc_ref: (0, i, 0))],
            out_specs=pl.BlockSpec((tr, C), lambda i, kc_ref: (kc_ref[1] * per + i, 0))),
        out_shape=jax.ShapeDtypeStruct((R, C), F32),
        compiler_params=_cparams(("arbitrary",)), name=name)(kc, part, land)


def allgather_sum(v, *, name):
    m_per, n = v.shape

    def body(x_ref, out_ref, sum_ref, send_sems, recv_sems, local_sem):
        x, y, c, chips = _place()
        me, sibling = (x, y, c), (x, y, 1 - c)

        def rows(px, py, pc):
            return out_ref.at[pl.ds(pl.multiple_of((4 * px + 2 * py + pc) * m_per, 8), m_per), :]

        def copy(k, block, to, src=None):
            return pltpu.make_async_remote_copy(
                src_ref=rows(*block) if src is None else src, dst_ref=rows(*block),
                send_sem=send_sems.at[k], recv_sem=recv_sems.at[k], device_id=to, device_id_type=MESH)

        mine = pltpu.make_async_copy(x_ref, rows(*me), local_sem)
        mine.start()
        first = [copy(0, me, sibling, src=x_ref)]
        first += [copy(1 + j, me, (*chip, c), src=x_ref) for j, chip in enumerate(chips)]
        for cp in first:
            cp.start()
        passed = [copy(4 + j, (*chip, c), sibling) for j, chip in enumerate(chips)]
        for j, chip in enumerate(chips):
            copy(1 + j, (*chip, c), me).wait_recv()
            passed[j].start()
        copy(0, sibling, me).wait_recv()
        for j, chip in enumerate(chips):
            copy(4 + j, (*chip, 1 - c), me).wait_recv()
        for cp in first + passed:
            cp.wait_send()
        mine.wait()
        acc = out_ref[0:m_per, :]
        for d in range(1, N_DEV):
            acc = acc + out_ref[d * m_per:(d + 1) * m_per, :]
        sum_ref[...] = acc

    vm = pl.BlockSpec(memory_space=pltpu.VMEM)
    return pl.pallas_call(
        body, in_specs=[vm], out_specs=[vm, vm],
        out_shape=[jax.ShapeDtypeStruct((N_DEV * m_per, n), v.dtype), jax.ShapeDtypeStruct((m_per, n), v.dtype)],
        scratch_shapes=[pltpu.SemaphoreType.DMA((7,)), pltpu.SemaphoreType.DMA((7,)), pltpu.SemaphoreType.DMA],
        compiler_params=pltpu.CompilerParams(vmem_limit_bytes=VMEM_LIMIT_BYTES), name=name)(v)


def _pack(arrs):
    cols = []
    for a in arrs:
        f = a.reshape(-1)
        pad = (-f.shape[0]) % 128
        cols.append(jnp.pad(f, (0, pad)).reshape(-1, 128))
    out = jnp.concatenate(cols, axis=0)
    pad = (-out.shape[0]) % 8
    return jnp.pad(out, ((0, pad), (0, 0)))


def _unpack(buf, shapes):
    outs, r = [], 0
    for s in shapes:
        nel = math.prod(s)
        nr = -(-nel // 128)
        outs.append(buf[r:r + nr].reshape(-1)[:nel].reshape(s))
        r += nr
    return outs


GA_CONV_OUT, GA_MIX_OUT, GA_WQ, GA_WO, GA_DOWN, GA_UP, GA_ROWS = 0, 256, 512, 768, 1024, 2048, 3072
G1_DOWN, G1_UP, G1_ROWS = 0, 1024, 2048
G2_CONV_OUT, G2_MIX_OUT, G2_WQ, G2_WO, G2_ROWS = 0, 256, 512, 768, 1024


def kernel(x, mem, in_norm_g, in_norm_b, w_in, conv_dw, conv_db, conv_norm_g, conv_norm_b, w_conv_out, ssm_log_step, ssm_lambda_re, ssm_lambda_im, ssm_b_re, ssm_b_im, ssm_c_re, ssm_c_im, ssm_d, w_ssm_glu, w_mix_out, ln1_g, ln1_b, xa_wq, xa_wkv, xa_wo, ln2_g, ln2_b, mlp_w_up, mlp_w_down, ln3_g, ln3_b, loss_target, m_in_norm_g, m_in_norm_b, m_w_in, m_conv_dw, m_conv_db, m_conv_norm_g, m_conv_norm_b, m_w_conv_out, m_ssm_log_step, m_ssm_lambda_re, m_ssm_lambda_im, m_ssm_b_re, m_ssm_b_im, m_ssm_c_re, m_ssm_c_im, m_ssm_d, m_w_ssm_glu, m_w_mix_out, m_ln1_g, m_ln1_b, m_xa_wq, m_xa_wkv, m_xa_wo, m_ln2_g, m_ln2_b, m_mlp_w_up, m_mlp_w_down, m_ln3_g, m_ln3_b, v_in_norm_g, v_in_norm_b, v_w_in, v_conv_dw, v_conv_db, v_conv_norm_g, v_conv_norm_b, v_w_conv_out, v_ssm_log_step, v_ssm_lambda_re, v_ssm_lambda_im, v_ssm_b_re, v_ssm_b_im, v_ssm_c_re, v_ssm_c_im, v_ssm_d, v_w_ssm_glu, v_w_mix_out, v_ln1_g, v_ln1_b, v_xa_wq, v_xa_wkv, v_xa_wo, v_ln2_g, v_ln2_b, v_mlp_w_up, v_mlp_w_down, v_ln3_g, v_ln3_b):
    D = D_MODEL
    xs = x[0]
    T = xs.shape[0]
    mems = mem[0]
    NM = mems.shape[0]
    tgt = loss_target[0]
    my_c = lax.axis_index("c")
    k_me = 2 * lax.axis_index("x") + lax.axis_index("y")
    c_arr = jnp.reshape(my_c, (1,)).astype(jnp.int32)
    k_arr = jnp.reshape(k_me, (1,)).astype(jnp.int32)

    sh_a = jnp.concatenate([w_conv_out[0], w_mix_out[0], xa_wq[0], xa_wo[0], mlp_w_down[0], mlp_w_up[0]], axis=0).astype(BF16)
    def own_block(shard):
        buf = jnp.zeros((N_CHIPS,) + shard.shape, shard.dtype)
        return lax.dynamic_update_slice(buf, shard[None], (k_me, 0, 0))

    dw_pad = jnp.pad(conv_dw[0], ((0, CONV_HALO - CONV_K), (0, 0)))
    ag_bufs = [own_block(s) for s in (sh_a, w_in[0].astype(BF16), xa_wkv[0].astype(BF16), w_ssm_glu[0].astype(BF16), dw_pad)]
    ag_pieces = [(1, 0, D), (4, 0, CONV_HALO), (0, GA_CONV_OUT, 256), (3, 0, D_SSM), (0, GA_MIX_OUT, 256), (0, GA_WQ, 256),
                 (2, 0, D), (0, GA_WO, 256), (0, GA_UP, D), (0, GA_DOWN, D)]
    ag_send, ag_recv, GA, GIN, GKV, GGLU, GDW = gather_start(ag_bufs, ag_pieces, name="gather_start")

    def w_rowshard(row0):
        return dict(b_spec=((N_CHIPS, 256, D), lambda i, j, k: (0, row0 // 256, 0)), b_view=(D, D), tn=D, tk=D)

    _, h0, h0b = ln_fwd(xs, in_norm_g, in_norm_b, name="ln0_fwd")
    (GIN,) = gather_wait(ag_send, ag_recv, [GIN], [(0, 0, D, 0)], h0b, name="gather_wait_in")
    p = mm_nn(h0b, GIN, ((None, D, 1152), lambda i, j, k: (j, 0, 0)), D_IN, tn=1152, tk=D, name="mm_w_in")[0]
    GA, GGLU, GDW = gather_wait(
        ag_send, ag_recv, [GA, GGLU, GDW],
        [(2, 0, CONV_HALO, 1), (0, GA_CONV_OUT, 256, 2), (1, 0, D_SSM, 3), (0, GA_MIX_OUT, 256, 4)], p, name="gather_wait_mixer")
    dw_taps = jnp.transpose(GDW, (1, 0, 2)).reshape(CONV_HALO, D)
    c_pre, actb = conv_fwd(p, dw_taps, conv_db, conv_norm_g[0].reshape(1, D), conv_norm_b[0].reshape(1, D), name="conv_fwd")
    ya = mm_nn(actb, GA, N=D, name="mm_conv_out", **w_rowshard(GA_CONV_OUT))[0]

    lstep, lre, lim = ssm_log_step[0], ssm_lambda_re[0], ssm_lambda_im[0]
    bre, bim, cre, cim = ssm_b_re[0], ssm_b_im[0], ssm_c_re[0], ssm_c_im[0]
    (ar, ai, bbr, bbi), disc_vjp = jax.vjp(_ssm_discretise, lstep, lre, lim, bre, bim)
    Br, Bi = _blockdiag_in(bbr), _blockdiag_in(bbi)
    Cr, Ci = _blockdiag_out(cre), _blockdiag_out(cim)
    pw_r, pw_i, pwrev_r, pwrev_i = _powers(ar.reshape(-1), ai.reshape(-1))
    dvec = ssm_d[0].reshape(1, D_SSM)
    xr, xi, yssm = ssm_fwd(p, Br, Bi, Cr, Ci, pw_r, pw_i, dvec, name="ssm_fwd")
    z = mm_nn(yssm, GGLU, ((None, D_SSM, 512), lambda i, j, k: (j, 0, 0)), 2 * D, tn=512, tk=D_SSM, name="mm_ssm_glu")[0]
    mergedb = merge_fwd(p, ya, z, name="merge_fwd")
    tm_ln = _pick(T, [512, 256, 128])
    row_spec = ((1, D), lambda i, j, k: (0, 0))

    def ln_epilogue(acc, res, g, b):
        r = ALPHA * res + acc
        xhat, _ = _ln_stats(r)
        h = xhat * g + b
        return r, h, h

    def mm_ln(a, row0, res, g, b, name):
        return mm_nn(a, GA, N=D, tm=tm_ln, extras=(res, g.reshape(1, D), b.reshape(1, D)),
                     extra_specs=[_mn(tm_ln, D), row_spec, row_spec], epilogue=ln_epilogue, out_dtypes=[F32, F32, BF16],
                     name=name, **w_rowshard(row0))

    r1, h1, h1b = mm_ln(mergedb, GA_MIX_OUT, h0, ln1_g[0], ln1_b[0], "mm_mix_out_ln1")
    GA, GKV = gather_wait(ag_send, ag_recv, [GA, GKV], [(0, GA_WQ, 256, 5), (1, 0, D, 6), (0, GA_WO, 256, 7)], r1,
                          name="gather_wait_attn")

    qb = mm_nn(h1b, GA, N=D, out_dtype=BF16, name="mm_wq", **w_rowshard(GA_WQ))[0]
    kv = mm_nn(mems, GKV, ((None, D, 512), lambda i, j, k: (j, 0, 0)), 2 * D, tn=512, tk=D, name="mm_wkv")[0]
    ob = attn_fwd(qb, kv, name="attn_fwd")
    r2, h2, h2b = mm_ln(ob, GA_WO, h1, ln2_g[0], ln2_b[0], "mm_wo_ln2")
    (GA,) = gather_wait(ag_send, ag_recv, [GA], [(0, GA_UP, D, 8), (0, GA_DOWN, D, 9)], r2, name="gather_wait_mlp")

    def relu2(acc):
        zr = jnp.maximum(acc, 0.0)
        return acc, zr * zr

    zpre, zzb = mm_nn(h2b, GA, ((None, D, D), lambda i, j, k: (j, GA_UP // D, 0)), D_FF, tn=D, tk=D,
                      out_dtypes=[F32, BF16], epilogue=relu2, name="mm_up")
    ff = mm_nn(zzb, GA, ((N_CHIPS, D, D), lambda i, j, k: (0, GA_DOWN // D, 0)), D, tm=_pick(T, [512, 256, 128]), tn=D, tk=D_FF,
               b_view=(D_FF, D), name="mm_down")[0]
    dr3, dr3b, dg3, db3, sq = ln_loss_bwd(ff, h2, ln3_g[0], ln3_b[0], tgt, name="ln3_loss_bwd")

    def rs_begin(grads, rnd):
        return scatter_start(grads, name=f"rs{rnd}_scatter_start")

    g1_shape = jax.ShapeDtypeStruct((N_CHIPS, G1_ROWS, D), BF16)
    g2_shape = jax.ShapeDtypeStruct((N_CHIPS, G2_ROWS, D), BF16)
    dzpreb = mm_nt(dr3b, GA, ((None, D, D), lambda i, j, k: (j, GA_DOWN // D, 0)), D_FF, tn=D, tk=D, out_dtype=BF16,
                   extras=(zpre,), epilogue=lambda acc, zp: (acc * (2.0 * jnp.maximum(zp, 0.0)),), name="mm_down_t")[0]
    G1g = mm_tn(zzb, dr3b, tm=D, tn=D, out_spec=((None, D, D), lambda i, j, k: (i, G1_DOWN // D, 0)), out_shape=g1_shape,
                name="mm_down_g")
    G1g = mm_tn(h2b, dzpreb, tm=D, tn=D, out_spec=((None, D, D), lambda i, j, k: (j, G1_UP // D, 0)), out_shape=g1_shape,
                out_buf=G1g, name="mm_up_g")
    round1 = rs_begin([G1g], 1)
    dh2 = mm_nt(dzpreb, GA, ((None, D, D), lambda i, j, k: (k, GA_UP // D, 0)), D, tn=D, tk=D,
                extras=(dr3,), epilogue=lambda acc, d: (acc + ALPHA * d,), after=(round1[2][0],), name="mm_up_t")[0]
    dr2, dr2b, dg2, db2 = ln_bwd(r2, dh2, ln2_g[0], name="ln2_bwd")

    def g_rowshard(row0, out_buf):
        return dict(tm=D, tn=D, out_spec=((N_CHIPS, 256, D), lambda i, j, k: (0, row0 // 256, 0)), out_shape=g2_shape,
                    out_buf=out_buf)

    dob = mm_nt(dr2b, GA, N=D, out_dtype=BF16, name="mm_wo_t", **w_rowshard(GA_WO))[0]
    G2g = mm_tn(ob, dr2b, name="mm_wo_g", **g_rowshard(G2_WO, None))
    dqb, dkv = attn_bwd(qb, kv, dob, name="attn_bwd")
    G2g = mm_tn(h1b, dqb, name="mm_wq_g", **g_rowshard(G2_WQ, G2g))
    GKVg = mm_tn(mems, dkv, tm=D, tn=512, tk=NM, out_spec=((None, D, 512), lambda i, j, k: (j, 0, 0)),
                 out_shape=jax.ShapeDtypeStruct((N_CHIPS, D, 512), BF16), name="mm_wkv_g")
    dh1 = mm_nt(dqb, GA, N=D, extras=(dr2,), epilogue=lambda acc, d: (acc + ALPHA * d,), name="mm_wq_t",
                **w_rowshard(GA_WQ))[0]
    dr1, dr1b, dg1, db1 = ln_bwd(r1, dh1, ln1_g[0], name="ln1_bwd")

    dmerged = mm_nt(dr1b, GA, N=D, name="mm_mix_t", **w_rowshard(GA_MIX_OUT))[0]
    G2g = mm_tn(mergedb, dr1b, name="mm_mix_g", **g_rowshard(G2_MIX_OUT, G2g))
    dyab, dgab, dgbb, dz1b, dz2b = merge_bwd(dmerged, p, ya, z, name="merge_bwd")
    dzb = jnp.concatenate([dz1b, dz2b], axis=1)
    GGLUg = mm_tn(yssm, dzb, tm=D_SSM, tn=512, out_spec=((None, D_SSM, 512), lambda i, j, k: (j, 0, 0)),
                  out_shape=jax.ShapeDtypeStruct((N_CHIPS, D_SSM, 512), BF16), name="mm_glu_g")
    dyssm = mm_nt(dzb, GGLU, ((None, D_SSM, 512), lambda i, j, k: (k, 0, 0)), D_SSM, tn=D_SSM, tk=512, name="mm_glu_t")[0]
    dub, dBr, dBi, dCr, dCi, dar8, dai8, dd8 = ssm_bwd(dyssm, p, xr, xi, Br, Bi, Cr, Ci, pw_r, -pw_i, pwrev_r, -pwrev_i, dvec,
                                                       name="ssm_bwd")
    dar = jnp.sum(dar8, axis=0).reshape(SSM_GROUPS, SSM_STATE)
    dai = jnp.sum(dai8, axis=0).reshape(SSM_GROUPS, SSM_STATE)
    g_lstep, g_lre, g_lim, g_bre, g_bim = disc_vjp((dar, dai, _diag_in(dBr), _diag_in(dBi)))
    g_cre, g_cim = _diag_out(dCr), _diag_out(dCi)
    g_d = jnp.sum(dd8, axis=0).reshape(1, D_SSM)

    dact = mm_nt(dyab, GA, N=D, name="mm_conv_out_t", **w_rowshard(GA_CONV_OUT))[0]
    G2g = mm_tn(actb, dyab, name="mm_conv_out_g", **g_rowshard(G2_CONV_OUT, G2g))
    round2 = rs_begin([G2g, GKVg, GGLUg], 2)
    dc, dng, dnb, ddb = conv_bwd_norm(dact, c_pre, conv_norm_g[0].reshape(1, D), conv_norm_b[0].reshape(1, D),
                                      round2[2][0], name="conv_bwd_norm")
    dvgb, ddw = conv_bwd_taps(dc, p, dw_taps, name="conv_bwd_taps")
    dpb = jnp.concatenate([dvgb, dub, dgab, dgbb], axis=1)
    GINg = mm_tn(h0b, dpb, tm=D, tn=1152, out_spec=((None, D, 1152), lambda i, j, k: (j, 0, 0)),
                 out_shape=jax.ShapeDtypeStruct((N_CHIPS, D, 1152), BF16), name="mm_w_in_g")
    round3 = rs_begin([GINg], 3)
    dh0 = mm_nt(dpb, GIN, ((None, 512, 1152), lambda i, j, k: (k, j, 0)), D, tn=512, tk=1152,
                extras=(dr1,), epilogue=lambda acc, d: (acc + ALPHA * d,), after=(round3[2][0],), name="mm_w_in_t")[0]
    gx, _, dg0, db0 = ln_bwd(xs, dh0, in_norm_g, name="ln0_bwd")

    kc_arr = jnp.concatenate([k_arr, c_arr])
    landed = scatter_wait([round1, round2, round3], gx, name="rs_scatter_wait")
    tags = ["mlp", "sq", "kv", "glu", "in"]
    pairs = [(pt, l2) for parts, lands2 in landed for pt, l2 in zip(parts, lands2)]
    halves = [add_partials(pt, l2, kc_arr, name="rs_add_partials_" + t) for (pt, l2), t in zip(pairs, tags)]
    g1, g2, gKV, gGLU, gIN = join_halves(halves, name="rs_join_halves")

    small_names = ["in_norm_g", "in_norm_b", "conv_db", "conv_norm_g", "conv_norm_b", "ssm_log_step", "ssm_lambda_re",
                   "ssm_lambda_im", "ssm_b_re", "ssm_b_im", "ssm_c_re", "ssm_c_im", "ssm_d", "ln1_g", "ln1_b",
                   "ln2_g", "ln2_b", "ln3_g", "ln3_b"]
    small_w = [in_norm_g, in_norm_b, conv_db, conv_norm_g, conv_norm_b, ssm_log_step, ssm_lambda_re, ssm_lambda_im,
               ssm_b_re, ssm_b_im, ssm_c_re, ssm_c_im, ssm_d, ln1_g, ln1_b, ln2_g, ln2_b, ln3_g, ln3_b]
    small_m = [m_in_norm_g, m_in_norm_b, m_conv_db, m_conv_norm_g, m_conv_norm_b, m_ssm_log_step, m_ssm_lambda_re,
               m_ssm_lambda_im, m_ssm_b_re, m_ssm_b_im, m_ssm_c_re, m_ssm_c_im, m_ssm_d, m_ln1_g, m_ln1_b, m_ln2_g,
               m_ln2_b, m_ln3_g, m_ln3_b]
    small_v = [v_in_norm_g, v_in_norm_b, v_conv_db, v_conv_norm_g, v_conv_norm_b, v_ssm_log_step, v_ssm_lambda_re,
               v_ssm_lambda_im, v_ssm_b_re, v_ssm_b_im, v_ssm_c_re, v_ssm_c_im, v_ssm_d, v_ln1_g, v_ln1_b, v_ln2_g,
               v_ln2_b, v_ln3_g, v_ln3_b]
    small_g = [dg0, db0, ddb, dng, dnb, g_lstep, g_lre, g_lim, g_bre, g_bim, g_cre, g_cim, g_d, dg1, db1, dg2, db2, dg3, db3]
    small_shapes = [w.shape for w in small_w]
    n_small_rows = _pack(small_w).shape[0]
    packed_g = _pack(small_g + [ddw, sq])
    _, summed = allgather_sum(packed_g, name="allreduce_small")
    small_rows = sum(-(-math.prod(s) // 128) for s in small_shapes)
    dw_rows = CONV_HALO * D // 128
    loss = 0.5 * summed[small_rows + dw_rows, 0] / D
    ddw_full = summed[small_rows:small_rows + dw_rows].reshape(CONV_HALO, D)
    g_dw = lax.dynamic_slice_in_dim(ddw_full, k_me * (D // N_CHIPS), D // N_CHIPS, axis=1)
    gs_packed = jnp.pad(summed[:small_rows], ((0, n_small_rows - small_rows), (0, 0)))

    res = {}

    def upd(nm, w, m, v, g_arr, row0=0):
        shp = w.shape
        w2, m2, v2 = (a.reshape(-1, shp[-1]) for a in (w, m, v))
        outs = adamw(w2, m2, v2, g_arr, row0, name="adamw_" + nm)
        res[nm] = tuple(o.reshape(shp) for o in outs)

    upd("w_conv_out", w_conv_out, m_w_conv_out, v_w_conv_out, g2, G2_CONV_OUT)
    upd("w_mix_out", w_mix_out, m_w_mix_out, v_w_mix_out, g2, G2_MIX_OUT)
    upd("xa_wq", xa_wq, m_xa_wq, v_xa_wq, g2, G2_WQ)
    upd("xa_wo", xa_wo, m_xa_wo, v_xa_wo, g2, G2_WO)
    upd("mlp_w_down", mlp_w_down, m_mlp_w_down, v_mlp_w_down, g1, G1_DOWN)
    upd("mlp_w_up", mlp_w_up, m_mlp_w_up, v_mlp_w_up, g1, G1_UP)
    upd("w_in", w_in, m_w_in, v_w_in, gIN)
    upd("xa_wkv", xa_wkv, m_xa_wkv, v_xa_wkv, gKV)
    upd("w_ssm_glu", w_ssm_glu, m_w_ssm_glu, v_w_ssm_glu, gGLU)
    pad_dw = lambda a: jnp.pad(a[0], ((0, CONV_HALO - CONV_K), (0, 0)))
    dw_outs = adamw(pad_dw(conv_dw), pad_dw(m_conv_dw), pad_dw(v_conv_dw), g_dw, 0, name="adamw_conv_dw")
    res["conv_dw"] = tuple(o[:CONV_K][None] for o in dw_outs)
    sm_outs = adamw(_pack(small_w), _pack(small_m), _pack(small_v), gs_packed, 0, name="adamw_small")
    sm_un = [_unpack(o, small_shapes) for o in sm_outs]
    for idx, nm in enumerate(small_names):
        res[nm] = tuple(sm_un[q][idx] for q in range(4))

    order = ["in_norm_g", "in_norm_b", "w_in", "conv_dw", "conv_db", "conv_norm_g", "conv_norm_b", "w_conv_out",
             "ssm_log_step", "ssm_lambda_re", "ssm_lambda_im", "ssm_b_re", "ssm_b_im", "ssm_c_re", "ssm_c_im", "ssm_d",
             "w_ssm_glu", "w_mix_out", "ln1_g", "ln1_b", "xa_wq", "xa_wkv", "xa_wo", "ln2_g", "ln2_b", "mlp_w_up",
             "mlp_w_down", "ln3_g", "ln3_b"]
    return (loss, gx[None], *[res[n][0] for n in order], *[res[n][1] for n in order],
            *[res[n][2] for n in order], *[res[n][3] for n in order])
```

```python
import functools
import math

import jax
import jax.numpy as jnp
from jax import lax
from jax.experimental import pallas as pl
from jax.experimental.pallas import tpu as pltpu

F32 = jnp.float32
BF16 = jnp.bfloat16
MESH = pl.DeviceIdType.MESH

D_MODEL = 1024
N_HEADS = 4
HEAD_DIM = D_MODEL // N_HEADS
CONV_K = 31
CONV_HALO = 32
D_SSM = 512
SSM_GROUPS = 32
SSM_GROUP = 16
SSM_STATE = 64
SSM_BLOCKS = 4
SSM_BLOCK_IN = D_SSM // SSM_BLOCKS
SSM_BLOCK_STATE = SSM_GROUPS * SSM_STATE // SSM_BLOCKS
D_FF = 4096
D_IN = 4608
LN_EPS = 1e-5
ALPHA = (2.0 * 1) ** 0.25
N_CHIPS = 4
N_DEV = 8
ADAM_LR, ADAM_B1, ADAM_B2, ADAM_EPS, ADAM_WD, ADAM_STEP = 0.001, 0.9, 0.999, 1e-08, 0.01, 10
VMEM_LIMIT_BYTES = 56 * 1024 * 1024


def _pick(dim, cands):
    for c in cands:
        if dim % c == 0:
            return c
    return dim


def _cparams(sem=None):
    return pltpu.CompilerParams(dimension_semantics=sem, vmem_limit_bytes=VMEM_LIMIT_BYTES)


def _sigmoid(x):
    return 1.0 / (1.0 + jnp.exp(-x))


_DIMS = {"nn": (((1,), (0,)), ((), ())), "nt": (((1,), (1,)), ((), ())), "tn": (((0,), (0,)), ((), ()))}


def matmul(a, b, *, mode, M, N, K, tm, tn, tk, a_spec, b_spec, out_specs, out_shapes, name,
           extras=(), extra_specs=(), epilogue=None, alias_buf=None, b_view=None, after=(), b_chunks=None):
    nk = K // tk
    ne = len(extras)
    no = len(out_shapes)
    na = (0 if alias_buf is None else 1) + len(after)
    dims = _DIMS[mode]

    def body(*refs):
        a_ref, b_ref = refs[0], refs[1]
        e_refs = refs[2:2 + ne]
        o_refs = refs[2 + ne + na:2 + ne + na + no]

        def finish(acc):
            outs = (acc,) if epilogue is None else epilogue(acc, *[r[...] for r in e_refs])
            for o, r in zip(outs, o_refs):
                r[...] = o.astype(r.dtype).reshape(r.shape)

        if b_chunks:
            kc = a_ref.shape[1] // b_chunks
            prod = None
            for q in range(b_chunks):
                part = lax.dot_general(a_ref[:, q * kc:(q + 1) * kc].astype(BF16), b_ref[q].astype(BF16), dims,
                                       preferred_element_type=F32)
                prod = part if prod is None else prod + part
        else:
            b_blk = b_ref[...] if b_view is None else b_ref[...].reshape(b_view)
            prod = lax.dot_general(a_ref[...].astype(BF16), b_blk.astype(BF16), dims, preferred_element_type=F32)
        if nk == 1:
            finish(prod)
        else:
            acc_ref = refs[-1]
            k = pl.program_id(2)

            @pl.when(k == 0)
            def _():
                acc_ref[...] = prod

            @pl.when(k > 0)
            def _():
                acc_ref[...] += prod

            @pl.when(k == nk - 1)
            def _():
                finish(acc_ref[...])

    in_specs = [pl.BlockSpec(*a_spec), pl.BlockSpec(*b_spec)] + [pl.BlockSpec(*s) for s in extra_specs]
    ins = [a, b, *extras]
    if alias_buf is not None:
        in_specs.append(pl.BlockSpec(memory_space=pl.ANY))
        ins.append(alias_buf)
    for dep in after:
        in_specs.append(pl.BlockSpec(memory_space=pl.ANY))
        ins.append(dep)
    res = pl.pallas_call(
        body,
        grid=(M // tm, N // tn, nk),
        in_specs=in_specs,
        out_specs=[pl.BlockSpec(*s) for s in out_specs],
        out_shape=out_shapes,
        scratch_shapes=[] if nk == 1 else [pltpu.VMEM((tm, tn), F32)],
        input_output_aliases={2 + ne: 0} if alias_buf is not None else {},
        compiler_params=_cparams(("parallel", "parallel", "arbitrary")),
        name=name,
    )(*ins)
    return res


def _mn(tm, tn):
    return ((tm, tn), lambda i, j, k: (i, j))


def mm_nn(a, b_arr, b_spec, N, *, name, tm=None, tn, tk, out_dtype=F32, extras=(), epilogue=None, out_dtypes=None,
          b_view=None, extra_specs=None):
    M, K = a.shape
    tm = tm or _pick(M, [1024, 512, 256, 128])
    dts = out_dtypes or [out_dtype]
    return matmul(a, b_arr, mode="nn", M=M, N=N, K=K, tm=tm, tn=tn, tk=tk,
                  a_spec=((tm, tk), lambda i, j, k: (i, k)), b_spec=b_spec, b_view=b_view,
                  out_specs=[_mn(tm, tn)] * len(dts), out_shapes=[jax.ShapeDtypeStruct((M, N), d) for d in dts],
                  extras=extras, extra_specs=extra_specs or [_mn(tm, tn)] * len(extras), epilogue=epilogue, name=name)


def mm_nt(a, b_arr, b_spec, N, *, name, tm=None, tn, tk, out_dtype=F32, extras=(), epilogue=None, out_dtypes=None,
          b_view=None, after=(), b_chunks=None):
    M, K = a.shape
    tm = tm or _pick(M, [1024, 512, 256, 128])
    dts = out_dtypes or [out_dtype]
    return matmul(a, b_arr, mode="nt", M=M, N=N, K=K, tm=tm, tn=tn, tk=tk, after=after, b_chunks=b_chunks,
                  a_spec=((tm, tk), lambda i, j, k: (i, k)), b_spec=b_spec, b_view=b_view,
                  out_specs=[_mn(tm, tn)] * len(dts), out_shapes=[jax.ShapeDtypeStruct((M, N), d) for d in dts],
                  extras=extras, extra_specs=[_mn(tm, tn)] * len(extras), epilogue=epilogue, name=name)


def mm_tn(a, b, *, name, tm, tn, tk=None, out_spec, out_shape, out_buf=None):
    K, M = a.shape
    N = b.shape[1]
    tk = tk or _pick(K, [2048, 1024, 512, 256, 128])
    return matmul(a, b, mode="tn", M=M, N=N, K=K, tm=tm, tn=tn, tk=tk,
                  a_spec=((tk, tm), lambda i, j, k: (k, i)), b_spec=((tk, tn), lambda i, j, k: (k, j)),
                  out_specs=[out_spec], out_shapes=[out_shape], alias_buf=out_buf, name=name)[0]


def _rows(tc, w, cb=0):
    return pl.BlockSpec((tc, w), lambda i: (i, cb))


def _const(shape):
    return pl.BlockSpec(shape, lambda i: tuple([0] * len(shape)))


def _ln_stats(r):
    mu = jnp.mean(r, axis=-1, keepdims=True)
    xc = r - mu
    var = jnp.mean(xc * xc, axis=-1, keepdims=True)
    rstd = lax.rsqrt(var + LN_EPS)
    return xc * rstd, rstd


def _rowsum8(v):
    tc, w = v.shape
    return jnp.sum(v.reshape(tc // 8, 8, w), axis=0)


def ln_fwd(x, g, b, *, name, res=None):
    T, D = x.shape
    tc = _pick(T, [512, 256, 128])
    has_res = res is not None

    def body(*refs):
        if has_res:
            x_ref, res_ref, g_ref, b_ref, r_ref, h_ref, hb_ref = refs
            r = ALPHA * res_ref[...] + x_ref[...]
            r_ref[...] = r
        else:
            x_ref, g_ref, b_ref, h_ref, hb_ref = refs
            r = x_ref[...]
        xhat, _ = _ln_stats(r)
        y = xhat * g_ref[...] + b_ref[...]
        h_ref[...] = y
        hb_ref[...] = y.astype(BF16)

    ins = [x] + ([res] if has_res else []) + [g.reshape(1, D), b.reshape(1, D)]
    in_specs = [_rows(tc, D)] * (2 if has_res else 1) + [_const((1, D))] * 2
    n_out = 3 if has_res else 2
    outs = pl.pallas_call(
        body, grid=(T // tc,), in_specs=in_specs, out_specs=[_rows(tc, D)] * n_out,
        out_shape=[jax.ShapeDtypeStruct((T, D), F32)] * (n_out - 1) + [jax.ShapeDtypeStruct((T, D), BF16)],
        compiler_params=_cparams(("arbitrary",)), name=name)(*ins)
    if has_res:
        return outs
    return (x,) + tuple(outs)


def ln_bwd(r, dy, g, *, name):
    T, D = r.shape
    tc = _pick(T, [512, 256, 128])
    nt = T // tc

    def body(r_ref, dy_ref, g_ref, dr_ref, drb_ref, dg_ref, db_ref, accg, accb):
        i = pl.program_id(0)

        @pl.when(i == 0)
        def _():
            accg[...] = jnp.zeros_like(accg)
            accb[...] = jnp.zeros_like(accb)

        xhat, rstd = _ln_stats(r_ref[...])
        dy = dy_ref[...]
        dxh = dy * g_ref[...]
        m1 = jnp.mean(dxh, axis=-1, keepdims=True)
        m2 = jnp.mean(dxh * xhat, axis=-1, keepdims=True)
        dr = rstd * (dxh - m1 - xhat * m2)
        dr_ref[...] = dr
        drb_ref[...] = dr.astype(BF16)
        accg[...] += _rowsum8(dy * xhat)
        accb[...] += _rowsum8(dy)

        @pl.when(i == nt - 1)
        def _():
            dg_ref[...] = jnp.sum(accg[...], axis=0, keepdims=True)
            db_ref[...] = jnp.sum(accb[...], axis=0, keepdims=True)

    return pl.pallas_call(
        body, grid=(nt,), in_specs=[_rows(tc, D), _rows(tc, D), _const((1, D))],
        out_specs=[_rows(tc, D), _rows(tc, D), _const((1, D)), _const((1, D))],
        out_shape=[jax.ShapeDtypeStruct((T, D), F32), jax.ShapeDtypeStruct((T, D), BF16),
                   jax.ShapeDtypeStruct((1, D), F32), jax.ShapeDtypeStruct((1, D), F32)],
        scratch_shapes=[pltpu.VMEM((8, D), F32), pltpu.VMEM((8, D), F32)],
        compiler_params=_cparams(("arbitrary",)), name=name)(r, dy, g.reshape(1, D))


def ln_loss_bwd(x, res, g, b, target, *, name):
    T, D = x.shape
    tc = _pick(T, [512, 256, 128])
    nt = T // tc

    def body(x_ref, res_ref, g_ref, b_ref, t_ref, dr_ref, drb_ref, dg_ref, db_ref, loss_ref, accg, accb, accl):
        i = pl.program_id(0)

        @pl.when(i == 0)
        def _():
            accg[...] = jnp.zeros_like(accg)
            accb[...] = jnp.zeros_like(accb)
            accl[...] = jnp.zeros_like(accl)

        r = ALPHA * res_ref[...] + x_ref[...]
        xhat, rstd = _ln_stats(r)
        e = xhat * g_ref[...] + b_ref[...] - t_ref[...]
        dy = e * (1.0 / D)
        dxh = dy * g_ref[...]
        m1 = jnp.mean(dxh, axis=-1, keepdims=True)
        m2 = jnp.mean(dxh * xhat, axis=-1, keepdims=True)
        dr = rstd * (dxh - m1 - xhat * m2)
        dr_ref[...] = dr
        drb_ref[...] = dr.astype(BF16)
        accg[...] += _rowsum8(dy * xhat)
        accb[...] += _rowsum8(dy)
        accl[...] += _rowsum8(e * e)

        @pl.when(i == nt - 1)
        def _():
            dg_ref[...] = jnp.sum(accg[...], axis=0, keepdims=True)
            db_ref[...] = jnp.sum(accb[...], axis=0, keepdims=True)
            s = jnp.sum(jnp.sum(accl[...], axis=0, keepdims=True), axis=1, keepdims=True)
            loss_ref[...] = jnp.broadcast_to(s, (1, 128))

    return pl.pallas_call(
        body, grid=(nt,), in_specs=[_rows(tc, D), _rows(tc, D), _const((1, D)), _const((1, D)), _rows(tc, D)],
        out_specs=[_rows(tc, D), _rows(tc, D), _const((1, D)), _const((1, D)), _const((1, 128))],
        out_shape=[jax.ShapeDtypeStruct((T, D), F32), jax.ShapeDtypeStruct((T, D), BF16),
                   jax.ShapeDtypeStruct((1, D), F32), jax.ShapeDtypeStruct((1, D), F32), jax.ShapeDtypeStruct((1, 128), F32)],
        scratch_shapes=[pltpu.VMEM((8, D), F32)] * 3,
        compiler_params=_cparams(("arbitrary",)), name=name)(x, res, g.reshape(1, D), b.reshape(1, D), target)


def _halo_prev(tc):
    per = tc // CONV_HALO
    return lambda i: jnp.maximum(i * per - 1, 0)


CONV_ROWS = 32
CONV_TAP_GROUP = 4
CONV_TILE_UNROLL = 4


def _fill_shifts(S, nrows):
    for b in range(1, 8):
        S[b, 0:nrows - 8, :] = S[0, b:b + nrows - 8, :]


def _tap_sum(S, w_ref, offs, r0, nrows):
    acc = None
    for k, o in enumerate(offs):
        a, b = divmod(o, 8)
        term = w_ref[k:k + 1, :] * S[b, pl.ds(pl.multiple_of(r0 + 8 * a, 8), nrows), :]
        acc = term if acc is None else acc + term
    return acc


def conv_fwd(p, dw, db, ng, nb, *, name):
    T = p.shape[0]
    D = D_MODEL
    tc = _pick(T, [256, 128])
    prev = _halo_prev(tc)
    off = CONV_HALO - (CONV_K - 1)
    offs = [off + k for k in range(CONV_K)]

    def body(val_ref, gate_ref, valp_ref, gatep_ref, dw_ref, db_ref, ng_ref, nb_ref, c_ref, act_ref, S):
        i = pl.program_id(0)
        u_prev = valp_ref[...] * _sigmoid(gatep_ref[...])
        S[0, 0:CONV_HALO, :] = jnp.where(i > 0, u_prev, 0.0)
        S[0, CONV_HALO:CONV_HALO + tc, :] = val_ref[...] * _sigmoid(gate_ref[...])
        _fill_shifts(S, CONV_HALO + tc)

        def rows(j, carry):
            r0 = pl.multiple_of(j * CONV_ROWS, CONV_ROWS)
            c_ref[pl.ds(r0, CONV_ROWS), :] = _tap_sum(S, dw_ref, offs, r0, CONV_ROWS) + db_ref[...]
            return carry

        lax.fori_loop(0, tc // CONV_ROWS, rows, 0)
        c = c_ref[...]
        xhat, _ = _ln_stats(c)
        cn = xhat * ng_ref[...] + nb_ref[...]
        act_ref[...] = (cn * _sigmoid(cn)).astype(BF16)

    return pl.pallas_call(
        body, grid=(T // tc,),
        in_specs=[_rows(tc, D, 0), _rows(tc, D, 1),
                  pl.BlockSpec((CONV_HALO, D), lambda i: (prev(i), 0)), pl.BlockSpec((CONV_HALO, D), lambda i: (prev(i), 1)),
                  _const((CONV_HALO, D)), _const((1, D)), _const((1, D)), _const((1, D))],
        out_specs=[_rows(tc, D), _rows(tc, D)],
        out_shape=[jax.ShapeDtypeStruct((T, D), F32), jax.ShapeDtypeStruct((T, D), BF16)],
        scratch_shapes=[pltpu.VMEM((8, CONV_HALO + tc, D), F32)],
        compiler_params=_cparams(("arbitrary",)), name=name)(p, p, p, p, dw, db, ng, nb)


def conv_bwd_norm(dact, c_pre, ng, nb, after, *, name):
    T, D = c_pre.shape
    tc = _pick(T, [512, 256, 128])
    nt = T // tc

    def body(da_ref, c_ref, ng_ref, nb_ref, after_ref, dc_ref, dng_ref, dnb_ref, ddb_ref, accg, accb, accd):
        i = pl.program_id(0)

        @pl.when(i == 0)
        def _():
            accg[...] = jnp.zeros_like(accg)
            accb[...] = jnp.zeros_like(accb)
            accd[...] = jnp.zeros_like(accd)

        xhat, rstd = _ln_stats(c_ref[...])
        cn = xhat * ng_ref[...] + nb_ref[...]
        s = _sigmoid(cn)
        dcn = da_ref[...] * (s * (1.0 + cn * (1.0 - s)))
        dxh = dcn * ng_ref[...]
        m1 = jnp.mean(dxh, axis=-1, keepdims=True)
        m2 = jnp.mean(dxh * xhat, axis=-1, keepdims=True)
        dc = rstd * (dxh - m1 - xhat * m2)
        dc_ref[...] = dc
        accg[...] += _rowsum8(dcn * xhat)
        accb[...] += _rowsum8(dcn)
        accd[...] += _rowsum8(dc)

        @pl.when(i == nt - 1)
        def _():
            dng_ref[...] = jnp.sum(accg[...], axis=0, keepdims=True)
            dnb_ref[...] = jnp.sum(accb[...], axis=0, keepdims=True)
            ddb_ref[...] = jnp.sum(accd[...], axis=0, keepdims=True)

    vec = jax.ShapeDtypeStruct((1, D), F32)
    return pl.pallas_call(
        body, grid=(nt,), in_specs=[_rows(tc, D), _rows(tc, D), _const((1, D)), _const((1, D)), ANY],
        out_specs=[_rows(tc, D), _const((1, D)), _const((1, D)), _const((1, D))],
        out_shape=[jax.ShapeDtypeStruct((T, D), F32), vec, vec, vec],
        scratch_shapes=[pltpu.VMEM((8, D), F32)] * 3,
        compiler_params=_cparams(("arbitrary",)), name=name)(dact, c_pre, ng, nb, after)


def conv_bwd_taps(dc, p, dw, *, name):
    T, D = dc.shape
    tc = _pick(T, [256, 128])
    nt = T // tc
    per = tc // CONV_HALO
    prev = _halo_prev(tc)
    last_halo = T // CONV_HALO - 1
    nxt = lambda i: jnp.minimum((i + 1) * per, last_halo)
    off = CONV_HALO - (CONV_K - 1)

    def body(dc_ref, dcn_ref, val_ref, gate_ref, valp_ref, gatep_ref, dw_ref, dvg_ref, ddw_ref, ext_u, ext_d, acc):
        i = pl.program_id(0)

        @pl.when(i == 0)
        def _():
            acc[...] = jnp.zeros_like(acc)

        u_prev = valp_ref[...] * _sigmoid(gatep_ref[...])
        ext_u[0, 0:CONV_HALO, :] = jnp.where(i > 0, u_prev, 0.0)
        ext_u[0, CONV_HALO:CONV_HALO + tc, :] = val_ref[...] * _sigmoid(gate_ref[...])
        ext_d[0, 0:tc, :] = dc_ref[...]
        ext_d[0, tc:tc + CONV_HALO, :] = jnp.where(i < nt - 1, dcn_ref[...], 0.0)
        _fill_shifts(ext_u, CONV_HALO + tc)
        _fill_shifts(ext_d, CONV_HALO + tc)

        def rows(j, carry):
            r0 = pl.multiple_of(j * CONV_ROWS, CONV_ROWS)
            sl = pl.ds(r0, CONV_ROWS)
            du = _tap_sum(ext_d, dw_ref, [CONV_K - 1 - k for k in range(CONV_K)], r0, CONV_ROWS)
            sg = _sigmoid(gate_ref[sl, :])
            dvg_ref[sl, 0:D] = (du * sg).astype(BF16)
            dvg_ref[sl, D:2 * D] = (du * val_ref[sl, :] * sg * (1.0 - sg)).astype(BF16)
            return carry

        lax.fori_loop(0, tc // CONV_ROWS, rows, 0)

        for k0 in range(0, CONV_K, CONV_TAP_GROUP):
            ks = list(range(k0, min(k0 + CONV_TAP_GROUP, CONV_K)))

            def taps(j, accs, ks=ks):
                out = list(accs)
                for t in range(CONV_TILE_UNROLL):
                    r0 = pl.multiple_of((j * CONV_TILE_UNROLL + t) * 8, 8)
                    dct = dc_ref[pl.ds(r0, 8), :]
                    for q, k in enumerate(ks):
                        a, b = divmod(off + k, 8)
                        out[q] = out[q] + dct * ext_u[b, pl.ds(pl.multiple_of(r0 + 8 * a, 8), 8), :]
                return tuple(out)

            accs = lax.fori_loop(0, tc // (8 * CONV_TILE_UNROLL), taps, tuple(jnp.zeros((8, D), F32) for _ in ks))
            for k, a_k in zip(ks, accs):
                acc[k] += a_k

        @pl.when(i == nt - 1)
        def _():
            ddw_ref[...] = jnp.zeros_like(ddw_ref)
            for k in range(CONV_K):
                ddw_ref[k:k + 1, :] = jnp.sum(acc[k], axis=0, keepdims=True)

    return pl.pallas_call(
        body, grid=(nt,),
        in_specs=[_rows(tc, D), pl.BlockSpec((CONV_HALO, D), lambda i: (nxt(i), 0)),
                  _rows(tc, D, 0), _rows(tc, D, 1),
                  pl.BlockSpec((CONV_HALO, D), lambda i: (prev(i), 0)), pl.BlockSpec((CONV_HALO, D), lambda i: (prev(i), 1)),
                  _const((CONV_HALO, D))],
        out_specs=[_rows(tc, 2 * D), _const((CONV_HALO, D))],
        out_shape=[jax.ShapeDtypeStruct((T, 2 * D), BF16), jax.ShapeDtypeStruct((CONV_HALO, D), F32)],
        scratch_shapes=[pltpu.VMEM((8, CONV_HALO + tc, D), F32), pltpu.VMEM((8, CONV_HALO + tc, D), F32),
                        pltpu.VMEM((CONV_K, 8, D), F32)],
        compiler_params=_cparams(("arbitrary",)), name=name)(dc, dc, p, p, p, p, dw)


GATE_A0 = (2 * D_MODEL + D_SSM) // 512
GATE_B0 = GATE_A0 + 2


def merge_fwd(p, ya, z, *, name):
    T = p.shape[0]
    D = D_MODEL
    tc = _pick(T, [512, 256, 128])
    W = 512

    def body(ga_ref, gb_ref, ya_ref, z1_ref, z2_ref, o_ref):
        yb = z1_ref[...] * _sigmoid(z2_ref[...])
        o_ref[...] = (_sigmoid(ga_ref[...]) * ya_ref[...] + _sigmoid(gb_ref[...]) * yb).astype(BF16)

    return pl.pallas_call(
        body, grid=(T // tc, D // W),
        in_specs=[pl.BlockSpec((tc, W), lambda i, j: (i, GATE_A0 + j)), pl.BlockSpec((tc, W), lambda i, j: (i, GATE_B0 + j)),
                  pl.BlockSpec((tc, W), lambda i, j: (i, j)), pl.BlockSpec((tc, W), lambda i, j: (i, j)),
                  pl.BlockSpec((tc, W), lambda i, j: (i, D // W + j))],
        out_specs=pl.BlockSpec((tc, W), lambda i, j: (i, j)),
        out_shape=jax.ShapeDtypeStruct((T, D), BF16),
        compiler_params=_cparams(("arbitrary", "arbitrary")), name=name)(p, p, ya, z, z)


def merge_bwd(dm, p, ya, z, *, name):
    T = p.shape[0]
    D = D_MODEL
    tc = _pick(T, [512, 256, 128])
    W = 512
    nb = D // W

    def body(dm_ref, ga_ref, gb_ref, ya_ref, z1_ref, z2_ref, dya_ref, dga_ref, dgb_ref, dz1_ref, dz2_ref):
        dm = dm_ref[...]
        sa = _sigmoid(ga_ref[...])
        sb = _sigmoid(gb_ref[...])
        s2 = _sigmoid(z2_ref[...])
        z1 = z1_ref[...]
        yb = z1 * s2
        dya_ref[...] = (dm * sa).astype(BF16)
        dga_ref[...] = (dm * ya_ref[...] * sa * (1.0 - sa)).astype(BF16)
        dgb_ref[...] = (dm * yb * sb * (1.0 - sb)).astype(BF16)
        dyb = dm * sb
        dz1_ref[...] = (dyb * s2).astype(BF16)
        dz2_ref[...] = (dyb * z1 * s2 * (1.0 - s2)).astype(BF16)

    blk = lambda off: pl.BlockSpec((tc, W), lambda i, j: (i, off + j))
    dya, dga, dgb, dz1, dz2 = pl.pallas_call(
        body, grid=(T // tc, nb),
        in_specs=[blk(0), blk(GATE_A0), blk(GATE_B0), blk(0), blk(0), blk(nb)],
        out_specs=[blk(0)] * 5,
        out_shape=[jax.ShapeDtypeStruct((T, D), BF16)] * 5,
        compiler_params=_cparams(("arbitrary", "arbitrary")), name=name)(dm, p, p, ya, z, z)
    return dya, dga, dgb, dz1, dz2


def _scan_block(src_r, src_i, dst_r, dst_i, car_r, car_i, pw_r, pw_i, cw_r, cw_i, ntiles, reverse, extra=None):
    W = src_r.shape[1]
    rows = lax.broadcasted_iota(jnp.int32, (8, W), 0)
    steps = []
    for d, pr in ((1, 0), (2, 1), (4, 3)):
        valid = rows < 8 - d if reverse else rows >= d
        steps.append((d, jnp.where(valid, jnp.broadcast_to(pw_r[pr:pr + 1, :], (8, W)), 0.0),
                      jnp.where(valid, jnp.broadcast_to(pw_i[pr:pr + 1, :], (8, W)), 0.0)))
    cw_r, cw_i = cw_r[...], cw_i[...]

    def tile(jj, carry):
        j = ntiles - 1 - jj if reverse else jj
        sl = pl.ds(pl.multiple_of(j * 8, 8), 8)
        xr, xi = src_r[sl, :], src_i[sl, :]
        for d, lr, li in steps:
            sr = pltpu.roll(xr, 8 - d if reverse else d, 0)
            si = pltpu.roll(xi, 8 - d if reverse else d, 0)
            xr, xi = xr + lr * sr - li * si, xi + lr * si + li * sr
        cr, ci = car_r[...], car_i[...]
        xr, xi = xr + cw_r * cr - cw_i * ci, xi + cw_r * ci + cw_i * cr
        dst_r[sl, :] = xr
        dst_i[sl, :] = xi
        edge = 0 if reverse else 7
        car_r[...] = jnp.broadcast_to(xr[edge:edge + 1, :], (8, W))
        car_i[...] = jnp.broadcast_to(xi[edge:edge + 1, :], (8, W))
        if extra is not None:
            carry = extra(j, xr, xi, carry)
        return carry

    return tile


def ssm_fwd(p, Br, Bi, Cr, Ci, pw_r, pw_i, dvec, *, name):
    T = p.shape[0]
    tt = _pick(T, [512, 256, 128])
    nt = T // tt
    WI, WS = SSM_BLOCK_IN, SSM_BLOCK_STATE
    u0 = 2 * D_MODEL // WI

    def body(u_ref, br_ref, bi_ref, cr_ref, ci_ref, pwr_ref, pwi_ref, d_ref, xr_ref, xi_ref, y_ref, bur, bui, car_r, car_i):
        i = pl.program_id(1)

        @pl.when(i == 0)
        def _():
            car_r[...] = jnp.zeros_like(car_r)
            car_i[...] = jnp.zeros_like(car_i)

        u = u_ref[...]
        ub = u.astype(BF16)
        bur[...] = jnp.dot(ub, br_ref[...].astype(BF16), preferred_element_type=F32)
        bui[...] = jnp.dot(ub, bi_ref[...].astype(BF16), preferred_element_type=F32)
        tile = _scan_block(bur, bui, xr_ref, xi_ref, car_r, car_i, pwr_ref, pwi_ref, pwr_ref, pwi_ref, tt // 8, False)
        lax.fori_loop(0, tt // 8, tile, 0)
        y = (jnp.dot(xr_ref[...].astype(BF16), cr_ref[...].astype(BF16), preferred_element_type=F32)
             - jnp.dot(xi_ref[...].astype(BF16), ci_ref[...].astype(BF16), preferred_element_type=F32)
             + d_ref[...] * u)
        y_ref[...] = y.astype(BF16)

    return pl.pallas_call(
        body, grid=(SSM_BLOCKS, nt),
        in_specs=[pl.BlockSpec((tt, WI), lambda b, i: (i, u0 + b)),
                  pl.BlockSpec((None, WI, WS), lambda b, i: (b, 0, 0)), pl.BlockSpec((None, WI, WS), lambda b, i: (b, 0, 0)),
                  pl.BlockSpec((None, WS, WI), lambda b, i: (b, 0, 0)), pl.BlockSpec((None, WS, WI), lambda b, i: (b, 0, 0)),
                  pl.BlockSpec((8, WS), lambda b, i: (0, b)), pl.BlockSpec((8, WS), lambda b, i: (0, b)),
                  pl.BlockSpec((1, WI), lambda b, i: (0, b))],
        out_specs=[pl.BlockSpec((tt, WS), lambda b, i: (i, b)), pl.BlockSpec((tt, WS), lambda b, i: (i, b)),
                   pl.BlockSpec((tt, WI), lambda b, i: (i, b))],
        out_shape=[jax.ShapeDtypeStruct((T, SSM_BLOCKS * WS), F32)] * 2 + [jax.ShapeDtypeStruct((T, D_SSM), BF16)],
        scratch_shapes=[pltpu.VMEM((tt, WS), F32), pltpu.VMEM((tt, WS), F32), pltpu.VMEM((8, WS), F32), pltpu.VMEM((8, WS), F32)],
        compiler_params=_cparams(("arbitrary", "arbitrary")), name=name)(p, Br, Bi, Cr, Ci, pw_r, pw_i, dvec)


def ssm_bwd(dy, p, xr, xi, Br, Bi, Cr, Ci, pwc_r, pwc_i, cwc_r, cwc_i, dvec, *, name):
    T = p.shape[0]
    tt = _pick(T, [512, 256, 128])
    nt = T // tt
    WI, WS = SSM_BLOCK_IN, SSM_BLOCK_STATE
    u0 = 2 * D_MODEL // WI
    tb = lambda i: nt - 1 - i
    xprev = lambda i: jnp.maximum(tb(i) * (tt // 8) - 1, 0)
    tn_dims = _DIMS["tn"]
    nt_dims = _DIMS["nt"]

    def body(dy_ref, u_ref, xr_ref, xi_ref, xpr_ref, xpi_ref, br_ref, bi_ref, cr_ref, ci_ref, pwr_ref, pwi_ref,
             cwr_ref, cwi_ref, d_ref,
             du_ref, dbr_ref, dbi_ref, dcr_ref, dci_ref, dar_ref, dai_ref, dd_ref,
             gr, gi, ext_r, ext_i, car_r, car_i):
        i = pl.program_id(1)

        @pl.when(i == 0)
        def _():
            car_r[...] = jnp.zeros_like(car_r)
            car_i[...] = jnp.zeros_like(car_i)
            dbr_ref[...] = jnp.zeros_like(dbr_ref)
            dbi_ref[...] = jnp.zeros_like(dbi_ref)
            dcr_ref[...] = jnp.zeros_like(dcr_ref)
            dci_ref[...] = jnp.zeros_like(dci_ref)
            dar_ref[...] = jnp.zeros_like(dar_ref)
            dai_ref[...] = jnp.zeros_like(dai_ref)
            dd_ref[...] = jnp.zeros_like(dd_ref)

        dy = dy_ref[...]
        dyb = dy.astype(BF16)
        u = u_ref[...]
        ub = u.astype(BF16)
        gr[...] = lax.dot_general(dyb, cr_ref[...].astype(BF16), nt_dims, preferred_element_type=F32)
        gi[...] = -lax.dot_general(dyb, ci_ref[...].astype(BF16), nt_dims, preferred_element_type=F32)
        first = tb(i) == 0
        ext_r[0:8, :] = jnp.where(first, 0.0, xpr_ref[...])
        ext_i[0:8, :] = jnp.where(first, 0.0, xpi_ref[...])
        ext_r[8:8 + tt, :] = xr_ref[...]
        ext_i[8:8 + tt, :] = xi_ref[...]
        rows = lax.broadcasted_iota(jnp.int32, (8, WS), 0)

        def lam_grad(j, g_r, g_i, carry):
            a_r, a_i = carry
            cur = pl.ds(pl.multiple_of(j * 8 + 8, 8), 8)
            prv = pl.ds(pl.multiple_of(j * 8, 8), 8)
            xc_r, xc_i = ext_r[cur, :], ext_i[cur, :]
            xl_r, xl_i = ext_r[prv, :], ext_i[prv, :]
            xp_r = jnp.where(rows == 0, jnp.broadcast_to(xl_r[7:8, :], (8, WS)), pltpu.roll(xc_r, 1, 0))
            xp_i = jnp.where(rows == 0, jnp.broadcast_to(xl_i[7:8, :], (8, WS)), pltpu.roll(xc_i, 1, 0))
            return (a_r + g_r * xp_r + g_i * xp_i, a_i + g_i * xp_r - g_r * xp_i)

        tile = _scan_block(gr, gi, gr, gi, car_r, car_i, pwr_ref, pwi_ref, cwr_ref, cwi_ref, tt // 8, True, extra=lam_grad)
        z8 = jnp.zeros((8, WS), F32)
        a_r, a_i = lax.fori_loop(0, tt // 8, tile, (z8, z8))
        dar_ref[...] += a_r
        dai_ref[...] += a_i
        grb = gr[...].astype(BF16)
        gib = gi[...].astype(BF16)
        dbr_ref[...] += lax.dot_general(ub, grb, tn_dims, preferred_element_type=F32)
        dbi_ref[...] += lax.dot_general(ub, gib, tn_dims, preferred_element_type=F32)
        dcr_ref[...] += lax.dot_general(xr_ref[...].astype(BF16), dyb, tn_dims, preferred_element_type=F32)
        dci_ref[...] -= lax.dot_general(xi_ref[...].astype(BF16), dyb, tn_dims, preferred_element_type=F32)
        du = (lax.dot_general(grb, br_ref[...].astype(BF16), nt_dims, preferred_element_type=F32)
              + lax.dot_general(gib, bi_ref[...].astype(BF16), nt_dims, preferred_element_type=F32)
              + d_ref[...] * dy)
        du_ref[...] = du.astype(BF16)
        dd_ref[...] += _rowsum8(dy * u)

    wspec = lambda shp: pl.BlockSpec((None,) + shp, lambda b, i: (b, 0, 0))
    return pl.pallas_call(
        body, grid=(SSM_BLOCKS, nt),
        in_specs=[pl.BlockSpec((tt, WI), lambda b, i: (tb(i), b)),
                  pl.BlockSpec((tt, WI), lambda b, i: (tb(i), u0 + b)),
                  pl.BlockSpec((tt, WS), lambda b, i: (tb(i), b)), pl.BlockSpec((tt, WS), lambda b, i: (tb(i), b)),
                  pl.BlockSpec((8, WS), lambda b, i: (xprev(i), b)), pl.BlockSpec((8, WS), lambda b, i: (xprev(i), b)),
                  wspec((WI, WS)), wspec((WI, WS)), wspec((WS, WI)), wspec((WS, WI)),
                  pl.BlockSpec((8, WS), lambda b, i: (0, b)), pl.BlockSpec((8, WS), lambda b, i: (0, b)),
                  pl.BlockSpec((8, WS), lambda b, i: (0, b)), pl.BlockSpec((8, WS), lambda b, i: (0, b)),
                  pl.BlockSpec((1, WI), lambda b, i: (0, b))],
        out_specs=[pl.BlockSpec((tt, WI), lambda b, i: (tb(i), b)),
                   wspec((WI, WS)), wspec((WI, WS)), wspec((WS, WI)), wspec((WS, WI)),
                   pl.BlockSpec((8, WS), lambda b, i: (0, b)), pl.BlockSpec((8, WS), lambda b, i: (0, b)),
                   pl.BlockSpec((8, WI), lambda b, i: (0, b))],
        out_shape=[jax.ShapeDtypeStruct((T, D_SSM), BF16),
                   jax.ShapeDtypeStruct((SSM_BLOCKS, WI, WS), F32), jax.ShapeDtypeStruct((SSM_BLOCKS, WI, WS), F32),
                   jax.ShapeDtypeStruct((SSM_BLOCKS, WS, WI), F32), jax.ShapeDtypeStruct((SSM_BLOCKS, WS, WI), F32),
                   jax.ShapeDtypeStruct((8, SSM_BLOCKS * WS), F32), jax.ShapeDtypeStruct((8, SSM_BLOCKS * WS), F32),
                   jax.ShapeDtypeStruct((8, D_SSM), F32)],
        scratch_shapes=[pltpu.VMEM((tt, WS), F32), pltpu.VMEM((tt, WS), F32),
                        pltpu.VMEM((tt + 8, WS), F32), pltpu.VMEM((tt + 8, WS), F32),
                        pltpu.VMEM((8, WS), F32), pltpu.VMEM((8, WS), F32)],
        compiler_params=_cparams(("arbitrary", "arbitrary")), name=name,
    )(dy, p, xr, xi, xr, xi, Br, Bi, Cr, Ci, pwc_r, pwc_i, cwc_r, cwc_i, dvec)


def _ssm_discretise(log_step, lam_re, lam_im, b_re, b_im):
    step = jnp.exp(log_step)[:, None]
    mag = jnp.exp(lam_re * step)
    ar = mag * jnp.cos(lam_im * step)
    ai = mag * jnp.sin(lam_im * step)
    den = lam_re * lam_re + lam_im * lam_im
    nr = ar - 1.0
    cr = (nr * lam_re + ai * lam_im) / den
    ci = (ai * lam_re - nr * lam_im) / den
    bbr = cr[..., None] * b_re - ci[..., None] * b_im
    bbi = cr[..., None] * b_im + ci[..., None] * b_re
    return ar, ai, bbr, bbi


def _blockdiag_in(bb):
    t = jnp.transpose(bb, (0, 2, 1)).reshape(SSM_BLOCKS, 8, SSM_GROUP, SSM_STATE)
    eye = jnp.eye(8, dtype=bb.dtype)
    return (t[:, :, :, None, :] * eye[None, :, None, :, None]).reshape(SSM_BLOCKS, SSM_BLOCK_IN, SSM_BLOCK_STATE)


def _blockdiag_out(cc):
    t = jnp.transpose(cc, (0, 2, 1)).reshape(SSM_BLOCKS, 8, SSM_STATE, SSM_GROUP)
    eye = jnp.eye(8, dtype=cc.dtype)
    return (t[:, :, :, None, :] * eye[None, :, None, :, None]).reshape(SSM_BLOCKS, SSM_BLOCK_STATE, SSM_BLOCK_IN)


def _diag_in(d):
    t = d.reshape(SSM_BLOCKS, 8, SSM_GROUP, 8, SSM_STATE)
    t = jnp.einsum("bghgp->bghp", t).reshape(SSM_GROUPS, SSM_GROUP, SSM_STATE)
    return jnp.transpose(t, (0, 2, 1))


def _diag_out(d):
    t = d.reshape(SSM_BLOCKS, 8, SSM_STATE, 8, SSM_GROUP)
    t = jnp.einsum("bgpgh->bgph", t).reshape(SSM_GROUPS, SSM_STATE, SSM_GROUP)
    return jnp.transpose(t, (0, 2, 1))


def _powers(ar, ai):
    rs, is_ = [ar], [ai]
    for _ in range(7):
        r, i = rs[-1], is_[-1]
        rs.append(r * ar - i * ai)
        is_.append(r * ai + i * ar)
    return jnp.stack(rs), jnp.stack(is_), jnp.stack(rs[::-1]), jnp.stack(is_[::-1])


def attn_fwd(q, kv, *, name):
    T, D = q.shape
    nm = kv.shape[0]
    tq = _pick(T, [512, 256, 128])
    scale = HEAD_DIM ** -0.5

    def body(q_ref, k_ref, v_ref, o_ref):
        for h in range(N_HEADS):
            sl = slice(h * HEAD_DIM, (h + 1) * HEAD_DIM)
            s = lax.dot_general(q_ref[:, sl], k_ref[:, sl].astype(BF16), _DIMS["nt"], preferred_element_type=F32) * scale
            e = jnp.exp(s - jnp.max(s, axis=-1, keepdims=True))
            pr = e / jnp.sum(e, axis=-1, keepdims=True)
            o_ref[:, sl] = jnp.dot(pr.astype(BF16), v_ref[:, sl].astype(BF16), preferred_element_type=F32).astype(BF16)

    return pl.pallas_call(
        body, grid=(T // tq,),
        in_specs=[_rows(tq, D), pl.BlockSpec((nm, D), lambda i: (0, 0)), pl.BlockSpec((nm, D), lambda i: (0, 1))],
        out_specs=_rows(tq, D), out_shape=jax.ShapeDtypeStruct((T, D), BF16),
        compiler_params=_cparams(("arbitrary",)), name=name)(q, kv, kv)


def attn_bwd(q, kv, do, *, name):
    T, D = q.shape
    nm = kv.shape[0]
    tq = _pick(T, [512, 256, 128])
    nt = T // tq
    scale = HEAD_DIM ** -0.5

    def body(q_ref, k_ref, v_ref, do_ref, dq_ref, dkv_ref):
        i = pl.program_id(0)

        @pl.when(i == 0)
        def _():
            dkv_ref[...] = jnp.zeros_like(dkv_ref)

        for h in range(N_HEADS):
            sl = slice(h * HEAD_DIM, (h + 1) * HEAD_DIM)
            slv = slice(D + h * HEAD_DIM, D + (h + 1) * HEAD_DIM)
            qh = q_ref[:, sl]
            kh = k_ref[:, sl].astype(BF16)
            vh = v_ref[:, sl].astype(BF16)
            doh = do_ref[:, sl].astype(BF16)
            s = lax.dot_general(qh, kh, _DIMS["nt"], preferred_element_type=F32) * scale
            e = jnp.exp(s - jnp.max(s, axis=-1, keepdims=True))
            pr = e / jnp.sum(e, axis=-1, keepdims=True)
            dp = lax.dot_general(doh, vh, _DIMS["nt"], preferred_element_type=F32)
            ds = (pr * (dp - jnp.sum(pr * dp, axis=-1, keepdims=True)) * scale).astype(BF16)
            dq_ref[:, sl] = jnp.dot(ds, kh, preferred_element_type=F32).astype(BF16)
            dkv_ref[:, sl] += lax.dot_general(ds, qh, _DIMS["tn"], preferred_element_type=F32)
            dkv_ref[:, slv] += lax.dot_general(pr.astype(BF16), doh, _DIMS["tn"], preferred_element_type=F32)

    return pl.pallas_call(
        body, grid=(nt,),
        in_specs=[_rows(tq, D), pl.BlockSpec((nm, D), lambda i: (0, 0)), pl.BlockSpec((nm, D), lambda i: (0, 1)), _rows(tq, D)],
        out_specs=[_rows(tq, D), _const((nm, 2 * D))],
        out_shape=[jax.ShapeDtypeStruct((T, D), BF16), jax.ShapeDtypeStruct((nm, 2 * D), F32)],
        compiler_params=_cparams(("arbitrary",)), name=name)(q, kv, kv, do)


def _adam_math(w, g, m, v):
    m = ADAM_B1 * m + (1.0 - ADAM_B1) * g
    v = ADAM_B2 * v + (1.0 - ADAM_B2) * (g * g)
    m_hat = m / (1.0 - ADAM_B1 ** ADAM_STEP)
    v_hat = v / (1.0 - ADAM_B2 ** ADAM_STEP)
    delta = -ADAM_LR * (m_hat / (jnp.sqrt(v_hat) + ADAM_EPS) + ADAM_WD * w)
    return delta, m, v


def adamw(w, m, v, g_arr, g_row0, *, name):
    R, C = w.shape
    tr = _pick(R, [256, 128, 64, 32, 16, 8])
    assert g_row0 % tr == 0
    g0 = g_row0 // tr

    def body(w_ref, m_ref, v_ref, g_ref, go_ref, d_ref, mo_ref, vo_ref):
        g = g_ref[...]
        d, mn, vn = _adam_math(w_ref[...], g, m_ref[...], v_ref[...])
        go_ref[...] = g
        d_ref[...] = d
        mo_ref[...] = mn
        vo_ref[...] = vn

    sp = pl.BlockSpec((tr, C), lambda i: (i, 0))
    return pl.pallas_call(
        body, grid=(R // tr,), in_specs=[sp, sp, sp, pl.BlockSpec((tr, C), lambda i: (g0 + i, 0))],
        out_specs=[sp] * 4, out_shape=[jax.ShapeDtypeStruct((R, C), F32)] * 4,
        compiler_params=_cparams(("arbitrary",)), name=name)(w, m, v, g_arr)


def _place():
    x, y, c = lax.axis_index("x"), lax.axis_index("y"), lax.axis_index("c")
    chips = [(1 - x, y), (x, 1 - y), (1 - x, 1 - y)]
    return x, y, c, chips


ANY = pl.BlockSpec(memory_space=pl.ANY)


def allgather_weights(bufs, *, name):
    n = len(bufs)

    def body(*refs):
        o_refs = refs[n:2 * n]
        send_sems, recv_sems, fsend_sems, frecv_sems = refs[2 * n:]
        x, y, c, chips = _place()
        k_me = 2 * x + y
        sib = (x, y, 1 - c)
        halves = [b.shape[1] // 2 for b in bufs]

        def half(a, cc):
            return pl.ds(pl.multiple_of(cc * halves[a], 16), halves[a])

        sends = []
        for a in range(n):
            for r, (px, py) in enumerate(chips):
                cp = pltpu.make_async_remote_copy(
                    src_ref=o_refs[a].at[k_me, half(a, c)], dst_ref=o_refs[a].at[k_me, half(a, c)],
                    send_sem=send_sems.at[3 * a + r], recv_sem=recv_sems.at[3 * a + r],
                    device_id=(px, py, c), device_id_type=MESH)
                cp.start()
                sends.append(cp)
        passed = []
        for a in range(n):
            for r, (px, py) in enumerate(chips):
                win = o_refs[a].at[2 * px + py, half(a, c)]
                pltpu.make_async_remote_copy(
                    src_ref=win, dst_ref=win, send_sem=send_sems.at[3 * a + r], recv_sem=recv_sems.at[3 * a + r],
                    device_id=(px, py, c), device_id_type=MESH).wait_recv()
                cp = pltpu.make_async_remote_copy(
                    src_ref=win, dst_ref=win, send_sem=fsend_sems.at[3 * a + r], recv_sem=frecv_sems.at[3 * a + r],
                    device_id=sib, device_id_type=MESH)
                cp.start()
                passed.append(cp)
        for a in range(n):
            for r, (px, py) in enumerate(chips):
                win = o_refs[a].at[2 * px + py, half(a, 1 - c)]
                pltpu.make_async_remote_copy(
                    src_ref=win, dst_ref=win, send_sem=fsend_sems.at[3 * a + r], recv_sem=frecv_sems.at[3 * a + r],
                    device_id=sib, device_id_type=MESH).wait_recv()
        for cp in sends + passed:
            cp.wait_send()

    return pl.pallas_call(
        body, in_specs=[ANY] * n, out_specs=[ANY] * n,
        out_shape=[jax.ShapeDtypeStruct(b.shape, b.dtype) for b in bufs],
        scratch_shapes=[pltpu.SemaphoreType.DMA((3 * n,))] * 4,
        input_output_aliases={a: a for a in range(n)},
        name=name)(*bufs)


HBM_SPEC = pl.BlockSpec(memory_space=pltpu.HBM)
SEM_SPEC = pl.BlockSpec(memory_space=pltpu.SEMAPHORE)


def _hbm(a):
    return pltpu.with_memory_space_constraint(a, pltpu.HBM)


def gather_start(bufs, pieces, *, name):
    n = len(bufs)
    npc = len(pieces)

    def body(*refs):
        b_refs = refs[:n]
        send_sems, recv_sems = refs[n], refs[n + 1]
        x, y, c, chips = _place()
        k_me = 2 * x + y
        for q, (a, row0, rows) in enumerate(pieces):
            win = b_refs[a].at[k_me, pl.ds(row0, rows)]
            for r, (px, py) in enumerate(chips):
                pltpu.make_async_remote_copy(
                    src_ref=win, dst_ref=win, send_sem=send_sems.at[3 * q + r], recv_sem=recv_sems.at[3 * q + r],
                    device_id=(px, py, c), device_id_type=MESH).start()

    return pl.pallas_call(
        body, in_specs=[HBM_SPEC] * n, out_specs=[SEM_SPEC, SEM_SPEC] + [HBM_SPEC] * n,
        out_shape=[pltpu.SemaphoreType.DMA((3 * npc,)), pltpu.SemaphoreType.DMA((3 * npc,))]
        + [pltpu.HBM(b.shape, b.dtype) for b in bufs],
        input_output_aliases={a: 2 + a for a in range(n)},
        compiler_params=pltpu.CompilerParams(has_side_effects=pltpu.SideEffectType.DATAFLOW_SIDE_EFFECTING),
        name=name)(*[_hbm(b) for b in bufs])


def gather_wait(send_sems, recv_sems, bufs, which, after, *, name):
    n = len(bufs)

    def body(*refs):
        b_refs = refs[:n]
        send_sems, recv_sems = refs[n], refs[n + 1]
        x, y, c, chips = _place()
        k_me = 2 * x + y
        for a, row0, rows, q in which:
            for r, (px, py) in enumerate(chips):
                cp = pltpu.make_async_remote_copy(
                    src_ref=b_refs[a].at[k_me, pl.ds(row0, rows)], dst_ref=b_refs[a].at[2 * px + py, pl.ds(row0, rows)],
                    send_sem=send_sems.at[3 * q + r], recv_sem=recv_sems.at[3 * q + r],
                    device_id=(px, py, c), device_id_type=MESH)
                cp.wait_send()
                cp.wait_recv()

    return pl.pallas_call(
        body, in_specs=[HBM_SPEC] * n + [SEM_SPEC, SEM_SPEC, ANY], out_specs=[HBM_SPEC] * n,
        out_shape=[pltpu.HBM(b.shape, b.dtype) for b in bufs],
        input_output_aliases={a: a for a in range(n)},
        compiler_params=pltpu.CompilerParams(has_side_effects=pltpu.SideEffectType.DATAFLOW_SIDE_EFFECTING),
        name=name)(*bufs, send_sems, recv_sems, after)


def exchange_halves(grads, *, name):
    n = len(grads)

    def body(*refs):
        g_refs, l_refs = refs[:n], refs[n:2 * n]
        send_sems, recv_sems = refs[2 * n:]
        x, y, c, _ = _place()
        cps = []
        for a in range(n):
            h = grads[a].shape[1] // 2
            cp = pltpu.make_async_remote_copy(
                src_ref=g_refs[a].at[:, pl.ds(pl.multiple_of((1 - c) * h, 8), h)], dst_ref=l_refs[a],
                send_sem=send_sems.at[a], recv_sem=recv_sems.at[a], device_id=(x, y, 1 - c), device_id_type=MESH)
            cp.start()
            cps.append(cp)
        for cp in cps:
            cp.wait()

    return pl.pallas_call(
        body, in_specs=[ANY] * n, out_specs=[ANY] * n,
        out_shape=[jax.ShapeDtypeStruct((g.shape[0], g.shape[1] // 2, g.shape[2]), g.dtype) for g in grads],
        scratch_shapes=[pltpu.SemaphoreType.DMA((n,))] * 2,
        name=name)(*grads)


N_PEERS = N_DEV - 1


def _scatter_copies(p_refs, l_refs, send_sems, recv_sems):
    x, y, c, _ = _place()
    cps = []
    for a in range(len(p_refs)):
        h = p_refs[a].shape[1] // 2
        for fx, fy in ((0, 0), (1, 0), (0, 1), (1, 1)):
            for fc in (0, 1):
                if (fx, fy, fc) == (0, 0, 0):
                    continue
                slot = 2 * (fx + 2 * fy) + fc - 1
                px, py, pc = (1 - x if fx else x), (1 - y if fy else y), (1 - c if fc else c)
                cps.append(pltpu.make_async_remote_copy(
                    src_ref=p_refs[a].at[2 * px + py, pl.ds(pl.multiple_of(pc * h, 16), h)], dst_ref=l_refs[a].at[slot],
                    send_sem=send_sems.at[N_PEERS * a + slot], recv_sem=recv_sems.at[N_PEERS * a + slot],
                    device_id=(px, py, pc), device_id_type=MESH))
    return cps


def scatter_start(parts, *, name):
    n = len(parts)
    lands = [lax.empty((N_PEERS, p.shape[1] // 2, p.shape[2]), p.dtype) for p in parts]

    def body(*refs):
        for cp in _scatter_copies(refs[:n], refs[n:2 * n], refs[2 * n], refs[2 * n + 1]):
            cp.start()

    outs = pl.pallas_call(
        body, in_specs=[HBM_SPEC] * (2 * n), out_specs=[SEM_SPEC, SEM_SPEC] + [HBM_SPEC] * (2 * n),
        out_shape=[pltpu.SemaphoreType.DMA((N_PEERS * n,)), pltpu.SemaphoreType.DMA((N_PEERS * n,))]
        + [pltpu.HBM(a.shape, a.dtype) for a in parts + lands],
        input_output_aliases={a: 2 + a for a in range(2 * n)},
        compiler_params=pltpu.CompilerParams(has_side_effects=pltpu.SideEffectType.DATAFLOW_SIDE_EFFECTING),
        name=name)(*[_hbm(a) for a in parts + lands])
    return outs[0], outs[1], list(outs[2:2 + n]), list(outs[2 + n:])


def scatter_wait(rounds, after, *, name):
    sizes = [len(r[2]) for r in rounds]
    flat = [a for r in rounds for a in r[2] + r[3]]
    sems = [s for r in rounds for s in (r[0], r[1])]
    nflat = len(flat)

    def body(*refs):
        pos = 0
        for ri, n in enumerate(sizes):
            for cp in _scatter_copies(refs[pos:pos + n], refs[pos + n:pos + 2 * n], refs[nflat + 2 * ri], refs[nflat + 2 * ri + 1]):
                cp.wait_send()
                cp.wait_recv()
            pos += 2 * n

    outs = pl.pallas_call(
        body, in_specs=[HBM_SPEC] * nflat + [SEM_SPEC] * len(sems) + [ANY], out_specs=[HBM_SPEC] * nflat,
        out_shape=[pltpu.HBM(a.shape, a.dtype) for a in flat],
        input_output_aliases={a: a for a in range(nflat)},
        compiler_params=pltpu.CompilerParams(has_side_effects=pltpu.SideEffectType.DATAFLOW_SIDE_EFFECTING),
        name=name)(*flat, *sems, after)
    res, pos = [], 0
    for n in sizes:
        res.append((list(outs[pos:pos + n]), list(outs[pos + n:pos + 2 * n])))
        pos += 2 * n
    return res


def join_halves(fulls, *, name):
    n = len(fulls)

    def body(*refs):
        o_refs = refs[n:2 * n]
        send_sems, recv_sems = refs[2 * n:]
        x, y, c, _ = _place()
        cps = []
        for a in range(n):
            h = fulls[a].shape[0] // 2
            win = o_refs[a].at[pl.ds(pl.multiple_of(c * h, 8), h)]
            cp = pltpu.make_async_remote_copy(
                src_ref=win, dst_ref=win, send_sem=send_sems.at[a], recv_sem=recv_sems.at[a],
                device_id=(x, y, 1 - c), device_id_type=MESH)
            cp.start()
            cps.append(cp)
        for a in range(n):
            h = fulls[a].shape[0] // 2
            other = o_refs[a].at[pl.ds(pl.multiple_of((1 - c) * h, 8), h)]
            pltpu.make_async_remote_copy(
                src_ref=other, dst_ref=other, send_sem=send_sems.at[a], recv_sem=recv_sems.at[a],
                device_id=(x, y, 1 - c), device_id_type=MESH).wait_recv()
        for cp in cps:
            cp.wait_send()

    return pl.pallas_call(
        body, in_specs=[ANY] * n, out_specs=[ANY] * n,
        out_shape=[jax.ShapeDtypeStruct(f.shape, f.dtype) for f in fulls],
        scratch_shapes=[pltpu.SemaphoreType.DMA((n,))] * 2,
        input_output_aliases={a: a for a in range(n)},
        name=name)(*fulls)


def add_partials(part, land, kc, *, name):
    _, R, C = part.shape
    H = R // 2
    tr = _pick(H, [256, 128, 64, 32, 16])
    per = H // tr

    def body(kc_ref, p_ref, l_ref, o_ref):
        acc = p_ref[...].astype(F32)
        for s in range(N_PEERS):
            acc = acc + l_ref[s].astype(F32)
        o_ref[...] = acc

    return pl.pallas_call(
        body,
        grid_spec=pltpu.PrefetchScalarGridSpec(
            num_scalar_prefetch=1, grid=(per,),
            in_specs=[pl.BlockSpec((None, tr, C), lambda i, kc_ref: (kc_ref[0], kc_ref[1] * per + i, 0)),
                      pl.BlockSpec((N_PEERS, tr, C), lambda i, kc_ref: (0, i, 0))],
            out_specs=pl.BlockSpec((tr, C), lambda i, kc_ref: (kc_ref[1] * per + i, 0))),
        out_shape=jax.ShapeDtypeStruct((R, C), F32),
        compiler_params=_cparams(("arbitrary",)), name=name)(kc, part, land)


def allgather_sum(v, *, name):
    m_per, n = v.shape

    def body(x_ref, out_ref, sum_ref, send_sems, recv_sems, local_sem):
        x, y, c, chips = _place()
        me, sibling = (x, y, c), (x, y, 1 - c)

        def rows(px, py, pc):
            return out_ref.at[pl.ds(pl.multiple_of((4 * px + 2 * py + pc) * m_per, 8), m_per), :]

        def copy(k, block, to, src=None):
            return pltpu.make_async_remote_copy(
                src_ref=rows(*block) if src is None else src, dst_ref=rows(*block),
                send_sem=send_sems.at[k], recv_sem=recv_sems.at[k], device_id=to, device_id_type=MESH)

        mine = pltpu.make_async_copy(x_ref, rows(*me), local_sem)
        mine.start()
        first = [copy(0, me, sibling, src=x_ref)]
        first += [copy(1 + j, me, (*chip, c), src=x_ref) for j, chip in enumerate(chips)]
        for cp in first:
            cp.start()
        passed = [copy(4 + j, (*chip, c), sibling) for j, chip in enumerate(chips)]
        for j, chip in enumerate(chips):
            copy(1 + j, (*chip, c), me).wait_recv()
            passed[j].start()
        copy(0, sibling, me).wait_recv()
        for j, chip in enumerate(chips):
            copy(4 + j, (*chip, 1 - c), me).wait_recv()
        for cp in first + passed:
            cp.wait_send()
        mine.wait()
        acc = out_ref[0:m_per, :]
        for d in range(1, N_DEV):
            acc = acc + out_ref[d * m_per:(d + 1) * m_per, :]
        sum_ref[...] = acc

    vm = pl.BlockSpec(memory_space=pltpu.VMEM)
    return pl.pallas_call(
        body, in_specs=[vm], out_specs=[vm, vm],
        out_shape=[jax.ShapeDtypeStruct((N_DEV * m_per, n), v.dtype), jax.ShapeDtypeStruct((m_per, n), v.dtype)],
        scratch_shapes=[pltpu.SemaphoreType.DMA((7,)), pltpu.SemaphoreType.DMA((7,)), pltpu.SemaphoreType.DMA],
        compiler_params=pltpu.CompilerParams(vmem_limit_bytes=VMEM_LIMIT_BYTES), name=name)(v)


def _pack(arrs):
    cols = []
    for a in arrs:
        f = a.reshape(-1)
        pad = (-f.shape[0]) % 128
        cols.append(jnp.pad(f, (0, pad)).reshape(-1, 128))
    out = jnp.concatenate(cols, axis=0)
    pad = (-out.shape[0]) % 8
    return jnp.pad(out, ((0, pad), (0, 0)))


def _unpack(buf, shapes):
    outs, r = [], 0
    for s in shapes:
        nel = math.prod(s)
        nr = -(-nel // 128)
        outs.append(buf[r:r + nr].reshape(-1)[:nel].reshape(s))
        r += nr
    return outs


GA_CONV_OUT, GA_MIX_OUT, GA_WQ, GA_WO, GA_DOWN, GA_UP, GA_ROWS = 0, 256, 512, 768, 1024, 2048, 3072
G1_DOWN, G1_UP, G1_ROWS = 0, 1024, 2048
G2_CONV_OUT, G2_MIX_OUT, G2_WQ, G2_WO, G2_ROWS = 0, 256, 512, 768, 1024


def kernel(x, mem, in_norm_g, in_norm_b, w_in, conv_dw, conv_db, conv_norm_g, conv_norm_b, w_conv_out, ssm_log_step, ssm_lambda_re, ssm_lambda_im, ssm_b_re, ssm_b_im, ssm_c_re, ssm_c_im, ssm_d, w_ssm_glu, w_mix_out, ln1_g, ln1_b, xa_wq, xa_wkv, xa_wo, ln2_g, ln2_b, mlp_w_up, mlp_w_down, ln3_g, ln3_b, loss_target, m_in_norm_g, m_in_norm_b, m_w_in, m_conv_dw, m_conv_db, m_conv_norm_g, m_conv_norm_b, m_w_conv_out, m_ssm_log_step, m_ssm_lambda_re, m_ssm_lambda_im, m_ssm_b_re, m_ssm_b_im, m_ssm_c_re, m_ssm_c_im, m_ssm_d, m_w_ssm_glu, m_w_mix_out, m_ln1_g, m_ln1_b, m_xa_wq, m_xa_wkv, m_xa_wo, m_ln2_g, m_ln2_b, m_mlp_w_up, m_mlp_w_down, m_ln3_g, m_ln3_b, v_in_norm_g, v_in_norm_b, v_w_in, v_conv_dw, v_conv_db, v_conv_norm_g, v_conv_norm_b, v_w_conv_out, v_ssm_log_step, v_ssm_lambda_re, v_ssm_lambda_im, v_ssm_b_re, v_ssm_b_im, v_ssm_c_re, v_ssm_c_im, v_ssm_d, v_w_ssm_glu, v_w_mix_out, v_ln1_g, v_ln1_b, v_xa_wq, v_xa_wkv, v_xa_wo, v_ln2_g, v_ln2_b, v_mlp_w_up, v_mlp_w_down, v_ln3_g, v_ln3_b):
    D = D_MODEL
    xs = x[0]
    T = xs.shape[0]
    mems = mem[0]
    NM = mems.shape[0]
    tgt = loss_target[0]
    my_c = lax.axis_index("c")
    k_me = 2 * lax.axis_index("x") + lax.axis_index("y")
    c_arr = jnp.reshape(my_c, (1,)).astype(jnp.int32)
    k_arr = jnp.reshape(k_me, (1,)).astype(jnp.int32)

    sh_a = jnp.concatenate([w_conv_out[0], w_mix_out[0], xa_wq[0], xa_wo[0], mlp_w_down[0], mlp_w_up[0]], axis=0).astype(BF16)
    def own_block(shard):
        buf = jnp.zeros((N_CHIPS,) + shard.shape, shard.dtype)
        return lax.dynamic_update_slice(buf, shard[None], (k_me, 0, 0))

    dw_pad = jnp.pad(conv_dw[0], ((0, CONV_HALO - CONV_K), (0, 0)))
    (GIN,) = allgather_weights([own_block(w_in[0].astype(BF16))], name="gather_w_in")
    ag_bufs = [own_block(sh_a), GIN] + [own_block(s) for s in (xa_wkv[0].astype(BF16), w_ssm_glu[0].astype(BF16), dw_pad)]
    ag_pieces = [(4, 0, CONV_HALO), (0, GA_CONV_OUT, 256), (3, 0, D_SSM), (0, GA_MIX_OUT, 256), (0, GA_WQ, 256),
                 (2, 0, D), (0, GA_WO, 256), (0, GA_UP, D), (0, GA_DOWN, D)]
    ag_send, ag_recv, GA, GIN, GKV, GGLU, GDW = gather_start(ag_bufs, ag_pieces, name="gather_start")

    def w_rowshard(row0):
        return dict(b_spec=((N_CHIPS, 256, D), lambda i, j, k: (0, row0 // 256, 0)), b_view=(D, D), tn=D, tk=D)

    _, h0, h0b = ln_fwd(xs, in_norm_g, in_norm_b, name="ln0_fwd")
    p = mm_nn(h0b, GIN, ((None, D, 1152), lambda i, j, k: (j, 0, 0)), D_IN, tn=1152, tk=D, name="mm_w_in")[0]
    GA, GGLU, GDW = gather_wait(
        ag_send, ag_recv, [GA, GGLU, GDW],
        [(2, 0, CONV_HALO, 0), (0, GA_CONV_OUT, 256, 1), (1, 0, D_SSM, 2), (0, GA_MIX_OUT, 256, 3)], p, name="gather_wait_mixer")
    dw_taps = jnp.transpose(GDW, (1, 0, 2)).reshape(CONV_HALO, D)
    c_pre, actb = conv_fwd(p, dw_taps, conv_db, conv_norm_g[0].reshape(1, D), conv_norm_b[0].reshape(1, D), name="conv_fwd")
    ya = mm_nn(actb, GA, N=D, name="mm_conv_out", **w_rowshard(GA_CONV_OUT))[0]

    lstep, lre, lim = ssm_log_step[0], ssm_lambda_re[0], ssm_lambda_im[0]
    bre, bim, cre, cim = ssm_b_re[0], ssm_b_im[0], ssm_c_re[0], ssm_c_im[0]
    (ar, ai, bbr, bbi), disc_vjp = jax.vjp(_ssm_discretise, lstep, lre, lim, bre, bim)
    Br, Bi = _blockdiag_in(bbr), _blockdiag_in(bbi)
    Cr, Ci = _blockdiag_out(cre), _blockdiag_out(cim)
    pw_r, pw_i, pwrev_r, pwrev_i = _powers(ar.reshape(-1), ai.reshape(-1))
    dvec = ssm_d[0].reshape(1, D_SSM)
    xr, xi, yssm = ssm_fwd(p, Br, Bi, Cr, Ci, pw_r, pw_i, dvec, name="ssm_fwd")
    z = mm_nn(yssm, GGLU, ((None, D_SSM, 512), lambda i, j, k: (j, 0, 0)), 2 * D, tn=512, tk=D_SSM, name="mm_ssm_glu")[0]
    mergedb = merge_fwd(p, ya, z, name="merge_fwd")
    tm_ln = _pick(T, [512, 256, 128])
    row_spec = ((1, D), lambda i, j, k: (0, 0))

    def ln_epilogue(acc, res, g, b):
        r = ALPHA * res + acc
        xhat, _ = _ln_stats(r)
        h = xhat * g + b
        return r, h, h

    def mm_ln(a, row0, res, g, b, name):
        return mm_nn(a, GA, N=D, tm=tm_ln, extras=(res, g.reshape(1, D), b.reshape(1, D)),
                     extra_specs=[_mn(tm_ln, D), row_spec, row_spec], epilogue=ln_epilogue, out_dtypes=[F32, F32, BF16],
                     name=name, **w_rowshard(row0))

    r1, h1, h1b = mm_ln(mergedb, GA_MIX_OUT, h0, ln1_g[0], ln1_b[0], "mm_mix_out_ln1")
    GA, GKV = gather_wait(ag_send, ag_recv, [GA, GKV], [(0, GA_WQ, 256, 4), (1, 0, D, 5), (0, GA_WO, 256, 6)], r1,
                          name="gather_wait_attn")

    qb = mm_nn(h1b, GA, N=D, out_dtype=BF16, name="mm_wq", **w_rowshard(GA_WQ))[0]
    kv = mm_nn(mems, GKV, ((None, D, 512), lambda i, j, k: (j, 0, 0)), 2 * D, tn=512, tk=D, name="mm_wkv")[0]
    ob = attn_fwd(qb, kv, name="attn_fwd")
    r2, h2, h2b = mm_ln(ob, GA_WO, h1, ln2_g[0], ln2_b[0], "mm_wo_ln2")
    (GA,) = gather_wait(ag_send, ag_recv, [GA], [(0, GA_UP, D, 7), (0, GA_DOWN, D, 8)], r2, name="gather_wait_mlp")

    def relu2(acc):
        zr = jnp.maximum(acc, 0.0)
        return acc, zr * zr

    zpre, zzb = mm_nn(h2b, GA, ((None, D, D), lambda i, j, k: (j, GA_UP // D, 0)), D_FF, tn=D, tk=D,
                      out_dtypes=[F32, BF16], epilogue=relu2, name="mm_up")
    ff = mm_nn(zzb, GA, ((N_CHIPS, D, D), lambda i, j, k: (0, GA_DOWN // D, 0)), D, tm=_pick(T, [512, 256, 128]), tn=D, tk=D_FF,
               b_view=(D_FF, D), name="mm_down")[0]
    dr3, dr3b, dg3, db3, sq = ln_loss_bwd(ff, h2, ln3_g[0], ln3_b[0], tgt, name="ln3_loss_bwd")

    def rs_begin(grads, rnd):
        return scatter_start(grads, name=f"rs{rnd}_scatter_start")

    g1_shape = jax.ShapeDtypeStruct((N_CHIPS, G1_ROWS, D), BF16)
    g2_shape = jax.ShapeDtypeStruct((N_CHIPS, G2_ROWS, D), BF16)
    dzpreb = mm_nt(dr3b, GA, ((None, D, D), lambda i, j, k: (j, GA_DOWN // D, 0)), D_FF, tn=D, tk=D, out_dtype=BF16,
                   extras=(zpre,), epilogue=lambda acc, zp: (acc * (2.0 * jnp.maximum(zp, 0.0)),), name="mm_down_t")[0]
    G1g = mm_tn(zzb, dr3b, tm=D, tn=D, out_spec=((None, D, D), lambda i, j, k: (i, G1_DOWN // D, 0)), out_shape=g1_shape,
                name="mm_down_g")
    G1g = mm_tn(h2b, dzpreb, tm=D, tn=D, out_spec=((None, D, D), lambda i, j, k: (j, G1_UP // D, 0)), out_shape=g1_shape,
                out_buf=G1g, name="mm_up_g")
    round1 = rs_begin([G1g], 1)
    dh2 = mm_nt(dzpreb, GA, ((N_CHIPS, D, D), lambda i, j, k: (0, GA_UP // D, 0)), D, tm=_pick(T, [512, 256, 128]), tn=D,
                tk=D_FF, b_chunks=N_CHIPS, extras=(dr3,), epilogue=lambda acc, d: (acc + ALPHA * d,),
                after=(round1[2][0],), name="mm_up_t")[0]
    dr2, dr2b, dg2, db2 = ln_bwd(r2, dh2, ln2_g[0], name="ln2_bwd")

    def g_rowshard(row0, out_buf):
        return dict(tm=D, tn=D, out_spec=((N_CHIPS, 256, D), lambda i, j, k: (0, row0 // 256, 0)), out_shape=g2_shape,
                    out_buf=out_buf)

    dob = mm_nt(dr2b, GA, N=D, out_dtype=BF16, name="mm_wo_t", **w_rowshard(GA_WO))[0]
    G2g = mm_tn(ob, dr2b, name="mm_wo_g", **g_rowshard(G2_WO, None))
    dqb, dkv = attn_bwd(qb, kv, dob, name="attn_bwd")
    G2g = mm_tn(h1b, dqb, name="mm_wq_g", **g_rowshard(G2_WQ, G2g))
    GKVg = mm_tn(mems, dkv, tm=D, tn=512, tk=NM, out_spec=((None, D, 512), lambda i, j, k: (j, 0, 0)),
                 out_shape=jax.ShapeDtypeStruct((N_CHIPS, D, 512), BF16), name="mm_wkv_g")
    dh1 = mm_nt(dqb, GA, N=D, extras=(dr2,), epilogue=lambda acc, d: (acc + ALPHA * d,), name="mm_wq_t",
                **w_rowshard(GA_WQ))[0]
    dr1, dr1b, dg1, db1 = ln_bwd(r1, dh1, ln1_g[0], name="ln1_bwd")

    dmerged = mm_nt(dr1b, GA, N=D, name="mm_mix_t", **w_rowshard(GA_MIX_OUT))[0]
    G2g = mm_tn(mergedb, dr1b, name="mm_mix_g", **g_rowshard(G2_MIX_OUT, G2g))
    dyab, dgab, dgbb, dz1b, dz2b = merge_bwd(dmerged, p, ya, z, name="merge_bwd")
    dzb = jnp.concatenate([dz1b, dz2b], axis=1)
    GGLUg = mm_tn(yssm, dzb, tm=D_SSM, tn=512, out_spec=((None, D_SSM, 512), lambda i, j, k: (j, 0, 0)),
                  out_shape=jax.ShapeDtypeStruct((N_CHIPS, D_SSM, 512), BF16), name="mm_glu_g")
    dyssm = mm_nt(dzb, GGLU, ((None, D_SSM, 512), lambda i, j, k: (k, 0, 0)), D_SSM, tn=D_SSM, tk=512, name="mm_glu_t")[0]
    dub, dBr, dBi, dCr, dCi, dar8, dai8, dd8 = ssm_bwd(dyssm, p, xr, xi, Br, Bi, Cr, Ci, pw_r, -pw_i, pwrev_r, -pwrev_i, dvec,
                                                       name="ssm_bwd")
    dar = jnp.sum(dar8, axis=0).reshape(SSM_GROUPS, SSM_STATE)
    dai = jnp.sum(dai8, axis=0).reshape(SSM_GROUPS, SSM_STATE)
    g_lstep, g_lre, g_lim, g_bre, g_bim = disc_vjp((dar, dai, _diag_in(dBr), _diag_in(dBi)))
    g_cre, g_cim = _diag_out(dCr), _diag_out(dCi)
    g_d = jnp.sum(dd8, axis=0).reshape(1, D_SSM)

    dact = mm_nt(dyab, GA, N=D, name="mm_conv_out_t", **w_rowshard(GA_CONV_OUT))[0]
    G2g = mm_tn(actb, dyab, name="mm_conv_out_g", **g_rowshard(G2_CONV_OUT, G2g))
    round2 = rs_begin([G2g, GKVg, GGLUg], 2)
    dc, dng, dnb, ddb = conv_bwd_norm(dact, c_pre, conv_norm_g[0].reshape(1, D), conv_norm_b[0].reshape(1, D),
                                      round2[2][0], name="conv_bwd_norm")
    dvgb, ddw = conv_bwd_taps(dc, p, dw_taps, name="conv_bwd_taps")
    dpb = jnp.concatenate([dvgb, dub, dgab, dgbb], axis=1)
    GINg = mm_tn(h0b, dpb, tm=D, tn=1152, out_spec=((None, D, 1152), lambda i, j, k: (j, 0, 0)),
                 out_shape=jax.ShapeDtypeStruct((N_CHIPS, D, 1152), BF16), name="mm_w_in_g")
    round3 = rs_begin([GINg], 3)
    dh0 = mm_nt(dpb, GIN, ((N_CHIPS, D, 1152), lambda i, j, k: (0, 0, 0)), D, tm=_pick(T, [512, 256, 128]), tn=D, tk=D_IN,
                b_chunks=N_CHIPS, extras=(dr1,), epilogue=lambda acc, d: (acc + ALPHA * d,), after=(round3[2][0],),
                name="mm_w_in_t")[0]
    gx, _, dg0, db0 = ln_bwd(xs, dh0, in_norm_g, name="ln0_bwd")

    kc_arr = jnp.concatenate([k_arr, c_arr])
    landed = scatter_wait([round1, round2, round3], gx, name="rs_scatter_wait")
    tags = ["mlp", "sq", "kv", "glu", "in"]
    pairs = [(pt, l2) for parts, lands2 in landed for pt, l2 in zip(parts, lands2)]
    halves = [add_partials(pt, l2, kc_arr, name="rs_add_partials_" + t) for (pt, l2), t in zip(pairs, tags)]
    g1, g2, gKV, gGLU, gIN = join_halves(halves, name="rs_join_halves")

    small_names = ["in_norm_g", "in_norm_b", "conv_db", "conv_norm_g", "conv_norm_b", "ssm_log_step", "ssm_lambda_re",
                   "ssm_lambda_im", "ssm_b_re", "ssm_b_im", "ssm_c_re", "ssm_c_im", "ssm_d", "ln1_g", "ln1_b",
                   "ln2_g", "ln2_b", "ln3_g", "ln3_b"]
    small_w = [in_norm_g, in_norm_b, conv_db, conv_norm_g, conv_norm_b, ssm_log_step, ssm_lambda_re, ssm_lambda_im,
               ssm_b_re, ssm_b_im, ssm_c_re, ssm_c_im, ssm_d, ln1_g, ln1_b, ln2_g, ln2_b, ln3_g, ln3_b]
    small_m = [m_in_norm_g, m_in_norm_b, m_conv_db, m_conv_norm_g, m_conv_norm_b, m_ssm_log_step, m_ssm_lambda_re,
               m_ssm_lambda_im, m_ssm_b_re, m_ssm_b_im, m_ssm_c_re, m_ssm_c_im, m_ssm_d, m_ln1_g, m_ln1_b, m_ln2_g,
               m_ln2_b, m_ln3_g, m_ln3_b]
    small_v = [v_in_norm_g, v_in_norm_b, v_conv_db, v_conv_norm_g, v_conv_norm_b, v_ssm_log_step, v_ssm_lambda_re,
               v_ssm_lambda_im, v_ssm_b_re, v_ssm_b_im, v_ssm_c_re, v_ssm_c_im, v_ssm_d, v_ln1_g, v_ln1_b, v_ln2_g,
               v_ln2_b, v_ln3_g, v_ln3_b]
    small_g = [dg0, db0, ddb, dng, dnb, g_lstep, g_lre, g_lim, g_bre, g_bim, g_cre, g_cim, g_d, dg1, db1, dg2, db2, dg3, db3]
    small_shapes = [w.shape for w in small_w]
    n_small_rows = _pack(small_w).shape[0]
    packed_g = _pack(small_g + [ddw, sq])
    _, summed = allgather_sum(packed_g, name="allreduce_small")
    small_rows = sum(-(-math.prod(s) // 128) for s in small_shapes)
    dw_rows = CONV_HALO * D // 128
    loss = 0.5 * summed[small_rows + dw_rows, 0] / D
    ddw_full = summed[small_rows:small_rows + dw_rows].reshape(CONV_HALO, D)
    g_dw = lax.dynamic_slice_in_dim(ddw_full, k_me * (D // N_CHIPS), D // N_CHIPS, axis=1)
    gs_packed = jnp.pad(summed[:small_rows], ((0, n_small_rows - small_rows), (0, 0)))

    res = {}

    def upd(nm, w, m, v, g_arr, row0=0):
        shp = w.shape
        w2, m2, v2 = (a.reshape(-1, shp[-1]) for a in (w, m, v))
        outs = adamw(w2, m2, v2, g_arr, row0, name="adamw_" + nm)
        res[nm] = tuple(o.reshape(shp) for o in outs)

    upd("w_conv_out", w_conv_out, m_w_conv_out, v_w_conv_out, g2, G2_CONV_OUT)
    upd("w_mix_out", w_mix_out, m_w_mix_out, v_w_mix_out, g2, G2_MIX_OUT)
    upd("xa_wq", xa_wq, m_xa_wq, v_xa_wq, g2, G2_WQ)
    upd("xa_wo", xa_wo, m_xa_wo, v_xa_wo, g2, G2_WO)
    upd("mlp_w_down", mlp_w_down, m_mlp_w_down, v_mlp_w_down, g1, G1_DOWN)
    upd("mlp_w_up", mlp_w_up, m_mlp_w_up, v_mlp_w_up, g1, G1_UP)
    upd("w_in", w_in, m_w_in, v_w_in, gIN)
    upd("xa_wkv", xa_wkv, m_xa_wkv, v_xa_wkv, gKV)
    upd("w_ssm_glu", w_ssm_glu, m_w_ssm_glu, v_w_ssm_glu, gGLU)
    pad_dw = lambda a: jnp.pad(a[0], ((0, CONV_HALO - CONV_K), (0, 0)))
    dw_outs = adamw(pad_dw(conv_dw), pad_dw(m_conv_dw), pad_dw(v_conv_dw), g_dw, 0, name="adamw_conv_dw")
    res["conv_dw"] = tuple(o[:CONV_K][None] for o in dw_outs)
    sm_outs = adamw(_pack(small_w), _pack(small_m), _pack(small_v), gs_packed, 0, name="adamw_small")
    sm_un = [_unpack(o, small_shapes) for o in sm_outs]
    for idx, nm in enumerate(small_names):
        res[nm] = tuple(sm_un[q][idx] for q in range(4))

    order = ["in_norm_g", "in_norm_b", "w_in", "conv_dw", "conv_db", "conv_norm_g", "conv_norm_b", "w_conv_out",
             "ssm_log_step", "ssm_lambda_re", "ssm_lambda_im", "ssm_b_re", "ssm_b_im", "ssm_c_re", "ssm_c_im", "ssm_d",
             "w_ssm_glu", "w_mix_out", "ln1_g", "ln1_b", "xa_wq", "xa_wkv", "xa_wo", "ln2_g", "ln2_b", "mlp_w_up",
             "mlp_w_down", "ln3_g", "ln3_b"]
    return (loss, gx[None], *[res[n][0] for n in order], *[res[n][1] for n in order],
            *[res[n][2] for n in order], *[res[n][3] for n in order])
```

```python
import functools
import math

import jax
import jax.numpy as jnp
from jax import lax
from jax.experimental import pallas as pl
from jax.experimental.pallas import tpu as pltpu

F32 = jnp.float32
BF16 = jnp.bfloat16
MESH = pl.DeviceIdType.MESH

D_MODEL = 1024
N_HEADS = 4
HEAD_DIM = D_MODEL // N_HEADS
CONV_K = 31
CONV_HALO = 32
D_SSM = 512
SSM_GROUPS = 32
SSM_GROUP = 16
SSM_STATE = 64
SSM_BLOCKS = 4
SSM_BLOCK_IN = D_SSM // SSM_BLOCKS
SSM_BLOCK_STATE = SSM_GROUPS * SSM_STATE // SSM_BLOCKS
D_FF = 4096
D_IN = 4608
LN_EPS = 1e-5
ALPHA = (2.0 * 1) ** 0.25
N_CHIPS = 4
N_DEV = 8
ADAM_LR, ADAM_B1, ADAM_B2, ADAM_EPS, ADAM_WD, ADAM_STEP = 0.001, 0.9, 0.999, 1e-08, 0.01, 10
VMEM_LIMIT_BYTES = 56 * 1024 * 1024


def _pick(dim, cands):
    for c in cands:
        if dim % c == 0:
            return c
    return dim


def _cparams(sem=None):
    return pltpu.CompilerParams(dimension_semantics=sem, vmem_limit_bytes=VMEM_LIMIT_BYTES)


def _sigmoid(x):
    return 1.0 / (1.0 + jnp.exp(-x))


_DIMS = {"nn": (((1,), (0,)), ((), ())), "nt": (((1,), (1,)), ((), ())), "tn": (((0,), (0,)), ((), ()))}


def matmul(a, b, *, mode, M, N, K, tm, tn, tk, a_spec, b_spec, out_specs, out_shapes, name,
           extras=(), extra_specs=(), epilogue=None, alias_buf=None, b_view=None, after=(), b_chunks=None):
    nk = K // tk
    ne = len(extras)
    no = len(out_shapes)
    na = (0 if alias_buf is None else 1) + len(after)
    dims = _DIMS[mode]

    def body(*refs):
        a_ref, b_ref = refs[0], refs[1]
        e_refs = refs[2:2 + ne]
        o_refs = refs[2 + ne + na:2 + ne + na + no]

        def finish(acc):
            outs = (acc,) if epilogue is None else epilogue(acc, *[r[...] for r in e_refs])
            for o, r in zip(outs, o_refs):
                r[...] = o.astype(r.dtype).reshape(r.shape)

        if b_chunks:
            kc = a_ref.shape[1] // b_chunks
            prod = None
            for q in range(b_chunks):
                part = lax.dot_general(a_ref[:, q * kc:(q + 1) * kc].astype(BF16), b_ref[q].astype(BF16), dims,
                                       preferred_element_type=F32)
                prod = part if prod is None else prod + part
        else:
            b_blk = b_ref[...] if b_view is None else b_ref[...].reshape(b_view)
            prod = lax.dot_general(a_ref[...].astype(BF16), b_blk.astype(BF16), dims, preferred_element_type=F32)
        if nk == 1:
            finish(prod)
        else:
            acc_ref = refs[-1]
            k = pl.program_id(2)

            @pl.when(k == 0)
            def _():
                acc_ref[...] = prod

            @pl.when(k > 0)
            def _():
                acc_ref[...] += prod

            @pl.when(k == nk - 1)
            def _():
                finish(acc_ref[...])

    in_specs = [pl.BlockSpec(*a_spec), pl.BlockSpec(*b_spec)] + [pl.BlockSpec(*s) for s in extra_specs]
    ins = [a, b, *extras]
    if alias_buf is not None:
        in_specs.append(pl.BlockSpec(memory_space=pl.ANY))
        ins.append(alias_buf)
    for dep in after:
        in_specs.append(pl.BlockSpec(memory_space=pl.ANY))
        ins.append(dep)
    res = pl.pallas_call(
        body,
        grid=(M // tm, N // tn, nk),
        in_specs=in_specs,
        out_specs=[pl.BlockSpec(*s) for s in out_specs],
        out_shape=out_shapes,
        scratch_shapes=[] if nk == 1 else [pltpu.VMEM((tm, tn), F32)],
        input_output_aliases={2 + ne: 0} if alias_buf is not None else {},
        compiler_params=_cparams(("parallel", "parallel", "arbitrary")),
        name=name,
    )(*ins)
    return res


def _mn(tm, tn):
    return ((tm, tn), lambda i, j, k: (i, j))


def mm_nn(a, b_arr, b_spec, N, *, name, tm=None, tn, tk, out_dtype=F32, extras=(), epilogue=None, out_dtypes=None,
          b_view=None, extra_specs=None):
    M, K = a.shape
    tm = tm or _pick(M, [1024, 512, 256, 128])
    dts = out_dtypes or [out_dtype]
    return matmul(a, b_arr, mode="nn", M=M, N=N, K=K, tm=tm, tn=tn, tk=tk,
                  a_spec=((tm, tk), lambda i, j, k: (i, k)), b_spec=b_spec, b_view=b_view,
                  out_specs=[_mn(tm, tn)] * len(dts), out_shapes=[jax.ShapeDtypeStruct((M, N), d) for d in dts],
                  extras=extras, extra_specs=extra_specs or [_mn(tm, tn)] * len(extras), epilogue=epilogue, name=name)


def mm_nt(a, b_arr, b_spec, N, *, name, tm=None, tn, tk, out_dtype=F32, extras=(), epilogue=None, out_dtypes=None,
          b_view=None, after=(), b_chunks=None):
    M, K = a.shape
    tm = tm or _pick(M, [1024, 512, 256, 128])
    dts = out_dtypes or [out_dtype]
    return matmul(a, b_arr, mode="nt", M=M, N=N, K=K, tm=tm, tn=tn, tk=tk, after=after, b_chunks=b_chunks,
                  a_spec=((tm, tk), lambda i, j, k: (i, k)), b_spec=b_spec, b_view=b_view,
                  out_specs=[_mn(tm, tn)] * len(dts), out_shapes=[jax.ShapeDtypeStruct((M, N), d) for d in dts],
                  extras=extras, extra_specs=[_mn(tm, tn)] * len(extras), epilogue=epilogue, name=name)


def mm_tn(a, b, *, name, tm, tn, tk=None, out_spec, out_shape, out_buf=None):
    K, M = a.shape
    N = b.shape[1]
    tk = tk or _pick(K, [2048, 1024, 512, 256, 128])
    return matmul(a, b, mode="tn", M=M, N=N, K=K, tm=tm, tn=tn, tk=tk,
                  a_spec=((tk, tm), lambda i, j, k: (k, i)), b_spec=((tk, tn), lambda i, j, k: (k, j)),
                  out_specs=[out_spec], out_shapes=[out_shape], alias_buf=out_buf, name=name)[0]


def _rows(tc, w, cb=0):
    return pl.BlockSpec((tc, w), lambda i: (i, cb))


def _const(shape):
    return pl.BlockSpec(shape, lambda i: tuple([0] * len(shape)))


def _ln_stats(r):
    mu = jnp.mean(r, axis=-1, keepdims=True)
    xc = r - mu
    var = jnp.mean(xc * xc, axis=-1, keepdims=True)
    rstd = lax.rsqrt(var + LN_EPS)
    return xc * rstd, rstd


def _rowsum8(v):
    tc, w = v.shape
    return jnp.sum(v.reshape(tc // 8, 8, w), axis=0)


def ln_fwd(x, g, b, *, name, res=None):
    T, D = x.shape
    tc = _pick(T, [512, 256, 128])
    has_res = res is not None

    def body(*refs):
        if has_res:
            x_ref, res_ref, g_ref, b_ref, r_ref, h_ref, hb_ref = refs
            r = ALPHA * res_ref[...] + x_ref[...]
            r_ref[...] = r
        else:
            x_ref, g_ref, b_ref, h_ref, hb_ref = refs
            r = x_ref[...]
        xhat, _ = _ln_stats(r)
        y = xhat * g_ref[...] + b_ref[...]
        h_ref[...] = y
        hb_ref[...] = y.astype(BF16)

    ins = [x] + ([res] if has_res else []) + [g.reshape(1, D), b.reshape(1, D)]
    in_specs = [_rows(tc, D)] * (2 if has_res else 1) + [_const((1, D))] * 2
    n_out = 3 if has_res else 2
    outs = pl.pallas_call(
        body, grid=(T // tc,), in_specs=in_specs, out_specs=[_rows(tc, D)] * n_out,
        out_shape=[jax.ShapeDtypeStruct((T, D), F32)] * (n_out - 1) + [jax.ShapeDtypeStruct((T, D), BF16)],
        compiler_params=_cparams(("arbitrary",)), name=name)(*ins)
    if has_res:
        return outs
    return (x,) + tuple(outs)


def ln_bwd(r, dy, g, *, name):
    T, D = r.shape
    tc = _pick(T, [512, 256, 128])
    nt = T // tc

    def body(r_ref, dy_ref, g_ref, dr_ref, drb_ref, dg_ref, db_ref, accg, accb):
        i = pl.program_id(0)

        @pl.when(i == 0)
        def _():
            accg[...] = jnp.zeros_like(accg)
            accb[...] = jnp.zeros_like(accb)

        xhat, rstd = _ln_stats(r_ref[...])
        dy = dy_ref[...]
        dxh = dy * g_ref[...]
        m1 = jnp.mean(dxh, axis=-1, keepdims=True)
        m2 = jnp.mean(dxh * xhat, axis=-1, keepdims=True)
        dr = rstd * (dxh - m1 - xhat * m2)
        dr_ref[...] = dr
        drb_ref[...] = dr.astype(BF16)
        accg[...] += _rowsum8(dy * xhat)
        accb[...] += _rowsum8(dy)

        @pl.when(i == nt - 1)
        def _():
            dg_ref[...] = jnp.sum(accg[...], axis=0, keepdims=True)
            db_ref[...] = jnp.sum(accb[...], axis=0, keepdims=True)

    return pl.pallas_call(
        body, grid=(nt,), in_specs=[_rows(tc, D), _rows(tc, D), _const((1, D))],
        out_specs=[_rows(tc, D), _rows(tc, D), _const((1, D)), _const((1, D))],
        out_shape=[jax.ShapeDtypeStruct((T, D), F32), jax.ShapeDtypeStruct((T, D), BF16),
                   jax.ShapeDtypeStruct((1, D), F32), jax.ShapeDtypeStruct((1, D), F32)],
        scratch_shapes=[pltpu.VMEM((8, D), F32), pltpu.VMEM((8, D), F32)],
        compiler_params=_cparams(("arbitrary",)), name=name)(r, dy, g.reshape(1, D))


def ln_loss_bwd(x, res, g, b, target, *, name):
    T, D = x.shape
    tc = _pick(T, [512, 256, 128])
    nt = T // tc

    def body(x_ref, res_ref, g_ref, b_ref, t_ref, dr_ref, drb_ref, dg_ref, db_ref, loss_ref, accg, accb, accl):
        i = pl.program_id(0)

        @pl.when(i == 0)
        def _():
            accg[...] = jnp.zeros_like(accg)
            accb[...] = jnp.zeros_like(accb)
            accl[...] = jnp.zeros_like(accl)

        r = ALPHA * res_ref[...] + x_ref[...]
        xhat, rstd = _ln_stats(r)
        e = xhat * g_ref[...] + b_ref[...] - t_ref[...]
        dy = e * (1.0 / D)
        dxh = dy * g_ref[...]
        m1 = jnp.mean(dxh, axis=-1, keepdims=True)
        m2 = jnp.mean(dxh * xhat, axis=-1, keepdims=True)
        dr = rstd * (dxh - m1 - xhat * m2)
        dr_ref[...] = dr
        drb_ref[...] = dr.astype(BF16)
        accg[...] += _rowsum8(dy * xhat)
        accb[...] += _rowsum8(dy)
        accl[...] += _rowsum8(e * e)

        @pl.when(i == nt - 1)
        def _():
            dg_ref[...] = jnp.sum(accg[...], axis=0, keepdims=True)
            db_ref[...] = jnp.sum(accb[...], axis=0, keepdims=True)
            s = jnp.sum(jnp.sum(accl[...], axis=0, keepdims=True), axis=1, keepdims=True)
            loss_ref[...] = jnp.broadcast_to(s, (1, 128))

    return pl.pallas_call(
        body, grid=(nt,), in_specs=[_rows(tc, D), _rows(tc, D), _const((1, D)), _const((1, D)), _rows(tc, D)],
        out_specs=[_rows(tc, D), _rows(tc, D), _const((1, D)), _const((1, D)), _const((1, 128))],
        out_shape=[jax.ShapeDtypeStruct((T, D), F32), jax.ShapeDtypeStruct((T, D), BF16),
                   jax.ShapeDtypeStruct((1, D), F32), jax.ShapeDtypeStruct((1, D), F32), jax.ShapeDtypeStruct((1, 128), F32)],
        scratch_shapes=[pltpu.VMEM((8, D), F32)] * 3,
        compiler_params=_cparams(("arbitrary",)), name=name)(x, res, g.reshape(1, D), b.reshape(1, D), target)


def _halo_prev(tc):
    per = tc // CONV_HALO
    return lambda i: jnp.maximum(i * per - 1, 0)


CONV_ROWS = 32
CONV_TAP_GROUP = 4
CONV_TILE_UNROLL = 4


def _fill_shifts(S, nrows):
    for b in range(1, 8):
        S[b, 0:nrows - 8, :] = S[0, b:b + nrows - 8, :]


def _tap_sum(S, w_ref, offs, r0, nrows):
    acc = None
    for k, o in enumerate(offs):
        a, b = divmod(o, 8)
        term = w_ref[k:k + 1, :] * S[b, pl.ds(pl.multiple_of(r0 + 8 * a, 8), nrows), :]
        acc = term if acc is None else acc + term
    return acc


def conv_fwd(p, dw, db, ng, nb, *, name):
    T = p.shape[0]
    D = D_MODEL
    tc = _pick(T, [256, 128])
    prev = _halo_prev(tc)
    off = CONV_HALO - (CONV_K - 1)
    offs = [off + k for k in range(CONV_K)]

    def body(val_ref, gate_ref, valp_ref, gatep_ref, dw_ref, db_ref, ng_ref, nb_ref, c_ref, act_ref, S):
        i = pl.program_id(0)
        u_prev = valp_ref[...] * _sigmoid(gatep_ref[...])
        S[0, 0:CONV_HALO, :] = jnp.where(i > 0, u_prev, 0.0)
        S[0, CONV_HALO:CONV_HALO + tc, :] = val_ref[...] * _sigmoid(gate_ref[...])
        _fill_shifts(S, CONV_HALO + tc)

        def rows(j, carry):
            r0 = pl.multiple_of(j * CONV_ROWS, CONV_ROWS)
            c_ref[pl.ds(r0, CONV_ROWS), :] = _tap_sum(S, dw_ref, offs, r0, CONV_ROWS) + db_ref[...]
            return carry

        lax.fori_loop(0, tc // CONV_ROWS, rows, 0)
        c = c_ref[...]
        xhat, _ = _ln_stats(c)
        cn = xhat * ng_ref[...] + nb_ref[...]
        act_ref[...] = (cn * _sigmoid(cn)).astype(BF16)

    return pl.pallas_call(
        body, grid=(T // tc,),
        in_specs=[_rows(tc, D, 0), _rows(tc, D, 1),
                  pl.BlockSpec((CONV_HALO, D), lambda i: (prev(i), 0)), pl.BlockSpec((CONV_HALO, D), lambda i: (prev(i), 1)),
                  _const((CONV_HALO, D)), _const((1, D)), _const((1, D)), _const((1, D))],
        out_specs=[_rows(tc, D), _rows(tc, D)],
        out_shape=[jax.ShapeDtypeStruct((T, D), F32), jax.ShapeDtypeStruct((T, D), BF16)],
        scratch_shapes=[pltpu.VMEM((8, CONV_HALO + tc, D), F32)],
        compiler_params=_cparams(("arbitrary",)), name=name)(p, p, p, p, dw, db, ng, nb)


def conv_bwd_norm(dact, c_pre, ng, nb, after, *, name):
    T, D = c_pre.shape
    tc = _pick(T, [512, 256, 128])
    nt = T // tc

    def body(da_ref, c_ref, ng_ref, nb_ref, after_ref, dc_ref, dng_ref, dnb_ref, ddb_ref, accg, accb, accd):
        i = pl.program_id(0)

        @pl.when(i == 0)
        def _():
            accg[...] = jnp.zeros_like(accg)
            accb[...] = jnp.zeros_like(accb)
            accd[...] = jnp.zeros_like(accd)

        xhat, rstd = _ln_stats(c_ref[...])
        cn = xhat * ng_ref[...] + nb_ref[...]
        s = _sigmoid(cn)
        dcn = da_ref[...] * (s * (1.0 + cn * (1.0 - s)))
        dxh = dcn * ng_ref[...]
        m1 = jnp.mean(dxh, axis=-1, keepdims=True)
        m2 = jnp.mean(dxh * xhat, axis=-1, keepdims=True)
        dc = rstd * (dxh - m1 - xhat * m2)
        dc_ref[...] = dc
        accg[...] += _rowsum8(dcn * xhat)
        accb[...] += _rowsum8(dcn)
        accd[...] += _rowsum8(dc)

        @pl.when(i == nt - 1)
        def _():
            dng_ref[...] = jnp.sum(accg[...], axis=0, keepdims=True)
            dnb_ref[...] = jnp.sum(accb[...], axis=0, keepdims=True)
            ddb_ref[...] = jnp.sum(accd[...], axis=0, keepdims=True)

    vec = jax.ShapeDtypeStruct((1, D), F32)
    return pl.pallas_call(
        body, grid=(nt,), in_specs=[_rows(tc, D), _rows(tc, D), _const((1, D)), _const((1, D)), ANY],
        out_specs=[_rows(tc, D), _const((1, D)), _const((1, D)), _const((1, D))],
        out_shape=[jax.ShapeDtypeStruct((T, D), F32), vec, vec, vec],
        scratch_shapes=[pltpu.VMEM((8, D), F32)] * 3,
        compiler_params=_cparams(("arbitrary",)), name=name)(dact, c_pre, ng, nb, after)


def conv_bwd_taps(dc, p, dw, *, name):
    T, D = dc.shape
    tc = _pick(T, [256, 128])
    nt = T // tc
    per = tc // CONV_HALO
    prev = _halo_prev(tc)
    last_halo = T // CONV_HALO - 1
    nxt = lambda i: jnp.minimum((i + 1) * per, last_halo)
    off = CONV_HALO - (CONV_K - 1)

    def body(dc_ref, dcn_ref, val_ref, gate_ref, valp_ref, gatep_ref, dw_ref, dvg_ref, ddw_ref, ext_u, ext_d, acc):
        i = pl.program_id(0)

        @pl.when(i == 0)
        def _():
            acc[...] = jnp.zeros_like(acc)

        u_prev = valp_ref[...] * _sigmoid(gatep_ref[...])
        ext_u[0, 0:CONV_HALO, :] = jnp.where(i > 0, u_prev, 0.0)
        ext_u[0, CONV_HALO:CONV_HALO + tc, :] = val_ref[...] * _sigmoid(gate_ref[...])
        ext_d[0, 0:tc, :] = dc_ref[...]
        ext_d[0, tc:tc + CONV_HALO, :] = jnp.where(i < nt - 1, dcn_ref[...], 0.0)
        _fill_shifts(ext_u, CONV_HALO + tc)
        _fill_shifts(ext_d, CONV_HALO + tc)

        def rows(j, carry):
            r0 = pl.multiple_of(j * CONV_ROWS, CONV_ROWS)
            sl = pl.ds(r0, CONV_ROWS)
            du = _tap_sum(ext_d, dw_ref, [CONV_K - 1 - k for k in range(CONV_K)], r0, CONV_ROWS)
            sg = _sigmoid(gate_ref[sl, :])
            dvg_ref[sl, 0:D] = (du * sg).astype(BF16)
            dvg_ref[sl, D:2 * D] = (du * val_ref[sl, :] * sg * (1.0 - sg)).astype(BF16)
            return carry

        lax.fori_loop(0, tc // CONV_ROWS, rows, 0)

        for k0 in range(0, CONV_K, CONV_TAP_GROUP):
            ks = list(range(k0, min(k0 + CONV_TAP_GROUP, CONV_K)))

            def taps(j, accs, ks=ks):
                out = list(accs)
                for t in range(CONV_TILE_UNROLL):
                    r0 = pl.multiple_of((j * CONV_TILE_UNROLL + t) * 8, 8)
                    dct = dc_ref[pl.ds(r0, 8), :]
                    for q, k in enumerate(ks):
                        a, b = divmod(off + k, 8)
                        out[q] = out[q] + dct * ext_u[b, pl.ds(pl.multiple_of(r0 + 8 * a, 8), 8), :]
                return tuple(out)

            accs = lax.fori_loop(0, tc // (8 * CONV_TILE_UNROLL), taps, tuple(jnp.zeros((8, D), F32) for _ in ks))
            for k, a_k in zip(ks, accs):
                acc[k] += a_k

        @pl.when(i == nt - 1)
        def _():
            ddw_ref[...] = jnp.zeros_like(ddw_ref)
            for k in range(CONV_K):
                ddw_ref[k:k + 1, :] = jnp.sum(acc[k], axis=0, keepdims=True)

    return pl.pallas_call(
        body, grid=(nt,),
        in_specs=[_rows(tc, D), pl.BlockSpec((CONV_HALO, D), lambda i: (nxt(i), 0)),
                  _rows(tc, D, 0), _rows(tc, D, 1),
                  pl.BlockSpec((CONV_HALO, D), lambda i: (prev(i), 0)), pl.BlockSpec((CONV_HALO, D), lambda i: (prev(i), 1)),
                  _const((CONV_HALO, D))],
        out_specs=[_rows(tc, 2 * D), _const((CONV_HALO, D))],
        out_shape=[jax.ShapeDtypeStruct((T, 2 * D), BF16), jax.ShapeDtypeStruct((CONV_HALO, D), F32)],
        scratch_shapes=[pltpu.VMEM((8, CONV_HALO + tc, D), F32), pltpu.VMEM((8, CONV_HALO + tc, D), F32),
                        pltpu.VMEM((CONV_K, 8, D), F32)],
        compiler_params=_cparams(("arbitrary",)), name=name)(dc, dc, p, p, p, p, dw)


GATE_A0 = (2 * D_MODEL + D_SSM) // 512
GATE_B0 = GATE_A0 + 2


def merge_fwd(p, ya, z, *, name):
    T = p.shape[0]
    D = D_MODEL
    tc = _pick(T, [512, 256, 128])
    W = 512

    def body(ga_ref, gb_ref, ya_ref, z1_ref, z2_ref, o_ref):
        yb = z1_ref[...] * _sigmoid(z2_ref[...])
        o_ref[...] = (_sigmoid(ga_ref[...]) * ya_ref[...] + _sigmoid(gb_ref[...]) * yb).astype(BF16)

    return pl.pallas_call(
        body, grid=(T // tc, D // W),
        in_specs=[pl.BlockSpec((tc, W), lambda i, j: (i, GATE_A0 + j)), pl.BlockSpec((tc, W), lambda i, j: (i, GATE_B0 + j)),
                  pl.BlockSpec((tc, W), lambda i, j: (i, j)), pl.BlockSpec((tc, W), lambda i, j: (i, j)),
                  pl.BlockSpec((tc, W), lambda i, j: (i, D // W + j))],
        out_specs=pl.BlockSpec((tc, W), lambda i, j: (i, j)),
        out_shape=jax.ShapeDtypeStruct((T, D), BF16),
        compiler_params=_cparams(("arbitrary", "arbitrary")), name=name)(p, p, ya, z, z)


def merge_bwd(dm, p, ya, z, *, name):
    T = p.shape[0]
    D = D_MODEL
    tc = _pick(T, [512, 256, 128])
    W = 512
    nb = D // W

    def body(dm_ref, ga_ref, gb_ref, ya_ref, z1_ref, z2_ref, dya_ref, dga_ref, dgb_ref, dz1_ref, dz2_ref):
        dm = dm_ref[...]
        sa = _sigmoid(ga_ref[...])
        sb = _sigmoid(gb_ref[...])
        s2 = _sigmoid(z2_ref[...])
        z1 = z1_ref[...]
        yb = z1 * s2
        dya_ref[...] = (dm * sa).astype(BF16)
        dga_ref[...] = (dm * ya_ref[...] * sa * (1.0 - sa)).astype(BF16)
        dgb_ref[...] = (dm * yb * sb * (1.0 - sb)).astype(BF16)
        dyb = dm * sb
        dz1_ref[...] = (dyb * s2).astype(BF16)
        dz2_ref[...] = (dyb * z1 * s2 * (1.0 - s2)).astype(BF16)

    blk = lambda off: pl.BlockSpec((tc, W), lambda i, j: (i, off + j))
    dya, dga, dgb, dz1, dz2 = pl.pallas_call(
        body, grid=(T // tc, nb),
        in_specs=[blk(0), blk(GATE_A0), blk(GATE_B0), blk(0), blk(0), blk(nb)],
        out_specs=[blk(0)] * 5,
        out_shape=[jax.ShapeDtypeStruct((T, D), BF16)] * 5,
        compiler_params=_cparams(("arbitrary", "arbitrary")), name=name)(dm, p, p, ya, z, z)
    return dya, dga, dgb, dz1, dz2


def _scan_block(src_r, src_i, dst_r, dst_i, car_r, car_i, pw_r, pw_i, cw_r, cw_i, ntiles, reverse, extra=None):
    W = src_r.shape[1]
    rows = lax.broadcasted_iota(jnp.int32, (8, W), 0)
    steps = []
    for d, pr in ((1, 0), (2, 1), (4, 3)):
        valid = rows < 8 - d if reverse else rows >= d
        steps.append((d, jnp.where(valid, jnp.broadcast_to(pw_r[pr:pr + 1, :], (8, W)), 0.0),
                      jnp.where(valid, jnp.broadcast_to(pw_i[pr:pr + 1, :], (8, W)), 0.0)))
    cw_r, cw_i = cw_r[...], cw_i[...]

    def tile(jj, carry):
        j = ntiles - 1 - jj if reverse else jj
        sl = pl.ds(pl.multiple_of(j * 8, 8), 8)
        xr, xi = src_r[sl, :], src_i[sl, :]
        for d, lr, li in steps:
            sr = pltpu.roll(xr, 8 - d if reverse else d, 0)
            si = pltpu.roll(xi, 8 - d if reverse else d, 0)
            xr, xi = xr + lr * sr - li * si, xi + lr * si + li * sr
        cr, ci = car_r[...], car_i[...]
        xr, xi = xr + cw_r * cr - cw_i * ci, xi + cw_r * ci + cw_i * cr
        dst_r[sl, :] = xr
        dst_i[sl, :] = xi
        edge = 0 if reverse else 7
        car_r[...] = jnp.broadcast_to(xr[edge:edge + 1, :], (8, W))
        car_i[...] = jnp.broadcast_to(xi[edge:edge + 1, :], (8, W))
        if extra is not None:
            carry = extra(j, xr, xi, carry)
        return carry

    return tile


def ssm_fwd(p, Br, Bi, Cr, Ci, pw_r, pw_i, dvec, *, name):
    T = p.shape[0]
    tt = _pick(T, [512, 256, 128])
    nt = T // tt
    WI, WS = SSM_BLOCK_IN, SSM_BLOCK_STATE
    u0 = 2 * D_MODEL // WI

    def body(u_ref, br_ref, bi_ref, cr_ref, ci_ref, pwr_ref, pwi_ref, d_ref, xr_ref, xi_ref, y_ref, bur, bui, car_r, car_i):
        i = pl.program_id(1)

        @pl.when(i == 0)
        def _():
            car_r[...] = jnp.zeros_like(car_r)
            car_i[...] = jnp.zeros_like(car_i)

        u = u_ref[...]
        ub = u.astype(BF16)
        bur[...] = jnp.dot(ub, br_ref[...].astype(BF16), preferred_element_type=F32)
        bui[...] = jnp.dot(ub, bi_ref[...].astype(BF16), preferred_element_type=F32)
        tile = _scan_block(bur, bui, xr_ref, xi_ref, car_r, car_i, pwr_ref, pwi_ref, pwr_ref, pwi_ref, tt // 8, False)
        lax.fori_loop(0, tt // 8, tile, 0)
        y = (jnp.dot(xr_ref[...].astype(BF16), cr_ref[...].astype(BF16), preferred_element_type=F32)
             - jnp.dot(xi_ref[...].astype(BF16), ci_ref[...].astype(BF16), preferred_element_type=F32)
             + d_ref[...] * u)
        y_ref[...] = y.astype(BF16)

    return pl.pallas_call(
        body, grid=(SSM_BLOCKS, nt),
        in_specs=[pl.BlockSpec((tt, WI), lambda b, i: (i, u0 + b)),
                  pl.BlockSpec((None, WI, WS), lambda b, i: (b, 0, 0)), pl.BlockSpec((None, WI, WS), lambda b, i: (b, 0, 0)),
                  pl.BlockSpec((None, WS, WI), lambda b, i: (b, 0, 0)), pl.BlockSpec((None, WS, WI), lambda b, i: (b, 0, 0)),
                  pl.BlockSpec((8, WS), lambda b, i: (0, b)), pl.BlockSpec((8, WS), lambda b, i: (0, b)),
                  pl.BlockSpec((1, WI), lambda b, i: (0, b))],
        out_specs=[pl.BlockSpec((tt, WS), lambda b, i: (i, b)), pl.BlockSpec((tt, WS), lambda b, i: (i, b)),
                   pl.BlockSpec((tt, WI), lambda b, i: (i, b))],
        out_shape=[jax.ShapeDtypeStruct((T, SSM_BLOCKS * WS), F32)] * 2 + [jax.ShapeDtypeStruct((T, D_SSM), BF16)],
        scratch_shapes=[pltpu.VMEM((tt, WS), F32), pltpu.VMEM((tt, WS), F32), pltpu.VMEM((8, WS), F32), pltpu.VMEM((8, WS), F32)],
        compiler_params=_cparams(("arbitrary", "arbitrary")), name=name)(p, Br, Bi, Cr, Ci, pw_r, pw_i, dvec)


def ssm_bwd(dy, p, xr, xi, Br, Bi, Cr, Ci, pwc_r, pwc_i, cwc_r, cwc_i, dvec, *, name):
    T = p.shape[0]
    tt = _pick(T, [512, 256, 128])
    nt = T // tt
    WI, WS = SSM_BLOCK_IN, SSM_BLOCK_STATE
    u0 = 2 * D_MODEL // WI
    tb = lambda i: nt - 1 - i
    xprev = lambda i: jnp.maximum(tb(i) * (tt // 8) - 1, 0)
    tn_dims = _DIMS["tn"]
    nt_dims = _DIMS["nt"]

    def body(dy_ref, u_ref, xr_ref, xi_ref, xpr_ref, xpi_ref, br_ref, bi_ref, cr_ref, ci_ref, pwr_ref, pwi_ref,
             cwr_ref, cwi_ref, d_ref,
             du_ref, dbr_ref, dbi_ref, dcr_ref, dci_ref, dar_ref, dai_ref, dd_ref,
             gr, gi, ext_r, ext_i, car_r, car_i):
        i = pl.program_id(1)

        @pl.when(i == 0)
        def _():
            car_r[...] = jnp.zeros_like(car_r)
            car_i[...] = jnp.zeros_like(car_i)
            dbr_ref[...] = jnp.zeros_like(dbr_ref)
            dbi_ref[...] = jnp.zeros_like(dbi_ref)
            dcr_ref[...] = jnp.zeros_like(dcr_ref)
            dci_ref[...] = jnp.zeros_like(dci_ref)
            dar_ref[...] = jnp.zeros_like(dar_ref)
            dai_ref[...] = jnp.zeros_like(dai_ref)
            dd_ref[...] = jnp.zeros_like(dd_ref)

        dy = dy_ref[...]
        dyb = dy.astype(BF16)
        u = u_ref[...]
        ub = u.astype(BF16)
        gr[...] = lax.dot_general(dyb, cr_ref[...].astype(BF16), nt_dims, preferred_element_type=F32)
        gi[...] = -lax.dot_general(dyb, ci_ref[...].astype(BF16), nt_dims, preferred_element_type=F32)
        first = tb(i) == 0
        ext_r[0:8, :] = jnp.where(first, 0.0, xpr_ref[...])
        ext_i[0:8, :] = jnp.where(first, 0.0, xpi_ref[...])
        ext_r[8:8 + tt, :] = xr_ref[...]
        ext_i[8:8 + tt, :] = xi_ref[...]
        rows = lax.broadcasted_iota(jnp.int32, (8, WS), 0)

        def lam_grad(j, g_r, g_i, carry):
            a_r, a_i = carry
            cur = pl.ds(pl.multiple_of(j * 8 + 8, 8), 8)
            prv = pl.ds(pl.multiple_of(j * 8, 8), 8)
            xc_r, xc_i = ext_r[cur, :], ext_i[cur, :]
            xl_r, xl_i = ext_r[prv, :], ext_i[prv, :]
            xp_r = jnp.where(rows == 0, jnp.broadcast_to(xl_r[7:8, :], (8, WS)), pltpu.roll(xc_r, 1, 0))
            xp_i = jnp.where(rows == 0, jnp.broadcast_to(xl_i[7:8, :], (8, WS)), pltpu.roll(xc_i, 1, 0))
            return (a_r + g_r * xp_r + g_i * xp_i, a_i + g_i * xp_r - g_r * xp_i)

        tile = _scan_block(gr, gi, gr, gi, car_r, car_i, pwr_ref, pwi_ref, cwr_ref, cwi_ref, tt // 8, True, extra=lam_grad)
        z8 = jnp.zeros((8, WS), F32)
        a_r, a_i = lax.fori_loop(0, tt // 8, tile, (z8, z8))
        dar_ref[...] += a_r
        dai_ref[...] += a_i
        grb = gr[...].astype(BF16)
        gib = gi[...].astype(BF16)
        dbr_ref[...] += lax.dot_general(ub, grb, tn_dims, preferred_element_type=F32)
        dbi_ref[...] += lax.dot_general(ub, gib, tn_dims, preferred_element_type=F32)
        dcr_ref[...] += lax.dot_general(xr_ref[...].astype(BF16), dyb, tn_dims, preferred_element_type=F32)
        dci_ref[...] -= lax.dot_general(xi_ref[...].astype(BF16), dyb, tn_dims, preferred_element_type=F32)
        du = (lax.dot_general(grb, br_ref[...].astype(BF16), nt_dims, preferred_element_type=F32)
              + lax.dot_general(gib, bi_ref[...].astype(BF16), nt_dims, preferred_element_type=F32)
              + d_ref[...] * dy)
        du_ref[...] = du.astype(BF16)
        dd_ref[...] += _rowsum8(dy * u)

    wspec = lambda shp: pl.BlockSpec((None,) + shp, lambda b, i: (b, 0, 0))
    return pl.pallas_call(
        body, grid=(SSM_BLOCKS, nt),
        in_specs=[pl.BlockSpec((tt, WI), lambda b, i: (tb(i), b)),
                  pl.BlockSpec((tt, WI), lambda b, i: (tb(i), u0 + b)),
                  pl.BlockSpec((tt, WS), lambda b, i: (tb(i), b)), pl.BlockSpec((tt, WS), lambda b, i: (tb(i), b)),
                  pl.BlockSpec((8, WS), lambda b, i: (xprev(i), b)), pl.BlockSpec((8, WS), lambda b, i: (xprev(i), b)),
                  wspec((WI, WS)), wspec((WI, WS)), wspec((WS, WI)), wspec((WS, WI)),
                  pl.BlockSpec((8, WS), lambda b, i: (0, b)), pl.BlockSpec((8, WS), lambda b, i: (0, b)),
                  pl.BlockSpec((8, WS), lambda b, i: (0, b)), pl.BlockSpec((8, WS), lambda b, i: (0, b)),
                  pl.BlockSpec((1, WI), lambda b, i: (0, b))],
        out_specs=[pl.BlockSpec((tt, WI), lambda b, i: (tb(i), b)),
                   wspec((WI, WS)), wspec((WI, WS)), wspec((WS, WI)), wspec((WS, WI)),
                   pl.BlockSpec((8, WS), lambda b, i: (0, b)), pl.BlockSpec((8, WS), lambda b, i: (0, b)),
                   pl.BlockSpec((8, WI), lambda b, i: (0, b))],
        out_shape=[jax.ShapeDtypeStruct((T, D_SSM), BF16),
                   jax.ShapeDtypeStruct((SSM_BLOCKS, WI, WS), F32), jax.ShapeDtypeStruct((SSM_BLOCKS, WI, WS), F32),
                   jax.ShapeDtypeStruct((SSM_BLOCKS, WS, WI), F32), jax.ShapeDtypeStruct((SSM_BLOCKS, WS, WI), F32),
                   jax.ShapeDtypeStruct((8, SSM_BLOCKS * WS), F32), jax.ShapeDtypeStruct((8, SSM_BLOCKS * WS), F32),
                   jax.ShapeDtypeStruct((8, D_SSM), F32)],
        scratch_shapes=[pltpu.VMEM((tt, WS), F32), pltpu.VMEM((tt, WS), F32),
                        pltpu.VMEM((tt + 8, WS), F32), pltpu.VMEM((tt + 8, WS), F32),
                        pltpu.VMEM((8, WS), F32), pltpu.VMEM((8, WS), F32)],
        compiler_params=_cparams(("arbitrary", "arbitrary")), name=name,
    )(dy, p, xr, xi, xr, xi, Br, Bi, Cr, Ci, pwc_r, pwc_i, cwc_r, cwc_i, dvec)


def _ssm_discretise(log_step, lam_re, lam_im, b_re, b_im):
    step = jnp.exp(log_step)[:, None]
    mag = jnp.exp(lam_re * step)
    ar = mag * jnp.cos(lam_im * step)
    ai = mag * jnp.sin(lam_im * step)
    den = lam_re * lam_re + lam_im * lam_im
    nr = ar - 1.0
    cr = (nr * lam_re + ai * lam_im) / den
    ci = (ai * lam_re - nr * lam_im) / den
    bbr = cr[..., None] * b_re - ci[..., None] * b_im
    bbi = cr[..., None] * b_im + ci[..., None] * b_re
    return ar, ai, bbr, bbi


def _blockdiag_in(bb):
    t = jnp.transpose(bb, (0, 2, 1)).reshape(SSM_BLOCKS, 8, SSM_GROUP, SSM_STATE)
    eye = jnp.eye(8, dtype=bb.dtype)
    return (t[:, :, :, None, :] * eye[None, :, None, :, None]).reshape(SSM_BLOCKS, SSM_BLOCK_IN, SSM_BLOCK_STATE)


def _blockdiag_out(cc):
    t = jnp.transpose(cc, (0, 2, 1)).reshape(SSM_BLOCKS, 8, SSM_STATE, SSM_GROUP)
    eye = jnp.eye(8, dtype=cc.dtype)
    return (t[:, :, :, None, :] * eye[None, :, None, :, None]).reshape(SSM_BLOCKS, SSM_BLOCK_STATE, SSM_BLOCK_IN)


def _diag_in(d):
    t = d.reshape(SSM_BLOCKS, 8, SSM_GROUP, 8, SSM_STATE)
    t = jnp.einsum("bghgp->bghp", t).reshape(SSM_GROUPS, SSM_GROUP, SSM_STATE)
    return jnp.transpose(t, (0, 2, 1))


def _diag_out(d):
    t = d.reshape(SSM_BLOCKS, 8, SSM_STATE, 8, SSM_GROUP)
    t = jnp.einsum("bgpgh->bgph", t).reshape(SSM_GROUPS, SSM_STATE, SSM_GROUP)
    return jnp.transpose(t, (0, 2, 1))


def _powers(ar, ai):
    rs, is_ = [ar], [ai]
    for _ in range(7):
        r, i = rs[-1], is_[-1]
        rs.append(r * ar - i * ai)
        is_.append(r * ai + i * ar)
    return jnp.stack(rs), jnp.stack(is_), jnp.stack(rs[::-1]), jnp.stack(is_[::-1])


def attn_fwd(q, kv, *, name):
    T, D = q.shape
    nm = kv.shape[0]
    tq = _pick(T, [512, 256, 128])
    scale = HEAD_DIM ** -0.5

    def body(q_ref, k_ref, v_ref, o_ref):
        for h in range(N_HEADS):
            sl = slice(h * HEAD_DIM, (h + 1) * HEAD_DIM)
            s = lax.dot_general(q_ref[:, sl], k_ref[:, sl].astype(BF16), _DIMS["nt"], preferred_element_type=F32) * scale
            e = jnp.exp(s - jnp.max(s, axis=-1, keepdims=True))
            pr = e / jnp.sum(e, axis=-1, keepdims=True)
            o_ref[:, sl] = jnp.dot(pr.astype(BF16), v_ref[:, sl].astype(BF16), preferred_element_type=F32).astype(BF16)

    return pl.pallas_call(
        body, grid=(T // tq,),
        in_specs=[_rows(tq, D), pl.BlockSpec((nm, D), lambda i: (0, 0)), pl.BlockSpec((nm, D), lambda i: (0, 1))],
        out_specs=_rows(tq, D), out_shape=jax.ShapeDtypeStruct((T, D), BF16),
        compiler_params=_cparams(("arbitrary",)), name=name)(q, kv, kv)


def attn_bwd(q, kv, do, *, name):
    T, D = q.shape
    nm = kv.shape[0]
    tq = _pick(T, [512, 256, 128])
    nt = T // tq
    scale = HEAD_DIM ** -0.5

    def body(q_ref, k_ref, v_ref, do_ref, dq_ref, dkv_ref):
        i = pl.program_id(0)

        @pl.when(i == 0)
        def _():
            dkv_ref[...] = jnp.zeros_like(dkv_ref)

        for h in range(N_HEADS):
            sl = slice(h * HEAD_DIM, (h + 1) * HEAD_DIM)
            slv = slice(D + h * HEAD_DIM, D + (h + 1) * HEAD_DIM)
            qh = q_ref[:, sl]
            kh = k_ref[:, sl].astype(BF16)
            vh = v_ref[:, sl].astype(BF16)
            doh = do_ref[:, sl].astype(BF16)
            s = lax.dot_general(qh, kh, _DIMS["nt"], preferred_element_type=F32) * scale
            e = jnp.exp(s - jnp.max(s, axis=-1, keepdims=True))
            pr = e / jnp.sum(e, axis=-1, keepdims=True)
            dp = lax.dot_general(doh, vh, _DIMS["nt"], preferred_element_type=F32)
            ds = (pr * (dp - jnp.sum(pr * dp, axis=-1, keepdims=True)) * scale).astype(BF16)
            dq_ref[:, sl] = jnp.dot(ds, kh, preferred_element_type=F32).astype(BF16)
            dkv_ref[:, sl] += lax.dot_general(ds, qh, _DIMS["tn"], preferred_element_type=F32)
            dkv_ref[:, slv] += lax.dot_general(pr.astype(BF16), doh, _DIMS["tn"], preferred_element_type=F32)

    return pl.pallas_call(
        body, grid=(nt,),
        in_specs=[_rows(tq, D), pl.BlockSpec((nm, D), lambda i: (0, 0)), pl.BlockSpec((nm, D), lambda i: (0, 1)), _rows(tq, D)],
        out_specs=[_rows(tq, D), _const((nm, 2 * D))],
        out_shape=[jax.ShapeDtypeStruct((T, D), BF16), jax.ShapeDtypeStruct((nm, 2 * D), F32)],
        compiler_params=_cparams(("arbitrary",)), name=name)(q, kv, kv, do)


def _adam_math(w, g, m, v):
    m = ADAM_B1 * m + (1.0 - ADAM_B1) * g
    v = ADAM_B2 * v + (1.0 - ADAM_B2) * (g * g)
    m_hat = m / (1.0 - ADAM_B1 ** ADAM_STEP)
    v_hat = v / (1.0 - ADAM_B2 ** ADAM_STEP)
    delta = -ADAM_LR * (m_hat / (jnp.sqrt(v_hat) + ADAM_EPS) + ADAM_WD * w)
    return delta, m, v


def adamw(w, m, v, g_arr, g_row0, *, name):
    R, C = w.shape
    tr = _pick(R, [256, 128, 64, 32, 16, 8])
    assert g_row0 % tr == 0
    g0 = g_row0 // tr

    def body(w_ref, m_ref, v_ref, g_ref, go_ref, d_ref, mo_ref, vo_ref):
        g = g_ref[...]
        d, mn, vn = _adam_math(w_ref[...], g, m_ref[...], v_ref[...])
        go_ref[...] = g
        d_ref[...] = d
        mo_ref[...] = mn
        vo_ref[...] = vn

    sp = pl.BlockSpec((tr, C), lambda i: (i, 0))
    return pl.pallas_call(
        body, grid=(R // tr,), in_specs=[sp, sp, sp, pl.BlockSpec((tr, C), lambda i: (g0 + i, 0))],
        out_specs=[sp] * 4, out_shape=[jax.ShapeDtypeStruct((R, C), F32)] * 4,
        compiler_params=_cparams(("arbitrary",)), name=name)(w, m, v, g_arr)


def _place():
    x, y, c = lax.axis_index("x"), lax.axis_index("y"), lax.axis_index("c")
    chips = [(1 - x, y), (x, 1 - y), (1 - x, 1 - y)]
    return x, y, c, chips


ANY = pl.BlockSpec(memory_space=pl.ANY)


def allgather_weights(bufs, *, name):
    n = len(bufs)

    def body(*refs):
        o_refs = refs[n:2 * n]
        send_sems, recv_sems, fsend_sems, frecv_sems = refs[2 * n:]
        x, y, c, chips = _place()
        k_me = 2 * x + y
        sib = (x, y, 1 - c)
        halves = [b.shape[1] // 2 for b in bufs]

        def half(a, cc):
            return pl.ds(pl.multiple_of(cc * halves[a], 16), halves[a])

        sends = []
        for a in range(n):
            for r, (px, py) in enumerate(chips):
                cp = pltpu.make_async_remote_copy(
                    src_ref=o_refs[a].at[k_me, half(a, c)], dst_ref=o_refs[a].at[k_me, half(a, c)],
                    send_sem=send_sems.at[3 * a + r], recv_sem=recv_sems.at[3 * a + r],
                    device_id=(px, py, c), device_id_type=MESH)
                cp.start()
                sends.append(cp)
        passed = []
        for a in range(n):
            for r, (px, py) in enumerate(chips):
                win = o_refs[a].at[2 * px + py, half(a, c)]
                pltpu.make_async_remote_copy(
                    src_ref=win, dst_ref=win, send_sem=send_sems.at[3 * a + r], recv_sem=recv_sems.at[3 * a + r],
                    device_id=(px, py, c), device_id_type=MESH).wait_recv()
                cp = pltpu.make_async_remote_copy(
                    src_ref=win, dst_ref=win, send_sem=fsend_sems.at[3 * a + r], recv_sem=frecv_sems.at[3 * a + r],
                    device_id=sib, device_id_type=MESH)
                cp.start()
                passed.append(cp)
        for a in range(n):
            for r, (px, py) in enumerate(chips):
                win = o_refs[a].at[2 * px + py, half(a, 1 - c)]
                pltpu.make_async_remote_copy(
                    src_ref=win, dst_ref=win, send_sem=fsend_sems.at[3 * a + r], recv_sem=frecv_sems.at[3 * a + r],
                    device_id=sib, device_id_type=MESH).wait_recv()
        for cp in sends + passed:
            cp.wait_send()

    return pl.pallas_call(
        body, in_specs=[ANY] * n, out_specs=[ANY] * n,
        out_shape=[jax.ShapeDtypeStruct(b.shape, b.dtype) for b in bufs],
        scratch_shapes=[pltpu.SemaphoreType.DMA((3 * n,))] * 4,
        input_output_aliases={a: a for a in range(n)},
        name=name)(*bufs)


HBM_SPEC = pl.BlockSpec(memory_space=pltpu.HBM)
SEM_SPEC = pl.BlockSpec(memory_space=pltpu.SEMAPHORE)


def _hbm(a):
    return pltpu.with_memory_space_constraint(a, pltpu.HBM)


def gather_start(bufs, pieces, *, name):
    n = len(bufs)
    npc = len(pieces)

    def body(*refs):
        b_refs = refs[:n]
        send_sems, recv_sems = refs[n], refs[n + 1]
        x, y, c, chips = _place()
        k_me = 2 * x + y
        for q, (a, row0, rows) in enumerate(pieces):
            win = b_refs[a].at[k_me, pl.ds(row0, rows)]
            for r, (px, py) in enumerate(chips):
                pltpu.make_async_remote_copy(
                    src_ref=win, dst_ref=win, send_sem=send_sems.at[3 * q + r], recv_sem=recv_sems.at[3 * q + r],
                    device_id=(px, py, c), device_id_type=MESH).start()

    return pl.pallas_call(
        body, in_specs=[HBM_SPEC] * n, out_specs=[SEM_SPEC, SEM_SPEC] + [HBM_SPEC] * n,
        out_shape=[pltpu.SemaphoreType.DMA((3 * npc,)), pltpu.SemaphoreType.DMA((3 * npc,))]
        + [pltpu.HBM(b.shape, b.dtype) for b in bufs],
        input_output_aliases={a: 2 + a for a in range(n)},
        compiler_params=pltpu.CompilerParams(has_side_effects=pltpu.SideEffectType.DATAFLOW_SIDE_EFFECTING),
        name=name)(*[_hbm(b) for b in bufs])


def gather_wait(send_sems, recv_sems, bufs, which, after, *, name):
    n = len(bufs)

    def body(*refs):
        b_refs = refs[:n]
        send_sems, recv_sems = refs[n], refs[n + 1]
        x, y, c, chips = _place()
        k_me = 2 * x + y
        for a, row0, rows, q in which:
            for r, (px, py) in enumerate(chips):
                cp = pltpu.make_async_remote_copy(
                    src_ref=b_refs[a].at[k_me, pl.ds(row0, rows)], dst_ref=b_refs[a].at[2 * px + py, pl.ds(row0, rows)],
                    send_sem=send_sems.at[3 * q + r], recv_sem=recv_sems.at[3 * q + r],
                    device_id=(px, py, c), device_id_type=MESH)
                cp.wait_send()
                cp.wait_recv()

    return pl.pallas_call(
        body, in_specs=[HBM_SPEC] * n + [SEM_SPEC, SEM_SPEC, ANY], out_specs=[HBM_SPEC] * n,
        out_shape=[pltpu.HBM(b.shape, b.dtype) for b in bufs],
        input_output_aliases={a: a for a in range(n)},
        compiler_params=pltpu.CompilerParams(has_side_effects=pltpu.SideEffectType.DATAFLOW_SIDE_EFFECTING),
        name=name)(*bufs, send_sems, recv_sems, after)


def exchange_halves(grads, *, name):
    n = len(grads)

    def body(*refs):
        g_refs, l_refs = refs[:n], refs[n:2 * n]
        send_sems, recv_sems = refs[2 * n:]
        x, y, c, _ = _place()
        cps = []
        for a in range(n):
            h = grads[a].shape[1] // 2
            cp = pltpu.make_async_remote_copy(
                src_ref=g_refs[a].at[:, pl.ds(pl.multiple_of((1 - c) * h, 8), h)], dst_ref=l_refs[a],
                send_sem=send_sems.at[a], recv_sem=recv_sems.at[a], device_id=(x, y, 1 - c), device_id_type=MESH)
            cp.start()
            cps.append(cp)
        for cp in cps:
            cp.wait()

    return pl.pallas_call(
        body, in_specs=[ANY] * n, out_specs=[ANY] * n,
        out_shape=[jax.ShapeDtypeStruct((g.shape[0], g.shape[1] // 2, g.shape[2]), g.dtype) for g in grads],
        scratch_shapes=[pltpu.SemaphoreType.DMA((n,))] * 2,
        name=name)(*grads)


N_PEERS = N_DEV - 1


def _scatter_copies(p_refs, l_refs, send_sems, recv_sems):
    x, y, c, _ = _place()
    cps = []
    for a in range(len(p_refs)):
        h = p_refs[a].shape[1] // 2
        for fx, fy in ((0, 0), (1, 0), (0, 1), (1, 1)):
            for fc in (0, 1):
                if (fx, fy, fc) == (0, 0, 0):
                    continue
                slot = 2 * (fx + 2 * fy) + fc - 1
                px, py, pc = (1 - x if fx else x), (1 - y if fy else y), (1 - c if fc else c)
                cps.append(pltpu.make_async_remote_copy(
                    src_ref=p_refs[a].at[2 * px + py, pl.ds(pl.multiple_of(pc * h, 16), h)], dst_ref=l_refs[a].at[slot],
                    send_sem=send_sems.at[N_PEERS * a + slot], recv_sem=recv_sems.at[N_PEERS * a + slot],
                    device_id=(px, py, pc), device_id_type=MESH))
    return cps


def scatter_start(parts, *, name):
    n = len(parts)
    lands = [lax.empty((N_PEERS, p.shape[1] // 2, p.shape[2]), p.dtype) for p in parts]

    def body(*refs):
        for cp in _scatter_copies(refs[:n], refs[n:2 * n], refs[2 * n], refs[2 * n + 1]):
            cp.start()

    outs = pl.pallas_call(
        body, in_specs=[HBM_SPEC] * (2 * n), out_specs=[SEM_SPEC, SEM_SPEC] + [HBM_SPEC] * (2 * n),
        out_shape=[pltpu.SemaphoreType.DMA((N_PEERS * n,)), pltpu.SemaphoreType.DMA((N_PEERS * n,))]
        + [pltpu.HBM(a.shape, a.dtype) for a in parts + lands],
        input_output_aliases={a: 2 + a for a in range(2 * n)},
        compiler_params=pltpu.CompilerParams(has_side_effects=pltpu.SideEffectType.DATAFLOW_SIDE_EFFECTING),
        name=name)(*[_hbm(a) for a in parts + lands])
    return outs[0], outs[1], list(outs[2:2 + n]), list(outs[2 + n:])


def scatter_wait(rounds, after, *, name):
    sizes = [len(r[2]) for r in rounds]
    flat = [a for r in rounds for a in r[2] + r[3]]
    sems = [s for r in rounds for s in (r[0], r[1])]
    nflat = len(flat)

    def body(*refs):
        pos = 0
        for ri, n in enumerate(sizes):
            for cp in _scatter_copies(refs[pos:pos + n], refs[pos + n:pos + 2 * n], refs[nflat + 2 * ri], refs[nflat + 2 * ri + 1]):
                cp.wait_send()
                cp.wait_recv()
            pos += 2 * n

    outs = pl.pallas_call(
        body, in_specs=[HBM_SPEC] * nflat + [SEM_SPEC] * len(sems) + [ANY], out_specs=[HBM_SPEC] * nflat,
        out_shape=[pltpu.HBM(a.shape, a.dtype) for a in flat],
        input_output_aliases={a: a for a in range(nflat)},
        compiler_params=pltpu.CompilerParams(has_side_effects=pltpu.SideEffectType.DATAFLOW_SIDE_EFFECTING),
        name=name)(*flat, *sems, after)
    res, pos = [], 0
    for n in sizes:
        res.append((list(outs[pos:pos + n]), list(outs[pos + n:pos + 2 * n])))
        pos += 2 * n
    return res


def join_halves(fulls, *, name):
    n = len(fulls)

    def body(*refs):
        o_refs = refs[n:2 * n]
        send_sems, recv_sems = refs[2 * n:]
        x, y, c, _ = _place()
        cps = []
        for a in range(n):
            h = fulls[a].shape[0] // 2
            win = o_refs[a].at[pl.ds(pl.multiple_of(c * h, 8), h)]
            cp = pltpu.make_async_remote_copy(
                src_ref=win, dst_ref=win, send_sem=send_sems.at[a], recv_sem=recv_sems.at[a],
                device_id=(x, y, 1 - c), device_id_type=MESH)
            cp.start()
            cps.append(cp)
        for a in range(n):
            h = fulls[a].shape[0] // 2
            other = o_refs[a].at[pl.ds(pl.multiple_of((1 - c) * h, 8), h)]
            pltpu.make_async_remote_copy(
                src_ref=other, dst_ref=other, send_sem=send_sems.at[a], recv_sem=recv_sems.at[a],
                device_id=(x, y, 1 - c), device_id_type=MESH).wait_recv()
        for cp in cps:
            cp.wait_send()

    return pl.pallas_call(
        body, in_specs=[ANY] * n, out_specs=[ANY] * n,
        out_shape=[jax.ShapeDtypeStruct(f.shape, f.dtype) for f in fulls],
        scratch_shapes=[pltpu.SemaphoreType.DMA((n,))] * 2,
        input_output_aliases={a: a for a in range(n)},
        name=name)(*fulls)


def add_partials(part, land, kc, *, name):
    _, R, C = part.shape
    H = R // 2
    tr = _pick(H, [256, 128, 64, 32, 16])
    per = H // tr

    def body(kc_ref, p_ref, l_ref, o_ref):
        acc = p_ref[...].astype(F32)
        for s in range(N_PEERS):
            acc = acc + l_ref[s].astype(F32)
        o_ref[...] = acc

    return pl.pallas_call(
        body,
        grid_spec=pltpu.PrefetchScalarGridSpec(
            num_scalar_prefetch=1, grid=(per,),
            in_specs=[pl.BlockSpec((None, tr, C), lambda i, kc_ref: (kc_ref[0], kc_ref[1] * per + i, 0)),
                      pl.BlockSpec((N_PEERS, tr, C), lambda i, kc_ref: (0, i, 0))],
            out_specs=pl.BlockSpec((tr, C), lambda i, kc_ref: (kc_ref[1] * per + i, 0))),
        out_shape=jax.ShapeDtypeStruct((R, C), F32),
        compiler_params=_cparams(("arbitrary",)), name=name)(kc, part, land)


def allgather_sum(v, *, name):
    m_per, n = v.shape

    def body(x_ref, out_ref, sum_ref, send_sems, recv_sems, local_sem):
        x, y, c, chips = _place()
        me, sibling = (x, y, c), (x, y, 1 - c)

        def rows(px, py, pc):
            return out_ref.at[pl.ds(pl.multiple_of((4 * px + 2 * py + pc) * m_per, 8), m_per), :]

        def copy(k, block, to, src=None):
            return pltpu.make_async_remote_copy(
                src_ref=rows(*block) if src is None else src, dst_ref=rows(*block),
                send_sem=send_sems.at[k], recv_sem=recv_sems.at[k], device_id=to, device_id_type=MESH)

        mine = pltpu.make_async_copy(x_ref, rows(*me), local_sem)
        mine.start()
        first = [copy(0, me, sibling, src=x_ref)]
        first += [copy(1 + j, me, (*chip, c), src=x_ref) for j, chip in enumerate(chips)]
        for cp in first:
            cp.start()
        passed = [copy(4 + j, (*chip, c), sibling) for j, chip in enumerate(chips)]
        for j, chip in enumerate(chips):
            copy(1 + j, (*chip, c), me).wait_recv()
            passed[j].start()
        copy(0, sibling, me).wait_recv()
        for j, chip in enumerate(chips):
            copy(4 + j, (*chip, 1 - c), me).wait_recv()
        for cp in first + passed:
            cp.wait_send()
        mine.wait()
        acc = out_ref[0:m_per, :]
        for d in range(1, N_DEV):
            acc = acc + out_ref[d * m_per:(d + 1) * m_per, :]
        sum_ref[...] = acc

    vm = pl.BlockSpec(memory_space=pltpu.VMEM)
    return pl.pallas_call(
        body, in_specs=[vm], out_specs=[vm, vm],
        out_shape=[jax.ShapeDtypeStruct((N_DEV * m_per, n), v.dtype), jax.ShapeDtypeStruct((m_per, n), v.dtype)],
        scratch_shapes=[pltpu.SemaphoreType.DMA((7,)), pltpu.SemaphoreType.DMA((7,)), pltpu.SemaphoreType.DMA],
        compiler_params=pltpu.CompilerParams(vmem_limit_bytes=VMEM_LIMIT_BYTES), name=name)(v)


def allreduce_two_level(v, *, name):
    m, n = v.shape
    h = m // 2

    def body(x_ref, out_ref, sib_ref, chip_ref, sems_send, sems_recv):
        x, y, c, chips = _place()
        k_me = 2 * x + y
        sib = (x, y, 1 - c)
        mine = pl.ds(pl.multiple_of(c * h, 8), h)
        other = pl.ds(pl.multiple_of((1 - c) * h, 8), h)

        def copy(q, src, dst, to):
            return pltpu.make_async_remote_copy(src_ref=src, dst_ref=dst, send_sem=sems_send.at[q], recv_sem=sems_recv.at[q],
                                                device_id=to, device_id_type=MESH)

        first = copy(0, x_ref.at[other], sib_ref, sib)
        first.start()
        first.wait()
        chip_ref[k_me] = x_ref[mine, :] + sib_ref[...]
        sends = [copy(1 + r, chip_ref.at[k_me], chip_ref.at[k_me], (px, py, c)) for r, (px, py) in enumerate(chips)]
        for cp in sends:
            cp.start()
        for r, (px, py) in enumerate(chips):
            copy(1 + r, chip_ref.at[2 * px + py], chip_ref.at[2 * px + py], (px, py, c)).wait_recv()
        for cp in sends:
            cp.wait_send()
        total = ((chip_ref[0] + chip_ref[1]) + chip_ref[2]) + chip_ref[3]
        out_ref[mine, :] = total
        last = copy(4, out_ref.at[mine], out_ref.at[mine], sib)
        last.start()
        copy(4, out_ref.at[other], out_ref.at[other], sib).wait_recv()
        last.wait_send()

    vm = pl.BlockSpec(memory_space=pltpu.VMEM)
    return pl.pallas_call(
        body, in_specs=[vm], out_specs=vm, out_shape=jax.ShapeDtypeStruct((m, n), v.dtype),
        scratch_shapes=[pltpu.VMEM((h, n), v.dtype), pltpu.VMEM((N_CHIPS, h, n), v.dtype),
                        pltpu.SemaphoreType.DMA((5,)), pltpu.SemaphoreType.DMA((5,))],
        compiler_params=pltpu.CompilerParams(vmem_limit_bytes=VMEM_LIMIT_BYTES), name=name)(v)


def _pack(arrs):
    cols = []
    for a in arrs:
        f = a.reshape(-1)
        pad = (-f.shape[0]) % 128
        cols.append(jnp.pad(f, (0, pad)).reshape(-1, 128))
    out = jnp.concatenate(cols, axis=0)
    pad = (-out.shape[0]) % 8
    return jnp.pad(out, ((0, pad), (0, 0)))


def _unpack(buf, shapes):
    outs, r = [], 0
    for s in shapes:
        nel = math.prod(s)
        nr = -(-nel // 128)
        outs.append(buf[r:r + nr].reshape(-1)[:nel].reshape(s))
        r += nr
    return outs


GA_CONV_OUT, GA_MIX_OUT, GA_WQ, GA_WO, GA_DOWN, GA_UP, GA_ROWS = 0, 256, 512, 768, 1024, 2048, 3072
G1_DOWN, G1_UP, G1_ROWS = 0, 1024, 2048
G2_CONV_OUT, G2_MIX_OUT, G2_WQ, G2_WO, G2_ROWS = 0, 256, 512, 768, 1024


def kernel(x, mem, in_norm_g, in_norm_b, w_in, conv_dw, conv_db, conv_norm_g, conv_norm_b, w_conv_out, ssm_log_step, ssm_lambda_re, ssm_lambda_im, ssm_b_re, ssm_b_im, ssm_c_re, ssm_c_im, ssm_d, w_ssm_glu, w_mix_out, ln1_g, ln1_b, xa_wq, xa_wkv, xa_wo, ln2_g, ln2_b, mlp_w_up, mlp_w_down, ln3_g, ln3_b, loss_target, m_in_norm_g, m_in_norm_b, m_w_in, m_conv_dw, m_conv_db, m_conv_norm_g, m_conv_norm_b, m_w_conv_out, m_ssm_log_step, m_ssm_lambda_re, m_ssm_lambda_im, m_ssm_b_re, m_ssm_b_im, m_ssm_c_re, m_ssm_c_im, m_ssm_d, m_w_ssm_glu, m_w_mix_out, m_ln1_g, m_ln1_b, m_xa_wq, m_xa_wkv, m_xa_wo, m_ln2_g, m_ln2_b, m_mlp_w_up, m_mlp_w_down, m_ln3_g, m_ln3_b, v_in_norm_g, v_in_norm_b, v_w_in, v_conv_dw, v_conv_db, v_conv_norm_g, v_conv_norm_b, v_w_conv_out, v_ssm_log_step, v_ssm_lambda_re, v_ssm_lambda_im, v_ssm_b_re, v_ssm_b_im, v_ssm_c_re, v_ssm_c_im, v_ssm_d, v_w_ssm_glu, v_w_mix_out, v_ln1_g, v_ln1_b, v_xa_wq, v_xa_wkv, v_xa_wo, v_ln2_g, v_ln2_b, v_mlp_w_up, v_mlp_w_down, v_ln3_g, v_ln3_b):
    D = D_MODEL
    xs = x[0]
    T = xs.shape[0]
    mems = mem[0]
    NM = mems.shape[0]
    tgt = loss_target[0]
    my_c = lax.axis_index("c")
    k_me = 2 * lax.axis_index("x") + lax.axis_index("y")
    c_arr = jnp.reshape(my_c, (1,)).astype(jnp.int32)
    k_arr = jnp.reshape(k_me, (1,)).astype(jnp.int32)

    sh_a = jnp.concatenate([w_conv_out[0], w_mix_out[0], xa_wq[0], xa_wo[0], mlp_w_down[0], mlp_w_up[0]], axis=0).astype(BF16)
    def own_block(shard):
        buf = jnp.zeros((N_CHIPS,) + shard.shape, shard.dtype)
        return lax.dynamic_update_slice(buf, shard[None], (k_me, 0, 0))

    dw_pad = jnp.pad(conv_dw[0], ((0, CONV_HALO - CONV_K), (0, 0)))
    (GIN,) = allgather_weights([own_block(w_in[0].astype(BF16))], name="gather_w_in")
    ag_bufs = [own_block(sh_a), GIN] + [own_block(s) for s in (xa_wkv[0].astype(BF16), w_ssm_glu[0].astype(BF16), dw_pad)]
    ag_pieces = [(4, 0, CONV_HALO), (0, GA_CONV_OUT, 256), (3, 0, D_SSM), (0, GA_MIX_OUT, 256), (0, GA_WQ, 256),
                 (2, 0, D), (0, GA_WO, 256), (0, GA_UP, D), (0, GA_DOWN, D)]
    ag_send, ag_recv, GA, GIN, GKV, GGLU, GDW = gather_start(ag_bufs, ag_pieces, name="gather_start")

    def w_rowshard(row0):
        return dict(b_spec=((N_CHIPS, 256, D), lambda i, j, k: (0, row0 // 256, 0)), b_view=(D, D), tn=D, tk=D)

    _, h0, h0b = ln_fwd(xs, in_norm_g, in_norm_b, name="ln0_fwd")
    p = mm_nn(h0b, GIN, ((None, D, 1152), lambda i, j, k: (j, 0, 0)), D_IN, tm=_pick(T, [512, 256, 128]), tn=1152, tk=D,
              name="mm_w_in")[0]
    GA, GGLU, GDW = gather_wait(
        ag_send, ag_recv, [GA, GGLU, GDW],
        [(2, 0, CONV_HALO, 0), (0, GA_CONV_OUT, 256, 1), (1, 0, D_SSM, 2), (0, GA_MIX_OUT, 256, 3)], p, name="gather_wait_mixer")
    dw_taps = jnp.transpose(GDW, (1, 0, 2)).reshape(CONV_HALO, D)
    c_pre, actb = conv_fwd(p, dw_taps, conv_db, conv_norm_g[0].reshape(1, D), conv_norm_b[0].reshape(1, D), name="conv_fwd")
    ya = mm_nn(actb, GA, N=D, name="mm_conv_out", **w_rowshard(GA_CONV_OUT))[0]

    lstep, lre, lim = ssm_log_step[0], ssm_lambda_re[0], ssm_lambda_im[0]
    bre, bim, cre, cim = ssm_b_re[0], ssm_b_im[0], ssm_c_re[0], ssm_c_im[0]
    (ar, ai, bbr, bbi), disc_vjp = jax.vjp(_ssm_discretise, lstep, lre, lim, bre, bim)
    Br, Bi = _blockdiag_in(bbr), _blockdiag_in(bbi)
    Cr, Ci = _blockdiag_out(cre), _blockdiag_out(cim)
    pw_r, pw_i, pwrev_r, pwrev_i = _powers(ar.reshape(-1), ai.reshape(-1))
    dvec = ssm_d[0].reshape(1, D_SSM)
    xr, xi, yssm = ssm_fwd(p, Br, Bi, Cr, Ci, pw_r, pw_i, dvec, name="ssm_fwd")
    z = mm_nn(yssm, GGLU, ((None, D_SSM, 512), lambda i, j, k: (j, 0, 0)), 2 * D, tn=512, tk=D_SSM, name="mm_ssm_glu")[0]
    mergedb = merge_fwd(p, ya, z, name="merge_fwd")
    tm_ln = _pick(T, [512, 256, 128])
    row_spec = ((1, D), lambda i, j, k: (0, 0))

    def ln_epilogue(acc, res, g, b):
        r = ALPHA * res + acc
        xhat, _ = _ln_stats(r)
        h = xhat * g + b
        return r, h, h

    def mm_ln(a, row0, res, g, b, name):
        return mm_nn(a, GA, N=D, tm=tm_ln, extras=(res, g.reshape(1, D), b.reshape(1, D)),
                     extra_specs=[_mn(tm_ln, D), row_spec, row_spec], epilogue=ln_epilogue, out_dtypes=[F32, F32, BF16],
                     name=name, **w_rowshard(row0))

    r1, h1, h1b = mm_ln(mergedb, GA_MIX_OUT, h0, ln1_g[0], ln1_b[0], "mm_mix_out_ln1")
    GA, GKV = gather_wait(ag_send, ag_recv, [GA, GKV], [(0, GA_WQ, 256, 4), (1, 0, D, 5), (0, GA_WO, 256, 6)], r1,
                          name="gather_wait_attn")

    qb = mm_nn(h1b, GA, N=D, out_dtype=BF16, name="mm_wq", **w_rowshard(GA_WQ))[0]
    kv = mm_nn(mems, GKV, ((None, D, 512), lambda i, j, k: (j, 0, 0)), 2 * D, tn=512, tk=D, name="mm_wkv")[0]
    ob = attn_fwd(qb, kv, name="attn_fwd")
    r2, h2, h2b = mm_ln(ob, GA_WO, h1, ln2_g[0], ln2_b[0], "mm_wo_ln2")
    (GA,) = gather_wait(ag_send, ag_recv, [GA], [(0, GA_UP, D, 7), (0, GA_DOWN, D, 8)], r2, name="gather_wait_mlp")

    def relu2(acc):
        zr = jnp.maximum(acc, 0.0)
        return acc, zr * zr

    zpre, zzb = mm_nn(h2b, GA, ((None, D, D), lambda i, j, k: (j, GA_UP // D, 0)), D_FF, tm=tm_ln, tn=D, tk=D,
                      out_dtypes=[F32, BF16], epilogue=relu2, name="mm_up")
    ff = mm_nn(zzb, GA, ((N_CHIPS, D, D), lambda i, j, k: (0, GA_DOWN // D, 0)), D, tm=_pick(T, [512, 256, 128]), tn=D, tk=D_FF,
               b_view=(D_FF, D), name="mm_down")[0]
    dr3, dr3b, dg3, db3, sq = ln_loss_bwd(ff, h2, ln3_g[0], ln3_b[0], tgt, name="ln3_loss_bwd")

    def rs_begin(grads, rnd):
        return scatter_start(grads, name=f"rs{rnd}_scatter_start")

    g1_shape = jax.ShapeDtypeStruct((N_CHIPS, G1_ROWS, D), BF16)
    g2_shape = jax.ShapeDtypeStruct((N_CHIPS, G2_ROWS, D), BF16)
    dzpreb = mm_nt(dr3b, GA, ((None, D, D), lambda i, j, k: (j, GA_DOWN // D, 0)), D_FF, tm=tm_ln, tn=D, tk=D, out_dtype=BF16,
                   extras=(zpre,), epilogue=lambda acc, zp: (acc * (2.0 * jnp.maximum(zp, 0.0)),), name="mm_down_t")[0]
    G1g = mm_tn(zzb, dr3b, tm=D, tn=D, out_spec=((None, D, D), lambda i, j, k: (i, G1_DOWN // D, 0)), out_shape=g1_shape,
                name="mm_down_g")
    G1g = mm_tn(h2b, dzpreb, tm=D, tn=D, out_spec=((None, D, D), lambda i, j, k: (j, G1_UP // D, 0)), out_shape=g1_shape,
                out_buf=G1g, name="mm_up_g")
    round1 = rs_begin([G1g], 1)
    dh2 = mm_nt(dzpreb, GA, ((N_CHIPS, D, D), lambda i, j, k: (0, GA_UP // D, 0)), D, tm=_pick(T, [512, 256, 128]), tn=D,
                tk=D_FF, b_chunks=N_CHIPS, extras=(dr3,), epilogue=lambda acc, d: (acc + ALPHA * d,),
                after=(round1[2][0],), name="mm_up_t")[0]
    dr2, dr2b, dg2, db2 = ln_bwd(r2, dh2, ln2_g[0], name="ln2_bwd")

    def g_rowshard(row0, out_buf):
        return dict(tm=D, tn=D, out_spec=((N_CHIPS, 256, D), lambda i, j, k: (0, row0 // 256, 0)), out_shape=g2_shape,
                    out_buf=out_buf)

    dob = mm_nt(dr2b, GA, N=D, out_dtype=BF16, name="mm_wo_t", **w_rowshard(GA_WO))[0]
    G2g = mm_tn(ob, dr2b, name="mm_wo_g", **g_rowshard(G2_WO, None))
    dqb, dkv = attn_bwd(qb, kv, dob, name="attn_bwd")
    G2g = mm_tn(h1b, dqb, name="mm_wq_g", **g_rowshard(G2_WQ, G2g))
    GKVg = mm_tn(mems, dkv, tm=D, tn=512, tk=NM, out_spec=((None, D, 512), lambda i, j, k: (j, 0, 0)),
                 out_shape=jax.ShapeDtypeStruct((N_CHIPS, D, 512), BF16), name="mm_wkv_g")
    dh1 = mm_nt(dqb, GA, N=D, extras=(dr2,), epilogue=lambda acc, d: (acc + ALPHA * d,), name="mm_wq_t",
                **w_rowshard(GA_WQ))[0]
    dr1, dr1b, dg1, db1 = ln_bwd(r1, dh1, ln1_g[0], name="ln1_bwd")

    dmerged = mm_nt(dr1b, GA, N=D, name="mm_mix_t", **w_rowshard(GA_MIX_OUT))[0]
    G2g = mm_tn(mergedb, dr1b, name="mm_mix_g", **g_rowshard(G2_MIX_OUT, G2g))
    dyab, dgab, dgbb, dz1b, dz2b = merge_bwd(dmerged, p, ya, z, name="merge_bwd")
    dzb = jnp.concatenate([dz1b, dz2b], axis=1)
    GGLUg = mm_tn(yssm, dzb, tm=D_SSM, tn=512, out_spec=((None, D_SSM, 512), lambda i, j, k: (j, 0, 0)),
                  out_shape=jax.ShapeDtypeStruct((N_CHIPS, D_SSM, 512), BF16), name="mm_glu_g")
    dyssm = mm_nt(dzb, GGLU, ((None, D_SSM, 512), lambda i, j, k: (k, 0, 0)), D_SSM, tn=D_SSM, tk=512, name="mm_glu_t")[0]
    dub, dBr, dBi, dCr, dCi, dar8, dai8, dd8 = ssm_bwd(dyssm, p, xr, xi, Br, Bi, Cr, Ci, pw_r, -pw_i, pwrev_r, -pwrev_i, dvec,
                                                       name="ssm_bwd")
    dar = jnp.sum(dar8, axis=0).reshape(SSM_GROUPS, SSM_STATE)
    dai = jnp.sum(dai8, axis=0).reshape(SSM_GROUPS, SSM_STATE)
    g_lstep, g_lre, g_lim, g_bre, g_bim = disc_vjp((dar, dai, _diag_in(dBr), _diag_in(dBi)))
    g_cre, g_cim = _diag_out(dCr), _diag_out(dCi)
    g_d = jnp.sum(dd8, axis=0).reshape(1, D_SSM)

    dact = mm_nt(dyab, GA, N=D, name="mm_conv_out_t", **w_rowshard(GA_CONV_OUT))[0]
    G2g = mm_tn(actb, dyab, name="mm_conv_out_g", **g_rowshard(G2_CONV_OUT, G2g))
    round2 = rs_begin([G2g, GKVg, GGLUg], 2)
    dc, dng, dnb, ddb = conv_bwd_norm(dact, c_pre, conv_norm_g[0].reshape(1, D), conv_norm_b[0].reshape(1, D),
                                      round2[2][0], name="conv_bwd_norm")
    dvgb, ddw = conv_bwd_taps(dc, p, dw_taps, name="conv_bwd_taps")
    dpb = jnp.concatenate([dvgb, dub, dgab, dgbb], axis=1)
    GINg = mm_tn(h0b, dpb, tm=D, tn=1152, out_spec=((None, D, 1152), lambda i, j, k: (j, 0, 0)),
                 out_shape=jax.ShapeDtypeStruct((N_CHIPS, D, 1152), BF16), name="mm_w_in_g")
    round3 = rs_begin([GINg], 3)
    dh0 = mm_nt(dpb, GIN, ((N_CHIPS, D, 1152), lambda i, j, k: (0, 0, 0)), D, tm=_pick(T, [512, 256, 128]), tn=D, tk=D_IN,
                b_chunks=N_CHIPS, extras=(dr1,), epilogue=lambda acc, d: (acc + ALPHA * d,), after=(round3[2][0],),
                name="mm_w_in_t")[0]
    gx, _, dg0, db0 = ln_bwd(xs, dh0, in_norm_g, name="ln0_bwd")

    kc_arr = jnp.concatenate([k_arr, c_arr])
    landed = scatter_wait([round1, round2, round3], gx, name="rs_scatter_wait")
    tags = ["mlp", "sq", "kv", "glu", "in"]
    pairs = [(pt, l2) for parts, lands2 in landed for pt, l2 in zip(parts, lands2)]
    halves = [add_partials(pt, l2, kc_arr, name="rs_add_partials_" + t) for (pt, l2), t in zip(pairs, tags)]
    g1, g2, gKV, gGLU, gIN = join_halves(halves, name="rs_join_halves")

    small_names = ["in_norm_g", "in_norm_b", "conv_db", "conv_norm_g", "conv_norm_b", "ssm_log_step", "ssm_lambda_re",
                   "ssm_lambda_im", "ssm_b_re", "ssm_b_im", "ssm_c_re", "ssm_c_im", "ssm_d", "ln1_g", "ln1_b",
                   "ln2_g", "ln2_b", "ln3_g", "ln3_b"]
    small_w = [in_norm_g, in_norm_b, conv_db, conv_norm_g, conv_norm_b, ssm_log_step, ssm_lambda_re, ssm_lambda_im,
               ssm_b_re, ssm_b_im, ssm_c_re, ssm_c_im, ssm_d, ln1_g, ln1_b, ln2_g, ln2_b, ln3_g, ln3_b]
    small_m = [m_in_norm_g, m_in_norm_b, m_conv_db, m_conv_norm_g, m_conv_norm_b, m_ssm_log_step, m_ssm_lambda_re,
               m_ssm_lambda_im, m_ssm_b_re, m_ssm_b_im, m_ssm_c_re, m_ssm_c_im, m_ssm_d, m_ln1_g, m_ln1_b, m_ln2_g,
               m_ln2_b, m_ln3_g, m_ln3_b]
    small_v = [v_in_norm_g, v_in_norm_b, v_conv_db, v_conv_norm_g, v_conv_norm_b, v_ssm_log_step, v_ssm_lambda_re,
               v_ssm_lambda_im, v_ssm_b_re, v_ssm_b_im, v_ssm_c_re, v_ssm_c_im, v_ssm_d, v_ln1_g, v_ln1_b, v_ln2_g,
               v_ln2_b, v_ln3_g, v_ln3_b]
    small_g = [dg0, db0, ddb, dng, dnb, g_lstep, g_lre, g_lim, g_bre, g_bim, g_cre, g_cim, g_d, dg1, db1, dg2, db2, dg3, db3]
    small_shapes = [w.shape for w in small_w]
    n_small_rows = _pack(small_w).shape[0]
    packed_g = _pack(small_g + [ddw, sq])
    packed_g = jnp.pad(packed_g, ((0, (-packed_g.shape[0]) % 16), (0, 0)))
    summed = allreduce_two_level(packed_g, name="allreduce_small")
    small_rows = sum(-(-math.prod(s) // 128) for s in small_shapes)
    dw_rows = CONV_HALO * D // 128
    loss = 0.5 * summed[small_rows + dw_rows, 0] / D
    ddw_full = summed[small_rows:small_rows + dw_rows].reshape(CONV_HALO, D)
    g_dw = lax.dynamic_slice_in_dim(ddw_full, k_me * (D // N_CHIPS), D // N_CHIPS, axis=1)
    gs_packed = jnp.pad(summed[:small_rows], ((0, n_small_rows - small_rows), (0, 0)))

    res = {}

    def upd(nm, w, m, v, g_arr, row0=0):
        shp = w.shape
        w2, m2, v2 = (a.reshape(-1, shp[-1]) for a in (w, m, v))
        outs = adamw(w2, m2, v2, g_arr, row0, name="adamw_" + nm)
        res[nm] = tuple(o.reshape(shp) for o in outs)

    upd("w_conv_out", w_conv_out, m_w_conv_out, v_w_conv_out, g2, G2_CONV_OUT)
    upd("w_mix_out", w_mix_out, m_w_mix_out, v_w_mix_out, g2, G2_MIX_OUT)
    upd("xa_wq", xa_wq, m_xa_wq, v_xa_wq, g2, G2_WQ)
    upd("xa_wo", xa_wo, m_xa_wo, v_xa_wo, g2, G2_WO)
    upd("mlp_w_down", mlp_w_down, m_mlp_w_down, v_mlp_w_down, g1, G1_DOWN)
    upd("mlp_w_up", mlp_w_up, m_mlp_w_up, v_mlp_w_up, g1, G1_UP)
    upd("w_in", w_in, m_w_in, v_w_in, gIN)
    upd("xa_wkv", xa_wkv, m_xa_wkv, v_xa_wkv, gKV)
    upd("w_ssm_glu", w_ssm_glu, m_w_ssm_glu, v_w_ssm_glu, gGLU)
    pad_dw = lambda a: jnp.pad(a[0], ((0, CONV_HALO - CONV_K), (0, 0)))
    dw_outs = adamw(pad_dw(conv_dw), pad_dw(m_conv_dw), pad_dw(v_conv_dw), g_dw, 0, name="adamw_conv_dw")
    res["conv_dw"] = tuple(o[:CONV_K][None] for o in dw_outs)
    sm_outs = adamw(_pack(small_w), _pack(small_m), _pack(small_v), gs_packed, 0, name="adamw_small")
    sm_un = [_unpack(o, small_shapes) for o in sm_outs]
    for idx, nm in enumerate(small_names):
        res[nm] = tuple(sm_un[q][idx] for q in range(4))

    order = ["in_norm_g", "in_norm_b", "w_in", "conv_dw", "conv_db", "conv_norm_g", "conv_norm_b", "w_conv_out",
             "ssm_log_step", "ssm_lambda_re", "ssm_lambda_im", "ssm_b_re", "ssm_b_im", "ssm_c_re", "ssm_c_im", "ssm_d",
             "w_ssm_glu", "w_mix_out", "ln1_g", "ln1_b", "xa_wq", "xa_wkv", "xa_wo", "ln2_g", "ln2_b", "mlp_w_up",
             "mlp_w_down", "ln3_g", "ln3_b"]
    return (loss, gx[None], *[res[n][0] for n in order], *[res[n][1] for n in order],
            *[res[n][2] for n in order], *[res[n][3] for n in order])
```

```python
import functools
import math

import jax
import jax.numpy as jnp
from jax import lax
from jax.experimental import pallas as pl
from jax.experimental.pallas import tpu as pltpu

F32 = jnp.float32
BF16 = jnp.bfloat16
MESH = pl.DeviceIdType.MESH

D_MODEL = 1024
N_HEADS = 4
HEAD_DIM = D_MODEL // N_HEADS
CONV_K = 31
CONV_HALO = 32
D_SSM = 512
SSM_GROUPS = 32
SSM_GROUP = 16
SSM_STATE = 64
SSM_BLOCKS = 4
SSM_BLOCK_IN = D_SSM // SSM_BLOCKS
SSM_BLOCK_STATE = SSM_GROUPS * SSM_STATE // SSM_BLOCKS
D_FF = 4096
D_IN = 4608
LN_EPS = 1e-5
ALPHA = (2.0 * 1) ** 0.25
N_CHIPS = 4
N_DEV = 8
ADAM_LR, ADAM_B1, ADAM_B2, ADAM_EPS, ADAM_WD, ADAM_STEP = 0.001, 0.9, 0.999, 1e-08, 0.01, 10
VMEM_LIMIT_BYTES = 56 * 1024 * 1024


def _pick(dim, cands):
    for c in cands:
        if dim % c == 0:
            return c
    return dim


def _cparams(sem=None):
    return pltpu.CompilerParams(dimension_semantics=sem, vmem_limit_bytes=VMEM_LIMIT_BYTES)


def _sigmoid(x):
    return 1.0 / (1.0 + jnp.exp(-x))


_DIMS = {"nn": (((1,), (0,)), ((), ())), "nt": (((1,), (1,)), ((), ())), "tn": (((0,), (0,)), ((), ()))}


def matmul(a, b, *, mode, M, N, K, tm, tn, tk, a_spec, b_spec, out_specs, out_shapes, name,
           extras=(), extra_specs=(), epilogue=None, alias_buf=None, b_view=None, after=(), b_chunks=None):
    nk = K // tk
    ne = len(extras)
    no = len(out_shapes)
    na = (0 if alias_buf is None else 1) + len(after)
    dims = _DIMS[mode]

    def body(*refs):
        a_ref, b_ref = refs[0], refs[1]
        e_refs = refs[2:2 + ne]
        o_refs = refs[2 + ne + na:2 + ne + na + no]

        def finish(acc):
            outs = (acc,) if epilogue is None else epilogue(acc, *[r[...] for r in e_refs])
            for o, r in zip(outs, o_refs):
                r[...] = o.astype(r.dtype).reshape(r.shape)

        if b_chunks:
            kc = a_ref.shape[1] // b_chunks
            prod = None
            for q in range(b_chunks):
                part = lax.dot_general(a_ref[:, q * kc:(q + 1) * kc].astype(BF16), b_ref[q].astype(BF16), dims,
                                       preferred_element_type=F32)
                prod = part if prod is None else prod + part
        else:
            b_blk = b_ref[...] if b_view is None else b_ref[...].reshape(b_view)
            prod = lax.dot_general(a_ref[...].astype(BF16), b_blk.astype(BF16), dims, preferred_element_type=F32)
        if nk == 1:
            finish(prod)
        else:
            acc_ref = refs[-1]
            k = pl.program_id(2)

            @pl.when(k == 0)
            def _():
                acc_ref[...] = prod

            @pl.when(k > 0)
            def _():
                acc_ref[...] += prod

            @pl.when(k == nk - 1)
            def _():
                finish(acc_ref[...])

    in_specs = [pl.BlockSpec(*a_spec), pl.BlockSpec(*b_spec)] + [pl.BlockSpec(*s) for s in extra_specs]
    ins = [a, b, *extras]
    if alias_buf is not None:
        in_specs.append(pl.BlockSpec(memory_space=pl.ANY))
        ins.append(alias_buf)
    for dep in after:
        in_specs.append(pl.BlockSpec(memory_space=pl.ANY))
        ins.append(dep)
    res = pl.pallas_call(
        body,
        grid=(M // tm, N // tn, nk),
        in_specs=in_specs,
        out_specs=[pl.BlockSpec(*s) for s in out_specs],
        out_shape=out_shapes,
        scratch_shapes=[] if nk == 1 else [pltpu.VMEM((tm, tn), F32)],
        input_output_aliases={2 + ne: 0} if alias_buf is not None else {},
        compiler_params=_cparams(("parallel", "parallel", "arbitrary")),
        name=name,
    )(*ins)
    return res


def _mn(tm, tn):
    return ((tm, tn), lambda i, j, k: (i, j))


def mm_nn(a, b_arr, b_spec, N, *, name, tm=None, tn, tk, out_dtype=F32, extras=(), epilogue=None, out_dtypes=None,
          b_view=None, extra_specs=None):
    M, K = a.shape
    tm = tm or _pick(M, [1024, 512, 256, 128])
    dts = out_dtypes or [out_dtype]
    return matmul(a, b_arr, mode="nn", M=M, N=N, K=K, tm=tm, tn=tn, tk=tk,
                  a_spec=((tm, tk), lambda i, j, k: (i, k)), b_spec=b_spec, b_view=b_view,
                  out_specs=[_mn(tm, tn)] * len(dts), out_shapes=[jax.ShapeDtypeStruct((M, N), d) for d in dts],
                  extras=extras, extra_specs=extra_specs or [_mn(tm, tn)] * len(extras), epilogue=epilogue, name=name)


def mm_nt(a, b_arr, b_spec, N, *, name, tm=None, tn, tk, out_dtype=F32, extras=(), epilogue=None, out_dtypes=None,
          b_view=None, after=(), b_chunks=None):
    M, K = a.shape
    tm = tm or _pick(M, [1024, 512, 256, 128])
    dts = out_dtypes or [out_dtype]
    return matmul(a, b_arr, mode="nt", M=M, N=N, K=K, tm=tm, tn=tn, tk=tk, after=after, b_chunks=b_chunks,
                  a_spec=((tm, tk), lambda i, j, k: (i, k)), b_spec=b_spec, b_view=b_view,
                  out_specs=[_mn(tm, tn)] * len(dts), out_shapes=[jax.ShapeDtypeStruct((M, N), d) for d in dts],
                  extras=extras, extra_specs=[_mn(tm, tn)] * len(extras), epilogue=epilogue, name=name)


def mm_tn(a, b, *, name, tm, tn, tk=None, out_spec, out_shape, out_buf=None):
    K, M = a.shape
    N = b.shape[1]
    tk = tk or _pick(K, [2048, 1024, 512, 256, 128])
    return matmul(a, b, mode="tn", M=M, N=N, K=K, tm=tm, tn=tn, tk=tk,
                  a_spec=((tk, tm), lambda i, j, k: (k, i)), b_spec=((tk, tn), lambda i, j, k: (k, j)),
                  out_specs=[out_spec], out_shapes=[out_shape], alias_buf=out_buf, name=name)[0]


def _rows(tc, w, cb=0):
    return pl.BlockSpec((tc, w), lambda i: (i, cb))


def _const(shape):
    return pl.BlockSpec(shape, lambda i: tuple([0] * len(shape)))


def _ln_stats(r):
    mu = jnp.mean(r, axis=-1, keepdims=True)
    xc = r - mu
    var = jnp.mean(xc * xc, axis=-1, keepdims=True)
    rstd = lax.rsqrt(var + LN_EPS)
    return xc * rstd, rstd


def _rowsum8(v):
    tc, w = v.shape
    return jnp.sum(v.reshape(tc // 8, 8, w), axis=0)


def ln_fwd(x, g, b, *, name, res=None):
    T, D = x.shape
    tc = _pick(T, [512, 256, 128])
    has_res = res is not None

    def body(*refs):
        if has_res:
            x_ref, res_ref, g_ref, b_ref, r_ref, h_ref, hb_ref = refs
            r = ALPHA * res_ref[...] + x_ref[...]
            r_ref[...] = r
        else:
            x_ref, g_ref, b_ref, h_ref, hb_ref = refs
            r = x_ref[...]
        xhat, _ = _ln_stats(r)
        y = xhat * g_ref[...] + b_ref[...]
        h_ref[...] = y
        hb_ref[...] = y.astype(BF16)

    ins = [x] + ([res] if has_res else []) + [g.reshape(1, D), b.reshape(1, D)]
    in_specs = [_rows(tc, D)] * (2 if has_res else 1) + [_const((1, D))] * 2
    n_out = 3 if has_res else 2
    outs = pl.pallas_call(
        body, grid=(T // tc,), in_specs=in_specs, out_specs=[_rows(tc, D)] * n_out,
        out_shape=[jax.ShapeDtypeStruct((T, D), F32)] * (n_out - 1) + [jax.ShapeDtypeStruct((T, D), BF16)],
        compiler_params=_cparams(("arbitrary",)), name=name)(*ins)
    if has_res:
        return outs
    return (x,) + tuple(outs)


def ln_bwd(r, dy, g, *, name):
    T, D = r.shape
    tc = _pick(T, [512, 256, 128])
    nt = T // tc

    def body(r_ref, dy_ref, g_ref, dr_ref, drb_ref, dg_ref, db_ref, accg, accb):
        i = pl.program_id(0)

        @pl.when(i == 0)
        def _():
            accg[...] = jnp.zeros_like(accg)
            accb[...] = jnp.zeros_like(accb)

        xhat, rstd = _ln_stats(r_ref[...])
        dy = dy_ref[...]
        dxh = dy * g_ref[...]
        m1 = jnp.mean(dxh, axis=-1, keepdims=True)
        m2 = jnp.mean(dxh * xhat, axis=-1, keepdims=True)
        dr = rstd * (dxh - m1 - xhat * m2)
        dr_ref[...] = dr
        drb_ref[...] = dr.astype(BF16)
        accg[...] += _rowsum8(dy * xhat)
        accb[...] += _rowsum8(dy)

        @pl.when(i == nt - 1)
        def _():
            dg_ref[...] = jnp.sum(accg[...], axis=0, keepdims=True)
            db_ref[...] = jnp.sum(accb[...], axis=0, keepdims=True)

    return pl.pallas_call(
        body, grid=(nt,), in_specs=[_rows(tc, D), _rows(tc, D), _const((1, D))],
        out_specs=[_rows(tc, D), _rows(tc, D), _const((1, D)), _const((1, D))],
        out_shape=[jax.ShapeDtypeStruct((T, D), F32), jax.ShapeDtypeStruct((T, D), BF16),
                   jax.ShapeDtypeStruct((1, D), F32), jax.ShapeDtypeStruct((1, D), F32)],
        scratch_shapes=[pltpu.VMEM((8, D), F32), pltpu.VMEM((8, D), F32)],
        compiler_params=_cparams(("arbitrary",)), name=name)(r, dy, g.reshape(1, D))


def ln_loss_bwd(x, res, g, b, target, *, name):
    T, D = x.shape
    tc = _pick(T, [512, 256, 128])
    nt = T // tc

    def body(x_ref, res_ref, g_ref, b_ref, t_ref, dr_ref, drb_ref, dg_ref, db_ref, loss_ref, accg, accb, accl):
        i = pl.program_id(0)

        @pl.when(i == 0)
        def _():
            accg[...] = jnp.zeros_like(accg)
            accb[...] = jnp.zeros_like(accb)
            accl[...] = jnp.zeros_like(accl)

        r = ALPHA * res_ref[...] + x_ref[...]
        xhat, rstd = _ln_stats(r)
        e = xhat * g_ref[...] + b_ref[...] - t_ref[...]
        dy = e * (1.0 / D)
        dxh = dy * g_ref[...]
        m1 = jnp.mean(dxh, axis=-1, keepdims=True)
        m2 = jnp.mean(dxh * xhat, axis=-1, keepdims=True)
        dr = rstd * (dxh - m1 - xhat * m2)
        dr_ref[...] = dr
        drb_ref[...] = dr.astype(BF16)
        accg[...] += _rowsum8(dy * xhat)
        accb[...] += _rowsum8(dy)
        accl[...] += _rowsum8(e * e)

        @pl.when(i == nt - 1)
        def _():
            dg_ref[...] = jnp.sum(accg[...], axis=0, keepdims=True)
            db_ref[...] = jnp.sum(accb[...], axis=0, keepdims=True)
            s = jnp.sum(jnp.sum(accl[...], axis=0, keepdims=True), axis=1, keepdims=True)
            loss_ref[...] = jnp.broadcast_to(s, (1, 128))

    return pl.pallas_call(
        body, grid=(nt,), in_specs=[_rows(tc, D), _rows(tc, D), _const((1, D)), _const((1, D)), _rows(tc, D)],
        out_specs=[_rows(tc, D), _rows(tc, D), _const((1, D)), _const((1, D)), _const((1, 128))],
        out_shape=[jax.ShapeDtypeStruct((T, D), F32), jax.ShapeDtypeStruct((T, D), BF16),
                   jax.ShapeDtypeStruct((1, D), F32), jax.ShapeDtypeStruct((1, D), F32), jax.ShapeDtypeStruct((1, 128), F32)],
        scratch_shapes=[pltpu.VMEM((8, D), F32)] * 3,
        compiler_params=_cparams(("arbitrary",)), name=name)(x, res, g.reshape(1, D), b.reshape(1, D), target)


def _halo_prev(tc):
    per = tc // CONV_HALO
    return lambda i: jnp.maximum(i * per - 1, 0)


CONV_ROWS = 32
CONV_TAP_GROUP = 4
CONV_TILE_UNROLL = 4


def _fill_shifts(S, nrows):
    for b in range(1, 8):
        S[b, 0:nrows - 8, :] = S[0, b:b + nrows - 8, :]


def _tap_sum(S, w_ref, offs, r0, nrows):
    acc = None
    for k, o in enumerate(offs):
        a, b = divmod(o, 8)
        term = w_ref[k:k + 1, :] * S[b, pl.ds(pl.multiple_of(r0 + 8 * a, 8), nrows), :]
        acc = term if acc is None else acc + term
    return acc


def conv_fwd(p, dw, db, ng, nb, *, name):
    T = p.shape[0]
    D = D_MODEL
    tc = _pick(T, [256, 128])
    prev = _halo_prev(tc)
    off = CONV_HALO - (CONV_K - 1)
    offs = [off + k for k in range(CONV_K)]

    def body(val_ref, gate_ref, valp_ref, gatep_ref, dw_ref, db_ref, ng_ref, nb_ref, c_ref, act_ref, S):
        i = pl.program_id(0)
        u_prev = valp_ref[...] * _sigmoid(gatep_ref[...])
        S[0, 0:CONV_HALO, :] = jnp.where(i > 0, u_prev, 0.0)
        S[0, CONV_HALO:CONV_HALO + tc, :] = val_ref[...] * _sigmoid(gate_ref[...])
        _fill_shifts(S, CONV_HALO + tc)

        def rows(j, carry):
            r0 = pl.multiple_of(j * CONV_ROWS, CONV_ROWS)
            c_ref[pl.ds(r0, CONV_ROWS), :] = _tap_sum(S, dw_ref, offs, r0, CONV_ROWS) + db_ref[...]
            return carry

        lax.fori_loop(0, tc // CONV_ROWS, rows, 0)
        c = c_ref[...]
        xhat, _ = _ln_stats(c)
        cn = xhat * ng_ref[...] + nb_ref[...]
        act_ref[...] = (cn * _sigmoid(cn)).astype(BF16)

    return pl.pallas_call(
        body, grid=(T // tc,),
        in_specs=[_rows(tc, D, 0), _rows(tc, D, 1),
                  pl.BlockSpec((CONV_HALO, D), lambda i: (prev(i), 0)), pl.BlockSpec((CONV_HALO, D), lambda i: (prev(i), 1)),
                  _const((CONV_HALO, D)), _const((1, D)), _const((1, D)), _const((1, D))],
        out_specs=[_rows(tc, D), _rows(tc, D)],
        out_shape=[jax.ShapeDtypeStruct((T, D), F32), jax.ShapeDtypeStruct((T, D), BF16)],
        scratch_shapes=[pltpu.VMEM((8, CONV_HALO + tc, D), F32)],
        compiler_params=_cparams(("arbitrary",)), name=name)(p, p, p, p, dw, db, ng, nb)


def conv_bwd_norm(dact, c_pre, ng, nb, after, *, name):
    T, D = c_pre.shape
    tc = _pick(T, [512, 256, 128])
    nt = T // tc

    def body(da_ref, c_ref, ng_ref, nb_ref, after_ref, dc_ref, dng_ref, dnb_ref, ddb_ref, accg, accb, accd):
        i = pl.program_id(0)

        @pl.when(i == 0)
        def _():
            accg[...] = jnp.zeros_like(accg)
            accb[...] = jnp.zeros_like(accb)
            accd[...] = jnp.zeros_like(accd)

        xhat, rstd = _ln_stats(c_ref[...])
        cn = xhat * ng_ref[...] + nb_ref[...]
        s = _sigmoid(cn)
        dcn = da_ref[...] * (s * (1.0 + cn * (1.0 - s)))
        dxh = dcn * ng_ref[...]
        m1 = jnp.mean(dxh, axis=-1, keepdims=True)
        m2 = jnp.mean(dxh * xhat, axis=-1, keepdims=True)
        dc = rstd * (dxh - m1 - xhat * m2)
        dc_ref[...] = dc
        accg[...] += _rowsum8(dcn * xhat)
        accb[...] += _rowsum8(dcn)
        accd[...] += _rowsum8(dc)

        @pl.when(i == nt - 1)
        def _():
            dng_ref[...] = jnp.sum(accg[...], axis=0, keepdims=True)
            dnb_ref[...] = jnp.sum(accb[...], axis=0, keepdims=True)
            ddb_ref[...] = jnp.sum(accd[...], axis=0, keepdims=True)

    vec = jax.ShapeDtypeStruct((1, D), F32)
    return pl.pallas_call(
        body, grid=(nt,), in_specs=[_rows(tc, D), _rows(tc, D), _const((1, D)), _const((1, D)), ANY],
        out_specs=[_rows(tc, D), _const((1, D)), _const((1, D)), _const((1, D))],
        out_shape=[jax.ShapeDtypeStruct((T, D), F32), vec, vec, vec],
        scratch_shapes=[pltpu.VMEM((8, D), F32)] * 3,
        compiler_params=_cparams(("arbitrary",)), name=name)(dact, c_pre, ng, nb, after)


def conv_bwd_taps(dc, p, dw, *, name):
    T, D = dc.shape
    tc = _pick(T, [256, 128])
    nt = T // tc
    per = tc // CONV_HALO
    prev = _halo_prev(tc)
    last_halo = T // CONV_HALO - 1
    nxt = lambda i: jnp.minimum((i + 1) * per, last_halo)
    off = CONV_HALO - (CONV_K - 1)

    def body(dc_ref, dcn_ref, val_ref, gate_ref, valp_ref, gatep_ref, dw_ref, dvg_ref, ddw_ref, ext_u, ext_d, acc):
        i = pl.program_id(0)

        @pl.when(i == 0)
        def _():
            acc[...] = jnp.zeros_like(acc)

        u_prev = valp_ref[...] * _sigmoid(gatep_ref[...])
        ext_u[0, 0:CONV_HALO, :] = jnp.where(i > 0, u_prev, 0.0)
        ext_u[0, CONV_HALO:CONV_HALO + tc, :] = val_ref[...] * _sigmoid(gate_ref[...])
        ext_d[0, 0:tc, :] = dc_ref[...]
        ext_d[0, tc:tc + CONV_HALO, :] = jnp.where(i < nt - 1, dcn_ref[...], 0.0)
        _fill_shifts(ext_u, CONV_HALO + tc)
        _fill_shifts(ext_d, CONV_HALO + tc)

        def rows(j, carry):
            r0 = pl.multiple_of(j * CONV_ROWS, CONV_ROWS)
            sl = pl.ds(r0, CONV_ROWS)
            du = _tap_sum(ext_d, dw_ref, [CONV_K - 1 - k for k in range(CONV_K)], r0, CONV_ROWS)
            sg = _sigmoid(gate_ref[sl, :])
            dvg_ref[sl, 0:D] = (du * sg).astype(BF16)
            dvg_ref[sl, D:2 * D] = (du * val_ref[sl, :] * sg * (1.0 - sg)).astype(BF16)
            return carry

        lax.fori_loop(0, tc // CONV_ROWS, rows, 0)

        for k0 in range(0, CONV_K, CONV_TAP_GROUP):
            ks = list(range(k0, min(k0 + CONV_TAP_GROUP, CONV_K)))

            def taps(j, accs, ks=ks):
                out = list(accs)
                for t in range(CONV_TILE_UNROLL):
                    r0 = pl.multiple_of((j * CONV_TILE_UNROLL + t) * 8, 8)
                    dct = dc_ref[pl.ds(r0, 8), :]
                    for q, k in enumerate(ks):
                        a, b = divmod(off + k, 8)
                        out[q] = out[q] + dct * ext_u[b, pl.ds(pl.multiple_of(r0 + 8 * a, 8), 8), :]
                return tuple(out)

            accs = lax.fori_loop(0, tc // (8 * CONV_TILE_UNROLL), taps, tuple(jnp.zeros((8, D), F32) for _ in ks))
            for k, a_k in zip(ks, accs):
                acc[k] += a_k

        @pl.when(i == nt - 1)
        def _():
            ddw_ref[...] = jnp.zeros_like(ddw_ref)
            for k in range(CONV_K):
                ddw_ref[k:k + 1, :] = jnp.sum(acc[k], axis=0, keepdims=True)

    return pl.pallas_call(
        body, grid=(nt,),
        in_specs=[_rows(tc, D), pl.BlockSpec((CONV_HALO, D), lambda i: (nxt(i), 0)),
                  _rows(tc, D, 0), _rows(tc, D, 1),
                  pl.BlockSpec((CONV_HALO, D), lambda i: (prev(i), 0)), pl.BlockSpec((CONV_HALO, D), lambda i: (prev(i), 1)),
                  _const((CONV_HALO, D))],
        out_specs=[_rows(tc, 2 * D), _const((CONV_HALO, D))],
        out_shape=[jax.ShapeDtypeStruct((T, 2 * D), BF16), jax.ShapeDtypeStruct((CONV_HALO, D), F32)],
        scratch_shapes=[pltpu.VMEM((8, CONV_HALO + tc, D), F32), pltpu.VMEM((8, CONV_HALO + tc, D), F32),
                        pltpu.VMEM((CONV_K, 8, D), F32)],
        compiler_params=_cparams(("arbitrary",)), name=name)(dc, dc, p, p, p, p, dw)


GATE_A0 = (2 * D_MODEL + D_SSM) // 512
GATE_B0 = GATE_A0 + 2


def merge_fwd(p, ya, z, *, name):
    T = p.shape[0]
    D = D_MODEL
    tc = _pick(T, [512, 256, 128])
    W = 512

    def body(ga_ref, gb_ref, ya_ref, z1_ref, z2_ref, o_ref):
        yb = z1_ref[...] * _sigmoid(z2_ref[...])
        o_ref[...] = (_sigmoid(ga_ref[...]) * ya_ref[...] + _sigmoid(gb_ref[...]) * yb).astype(BF16)

    return pl.pallas_call(
        body, grid=(T // tc, D // W),
        in_specs=[pl.BlockSpec((tc, W), lambda i, j: (i, GATE_A0 + j)), pl.BlockSpec((tc, W), lambda i, j: (i, GATE_B0 + j)),
                  pl.BlockSpec((tc, W), lambda i, j: (i, j)), pl.BlockSpec((tc, W), lambda i, j: (i, j)),
                  pl.BlockSpec((tc, W), lambda i, j: (i, D // W + j))],
        out_specs=pl.BlockSpec((tc, W), lambda i, j: (i, j)),
        out_shape=jax.ShapeDtypeStruct((T, D), BF16),
        compiler_params=_cparams(("arbitrary", "arbitrary")), name=name)(p, p, ya, z, z)


def merge_bwd(dm, p, ya, z, *, name):
    T = p.shape[0]
    D = D_MODEL
    tc = _pick(T, [512, 256, 128])
    W = 512
    nb = D // W

    def body(dm_ref, ga_ref, gb_ref, ya_ref, z1_ref, z2_ref, dya_ref, dga_ref, dgb_ref, dz1_ref, dz2_ref):
        dm = dm_ref[...]
        sa = _sigmoid(ga_ref[...])
        sb = _sigmoid(gb_ref[...])
        s2 = _sigmoid(z2_ref[...])
        z1 = z1_ref[...]
        yb = z1 * s2
        dya_ref[...] = (dm * sa).astype(BF16)
        dga_ref[...] = (dm * ya_ref[...] * sa * (1.0 - sa)).astype(BF16)
        dgb_ref[...] = (dm * yb * sb * (1.0 - sb)).astype(BF16)
        dyb = dm * sb
        dz1_ref[...] = (dyb * s2).astype(BF16)
        dz2_ref[...] = (dyb * z1 * s2 * (1.0 - s2)).astype(BF16)

    blk = lambda off: pl.BlockSpec((tc, W), lambda i, j: (i, off + j))
    dya, dga, dgb, dz1, dz2 = pl.pallas_call(
        body, grid=(T // tc, nb),
        in_specs=[blk(0), blk(GATE_A0), blk(GATE_B0), blk(0), blk(0), blk(nb)],
        out_specs=[blk(0)] * 5,
        out_shape=[jax.ShapeDtypeStruct((T, D), BF16)] * 5,
        compiler_params=_cparams(("arbitrary", "arbitrary")), name=name)(dm, p, p, ya, z, z)
    return dya, dga, dgb, dz1, dz2


def _scan_block(src_r, src_i, dst_r, dst_i, car_r, car_i, pw_r, pw_i, cw_r, cw_i, ntiles, reverse, extra=None):
    W = src_r.shape[1]
    rows = lax.broadcasted_iota(jnp.int32, (8, W), 0)
    steps = []
    for d, pr in ((1, 0), (2, 1), (4, 3)):
        valid = rows < 8 - d if reverse else rows >= d
        steps.append((d, jnp.where(valid, jnp.broadcast_to(pw_r[pr:pr + 1, :], (8, W)), 0.0),
                      jnp.where(valid, jnp.broadcast_to(pw_i[pr:pr + 1, :], (8, W)), 0.0)))
    cw_r, cw_i = cw_r[...], cw_i[...]

    def tile(jj, carry):
        j = ntiles - 1 - jj if reverse else jj
        sl = pl.ds(pl.multiple_of(j * 8, 8), 8)
        xr, xi = src_r[sl, :], src_i[sl, :]
        for d, lr, li in steps:
            sr = pltpu.roll(xr, 8 - d if reverse else d, 0)
            si = pltpu.roll(xi, 8 - d if reverse else d, 0)
            xr, xi = xr + lr * sr - li * si, xi + lr * si + li * sr
        cr, ci = car_r[...], car_i[...]
        xr, xi = xr + cw_r * cr - cw_i * ci, xi + cw_r * ci + cw_i * cr
        dst_r[sl, :] = xr
        dst_i[sl, :] = xi
        edge = 0 if reverse else 7
        car_r[...] = jnp.broadcast_to(xr[edge:edge + 1, :], (8, W))
        car_i[...] = jnp.broadcast_to(xi[edge:edge + 1, :], (8, W))
        if extra is not None:
            carry = extra(j, xr, xi, carry)
        return carry

    return tile


def ssm_fwd(p, Br, Bi, Cr, Ci, pw_r, pw_i, dvec, *, name):
    T = p.shape[0]
    tt = _pick(T, [512, 256, 128])
    nt = T // tt
    WI, WS = SSM_BLOCK_IN, SSM_BLOCK_STATE
    u0 = 2 * D_MODEL // WI

    def body(u_ref, br_ref, bi_ref, cr_ref, ci_ref, pwr_ref, pwi_ref, d_ref, xr_ref, xi_ref, y_ref, bur, bui, car_r, car_i):
        i = pl.program_id(1)

        @pl.when(i == 0)
        def _():
            car_r[...] = jnp.zeros_like(car_r)
            car_i[...] = jnp.zeros_like(car_i)

        u = u_ref[...]
        ub = u.astype(BF16)
        bur[...] = jnp.dot(ub, br_ref[...].astype(BF16), preferred_element_type=F32)
        bui[...] = jnp.dot(ub, bi_ref[...].astype(BF16), preferred_element_type=F32)
        tile = _scan_block(bur, bui, xr_ref, xi_ref, car_r, car_i, pwr_ref, pwi_ref, pwr_ref, pwi_ref, tt // 8, False)
        lax.fori_loop(0, tt // 8, tile, 0)
        y = (jnp.dot(xr_ref[...].astype(BF16), cr_ref[...].astype(BF16), preferred_element_type=F32)
             - jnp.dot(xi_ref[...].astype(BF16), ci_ref[...].astype(BF16), preferred_element_type=F32)
             + d_ref[...] * u)
        y_ref[...] = y.astype(BF16)

    return pl.pallas_call(
        body, grid=(SSM_BLOCKS, nt),
        in_specs=[pl.BlockSpec((tt, WI), lambda b, i: (i, u0 + b)),
                  pl.BlockSpec((None, WI, WS), lambda b, i: (b, 0, 0)), pl.BlockSpec((None, WI, WS), lambda b, i: (b, 0, 0)),
                  pl.BlockSpec((None, WS, WI), lambda b, i: (b, 0, 0)), pl.BlockSpec((None, WS, WI), lambda b, i: (b, 0, 0)),
                  pl.BlockSpec((8, WS), lambda b, i: (0, b)), pl.BlockSpec((8, WS), lambda b, i: (0, b)),
                  pl.BlockSpec((1, WI), lambda b, i: (0, b))],
        out_specs=[pl.BlockSpec((tt, WS), lambda b, i: (i, b)), pl.BlockSpec((tt, WS), lambda b, i: (i, b)),
                   pl.BlockSpec((tt, WI), lambda b, i: (i, b))],
        out_shape=[jax.ShapeDtypeStruct((T, SSM_BLOCKS * WS), F32)] * 2 + [jax.ShapeDtypeStruct((T, D_SSM), BF16)],
        scratch_shapes=[pltpu.VMEM((tt, WS), F32), pltpu.VMEM((tt, WS), F32), pltpu.VMEM((8, WS), F32), pltpu.VMEM((8, WS), F32)],
        compiler_params=_cparams(("arbitrary", "arbitrary")), name=name)(p, Br, Bi, Cr, Ci, pw_r, pw_i, dvec)


def ssm_bwd(dy, p, xr, xi, Br, Bi, Cr, Ci, pwc_r, pwc_i, cwc_r, cwc_i, dvec, *, name):
    T = p.shape[0]
    tt = _pick(T, [512, 256, 128])
    nt = T // tt
    WI, WS = SSM_BLOCK_IN, SSM_BLOCK_STATE
    u0 = 2 * D_MODEL // WI
    tb = lambda i: nt - 1 - i
    xprev = lambda i: jnp.maximum(tb(i) * (tt // 8) - 1, 0)
    tn_dims = _DIMS["tn"]
    nt_dims = _DIMS["nt"]

    def body(dy_ref, u_ref, xr_ref, xi_ref, xpr_ref, xpi_ref, br_ref, bi_ref, cr_ref, ci_ref, pwr_ref, pwi_ref,
             cwr_ref, cwi_ref, d_ref,
             du_ref, dbr_ref, dbi_ref, dcr_ref, dci_ref, dar_ref, dai_ref, dd_ref,
             gr, gi, ext_r, ext_i, car_r, car_i):
        i = pl.program_id(1)

        @pl.when(i == 0)
        def _():
            car_r[...] = jnp.zeros_like(car_r)
            car_i[...] = jnp.zeros_like(car_i)
            dbr_ref[...] = jnp.zeros_like(dbr_ref)
            dbi_ref[...] = jnp.zeros_like(dbi_ref)
            dcr_ref[...] = jnp.zeros_like(dcr_ref)
            dci_ref[...] = jnp.zeros_like(dci_ref)
            dar_ref[...] = jnp.zeros_like(dar_ref)
            dai_ref[...] = jnp.zeros_like(dai_ref)
            dd_ref[...] = jnp.zeros_like(dd_ref)

        dy = dy_ref[...]
        dyb = dy.astype(BF16)
        u = u_ref[...]
        ub = u.astype(BF16)
        gr[...] = lax.dot_general(dyb, cr_ref[...].astype(BF16), nt_dims, preferred_element_type=F32)
        gi[...] = -lax.dot_general(dyb, ci_ref[...].astype(BF16), nt_dims, preferred_element_type=F32)
        first = tb(i) == 0
        ext_r[0:8, :] = jnp.where(first, 0.0, xpr_ref[...])
        ext_i[0:8, :] = jnp.where(first, 0.0, xpi_ref[...])
        ext_r[8:8 + tt, :] = xr_ref[...]
        ext_i[8:8 + tt, :] = xi_ref[...]
        rows = lax.broadcasted_iota(jnp.int32, (8, WS), 0)

        def lam_grad(j, g_r, g_i, carry):
            a_r, a_i = carry
            cur = pl.ds(pl.multiple_of(j * 8 + 8, 8), 8)
            prv = pl.ds(pl.multiple_of(j * 8, 8), 8)
            xc_r, xc_i = ext_r[cur, :], ext_i[cur, :]
            xl_r, xl_i = ext_r[prv, :], ext_i[prv, :]
            xp_r = jnp.where(rows == 0, jnp.broadcast_to(xl_r[7:8, :], (8, WS)), pltpu.roll(xc_r, 1, 0))
            xp_i = jnp.where(rows == 0, jnp.broadcast_to(xl_i[7:8, :], (8, WS)), pltpu.roll(xc_i, 1, 0))
            return (a_r + g_r * xp_r + g_i * xp_i, a_i + g_i * xp_r - g_r * xp_i)

        tile = _scan_block(gr, gi, gr, gi, car_r, car_i, pwr_ref, pwi_ref, cwr_ref, cwi_ref, tt // 8, True, extra=lam_grad)
        z8 = jnp.zeros((8, WS), F32)
        a_r, a_i = lax.fori_loop(0, tt // 8, tile, (z8, z8))
        dar_ref[...] += a_r
        dai_ref[...] += a_i
        grb = gr[...].astype(BF16)
        gib = gi[...].astype(BF16)
        dbr_ref[...] += lax.dot_general(ub, grb, tn_dims, preferred_element_type=F32)
        dbi_ref[...] += lax.dot_general(ub, gib, tn_dims, preferred_element_type=F32)
        dcr_ref[...] += lax.dot_general(xr_ref[...].astype(BF16), dyb, tn_dims, preferred_element_type=F32)
        dci_ref[...] -= lax.dot_general(xi_ref[...].astype(BF16), dyb, tn_dims, preferred_element_type=F32)
        du = (lax.dot_general(grb, br_ref[...].astype(BF16), nt_dims, preferred_element_type=F32)
              + lax.dot_general(gib, bi_ref[...].astype(BF16), nt_dims, preferred_element_type=F32)
              + d_ref[...] * dy)
        du_ref[...] = du.astype(BF16)
        dd_ref[...] += _rowsum8(dy * u)

    wspec = lambda shp: pl.BlockSpec((None,) + shp, lambda b, i: (b, 0, 0))
    return pl.pallas_call(
        body, grid=(SSM_BLOCKS, nt),
        in_specs=[pl.BlockSpec((tt, WI), lambda b, i: (tb(i), b)),
                  pl.BlockSpec((tt, WI), lambda b, i: (tb(i), u0 + b)),
                  pl.BlockSpec((tt, WS), lambda b, i: (tb(i), b)), pl.BlockSpec((tt, WS), lambda b, i: (tb(i), b)),
                  pl.BlockSpec((8, WS), lambda b, i: (xprev(i), b)), pl.BlockSpec((8, WS), lambda b, i: (xprev(i), b)),
                  wspec((WI, WS)), wspec((WI, WS)), wspec((WS, WI)), wspec((WS, WI)),
                  pl.BlockSpec((8, WS), lambda b, i: (0, b)), pl.BlockSpec((8, WS), lambda b, i: (0, b)),
                  pl.BlockSpec((8, WS), lambda b, i: (0, b)), pl.BlockSpec((8, WS), lambda b, i: (0, b)),
                  pl.BlockSpec((1, WI), lambda b, i: (0, b))],
        out_specs=[pl.BlockSpec((tt, WI), lambda b, i: (tb(i), b)),
                   wspec((WI, WS)), wspec((WI, WS)), wspec((WS, WI)), wspec((WS, WI)),
                   pl.BlockSpec((8, WS), lambda b, i: (0, b)), pl.BlockSpec((8, WS), lambda b, i: (0, b)),
                   pl.BlockSpec((8, WI), lambda b, i: (0, b))],
        out_shape=[jax.ShapeDtypeStruct((T, D_SSM), BF16),
                   jax.ShapeDtypeStruct((SSM_BLOCKS, WI, WS), F32), jax.ShapeDtypeStruct((SSM_BLOCKS, WI, WS), F32),
                   jax.ShapeDtypeStruct((SSM_BLOCKS, WS, WI), F32), jax.ShapeDtypeStruct((SSM_BLOCKS, WS, WI), F32),
                   jax.ShapeDtypeStruct((8, SSM_BLOCKS * WS), F32), jax.ShapeDtypeStruct((8, SSM_BLOCKS * WS), F32),
                   jax.ShapeDtypeStruct((8, D_SSM), F32)],
        scratch_shapes=[pltpu.VMEM((tt, WS), F32), pltpu.VMEM((tt, WS), F32),
                        pltpu.VMEM((tt + 8, WS), F32), pltpu.VMEM((tt + 8, WS), F32),
                        pltpu.VMEM((8, WS), F32), pltpu.VMEM((8, WS), F32)],
        compiler_params=_cparams(("arbitrary", "arbitrary")), name=name,
    )(dy, p, xr, xi, xr, xi, Br, Bi, Cr, Ci, pwc_r, pwc_i, cwc_r, cwc_i, dvec)


def _ssm_discretise(log_step, lam_re, lam_im, b_re, b_im):
    step = jnp.exp(log_step)[:, None]
    mag = jnp.exp(lam_re * step)
    ar = mag * jnp.cos(lam_im * step)
    ai = mag * jnp.sin(lam_im * step)
    den = lam_re * lam_re + lam_im * lam_im
    nr = ar - 1.0
    cr = (nr * lam_re + ai * lam_im) / den
    ci = (ai * lam_re - nr * lam_im) / den
    bbr = cr[..., None] * b_re - ci[..., None] * b_im
    bbi = cr[..., None] * b_im + ci[..., None] * b_re
    return ar, ai, bbr, bbi


def _blockdiag_in(bb):
    t = jnp.transpose(bb, (0, 2, 1)).reshape(SSM_BLOCKS, 8, SSM_GROUP, SSM_STATE)
    eye = jnp.eye(8, dtype=bb.dtype)
    return (t[:, :, :, None, :] * eye[None, :, None, :, None]).reshape(SSM_BLOCKS, SSM_BLOCK_IN, SSM_BLOCK_STATE)


def _blockdiag_out(cc):
    t = jnp.transpose(cc, (0, 2, 1)).reshape(SSM_BLOCKS, 8, SSM_STATE, SSM_GROUP)
    eye = jnp.eye(8, dtype=cc.dtype)
    return (t[:, :, :, None, :] * eye[None, :, None, :, None]).reshape(SSM_BLOCKS, SSM_BLOCK_STATE, SSM_BLOCK_IN)


def _diag_in(d):
    t = d.reshape(SSM_BLOCKS, 8, SSM_GROUP, 8, SSM_STATE)
    t = jnp.einsum("bghgp->bghp", t).reshape(SSM_GROUPS, SSM_GROUP, SSM_STATE)
    return jnp.transpose(t, (0, 2, 1))


def _diag_out(d):
    t = d.reshape(SSM_BLOCKS, 8, SSM_STATE, 8, SSM_GROUP)
    t = jnp.einsum("bgpgh->bgph", t).reshape(SSM_GROUPS, SSM_STATE, SSM_GROUP)
    return jnp.transpose(t, (0, 2, 1))


def _powers(ar, ai):
    rs, is_ = [ar], [ai]
    for _ in range(7):
        r, i = rs[-1], is_[-1]
        rs.append(r * ar - i * ai)
        is_.append(r * ai + i * ar)
    return jnp.stack(rs), jnp.stack(is_), jnp.stack(rs[::-1]), jnp.stack(is_[::-1])


def attn_fwd(q, kv, *, name):
    T, D = q.shape
    nm = kv.shape[0]
    tq = _pick(T, [512, 256, 128])
    scale = HEAD_DIM ** -0.5

    def body(q_ref, k_ref, v_ref, o_ref):
        for h in range(N_HEADS):
            sl = slice(h * HEAD_DIM, (h + 1) * HEAD_DIM)
            s = lax.dot_general(q_ref[:, sl], k_ref[:, sl].astype(BF16), _DIMS["nt"], preferred_element_type=F32) * scale
            e = jnp.exp(s - jnp.max(s, axis=-1, keepdims=True))
            pr = e / jnp.sum(e, axis=-1, keepdims=True)
            o_ref[:, sl] = jnp.dot(pr.astype(BF16), v_ref[:, sl].astype(BF16), preferred_element_type=F32).astype(BF16)

    return pl.pallas_call(
        body, grid=(T // tq,),
        in_specs=[_rows(tq, D), pl.BlockSpec((nm, D), lambda i: (0, 0)), pl.BlockSpec((nm, D), lambda i: (0, 1))],
        out_specs=_rows(tq, D), out_shape=jax.ShapeDtypeStruct((T, D), BF16),
        compiler_params=_cparams(("arbitrary",)), name=name)(q, kv, kv)


def attn_bwd(q, kv, do, *, name):
    T, D = q.shape
    nm = kv.shape[0]
    tq = _pick(T, [512, 256, 128])
    nt = T // tq
    scale = HEAD_DIM ** -0.5

    def body(q_ref, k_ref, v_ref, do_ref, dq_ref, dkv_ref):
        i = pl.program_id(0)

        @pl.when(i == 0)
        def _():
            dkv_ref[...] = jnp.zeros_like(dkv_ref)

        for h in range(N_HEADS):
            sl = slice(h * HEAD_DIM, (h + 1) * HEAD_DIM)
            slv = slice(D + h * HEAD_DIM, D + (h + 1) * HEAD_DIM)
            qh = q_ref[:, sl]
            kh = k_ref[:, sl].astype(BF16)
            vh = v_ref[:, sl].astype(BF16)
            doh = do_ref[:, sl].astype(BF16)
            s = lax.dot_general(qh, kh, _DIMS["nt"], preferred_element_type=F32) * scale
            e = jnp.exp(s - jnp.max(s, axis=-1, keepdims=True))
            pr = e / jnp.sum(e, axis=-1, keepdims=True)
            dp = lax.dot_general(doh, vh, _DIMS["nt"], preferred_element_type=F32)
            ds = (pr * (dp - jnp.sum(pr * dp, axis=-1, keepdims=True)) * scale).astype(BF16)
            dq_ref[:, sl] = jnp.dot(ds, kh, preferred_element_type=F32).astype(BF16)
            dkv_ref[:, sl] += lax.dot_general(ds, qh, _DIMS["tn"], preferred_element_type=F32)
            dkv_ref[:, slv] += lax.dot_general(pr.astype(BF16), doh, _DIMS["tn"], preferred_element_type=F32)

    return pl.pallas_call(
        body, grid=(nt,),
        in_specs=[_rows(tq, D), pl.BlockSpec((nm, D), lambda i: (0, 0)), pl.BlockSpec((nm, D), lambda i: (0, 1)), _rows(tq, D)],
        out_specs=[_rows(tq, D), _const((nm, 2 * D))],
        out_shape=[jax.ShapeDtypeStruct((T, D), BF16), jax.ShapeDtypeStruct((nm, 2 * D), F32)],
        compiler_params=_cparams(("arbitrary",)), name=name)(q, kv, kv, do)


def _adam_math(w, g, m, v):
    m = ADAM_B1 * m + (1.0 - ADAM_B1) * g
    v = ADAM_B2 * v + (1.0 - ADAM_B2) * (g * g)
    m_hat = m / (1.0 - ADAM_B1 ** ADAM_STEP)
    v_hat = v / (1.0 - ADAM_B2 ** ADAM_STEP)
    delta = -ADAM_LR * (m_hat / (jnp.sqrt(v_hat) + ADAM_EPS) + ADAM_WD * w)
    return delta, m, v


def adamw(w, m, v, g_arr, g_row0, *, name):
    R, C = w.shape
    tr = _pick(R, [256, 128, 64, 32, 16, 8])
    assert g_row0 % tr == 0
    g0 = g_row0 // tr

    def body(w_ref, m_ref, v_ref, g_ref, go_ref, d_ref, mo_ref, vo_ref):
        g = g_ref[...]
        d, mn, vn = _adam_math(w_ref[...], g, m_ref[...], v_ref[...])
        go_ref[...] = g
        d_ref[...] = d
        mo_ref[...] = mn
        vo_ref[...] = vn

    sp = pl.BlockSpec((tr, C), lambda i: (i, 0))
    return pl.pallas_call(
        body, grid=(R // tr,), in_specs=[sp, sp, sp, pl.BlockSpec((tr, C), lambda i: (g0 + i, 0))],
        out_specs=[sp] * 4, out_shape=[jax.ShapeDtypeStruct((R, C), F32)] * 4,
        compiler_params=_cparams(("arbitrary",)), name=name)(w, m, v, g_arr)


def _place():
    x, y, c = lax.axis_index("x"), lax.axis_index("y"), lax.axis_index("c")
    chips = [(1 - x, y), (x, 1 - y), (1 - x, 1 - y)]
    return x, y, c, chips


ANY = pl.BlockSpec(memory_space=pl.ANY)


def allgather_weights(bufs, *, name):
    n = len(bufs)

    def body(*refs):
        o_refs = refs[n:2 * n]
        send_sems, recv_sems, fsend_sems, frecv_sems = refs[2 * n:]
        x, y, c, chips = _place()
        k_me = 2 * x + y
        sib = (x, y, 1 - c)
        halves = [b.shape[1] // 2 for b in bufs]

        def half(a, cc):
            return pl.ds(pl.multiple_of(cc * halves[a], 16), halves[a])

        sends = []
        for a in range(n):
            for r, (px, py) in enumerate(chips):
                cp = pltpu.make_async_remote_copy(
                    src_ref=o_refs[a].at[k_me, half(a, c)], dst_ref=o_refs[a].at[k_me, half(a, c)],
                    send_sem=send_sems.at[3 * a + r], recv_sem=recv_sems.at[3 * a + r],
                    device_id=(px, py, c), device_id_type=MESH)
                cp.start()
                sends.append(cp)
        passed = []
        for a in range(n):
            for r, (px, py) in enumerate(chips):
                win = o_refs[a].at[2 * px + py, half(a, c)]
                pltpu.make_async_remote_copy(
                    src_ref=win, dst_ref=win, send_sem=send_sems.at[3 * a + r], recv_sem=recv_sems.at[3 * a + r],
                    device_id=(px, py, c), device_id_type=MESH).wait_recv()
                cp = pltpu.make_async_remote_copy(
                    src_ref=win, dst_ref=win, send_sem=fsend_sems.at[3 * a + r], recv_sem=frecv_sems.at[3 * a + r],
                    device_id=sib, device_id_type=MESH)
                cp.start()
                passed.append(cp)
        for a in range(n):
            for r, (px, py) in enumerate(chips):
                win = o_refs[a].at[2 * px + py, half(a, 1 - c)]
                pltpu.make_async_remote_copy(
                    src_ref=win, dst_ref=win, send_sem=fsend_sems.at[3 * a + r], recv_sem=frecv_sems.at[3 * a + r],
                    device_id=sib, device_id_type=MESH).wait_recv()
        for cp in sends + passed:
            cp.wait_send()

    return pl.pallas_call(
        body, in_specs=[ANY] * n, out_specs=[ANY] * n,
        out_shape=[jax.ShapeDtypeStruct(b.shape, b.dtype) for b in bufs],
        scratch_shapes=[pltpu.SemaphoreType.DMA((3 * n,))] * 4,
        input_output_aliases={a: a for a in range(n)},
        name=name)(*bufs)


HBM_SPEC = pl.BlockSpec(memory_space=pltpu.HBM)
SEM_SPEC = pl.BlockSpec(memory_space=pltpu.SEMAPHORE)


def _hbm(a):
    return pltpu.with_memory_space_constraint(a, pltpu.HBM)


def gather_start(bufs, pieces, *, name):
    n = len(bufs)
    npc = len(pieces)

    def body(*refs):
        b_refs = refs[:n]
        send_sems, recv_sems = refs[n], refs[n + 1]
        x, y, c, chips = _place()
        k_me = 2 * x + y
        for q, (a, row0, rows) in enumerate(pieces):
            win = b_refs[a].at[k_me, pl.ds(row0, rows)]
            for r, (px, py) in enumerate(chips):
                pltpu.make_async_remote_copy(
                    src_ref=win, dst_ref=win, send_sem=send_sems.at[3 * q + r], recv_sem=recv_sems.at[3 * q + r],
                    device_id=(px, py, c), device_id_type=MESH).start()

    return pl.pallas_call(
        body, in_specs=[HBM_SPEC] * n, out_specs=[SEM_SPEC, SEM_SPEC] + [HBM_SPEC] * n,
        out_shape=[pltpu.SemaphoreType.DMA((3 * npc,)), pltpu.SemaphoreType.DMA((3 * npc,))]
        + [pltpu.HBM(b.shape, b.dtype) for b in bufs],
        input_output_aliases={a: 2 + a for a in range(n)},
        compiler_params=pltpu.CompilerParams(has_side_effects=pltpu.SideEffectType.DATAFLOW_SIDE_EFFECTING),
        name=name)(*[_hbm(b) for b in bufs])


def gather_wait(send_sems, recv_sems, bufs, which, after, *, name):
    n = len(bufs)

    def body(*refs):
        b_refs = refs[:n]
        send_sems, recv_sems = refs[n], refs[n + 1]
        x, y, c, chips = _place()
        k_me = 2 * x + y
        for a, row0, rows, q in which:
            for r, (px, py) in enumerate(chips):
                cp = pltpu.make_async_remote_copy(
                    src_ref=b_refs[a].at[k_me, pl.ds(row0, rows)], dst_ref=b_refs[a].at[2 * px + py, pl.ds(row0, rows)],
                    send_sem=send_sems.at[3 * q + r], recv_sem=recv_sems.at[3 * q + r],
                    device_id=(px, py, c), device_id_type=MESH)
                cp.wait_send()
                cp.wait_recv()

    return pl.pallas_call(
        body, in_specs=[HBM_SPEC] * n + [SEM_SPEC, SEM_SPEC, ANY], out_specs=[HBM_SPEC] * n,
        out_shape=[pltpu.HBM(b.shape, b.dtype) for b in bufs],
        input_output_aliases={a: a for a in range(n)},
        compiler_params=pltpu.CompilerParams(has_side_effects=pltpu.SideEffectType.DATAFLOW_SIDE_EFFECTING),
        name=name)(*bufs, send_sems, recv_sems, after)


def exchange_halves(grads, *, name):
    n = len(grads)

    def body(*refs):
        g_refs, l_refs = refs[:n], refs[n:2 * n]
        send_sems, recv_sems = refs[2 * n:]
        x, y, c, _ = _place()
        cps = []
        for a in range(n):
            h = grads[a].shape[1] // 2
            cp = pltpu.make_async_remote_copy(
                src_ref=g_refs[a].at[:, pl.ds(pl.multiple_of((1 - c) * h, 8), h)], dst_ref=l_refs[a],
                send_sem=send_sems.at[a], recv_sem=recv_sems.at[a], device_id=(x, y, 1 - c), device_id_type=MESH)
            cp.start()
            cps.append(cp)
        for cp in cps:
            cp.wait()

    return pl.pallas_call(
        body, in_specs=[ANY] * n, out_specs=[ANY] * n,
        out_shape=[jax.ShapeDtypeStruct((g.shape[0], g.shape[1] // 2, g.shape[2]), g.dtype) for g in grads],
        scratch_shapes=[pltpu.SemaphoreType.DMA((n,))] * 2,
        name=name)(*grads)


N_PEERS = N_DEV - 1


def _scatter_copies(p_refs, l_refs, send_sems, recv_sems):
    x, y, c, _ = _place()
    cps = []
    for a in range(len(p_refs)):
        h = p_refs[a].shape[1] // 2
        for fx, fy in ((0, 0), (1, 0), (0, 1), (1, 1)):
            for fc in (0, 1):
                if (fx, fy, fc) == (0, 0, 0):
                    continue
                slot = 2 * (fx + 2 * fy) + fc - 1
                px, py, pc = (1 - x if fx else x), (1 - y if fy else y), (1 - c if fc else c)
                cps.append(pltpu.make_async_remote_copy(
                    src_ref=p_refs[a].at[2 * px + py, pl.ds(pl.multiple_of(pc * h, 16), h)], dst_ref=l_refs[a].at[slot],
                    send_sem=send_sems.at[N_PEERS * a + slot], recv_sem=recv_sems.at[N_PEERS * a + slot],
                    device_id=(px, py, pc), device_id_type=MESH))
    return cps


def scatter_start(parts, *, name):
    n = len(parts)
    lands = [lax.empty((N_PEERS, p.shape[1] // 2, p.shape[2]), p.dtype) for p in parts]

    def body(*refs):
        for cp in _scatter_copies(refs[:n], refs[n:2 * n], refs[2 * n], refs[2 * n + 1]):
            cp.start()

    outs = pl.pallas_call(
        body, in_specs=[HBM_SPEC] * (2 * n), out_specs=[SEM_SPEC, SEM_SPEC] + [HBM_SPEC] * (2 * n),
        out_shape=[pltpu.SemaphoreType.DMA((N_PEERS * n,)), pltpu.SemaphoreType.DMA((N_PEERS * n,))]
        + [pltpu.HBM(a.shape, a.dtype) for a in parts + lands],
        input_output_aliases={a: 2 + a for a in range(2 * n)},
        compiler_params=pltpu.CompilerParams(has_side_effects=pltpu.SideEffectType.DATAFLOW_SIDE_EFFECTING),
        name=name)(*[_hbm(a) for a in parts + lands])
    return outs[0], outs[1], list(outs[2:2 + n]), list(outs[2 + n:])


def scatter_wait(rounds, after, *, name):
    sizes = [len(r[2]) for r in rounds]
    flat = [a for r in rounds for a in r[2] + r[3]]
    sems = [s for r in rounds for s in (r[0], r[1])]
    nflat = len(flat)

    def body(*refs):
        pos = 0
        for ri, n in enumerate(sizes):
            for cp in _scatter_copies(refs[pos:pos + n], refs[pos + n:pos + 2 * n], refs[nflat + 2 * ri], refs[nflat + 2 * ri + 1]):
                cp.wait_send()
                cp.wait_recv()
            pos += 2 * n

    outs = pl.pallas_call(
        body, in_specs=[HBM_SPEC] * nflat + [SEM_SPEC] * len(sems) + [ANY], out_specs=[HBM_SPEC] * nflat,
        out_shape=[pltpu.HBM(a.shape, a.dtype) for a in flat],
        input_output_aliases={a: a for a in range(nflat)},
        compiler_params=pltpu.CompilerParams(has_side_effects=pltpu.SideEffectType.DATAFLOW_SIDE_EFFECTING),
        name=name)(*flat, *sems, after)
    res, pos = [], 0
    for n in sizes:
        res.append((list(outs[pos:pos + n]), list(outs[pos + n:pos + 2 * n])))
        pos += 2 * n
    return res


def join_halves(fulls, *, name):
    n = len(fulls)

    def body(*refs):
        o_refs = refs[n:2 * n]
        send_sems, recv_sems = refs[2 * n:]
        x, y, c, _ = _place()
        cps = []
        for a in range(n):
            h = fulls[a].shape[0] // 2
            win = o_refs[a].at[pl.ds(pl.multiple_of(c * h, 8), h)]
            cp = pltpu.make_async_remote_copy(
                src_ref=win, dst_ref=win, send_sem=send_sems.at[a], recv_sem=recv_sems.at[a],
                device_id=(x, y, 1 - c), device_id_type=MESH)
            cp.start()
            cps.append(cp)
        for a in range(n):
            h = fulls[a].shape[0] // 2
            other = o_refs[a].at[pl.ds(pl.multiple_of((1 - c) * h, 8), h)]
            pltpu.make_async_remote_copy(
                src_ref=other, dst_ref=other, send_sem=send_sems.at[a], recv_sem=recv_sems.at[a],
                device_id=(x, y, 1 - c), device_id_type=MESH).wait_recv()
        for cp in cps:
            cp.wait_send()

    return pl.pallas_call(
        body, in_specs=[ANY] * n, out_specs=[ANY] * n,
        out_shape=[jax.ShapeDtypeStruct(f.shape, f.dtype) for f in fulls],
        scratch_shapes=[pltpu.SemaphoreType.DMA((n,))] * 2,
        input_output_aliases={a: a for a in range(n)},
        name=name)(*fulls)


def add_partials(part, land, kc, *, name):
    _, R, C = part.shape
    H = R // 2
    tr = _pick(H, [256, 128, 64, 32, 16])
    per = H // tr

    def body(kc_ref, p_ref, l_ref, o_ref):
        acc = p_ref[...].astype(F32)
        for s in range(N_PEERS):
            acc = acc + l_ref[s].astype(F32)
        o_ref[...] = acc

    return pl.pallas_call(
        body,
        grid_spec=pltpu.PrefetchScalarGridSpec(
            num_scalar_prefetch=1, grid=(per,),
            in_specs=[pl.BlockSpec((None, tr, C), lambda i, kc_ref: (kc_ref[0], kc_ref[1] * per + i, 0)),
                      pl.BlockSpec((N_PEERS, tr, C), lambda i, kc_ref: (0, i, 0))],
            out_specs=pl.BlockSpec((tr, C), lambda i, kc_ref: (kc_ref[1] * per + i, 0))),
        out_shape=jax.ShapeDtypeStruct((R, C), F32),
        compiler_params=_cparams(("arbitrary",)), name=name)(kc, part, land)


def allgather_sum(v, *, name):
    m_per, n = v.shape

    def body(x_ref, out_ref, sum_ref, send_sems, recv_sems, local_sem):
        x, y, c, chips = _place()
        me, sibling = (x, y, c), (x, y, 1 - c)

        def rows(px, py, pc):
            return out_ref.at[pl.ds(pl.multiple_of((4 * px + 2 * py + pc) * m_per, 8), m_per), :]

        def copy(k, block, to, src=None):
            return pltpu.make_async_remote_copy(
                src_ref=rows(*block) if src is None else src, dst_ref=rows(*block),
                send_sem=send_sems.at[k], recv_sem=recv_sems.at[k], device_id=to, device_id_type=MESH)

        mine = pltpu.make_async_copy(x_ref, rows(*me), local_sem)
        mine.start()
        first = [copy(0, me, sibling, src=x_ref)]
        first += [copy(1 + j, me, (*chip, c), src=x_ref) for j, chip in enumerate(chips)]
        for cp in first:
            cp.start()
        passed = [copy(4 + j, (*chip, c), sibling) for j, chip in enumerate(chips)]
        for j, chip in enumerate(chips):
            copy(1 + j, (*chip, c), me).wait_recv()
            passed[j].start()
        copy(0, sibling, me).wait_recv()
        for j, chip in enumerate(chips):
            copy(4 + j, (*chip, 1 - c), me).wait_recv()
        for cp in first + passed:
            cp.wait_send()
        mine.wait()
        acc = out_ref[0:m_per, :]
        for d in range(1, N_DEV):
            acc = acc + out_ref[d * m_per:(d + 1) * m_per, :]
        sum_ref[...] = acc

    vm = pl.BlockSpec(memory_space=pltpu.VMEM)
    return pl.pallas_call(
        body, in_specs=[vm], out_specs=[vm, vm],
        out_shape=[jax.ShapeDtypeStruct((N_DEV * m_per, n), v.dtype), jax.ShapeDtypeStruct((m_per, n), v.dtype)],
        scratch_shapes=[pltpu.SemaphoreType.DMA((7,)), pltpu.SemaphoreType.DMA((7,)), pltpu.SemaphoreType.DMA],
        compiler_params=pltpu.CompilerParams(vmem_limit_bytes=VMEM_LIMIT_BYTES), name=name)(v)


def allreduce_two_level(v, *, name):
    m, n = v.shape
    h = m // 2

    def body(x_ref, out_ref, sib_ref, chip_ref, sems_send, sems_recv):
        x, y, c, chips = _place()
        k_me = 2 * x + y
        sib = (x, y, 1 - c)
        mine = pl.ds(pl.multiple_of(c * h, 8), h)
        other = pl.ds(pl.multiple_of((1 - c) * h, 8), h)

        def copy(q, src, dst, to):
            return pltpu.make_async_remote_copy(src_ref=src, dst_ref=dst, send_sem=sems_send.at[q], recv_sem=sems_recv.at[q],
                                                device_id=to, device_id_type=MESH)

        first = copy(0, x_ref.at[other], sib_ref, sib)
        first.start()
        first.wait()
        chip_ref[k_me] = x_ref[mine, :] + sib_ref[...]
        sends = [copy(1 + r, chip_ref.at[k_me], chip_ref.at[k_me], (px, py, c)) for r, (px, py) in enumerate(chips)]
        for cp in sends:
            cp.start()
        for r, (px, py) in enumerate(chips):
            copy(1 + r, chip_ref.at[2 * px + py], chip_ref.at[2 * px + py], (px, py, c)).wait_recv()
        for cp in sends:
            cp.wait_send()
        total = ((chip_ref[0] + chip_ref[1]) + chip_ref[2]) + chip_ref[3]
        out_ref[mine, :] = total
        last = copy(4, out_ref.at[mine], out_ref.at[mine], sib)
        last.start()
        copy(4, out_ref.at[other], out_ref.at[other], sib).wait_recv()
        last.wait_send()

    vm = pl.BlockSpec(memory_space=pltpu.VMEM)
    return pl.pallas_call(
        body, in_specs=[vm], out_specs=vm, out_shape=jax.ShapeDtypeStruct((m, n), v.dtype),
        scratch_shapes=[pltpu.VMEM((h, n), v.dtype), pltpu.VMEM((N_CHIPS, h, n), v.dtype),
                        pltpu.SemaphoreType.DMA((5,)), pltpu.SemaphoreType.DMA((5,))],
        compiler_params=pltpu.CompilerParams(vmem_limit_bytes=VMEM_LIMIT_BYTES), name=name)(v)


def _pack(arrs):
    cols = []
    for a in arrs:
        f = a.reshape(-1)
        pad = (-f.shape[0]) % 128
        cols.append(jnp.pad(f, (0, pad)).reshape(-1, 128))
    out = jnp.concatenate(cols, axis=0)
    pad = (-out.shape[0]) % 8
    return jnp.pad(out, ((0, pad), (0, 0)))


def _unpack(buf, shapes):
    outs, r = [], 0
    for s in shapes:
        nel = math.prod(s)
        nr = -(-nel // 128)
        outs.append(buf[r:r + nr].reshape(-1)[:nel].reshape(s))
        r += nr
    return outs


GA_CONV_OUT, GA_MIX_OUT, GA_WQ, GA_WO, GA_DOWN, GA_UP, GA_ROWS = 0, 256, 512, 768, 1024, 2048, 3072
G1_DOWN, G1_UP, G1_ROWS = 0, 1024, 2048
G2_CONV_OUT, G2_MIX_OUT, G2_WQ, G2_WO, G2_ROWS = 0, 256, 512, 768, 1024


def kernel(x, mem, in_norm_g, in_norm_b, w_in, conv_dw, conv_db, conv_norm_g, conv_norm_b, w_conv_out, ssm_log_step, ssm_lambda_re, ssm_lambda_im, ssm_b_re, ssm_b_im, ssm_c_re, ssm_c_im, ssm_d, w_ssm_glu, w_mix_out, ln1_g, ln1_b, xa_wq, xa_wkv, xa_wo, ln2_g, ln2_b, mlp_w_up, mlp_w_down, ln3_g, ln3_b, loss_target, m_in_norm_g, m_in_norm_b, m_w_in, m_conv_dw, m_conv_db, m_conv_norm_g, m_conv_norm_b, m_w_conv_out, m_ssm_log_step, m_ssm_lambda_re, m_ssm_lambda_im, m_ssm_b_re, m_ssm_b_im, m_ssm_c_re, m_ssm_c_im, m_ssm_d, m_w_ssm_glu, m_w_mix_out, m_ln1_g, m_ln1_b, m_xa_wq, m_xa_wkv, m_xa_wo, m_ln2_g, m_ln2_b, m_mlp_w_up, m_mlp_w_down, m_ln3_g, m_ln3_b, v_in_norm_g, v_in_norm_b, v_w_in, v_conv_dw, v_conv_db, v_conv_norm_g, v_conv_norm_b, v_w_conv_out, v_ssm_log_step, v_ssm_lambda_re, v_ssm_lambda_im, v_ssm_b_re, v_ssm_b_im, v_ssm_c_re, v_ssm_c_im, v_ssm_d, v_w_ssm_glu, v_w_mix_out, v_ln1_g, v_ln1_b, v_xa_wq, v_xa_wkv, v_xa_wo, v_ln2_g, v_ln2_b, v_mlp_w_up, v_mlp_w_down, v_ln3_g, v_ln3_b):
    D = D_MODEL
    xs = x[0]
    T = xs.shape[0]
    mems = mem[0]
    NM = mems.shape[0]
    tgt = loss_target[0]
    my_c = lax.axis_index("c")
    k_me = 2 * lax.axis_index("x") + lax.axis_index("y")
    c_arr = jnp.reshape(my_c, (1,)).astype(jnp.int32)
    k_arr = jnp.reshape(k_me, (1,)).astype(jnp.int32)

    sh_a = jnp.concatenate([w_conv_out[0], w_mix_out[0], xa_wq[0], xa_wo[0], mlp_w_down[0], mlp_w_up[0]], axis=0).astype(BF16)
    def own_block(shard):
        buf = jnp.zeros((N_CHIPS,) + shard.shape, shard.dtype)
        return lax.dynamic_update_slice(buf, shard[None], (k_me, 0, 0))

    dw_pad = jnp.pad(conv_dw[0], ((0, CONV_HALO - CONV_K), (0, 0)))
    (GIN,) = allgather_weights([own_block(w_in[0].astype(BF16))], name="gather_w_in")
    ag_bufs = [own_block(sh_a), GIN] + [own_block(s) for s in (xa_wkv[0].astype(BF16), w_ssm_glu[0].astype(BF16), dw_pad)]
    ag_pieces = [(4, 0, CONV_HALO), (0, GA_CONV_OUT, 256), (3, 0, D_SSM), (0, GA_MIX_OUT, 256), (0, GA_WQ, 256),
                 (2, 0, D), (0, GA_WO, 256), (0, GA_UP, D), (0, GA_DOWN, D)]
    ag_send, ag_recv, GA, GIN, GKV, GGLU, GDW = gather_start(ag_bufs, ag_pieces, name="gather_start")

    def w_rowshard(row0):
        return dict(b_spec=((N_CHIPS, 256, D), lambda i, j, k: (0, row0 // 256, 0)), b_view=(D, D), tn=D, tk=D)

    _, h0, h0b = ln_fwd(xs, in_norm_g, in_norm_b, name="ln0_fwd")
    p = mm_nn(h0b, GIN, ((None, D, 1152), lambda i, j, k: (j, 0, 0)), D_IN, tn=1152, tk=D, name="mm_w_in")[0]
    GA, GGLU, GDW = gather_wait(
        ag_send, ag_recv, [GA, GGLU, GDW],
        [(2, 0, CONV_HALO, 0), (0, GA_CONV_OUT, 256, 1), (1, 0, D_SSM, 2), (0, GA_MIX_OUT, 256, 3)], p, name="gather_wait_mixer")
    dw_taps = jnp.transpose(GDW, (1, 0, 2)).reshape(CONV_HALO, D)
    c_pre, actb = conv_fwd(p, dw_taps, conv_db, conv_norm_g[0].reshape(1, D), conv_norm_b[0].reshape(1, D), name="conv_fwd")
    ya = mm_nn(actb, GA, N=D, name="mm_conv_out", **w_rowshard(GA_CONV_OUT))[0]

    lstep, lre, lim = ssm_log_step[0], ssm_lambda_re[0], ssm_lambda_im[0]
    bre, bim, cre, cim = ssm_b_re[0], ssm_b_im[0], ssm_c_re[0], ssm_c_im[0]
    (ar, ai, bbr, bbi), disc_vjp = jax.vjp(_ssm_discretise, lstep, lre, lim, bre, bim)
    Br, Bi = _blockdiag_in(bbr), _blockdiag_in(bbi)
    Cr, Ci = _blockdiag_out(cre), _blockdiag_out(cim)
    pw_r, pw_i, pwrev_r, pwrev_i = _powers(ar.reshape(-1), ai.reshape(-1))
    dvec = ssm_d[0].reshape(1, D_SSM)
    xr, xi, yssm = ssm_fwd(p, Br, Bi, Cr, Ci, pw_r, pw_i, dvec, name="ssm_fwd")
    z = mm_nn(yssm, GGLU, ((None, D_SSM, 512), lambda i, j, k: (j, 0, 0)), 2 * D, tn=512, tk=D_SSM, name="mm_ssm_glu")[0]
    mergedb = merge_fwd(p, ya, z, name="merge_fwd")
    tm_ln = _pick(T, [512, 256, 128])
    row_spec = ((1, D), lambda i, j, k: (0, 0))

    def ln_epilogue(acc, res, g, b):
        r = ALPHA * res + acc
        xhat, _ = _ln_stats(r)
        h = xhat * g + b
        return r, h, h

    def mm_ln(a, row0, res, g, b, name):
        return mm_nn(a, GA, N=D, tm=tm_ln, extras=(res, g.reshape(1, D), b.reshape(1, D)),
                     extra_specs=[_mn(tm_ln, D), row_spec, row_spec], epilogue=ln_epilogue, out_dtypes=[F32, F32, BF16],
                     name=name, **w_rowshard(row0))

    r1, h1, h1b = mm_ln(mergedb, GA_MIX_OUT, h0, ln1_g[0], ln1_b[0], "mm_mix_out_ln1")
    GA, GKV = gather_wait(ag_send, ag_recv, [GA, GKV], [(0, GA_WQ, 256, 4), (1, 0, D, 5), (0, GA_WO, 256, 6)], r1,
                          name="gather_wait_attn")

    qb = mm_nn(h1b, GA, N=D, out_dtype=BF16, name="mm_wq", **w_rowshard(GA_WQ))[0]
    kv = mm_nn(mems, GKV, ((None, D, 512), lambda i, j, k: (j, 0, 0)), 2 * D, tn=512, tk=D, name="mm_wkv")[0]
    ob = attn_fwd(qb, kv, name="attn_fwd")
    r2, h2, h2b = mm_ln(ob, GA_WO, h1, ln2_g[0], ln2_b[0], "mm_wo_ln2")
    (GA,) = gather_wait(ag_send, ag_recv, [GA], [(0, GA_UP, D, 7), (0, GA_DOWN, D, 8)], r2, name="gather_wait_mlp")

    def relu2(acc):
        zr = jnp.maximum(acc, 0.0)
        return (zr * zr,)

    zzb = mm_nn(h2b, GA, ((None, D, D), lambda i, j, k: (j, GA_UP // D, 0)), D_FF, tn=D, tk=D,
                out_dtype=BF16, epilogue=relu2, name="mm_up")[0]
    ff = mm_nn(zzb, GA, ((N_CHIPS, D, D), lambda i, j, k: (0, GA_DOWN // D, 0)), D, tm=_pick(T, [512, 256, 128]), tn=D, tk=D_FF,
               b_view=(D_FF, D), name="mm_down")[0]
    dr3, dr3b, dg3, db3, sq = ln_loss_bwd(ff, h2, ln3_g[0], ln3_b[0], tgt, name="ln3_loss_bwd")

    def rs_begin(grads, rnd):
        return scatter_start(grads, name=f"rs{rnd}_scatter_start")

    g1_shape = jax.ShapeDtypeStruct((N_CHIPS, G1_ROWS, D), BF16)
    g2_shape = jax.ShapeDtypeStruct((N_CHIPS, G2_ROWS, D), BF16)
    dzpreb = mm_nt(dr3b, GA, ((None, D, D), lambda i, j, k: (j, GA_DOWN // D, 0)), D_FF, tn=D, tk=D, out_dtype=BF16,
                   extras=(zzb,), epilogue=lambda acc, zz: (acc * (2.0 * jnp.sqrt(zz.astype(F32))),), name="mm_down_t")[0]
    G1g = mm_tn(zzb, dr3b, tm=D, tn=D, out_spec=((None, D, D), lambda i, j, k: (i, G1_DOWN // D, 0)), out_shape=g1_shape,
                name="mm_down_g")
    G1g = mm_tn(h2b, dzpreb, tm=D, tn=D, out_spec=((None, D, D), lambda i, j, k: (j, G1_UP // D, 0)), out_shape=g1_shape,
                out_buf=G1g, name="mm_up_g")
    round1 = rs_begin([G1g], 1)
    dh2 = mm_nt(dzpreb, GA, ((N_CHIPS, D, D), lambda i, j, k: (0, GA_UP // D, 0)), D, tm=_pick(T, [512, 256, 128]), tn=D,
                tk=D_FF, b_chunks=N_CHIPS, extras=(dr3,), epilogue=lambda acc, d: (acc + ALPHA * d,),
                after=(round1[2][0],), name="mm_up_t")[0]
    dr2, dr2b, dg2, db2 = ln_bwd(r2, dh2, ln2_g[0], name="ln2_bwd")

    def g_rowshard(row0, out_buf):
        return dict(tm=D, tn=D, out_spec=((N_CHIPS, 256, D), lambda i, j, k: (0, row0 // 256, 0)), out_shape=g2_shape,
                    out_buf=out_buf)

    dob = mm_nt(dr2b, GA, N=D, out_dtype=BF16, name="mm_wo_t", **w_rowshard(GA_WO))[0]
    G2g = mm_tn(ob, dr2b, name="mm_wo_g", **g_rowshard(G2_WO, None))
    dqb, dkv = attn_bwd(qb, kv, dob, name="attn_bwd")
    G2g = mm_tn(h1b, dqb, name="mm_wq_g", **g_rowshard(G2_WQ, G2g))
    GKVg = mm_tn(mems, dkv, tm=D, tn=512, tk=NM, out_spec=((None, D, 512), lambda i, j, k: (j, 0, 0)),
                 out_shape=jax.ShapeDtypeStruct((N_CHIPS, D, 512), BF16), name="mm_wkv_g")
    dh1 = mm_nt(dqb, GA, N=D, extras=(dr2,), epilogue=lambda acc, d: (acc + ALPHA * d,), name="mm_wq_t",
                **w_rowshard(GA_WQ))[0]
    dr1, dr1b, dg1, db1 = ln_bwd(r1, dh1, ln1_g[0], name="ln1_bwd")

    dmerged = mm_nt(dr1b, GA, N=D, name="mm_mix_t", **w_rowshard(GA_MIX_OUT))[0]
    G2g = mm_tn(mergedb, dr1b, name="mm_mix_g", **g_rowshard(G2_MIX_OUT, G2g))
    dyab, dgab, dgbb, dz1b, dz2b = merge_bwd(dmerged, p, ya, z, name="merge_bwd")
    dzb = jnp.concatenate([dz1b, dz2b], axis=1)
    GGLUg = mm_tn(yssm, dzb, tm=D_SSM, tn=512, out_spec=((None, D_SSM, 512), lambda i, j, k: (j, 0, 0)),
                  out_shape=jax.ShapeDtypeStruct((N_CHIPS, D_SSM, 512), BF16), name="mm_glu_g")
    dyssm = mm_nt(dzb, GGLU, ((None, D_SSM, 512), lambda i, j, k: (k, 0, 0)), D_SSM, tn=D_SSM, tk=512, name="mm_glu_t")[0]
    dub, dBr, dBi, dCr, dCi, dar8, dai8, dd8 = ssm_bwd(dyssm, p, xr, xi, Br, Bi, Cr, Ci, pw_r, -pw_i, pwrev_r, -pwrev_i, dvec,
                                                       name="ssm_bwd")
    dar = jnp.sum(dar8, axis=0).reshape(SSM_GROUPS, SSM_STATE)
    dai = jnp.sum(dai8, axis=0).reshape(SSM_GROUPS, SSM_STATE)
    g_lstep, g_lre, g_lim, g_bre, g_bim = disc_vjp((dar, dai, _diag_in(dBr), _diag_in(dBi)))
    g_cre, g_cim = _diag_out(dCr), _diag_out(dCi)
    g_d = jnp.sum(dd8, axis=0).reshape(1, D_SSM)

    dact = mm_nt(dyab, GA, N=D, name="mm_conv_out_t", **w_rowshard(GA_CONV_OUT))[0]
    G2g = mm_tn(actb, dyab, name="mm_conv_out_g", **g_rowshard(G2_CONV_OUT, G2g))
    round2 = rs_begin([G2g, GKVg, GGLUg], 2)
    dc, dng, dnb, ddb = conv_bwd_norm(dact, c_pre, conv_norm_g[0].reshape(1, D), conv_norm_b[0].reshape(1, D),
                                      round2[2][0], name="conv_bwd_norm")
    dvgb, ddw = conv_bwd_taps(dc, p, dw_taps, name="conv_bwd_taps")
    dpb = jnp.concatenate([dvgb, dub, dgab, dgbb], axis=1)
    GINg = mm_tn(h0b, dpb, tm=D, tn=1152, out_spec=((None, D, 1152), lambda i, j, k: (j, 0, 0)),
                 out_shape=jax.ShapeDtypeStruct((N_CHIPS, D, 1152), BF16), name="mm_w_in_g")
    round3 = rs_begin([GINg], 3)
    dh0 = mm_nt(dpb, GIN, ((N_CHIPS, D, 1152), lambda i, j, k: (0, 0, 0)), D, tm=_pick(T, [512, 256, 128]), tn=D, tk=D_IN,
                b_chunks=N_CHIPS, extras=(dr1,), epilogue=lambda acc, d: (acc + ALPHA * d,), after=(round3[2][0],),
                name="mm_w_in_t")[0]
    gx, _, dg0, db0 = ln_bwd(xs, dh0, in_norm_g, name="ln0_bwd")

    kc_arr = jnp.concatenate([k_arr, c_arr])
    landed = scatter_wait([round1, round2, round3], gx, name="rs_scatter_wait")
    tags = ["mlp", "sq", "kv", "glu", "in"]
    pairs = [(pt, l2) for parts, lands2 in landed for pt, l2 in zip(parts, lands2)]
    halves = [add_partials(pt, l2, kc_arr, name="rs_add_partials_" + t) for (pt, l2), t in zip(pairs, tags)]
    g1, g2, gKV, gGLU, gIN = join_halves(halves, name="rs_join_halves")

    small_names = ["in_norm_g", "in_norm_b", "conv_db", "conv_norm_g", "conv_norm_b", "ssm_log_step", "ssm_lambda_re",
                   "ssm_lambda_im", "ssm_b_re", "ssm_b_im", "ssm_c_re", "ssm_c_im", "ssm_d", "ln1_g", "ln1_b",
                   "ln2_g", "ln2_b", "ln3_g", "ln3_b"]
    small_w = [in_norm_g, in_norm_b, conv_db, conv_norm_g, conv_norm_b, ssm_log_step, ssm_lambda_re, ssm_lambda_im,
               ssm_b_re, ssm_b_im, ssm_c_re, ssm_c_im, ssm_d, ln1_g, ln1_b, ln2_g, ln2_b, ln3_g, ln3_b]
    small_m = [m_in_norm_g, m_in_norm_b, m_conv_db, m_conv_norm_g, m_conv_norm_b, m_ssm_log_step, m_ssm_lambda_re,
               m_ssm_lambda_im, m_ssm_b_re, m_ssm_b_im, m_ssm_c_re, m_ssm_c_im, m_ssm_d, m_ln1_g, m_ln1_b, m_ln2_g,
               m_ln2_b, m_ln3_g, m_ln3_b]
    small_v = [v_in_norm_g, v_in_norm_b, v_conv_db, v_conv_norm_g, v_conv_norm_b, v_ssm_log_step, v_ssm_lambda_re,
               v_ssm_lambda_im, v_ssm_b_re, v_ssm_b_im, v_ssm_c_re, v_ssm_c_im, v_ssm_d, v_ln1_g, v_ln1_b, v_ln2_g,
               v_ln2_b, v_ln3_g, v_ln3_b]
    small_g = [dg0, db0, ddb, dng, dnb, g_lstep, g_lre, g_lim, g_bre, g_bim, g_cre, g_cim, g_d, dg1, db1, dg2, db2, dg3, db3]
    small_shapes = [w.shape for w in small_w]
    n_small_rows = _pack(small_w).shape[0]
    packed_g = _pack(small_g + [ddw, sq])
    packed_g = jnp.pad(packed_g, ((0, (-packed_g.shape[0]) % 16), (0, 0)))
    summed = allreduce_two_level(packed_g, name="allreduce_small")
    small_rows = sum(-(-math.prod(s) // 128) for s in small_shapes)
    dw_rows = CONV_HALO * D // 128
    loss = 0.5 * summed[small_rows + dw_rows, 0] / D
    ddw_full = summed[small_rows:small_rows + dw_rows].reshape(CONV_HALO, D)
    g_dw = lax.dynamic_slice_in_dim(ddw_full, k_me * (D // N_CHIPS), D // N_CHIPS, axis=1)
    gs_packed = jnp.pad(summed[:small_rows], ((0, n_small_rows - small_rows), (0, 0)))

    res = {}

    def upd(nm, w, m, v, g_arr, row0=0):
        shp = w.shape
        w2, m2, v2 = (a.reshape(-1, shp[-1]) for a in (w, m, v))
        outs = adamw(w2, m2, v2, g_arr, row0, name="adamw_" + nm)
        res[nm] = tuple(o.reshape(shp) for o in outs)

    upd("w_conv_out", w_conv_out, m_w_conv_out, v_w_conv_out, g2, G2_CONV_OUT)
    upd("w_mix_out", w_mix_out, m_w_mix_out, v_w_mix_out, g2, G2_MIX_OUT)
    upd("xa_wq", xa_wq, m_xa_wq, v_xa_wq, g2, G2_WQ)
    upd("xa_wo", xa_wo, m_xa_wo, v_xa_wo, g2, G2_WO)
    upd("mlp_w_down", mlp_w_down, m_mlp_w_down, v_mlp_w_down, g1, G1_DOWN)
    upd("mlp_w_up", mlp_w_up, m_mlp_w_up, v_mlp_w_up, g1, G1_UP)
    upd("w_in", w_in, m_w_in, v_w_in, gIN)
    upd("xa_wkv", xa_wkv, m_xa_wkv, v_xa_wkv, gKV)
    upd("w_ssm_glu", w_ssm_glu, m_w_ssm_glu, v_w_ssm_glu, gGLU)
    pad_dw = lambda a: jnp.pad(a[0], ((0, CONV_HALO - CONV_K), (0, 0)))
    dw_outs = adamw(pad_dw(conv_dw), pad_dw(m_conv_dw), pad_dw(v_conv_dw), g_dw, 0, name="adamw_conv_dw")
    res["conv_dw"] = tuple(o[:CONV_K][None] for o in dw_outs)
    sm_outs = adamw(_pack(small_w), _pack(small_m), _pack(small_v), gs_packed, 0, name="adamw_small")
    sm_un = [_unpack(o, small_shapes) for o in sm_outs]
    for idx, nm in enumerate(small_names):
        res[nm] = tuple(sm_un[q][idx] for q in range(4))

    order = ["in_norm_g", "in_norm_b", "w_in", "conv_dw", "conv_db", "conv_norm_g", "conv_norm_b", "w_conv_out",
             "ssm_log_step", "ssm_lambda_re", "ssm_lambda_im", "ssm_b_re", "ssm_b_im", "ssm_c_re", "ssm_c_im", "ssm_d",
             "w_ssm_glu", "w_mix_out", "ln1_g", "ln1_b", "xa_wq", "xa_wkv", "xa_wo", "ln2_g", "ln2_b", "mlp_w_up",
             "mlp_w_down", "ln3_g", "ln3_b"]
    return (loss, gx[None], *[res[n][0] for n in order], *[res[n][1] for n in order],
            *[res[n][2] for n in order], *[res[n][3] for n in order])
```

```python
import functools
import math

import jax
import jax.numpy as jnp
from jax import lax
from jax.experimental import pallas as pl
from jax.experimental.pallas import tpu as pltpu

F32 = jnp.float32
BF16 = jnp.bfloat16
MESH = pl.DeviceIdType.MESH

D_MODEL = 1024
N_HEADS = 4
HEAD_DIM = D_MODEL // N_HEADS
CONV_K = 31
CONV_HALO = 32
D_SSM = 512
SSM_GROUPS = 32
SSM_GROUP = 16
SSM_STATE = 64
SSM_BLOCKS = 4
SSM_BLOCK_IN = D_SSM // SSM_BLOCKS
SSM_BLOCK_STATE = SSM_GROUPS * SSM_STATE // SSM_BLOCKS
D_FF = 4096
D_IN = 4608
LN_EPS = 1e-5
ALPHA = (2.0 * 1) ** 0.25
N_CHIPS = 4
N_DEV = 8
ADAM_LR, ADAM_B1, ADAM_B2, ADAM_EPS, ADAM_WD, ADAM_STEP = 0.001, 0.9, 0.999, 1e-08, 0.01, 10
VMEM_LIMIT_BYTES = 56 * 1024 * 1024


def _pick(dim, cands):
    for c in cands:
        if dim % c == 0:
            return c
    return dim


def _cparams(sem=None):
    return pltpu.CompilerParams(dimension_semantics=sem, vmem_limit_bytes=VMEM_LIMIT_BYTES)


def _sigmoid(x):
    return 1.0 / (1.0 + jnp.exp(-x))


_DIMS = {"nn": (((1,), (0,)), ((), ())), "nt": (((1,), (1,)), ((), ())), "tn": (((0,), (0,)), ((), ()))}


def matmul(a, b, *, mode, M, N, K, tm, tn, tk, a_spec, b_spec, out_specs, out_shapes, name,
           extras=(), extra_specs=(), epilogue=None, alias_buf=None, b_view=None, after=(), b_chunks=None):
    nk = K // tk
    ne = len(extras)
    no = len(out_shapes)
    na = (0 if alias_buf is None else 1) + len(after)
    dims = _DIMS[mode]

    def body(*refs):
        a_ref, b_ref = refs[0], refs[1]
        e_refs = refs[2:2 + ne]
        o_refs = refs[2 + ne + na:2 + ne + na + no]

        def finish(acc):
            outs = (acc,) if epilogue is None else epilogue(acc, *[r[...] for r in e_refs])
            for o, r in zip(outs, o_refs):
                r[...] = o.astype(r.dtype).reshape(r.shape)

        if b_chunks:
            kc = a_ref.shape[1] // b_chunks
            prod = None
            for q in range(b_chunks):
                part = lax.dot_general(a_ref[:, q * kc:(q + 1) * kc].astype(BF16), b_ref[q].astype(BF16), dims,
                                       preferred_element_type=F32)
                prod = part if prod is None else prod + part
        else:
            b_blk = b_ref[...] if b_view is None else b_ref[...].reshape(b_view)
            prod = lax.dot_general(a_ref[...].astype(BF16), b_blk.astype(BF16), dims, preferred_element_type=F32)
        if nk == 1:
            finish(prod)
        else:
            acc_ref = refs[-1]
            k = pl.program_id(2)

            @pl.when(k == 0)
            def _():
                acc_ref[...] = prod

            @pl.when(k > 0)
            def _():
                acc_ref[...] += prod

            @pl.when(k == nk - 1)
            def _():
                finish(acc_ref[...])

    in_specs = [pl.BlockSpec(*a_spec), pl.BlockSpec(*b_spec)] + [pl.BlockSpec(*s) for s in extra_specs]
    ins = [a, b, *extras]
    if alias_buf is not None:
        in_specs.append(pl.BlockSpec(memory_space=pl.ANY))
        ins.append(alias_buf)
    for dep in after:
        in_specs.append(pl.BlockSpec(memory_space=pl.ANY))
        ins.append(dep)
    res = pl.pallas_call(
        body,
        grid=(M // tm, N // tn, nk),
        in_specs=in_specs,
        out_specs=[pl.BlockSpec(*s) for s in out_specs],
        out_shape=out_shapes,
        scratch_shapes=[] if nk == 1 else [pltpu.VMEM((tm, tn), F32)],
        input_output_aliases={2 + ne: 0} if alias_buf is not None else {},
        compiler_params=_cparams(("parallel", "parallel", "arbitrary")),
        name=name,
    )(*ins)
    return res


def _mn(tm, tn):
    return ((tm, tn), lambda i, j, k: (i, j))


def mm_nn(a, b_arr, b_spec, N, *, name, tm=None, tn, tk, out_dtype=F32, extras=(), epilogue=None, out_dtypes=None,
          b_view=None, extra_specs=None):
    M, K = a.shape
    tm = tm or _pick(M, [1024, 512, 256, 128])
    dts = out_dtypes or [out_dtype]
    return matmul(a, b_arr, mode="nn", M=M, N=N, K=K, tm=tm, tn=tn, tk=tk,
                  a_spec=((tm, tk), lambda i, j, k: (i, k)), b_spec=b_spec, b_view=b_view,
                  out_specs=[_mn(tm, tn)] * len(dts), out_shapes=[jax.ShapeDtypeStruct((M, N), d) for d in dts],
                  extras=extras, extra_specs=extra_specs or [_mn(tm, tn)] * len(extras), epilogue=epilogue, name=name)


def mm_nt(a, b_arr, b_spec, N, *, name, tm=None, tn, tk, out_dtype=F32, extras=(), epilogue=None, out_dtypes=None,
          b_view=None, after=(), b_chunks=None):
    M, K = a.shape
    tm = tm or _pick(M, [1024, 512, 256, 128])
    dts = out_dtypes or [out_dtype]
    return matmul(a, b_arr, mode="nt", M=M, N=N, K=K, tm=tm, tn=tn, tk=tk, after=after, b_chunks=b_chunks,
                  a_spec=((tm, tk), lambda i, j, k: (i, k)), b_spec=b_spec, b_view=b_view,
                  out_specs=[_mn(tm, tn)] * len(dts), out_shapes=[jax.ShapeDtypeStruct((M, N), d) for d in dts],
                  extras=extras, extra_specs=[_mn(tm, tn)] * len(extras), epilogue=epilogue, name=name)


def mm_tn(a, b, *, name, tm, tn, tk=None, out_spec, out_shape, out_buf=None):
    K, M = a.shape
    N = b.shape[1]
    tk = tk or _pick(K, [2048, 1024, 512, 256, 128])
    return matmul(a, b, mode="tn", M=M, N=N, K=K, tm=tm, tn=tn, tk=tk,
                  a_spec=((tk, tm), lambda i, j, k: (k, i)), b_spec=((tk, tn), lambda i, j, k: (k, j)),
                  out_specs=[out_spec], out_shapes=[out_shape], alias_buf=out_buf, name=name)[0]


def _rows(tc, w, cb=0):
    return pl.BlockSpec((tc, w), lambda i: (i, cb))


def _const(shape):
    return pl.BlockSpec(shape, lambda i: tuple([0] * len(shape)))


def _ln_stats(r):
    mu = jnp.mean(r, axis=-1, keepdims=True)
    xc = r - mu
    var = jnp.mean(xc * xc, axis=-1, keepdims=True)
    rstd = lax.rsqrt(var + LN_EPS)
    return xc * rstd, rstd


def _rowsum8(v):
    tc, w = v.shape
    return jnp.sum(v.reshape(tc // 8, 8, w), axis=0)


def ln_fwd(x, g, b, *, name, res=None):
    T, D = x.shape
    tc = _pick(T, [512, 256, 128])
    has_res = res is not None

    def body(*refs):
        if has_res:
            x_ref, res_ref, g_ref, b_ref, r_ref, h_ref, hb_ref = refs
            r = ALPHA * res_ref[...] + x_ref[...]
            r_ref[...] = r
        else:
            x_ref, g_ref, b_ref, h_ref, hb_ref = refs
            r = x_ref[...]
        xhat, _ = _ln_stats(r)
        y = xhat * g_ref[...] + b_ref[...]
        h_ref[...] = y
        hb_ref[...] = y.astype(BF16)

    ins = [x] + ([res] if has_res else []) + [g.reshape(1, D), b.reshape(1, D)]
    in_specs = [_rows(tc, D)] * (2 if has_res else 1) + [_const((1, D))] * 2
    n_out = 3 if has_res else 2
    outs = pl.pallas_call(
        body, grid=(T // tc,), in_specs=in_specs, out_specs=[_rows(tc, D)] * n_out,
        out_shape=[jax.ShapeDtypeStruct((T, D), F32)] * (n_out - 1) + [jax.ShapeDtypeStruct((T, D), BF16)],
        compiler_params=_cparams(("arbitrary",)), name=name)(*ins)
    if has_res:
        return outs
    return (x,) + tuple(outs)


def ln_bwd(r, dy, g, *, name):
    T, D = r.shape
    tc = _pick(T, [512, 256, 128])
    nt = T // tc

    def body(r_ref, dy_ref, g_ref, dr_ref, drb_ref, dg_ref, db_ref, accg, accb):
        i = pl.program_id(0)

        @pl.when(i == 0)
        def _():
            accg[...] = jnp.zeros_like(accg)
            accb[...] = jnp.zeros_like(accb)

        xhat, rstd = _ln_stats(r_ref[...])
        dy = dy_ref[...]
        dxh = dy * g_ref[...]
        m1 = jnp.mean(dxh, axis=-1, keepdims=True)
        m2 = jnp.mean(dxh * xhat, axis=-1, keepdims=True)
        dr = rstd * (dxh - m1 - xhat * m2)
        dr_ref[...] = dr
        drb_ref[...] = dr.astype(BF16)
        accg[...] += _rowsum8(dy * xhat)
        accb[...] += _rowsum8(dy)

        @pl.when(i == nt - 1)
        def _():
            dg_ref[...] = jnp.sum(accg[...], axis=0, keepdims=True)
            db_ref[...] = jnp.sum(accb[...], axis=0, keepdims=True)

    return pl.pallas_call(
        body, grid=(nt,), in_specs=[_rows(tc, D), _rows(tc, D), _const((1, D))],
        out_specs=[_rows(tc, D), _rows(tc, D), _const((1, D)), _const((1, D))],
        out_shape=[jax.ShapeDtypeStruct((T, D), F32), jax.ShapeDtypeStruct((T, D), BF16),
                   jax.ShapeDtypeStruct((1, D), F32), jax.ShapeDtypeStruct((1, D), F32)],
        scratch_shapes=[pltpu.VMEM((8, D), F32), pltpu.VMEM((8, D), F32)],
        compiler_params=_cparams(("arbitrary",)), name=name)(r, dy, g.reshape(1, D))


def ln_loss_bwd(x, res, g, b, target, *, name):
    T, D = x.shape
    tc = _pick(T, [512, 256, 128])
    nt = T // tc

    def body(x_ref, res_ref, g_ref, b_ref, t_ref, dr_ref, drb_ref, dg_ref, db_ref, loss_ref, accg, accb, accl):
        i = pl.program_id(0)

        @pl.when(i == 0)
        def _():
            accg[...] = jnp.zeros_like(accg)
            accb[...] = jnp.zeros_like(accb)
            accl[...] = jnp.zeros_like(accl)

        r = ALPHA * res_ref[...] + x_ref[...]
        xhat, rstd = _ln_stats(r)
        e = xhat * g_ref[...] + b_ref[...] - t_ref[...]
        dy = e * (1.0 / D)
        dxh = dy * g_ref[...]
        m1 = jnp.mean(dxh, axis=-1, keepdims=True)
        m2 = jnp.mean(dxh * xhat, axis=-1, keepdims=True)
        dr = rstd * (dxh - m1 - xhat * m2)
        dr_ref[...] = dr
        drb_ref[...] = dr.astype(BF16)
        accg[...] += _rowsum8(dy * xhat)
        accb[...] += _rowsum8(dy)
        accl[...] += _rowsum8(e * e)

        @pl.when(i == nt - 1)
        def _():
            dg_ref[...] = jnp.sum(accg[...], axis=0, keepdims=True)
            db_ref[...] = jnp.sum(accb[...], axis=0, keepdims=True)
            s = jnp.sum(jnp.sum(accl[...], axis=0, keepdims=True), axis=1, keepdims=True)
            loss_ref[...] = jnp.broadcast_to(s, (1, 128))

    return pl.pallas_call(
        body, grid=(nt,), in_specs=[_rows(tc, D), _rows(tc, D), _const((1, D)), _const((1, D)), _rows(tc, D)],
        out_specs=[_rows(tc, D), _rows(tc, D), _const((1, D)), _const((1, D)), _const((1, 128))],
        out_shape=[jax.ShapeDtypeStruct((T, D), F32), jax.ShapeDtypeStruct((T, D), BF16),
                   jax.ShapeDtypeStruct((1, D), F32), jax.ShapeDtypeStruct((1, D), F32), jax.ShapeDtypeStruct((1, 128), F32)],
        scratch_shapes=[pltpu.VMEM((8, D), F32)] * 3,
        compiler_params=_cparams(("arbitrary",)), name=name)(x, res, g.reshape(1, D), b.reshape(1, D), target)


def _halo_prev(tc):
    per = tc // CONV_HALO
    return lambda i: jnp.maximum(i * per - 1, 0)


CONV_ROWS = 32
CONV_TAP_GROUP = 4
CONV_TILE_UNROLL = 4


def _fill_shifts(S, nrows):
    for b in range(1, 8):
        S[b, 0:nrows - 8, :] = S[0, b:b + nrows - 8, :]


def _tap_sum(S, w_ref, offs, r0, nrows):
    acc = None
    for k, o in enumerate(offs):
        a, b = divmod(o, 8)
        term = w_ref[k:k + 1, :] * S[b, pl.ds(pl.multiple_of(r0 + 8 * a, 8), nrows), :]
        acc = term if acc is None else acc + term
    return acc


def conv_fwd(p, dw, db, ng, nb, *, name):
    T = p.shape[0]
    D = D_MODEL
    tc = _pick(T, [256, 128])
    prev = _halo_prev(tc)
    off = CONV_HALO - (CONV_K - 1)
    offs = [off + k for k in range(CONV_K)]

    def body(val_ref, gate_ref, valp_ref, gatep_ref, dw_ref, db_ref, ng_ref, nb_ref, c_ref, act_ref, S):
        i = pl.program_id(0)
        u_prev = valp_ref[...] * _sigmoid(gatep_ref[...])
        S[0, 0:CONV_HALO, :] = jnp.where(i > 0, u_prev, 0.0)
        S[0, CONV_HALO:CONV_HALO + tc, :] = val_ref[...] * _sigmoid(gate_ref[...])
        _fill_shifts(S, CONV_HALO + tc)

        def rows(j, carry):
            r0 = pl.multiple_of(j * CONV_ROWS, CONV_ROWS)
            c_ref[pl.ds(r0, CONV_ROWS), :] = _tap_sum(S, dw_ref, offs, r0, CONV_ROWS) + db_ref[...]
            return carry

        lax.fori_loop(0, tc // CONV_ROWS, rows, 0)
        c = c_ref[...]
        xhat, _ = _ln_stats(c)
        cn = xhat * ng_ref[...] + nb_ref[...]
        act_ref[...] = (cn * _sigmoid(cn)).astype(BF16)

    return pl.pallas_call(
        body, grid=(T // tc,),
        in_specs=[_rows(tc, D, 0), _rows(tc, D, 1),
                  pl.BlockSpec((CONV_HALO, D), lambda i: (prev(i), 0)), pl.BlockSpec((CONV_HALO, D), lambda i: (prev(i), 1)),
                  _const((CONV_HALO, D)), _const((1, D)), _const((1, D)), _const((1, D))],
        out_specs=[_rows(tc, D), _rows(tc, D)],
        out_shape=[jax.ShapeDtypeStruct((T, D), F32), jax.ShapeDtypeStruct((T, D), BF16)],
        scratch_shapes=[pltpu.VMEM((8, CONV_HALO + tc, D), F32)],
        compiler_params=_cparams(("arbitrary",)), name=name)(p, p, p, p, dw, db, ng, nb)


def conv_bwd_norm(dact, c_pre, ng, nb, after, *, name):
    T, D = c_pre.shape
    tc = _pick(T, [512, 256, 128])
    nt = T // tc

    def body(da_ref, c_ref, ng_ref, nb_ref, after_ref, dc_ref, dng_ref, dnb_ref, ddb_ref, accg, accb, accd):
        i = pl.program_id(0)

        @pl.when(i == 0)
        def _():
            accg[...] = jnp.zeros_like(accg)
            accb[...] = jnp.zeros_like(accb)
            accd[...] = jnp.zeros_like(accd)

        xhat, rstd = _ln_stats(c_ref[...])
        cn = xhat * ng_ref[...] + nb_ref[...]
        s = _sigmoid(cn)
        dcn = da_ref[...] * (s * (1.0 + cn * (1.0 - s)))
        dxh = dcn * ng_ref[...]
        m1 = jnp.mean(dxh, axis=-1, keepdims=True)
        m2 = jnp.mean(dxh * xhat, axis=-1, keepdims=True)
        dc = rstd * (dxh - m1 - xhat * m2)
        dc_ref[...] = dc
        accg[...] += _rowsum8(dcn * xhat)
        accb[...] += _rowsum8(dcn)
        accd[...] += _rowsum8(dc)

        @pl.when(i == nt - 1)
        def _():
            dng_ref[...] = jnp.sum(accg[...], axis=0, keepdims=True)
            dnb_ref[...] = jnp.sum(accb[...], axis=0, keepdims=True)
            ddb_ref[...] = jnp.sum(accd[...], axis=0, keepdims=True)

    vec = jax.ShapeDtypeStruct((1, D), F32)
    return pl.pallas_call(
        body, grid=(nt,), in_specs=[_rows(tc, D), _rows(tc, D), _const((1, D)), _const((1, D)), ANY],
        out_specs=[_rows(tc, D), _const((1, D)), _const((1, D)), _const((1, D))],
        out_shape=[jax.ShapeDtypeStruct((T, D), F32), vec, vec, vec],
        scratch_shapes=[pltpu.VMEM((8, D), F32)] * 3,
        compiler_params=_cparams(("arbitrary",)), name=name)(dact, c_pre, ng, nb, after)


def conv_bwd_taps(dc, p, dw, du_ssm, dgates, *, name):
    T, D = dc.shape
    tc = _pick(T, [256, 128])
    nt = T // tc
    per = tc // CONV_HALO
    prev = _halo_prev(tc)
    last_halo = T // CONV_HALO - 1
    nxt = lambda i: jnp.minimum((i + 1) * per, last_halo)
    off = CONV_HALO - (CONV_K - 1)

    def body(dc_ref, dcn_ref, val_ref, gate_ref, valp_ref, gatep_ref, dw_ref, dus_ref, dg_ref, dvg_ref, ddw_ref,
             ext_u, ext_d, acc):
        i = pl.program_id(0)

        @pl.when(i == 0)
        def _():
            acc[...] = jnp.zeros_like(acc)

        dvg_ref[:, 2 * D:2 * D + D_SSM] = dus_ref[...]
        dvg_ref[:, 2 * D + D_SSM:D_IN] = dg_ref[...]

        u_prev = valp_ref[...] * _sigmoid(gatep_ref[...])
        ext_u[0, 0:CONV_HALO, :] = jnp.where(i > 0, u_prev, 0.0)
        ext_u[0, CONV_HALO:CONV_HALO + tc, :] = val_ref[...] * _sigmoid(gate_ref[...])
        ext_d[0, 0:tc, :] = dc_ref[...]
        ext_d[0, tc:tc + CONV_HALO, :] = jnp.where(i < nt - 1, dcn_ref[...], 0.0)
        _fill_shifts(ext_u, CONV_HALO + tc)
        _fill_shifts(ext_d, CONV_HALO + tc)

        def rows(j, carry):
            r0 = pl.multiple_of(j * CONV_ROWS, CONV_ROWS)
            sl = pl.ds(r0, CONV_ROWS)
            du = _tap_sum(ext_d, dw_ref, [CONV_K - 1 - k for k in range(CONV_K)], r0, CONV_ROWS)
            sg = _sigmoid(gate_ref[sl, :])
            dvg_ref[sl, 0:D] = (du * sg).astype(BF16)
            dvg_ref[sl, D:2 * D] = (du * val_ref[sl, :] * sg * (1.0 - sg)).astype(BF16)
            return carry

        lax.fori_loop(0, tc // CONV_ROWS, rows, 0)

        for k0 in range(0, CONV_K, CONV_TAP_GROUP):
            ks = list(range(k0, min(k0 + CONV_TAP_GROUP, CONV_K)))

            def taps(j, accs, ks=ks):
                out = list(accs)
                for t in range(CONV_TILE_UNROLL):
                    r0 = pl.multiple_of((j * CONV_TILE_UNROLL + t) * 8, 8)
                    dct = dc_ref[pl.ds(r0, 8), :]
                    for q, k in enumerate(ks):
                        a, b = divmod(off + k, 8)
                        out[q] = out[q] + dct * ext_u[b, pl.ds(pl.multiple_of(r0 + 8 * a, 8), 8), :]
                return tuple(out)

            accs = lax.fori_loop(0, tc // (8 * CONV_TILE_UNROLL), taps, tuple(jnp.zeros((8, D), F32) for _ in ks))
            for k, a_k in zip(ks, accs):
                acc[k] += a_k

        @pl.when(i == nt - 1)
        def _():
            ddw_ref[...] = jnp.zeros_like(ddw_ref)
            for k in range(CONV_K):
                ddw_ref[k:k + 1, :] = jnp.sum(acc[k], axis=0, keepdims=True)

    return pl.pallas_call(
        body, grid=(nt,),
        in_specs=[_rows(tc, D), pl.BlockSpec((CONV_HALO, D), lambda i: (nxt(i), 0)),
                  _rows(tc, D, 0), _rows(tc, D, 1),
                  pl.BlockSpec((CONV_HALO, D), lambda i: (prev(i), 0)), pl.BlockSpec((CONV_HALO, D), lambda i: (prev(i), 1)),
                  _const((CONV_HALO, D)), _rows(tc, D_SSM), _rows(tc, 2 * D)],
        out_specs=[_rows(tc, D_IN), _const((CONV_HALO, D))],
        out_shape=[jax.ShapeDtypeStruct((T, D_IN), BF16), jax.ShapeDtypeStruct((CONV_HALO, D), F32)],
        scratch_shapes=[pltpu.VMEM((8, CONV_HALO + tc, D), F32), pltpu.VMEM((8, CONV_HALO + tc, D), F32),
                        pltpu.VMEM((CONV_K, 8, D), F32)],
        compiler_params=_cparams(("arbitrary",)), name=name)(dc, dc, p, p, p, p, dw, du_ssm, dgates)


GATE_A0 = (2 * D_MODEL + D_SSM) // 512
GATE_B0 = GATE_A0 + 2


def merge_fwd(p, ya, z, *, name):
    T = p.shape[0]
    D = D_MODEL
    tc = _pick(T, [512, 256, 128])
    W = 512

    def body(ga_ref, gb_ref, ya_ref, z1_ref, z2_ref, o_ref):
        yb = z1_ref[...] * _sigmoid(z2_ref[...])
        o_ref[...] = (_sigmoid(ga_ref[...]) * ya_ref[...] + _sigmoid(gb_ref[...]) * yb).astype(BF16)

    return pl.pallas_call(
        body, grid=(T // tc, D // W),
        in_specs=[pl.BlockSpec((tc, W), lambda i, j: (i, GATE_A0 + j)), pl.BlockSpec((tc, W), lambda i, j: (i, GATE_B0 + j)),
                  pl.BlockSpec((tc, W), lambda i, j: (i, j)), pl.BlockSpec((tc, W), lambda i, j: (i, j)),
                  pl.BlockSpec((tc, W), lambda i, j: (i, D // W + j))],
        out_specs=pl.BlockSpec((tc, W), lambda i, j: (i, j)),
        out_shape=jax.ShapeDtypeStruct((T, D), BF16),
        compiler_params=_cparams(("arbitrary", "arbitrary")), name=name)(p, p, ya, z, z)


def merge_bwd(dm, p, ya, z, *, name):
    T = p.shape[0]
    D = D_MODEL
    tc = _pick(T, [256, 128])
    W = 512
    nb = D // W

    def body(dm_ref, ga0_ref, ga1_ref, gb0_ref, gb1_ref, ya_ref, z_ref, dya_ref, dg_ref, dz_ref):
        for j, (ga_ref, gb_ref) in enumerate(((ga0_ref, gb0_ref), (ga1_ref, gb1_ref))):
            c0 = slice(j * W, (j + 1) * W)
            c1 = slice(D + j * W, D + (j + 1) * W)
            dm = dm_ref[:, c0]
            sa = _sigmoid(ga_ref[...])
            sb = _sigmoid(gb_ref[...])
            s2 = _sigmoid(z_ref[:, c1])
            z1 = z_ref[:, c0]
            yb = z1 * s2
            dya_ref[:, c0] = (dm * sa).astype(BF16)
            dg_ref[:, c0] = (dm * ya_ref[:, c0] * sa * (1.0 - sa)).astype(BF16)
            dg_ref[:, c1] = (dm * yb * sb * (1.0 - sb)).astype(BF16)
            dyb = dm * sb
            dz_ref[:, c0] = (dyb * s2).astype(BF16)
            dz_ref[:, c1] = (dyb * z1 * s2 * (1.0 - s2)).astype(BF16)

    gate = lambda cb: pl.BlockSpec((tc, W), lambda i: (i, cb))
    return pl.pallas_call(
        body, grid=(T // tc,),
        in_specs=[_rows(tc, D), gate(GATE_A0), gate(GATE_A0 + 1), gate(GATE_B0), gate(GATE_B0 + 1), _rows(tc, D),
                  _rows(tc, 2 * D)],
        out_specs=[_rows(tc, D), _rows(tc, 2 * D), _rows(tc, 2 * D)],
        out_shape=[jax.ShapeDtypeStruct((T, D), BF16), jax.ShapeDtypeStruct((T, 2 * D), BF16),
                   jax.ShapeDtypeStruct((T, 2 * D), BF16)],
        compiler_params=_cparams(("arbitrary",)), name=name)(dm, p, p, p, p, ya, z)


def _scan_block(src_r, src_i, dst_r, dst_i, car_r, car_i, pw_r, pw_i, cw_r, cw_i, ntiles, reverse, extra=None):
    W = src_r.shape[1]
    rows = lax.broadcasted_iota(jnp.int32, (8, W), 0)
    steps = []
    for d, pr in ((1, 0), (2, 1), (4, 3)):
        valid = rows < 8 - d if reverse else rows >= d
        steps.append((d, jnp.where(valid, jnp.broadcast_to(pw_r[pr:pr + 1, :], (8, W)), 0.0),
                      jnp.where(valid, jnp.broadcast_to(pw_i[pr:pr + 1, :], (8, W)), 0.0)))
    cw_r, cw_i = cw_r[...], cw_i[...]

    def tile(jj, carry):
        j = ntiles - 1 - jj if reverse else jj
        sl = pl.ds(pl.multiple_of(j * 8, 8), 8)
        xr, xi = src_r[sl, :], src_i[sl, :]
        for d, lr, li in steps:
            sr = pltpu.roll(xr, 8 - d if reverse else d, 0)
            si = pltpu.roll(xi, 8 - d if reverse else d, 0)
            xr, xi = xr + lr * sr - li * si, xi + lr * si + li * sr
        cr, ci = car_r[...], car_i[...]
        xr, xi = xr + cw_r * cr - cw_i * ci, xi + cw_r * ci + cw_i * cr
        dst_r[sl, :] = xr
        dst_i[sl, :] = xi
        edge = 0 if reverse else 7
        car_r[...] = jnp.broadcast_to(xr[edge:edge + 1, :], (8, W))
        car_i[...] = jnp.broadcast_to(xi[edge:edge + 1, :], (8, W))
        if extra is not None:
            carry = extra(j, xr, xi, carry)
        return carry

    return tile


def ssm_fwd(p, Br, Bi, Cr, Ci, pw_r, pw_i, dvec, *, name):
    T = p.shape[0]
    tt = _pick(T, [512, 256, 128])
    nt = T // tt
    WI, WS = SSM_BLOCK_IN, SSM_BLOCK_STATE
    u0 = 2 * D_MODEL // WI

    def body(u_ref, br_ref, bi_ref, cr_ref, ci_ref, pwr_ref, pwi_ref, d_ref, xr_ref, xi_ref, y_ref, bur, bui, car_r, car_i):
        i = pl.program_id(1)

        @pl.when(i == 0)
        def _():
            car_r[...] = jnp.zeros_like(car_r)
            car_i[...] = jnp.zeros_like(car_i)

        u = u_ref[...]
        ub = u.astype(BF16)
        bur[...] = jnp.dot(ub, br_ref[...].astype(BF16), preferred_element_type=F32)
        bui[...] = jnp.dot(ub, bi_ref[...].astype(BF16), preferred_element_type=F32)
        tile = _scan_block(bur, bui, xr_ref, xi_ref, car_r, car_i, pwr_ref, pwi_ref, pwr_ref, pwi_ref, tt // 8, False)
        lax.fori_loop(0, tt // 8, tile, 0)
        y = (jnp.dot(xr_ref[...].astype(BF16), cr_ref[...].astype(BF16), preferred_element_type=F32)
             - jnp.dot(xi_ref[...].astype(BF16), ci_ref[...].astype(BF16), preferred_element_type=F32)
             + d_ref[...] * u)
        y_ref[...] = y.astype(BF16)

    return pl.pallas_call(
        body, grid=(SSM_BLOCKS, nt),
        in_specs=[pl.BlockSpec((tt, WI), lambda b, i: (i, u0 + b)),
                  pl.BlockSpec((None, WI, WS), lambda b, i: (b, 0, 0)), pl.BlockSpec((None, WI, WS), lambda b, i: (b, 0, 0)),
                  pl.BlockSpec((None, WS, WI), lambda b, i: (b, 0, 0)), pl.BlockSpec((None, WS, WI), lambda b, i: (b, 0, 0)),
                  pl.BlockSpec((8, WS), lambda b, i: (0, b)), pl.BlockSpec((8, WS), lambda b, i: (0, b)),
                  pl.BlockSpec((1, WI), lambda b, i: (0, b))],
        out_specs=[pl.BlockSpec((tt, WS), lambda b, i: (i, b)), pl.BlockSpec((tt, WS), lambda b, i: (i, b)),
                   pl.BlockSpec((tt, WI), lambda b, i: (i, b))],
        out_shape=[jax.ShapeDtypeStruct((T, SSM_BLOCKS * WS), F32)] * 2 + [jax.ShapeDtypeStruct((T, D_SSM), BF16)],
        scratch_shapes=[pltpu.VMEM((tt, WS), F32), pltpu.VMEM((tt, WS), F32), pltpu.VMEM((8, WS), F32), pltpu.VMEM((8, WS), F32)],
        compiler_params=_cparams(("arbitrary", "arbitrary")), name=name)(p, Br, Bi, Cr, Ci, pw_r, pw_i, dvec)


def ssm_bwd(dy, p, xr, xi, Br, Bi, Cr, Ci, pwc_r, pwc_i, cwc_r, cwc_i, dvec, *, name):
    T = p.shape[0]
    tt = _pick(T, [512, 256, 128])
    nt = T // tt
    WI, WS = SSM_BLOCK_IN, SSM_BLOCK_STATE
    u0 = 2 * D_MODEL // WI
    tb = lambda i: nt - 1 - i
    xprev = lambda i: jnp.maximum(tb(i) * (tt // 8) - 1, 0)
    tn_dims = _DIMS["tn"]
    nt_dims = _DIMS["nt"]

    def body(dy_ref, u_ref, xr_ref, xi_ref, xpr_ref, xpi_ref, br_ref, bi_ref, cr_ref, ci_ref, pwr_ref, pwi_ref,
             cwr_ref, cwi_ref, d_ref,
             du_ref, dbr_ref, dbi_ref, dcr_ref, dci_ref, dar_ref, dai_ref, dd_ref,
             gr, gi, ext_r, ext_i, car_r, car_i):
        i = pl.program_id(1)

        @pl.when(i == 0)
        def _():
            car_r[...] = jnp.zeros_like(car_r)
            car_i[...] = jnp.zeros_like(car_i)
            dbr_ref[...] = jnp.zeros_like(dbr_ref)
            dbi_ref[...] = jnp.zeros_like(dbi_ref)
            dcr_ref[...] = jnp.zeros_like(dcr_ref)
            dci_ref[...] = jnp.zeros_like(dci_ref)
            dar_ref[...] = jnp.zeros_like(dar_ref)
            dai_ref[...] = jnp.zeros_like(dai_ref)
            dd_ref[...] = jnp.zeros_like(dd_ref)

        dy = dy_ref[...]
        dyb = dy.astype(BF16)
        u = u_ref[...]
        ub = u.astype(BF16)
        gr[...] = lax.dot_general(dyb, cr_ref[...].astype(BF16), nt_dims, preferred_element_type=F32)
        gi[...] = -lax.dot_general(dyb, ci_ref[...].astype(BF16), nt_dims, preferred_element_type=F32)
        first = tb(i) == 0
        ext_r[0:8, :] = jnp.where(first, 0.0, xpr_ref[...])
        ext_i[0:8, :] = jnp.where(first, 0.0, xpi_ref[...])
        ext_r[8:8 + tt, :] = xr_ref[...]
        ext_i[8:8 + tt, :] = xi_ref[...]
        rows = lax.broadcasted_iota(jnp.int32, (8, WS), 0)

        def lam_grad(j, g_r, g_i, carry):
            a_r, a_i = carry
            cur = pl.ds(pl.multiple_of(j * 8 + 8, 8), 8)
            prv = pl.ds(pl.multiple_of(j * 8, 8), 8)
            xc_r, xc_i = ext_r[cur, :], ext_i[cur, :]
            xl_r, xl_i = ext_r[prv, :], ext_i[prv, :]
            xp_r = jnp.where(rows == 0, jnp.broadcast_to(xl_r[7:8, :], (8, WS)), pltpu.roll(xc_r, 1, 0))
            xp_i = jnp.where(rows == 0, jnp.broadcast_to(xl_i[7:8, :], (8, WS)), pltpu.roll(xc_i, 1, 0))
            return (a_r + g_r * xp_r + g_i * xp_i, a_i + g_i * xp_r - g_r * xp_i)

        tile = _scan_block(gr, gi, gr, gi, car_r, car_i, pwr_ref, pwi_ref, cwr_ref, cwi_ref, tt // 8, True, extra=lam_grad)
        z8 = jnp.zeros((8, WS), F32)
        a_r, a_i = lax.fori_loop(0, tt // 8, tile, (z8, z8))
        dar_ref[...] += a_r
        dai_ref[...] += a_i
        grb = gr[...].astype(BF16)
        gib = gi[...].astype(BF16)
        dbr_ref[...] += lax.dot_general(ub, grb, tn_dims, preferred_element_type=F32)
        dbi_ref[...] += lax.dot_general(ub, gib, tn_dims, preferred_element_type=F32)
        dcr_ref[...] += lax.dot_general(xr_ref[...].astype(BF16), dyb, tn_dims, preferred_element_type=F32)
        dci_ref[...] -= lax.dot_general(xi_ref[...].astype(BF16), dyb, tn_dims, preferred_element_type=F32)
        du = (lax.dot_general(grb, br_ref[...].astype(BF16), nt_dims, preferred_element_type=F32)
              + lax.dot_general(gib, bi_ref[...].astype(BF16), nt_dims, preferred_element_type=F32)
              + d_ref[...] * dy)
        du_ref[...] = du.astype(BF16)
        dd_ref[...] += _rowsum8(dy * u)

    wspec = lambda shp: pl.BlockSpec((None,) + shp, lambda b, i: (b, 0, 0))
    return pl.pallas_call(
        body, grid=(SSM_BLOCKS, nt),
        in_specs=[pl.BlockSpec((tt, WI), lambda b, i: (tb(i), b)),
                  pl.BlockSpec((tt, WI), lambda b, i: (tb(i), u0 + b)),
                  pl.BlockSpec((tt, WS), lambda b, i: (tb(i), b)), pl.BlockSpec((tt, WS), lambda b, i: (tb(i), b)),
                  pl.BlockSpec((8, WS), lambda b, i: (xprev(i), b)), pl.BlockSpec((8, WS), lambda b, i: (xprev(i), b)),
                  wspec((WI, WS)), wspec((WI, WS)), wspec((WS, WI)), wspec((WS, WI)),
                  pl.BlockSpec((8, WS), lambda b, i: (0, b)), pl.BlockSpec((8, WS), lambda b, i: (0, b)),
                  pl.BlockSpec((8, WS), lambda b, i: (0, b)), pl.BlockSpec((8, WS), lambda b, i: (0, b)),
                  pl.BlockSpec((1, WI), lambda b, i: (0, b))],
        out_specs=[pl.BlockSpec((tt, WI), lambda b, i: (tb(i), b)),
                   wspec((WI, WS)), wspec((WI, WS)), wspec((WS, WI)), wspec((WS, WI)),
                   pl.BlockSpec((8, WS), lambda b, i: (0, b)), pl.BlockSpec((8, WS), lambda b, i: (0, b)),
                   pl.BlockSpec((8, WI), lambda b, i: (0, b))],
        out_shape=[jax.ShapeDtypeStruct((T, D_SSM), BF16),
                   jax.ShapeDtypeStruct((SSM_BLOCKS, WI, WS), F32), jax.ShapeDtypeStruct((SSM_BLOCKS, WI, WS), F32),
                   jax.ShapeDtypeStruct((SSM_BLOCKS, WS, WI), F32), jax.ShapeDtypeStruct((SSM_BLOCKS, WS, WI), F32),
                   jax.ShapeDtypeStruct((8, SSM_BLOCKS * WS), F32), jax.ShapeDtypeStruct((8, SSM_BLOCKS * WS), F32),
                   jax.ShapeDtypeStruct((8, D_SSM), F32)],
        scratch_shapes=[pltpu.VMEM((tt, WS), F32), pltpu.VMEM((tt, WS), F32),
                        pltpu.VMEM((tt + 8, WS), F32), pltpu.VMEM((tt + 8, WS), F32),
                        pltpu.VMEM((8, WS), F32), pltpu.VMEM((8, WS), F32)],
        compiler_params=_cparams(("arbitrary", "arbitrary")), name=name,
    )(dy, p, xr, xi, xr, xi, Br, Bi, Cr, Ci, pwc_r, pwc_i, cwc_r, cwc_i, dvec)


def _ssm_discretise(log_step, lam_re, lam_im, b_re, b_im):
    step = jnp.exp(log_step)[:, None]
    mag = jnp.exp(lam_re * step)
    ar = mag * jnp.cos(lam_im * step)
    ai = mag * jnp.sin(lam_im * step)
    den = lam_re * lam_re + lam_im * lam_im
    nr = ar - 1.0
    cr = (nr * lam_re + ai * lam_im) / den
    ci = (ai * lam_re - nr * lam_im) / den
    bbr = cr[..., None] * b_re - ci[..., None] * b_im
    bbi = cr[..., None] * b_im + ci[..., None] * b_re
    return ar, ai, bbr, bbi


def _blockdiag_in(bb):
    t = jnp.transpose(bb, (0, 2, 1)).reshape(SSM_BLOCKS, 8, SSM_GROUP, SSM_STATE)
    eye = jnp.eye(8, dtype=bb.dtype)
    return (t[:, :, :, None, :] * eye[None, :, None, :, None]).reshape(SSM_BLOCKS, SSM_BLOCK_IN, SSM_BLOCK_STATE)


def _blockdiag_out(cc):
    t = jnp.transpose(cc, (0, 2, 1)).reshape(SSM_BLOCKS, 8, SSM_STATE, SSM_GROUP)
    eye = jnp.eye(8, dtype=cc.dtype)
    return (t[:, :, :, None, :] * eye[None, :, None, :, None]).reshape(SSM_BLOCKS, SSM_BLOCK_STATE, SSM_BLOCK_IN)


def _diag_in(d):
    t = d.reshape(SSM_BLOCKS, 8, SSM_GROUP, 8, SSM_STATE)
    t = jnp.einsum("bghgp->bghp", t).reshape(SSM_GROUPS, SSM_GROUP, SSM_STATE)
    return jnp.transpose(t, (0, 2, 1))


def _diag_out(d):
    t = d.reshape(SSM_BLOCKS, 8, SSM_STATE, 8, SSM_GROUP)
    t = jnp.einsum("bgpgh->bgph", t).reshape(SSM_GROUPS, SSM_STATE, SSM_GROUP)
    return jnp.transpose(t, (0, 2, 1))


def _powers(ar, ai):
    rs, is_ = [ar], [ai]
    for _ in range(7):
        r, i = rs[-1], is_[-1]
        rs.append(r * ar - i * ai)
        is_.append(r * ai + i * ar)
    return jnp.stack(rs), jnp.stack(is_), jnp.stack(rs[::-1]), jnp.stack(is_[::-1])


def attn_fwd(q, kv, *, name):
    T, D = q.shape
    nm = kv.shape[0]
    tq = _pick(T, [512, 256, 128])
    scale = HEAD_DIM ** -0.5

    def body(q_ref, k_ref, v_ref, o_ref):
        for h in range(N_HEADS):
            sl = slice(h * HEAD_DIM, (h + 1) * HEAD_DIM)
            s = lax.dot_general(q_ref[:, sl], k_ref[:, sl].astype(BF16), _DIMS["nt"], preferred_element_type=F32) * scale
            e = jnp.exp(s - jnp.max(s, axis=-1, keepdims=True))
            pr = e / jnp.sum(e, axis=-1, keepdims=True)
            o_ref[:, sl] = jnp.dot(pr.astype(BF16), v_ref[:, sl].astype(BF16), preferred_element_type=F32).astype(BF16)

    return pl.pallas_call(
        body, grid=(T // tq,),
        in_specs=[_rows(tq, D), pl.BlockSpec((nm, D), lambda i: (0, 0)), pl.BlockSpec((nm, D), lambda i: (0, 1))],
        out_specs=_rows(tq, D), out_shape=jax.ShapeDtypeStruct((T, D), BF16),
        compiler_params=_cparams(("arbitrary",)), name=name)(q, kv, kv)


def attn_bwd(q, kv, do, *, name):
    T, D = q.shape
    nm = kv.shape[0]
    tq = _pick(T, [512, 256, 128])
    nt = T // tq
    scale = HEAD_DIM ** -0.5

    def body(q_ref, k_ref, v_ref, do_ref, dq_ref, dkv_ref):
        i = pl.program_id(0)

        @pl.when(i == 0)
        def _():
            dkv_ref[...] = jnp.zeros_like(dkv_ref)

        for h in range(N_HEADS):
            sl = slice(h * HEAD_DIM, (h + 1) * HEAD_DIM)
            slv = slice(D + h * HEAD_DIM, D + (h + 1) * HEAD_DIM)
            qh = q_ref[:, sl]
            kh = k_ref[:, sl].astype(BF16)
            vh = v_ref[:, sl].astype(BF16)
            doh = do_ref[:, sl].astype(BF16)
            s = lax.dot_general(qh, kh, _DIMS["nt"], preferred_element_type=F32) * scale
            e = jnp.exp(s - jnp.max(s, axis=-1, keepdims=True))
            pr = e / jnp.sum(e, axis=-1, keepdims=True)
            dp = lax.dot_general(doh, vh, _DIMS["nt"], preferred_element_type=F32)
            ds = (pr * (dp - jnp.sum(pr * dp, axis=-1, keepdims=True)) * scale).astype(BF16)
            dq_ref[:, sl] = jnp.dot(ds, kh, preferred_element_type=F32).astype(BF16)
            dkv_ref[:, sl] += lax.dot_general(ds, qh, _DIMS["tn"], preferred_element_type=F32)
            dkv_ref[:, slv] += lax.dot_general(pr.astype(BF16), doh, _DIMS["tn"], preferred_element_type=F32)

    return pl.pallas_call(
        body, grid=(nt,),
        in_specs=[_rows(tq, D), pl.BlockSpec((nm, D), lambda i: (0, 0)), pl.BlockSpec((nm, D), lambda i: (0, 1)), _rows(tq, D)],
        out_specs=[_rows(tq, D), _const((nm, 2 * D))],
        out_shape=[jax.ShapeDtypeStruct((T, D), BF16), jax.ShapeDtypeStruct((nm, 2 * D), F32)],
        compiler_params=_cparams(("arbitrary",)), name=name)(q, kv, kv, do)


def _adam_math(w, g, m, v):
    m = ADAM_B1 * m + (1.0 - ADAM_B1) * g
    v = ADAM_B2 * v + (1.0 - ADAM_B2) * (g * g)
    m_hat = m / (1.0 - ADAM_B1 ** ADAM_STEP)
    v_hat = v / (1.0 - ADAM_B2 ** ADAM_STEP)
    delta = -ADAM_LR * (m_hat / (jnp.sqrt(v_hat) + ADAM_EPS) + ADAM_WD * w)
    return delta, m, v


def adamw(w, m, v, g_arr, g_row0, *, name):
    R, C = w.shape
    tr = _pick(R, [256, 128, 64, 32, 16, 8])
    assert g_row0 % tr == 0
    g0 = g_row0 // tr

    def body(w_ref, m_ref, v_ref, g_ref, go_ref, d_ref, mo_ref, vo_ref):
        g = g_ref[...]
        d, mn, vn = _adam_math(w_ref[...], g, m_ref[...], v_ref[...])
        go_ref[...] = g
        d_ref[...] = d
        mo_ref[...] = mn
        vo_ref[...] = vn

    sp = pl.BlockSpec((tr, C), lambda i: (i, 0))
    return pl.pallas_call(
        body, grid=(R // tr,), in_specs=[sp, sp, sp, pl.BlockSpec((tr, C), lambda i: (g0 + i, 0))],
        out_specs=[sp] * 4, out_shape=[jax.ShapeDtypeStruct((R, C), F32)] * 4,
        compiler_params=_cparams(("arbitrary",)), name=name)(w, m, v, g_arr)


def _place():
    x, y, c = lax.axis_index("x"), lax.axis_index("y"), lax.axis_index("c")
    chips = [(1 - x, y), (x, 1 - y), (1 - x, 1 - y)]
    return x, y, c, chips


ANY = pl.BlockSpec(memory_space=pl.ANY)


def allgather_weights(bufs, *, name):
    n = len(bufs)

    def body(*refs):
        o_refs = refs[n:2 * n]
        send_sems, recv_sems, fsend_sems, frecv_sems = refs[2 * n:]
        x, y, c, chips = _place()
        k_me = 2 * x + y
        sib = (x, y, 1 - c)
        halves = [b.shape[1] // 2 for b in bufs]

        def half(a, cc):
            return pl.ds(pl.multiple_of(cc * halves[a], 16), halves[a])

        sends = []
        for a in range(n):
            for r, (px, py) in enumerate(chips):
                cp = pltpu.make_async_remote_copy(
                    src_ref=o_refs[a].at[k_me, half(a, c)], dst_ref=o_refs[a].at[k_me, half(a, c)],
                    send_sem=send_sems.at[3 * a + r], recv_sem=recv_sems.at[3 * a + r],
                    device_id=(px, py, c), device_id_type=MESH)
                cp.start()
                sends.append(cp)
        passed = []
        for a in range(n):
            for r, (px, py) in enumerate(chips):
                win = o_refs[a].at[2 * px + py, half(a, c)]
                pltpu.make_async_remote_copy(
                    src_ref=win, dst_ref=win, send_sem=send_sems.at[3 * a + r], recv_sem=recv_sems.at[3 * a + r],
                    device_id=(px, py, c), device_id_type=MESH).wait_recv()
                cp = pltpu.make_async_remote_copy(
                    src_ref=win, dst_ref=win, send_sem=fsend_sems.at[3 * a + r], recv_sem=frecv_sems.at[3 * a + r],
                    device_id=sib, device_id_type=MESH)
                cp.start()
                passed.append(cp)
        for a in range(n):
            for r, (px, py) in enumerate(chips):
                win = o_refs[a].at[2 * px + py, half(a, 1 - c)]
                pltpu.make_async_remote_copy(
                    src_ref=win, dst_ref=win, send_sem=fsend_sems.at[3 * a + r], recv_sem=frecv_sems.at[3 * a + r],
                    device_id=sib, device_id_type=MESH).wait_recv()
        for cp in sends + passed:
            cp.wait_send()

    return pl.pallas_call(
        body, in_specs=[ANY] * n, out_specs=[ANY] * n,
        out_shape=[jax.ShapeDtypeStruct(b.shape, b.dtype) for b in bufs],
        scratch_shapes=[pltpu.SemaphoreType.DMA((3 * n,))] * 4,
        input_output_aliases={a: a for a in range(n)},
        name=name)(*bufs)


HBM_SPEC = pl.BlockSpec(memory_space=pltpu.HBM)
SEM_SPEC = pl.BlockSpec(memory_space=pltpu.SEMAPHORE)


def _hbm(a):
    return pltpu.with_memory_space_constraint(a, pltpu.HBM)


def gather_start(bufs, pieces, *, name):
    n = len(bufs)
    npc = len(pieces)

    def body(*refs):
        b_refs = refs[:n]
        send_sems, recv_sems = refs[n], refs[n + 1]
        x, y, c, chips = _place()
        k_me = 2 * x + y
        for q, (a, row0, rows) in enumerate(pieces):
            win = b_refs[a].at[k_me, pl.ds(row0, rows)]
            for r, (px, py) in enumerate(chips):
                pltpu.make_async_remote_copy(
                    src_ref=win, dst_ref=win, send_sem=send_sems.at[3 * q + r], recv_sem=recv_sems.at[3 * q + r],
                    device_id=(px, py, c), device_id_type=MESH).start()

    return pl.pallas_call(
        body, in_specs=[HBM_SPEC] * n, out_specs=[SEM_SPEC, SEM_SPEC] + [HBM_SPEC] * n,
        out_shape=[pltpu.SemaphoreType.DMA((3 * npc,)), pltpu.SemaphoreType.DMA((3 * npc,))]
        + [pltpu.HBM(b.shape, b.dtype) for b in bufs],
        input_output_aliases={a: 2 + a for a in range(n)},
        compiler_params=pltpu.CompilerParams(has_side_effects=pltpu.SideEffectType.DATAFLOW_SIDE_EFFECTING),
        name=name)(*[_hbm(b) for b in bufs])


def gather_wait(send_sems, recv_sems, bufs, which, after, *, name):
    n = len(bufs)

    def body(*refs):
        b_refs = refs[:n]
        send_sems, recv_sems = refs[n], refs[n + 1]
        x, y, c, chips = _place()
        k_me = 2 * x + y
        for a, row0, rows, q in which:
            for r, (px, py) in enumerate(chips):
                cp = pltpu.make_async_remote_copy(
                    src_ref=b_refs[a].at[k_me, pl.ds(row0, rows)], dst_ref=b_refs[a].at[2 * px + py, pl.ds(row0, rows)],
                    send_sem=send_sems.at[3 * q + r], recv_sem=recv_sems.at[3 * q + r],
                    device_id=(px, py, c), device_id_type=MESH)
                cp.wait_send()
                cp.wait_recv()

    return pl.pallas_call(
        body, in_specs=[HBM_SPEC] * n + [SEM_SPEC, SEM_SPEC, ANY], out_specs=[HBM_SPEC] * n,
        out_shape=[pltpu.HBM(b.shape, b.dtype) for b in bufs],
        input_output_aliases={a: a for a in range(n)},
        compiler_params=pltpu.CompilerParams(has_side_effects=pltpu.SideEffectType.DATAFLOW_SIDE_EFFECTING),
        name=name)(*bufs, send_sems, recv_sems, after)


def exchange_halves(grads, *, name):
    n = len(grads)

    def body(*refs):
        g_refs, l_refs = refs[:n], refs[n:2 * n]
        send_sems, recv_sems = refs[2 * n:]
        x, y, c, _ = _place()
        cps = []
        for a in range(n):
            h = grads[a].shape[1] // 2
            cp = pltpu.make_async_remote_copy(
                src_ref=g_refs[a].at[:, pl.ds(pl.multiple_of((1 - c) * h, 8), h)], dst_ref=l_refs[a],
                send_sem=send_sems.at[a], recv_sem=recv_sems.at[a], device_id=(x, y, 1 - c), device_id_type=MESH)
            cp.start()
            cps.append(cp)
        for cp in cps:
            cp.wait()

    return pl.pallas_call(
        body, in_specs=[ANY] * n, out_specs=[ANY] * n,
        out_shape=[jax.ShapeDtypeStruct((g.shape[0], g.shape[1] // 2, g.shape[2]), g.dtype) for g in grads],
        scratch_shapes=[pltpu.SemaphoreType.DMA((n,))] * 2,
        name=name)(*grads)


N_PEERS = N_DEV - 1


def _scatter_copies(p_refs, l_refs, send_sems, recv_sems):
    x, y, c, _ = _place()
    cps = []
    for a in range(len(p_refs)):
        h = p_refs[a].shape[1] // 2
        for fx, fy in ((0, 0), (1, 0), (0, 1), (1, 1)):
            for fc in (0, 1):
                if (fx, fy, fc) == (0, 0, 0):
                    continue
                slot = 2 * (fx + 2 * fy) + fc - 1
                px, py, pc = (1 - x if fx else x), (1 - y if fy else y), (1 - c if fc else c)
                cps.append(pltpu.make_async_remote_copy(
                    src_ref=p_refs[a].at[2 * px + py, pl.ds(pl.multiple_of(pc * h, 16), h)], dst_ref=l_refs[a].at[slot],
                    send_sem=send_sems.at[N_PEERS * a + slot], recv_sem=recv_sems.at[N_PEERS * a + slot],
                    device_id=(px, py, pc), device_id_type=MESH))
    return cps


def scatter_start(parts, *, name):
    n = len(parts)
    lands = [lax.empty((N_PEERS, p.shape[1] // 2, p.shape[2]), p.dtype) for p in parts]

    def body(*refs):
        for cp in _scatter_copies(refs[:n], refs[n:2 * n], refs[2 * n], refs[2 * n + 1]):
            cp.start()

    outs = pl.pallas_call(
        body, in_specs=[HBM_SPEC] * (2 * n), out_specs=[SEM_SPEC, SEM_SPEC] + [HBM_SPEC] * (2 * n),
        out_shape=[pltpu.SemaphoreType.DMA((N_PEERS * n,)), pltpu.SemaphoreType.DMA((N_PEERS * n,))]
        + [pltpu.HBM(a.shape, a.dtype) for a in parts + lands],
        input_output_aliases={a: 2 + a for a in range(2 * n)},
        compiler_params=pltpu.CompilerParams(has_side_effects=pltpu.SideEffectType.DATAFLOW_SIDE_EFFECTING),
        name=name)(*[_hbm(a) for a in parts + lands])
    return outs[0], outs[1], list(outs[2:2 + n]), list(outs[2 + n:])


def scatter_wait(rounds, after, *, name):
    sizes = [len(r[2]) for r in rounds]
    flat = [a for r in rounds for a in r[2] + r[3]]
    sems = [s for r in rounds for s in (r[0], r[1])]
    nflat = len(flat)

    def body(*refs):
        pos = 0
        for ri, n in enumerate(sizes):
            for cp in _scatter_copies(refs[pos:pos + n], refs[pos + n:pos + 2 * n], refs[nflat + 2 * ri], refs[nflat + 2 * ri + 1]):
                cp.wait_send()
                cp.wait_recv()
            pos += 2 * n

    outs = pl.pallas_call(
        body, in_specs=[HBM_SPEC] * nflat + [SEM_SPEC] * len(sems) + [ANY], out_specs=[HBM_SPEC] * nflat,
        out_shape=[pltpu.HBM(a.shape, a.dtype) for a in flat],
        input_output_aliases={a: a for a in range(nflat)},
        compiler_params=pltpu.CompilerParams(has_side_effects=pltpu.SideEffectType.DATAFLOW_SIDE_EFFECTING),
        name=name)(*flat, *sems, after)
    res, pos = [], 0
    for n in sizes:
        res.append((list(outs[pos:pos + n]), list(outs[pos + n:pos + 2 * n])))
        pos += 2 * n
    return res


def join_halves(fulls, *, name):
    n = len(fulls)

    def body(*refs):
        o_refs = refs[n:2 * n]
        send_sems, recv_sems = refs[2 * n:]
        x, y, c, _ = _place()
        cps = []
        for a in range(n):
            h = fulls[a].shape[0] // 2
            win = o_refs[a].at[pl.ds(pl.multiple_of(c * h, 8), h)]
            cp = pltpu.make_async_remote_copy(
                src_ref=win, dst_ref=win, send_sem=send_sems.at[a], recv_sem=recv_sems.at[a],
                device_id=(x, y, 1 - c), device_id_type=MESH)
            cp.start()
            cps.append(cp)
        for a in range(n):
            h = fulls[a].shape[0] // 2
            other = o_refs[a].at[pl.ds(pl.multiple_of((1 - c) * h, 8), h)]
            pltpu.make_async_remote_copy(
                src_ref=other, dst_ref=other, send_sem=send_sems.at[a], recv_sem=recv_sems.at[a],
                device_id=(x, y, 1 - c), device_id_type=MESH).wait_recv()
        for cp in cps:
            cp.wait_send()

    return pl.pallas_call(
        body, in_specs=[ANY] * n, out_specs=[ANY] * n,
        out_shape=[jax.ShapeDtypeStruct(f.shape, f.dtype) for f in fulls],
        scratch_shapes=[pltpu.SemaphoreType.DMA((n,))] * 2,
        input_output_aliases={a: a for a in range(n)},
        name=name)(*fulls)


def add_partials(part, land, kc, *, name):
    _, R, C = part.shape
    H = R // 2
    tr = _pick(H, [256, 128, 64, 32, 16])
    per = H // tr

    def body(kc_ref, p_ref, l_ref, o_ref):
        acc = p_ref[...].astype(F32)
        for s in range(N_PEERS):
            acc = acc + l_ref[s].astype(F32)
        o_ref[...] = acc

    return pl.pallas_call(
        body,
        grid_spec=pltpu.PrefetchScalarGridSpec(
            num_scalar_prefetch=1, grid=(per,),
            in_specs=[pl.BlockSpec((None, tr, C), lambda i, kc_ref: (kc_ref[0], kc_ref[1] * per + i, 0)),
                      pl.BlockSpec((N_PEERS, tr, C), lambda i, kc_ref: (0, i, 0))],
            out_specs=pl.BlockSpec((tr, C), lambda i, kc_ref: (kc_ref[1] * per + i, 0))),
        out_shape=jax.ShapeDtypeStruct((R, C), F32),
        compiler_params=_cparams(("arbitrary",)), name=name)(kc, part, land)


def allgather_sum(v, *, name):
    m_per, n = v.shape

    def body(x_ref, out_ref, sum_ref, send_sems, recv_sems, local_sem):
        x, y, c, chips = _place()
        me, sibling = (x, y, c), (x, y, 1 - c)

        def rows(px, py, pc):
            return out_ref.at[pl.ds(pl.multiple_of((4 * px + 2 * py + pc) * m_per, 8), m_per), :]

        def copy(k, block, to, src=None):
            return pltpu.make_async_remote_copy(
                src_ref=rows(*block) if src is None else src, dst_ref=rows(*block),
                send_sem=send_sems.at[k], recv_sem=recv_sems.at[k], device_id=to, device_id_type=MESH)

        mine = pltpu.make_async_copy(x_ref, rows(*me), local_sem)
        mine.start()
        first = [copy(0, me, sibling, src=x_ref)]
        first += [copy(1 + j, me, (*chip, c), src=x_ref) for j, chip in enumerate(chips)]
        for cp in first:
            cp.start()
        passed = [copy(4 + j, (*chip, c), sibling) for j, chip in enumerate(chips)]
        for j, chip in enumerate(chips):
            copy(1 + j, (*chip, c), me).wait_recv()
            passed[j].start()
        copy(0, sibling, me).wait_recv()
        for j, chip in enumerate(chips):
            copy(4 + j, (*chip, 1 - c), me).wait_recv()
        for cp in first + passed:
            cp.wait_send()
        mine.wait()
        acc = out_ref[0:m_per, :]
        for d in range(1, N_DEV):
            acc = acc + out_ref[d * m_per:(d + 1) * m_per, :]
        sum_ref[...] = acc

    vm = pl.BlockSpec(memory_space=pltpu.VMEM)
    return pl.pallas_call(
        body, in_specs=[vm], out_specs=[vm, vm],
        out_shape=[jax.ShapeDtypeStruct((N_DEV * m_per, n), v.dtype), jax.ShapeDtypeStruct((m_per, n), v.dtype)],
        scratch_shapes=[pltpu.SemaphoreType.DMA((7,)), pltpu.SemaphoreType.DMA((7,)), pltpu.SemaphoreType.DMA],
        compiler_params=pltpu.CompilerParams(vmem_limit_bytes=VMEM_LIMIT_BYTES), name=name)(v)


def allreduce_two_level(v, *, name):
    m, n = v.shape
    h = m // 2

    def body(x_ref, out_ref, sib_ref, chip_ref, sems_send, sems_recv):
        x, y, c, chips = _place()
        k_me = 2 * x + y
        sib = (x, y, 1 - c)
        mine = pl.ds(pl.multiple_of(c * h, 8), h)
        other = pl.ds(pl.multiple_of((1 - c) * h, 8), h)

        def copy(q, src, dst, to):
            return pltpu.make_async_remote_copy(src_ref=src, dst_ref=dst, send_sem=sems_send.at[q], recv_sem=sems_recv.at[q],
                                                device_id=to, device_id_type=MESH)

        first = copy(0, x_ref.at[other], sib_ref, sib)
        first.start()
        first.wait()
        chip_ref[k_me] = x_ref[mine, :] + sib_ref[...]
        sends = [copy(1 + r, chip_ref.at[k_me], chip_ref.at[k_me], (px, py, c)) for r, (px, py) in enumerate(chips)]
        for cp in sends:
            cp.start()
        for r, (px, py) in enumerate(chips):
            copy(1 + r, chip_ref.at[2 * px + py], chip_ref.at[2 * px + py], (px, py, c)).wait_recv()
        for cp in sends:
            cp.wait_send()
        total = ((chip_ref[0] + chip_ref[1]) + chip_ref[2]) + chip_ref[3]
        out_ref[mine, :] = total
        last = copy(4, out_ref.at[mine], out_ref.at[mine], sib)
        last.start()
        copy(4, out_ref.at[other], out_ref.at[other], sib).wait_recv()
        last.wait_send()

    vm = pl.BlockSpec(memory_space=pltpu.VMEM)
    return pl.pallas_call(
        body, in_specs=[vm], out_specs=vm, out_shape=jax.ShapeDtypeStruct((m, n), v.dtype),
        scratch_shapes=[pltpu.VMEM((h, n), v.dtype), pltpu.VMEM((N_CHIPS, h, n), v.dtype),
                        pltpu.SemaphoreType.DMA((5,)), pltpu.SemaphoreType.DMA((5,))],
        compiler_params=pltpu.CompilerParams(vmem_limit_bytes=VMEM_LIMIT_BYTES), name=name)(v)


def _pack(arrs):
    cols = []
    for a in arrs:
        f = a.reshape(-1)
        pad = (-f.shape[0]) % 128
        cols.append(jnp.pad(f, (0, pad)).reshape(-1, 128))
    out = jnp.concatenate(cols, axis=0)
    pad = (-out.shape[0]) % 8
    return jnp.pad(out, ((0, pad), (0, 0)))


def _unpack(buf, shapes):
    outs, r = [], 0
    for s in shapes:
        nel = math.prod(s)
        nr = -(-nel // 128)
        outs.append(buf[r:r + nr].reshape(-1)[:nel].reshape(s))
        r += nr
    return outs


GA_CONV_OUT, GA_MIX_OUT, GA_WQ, GA_WO, GA_DOWN, GA_UP, GA_ROWS = 0, 256, 512, 768, 1024, 2048, 3072
G1_DOWN, G1_UP, G1_ROWS = 0, 1024, 2048
G2_CONV_OUT, G2_MIX_OUT, G2_WQ, G2_WO, G2_ROWS = 0, 256, 512, 768, 1024


def kernel(x, mem, in_norm_g, in_norm_b, w_in, conv_dw, conv_db, conv_norm_g, conv_norm_b, w_conv_out, ssm_log_step, ssm_lambda_re, ssm_lambda_im, ssm_b_re, ssm_b_im, ssm_c_re, ssm_c_im, ssm_d, w_ssm_glu, w_mix_out, ln1_g, ln1_b, xa_wq, xa_wkv, xa_wo, ln2_g, ln2_b, mlp_w_up, mlp_w_down, ln3_g, ln3_b, loss_target, m_in_norm_g, m_in_norm_b, m_w_in, m_conv_dw, m_conv_db, m_conv_norm_g, m_conv_norm_b, m_w_conv_out, m_ssm_log_step, m_ssm_lambda_re, m_ssm_lambda_im, m_ssm_b_re, m_ssm_b_im, m_ssm_c_re, m_ssm_c_im, m_ssm_d, m_w_ssm_glu, m_w_mix_out, m_ln1_g, m_ln1_b, m_xa_wq, m_xa_wkv, m_xa_wo, m_ln2_g, m_ln2_b, m_mlp_w_up, m_mlp_w_down, m_ln3_g, m_ln3_b, v_in_norm_g, v_in_norm_b, v_w_in, v_conv_dw, v_conv_db, v_conv_norm_g, v_conv_norm_b, v_w_conv_out, v_ssm_log_step, v_ssm_lambda_re, v_ssm_lambda_im, v_ssm_b_re, v_ssm_b_im, v_ssm_c_re, v_ssm_c_im, v_ssm_d, v_w_ssm_glu, v_w_mix_out, v_ln1_g, v_ln1_b, v_xa_wq, v_xa_wkv, v_xa_wo, v_ln2_g, v_ln2_b, v_mlp_w_up, v_mlp_w_down, v_ln3_g, v_ln3_b):
    D = D_MODEL
    xs = x[0]
    T = xs.shape[0]
    mems = mem[0]
    NM = mems.shape[0]
    tgt = loss_target[0]
    my_c = lax.axis_index("c")
    k_me = 2 * lax.axis_index("x") + lax.axis_index("y")
    c_arr = jnp.reshape(my_c, (1,)).astype(jnp.int32)
    k_arr = jnp.reshape(k_me, (1,)).astype(jnp.int32)

    sh_a = jnp.concatenate([w_conv_out[0], w_mix_out[0], xa_wq[0], xa_wo[0], mlp_w_down[0], mlp_w_up[0]], axis=0).astype(BF16)
    def own_block(shard):
        buf = lax.empty((N_CHIPS,) + shard.shape, shard.dtype)
        return lax.dynamic_update_slice(buf, shard[None], (k_me, 0, 0))

    dw_pad = jnp.pad(conv_dw[0], ((0, CONV_HALO - CONV_K), (0, 0)))
    (GIN,) = allgather_weights([own_block(w_in[0].astype(BF16))], name="gather_w_in")
    ag_bufs = [own_block(sh_a), GIN] + [own_block(s) for s in (xa_wkv[0].astype(BF16), w_ssm_glu[0].astype(BF16), dw_pad)]
    ag_pieces = [(4, 0, CONV_HALO), (0, GA_CONV_OUT, 256), (3, 0, D_SSM), (0, GA_MIX_OUT, 256), (0, GA_WQ, 256),
                 (2, 0, D), (0, GA_WO, 256), (0, GA_UP, D), (0, GA_DOWN, D)]
    ag_send, ag_recv, GA, GIN, GKV, GGLU, GDW = gather_start(ag_bufs, ag_pieces, name="gather_start")

    def w_rowshard(row0):
        return dict(b_spec=((N_CHIPS, 256, D), lambda i, j, k: (0, row0 // 256, 0)), b_view=(D, D), tn=D, tk=D)

    _, h0, h0b = ln_fwd(xs, in_norm_g, in_norm_b, name="ln0_fwd")
    p = mm_nn(h0b, GIN, ((None, D, 1152), lambda i, j, k: (j, 0, 0)), D_IN, tn=1152, tk=D, name="mm_w_in")[0]
    GA, GGLU, GDW = gather_wait(
        ag_send, ag_recv, [GA, GGLU, GDW],
        [(2, 0, CONV_HALO, 0), (0, GA_CONV_OUT, 256, 1), (1, 0, D_SSM, 2), (0, GA_MIX_OUT, 256, 3)], p, name="gather_wait_mixer")
    dw_taps = jnp.transpose(GDW, (1, 0, 2)).reshape(CONV_HALO, D)
    c_pre, actb = conv_fwd(p, dw_taps, conv_db, conv_norm_g[0].reshape(1, D), conv_norm_b[0].reshape(1, D), name="conv_fwd")
    ya = mm_nn(actb, GA, N=D, name="mm_conv_out", **w_rowshard(GA_CONV_OUT))[0]

    lstep, lre, lim = ssm_log_step[0], ssm_lambda_re[0], ssm_lambda_im[0]
    bre, bim, cre, cim = ssm_b_re[0], ssm_b_im[0], ssm_c_re[0], ssm_c_im[0]
    (ar, ai, bbr, bbi), disc_vjp = jax.vjp(_ssm_discretise, lstep, lre, lim, bre, bim)
    Br, Bi = _blockdiag_in(bbr), _blockdiag_in(bbi)
    Cr, Ci = _blockdiag_out(cre), _blockdiag_out(cim)
    pw_r, pw_i, pwrev_r, pwrev_i = _powers(ar.reshape(-1), ai.reshape(-1))
    dvec = ssm_d[0].reshape(1, D_SSM)
    xr, xi, yssm = ssm_fwd(p, Br, Bi, Cr, Ci, pw_r, pw_i, dvec, name="ssm_fwd")
    z = mm_nn(yssm, GGLU, ((None, D_SSM, 512), lambda i, j, k: (j, 0, 0)), 2 * D, tn=512, tk=D_SSM, name="mm_ssm_glu")[0]
    mergedb = merge_fwd(p, ya, z, name="merge_fwd")
    tm_ln = _pick(T, [512, 256, 128])
    row_spec = ((1, D), lambda i, j, k: (0, 0))

    def ln_epilogue(acc, res, g, b):
        r = ALPHA * res + acc
        xhat, _ = _ln_stats(r)
        h = xhat * g + b
        return r, h, h

    def mm_ln(a, row0, res, g, b, name):
        return mm_nn(a, GA, N=D, tm=tm_ln, extras=(res, g.reshape(1, D), b.reshape(1, D)),
                     extra_specs=[_mn(tm_ln, D), row_spec, row_spec], epilogue=ln_epilogue, out_dtypes=[F32, F32, BF16],
                     name=name, **w_rowshard(row0))

    r1, h1, h1b = mm_ln(mergedb, GA_MIX_OUT, h0, ln1_g[0], ln1_b[0], "mm_mix_out_ln1")
    GA, GKV = gather_wait(ag_send, ag_recv, [GA, GKV], [(0, GA_WQ, 256, 4), (1, 0, D, 5), (0, GA_WO, 256, 6)], r1,
                          name="gather_wait_attn")

    qb = mm_nn(h1b, GA, N=D, out_dtype=BF16, name="mm_wq", **w_rowshard(GA_WQ))[0]
    kv = mm_nn(mems, GKV, ((None, D, 512), lambda i, j, k: (j, 0, 0)), 2 * D, tn=512, tk=D, name="mm_wkv")[0]
    ob = attn_fwd(qb, kv, name="attn_fwd")
    r2, h2, h2b = mm_ln(ob, GA_WO, h1, ln2_g[0], ln2_b[0], "mm_wo_ln2")
    (GA,) = gather_wait(ag_send, ag_recv, [GA], [(0, GA_UP, D, 7), (0, GA_DOWN, D, 8)], r2, name="gather_wait_mlp")

    def relu2(acc):
        zr = jnp.maximum(acc, 0.0)
        return (zr * zr,)

    zzb = mm_nn(h2b, GA, ((None, D, D), lambda i, j, k: (j, GA_UP // D, 0)), D_FF, tn=D, tk=D,
                out_dtype=BF16, epilogue=relu2, name="mm_up")[0]
    ff = mm_nn(zzb, GA, ((N_CHIPS, D, D), lambda i, j, k: (0, GA_DOWN // D, 0)), D, tm=_pick(T, [512, 256, 128]), tn=D, tk=D_FF,
               b_view=(D_FF, D), name="mm_down")[0]
    dr3, dr3b, dg3, db3, sq = ln_loss_bwd(ff, h2, ln3_g[0], ln3_b[0], tgt, name="ln3_loss_bwd")

    def rs_begin(grads, rnd):
        return scatter_start(grads, name=f"rs{rnd}_scatter_start")

    g1_shape = jax.ShapeDtypeStruct((N_CHIPS, G1_ROWS, D), BF16)
    g2_shape = jax.ShapeDtypeStruct((N_CHIPS, G2_ROWS, D), BF16)
    dzpreb = mm_nt(dr3b, GA, ((None, D, D), lambda i, j, k: (j, GA_DOWN // D, 0)), D_FF, tn=D, tk=D, out_dtype=BF16,
                   extras=(zzb,), epilogue=lambda acc, zz: (acc * (2.0 * jnp.sqrt(zz.astype(F32))),), name="mm_down_t")[0]
    G1g = mm_tn(zzb, dr3b, tm=D, tn=D, out_spec=((None, D, D), lambda i, j, k: (i, G1_DOWN // D, 0)), out_shape=g1_shape,
                name="mm_down_g")
    G1g = mm_tn(h2b, dzpreb, tm=D, tn=D, out_spec=((None, D, D), lambda i, j, k: (j, G1_UP // D, 0)), out_shape=g1_shape,
                out_buf=G1g, name="mm_up_g")
    round1 = rs_begin([G1g], 1)
    dh2 = mm_nt(dzpreb, GA, ((N_CHIPS, D, D), lambda i, j, k: (0, GA_UP // D, 0)), D, tm=_pick(T, [512, 256, 128]), tn=D,
                tk=D_FF, b_chunks=N_CHIPS, extras=(dr3,), epilogue=lambda acc, d: (acc + ALPHA * d,),
                after=(round1[2][0],), name="mm_up_t")[0]
    dr2, dr2b, dg2, db2 = ln_bwd(r2, dh2, ln2_g[0], name="ln2_bwd")

    def g_rowshard(row0, out_buf):
        return dict(tm=D, tn=D, out_spec=((N_CHIPS, 256, D), lambda i, j, k: (0, row0 // 256, 0)), out_shape=g2_shape,
                    out_buf=out_buf)

    dob = mm_nt(dr2b, GA, N=D, out_dtype=BF16, name="mm_wo_t", **w_rowshard(GA_WO))[0]
    G2g = mm_tn(ob, dr2b, name="mm_wo_g", **g_rowshard(G2_WO, None))
    dqb, dkv = attn_bwd(qb, kv, dob, name="attn_bwd")
    G2g = mm_tn(h1b, dqb, name="mm_wq_g", **g_rowshard(G2_WQ, G2g))
    GKVg = mm_tn(mems, dkv, tm=D, tn=512, tk=NM, out_spec=((None, D, 512), lambda i, j, k: (j, 0, 0)),
                 out_shape=jax.ShapeDtypeStruct((N_CHIPS, D, 512), BF16), name="mm_wkv_g")
    dh1 = mm_nt(dqb, GA, N=D, extras=(dr2,), epilogue=lambda acc, d: (acc + ALPHA * d,), name="mm_wq_t",
                **w_rowshard(GA_WQ))[0]
    dr1, dr1b, dg1, db1 = ln_bwd(r1, dh1, ln1_g[0], name="ln1_bwd")

    dmerged = mm_nt(dr1b, GA, N=D, name="mm_mix_t", **w_rowshard(GA_MIX_OUT))[0]
    G2g = mm_tn(mergedb, dr1b, name="mm_mix_g", **g_rowshard(G2_MIX_OUT, G2g))
    dyab, dgatesb, dzb = merge_bwd(dmerged, p, ya, z, name="merge_bwd")
    GGLUg = mm_tn(yssm, dzb, tm=D_SSM, tn=512, out_spec=((None, D_SSM, 512), lambda i, j, k: (j, 0, 0)),
                  out_shape=jax.ShapeDtypeStruct((N_CHIPS, D_SSM, 512), BF16), name="mm_glu_g")
    dyssm = mm_nt(dzb, GGLU, ((None, D_SSM, 512), lambda i, j, k: (k, 0, 0)), D_SSM, tn=D_SSM, tk=512, name="mm_glu_t")[0]
    dub, dBr, dBi, dCr, dCi, dar8, dai8, dd8 = ssm_bwd(dyssm, p, xr, xi, Br, Bi, Cr, Ci, pw_r, -pw_i, pwrev_r, -pwrev_i, dvec,
                                                       name="ssm_bwd")
    dar = jnp.sum(dar8, axis=0).reshape(SSM_GROUPS, SSM_STATE)
    dai = jnp.sum(dai8, axis=0).reshape(SSM_GROUPS, SSM_STATE)
    g_lstep, g_lre, g_lim, g_bre, g_bim = disc_vjp((dar, dai, _diag_in(dBr), _diag_in(dBi)))
    g_cre, g_cim = _diag_out(dCr), _diag_out(dCi)
    g_d = jnp.sum(dd8, axis=0).reshape(1, D_SSM)

    dact = mm_nt(dyab, GA, N=D, name="mm_conv_out_t", **w_rowshard(GA_CONV_OUT))[0]
    G2g = mm_tn(actb, dyab, name="mm_conv_out_g", **g_rowshard(G2_CONV_OUT, G2g))
    round2 = rs_begin([G2g, GKVg, GGLUg], 2)
    dc, dng, dnb, ddb = conv_bwd_norm(dact, c_pre, conv_norm_g[0].reshape(1, D), conv_norm_b[0].reshape(1, D),
                                      round2[2][0], name="conv_bwd_norm")
    dpb, ddw = conv_bwd_taps(dc, p, dw_taps, dub, dgatesb, name="conv_bwd_taps")
    GINg = mm_tn(h0b, dpb, tm=D, tn=1152, out_spec=((None, D, 1152), lambda i, j, k: (j, 0, 0)),
                 out_shape=jax.ShapeDtypeStruct((N_CHIPS, D, 1152), BF16), name="mm_w_in_g")
    round3 = rs_begin([GINg], 3)
    dh0 = mm_nt(dpb, GIN, ((N_CHIPS, D, 1152), lambda i, j, k: (0, 0, 0)), D, tm=_pick(T, [512, 256, 128]), tn=D, tk=D_IN,
                b_chunks=N_CHIPS, extras=(dr1,), epilogue=lambda acc, d: (acc + ALPHA * d,), after=(round3[2][0],),
                name="mm_w_in_t")[0]
    gx, _, dg0, db0 = ln_bwd(xs, dh0, in_norm_g, name="ln0_bwd")

    kc_arr = jnp.concatenate([k_arr, c_arr])
    landed = scatter_wait([round1, round2, round3], gx, name="rs_scatter_wait")
    tags = ["mlp", "sq", "kv", "glu", "in"]
    pairs = [(pt, l2) for parts, lands2 in landed for pt, l2 in zip(parts, lands2)]
    halves = [add_partials(pt, l2, kc_arr, name="rs_add_partials_" + t) for (pt, l2), t in zip(pairs, tags)]
    g1, g2, gKV, gGLU, gIN = join_halves(halves, name="rs_join_halves")

    small_names = ["in_norm_g", "in_norm_b", "conv_db", "conv_norm_g", "conv_norm_b", "ssm_log_step", "ssm_lambda_re",
                   "ssm_lambda_im", "ssm_b_re", "ssm_b_im", "ssm_c_re", "ssm_c_im", "ssm_d", "ln1_g", "ln1_b",
                   "ln2_g", "ln2_b", "ln3_g", "ln3_b"]
    small_w = [in_norm_g, in_norm_b, conv_db, conv_norm_g, conv_norm_b, ssm_log_step, ssm_lambda_re, ssm_lambda_im,
               ssm_b_re, ssm_b_im, ssm_c_re, ssm_c_im, ssm_d, ln1_g, ln1_b, ln2_g, ln2_b, ln3_g, ln3_b]
    small_m = [m_in_norm_g, m_in_norm_b, m_conv_db, m_conv_norm_g, m_conv_norm_b, m_ssm_log_step, m_ssm_lambda_re,
               m_ssm_lambda_im, m_ssm_b_re, m_ssm_b_im, m_ssm_c_re, m_ssm_c_im, m_ssm_d, m_ln1_g, m_ln1_b, m_ln2_g,
               m_ln2_b, m_ln3_g, m_ln3_b]
    small_v = [v_in_norm_g, v_in_norm_b, v_conv_db, v_conv_norm_g, v_conv_norm_b, v_ssm_log_step, v_ssm_lambda_re,
               v_ssm_lambda_im, v_ssm_b_re, v_ssm_b_im, v_ssm_c_re, v_ssm_c_im, v_ssm_d, v_ln1_g, v_ln1_b, v_ln2_g,
               v_ln2_b, v_ln3_g, v_ln3_b]
    small_g = [dg0, db0, ddb, dng, dnb, g_lstep, g_lre, g_lim, g_bre, g_bim, g_cre, g_cim, g_d, dg1, db1, dg2, db2, dg3, db3]
    small_shapes = [w.shape for w in small_w]
    n_small_rows = _pack(small_w).shape[0]
    packed_g = _pack(small_g + [ddw, sq])
    packed_g = jnp.pad(packed_g, ((0, (-packed_g.shape[0]) % 16), (0, 0)))
    summed = allreduce_two_level(packed_g, name="allreduce_small")
    small_rows = sum(-(-math.prod(s) // 128) for s in small_shapes)
    dw_rows = CONV_HALO * D // 128
    loss = 0.5 * summed[small_rows + dw_rows, 0] / D
    ddw_full = summed[small_rows:small_rows + dw_rows].reshape(CONV_HALO, D)
    g_dw = lax.dynamic_slice_in_dim(ddw_full, k_me * (D // N_CHIPS), D // N_CHIPS, axis=1)
    gs_packed = jnp.pad(summed[:small_rows], ((0, n_small_rows - small_rows), (0, 0)))

    res = {}

    def upd(nm, w, m, v, g_arr, row0=0):
        shp = w.shape
        w2, m2, v2 = (a.reshape(-1, shp[-1]) for a in (w, m, v))
        outs = adamw(w2, m2, v2, g_arr, row0, name="adamw_" + nm)
        res[nm] = tuple(o.reshape(shp) for o in outs)

    upd("w_conv_out", w_conv_out, m_w_conv_out, v_w_conv_out, g2, G2_CONV_OUT)
    upd("w_mix_out", w_mix_out, m_w_mix_out, v_w_mix_out, g2, G2_MIX_OUT)
    upd("xa_wq", xa_wq, m_xa_wq, v_xa_wq, g2, G2_WQ)
    upd("xa_wo", xa_wo, m_xa_wo, v_xa_wo, g2, G2_WO)
    upd("mlp_w_down", mlp_w_down, m_mlp_w_down, v_mlp_w_down, g1, G1_DOWN)
    upd("mlp_w_up", mlp_w_up, m_mlp_w_up, v_mlp_w_up, g1, G1_UP)
    upd("w_in", w_in, m_w_in, v_w_in, gIN)
    upd("xa_wkv", xa_wkv, m_xa_wkv, v_xa_wkv, gKV)
    upd("w_ssm_glu", w_ssm_glu, m_w_ssm_glu, v_w_ssm_glu, gGLU)
    pad_dw = lambda a: jnp.pad(a[0], ((0, CONV_HALO - CONV_K), (0, 0)))
    dw_outs = adamw(pad_dw(conv_dw), pad_dw(m_conv_dw), pad_dw(v_conv_dw), g_dw, 0, name="adamw_conv_dw")
    res["conv_dw"] = tuple(o[:CONV_K][None] for o in dw_outs)
    sm_outs = adamw(_pack(small_w), _pack(small_m), _pack(small_v), gs_packed, 0, name="adamw_small")
    sm_un = [_unpack(o, small_shapes) for o in sm_outs]
    for idx, nm in enumerate(small_names):
        res[nm] = tuple(sm_un[q][idx] for q in range(4))

    order = ["in_norm_g", "in_norm_b", "w_in", "conv_dw", "conv_db", "conv_norm_g", "conv_norm_b", "w_conv_out",
             "ssm_log_step", "ssm_lambda_re", "ssm_lambda_im", "ssm_b_re", "ssm_b_im", "ssm_c_re", "ssm_c_im", "ssm_d",
             "w_ssm_glu", "w_mix_out", "ln1_g", "ln1_b", "xa_wq", "xa_wkv", "xa_wo", "ln2_g", "ln2_b", "mlp_w_up",
             "mlp_w_down", "ln3_g", "ln3_b"]
    return (loss, gx[None], *[res[n][0] for n in order], *[res[n][1] for n in order],
            *[res[n][2] for n in order], *[res[n][3] for n in order])
```

```python
import functools
import math

import jax
import jax.numpy as jnp
from jax import lax
from jax.experimental import pallas as pl
from jax.experimental.pallas import tpu as pltpu

F32 = jnp.float32
BF16 = jnp.bfloat16
MESH = pl.DeviceIdType.MESH

D_MODEL = 1024
N_HEADS = 4
HEAD_DIM = D_MODEL // N_HEADS
CONV_K = 31
CONV_HALO = 32
D_SSM = 512
SSM_GROUPS = 32
SSM_GROUP = 16
SSM_STATE = 64
SSM_BLOCKS = 4
SSM_BLOCK_IN = D_SSM // SSM_BLOCKS
SSM_BLOCK_STATE = SSM_GROUPS * SSM_STATE // SSM_BLOCKS
D_FF = 4096
D_IN = 4608
LN_EPS = 1e-5
ALPHA = (2.0 * 1) ** 0.25
N_CHIPS = 4
N_DEV = 8
ADAM_LR, ADAM_B1, ADAM_B2, ADAM_EPS, ADAM_WD, ADAM_STEP = 0.001, 0.9, 0.999, 1e-08, 0.01, 10
VMEM_LIMIT_BYTES = 56 * 1024 * 1024


def _pick(dim, cands):
    for c in cands:
        if dim % c == 0:
            return c
    return dim


def _cparams(sem=None):
    return pltpu.CompilerParams(dimension_semantics=sem, vmem_limit_bytes=VMEM_LIMIT_BYTES)


def _sigmoid(x):
    return 1.0 / (1.0 + jnp.exp(-x))


_DIMS = {"nn": (((1,), (0,)), ((), ())), "nt": (((1,), (1,)), ((), ())), "tn": (((0,), (0,)), ((), ()))}


def matmul(a, b, *, mode, M, N, K, tm, tn, tk, a_spec, b_spec, out_specs, out_shapes, name,
           extras=(), extra_specs=(), epilogue=None, alias_buf=None, b_view=None, after=(), b_chunks=None):
    nk = K // tk
    ne = len(extras)
    no = len(out_shapes)
    na = (0 if alias_buf is None else 1) + len(after)
    dims = _DIMS[mode]

    def body(*refs):
        a_ref, b_ref = refs[0], refs[1]
        e_refs = refs[2:2 + ne]
        o_refs = refs[2 + ne + na:2 + ne + na + no]

        def finish(acc):
            outs = (acc,) if epilogue is None else epilogue(acc, *[r[...] for r in e_refs])
            for o, r in zip(outs, o_refs):
                r[...] = o.astype(r.dtype).reshape(r.shape)

        if b_chunks:
            kc = a_ref.shape[1] // b_chunks
            prod = None
            for q in range(b_chunks):
                part = lax.dot_general(a_ref[:, q * kc:(q + 1) * kc].astype(BF16), b_ref[q].astype(BF16), dims,
                                       preferred_element_type=F32)
                prod = part if prod is None else prod + part
        else:
            b_blk = b_ref[...] if b_view is None else b_ref[...].reshape(b_view)
            prod = lax.dot_general(a_ref[...].astype(BF16), b_blk.astype(BF16), dims, preferred_element_type=F32)
        if nk == 1:
            finish(prod)
        else:
            acc_ref = refs[-1]
            k = pl.program_id(2)

            @pl.when(k == 0)
            def _():
                acc_ref[...] = prod

            @pl.when(k > 0)
            def _():
                acc_ref[...] += prod

            @pl.when(k == nk - 1)
            def _():
                finish(acc_ref[...])

    in_specs = [pl.BlockSpec(*a_spec), pl.BlockSpec(*b_spec)] + [pl.BlockSpec(*s) for s in extra_specs]
    ins = [a, b, *extras]
    if alias_buf is not None:
        in_specs.append(pl.BlockSpec(memory_space=pl.ANY))
        ins.append(alias_buf)
    for dep in after:
        in_specs.append(pl.BlockSpec(memory_space=pl.ANY))
        ins.append(dep)
    res = pl.pallas_call(
        body,
        grid=(M // tm, N // tn, nk),
        in_specs=in_specs,
        out_specs=[pl.BlockSpec(*s) for s in out_specs],
        out_shape=out_shapes,
        scratch_shapes=[] if nk == 1 else [pltpu.VMEM((tm, tn), F32)],
        input_output_aliases={2 + ne: 0} if alias_buf is not None else {},
        compiler_params=_cparams(("parallel", "parallel", "arbitrary")),
        name=name,
    )(*ins)
    return res


def _mn(tm, tn):
    return ((tm, tn), lambda i, j, k: (i, j))


def mm_nn(a, b_arr, b_spec, N, *, name, tm=None, tn, tk, out_dtype=F32, extras=(), epilogue=None, out_dtypes=None,
          b_view=None, extra_specs=None):
    M, K = a.shape
    tm = tm or _pick(M, [1024, 512, 256, 128])
    dts = out_dtypes or [out_dtype]
    return matmul(a, b_arr, mode="nn", M=M, N=N, K=K, tm=tm, tn=tn, tk=tk,
                  a_spec=((tm, tk), lambda i, j, k: (i, k)), b_spec=b_spec, b_view=b_view,
                  out_specs=[_mn(tm, tn)] * len(dts), out_shapes=[jax.ShapeDtypeStruct((M, N), d) for d in dts],
                  extras=extras, extra_specs=extra_specs or [_mn(tm, tn)] * len(extras), epilogue=epilogue, name=name)


def mm_nt(a, b_arr, b_spec, N, *, name, tm=None, tn, tk, out_dtype=F32, extras=(), epilogue=None, out_dtypes=None,
          b_view=None, after=(), b_chunks=None):
    M, K = a.shape
    tm = tm or _pick(M, [1024, 512, 256, 128])
    dts = out_dtypes or [out_dtype]
    return matmul(a, b_arr, mode="nt", M=M, N=N, K=K, tm=tm, tn=tn, tk=tk, after=after, b_chunks=b_chunks,
                  a_spec=((tm, tk), lambda i, j, k: (i, k)), b_spec=b_spec, b_view=b_view,
                  out_specs=[_mn(tm, tn)] * len(dts), out_shapes=[jax.ShapeDtypeStruct((M, N), d) for d in dts],
                  extras=extras, extra_specs=[_mn(tm, tn)] * len(extras), epilogue=epilogue, name=name)


def mm_tn(a, b, *, name, tm, tn, tk=None, out_spec, out_shape, out_buf=None):
    K, M = a.shape
    N = b.shape[1]
    tk = tk or _pick(K, [2048, 1024, 512, 256, 128])
    return matmul(a, b, mode="tn", M=M, N=N, K=K, tm=tm, tn=tn, tk=tk,
                  a_spec=((tk, tm), lambda i, j, k: (k, i)), b_spec=((tk, tn), lambda i, j, k: (k, j)),
                  out_specs=[out_spec], out_shapes=[out_shape], alias_buf=out_buf, name=name)[0]


def _rows(tc, w, cb=0):
    return pl.BlockSpec((tc, w), lambda i: (i, cb))


def _const(shape):
    return pl.BlockSpec(shape, lambda i: tuple([0] * len(shape)))


def _ln_stats(r):
    mu = jnp.mean(r, axis=-1, keepdims=True)
    xc = r - mu
    var = jnp.mean(xc * xc, axis=-1, keepdims=True)
    rstd = lax.rsqrt(var + LN_EPS)
    return xc * rstd, rstd


def _rowsum8(v):
    tc, w = v.shape
    return jnp.sum(v.reshape(tc // 8, 8, w), axis=0)


def ln_fwd(x, g, b, *, name, res=None):
    T, D = x.shape
    tc = _pick(T, [512, 256, 128])
    has_res = res is not None

    def body(*refs):
        if has_res:
            x_ref, res_ref, g_ref, b_ref, r_ref, h_ref, hb_ref = refs
            r = ALPHA * res_ref[...] + x_ref[...]
            r_ref[...] = r
        else:
            x_ref, g_ref, b_ref, h_ref, hb_ref = refs
            r = x_ref[...]
        xhat, _ = _ln_stats(r)
        y = xhat * g_ref[...] + b_ref[...]
        h_ref[...] = y
        hb_ref[...] = y.astype(BF16)

    ins = [x] + ([res] if has_res else []) + [g.reshape(1, D), b.reshape(1, D)]
    in_specs = [_rows(tc, D)] * (2 if has_res else 1) + [_const((1, D))] * 2
    n_out = 3 if has_res else 2
    outs = pl.pallas_call(
        body, grid=(T // tc,), in_specs=in_specs, out_specs=[_rows(tc, D)] * n_out,
        out_shape=[jax.ShapeDtypeStruct((T, D), F32)] * (n_out - 1) + [jax.ShapeDtypeStruct((T, D), BF16)],
        compiler_params=_cparams(("arbitrary",)), name=name)(*ins)
    if has_res:
        return outs
    return (x,) + tuple(outs)


def ln_bwd(r, dy, g, *, name):
    T, D = r.shape
    tc = _pick(T, [512, 256, 128])
    nt = T // tc

    def body(r_ref, dy_ref, g_ref, dr_ref, drb_ref, dg_ref, db_ref, accg, accb):
        i = pl.program_id(0)

        @pl.when(i == 0)
        def _():
            accg[...] = jnp.zeros_like(accg)
            accb[...] = jnp.zeros_like(accb)

        xhat, rstd = _ln_stats(r_ref[...])
        dy = dy_ref[...]
        dxh = dy * g_ref[...]
        m1 = jnp.mean(dxh, axis=-1, keepdims=True)
        m2 = jnp.mean(dxh * xhat, axis=-1, keepdims=True)
        dr = rstd * (dxh - m1 - xhat * m2)
        dr_ref[...] = dr
        drb_ref[...] = dr.astype(BF16)
        accg[...] += _rowsum8(dy * xhat)
        accb[...] += _rowsum8(dy)

        @pl.when(i == nt - 1)
        def _():
            dg_ref[...] = jnp.sum(accg[...], axis=0, keepdims=True)
            db_ref[...] = jnp.sum(accb[...], axis=0, keepdims=True)

    return pl.pallas_call(
        body, grid=(nt,), in_specs=[_rows(tc, D), _rows(tc, D), _const((1, D))],
        out_specs=[_rows(tc, D), _rows(tc, D), _const((1, D)), _const((1, D))],
        out_shape=[jax.ShapeDtypeStruct((T, D), F32), jax.ShapeDtypeStruct((T, D), BF16),
                   jax.ShapeDtypeStruct((1, D), F32), jax.ShapeDtypeStruct((1, D), F32)],
        scratch_shapes=[pltpu.VMEM((8, D), F32), pltpu.VMEM((8, D), F32)],
        compiler_params=_cparams(("arbitrary",)), name=name)(r, dy, g.reshape(1, D))


def ln_loss_bwd(x, res, g, b, target, *, name):
    T, D = x.shape
    tc = _pick(T, [512, 256, 128])
    nt = T // tc

    def body(x_ref, res_ref, g_ref, b_ref, t_ref, dr_ref, drb_ref, dg_ref, db_ref, loss_ref, accg, accb, accl):
        i = pl.program_id(0)

        @pl.when(i == 0)
        def _():
            accg[...] = jnp.zeros_like(accg)
            accb[...] = jnp.zeros_like(accb)
            accl[...] = jnp.zeros_like(accl)

        r = ALPHA * res_ref[...] + x_ref[...]
        xhat, rstd = _ln_stats(r)
        e = xhat * g_ref[...] + b_ref[...] - t_ref[...]
        dy = e * (1.0 / D)
        dxh = dy * g_ref[...]
        m1 = jnp.mean(dxh, axis=-1, keepdims=True)
        m2 = jnp.mean(dxh * xhat, axis=-1, keepdims=True)
        dr = rstd * (dxh - m1 - xhat * m2)
        dr_ref[...] = dr
        drb_ref[...] = dr.astype(BF16)
        accg[...] += _rowsum8(dy * xhat)
        accb[...] += _rowsum8(dy)
        accl[...] += _rowsum8(e * e)

        @pl.when(i == nt - 1)
        def _():
            dg_ref[...] = jnp.sum(accg[...], axis=0, keepdims=True)
            db_ref[...] = jnp.sum(accb[...], axis=0, keepdims=True)
            s = jnp.sum(jnp.sum(accl[...], axis=0, keepdims=True), axis=1, keepdims=True)
            loss_ref[...] = jnp.broadcast_to(s, (1, 128))

    return pl.pallas_call(
        body, grid=(nt,), in_specs=[_rows(tc, D), _rows(tc, D), _const((1, D)), _const((1, D)), _rows(tc, D)],
        out_specs=[_rows(tc, D), _rows(tc, D), _const((1, D)), _const((1, D)), _const((1, 128))],
        out_shape=[jax.ShapeDtypeStruct((T, D), F32), jax.ShapeDtypeStruct((T, D), BF16),
                   jax.ShapeDtypeStruct((1, D), F32), jax.ShapeDtypeStruct((1, D), F32), jax.ShapeDtypeStruct((1, 128), F32)],
        scratch_shapes=[pltpu.VMEM((8, D), F32)] * 3,
        compiler_params=_cparams(("arbitrary",)), name=name)(x, res, g.reshape(1, D), b.reshape(1, D), target)


def _halo_prev(tc):
    per = tc // CONV_HALO
    return lambda i: jnp.maximum(i * per - 1, 0)


CONV_ROWS = 32
CONV_TAP_GROUP = 4
CONV_TILE_UNROLL = 4


def _fill_shifts(S, nrows):
    for b in range(1, 8):
        S[b, 0:nrows - 8, :] = S[0, b:b + nrows - 8, :]


def _tap_sum(S, w_ref, offs, r0, nrows):
    acc = None
    for k, o in enumerate(offs):
        a, b = divmod(o, 8)
        term = w_ref[k:k + 1, :] * S[b, pl.ds(pl.multiple_of(r0 + 8 * a, 8), nrows), :]
        acc = term if acc is None else acc + term
    return acc


def conv_fwd(p, dw, db, ng, nb, *, name):
    T = p.shape[0]
    D = D_MODEL
    tc = _pick(T, [256, 128])
    prev = _halo_prev(tc)
    off = CONV_HALO - (CONV_K - 1)
    offs = [off + k for k in range(CONV_K)]

    def body(val_ref, gate_ref, valp_ref, gatep_ref, dw_ref, db_ref, ng_ref, nb_ref, c_ref, act_ref, S):
        i = pl.program_id(0)
        u_prev = valp_ref[...] * _sigmoid(gatep_ref[...])
        S[0, 0:CONV_HALO, :] = jnp.where(i > 0, u_prev, 0.0)
        S[0, CONV_HALO:CONV_HALO + tc, :] = val_ref[...] * _sigmoid(gate_ref[...])
        _fill_shifts(S, CONV_HALO + tc)

        def rows(j, carry):
            r0 = pl.multiple_of(j * CONV_ROWS, CONV_ROWS)
            c_ref[pl.ds(r0, CONV_ROWS), :] = _tap_sum(S, dw_ref, offs, r0, CONV_ROWS) + db_ref[...]
            return carry

        lax.fori_loop(0, tc // CONV_ROWS, rows, 0)
        c = c_ref[...]
        xhat, _ = _ln_stats(c)
        cn = xhat * ng_ref[...] + nb_ref[...]
        act_ref[...] = (cn * _sigmoid(cn)).astype(BF16)

    return pl.pallas_call(
        body, grid=(T // tc,),
        in_specs=[_rows(tc, D, 0), _rows(tc, D, 1),
                  pl.BlockSpec((CONV_HALO, D), lambda i: (prev(i), 0)), pl.BlockSpec((CONV_HALO, D), lambda i: (prev(i), 1)),
                  _const((CONV_HALO, D)), _const((1, D)), _const((1, D)), _const((1, D))],
        out_specs=[_rows(tc, D), _rows(tc, D)],
        out_shape=[jax.ShapeDtypeStruct((T, D), F32), jax.ShapeDtypeStruct((T, D), BF16)],
        scratch_shapes=[pltpu.VMEM((8, CONV_HALO + tc, D), F32)],
        compiler_params=_cparams(("arbitrary",)), name=name)(p, p, p, p, dw, db, ng, nb)


def conv_bwd_norm(dact, c_pre, ng, nb, after, *, name):
    T, D = c_pre.shape
    tc = _pick(T, [512, 256, 128])
    nt = T // tc

    def body(da_ref, c_ref, ng_ref, nb_ref, after_ref, dc_ref, dng_ref, dnb_ref, ddb_ref, accg, accb, accd):
        i = pl.program_id(0)

        @pl.when(i == 0)
        def _():
            accg[...] = jnp.zeros_like(accg)
            accb[...] = jnp.zeros_like(accb)
            accd[...] = jnp.zeros_like(accd)

        xhat, rstd = _ln_stats(c_ref[...])
        cn = xhat * ng_ref[...] + nb_ref[...]
        s = _sigmoid(cn)
        dcn = da_ref[...] * (s * (1.0 + cn * (1.0 - s)))
        dxh = dcn * ng_ref[...]
        m1 = jnp.mean(dxh, axis=-1, keepdims=True)
        m2 = jnp.mean(dxh * xhat, axis=-1, keepdims=True)
        dc = rstd * (dxh - m1 - xhat * m2)
        dc_ref[...] = dc
        accg[...] += _rowsum8(dcn * xhat)
        accb[...] += _rowsum8(dcn)
        accd[...] += _rowsum8(dc)

        @pl.when(i == nt - 1)
        def _():
            dng_ref[...] = jnp.sum(accg[...], axis=0, keepdims=True)
            dnb_ref[...] = jnp.sum(accb[...], axis=0, keepdims=True)
            ddb_ref[...] = jnp.sum(accd[...], axis=0, keepdims=True)

    vec = jax.ShapeDtypeStruct((1, D), F32)
    return pl.pallas_call(
        body, grid=(nt,), in_specs=[_rows(tc, D), _rows(tc, D), _const((1, D)), _const((1, D)), ANY],
        out_specs=[_rows(tc, D), _const((1, D)), _const((1, D)), _const((1, D))],
        out_shape=[jax.ShapeDtypeStruct((T, D), F32), vec, vec, vec],
        scratch_shapes=[pltpu.VMEM((8, D), F32)] * 3,
        compiler_params=_cparams(("arbitrary",)), name=name)(dact, c_pre, ng, nb, after)


def conv_bwd_taps(dc, p, dw, du_ssm, dgates, *, name):
    T, D = dc.shape
    tc = _pick(T, [256, 128])
    nt = T // tc
    per = tc // CONV_HALO
    prev = _halo_prev(tc)
    last_halo = T // CONV_HALO - 1
    nxt = lambda i: jnp.minimum((i + 1) * per, last_halo)
    off = CONV_HALO - (CONV_K - 1)

    def body(dc_ref, dcn_ref, val_ref, gate_ref, valp_ref, gatep_ref, dw_ref, dus_ref, dg_ref, dvg_ref, ddw_ref,
             ext_u, ext_d, acc):
        i = pl.program_id(0)

        @pl.when(i == 0)
        def _():
            acc[...] = jnp.zeros_like(acc)

        dvg_ref[:, 2 * D:2 * D + D_SSM] = dus_ref[...]
        dvg_ref[:, 2 * D + D_SSM:D_IN] = dg_ref[...]

        u_prev = valp_ref[...] * _sigmoid(gatep_ref[...])
        ext_u[0, 0:CONV_HALO, :] = jnp.where(i > 0, u_prev, 0.0)
        ext_u[0, CONV_HALO:CONV_HALO + tc, :] = val_ref[...] * _sigmoid(gate_ref[...])
        ext_d[0, 0:tc, :] = dc_ref[...]
        ext_d[0, tc:tc + CONV_HALO, :] = jnp.where(i < nt - 1, dcn_ref[...], 0.0)
        _fill_shifts(ext_u, CONV_HALO + tc)
        _fill_shifts(ext_d, CONV_HALO + tc)

        def rows(j, carry):
            r0 = pl.multiple_of(j * CONV_ROWS, CONV_ROWS)
            sl = pl.ds(r0, CONV_ROWS)
            du = _tap_sum(ext_d, dw_ref, [CONV_K - 1 - k for k in range(CONV_K)], r0, CONV_ROWS)
            sg = _sigmoid(gate_ref[sl, :])
            dvg_ref[sl, 0:D] = (du * sg).astype(BF16)
            dvg_ref[sl, D:2 * D] = (du * val_ref[sl, :] * sg * (1.0 - sg)).astype(BF16)
            return carry

        lax.fori_loop(0, tc // CONV_ROWS, rows, 0)

        for k0 in range(0, CONV_K, CONV_TAP_GROUP):
            ks = list(range(k0, min(k0 + CONV_TAP_GROUP, CONV_K)))

            def taps(j, accs, ks=ks):
                out = list(accs)
                for t in range(CONV_TILE_UNROLL):
                    r0 = pl.multiple_of((j * CONV_TILE_UNROLL + t) * 8, 8)
                    dct = dc_ref[pl.ds(r0, 8), :]
                    for q, k in enumerate(ks):
                        a, b = divmod(off + k, 8)
                        out[q] = out[q] + dct * ext_u[b, pl.ds(pl.multiple_of(r0 + 8 * a, 8), 8), :]
                return tuple(out)

            accs = lax.fori_loop(0, tc // (8 * CONV_TILE_UNROLL), taps, tuple(jnp.zeros((8, D), F32) for _ in ks))
            for k, a_k in zip(ks, accs):
                acc[k] += a_k

        @pl.when(i == nt - 1)
        def _():
            ddw_ref[...] = jnp.zeros_like(ddw_ref)
            for k in range(CONV_K):
                ddw_ref[k:k + 1, :] = jnp.sum(acc[k], axis=0, keepdims=True)

    return pl.pallas_call(
        body, grid=(nt,),
        in_specs=[_rows(tc, D), pl.BlockSpec((CONV_HALO, D), lambda i: (nxt(i), 0)),
                  _rows(tc, D, 0), _rows(tc, D, 1),
                  pl.BlockSpec((CONV_HALO, D), lambda i: (prev(i), 0)), pl.BlockSpec((CONV_HALO, D), lambda i: (prev(i), 1)),
                  _const((CONV_HALO, D)), _rows(tc, D_SSM), _rows(tc, 2 * D)],
        out_specs=[_rows(tc, D_IN), _const((CONV_HALO, D))],
        out_shape=[jax.ShapeDtypeStruct((T, D_IN), BF16), jax.ShapeDtypeStruct((CONV_HALO, D), F32)],
        scratch_shapes=[pltpu.VMEM((8, CONV_HALO + tc, D), F32), pltpu.VMEM((8, CONV_HALO + tc, D), F32),
                        pltpu.VMEM((CONV_K, 8, D), F32)],
        compiler_params=_cparams(("arbitrary",)), name=name)(dc, dc, p, p, p, p, dw, du_ssm, dgates)


GATE_A0 = (2 * D_MODEL + D_SSM) // 512
GATE_B0 = GATE_A0 + 2


def merge_fwd(p, ya, z, *, name):
    T = p.shape[0]
    D = D_MODEL
    tc = _pick(T, [512, 256, 128])
    W = 512

    def body(ga_ref, gb_ref, ya_ref, z1_ref, z2_ref, o_ref):
        yb = z1_ref[...] * _sigmoid(z2_ref[...])
        o_ref[...] = (_sigmoid(ga_ref[...]) * ya_ref[...] + _sigmoid(gb_ref[...]) * yb).astype(BF16)

    return pl.pallas_call(
        body, grid=(T // tc, D // W),
        in_specs=[pl.BlockSpec((tc, W), lambda i, j: (i, GATE_A0 + j)), pl.BlockSpec((tc, W), lambda i, j: (i, GATE_B0 + j)),
                  pl.BlockSpec((tc, W), lambda i, j: (i, j)), pl.BlockSpec((tc, W), lambda i, j: (i, j)),
                  pl.BlockSpec((tc, W), lambda i, j: (i, D // W + j))],
        out_specs=pl.BlockSpec((tc, W), lambda i, j: (i, j)),
        out_shape=jax.ShapeDtypeStruct((T, D), BF16),
        compiler_params=_cparams(("arbitrary", "arbitrary")), name=name)(p, p, ya, z, z)


def merge_bwd(dm, p, ya, z, *, name):
    T = p.shape[0]
    D = D_MODEL
    tc = _pick(T, [256, 128])
    W = 512
    nb = D // W

    def body(dm_ref, ga0_ref, ga1_ref, gb0_ref, gb1_ref, ya_ref, z_ref, dya_ref, dg_ref, dz_ref):
        for j, (ga_ref, gb_ref) in enumerate(((ga0_ref, gb0_ref), (ga1_ref, gb1_ref))):
            c0 = slice(j * W, (j + 1) * W)
            c1 = slice(D + j * W, D + (j + 1) * W)
            dm = dm_ref[:, c0]
            sa = _sigmoid(ga_ref[...])
            sb = _sigmoid(gb_ref[...])
            s2 = _sigmoid(z_ref[:, c1])
            z1 = z_ref[:, c0]
            yb = z1 * s2
            dya_ref[:, c0] = (dm * sa).astype(BF16)
            dg_ref[:, c0] = (dm * ya_ref[:, c0] * sa * (1.0 - sa)).astype(BF16)
            dg_ref[:, c1] = (dm * yb * sb * (1.0 - sb)).astype(BF16)
            dyb = dm * sb
            dz_ref[:, c0] = (dyb * s2).astype(BF16)
            dz_ref[:, c1] = (dyb * z1 * s2 * (1.0 - s2)).astype(BF16)

    gate = lambda cb: pl.BlockSpec((tc, W), lambda i: (i, cb))
    return pl.pallas_call(
        body, grid=(T // tc,),
        in_specs=[_rows(tc, D), gate(GATE_A0), gate(GATE_A0 + 1), gate(GATE_B0), gate(GATE_B0 + 1), _rows(tc, D),
                  _rows(tc, 2 * D)],
        out_specs=[_rows(tc, D), _rows(tc, 2 * D), _rows(tc, 2 * D)],
        out_shape=[jax.ShapeDtypeStruct((T, D), BF16), jax.ShapeDtypeStruct((T, 2 * D), BF16),
                   jax.ShapeDtypeStruct((T, 2 * D), BF16)],
        compiler_params=_cparams(("arbitrary",)), name=name)(dm, p, p, p, p, ya, z)


def _scan_block(src_r, src_i, dst_r, dst_i, car_r, car_i, pw_r, pw_i, cw_r, cw_i, ntiles, reverse, extra=None):
    W = src_r.shape[1]
    rows = lax.broadcasted_iota(jnp.int32, (8, W), 0)
    steps = []
    for d, pr in ((1, 0), (2, 1), (4, 3)):
        valid = rows < 8 - d if reverse else rows >= d
        steps.append((d, jnp.where(valid, jnp.broadcast_to(pw_r[pr:pr + 1, :], (8, W)), 0.0),
                      jnp.where(valid, jnp.broadcast_to(pw_i[pr:pr + 1, :], (8, W)), 0.0)))
    cw_r, cw_i = cw_r[...], cw_i[...]

    def tile(jj, carry):
        j = ntiles - 1 - jj if reverse else jj
        sl = pl.ds(pl.multiple_of(j * 8, 8), 8)
        xr, xi = src_r[sl, :], src_i[sl, :]
        for d, lr, li in steps:
            sr = pltpu.roll(xr, 8 - d if reverse else d, 0)
            si = pltpu.roll(xi, 8 - d if reverse else d, 0)
            xr, xi = xr + lr * sr - li * si, xi + lr * si + li * sr
        cr, ci = car_r[...], car_i[...]
        xr, xi = xr + cw_r * cr - cw_i * ci, xi + cw_r * ci + cw_i * cr
        dst_r[sl, :] = xr
        dst_i[sl, :] = xi
        edge = 0 if reverse else 7
        car_r[...] = jnp.broadcast_to(xr[edge:edge + 1, :], (8, W))
        car_i[...] = jnp.broadcast_to(xi[edge:edge + 1, :], (8, W))
        if extra is not None:
            carry = extra(j, xr, xi, carry)
        return carry

    return tile


def ssm_fwd(p, Br, Bi, Cr, Ci, pw_r, pw_i, dvec, *, name):
    T = p.shape[0]
    tt = _pick(T, [512, 256, 128])
    nt = T // tt
    WI, WS = SSM_BLOCK_IN, SSM_BLOCK_STATE
    u0 = 2 * D_MODEL // WI

    def body(u_ref, br_ref, bi_ref, cr_ref, ci_ref, pwr_ref, pwi_ref, d_ref, xr_ref, xi_ref, y_ref, bur, bui, car_r, car_i):
        i = pl.program_id(1)

        @pl.when(i == 0)
        def _():
            car_r[...] = jnp.zeros_like(car_r)
            car_i[...] = jnp.zeros_like(car_i)

        u = u_ref[...]
        ub = u.astype(BF16)
        bur[...] = jnp.dot(ub, br_ref[...].astype(BF16), preferred_element_type=F32)
        bui[...] = jnp.dot(ub, bi_ref[...].astype(BF16), preferred_element_type=F32)
        tile = _scan_block(bur, bui, xr_ref, xi_ref, car_r, car_i, pwr_ref, pwi_ref, pwr_ref, pwi_ref, tt // 8, False)
        lax.fori_loop(0, tt // 8, tile, 0)
        y = (jnp.dot(xr_ref[...].astype(BF16), cr_ref[...].astype(BF16), preferred_element_type=F32)
             - jnp.dot(xi_ref[...].astype(BF16), ci_ref[...].astype(BF16), preferred_element_type=F32)
             + d_ref[...] * u)
        y_ref[...] = y.astype(BF16)

    return pl.pallas_call(
        body, grid=(SSM_BLOCKS, nt),
        in_specs=[pl.BlockSpec((tt, WI), lambda b, i: (i, u0 + b)),
                  pl.BlockSpec((None, WI, WS), lambda b, i: (b, 0, 0)), pl.BlockSpec((None, WI, WS), lambda b, i: (b, 0, 0)),
                  pl.BlockSpec((None, WS, WI), lambda b, i: (b, 0, 0)), pl.BlockSpec((None, WS, WI), lambda b, i: (b, 0, 0)),
                  pl.BlockSpec((8, WS), lambda b, i: (0, b)), pl.BlockSpec((8, WS), lambda b, i: (0, b)),
                  pl.BlockSpec((1, WI), lambda b, i: (0, b))],
        out_specs=[pl.BlockSpec((tt, WS), lambda b, i: (i, b)), pl.BlockSpec((tt, WS), lambda b, i: (i, b)),
                   pl.BlockSpec((tt, WI), lambda b, i: (i, b))],
        out_shape=[jax.ShapeDtypeStruct((T, SSM_BLOCKS * WS), F32)] * 2 + [jax.ShapeDtypeStruct((T, D_SSM), BF16)],
        scratch_shapes=[pltpu.VMEM((tt, WS), F32), pltpu.VMEM((tt, WS), F32), pltpu.VMEM((8, WS), F32), pltpu.VMEM((8, WS), F32)],
        compiler_params=_cparams(("arbitrary", "arbitrary")), name=name)(p, Br, Bi, Cr, Ci, pw_r, pw_i, dvec)


def ssm_bwd(dy, p, xr, xi, Br, Bi, Cr, Ci, pwc_r, pwc_i, cwc_r, cwc_i, dvec, *, name):
    T = p.shape[0]
    tt = _pick(T, [512, 256, 128])
    nt = T // tt
    WI, WS = SSM_BLOCK_IN, SSM_BLOCK_STATE
    u0 = 2 * D_MODEL // WI
    tb = lambda i: nt - 1 - i
    xprev = lambda i: jnp.maximum(tb(i) * (tt // 8) - 1, 0)
    tn_dims = _DIMS["tn"]
    nt_dims = _DIMS["nt"]

    def body(dy_ref, u_ref, xr_ref, xi_ref, xpr_ref, xpi_ref, br_ref, bi_ref, cr_ref, ci_ref, pwr_ref, pwi_ref,
             cwr_ref, cwi_ref, d_ref,
             du_ref, dbr_ref, dbi_ref, dcr_ref, dci_ref, dar_ref, dai_ref, dd_ref,
             gr, gi, ext_r, ext_i, car_r, car_i):
        i = pl.program_id(1)

        @pl.when(i == 0)
        def _():
            car_r[...] = jnp.zeros_like(car_r)
            car_i[...] = jnp.zeros_like(car_i)
            dbr_ref[...] = jnp.zeros_like(dbr_ref)
            dbi_ref[...] = jnp.zeros_like(dbi_ref)
            dcr_ref[...] = jnp.zeros_like(dcr_ref)
            dci_ref[...] = jnp.zeros_like(dci_ref)
            dar_ref[...] = jnp.zeros_like(dar_ref)
            dai_ref[...] = jnp.zeros_like(dai_ref)
            dd_ref[...] = jnp.zeros_like(dd_ref)

        dy = dy_ref[...]
        dyb = dy.astype(BF16)
        u = u_ref[...]
        ub = u.astype(BF16)
        gr[...] = lax.dot_general(dyb, cr_ref[...].astype(BF16), nt_dims, preferred_element_type=F32)
        gi[...] = -lax.dot_general(dyb, ci_ref[...].astype(BF16), nt_dims, preferred_element_type=F32)
        first = tb(i) == 0
        ext_r[0:8, :] = jnp.where(first, 0.0, xpr_ref[...])
        ext_i[0:8, :] = jnp.where(first, 0.0, xpi_ref[...])
        ext_r[8:8 + tt, :] = xr_ref[...]
        ext_i[8:8 + tt, :] = xi_ref[...]
        rows = lax.broadcasted_iota(jnp.int32, (8, WS), 0)

        def lam_grad(j, g_r, g_i, carry):
            a_r, a_i = carry
            cur = pl.ds(pl.multiple_of(j * 8 + 8, 8), 8)
            prv = pl.ds(pl.multiple_of(j * 8, 8), 8)
            xc_r, xc_i = ext_r[cur, :], ext_i[cur, :]
            xl_r, xl_i = ext_r[prv, :], ext_i[prv, :]
            xp_r = jnp.where(rows == 0, jnp.broadcast_to(xl_r[7:8, :], (8, WS)), pltpu.roll(xc_r, 1, 0))
            xp_i = jnp.where(rows == 0, jnp.broadcast_to(xl_i[7:8, :], (8, WS)), pltpu.roll(xc_i, 1, 0))
            return (a_r + g_r * xp_r + g_i * xp_i, a_i + g_i * xp_r - g_r * xp_i)

        tile = _scan_block(gr, gi, gr, gi, car_r, car_i, pwr_ref, pwi_ref, cwr_ref, cwi_ref, tt // 8, True, extra=lam_grad)
        z8 = jnp.zeros((8, WS), F32)
        a_r, a_i = lax.fori_loop(0, tt // 8, tile, (z8, z8))
        dar_ref[...] += a_r
        dai_ref[...] += a_i
        grb = gr[...].astype(BF16)
        gib = gi[...].astype(BF16)
        dbr_ref[...] += lax.dot_general(ub, grb, tn_dims, preferred_element_type=F32)
        dbi_ref[...] += lax.dot_general(ub, gib, tn_dims, preferred_element_type=F32)
        dcr_ref[...] += lax.dot_general(xr_ref[...].astype(BF16), dyb, tn_dims, preferred_element_type=F32)
        dci_ref[...] -= lax.dot_general(xi_ref[...].astype(BF16), dyb, tn_dims, preferred_element_type=F32)
        du = (lax.dot_general(grb, br_ref[...].astype(BF16), nt_dims, preferred_element_type=F32)
              + lax.dot_general(gib, bi_ref[...].astype(BF16), nt_dims, preferred_element_type=F32)
              + d_ref[...] * dy)
        du_ref[...] = du.astype(BF16)
        dd_ref[...] += _rowsum8(dy * u)

    wspec = lambda shp: pl.BlockSpec((None,) + shp, lambda b, i: (b, 0, 0))
    return pl.pallas_call(
        body, grid=(SSM_BLOCKS, nt),
        in_specs=[pl.BlockSpec((tt, WI), lambda b, i: (tb(i), b)),
                  pl.BlockSpec((tt, WI), lambda b, i: (tb(i), u0 + b)),
                  pl.BlockSpec((tt, WS), lambda b, i: (tb(i), b)), pl.BlockSpec((tt, WS), lambda b, i: (tb(i), b)),
                  pl.BlockSpec((8, WS), lambda b, i: (xprev(i), b)), pl.BlockSpec((8, WS), lambda b, i: (xprev(i), b)),
                  wspec((WI, WS)), wspec((WI, WS)), wspec((WS, WI)), wspec((WS, WI)),
                  pl.BlockSpec((8, WS), lambda b, i: (0, b)), pl.BlockSpec((8, WS), lambda b, i: (0, b)),
                  pl.BlockSpec((8, WS), lambda b, i: (0, b)), pl.BlockSpec((8, WS), lambda b, i: (0, b)),
                  pl.BlockSpec((1, WI), lambda b, i: (0, b))],
        out_specs=[pl.BlockSpec((tt, WI), lambda b, i: (tb(i), b)),
                   wspec((WI, WS)), wspec((WI, WS)), wspec((WS, WI)), wspec((WS, WI)),
                   pl.BlockSpec((8, WS), lambda b, i: (0, b)), pl.BlockSpec((8, WS), lambda b, i: (0, b)),
                   pl.BlockSpec((8, WI), lambda b, i: (0, b))],
        out_shape=[jax.ShapeDtypeStruct((T, D_SSM), BF16),
                   jax.ShapeDtypeStruct((SSM_BLOCKS, WI, WS), F32), jax.ShapeDtypeStruct((SSM_BLOCKS, WI, WS), F32),
                   jax.ShapeDtypeStruct((SSM_BLOCKS, WS, WI), F32), jax.ShapeDtypeStruct((SSM_BLOCKS, WS, WI), F32),
                   jax.ShapeDtypeStruct((8, SSM_BLOCKS * WS), F32), jax.ShapeDtypeStruct((8, SSM_BLOCKS * WS), F32),
                   jax.ShapeDtypeStruct((8, D_SSM), F32)],
        scratch_shapes=[pltpu.VMEM((tt, WS), F32), pltpu.VMEM((tt, WS), F32),
                        pltpu.VMEM((tt + 8, WS), F32), pltpu.VMEM((tt + 8, WS), F32),
                        pltpu.VMEM((8, WS), F32), pltpu.VMEM((8, WS), F32)],
        compiler_params=_cparams(("arbitrary", "arbitrary")), name=name,
    )(dy, p, xr, xi, xr, xi, Br, Bi, Cr, Ci, pwc_r, pwc_i, cwc_r, cwc_i, dvec)


def _ssm_discretise(log_step, lam_re, lam_im, b_re, b_im):
    step = jnp.exp(log_step)[:, None]
    mag = jnp.exp(lam_re * step)
    ar = mag * jnp.cos(lam_im * step)
    ai = mag * jnp.sin(lam_im * step)
    den = lam_re * lam_re + lam_im * lam_im
    nr = ar - 1.0
    cr = (nr * lam_re + ai * lam_im) / den
    ci = (ai * lam_re - nr * lam_im) / den
    bbr = cr[..., None] * b_re - ci[..., None] * b_im
    bbi = cr[..., None] * b_im + ci[..., None] * b_re
    return ar, ai, bbr, bbi


def _blockdiag_in(bb):
    t = jnp.transpose(bb, (0, 2, 1)).reshape(SSM_BLOCKS, 8, SSM_GROUP, SSM_STATE)
    eye = jnp.eye(8, dtype=bb.dtype)
    return (t[:, :, :, None, :] * eye[None, :, None, :, None]).reshape(SSM_BLOCKS, SSM_BLOCK_IN, SSM_BLOCK_STATE)


def _blockdiag_out(cc):
    t = jnp.transpose(cc, (0, 2, 1)).reshape(SSM_BLOCKS, 8, SSM_STATE, SSM_GROUP)
    eye = jnp.eye(8, dtype=cc.dtype)
    return (t[:, :, :, None, :] * eye[None, :, None, :, None]).reshape(SSM_BLOCKS, SSM_BLOCK_STATE, SSM_BLOCK_IN)


def _diag_in(d):
    t = d.reshape(SSM_BLOCKS, 8, SSM_GROUP, 8, SSM_STATE)
    t = jnp.einsum("bghgp->bghp", t).reshape(SSM_GROUPS, SSM_GROUP, SSM_STATE)
    return jnp.transpose(t, (0, 2, 1))


def _diag_out(d):
    t = d.reshape(SSM_BLOCKS, 8, SSM_STATE, 8, SSM_GROUP)
    t = jnp.einsum("bgpgh->bgph", t).reshape(SSM_GROUPS, SSM_STATE, SSM_GROUP)
    return jnp.transpose(t, (0, 2, 1))


def _powers(ar, ai):
    rs, is_ = [ar], [ai]
    for _ in range(7):
        r, i = rs[-1], is_[-1]
        rs.append(r * ar - i * ai)
        is_.append(r * ai + i * ar)
    return jnp.stack(rs), jnp.stack(is_), jnp.stack(rs[::-1]), jnp.stack(is_[::-1])


def attn_fwd(q, kv, *, name):
    T, D = q.shape
    nm = kv.shape[0]
    tq = _pick(T, [512, 256, 128])
    scale = HEAD_DIM ** -0.5

    def body(q_ref, k_ref, v_ref, o_ref):
        for h in range(N_HEADS):
            sl = slice(h * HEAD_DIM, (h + 1) * HEAD_DIM)
            s = lax.dot_general(q_ref[:, sl], k_ref[:, sl].astype(BF16), _DIMS["nt"], preferred_element_type=F32) * scale
            e = jnp.exp(s - jnp.max(s, axis=-1, keepdims=True))
            pr = e / jnp.sum(e, axis=-1, keepdims=True)
            o_ref[:, sl] = jnp.dot(pr.astype(BF16), v_ref[:, sl].astype(BF16), preferred_element_type=F32).astype(BF16)

    return pl.pallas_call(
        body, grid=(T // tq,),
        in_specs=[_rows(tq, D), pl.BlockSpec((nm, D), lambda i: (0, 0)), pl.BlockSpec((nm, D), lambda i: (0, 1))],
        out_specs=_rows(tq, D), out_shape=jax.ShapeDtypeStruct((T, D), BF16),
        compiler_params=_cparams(("arbitrary",)), name=name)(q, kv, kv)


def attn_bwd(q, kv, do, *, name):
    T, D = q.shape
    nm = kv.shape[0]
    tq = _pick(T, [512, 256, 128])
    nt = T // tq
    scale = HEAD_DIM ** -0.5

    def body(q_ref, k_ref, v_ref, do_ref, dq_ref, dkv_ref):
        i = pl.program_id(0)

        @pl.when(i == 0)
        def _():
            dkv_ref[...] = jnp.zeros_like(dkv_ref)

        for h in range(N_HEADS):
            sl = slice(h * HEAD_DIM, (h + 1) * HEAD_DIM)
            slv = slice(D + h * HEAD_DIM, D + (h + 1) * HEAD_DIM)
            qh = q_ref[:, sl]
            kh = k_ref[:, sl].astype(BF16)
            vh = v_ref[:, sl].astype(BF16)
            doh = do_ref[:, sl].astype(BF16)
            s = lax.dot_general(qh, kh, _DIMS["nt"], preferred_element_type=F32) * scale
            e = jnp.exp(s - jnp.max(s, axis=-1, keepdims=True))
            pr = e / jnp.sum(e, axis=-1, keepdims=True)
            dp = lax.dot_general(doh, vh, _DIMS["nt"], preferred_element_type=F32)
            ds = (pr * (dp - jnp.sum(pr * dp, axis=-1, keepdims=True)) * scale).astype(BF16)
            dq_ref[:, sl] = jnp.dot(ds, kh, preferred_element_type=F32).astype(BF16)
            dkv_ref[:, sl] += lax.dot_general(ds, qh, _DIMS["tn"], preferred_element_type=F32)
            dkv_ref[:, slv] += lax.dot_general(pr.astype(BF16), doh, _DIMS["tn"], preferred_element_type=F32)

    return pl.pallas_call(
        body, grid=(nt,),
        in_specs=[_rows(tq, D), pl.BlockSpec((nm, D), lambda i: (0, 0)), pl.BlockSpec((nm, D), lambda i: (0, 1)), _rows(tq, D)],
        out_specs=[_rows(tq, D), _const((nm, 2 * D))],
        out_shape=[jax.ShapeDtypeStruct((T, D), BF16), jax.ShapeDtypeStruct((nm, 2 * D), F32)],
        compiler_params=_cparams(("arbitrary",)), name=name)(q, kv, kv, do)


def _adam_math(w, g, m, v):
    m = ADAM_B1 * m + (1.0 - ADAM_B1) * g
    v = ADAM_B2 * v + (1.0 - ADAM_B2) * (g * g)
    m_hat = m / (1.0 - ADAM_B1 ** ADAM_STEP)
    v_hat = v / (1.0 - ADAM_B2 ** ADAM_STEP)
    delta = -ADAM_LR * (m_hat / (jnp.sqrt(v_hat) + ADAM_EPS) + ADAM_WD * w)
    return delta, m, v


def adamw(w, m, v, g_arr, g_row0, *, name):
    R, C = w.shape
    tr = _pick(R, [256, 128, 64, 32, 16, 8])
    assert g_row0 % tr == 0
    g0 = g_row0 // tr

    def body(w_ref, m_ref, v_ref, g_ref, go_ref, d_ref, mo_ref, vo_ref):
        g = g_ref[...]
        d, mn, vn = _adam_math(w_ref[...], g, m_ref[...], v_ref[...])
        go_ref[...] = g
        d_ref[...] = d
        mo_ref[...] = mn
        vo_ref[...] = vn

    sp = pl.BlockSpec((tr, C), lambda i: (i, 0))
    return pl.pallas_call(
        body, grid=(R // tr,), in_specs=[sp, sp, sp, pl.BlockSpec((tr, C), lambda i: (g0 + i, 0))],
        out_specs=[sp] * 4, out_shape=[jax.ShapeDtypeStruct((R, C), F32)] * 4,
        compiler_params=_cparams(("arbitrary",)), name=name)(w, m, v, g_arr)


def _place():
    x, y, c = lax.axis_index("x"), lax.axis_index("y"), lax.axis_index("c")
    chips = [(1 - x, y), (x, 1 - y), (1 - x, 1 - y)]
    return x, y, c, chips


ANY = pl.BlockSpec(memory_space=pl.ANY)


def allgather_weights(bufs, *, name):
    n = len(bufs)

    def body(*refs):
        o_refs = refs[n:2 * n]
        send_sems, recv_sems, fsend_sems, frecv_sems = refs[2 * n:]
        x, y, c, chips = _place()
        k_me = 2 * x + y
        sib = (x, y, 1 - c)
        halves = [b.shape[1] // 2 for b in bufs]

        def half(a, cc):
            return pl.ds(pl.multiple_of(cc * halves[a], 16), halves[a])

        sends = []
        for a in range(n):
            for r, (px, py) in enumerate(chips):
                cp = pltpu.make_async_remote_copy(
                    src_ref=o_refs[a].at[k_me, half(a, c)], dst_ref=o_refs[a].at[k_me, half(a, c)],
                    send_sem=send_sems.at[3 * a + r], recv_sem=recv_sems.at[3 * a + r],
                    device_id=(px, py, c), device_id_type=MESH)
                cp.start()
                sends.append(cp)
        passed = []
        for a in range(n):
            for r, (px, py) in enumerate(chips):
                win = o_refs[a].at[2 * px + py, half(a, c)]
                pltpu.make_async_remote_copy(
                    src_ref=win, dst_ref=win, send_sem=send_sems.at[3 * a + r], recv_sem=recv_sems.at[3 * a + r],
                    device_id=(px, py, c), device_id_type=MESH).wait_recv()
                cp = pltpu.make_async_remote_copy(
                    src_ref=win, dst_ref=win, send_sem=fsend_sems.at[3 * a + r], recv_sem=frecv_sems.at[3 * a + r],
                    device_id=sib, device_id_type=MESH)
                cp.start()
                passed.append(cp)
        for a in range(n):
            for r, (px, py) in enumerate(chips):
                win = o_refs[a].at[2 * px + py, half(a, 1 - c)]
                pltpu.make_async_remote_copy(
                    src_ref=win, dst_ref=win, send_sem=fsend_sems.at[3 * a + r], recv_sem=frecv_sems.at[3 * a + r],
                    device_id=sib, device_id_type=MESH).wait_recv()
        for cp in sends + passed:
            cp.wait_send()

    return pl.pallas_call(
        body, in_specs=[ANY] * n, out_specs=[ANY] * n,
        out_shape=[jax.ShapeDtypeStruct(b.shape, b.dtype) for b in bufs],
        scratch_shapes=[pltpu.SemaphoreType.DMA((3 * n,))] * 4,
        input_output_aliases={a: a for a in range(n)},
        name=name)(*bufs)


HBM_SPEC = pl.BlockSpec(memory_space=pltpu.HBM)
SEM_SPEC = pl.BlockSpec(memory_space=pltpu.SEMAPHORE)


def _hbm(a):
    return pltpu.with_memory_space_constraint(a, pltpu.HBM)


def gather_start(bufs, pieces, *, name):
    n = len(bufs)
    npc = len(pieces)

    def body(*refs):
        b_refs = refs[:n]
        send_sems, recv_sems = refs[n], refs[n + 1]
        x, y, c, chips = _place()
        k_me = 2 * x + y
        for q, (a, row0, rows) in enumerate(pieces):
            win = b_refs[a].at[k_me, pl.ds(row0, rows)]
            for r, (px, py) in enumerate(chips):
                pltpu.make_async_remote_copy(
                    src_ref=win, dst_ref=win, send_sem=send_sems.at[3 * q + r], recv_sem=recv_sems.at[3 * q + r],
                    device_id=(px, py, c), device_id_type=MESH).start()

    return pl.pallas_call(
        body, in_specs=[HBM_SPEC] * n, out_specs=[SEM_SPEC, SEM_SPEC] + [HBM_SPEC] * n,
        out_shape=[pltpu.SemaphoreType.DMA((3 * npc,)), pltpu.SemaphoreType.DMA((3 * npc,))]
        + [pltpu.HBM(b.shape, b.dtype) for b in bufs],
        input_output_aliases={a: 2 + a for a in range(n)},
        compiler_params=pltpu.CompilerParams(has_side_effects=pltpu.SideEffectType.DATAFLOW_SIDE_EFFECTING),
        name=name)(*[_hbm(b) for b in bufs])


def gather_wait(send_sems, recv_sems, bufs, which, after, *, name):
    n = len(bufs)

    def body(*refs):
        b_refs = refs[:n]
        send_sems, recv_sems = refs[n], refs[n + 1]
        x, y, c, chips = _place()
        k_me = 2 * x + y
        for a, row0, rows, q in which:
            for r, (px, py) in enumerate(chips):
                cp = pltpu.make_async_remote_copy(
                    src_ref=b_refs[a].at[k_me, pl.ds(row0, rows)], dst_ref=b_refs[a].at[2 * px + py, pl.ds(row0, rows)],
                    send_sem=send_sems.at[3 * q + r], recv_sem=recv_sems.at[3 * q + r],
                    device_id=(px, py, c), device_id_type=MESH)
                cp.wait_send()
                cp.wait_recv()

    return pl.pallas_call(
        body, in_specs=[HBM_SPEC] * n + [SEM_SPEC, SEM_SPEC, ANY], out_specs=[HBM_SPEC] * n,
        out_shape=[pltpu.HBM(b.shape, b.dtype) for b in bufs],
        input_output_aliases={a: a for a in range(n)},
        compiler_params=pltpu.CompilerParams(has_side_effects=pltpu.SideEffectType.DATAFLOW_SIDE_EFFECTING),
        name=name)(*bufs, send_sems, recv_sems, after)


def exchange_halves(grads, *, name):
    n = len(grads)

    def body(*refs):
        g_refs, l_refs = refs[:n], refs[n:2 * n]
        send_sems, recv_sems = refs[2 * n:]
        x, y, c, _ = _place()
        cps = []
        for a in range(n):
            h = grads[a].shape[1] // 2
            cp = pltpu.make_async_remote_copy(
                src_ref=g_refs[a].at[:, pl.ds(pl.multiple_of((1 - c) * h, 8), h)], dst_ref=l_refs[a],
                send_sem=send_sems.at[a], recv_sem=recv_sems.at[a], device_id=(x, y, 1 - c), device_id_type=MESH)
            cp.start()
            cps.append(cp)
        for cp in cps:
            cp.wait()

    return pl.pallas_call(
        body, in_specs=[ANY] * n, out_specs=[ANY] * n,
        out_shape=[jax.ShapeDtypeStruct((g.shape[0], g.shape[1] // 2, g.shape[2]), g.dtype) for g in grads],
        scratch_shapes=[pltpu.SemaphoreType.DMA((n,))] * 2,
        name=name)(*grads)


N_PEERS = N_DEV - 1


def _scatter_copies(p_refs, l_refs, send_sems, recv_sems):
    x, y, c, _ = _place()
    cps = []
    for a in range(len(p_refs)):
        h = p_refs[a].shape[1] // 2
        for fx, fy in ((0, 0), (1, 0), (0, 1), (1, 1)):
            for fc in (0, 1):
                if (fx, fy, fc) == (0, 0, 0):
                    continue
                slot = 2 * (fx + 2 * fy) + fc - 1
                px, py, pc = (1 - x if fx else x), (1 - y if fy else y), (1 - c if fc else c)
                cps.append(pltpu.make_async_remote_copy(
                    src_ref=p_refs[a].at[2 * px + py, pl.ds(pl.multiple_of(pc * h, 16), h)], dst_ref=l_refs[a].at[slot],
                    send_sem=send_sems.at[N_PEERS * a + slot], recv_sem=recv_sems.at[N_PEERS * a + slot],
                    device_id=(px, py, pc), device_id_type=MESH))
    return cps


def scatter_start(parts, *, name):
    n = len(parts)
    lands = [lax.empty((N_PEERS, p.shape[1] // 2, p.shape[2]), p.dtype) for p in parts]

    def body(*refs):
        for cp in _scatter_copies(refs[:n], refs[n:2 * n], refs[2 * n], refs[2 * n + 1]):
            cp.start()

    outs = pl.pallas_call(
        body, in_specs=[HBM_SPEC] * (2 * n), out_specs=[SEM_SPEC, SEM_SPEC] + [HBM_SPEC] * (2 * n),
        out_shape=[pltpu.SemaphoreType.DMA((N_PEERS * n,)), pltpu.SemaphoreType.DMA((N_PEERS * n,))]
        + [pltpu.HBM(a.shape, a.dtype) for a in parts + lands],
        input_output_aliases={a: 2 + a for a in range(2 * n)},
        compiler_params=pltpu.CompilerParams(has_side_effects=pltpu.SideEffectType.DATAFLOW_SIDE_EFFECTING),
        name=name)(*[_hbm(a) for a in parts + lands])
    return outs[0], outs[1], list(outs[2:2 + n]), list(outs[2 + n:])


def scatter_wait(rounds, after, *, name):
    sizes = [len(r[2]) for r in rounds]
    flat = [a for r in rounds for a in r[2] + r[3]]
    sems = [s for r in rounds for s in (r[0], r[1])]
    nflat = len(flat)

    def body(*refs):
        pos = 0
        for ri, n in enumerate(sizes):
            for cp in _scatter_copies(refs[pos:pos + n], refs[pos + n:pos + 2 * n], refs[nflat + 2 * ri], refs[nflat + 2 * ri + 1]):
                cp.wait_send()
                cp.wait_recv()
            pos += 2 * n

    outs = pl.pallas_call(
        body, in_specs=[HBM_SPEC] * nflat + [SEM_SPEC] * len(sems) + [ANY], out_specs=[HBM_SPEC] * nflat,
        out_shape=[pltpu.HBM(a.shape, a.dtype) for a in flat],
        input_output_aliases={a: a for a in range(nflat)},
        compiler_params=pltpu.CompilerParams(has_side_effects=pltpu.SideEffectType.DATAFLOW_SIDE_EFFECTING),
        name=name)(*flat, *sems, after)
    res, pos = [], 0
    for n in sizes:
        res.append((list(outs[pos:pos + n]), list(outs[pos + n:pos + 2 * n])))
        pos += 2 * n
    return res


def join_halves(fulls, *, name):
    n = len(fulls)

    def body(*refs):
        o_refs = refs[n:2 * n]
        send_sems, recv_sems = refs[2 * n:]
        x, y, c, _ = _place()
        cps = []
        for a in range(n):
            h = fulls[a].shape[0] // 2
            win = o_refs[a].at[pl.ds(pl.multiple_of(c * h, 8), h)]
            cp = pltpu.make_async_remote_copy(
                src_ref=win, dst_ref=win, send_sem=send_sems.at[a], recv_sem=recv_sems.at[a],
                device_id=(x, y, 1 - c), device_id_type=MESH)
            cp.start()
            cps.append(cp)
        for a in range(n):
            h = fulls[a].shape[0] // 2
            other = o_refs[a].at[pl.ds(pl.multiple_of((1 - c) * h, 8), h)]
            pltpu.make_async_remote_copy(
                src_ref=other, dst_ref=other, send_sem=send_sems.at[a], recv_sem=recv_sems.at[a],
                device_id=(x, y, 1 - c), device_id_type=MESH).wait_recv()
        for cp in cps:
            cp.wait_send()

    return pl.pallas_call(
        body, in_specs=[ANY] * n, out_specs=[ANY] * n,
        out_shape=[jax.ShapeDtypeStruct(f.shape, f.dtype) for f in fulls],
        scratch_shapes=[pltpu.SemaphoreType.DMA((n,))] * 2,
        input_output_aliases={a: a for a in range(n)},
        name=name)(*fulls)


def add_partials(part, land, kc, *, name):
    _, R, C = part.shape
    H = R // 2
    tr = _pick(H, [256, 128, 64, 32, 16])
    per = H // tr

    def body(kc_ref, p_ref, l_ref, o_ref):
        acc = p_ref[...].astype(F32)
        for s in range(N_PEERS):
            acc = acc + l_ref[s].astype(F32)
        o_ref[...] = acc

    return pl.pallas_call(
        body,
        grid_spec=pltpu.PrefetchScalarGridSpec(
            num_scalar_prefetch=1, grid=(per,),
            in_specs=[pl.BlockSpec((None, tr, C), lambda i, kc_ref: (kc_ref[0], kc_ref[1] * per + i, 0)),
                      pl.BlockSpec((N_PEERS, tr, C), lambda i, kc_ref: (0, i, 0))],
            out_specs=pl.BlockSpec((tr, C), lambda i, kc_ref: (kc_ref[1] * per + i, 0))),
        out_shape=jax.ShapeDtypeStruct((R, C), F32),
        compiler_params=_cparams(("arbitrary",)), name=name)(kc, part, land)


def allgather_sum(v, *, name):
    m_per, n = v.shape

    def body(x_ref, out_ref, sum_ref, send_sems, recv_sems, local_sem):
        x, y, c, chips = _place()
        me, sibling = (x, y, c), (x, y, 1 - c)

        def rows(px, py, pc):
            return out_ref.at[pl.ds(pl.multiple_of((4 * px + 2 * py + pc) * m_per, 8), m_per), :]

        def copy(k, block, to, src=None):
            return pltpu.make_async_remote_copy(
                src_ref=rows(*block) if src is None else src, dst_ref=rows(*block),
                send_sem=send_sems.at[k], recv_sem=recv_sems.at[k], device_id=to, device_id_type=MESH)

        mine = pltpu.make_async_copy(x_ref, rows(*me), local_sem)
        mine.start()
        first = [copy(0, me, sibling, src=x_ref)]
        first += [copy(1 + j, me, (*chip, c), src=x_ref) for j, chip in enumerate(chips)]
        for cp in first:
            cp.start()
        passed = [copy(4 + j, (*chip, c), sibling) for j, chip in enumerate(chips)]
        for j, chip in enumerate(chips):
            copy(1 + j, (*chip, c), me).wait_recv()
            passed[j].start()
        copy(0, sibling, me).wait_recv()
        for j, chip in enumerate(chips):
            copy(4 + j, (*chip, 1 - c), me).wait_recv()
        for cp in first + passed:
            cp.wait_send()
        mine.wait()
        acc = out_ref[0:m_per, :]
        for d in range(1, N_DEV):
            acc = acc + out_ref[d * m_per:(d + 1) * m_per, :]
        sum_ref[...] = acc

    vm = pl.BlockSpec(memory_space=pltpu.VMEM)
    return pl.pallas_call(
        body, in_specs=[vm], out_specs=[vm, vm],
        out_shape=[jax.ShapeDtypeStruct((N_DEV * m_per, n), v.dtype), jax.ShapeDtypeStruct((m_per, n), v.dtype)],
        scratch_shapes=[pltpu.SemaphoreType.DMA((7,)), pltpu.SemaphoreType.DMA((7,)), pltpu.SemaphoreType.DMA],
        compiler_params=pltpu.CompilerParams(vmem_limit_bytes=VMEM_LIMIT_BYTES), name=name)(v)


def allreduce_two_level(v, *, name):
    m, n = v.shape
    h = m // 2

    def body(x_ref, out_ref, sib_ref, chip_ref, sems_send, sems_recv):
        x, y, c, chips = _place()
        k_me = 2 * x + y
        sib = (x, y, 1 - c)
        mine = pl.ds(pl.multiple_of(c * h, 8), h)
        other = pl.ds(pl.multiple_of((1 - c) * h, 8), h)

        def copy(q, src, dst, to):
            return pltpu.make_async_remote_copy(src_ref=src, dst_ref=dst, send_sem=sems_send.at[q], recv_sem=sems_recv.at[q],
                                                device_id=to, device_id_type=MESH)

        first = copy(0, x_ref.at[other], sib_ref, sib)
        first.start()
        first.wait()
        chip_ref[k_me] = x_ref[mine, :] + sib_ref[...]
        sends = [copy(1 + r, chip_ref.at[k_me], chip_ref.at[k_me], (px, py, c)) for r, (px, py) in enumerate(chips)]
        for cp in sends:
            cp.start()
        for r, (px, py) in enumerate(chips):
            copy(1 + r, chip_ref.at[2 * px + py], chip_ref.at[2 * px + py], (px, py, c)).wait_recv()
        for cp in sends:
            cp.wait_send()
        total = ((chip_ref[0] + chip_ref[1]) + chip_ref[2]) + chip_ref[3]
        out_ref[mine, :] = total
        last = copy(4, out_ref.at[mine], out_ref.at[mine], sib)
        last.start()
        copy(4, out_ref.at[other], out_ref.at[other], sib).wait_recv()
        last.wait_send()

    vm = pl.BlockSpec(memory_space=pltpu.VMEM)
    return pl.pallas_call(
        body, in_specs=[vm], out_specs=vm, out_shape=jax.ShapeDtypeStruct((m, n), v.dtype),
        scratch_shapes=[pltpu.VMEM((h, n), v.dtype), pltpu.VMEM((N_CHIPS, h, n), v.dtype),
                        pltpu.SemaphoreType.DMA((5,)), pltpu.SemaphoreType.DMA((5,))],
        compiler_params=pltpu.CompilerParams(vmem_limit_bytes=VMEM_LIMIT_BYTES), name=name)(v)


PACK_W = D_MODEL


def _pack_rows(shape):
    return -(-math.prod(shape) // PACK_W)


def _pack(arrs):
    cols = []
    for a in arrs:
        f = a.reshape(-1)
        pad = (-f.shape[0]) % PACK_W
        cols.append((jnp.pad(f, (0, pad)) if pad else f).reshape(-1, PACK_W))
    out = jnp.concatenate(cols, axis=0)
    pad = (-out.shape[0]) % 16
    return jnp.pad(out, ((0, pad), (0, 0)))


def _unpack(buf, shapes):
    outs, r = [], 0
    for s in shapes:
        nel = math.prod(s)
        nr = _pack_rows(s)
        outs.append(buf[r:r + nr].reshape(-1)[:nel].reshape(s))
        r += nr
    return outs


GA_CONV_OUT, GA_MIX_OUT, GA_WQ, GA_WO, GA_DOWN, GA_UP, GA_ROWS = 0, 256, 512, 768, 1024, 2048, 3072
G1_DOWN, G1_UP, G1_ROWS = 0, 1024, 2048
G2_CONV_OUT, G2_MIX_OUT, G2_WQ, G2_WO, G2_ROWS = 0, 256, 512, 768, 1024


def kernel(x, mem, in_norm_g, in_norm_b, w_in, conv_dw, conv_db, conv_norm_g, conv_norm_b, w_conv_out, ssm_log_step, ssm_lambda_re, ssm_lambda_im, ssm_b_re, ssm_b_im, ssm_c_re, ssm_c_im, ssm_d, w_ssm_glu, w_mix_out, ln1_g, ln1_b, xa_wq, xa_wkv, xa_wo, ln2_g, ln2_b, mlp_w_up, mlp_w_down, ln3_g, ln3_b, loss_target, m_in_norm_g, m_in_norm_b, m_w_in, m_conv_dw, m_conv_db, m_conv_norm_g, m_conv_norm_b, m_w_conv_out, m_ssm_log_step, m_ssm_lambda_re, m_ssm_lambda_im, m_ssm_b_re, m_ssm_b_im, m_ssm_c_re, m_ssm_c_im, m_ssm_d, m_w_ssm_glu, m_w_mix_out, m_ln1_g, m_ln1_b, m_xa_wq, m_xa_wkv, m_xa_wo, m_ln2_g, m_ln2_b, m_mlp_w_up, m_mlp_w_down, m_ln3_g, m_ln3_b, v_in_norm_g, v_in_norm_b, v_w_in, v_conv_dw, v_conv_db, v_conv_norm_g, v_conv_norm_b, v_w_conv_out, v_ssm_log_step, v_ssm_lambda_re, v_ssm_lambda_im, v_ssm_b_re, v_ssm_b_im, v_ssm_c_re, v_ssm_c_im, v_ssm_d, v_w_ssm_glu, v_w_mix_out, v_ln1_g, v_ln1_b, v_xa_wq, v_xa_wkv, v_xa_wo, v_ln2_g, v_ln2_b, v_mlp_w_up, v_mlp_w_down, v_ln3_g, v_ln3_b):
    D = D_MODEL
    xs = x[0]
    T = xs.shape[0]
    mems = mem[0]
    NM = mems.shape[0]
    tgt = loss_target[0]
    my_c = lax.axis_index("c")
    k_me = 2 * lax.axis_index("x") + lax.axis_index("y")
    c_arr = jnp.reshape(my_c, (1,)).astype(jnp.int32)
    k_arr = jnp.reshape(k_me, (1,)).astype(jnp.int32)

    sh_a = jnp.concatenate([w_conv_out[0], w_mix_out[0], xa_wq[0], xa_wo[0], mlp_w_down[0], mlp_w_up[0]], axis=0).astype(BF16)
    def own_block(shard):
        buf = lax.empty((N_CHIPS,) + shard.shape, shard.dtype)
        return lax.dynamic_update_slice(buf, shard[None], (k_me, 0, 0))

    dw_pad = jnp.pad(conv_dw[0], ((0, CONV_HALO - CONV_K), (0, 0)))
    (GIN,) = allgather_weights([own_block(w_in[0].astype(BF16))], name="gather_w_in")
    ag_bufs = [own_block(sh_a), GIN] + [own_block(s) for s in (xa_wkv[0].astype(BF16), w_ssm_glu[0].astype(BF16), dw_pad)]
    ag_pieces = [(4, 0, CONV_HALO), (0, GA_CONV_OUT, 256), (3, 0, D_SSM), (0, GA_MIX_OUT, 256), (0, GA_WQ, 256),
                 (2, 0, D), (0, GA_WO, 256), (0, GA_UP, D), (0, GA_DOWN, D)]
    ag_send, ag_recv, GA, GIN, GKV, GGLU, GDW = gather_start(ag_bufs, ag_pieces, name="gather_start")

    def w_rowshard(row0):
        return dict(b_spec=((N_CHIPS, 256, D), lambda i, j, k: (0, row0 // 256, 0)), b_view=(D, D), tn=D, tk=D)

    _, h0, h0b = ln_fwd(xs, in_norm_g, in_norm_b, name="ln0_fwd")
    p = mm_nn(h0b, GIN, ((None, D, 1152), lambda i, j, k: (j, 0, 0)), D_IN, tn=1152, tk=D, name="mm_w_in")[0]
    GA, GGLU, GDW = gather_wait(
        ag_send, ag_recv, [GA, GGLU, GDW],
        [(2, 0, CONV_HALO, 0), (0, GA_CONV_OUT, 256, 1), (1, 0, D_SSM, 2), (0, GA_MIX_OUT, 256, 3)], p, name="gather_wait_mixer")
    dw_taps = jnp.transpose(GDW, (1, 0, 2)).reshape(CONV_HALO, D)
    c_pre, actb = conv_fwd(p, dw_taps, conv_db, conv_norm_g[0].reshape(1, D), conv_norm_b[0].reshape(1, D), name="conv_fwd")
    ya = mm_nn(actb, GA, N=D, name="mm_conv_out", **w_rowshard(GA_CONV_OUT))[0]

    lstep, lre, lim = ssm_log_step[0], ssm_lambda_re[0], ssm_lambda_im[0]
    bre, bim, cre, cim = ssm_b_re[0], ssm_b_im[0], ssm_c_re[0], ssm_c_im[0]
    (ar, ai, bbr, bbi), disc_vjp = jax.vjp(_ssm_discretise, lstep, lre, lim, bre, bim)
    Br, Bi = _blockdiag_in(bbr), _blockdiag_in(bbi)
    Cr, Ci = _blockdiag_out(cre), _blockdiag_out(cim)
    pw_r, pw_i, pwrev_r, pwrev_i = _powers(ar.reshape(-1), ai.reshape(-1))
    dvec = ssm_d[0].reshape(1, D_SSM)
    xr, xi, yssm = ssm_fwd(p, Br, Bi, Cr, Ci, pw_r, pw_i, dvec, name="ssm_fwd")
    z = mm_nn(yssm, GGLU, ((None, D_SSM, 512), lambda i, j, k: (j, 0, 0)), 2 * D, tn=512, tk=D_SSM, name="mm_ssm_glu")[0]
    mergedb = merge_fwd(p, ya, z, name="merge_fwd")
    tm_ln = _pick(T, [512, 256, 128])
    row_spec = ((1, D), lambda i, j, k: (0, 0))

    def ln_epilogue(acc, res, g, b):
        r = ALPHA * res + acc
        xhat, _ = _ln_stats(r)
        h = xhat * g + b
        return r, h, h

    def mm_ln(a, row0, res, g, b, name):
        return mm_nn(a, GA, N=D, tm=tm_ln, extras=(res, g.reshape(1, D), b.reshape(1, D)),
                     extra_specs=[_mn(tm_ln, D), row_spec, row_spec], epilogue=ln_epilogue, out_dtypes=[F32, F32, BF16],
                     name=name, **w_rowshard(row0))

    r1, h1, h1b = mm_ln(mergedb, GA_MIX_OUT, h0, ln1_g[0], ln1_b[0], "mm_mix_out_ln1")
    GA, GKV = gather_wait(ag_send, ag_recv, [GA, GKV], [(0, GA_WQ, 256, 4), (1, 0, D, 5), (0, GA_WO, 256, 6)], r1,
                          name="gather_wait_attn")

    qb = mm_nn(h1b, GA, N=D, out_dtype=BF16, name="mm_wq", **w_rowshard(GA_WQ))[0]
    kv = mm_nn(mems, GKV, ((None, D, 512), lambda i, j, k: (j, 0, 0)), 2 * D, tn=512, tk=D, name="mm_wkv")[0]
    ob = attn_fwd(qb, kv, name="attn_fwd")
    r2, h2, h2b = mm_ln(ob, GA_WO, h1, ln2_g[0], ln2_b[0], "mm_wo_ln2")
    (GA,) = gather_wait(ag_send, ag_recv, [GA], [(0, GA_UP, D, 7), (0, GA_DOWN, D, 8)], r2, name="gather_wait_mlp")

    def relu2(acc):
        zr = jnp.maximum(acc, 0.0)
        return (zr * zr,)

    zzb = mm_nn(h2b, GA, ((None, D, D), lambda i, j, k: (j, GA_UP // D, 0)), D_FF, tn=D, tk=D,
                out_dtype=BF16, epilogue=relu2, name="mm_up")[0]
    ff = mm_nn(zzb, GA, ((N_CHIPS, D, D), lambda i, j, k: (0, GA_DOWN // D, 0)), D, tm=_pick(T, [512, 256, 128]), tn=D, tk=D_FF,
               b_view=(D_FF, D), name="mm_down")[0]
    dr3, dr3b, dg3, db3, sq = ln_loss_bwd(ff, h2, ln3_g[0], ln3_b[0], tgt, name="ln3_loss_bwd")

    def rs_begin(grads, rnd):
        return scatter_start(grads, name=f"rs{rnd}_scatter_start")

    g1_shape = jax.ShapeDtypeStruct((N_CHIPS, G1_ROWS, D), BF16)
    g2_shape = jax.ShapeDtypeStruct((N_CHIPS, G2_ROWS, D), BF16)
    dzpreb = mm_nt(dr3b, GA, ((None, D, D), lambda i, j, k: (j, GA_DOWN // D, 0)), D_FF, tn=D, tk=D, out_dtype=BF16,
                   extras=(zzb,), epilogue=lambda acc, zz: (acc * (2.0 * jnp.sqrt(zz.astype(F32))),), name="mm_down_t")[0]
    G1g = mm_tn(zzb, dr3b, tm=D, tn=D, tk=T, out_spec=((None, D, D), lambda i, j, k: (i, G1_DOWN // D, 0)),
                out_shape=g1_shape, name="mm_down_g")
    G1g = mm_tn(h2b, dzpreb, tm=D, tn=D, tk=T, out_spec=((None, D, D), lambda i, j, k: (j, G1_UP // D, 0)),
                out_shape=g1_shape, out_buf=G1g, name="mm_up_g")
    round1 = rs_begin([G1g], 1)
    dh2 = mm_nt(dzpreb, GA, ((N_CHIPS, D, D), lambda i, j, k: (0, GA_UP // D, 0)), D, tm=_pick(T, [512, 256, 128]), tn=D,
                tk=D_FF, b_chunks=N_CHIPS, extras=(dr3,), epilogue=lambda acc, d: (acc + ALPHA * d,),
                after=(round1[2][0],), name="mm_up_t")[0]
    dr2, dr2b, dg2, db2 = ln_bwd(r2, dh2, ln2_g[0], name="ln2_bwd")

    def g_rowshard(row0, out_buf):
        return dict(tm=D, tn=D, out_spec=((N_CHIPS, 256, D), lambda i, j, k: (0, row0 // 256, 0)), out_shape=g2_shape,
                    out_buf=out_buf)

    dob = mm_nt(dr2b, GA, N=D, out_dtype=BF16, name="mm_wo_t", **w_rowshard(GA_WO))[0]
    G2g = mm_tn(ob, dr2b, name="mm_wo_g", **g_rowshard(G2_WO, None))
    dqb, dkv = attn_bwd(qb, kv, dob, name="attn_bwd")
    G2g = mm_tn(h1b, dqb, name="mm_wq_g", **g_rowshard(G2_WQ, G2g))
    GKVg = mm_tn(mems, dkv, tm=D, tn=512, tk=NM, out_spec=((None, D, 512), lambda i, j, k: (j, 0, 0)),
                 out_shape=jax.ShapeDtypeStruct((N_CHIPS, D, 512), BF16), name="mm_wkv_g")
    dh1 = mm_nt(dqb, GA, N=D, extras=(dr2,), epilogue=lambda acc, d: (acc + ALPHA * d,), name="mm_wq_t",
                **w_rowshard(GA_WQ))[0]
    dr1, dr1b, dg1, db1 = ln_bwd(r1, dh1, ln1_g[0], name="ln1_bwd")

    dmerged = mm_nt(dr1b, GA, N=D, name="mm_mix_t", **w_rowshard(GA_MIX_OUT))[0]
    G2g = mm_tn(mergedb, dr1b, name="mm_mix_g", **g_rowshard(G2_MIX_OUT, G2g))
    dyab, dgatesb, dzb = merge_bwd(dmerged, p, ya, z, name="merge_bwd")
    GGLUg = mm_tn(yssm, dzb, tm=D_SSM, tn=512, out_spec=((None, D_SSM, 512), lambda i, j, k: (j, 0, 0)),
                  out_shape=jax.ShapeDtypeStruct((N_CHIPS, D_SSM, 512), BF16), name="mm_glu_g")
    dyssm = mm_nt(dzb, GGLU, ((N_CHIPS, D_SSM, 512), lambda i, j, k: (0, 0, 0)), D_SSM, tn=D_SSM, tk=2 * D, b_chunks=N_CHIPS,
                  name="mm_glu_t")[0]
    dub, dBr, dBi, dCr, dCi, dar8, dai8, dd8 = ssm_bwd(dyssm, p, xr, xi, Br, Bi, Cr, Ci, pw_r, -pw_i, pwrev_r, -pwrev_i, dvec,
                                                       name="ssm_bwd")
    dar = jnp.sum(dar8, axis=0).reshape(SSM_GROUPS, SSM_STATE)
    dai = jnp.sum(dai8, axis=0).reshape(SSM_GROUPS, SSM_STATE)
    g_lstep, g_lre, g_lim, g_bre, g_bim = disc_vjp((dar, dai, _diag_in(dBr), _diag_in(dBi)))
    g_cre, g_cim = _diag_out(dCr), _diag_out(dCi)
    g_d = jnp.sum(dd8, axis=0).reshape(1, D_SSM)

    dact = mm_nt(dyab, GA, N=D, name="mm_conv_out_t", **w_rowshard(GA_CONV_OUT))[0]
    G2g = mm_tn(actb, dyab, name="mm_conv_out_g", **g_rowshard(G2_CONV_OUT, G2g))
    round2 = rs_begin([G2g, GKVg, GGLUg], 2)
    dc, dng, dnb, ddb = conv_bwd_norm(dact, c_pre, conv_norm_g[0].reshape(1, D), conv_norm_b[0].reshape(1, D),
                                      round2[2][0], name="conv_bwd_norm")
    dpb, ddw = conv_bwd_taps(dc, p, dw_taps, dub, dgatesb, name="conv_bwd_taps")
    GINg = mm_tn(h0b, dpb, tm=D, tn=1152, tk=T, out_spec=((None, D, 1152), lambda i, j, k: (j, 0, 0)),
                 out_shape=jax.ShapeDtypeStruct((N_CHIPS, D, 1152), BF16), name="mm_w_in_g")
    round3 = rs_begin([GINg], 3)
    dh0 = mm_nt(dpb, GIN, ((N_CHIPS, D, 1152), lambda i, j, k: (0, 0, 0)), D, tm=_pick(T, [512, 256, 128]), tn=D, tk=D_IN,
                b_chunks=N_CHIPS, extras=(dr1,), epilogue=lambda acc, d: (acc + ALPHA * d,), after=(round3[2][0],),
                name="mm_w_in_t")[0]
    gx, _, dg0, db0 = ln_bwd(xs, dh0, in_norm_g, name="ln0_bwd")

    kc_arr = jnp.concatenate([k_arr, c_arr])
    landed = scatter_wait([round1, round2, round3], gx, name="rs_scatter_wait")
    tags = ["mlp", "sq", "kv", "glu", "in"]
    pairs = [(pt, l2) for parts, lands2 in landed for pt, l2 in zip(parts, lands2)]
    halves = [add_partials(pt, l2, kc_arr, name="rs_add_partials_" + t) for (pt, l2), t in zip(pairs, tags)]
    g1, g2, gKV, gGLU, gIN = join_halves(halves, name="rs_join_halves")

    small_names = ["in_norm_g", "in_norm_b", "conv_db", "conv_norm_g", "conv_norm_b", "ssm_log_step", "ssm_lambda_re",
                   "ssm_lambda_im", "ssm_b_re", "ssm_b_im", "ssm_c_re", "ssm_c_im", "ssm_d", "ln1_g", "ln1_b",
                   "ln2_g", "ln2_b", "ln3_g", "ln3_b"]
    small_w = [in_norm_g, in_norm_b, conv_db, conv_norm_g, conv_norm_b, ssm_log_step, ssm_lambda_re, ssm_lambda_im,
               ssm_b_re, ssm_b_im, ssm_c_re, ssm_c_im, ssm_d, ln1_g, ln1_b, ln2_g, ln2_b, ln3_g, ln3_b]
    small_m = [m_in_norm_g, m_in_norm_b, m_conv_db, m_conv_norm_g, m_conv_norm_b, m_ssm_log_step, m_ssm_lambda_re,
               m_ssm_lambda_im, m_ssm_b_re, m_ssm_b_im, m_ssm_c_re, m_ssm_c_im, m_ssm_d, m_ln1_g, m_ln1_b, m_ln2_g,
               m_ln2_b, m_ln3_g, m_ln3_b]
    small_v = [v_in_norm_g, v_in_norm_b, v_conv_db, v_conv_norm_g, v_conv_norm_b, v_ssm_log_step, v_ssm_lambda_re,
               v_ssm_lambda_im, v_ssm_b_re, v_ssm_b_im, v_ssm_c_re, v_ssm_c_im, v_ssm_d, v_ln1_g, v_ln1_b, v_ln2_g,
               v_ln2_b, v_ln3_g, v_ln3_b]
    small_g = [dg0, db0, ddb, dng, dnb, g_lstep, g_lre, g_lim, g_bre, g_bim, g_cre, g_cim, g_d, dg1, db1, dg2, db2, dg3, db3]
    small_shapes = [w.shape for w in small_w]
    n_small_rows = _pack(small_w).shape[0]
    packed_g = _pack(small_g + [ddw, sq])
    summed = allreduce_two_level(packed_g, name="allreduce_small")
    small_rows = sum(_pack_rows(s) for s in small_shapes)
    dw_rows = _pack_rows((CONV_HALO, D))
    loss = 0.5 * summed[small_rows + dw_rows, 0] / D
    ddw_full = summed[small_rows:small_rows + dw_rows].reshape(CONV_HALO, D)
    g_dw = lax.dynamic_slice_in_dim(ddw_full, k_me * (D // N_CHIPS), D // N_CHIPS, axis=1)
    gs_packed = jnp.pad(summed[:small_rows], ((0, n_small_rows - small_rows), (0, 0)))

    res = {}

    def upd(nm, w, m, v, g_arr, row0=0):
        shp = w.shape
        w2, m2, v2 = (a.reshape(-1, shp[-1]) for a in (w, m, v))
        outs = adamw(w2, m2, v2, g_arr, row0, name="adamw_" + nm)
        res[nm] = tuple(o.reshape(shp) for o in outs)

    upd("w_conv_out", w_conv_out, m_w_conv_out, v_w_conv_out, g2, G2_CONV_OUT)
    upd("w_mix_out", w_mix_out, m_w_mix_out, v_w_mix_out, g2, G2_MIX_OUT)
    upd("xa_wq", xa_wq, m_xa_wq, v_xa_wq, g2, G2_WQ)
    upd("xa_wo", xa_wo, m_xa_wo, v_xa_wo, g2, G2_WO)
    upd("mlp_w_down", mlp_w_down, m_mlp_w_down, v_mlp_w_down, g1, G1_DOWN)
    upd("mlp_w_up", mlp_w_up, m_mlp_w_up, v_mlp_w_up, g1, G1_UP)
    upd("w_in", w_in, m_w_in, v_w_in, gIN)
    upd("xa_wkv", xa_wkv, m_xa_wkv, v_xa_wkv, gKV)
    upd("w_ssm_glu", w_ssm_glu, m_w_ssm_glu, v_w_ssm_glu, gGLU)
    pad_dw = lambda a: jnp.pad(a[0], ((0, CONV_HALO - CONV_K), (0, 0)))
    dw_outs = adamw(pad_dw(conv_dw), pad_dw(m_conv_dw), pad_dw(v_conv_dw), g_dw, 0, name="adamw_conv_dw")
    res["conv_dw"] = tuple(o[:CONV_K][None] for o in dw_outs)
    sm_outs = adamw(_pack(small_w), _pack(small_m), _pack(small_v), gs_packed, 0, name="adamw_small")
    sm_un = [_unpack(o, small_shapes) for o in sm_outs]
    for idx, nm in enumerate(small_names):
        res[nm] = tuple(sm_un[q][idx] for q in range(4))

    order = ["in_norm_g", "in_norm_b", "w_in", "conv_dw", "conv_db", "conv_norm_g", "conv_norm_b", "w_conv_out",
             "ssm_log_step", "ssm_lambda_re", "ssm_lambda_im", "ssm_b_re", "ssm_b_im", "ssm_c_re", "ssm_c_im", "ssm_d",
             "w_ssm_glu", "w_mix_out", "ln1_g", "ln1_b", "xa_wq", "xa_wkv", "xa_wo", "ln2_g", "ln2_b", "mlp_w_up",
             "mlp_w_down", "ln3_g", "ln3_b"]
    return (loss, gx[None], *[res[n][0] for n in order], *[res[n][1] for n in order],
            *[res[n][2] for n in order], *[res[n][3] for n in order])
```

```python
import functools
import math

import jax
import jax.numpy as jnp
from jax import lax
from jax.experimental import pallas as pl
from jax.experimental.pallas import tpu as pltpu

F32 = jnp.float32
BF16 = jnp.bfloat16
MESH = pl.DeviceIdType.MESH

D_MODEL = 1024
N_HEADS = 4
HEAD_DIM = D_MODEL // N_HEADS
CONV_K = 31
CONV_HALO = 32
D_SSM = 512
SSM_GROUPS = 32
SSM_GROUP = 16
SSM_STATE = 64
SSM_BLOCKS = 4
SSM_BLOCK_IN = D_SSM // SSM_BLOCKS
SSM_BLOCK_STATE = SSM_GROUPS * SSM_STATE // SSM_BLOCKS
D_FF = 4096
D_IN = 4608
LN_EPS = 1e-5
ALPHA = (2.0 * 1) ** 0.25
N_CHIPS = 4
N_DEV = 8
ADAM_LR, ADAM_B1, ADAM_B2, ADAM_EPS, ADAM_WD, ADAM_STEP = 0.001, 0.9, 0.999, 1e-08, 0.01, 10
VMEM_LIMIT_BYTES = 56 * 1024 * 1024


def _pick(dim, cands):
    for c in cands:
        if dim % c == 0:
            return c
    return dim


def _cparams(sem=None):
    return pltpu.CompilerParams(dimension_semantics=sem, vmem_limit_bytes=VMEM_LIMIT_BYTES)


def _sigmoid(x):
    return 1.0 / (1.0 + jnp.exp(-x))


_DIMS = {"nn": (((1,), (0,)), ((), ())), "nt": (((1,), (1,)), ((), ())), "tn": (((0,), (0,)), ((), ()))}


def matmul(a, b, *, mode, M, N, K, tm, tn, tk, a_spec, b_spec, out_specs, out_shapes, name,
           extras=(), extra_specs=(), epilogue=None, alias_buf=None, b_view=None, after=(), b_chunks=None):
    nk = K // tk
    ne = len(extras)
    no = len(out_shapes)
    na = (0 if alias_buf is None else 1) + len(after)
    dims = _DIMS[mode]

    def body(*refs):
        a_ref, b_ref = refs[0], refs[1]
        e_refs = refs[2:2 + ne]
        o_refs = refs[2 + ne + na:2 + ne + na + no]

        def finish(acc):
            outs = (acc,) if epilogue is None else epilogue(acc, *[r[...] for r in e_refs])
            for o, r in zip(outs, o_refs):
                r[...] = o.astype(r.dtype).reshape(r.shape)

        if b_chunks:
            kc = a_ref.shape[1] // b_chunks
            prod = None
            for q in range(b_chunks):
                part = lax.dot_general(a_ref[:, q * kc:(q + 1) * kc].astype(BF16), b_ref[q].astype(BF16), dims,
                                       preferred_element_type=F32)
                prod = part if prod is None else prod + part
        else:
            b_blk = b_ref[...] if b_view is None else b_ref[...].reshape(b_view)
            prod = lax.dot_general(a_ref[...].astype(BF16), b_blk.astype(BF16), dims, preferred_element_type=F32)
        if nk == 1:
            finish(prod)
        else:
            acc_ref = refs[-1]
            k = pl.program_id(2)

            @pl.when(k == 0)
            def _():
                acc_ref[...] = prod

            @pl.when(k > 0)
            def _():
                acc_ref[...] += prod

            @pl.when(k == nk - 1)
            def _():
                finish(acc_ref[...])

    in_specs = [pl.BlockSpec(*a_spec), pl.BlockSpec(*b_spec)] + [pl.BlockSpec(*s) for s in extra_specs]
    ins = [a, b, *extras]
    if alias_buf is not None:
        in_specs.append(pl.BlockSpec(memory_space=pl.ANY))
        ins.append(alias_buf)
    for dep in after:
        in_specs.append(pl.BlockSpec(memory_space=pl.ANY))
        ins.append(dep)
    res = pl.pallas_call(
        body,
        grid=(M // tm, N // tn, nk),
        in_specs=in_specs,
        out_specs=[pl.BlockSpec(*s) for s in out_specs],
        out_shape=out_shapes,
        scratch_shapes=[] if nk == 1 else [pltpu.VMEM((tm, tn), F32)],
        input_output_aliases={2 + ne: 0} if alias_buf is not None else {},
        compiler_params=_cparams(("parallel", "parallel", "arbitrary")),
        name=name,
    )(*ins)
    return res


def _mn(tm, tn):
    return ((tm, tn), lambda i, j, k: (i, j))


def mm_nn(a, b_arr, b_spec, N, *, name, tm=None, tn, tk, out_dtype=F32, extras=(), epilogue=None, out_dtypes=None,
          b_view=None, extra_specs=None):
    M, K = a.shape
    tm = tm or _pick(M, [1024, 512, 256, 128])
    dts = out_dtypes or [out_dtype]
    return matmul(a, b_arr, mode="nn", M=M, N=N, K=K, tm=tm, tn=tn, tk=tk,
                  a_spec=((tm, tk), lambda i, j, k: (i, k)), b_spec=b_spec, b_view=b_view,
                  out_specs=[_mn(tm, tn)] * len(dts), out_shapes=[jax.ShapeDtypeStruct((M, N), d) for d in dts],
                  extras=extras, extra_specs=extra_specs or [_mn(tm, tn)] * len(extras), epilogue=epilogue, name=name)


def mm_nt(a, b_arr, b_spec, N, *, name, tm=None, tn, tk, out_dtype=F32, extras=(), epilogue=None, out_dtypes=None,
          b_view=None, after=(), b_chunks=None):
    M, K = a.shape
    tm = tm or _pick(M, [1024, 512, 256, 128])
    dts = out_dtypes or [out_dtype]
    return matmul(a, b_arr, mode="nt", M=M, N=N, K=K, tm=tm, tn=tn, tk=tk, after=after, b_chunks=b_chunks,
                  a_spec=((tm, tk), lambda i, j, k: (i, k)), b_spec=b_spec, b_view=b_view,
                  out_specs=[_mn(tm, tn)] * len(dts), out_shapes=[jax.ShapeDtypeStruct((M, N), d) for d in dts],
                  extras=extras, extra_specs=[_mn(tm, tn)] * len(extras), epilogue=epilogue, name=name)


def mm_tn(a, b, *, name, tm, tn, tk=None, out_spec, out_shape, out_buf=None):
    K, M = a.shape
    N = b.shape[1]
    tk = tk or _pick(K, [2048, 1024, 512, 256, 128])
    return matmul(a, b, mode="tn", M=M, N=N, K=K, tm=tm, tn=tn, tk=tk,
                  a_spec=((tk, tm), lambda i, j, k: (k, i)), b_spec=((tk, tn), lambda i, j, k: (k, j)),
                  out_specs=[out_spec], out_shapes=[out_shape], alias_buf=out_buf, name=name)[0]


def _rows(tc, w, cb=0):
    return pl.BlockSpec((tc, w), lambda i: (i, cb))


def _const(shape):
    return pl.BlockSpec(shape, lambda i: tuple([0] * len(shape)))


def _ln_stats(r):
    mu = jnp.mean(r, axis=-1, keepdims=True)
    xc = r - mu
    var = jnp.mean(xc * xc, axis=-1, keepdims=True)
    rstd = lax.rsqrt(var + LN_EPS)
    return xc * rstd, rstd


def _rowsum8(v):
    tc, w = v.shape
    return jnp.sum(v.reshape(tc // 8, 8, w), axis=0)


def ln_fwd(x, g, b, *, name, res=None):
    T, D = x.shape
    tc = _pick(T, [512, 256, 128])
    has_res = res is not None

    def body(*refs):
        if has_res:
            x_ref, res_ref, g_ref, b_ref, r_ref, h_ref, hb_ref = refs
            r = ALPHA * res_ref[...] + x_ref[...]
            r_ref[...] = r
        else:
            x_ref, g_ref, b_ref, h_ref, hb_ref = refs
            r = x_ref[...]
        xhat, _ = _ln_stats(r)
        y = xhat * g_ref[...] + b_ref[...]
        h_ref[...] = y
        hb_ref[...] = y.astype(BF16)

    ins = [x] + ([res] if has_res else []) + [g.reshape(1, D), b.reshape(1, D)]
    in_specs = [_rows(tc, D)] * (2 if has_res else 1) + [_const((1, D))] * 2
    n_out = 3 if has_res else 2
    outs = pl.pallas_call(
        body, grid=(T // tc,), in_specs=in_specs, out_specs=[_rows(tc, D)] * n_out,
        out_shape=[jax.ShapeDtypeStruct((T, D), F32)] * (n_out - 1) + [jax.ShapeDtypeStruct((T, D), BF16)],
        compiler_params=_cparams(("arbitrary",)), name=name)(*ins)
    if has_res:
        return outs
    return (x,) + tuple(outs)


def ln_bwd(r, dy, g, *, name):
    T, D = r.shape
    tc = _pick(T, [512, 256, 128])
    nt = T // tc

    def body(r_ref, dy_ref, g_ref, dr_ref, drb_ref, dg_ref, db_ref, accg, accb):
        i = pl.program_id(0)

        @pl.when(i == 0)
        def _():
            accg[...] = jnp.zeros_like(accg)
            accb[...] = jnp.zeros_like(accb)

        xhat, rstd = _ln_stats(r_ref[...])
        dy = dy_ref[...]
        dxh = dy * g_ref[...]
        m1 = jnp.mean(dxh, axis=-1, keepdims=True)
        m2 = jnp.mean(dxh * xhat, axis=-1, keepdims=True)
        dr = rstd * (dxh - m1 - xhat * m2)
        dr_ref[...] = dr
        drb_ref[...] = dr.astype(BF16)
        accg[...] += _rowsum8(dy * xhat)
        accb[...] += _rowsum8(dy)

        @pl.when(i == nt - 1)
        def _():
            dg_ref[...] = jnp.sum(accg[...], axis=0, keepdims=True)
            db_ref[...] = jnp.sum(accb[...], axis=0, keepdims=True)

    return pl.pallas_call(
        body, grid=(nt,), in_specs=[_rows(tc, D), _rows(tc, D), _const((1, D))],
        out_specs=[_rows(tc, D), _rows(tc, D), _const((1, D)), _const((1, D))],
        out_shape=[jax.ShapeDtypeStruct((T, D), F32), jax.ShapeDtypeStruct((T, D), BF16),
                   jax.ShapeDtypeStruct((1, D), F32), jax.ShapeDtypeStruct((1, D), F32)],
        scratch_shapes=[pltpu.VMEM((8, D), F32), pltpu.VMEM((8, D), F32)],
        compiler_params=_cparams(("arbitrary",)), name=name)(r, dy, g.reshape(1, D))


def ln_loss_bwd(x, res, g, b, target, *, name):
    T, D = x.shape
    tc = _pick(T, [512, 256, 128])
    nt = T // tc

    def body(x_ref, res_ref, g_ref, b_ref, t_ref, dr_ref, drb_ref, dg_ref, db_ref, loss_ref, accg, accb, accl):
        i = pl.program_id(0)

        @pl.when(i == 0)
        def _():
            accg[...] = jnp.zeros_like(accg)
            accb[...] = jnp.zeros_like(accb)
            accl[...] = jnp.zeros_like(accl)

        r = ALPHA * res_ref[...] + x_ref[...]
        xhat, rstd = _ln_stats(r)
        e = xhat * g_ref[...] + b_ref[...] - t_ref[...]
        dy = e * (1.0 / D)
        dxh = dy * g_ref[...]
        m1 = jnp.mean(dxh, axis=-1, keepdims=True)
        m2 = jnp.mean(dxh * xhat, axis=-1, keepdims=True)
        dr = rstd * (dxh - m1 - xhat * m2)
        dr_ref[...] = dr
        drb_ref[...] = dr.astype(BF16)
        accg[...] += _rowsum8(dy * xhat)
        accb[...] += _rowsum8(dy)
        accl[...] += _rowsum8(e * e)

        @pl.when(i == nt - 1)
        def _():
            dg_ref[...] = jnp.sum(accg[...], axis=0, keepdims=True)
            db_ref[...] = jnp.sum(accb[...], axis=0, keepdims=True)
            s = jnp.sum(jnp.sum(accl[...], axis=0, keepdims=True), axis=1, keepdims=True)
            loss_ref[...] = jnp.broadcast_to(s, (1, 128))

    return pl.pallas_call(
        body, grid=(nt,), in_specs=[_rows(tc, D), _rows(tc, D), _const((1, D)), _const((1, D)), _rows(tc, D)],
        out_specs=[_rows(tc, D), _rows(tc, D), _const((1, D)), _const((1, D)), _const((1, 128))],
        out_shape=[jax.ShapeDtypeStruct((T, D), F32), jax.ShapeDtypeStruct((T, D), BF16),
                   jax.ShapeDtypeStruct((1, D), F32), jax.ShapeDtypeStruct((1, D), F32), jax.ShapeDtypeStruct((1, 128), F32)],
        scratch_shapes=[pltpu.VMEM((8, D), F32)] * 3,
        compiler_params=_cparams(("arbitrary",)), name=name)(x, res, g.reshape(1, D), b.reshape(1, D), target)


def _halo_prev(tc):
    per = tc // CONV_HALO
    return lambda i: jnp.maximum(i * per - 1, 0)


CONV_ROWS = 32
CONV_TAP_GROUP = 4
CONV_TILE_UNROLL = 4


def _fill_shifts(S, nrows):
    for b in range(1, 8):
        S[b, 0:nrows - 8, :] = S[0, b:b + nrows - 8, :]


def _tap_sum(S, w_ref, offs, r0, nrows):
    acc = None
    for k, o in enumerate(offs):
        a, b = divmod(o, 8)
        term = w_ref[k:k + 1, :] * S[b, pl.ds(pl.multiple_of(r0 + 8 * a, 8), nrows), :]
        acc = term if acc is None else acc + term
    return acc


def conv_fwd(p, dw, db, ng, nb, *, name):
    T = p.shape[0]
    D = D_MODEL
    tc = _pick(T, [256, 128])
    prev = _halo_prev(tc)
    off = CONV_HALO - (CONV_K - 1)
    offs = [off + k for k in range(CONV_K)]

    def body(val_ref, gate_ref, valp_ref, gatep_ref, dw_ref, db_ref, ng_ref, nb_ref, c_ref, act_ref, S):
        i = pl.program_id(0)
        u_prev = valp_ref[...] * _sigmoid(gatep_ref[...])
        S[0, 0:CONV_HALO, :] = jnp.where(i > 0, u_prev, 0.0)
        S[0, CONV_HALO:CONV_HALO + tc, :] = val_ref[...] * _sigmoid(gate_ref[...])
        _fill_shifts(S, CONV_HALO + tc)

        def rows(j, carry):
            r0 = pl.multiple_of(j * CONV_ROWS, CONV_ROWS)
            c_ref[pl.ds(r0, CONV_ROWS), :] = _tap_sum(S, dw_ref, offs, r0, CONV_ROWS) + db_ref[...]
            return carry

        lax.fori_loop(0, tc // CONV_ROWS, rows, 0)
        c = c_ref[...]
        xhat, _ = _ln_stats(c)
        cn = xhat * ng_ref[...] + nb_ref[...]
        act_ref[...] = (cn * _sigmoid(cn)).astype(BF16)

    return pl.pallas_call(
        body, grid=(T // tc,),
        in_specs=[_rows(tc, D, 0), _rows(tc, D, 1),
                  pl.BlockSpec((CONV_HALO, D), lambda i: (prev(i), 0)), pl.BlockSpec((CONV_HALO, D), lambda i: (prev(i), 1)),
                  _const((CONV_HALO, D)), _const((1, D)), _const((1, D)), _const((1, D))],
        out_specs=[_rows(tc, D), _rows(tc, D)],
        out_shape=[jax.ShapeDtypeStruct((T, D), F32), jax.ShapeDtypeStruct((T, D), BF16)],
        scratch_shapes=[pltpu.VMEM((8, CONV_HALO + tc, D), F32)],
        compiler_params=_cparams(("arbitrary",)), name=name)(p, p, p, p, dw, db, ng, nb)


def conv_bwd_norm(dact, c_pre, ng, nb, after, *, name):
    T, D = c_pre.shape
    tc = _pick(T, [512, 256, 128])
    nt = T // tc

    def body(da_ref, c_ref, ng_ref, nb_ref, after_ref, dc_ref, dng_ref, dnb_ref, ddb_ref, accg, accb, accd):
        i = pl.program_id(0)

        @pl.when(i == 0)
        def _():
            accg[...] = jnp.zeros_like(accg)
            accb[...] = jnp.zeros_like(accb)
            accd[...] = jnp.zeros_like(accd)

        xhat, rstd = _ln_stats(c_ref[...])
        cn = xhat * ng_ref[...] + nb_ref[...]
        s = _sigmoid(cn)
        dcn = da_ref[...] * (s * (1.0 + cn * (1.0 - s)))
        dxh = dcn * ng_ref[...]
        m1 = jnp.mean(dxh, axis=-1, keepdims=True)
        m2 = jnp.mean(dxh * xhat, axis=-1, keepdims=True)
        dc = rstd * (dxh - m1 - xhat * m2)
        dc_ref[...] = dc
        accg[...] += _rowsum8(dcn * xhat)
        accb[...] += _rowsum8(dcn)
        accd[...] += _rowsum8(dc)

        @pl.when(i == nt - 1)
        def _():
            dng_ref[...] = jnp.sum(accg[...], axis=0, keepdims=True)
            dnb_ref[...] = jnp.sum(accb[...], axis=0, keepdims=True)
            ddb_ref[...] = jnp.sum(accd[...], axis=0, keepdims=True)

    vec = jax.ShapeDtypeStruct((1, D), F32)
    return pl.pallas_call(
        body, grid=(nt,), in_specs=[_rows(tc, D), _rows(tc, D), _const((1, D)), _const((1, D)), ANY],
        out_specs=[_rows(tc, D), _const((1, D)), _const((1, D)), _const((1, D))],
        out_shape=[jax.ShapeDtypeStruct((T, D), F32), vec, vec, vec],
        scratch_shapes=[pltpu.VMEM((8, D), F32)] * 3,
        compiler_params=_cparams(("arbitrary",)), name=name)(dact, c_pre, ng, nb, after)


def conv_bwd_taps(dc, p, dw, du_ssm, dgates, *, name):
    T, D = dc.shape
    tc = _pick(T, [256, 128])
    nt = T // tc
    per = tc // CONV_HALO
    prev = _halo_prev(tc)
    last_halo = T // CONV_HALO - 1
    nxt = lambda i: jnp.minimum((i + 1) * per, last_halo)
    off = CONV_HALO - (CONV_K - 1)

    def body(dc_ref, dcn_ref, val_ref, gate_ref, valp_ref, gatep_ref, dw_ref, dus_ref, dg_ref, dvg_ref, ddw_ref,
             ext_u, ext_d, acc):
        i = pl.program_id(0)

        @pl.when(i == 0)
        def _():
            acc[...] = jnp.zeros_like(acc)

        dvg_ref[:, 2 * D:2 * D + D_SSM] = dus_ref[...]
        dvg_ref[:, 2 * D + D_SSM:D_IN] = dg_ref[...]

        u_prev = valp_ref[...] * _sigmoid(gatep_ref[...])
        ext_u[0, 0:CONV_HALO, :] = jnp.where(i > 0, u_prev, 0.0)
        ext_u[0, CONV_HALO:CONV_HALO + tc, :] = val_ref[...] * _sigmoid(gate_ref[...])
        ext_d[0, 0:tc, :] = dc_ref[...]
        ext_d[0, tc:tc + CONV_HALO, :] = jnp.where(i < nt - 1, dcn_ref[...], 0.0)
        _fill_shifts(ext_u, CONV_HALO + tc)
        _fill_shifts(ext_d, CONV_HALO + tc)

        def rows(j, carry):
            r0 = pl.multiple_of(j * CONV_ROWS, CONV_ROWS)
            sl = pl.ds(r0, CONV_ROWS)
            du = _tap_sum(ext_d, dw_ref, [CONV_K - 1 - k for k in range(CONV_K)], r0, CONV_ROWS)
            sg = _sigmoid(gate_ref[sl, :])
            dvg_ref[sl, 0:D] = (du * sg).astype(BF16)
            dvg_ref[sl, D:2 * D] = (du * val_ref[sl, :] * sg * (1.0 - sg)).astype(BF16)
            return carry

        lax.fori_loop(0, tc // CONV_ROWS, rows, 0)

        for k0 in range(0, CONV_K, CONV_TAP_GROUP):
            ks = list(range(k0, min(k0 + CONV_TAP_GROUP, CONV_K)))

            def taps(j, accs, ks=ks):
                out = list(accs)
                for t in range(CONV_TILE_UNROLL):
                    r0 = pl.multiple_of((j * CONV_TILE_UNROLL + t) * 8, 8)
                    dct = dc_ref[pl.ds(r0, 8), :]
                    for q, k in enumerate(ks):
                        a, b = divmod(off + k, 8)
                        out[q] = out[q] + dct * ext_u[b, pl.ds(pl.multiple_of(r0 + 8 * a, 8), 8), :]
                return tuple(out)

            accs = lax.fori_loop(0, tc // (8 * CONV_TILE_UNROLL), taps, tuple(jnp.zeros((8, D), F32) for _ in ks))
            for k, a_k in zip(ks, accs):
                acc[k] += a_k

        @pl.when(i == nt - 1)
        def _():
            ddw_ref[...] = jnp.zeros_like(ddw_ref)
            for k in range(CONV_K):
                ddw_ref[k:k + 1, :] = jnp.sum(acc[k], axis=0, keepdims=True)

    return pl.pallas_call(
        body, grid=(nt,),
        in_specs=[_rows(tc, D), pl.BlockSpec((CONV_HALO, D), lambda i: (nxt(i), 0)),
                  _rows(tc, D, 0), _rows(tc, D, 1),
                  pl.BlockSpec((CONV_HALO, D), lambda i: (prev(i), 0)), pl.BlockSpec((CONV_HALO, D), lambda i: (prev(i), 1)),
                  _const((CONV_HALO, D)), _rows(tc, D_SSM), _rows(tc, 2 * D)],
        out_specs=[_rows(tc, D_IN), _const((CONV_HALO, D))],
        out_shape=[jax.ShapeDtypeStruct((T, D_IN), BF16), jax.ShapeDtypeStruct((CONV_HALO, D), F32)],
        scratch_shapes=[pltpu.VMEM((8, CONV_HALO + tc, D), F32), pltpu.VMEM((8, CONV_HALO + tc, D), F32),
                        pltpu.VMEM((CONV_K, 8, D), F32)],
        compiler_params=_cparams(("arbitrary",)), name=name)(dc, dc, p, p, p, p, dw, du_ssm, dgates)


GATE_A0 = (2 * D_MODEL + D_SSM) // 512
GATE_B0 = GATE_A0 + 2


def merge_fwd(p, ya, z, *, name):
    T = p.shape[0]
    D = D_MODEL
    tc = _pick(T, [512, 256, 128])
    W = 512

    def body(ga_ref, gb_ref, ya_ref, z1_ref, z2_ref, o_ref):
        yb = z1_ref[...] * _sigmoid(z2_ref[...])
        o_ref[...] = (_sigmoid(ga_ref[...]) * ya_ref[...] + _sigmoid(gb_ref[...]) * yb).astype(BF16)

    return pl.pallas_call(
        body, grid=(T // tc, D // W),
        in_specs=[pl.BlockSpec((tc, W), lambda i, j: (i, GATE_A0 + j)), pl.BlockSpec((tc, W), lambda i, j: (i, GATE_B0 + j)),
                  pl.BlockSpec((tc, W), lambda i, j: (i, j)), pl.BlockSpec((tc, W), lambda i, j: (i, j)),
                  pl.BlockSpec((tc, W), lambda i, j: (i, D // W + j))],
        out_specs=pl.BlockSpec((tc, W), lambda i, j: (i, j)),
        out_shape=jax.ShapeDtypeStruct((T, D), BF16),
        compiler_params=_cparams(("arbitrary", "arbitrary")), name=name)(p, p, ya, z, z)


def merge_bwd(dm, p, ya, z, *, name):
    T = p.shape[0]
    D = D_MODEL
    tc = _pick(T, [256, 128])
    W = 512
    nb = D // W

    def body(dm_ref, ga0_ref, ga1_ref, gb0_ref, gb1_ref, ya_ref, z_ref, dya_ref, dg_ref, dz_ref):
        for j, (ga_ref, gb_ref) in enumerate(((ga0_ref, gb0_ref), (ga1_ref, gb1_ref))):
            c0 = slice(j * W, (j + 1) * W)
            c1 = slice(D + j * W, D + (j + 1) * W)
            dm = dm_ref[:, c0]
            sa = _sigmoid(ga_ref[...])
            sb = _sigmoid(gb_ref[...])
            s2 = _sigmoid(z_ref[:, c1])
            z1 = z_ref[:, c0]
            yb = z1 * s2
            dya_ref[:, c0] = (dm * sa).astype(BF16)
            dg_ref[:, c0] = (dm * ya_ref[:, c0] * sa * (1.0 - sa)).astype(BF16)
            dg_ref[:, c1] = (dm * yb * sb * (1.0 - sb)).astype(BF16)
            dyb = dm * sb
            dz_ref[:, c0] = (dyb * s2).astype(BF16)
            dz_ref[:, c1] = (dyb * z1 * s2 * (1.0 - s2)).astype(BF16)

    gate = lambda cb: pl.BlockSpec((tc, W), lambda i: (i, cb))
    return pl.pallas_call(
        body, grid=(T // tc,),
        in_specs=[_rows(tc, D), gate(GATE_A0), gate(GATE_A0 + 1), gate(GATE_B0), gate(GATE_B0 + 1), _rows(tc, D),
                  _rows(tc, 2 * D)],
        out_specs=[_rows(tc, D), _rows(tc, 2 * D), _rows(tc, 2 * D)],
        out_shape=[jax.ShapeDtypeStruct((T, D), BF16), jax.ShapeDtypeStruct((T, 2 * D), BF16),
                   jax.ShapeDtypeStruct((T, 2 * D), BF16)],
        compiler_params=_cparams(("arbitrary",)), name=name)(dm, p, p, p, p, ya, z)


def _scan_block(src_r, src_i, dst_r, dst_i, car_r, car_i, pw_r, pw_i, cw_r, cw_i, ntiles, reverse, extra=None):
    W = src_r.shape[1]
    rows = lax.broadcasted_iota(jnp.int32, (8, W), 0)
    steps = []
    for d, pr in ((1, 0), (2, 1), (4, 3)):
        valid = rows < 8 - d if reverse else rows >= d
        steps.append((d, jnp.where(valid, jnp.broadcast_to(pw_r[pr:pr + 1, :], (8, W)), 0.0),
                      jnp.where(valid, jnp.broadcast_to(pw_i[pr:pr + 1, :], (8, W)), 0.0)))
    cw_r, cw_i = cw_r[...], cw_i[...]

    def tile(jj, carry):
        j = ntiles - 1 - jj if reverse else jj
        sl = pl.ds(pl.multiple_of(j * 8, 8), 8)
        xr, xi = src_r[sl, :], src_i[sl, :]
        for d, lr, li in steps:
            sr = pltpu.roll(xr, 8 - d if reverse else d, 0)
            si = pltpu.roll(xi, 8 - d if reverse else d, 0)
            xr, xi = xr + lr * sr - li * si, xi + lr * si + li * sr
        cr, ci = car_r[...], car_i[...]
        xr, xi = xr + cw_r * cr - cw_i * ci, xi + cw_r * ci + cw_i * cr
        dst_r[sl, :] = xr
        dst_i[sl, :] = xi
        edge = 0 if reverse else 7
        car_r[...] = jnp.broadcast_to(xr[edge:edge + 1, :], (8, W))
        car_i[...] = jnp.broadcast_to(xi[edge:edge + 1, :], (8, W))
        if extra is not None:
            carry = extra(j, xr, xi, carry)
        return carry

    return tile


def ssm_fwd(p, Br, Bi, Cr, Ci, pw_r, pw_i, dvec, *, name):
    T = p.shape[0]
    tt = _pick(T, [512, 256, 128])
    nt = T // tt
    WI, WS = SSM_BLOCK_IN, SSM_BLOCK_STATE
    u0 = 2 * D_MODEL // WI

    def body(u_ref, br_ref, bi_ref, cr_ref, ci_ref, pwr_ref, pwi_ref, d_ref, xr_ref, xi_ref, y_ref, bur, bui, car_r, car_i):
        i = pl.program_id(1)

        @pl.when(i == 0)
        def _():
            car_r[...] = jnp.zeros_like(car_r)
            car_i[...] = jnp.zeros_like(car_i)

        u = u_ref[...]
        ub = u.astype(BF16)
        bur[...] = jnp.dot(ub, br_ref[...].astype(BF16), preferred_element_type=F32)
        bui[...] = jnp.dot(ub, bi_ref[...].astype(BF16), preferred_element_type=F32)
        tile = _scan_block(bur, bui, xr_ref, xi_ref, car_r, car_i, pwr_ref, pwi_ref, pwr_ref, pwi_ref, tt // 8, False)
        lax.fori_loop(0, tt // 8, tile, 0)
        y = (jnp.dot(xr_ref[...].astype(BF16), cr_ref[...].astype(BF16), preferred_element_type=F32)
             - jnp.dot(xi_ref[...].astype(BF16), ci_ref[...].astype(BF16), preferred_element_type=F32)
             + d_ref[...] * u)
        y_ref[...] = y.astype(BF16)

    return pl.pallas_call(
        body, grid=(SSM_BLOCKS, nt),
        in_specs=[pl.BlockSpec((tt, WI), lambda b, i: (i, u0 + b)),
                  pl.BlockSpec((None, WI, WS), lambda b, i: (b, 0, 0)), pl.BlockSpec((None, WI, WS), lambda b, i: (b, 0, 0)),
                  pl.BlockSpec((None, WS, WI), lambda b, i: (b, 0, 0)), pl.BlockSpec((None, WS, WI), lambda b, i: (b, 0, 0)),
                  pl.BlockSpec((8, WS), lambda b, i: (0, b)), pl.BlockSpec((8, WS), lambda b, i: (0, b)),
                  pl.BlockSpec((1, WI), lambda b, i: (0, b))],
        out_specs=[pl.BlockSpec((tt, WS), lambda b, i: (i, b)), pl.BlockSpec((tt, WS), lambda b, i: (i, b)),
                   pl.BlockSpec((tt, WI), lambda b, i: (i, b))],
        out_shape=[jax.ShapeDtypeStruct((T, SSM_BLOCKS * WS), F32)] * 2 + [jax.ShapeDtypeStruct((T, D_SSM), BF16)],
        scratch_shapes=[pltpu.VMEM((tt, WS), F32), pltpu.VMEM((tt, WS), F32), pltpu.VMEM((8, WS), F32), pltpu.VMEM((8, WS), F32)],
        compiler_params=_cparams(("arbitrary", "arbitrary")), name=name)(p, Br, Bi, Cr, Ci, pw_r, pw_i, dvec)


def ssm_bwd(dy, p, xr, xi, Br, Bi, Cr, Ci, pwc_r, pwc_i, cwc_r, cwc_i, dvec, *, name):
    T = p.shape[0]
    tt = _pick(T, [512, 256, 128])
    nt = T // tt
    WI, WS = SSM_BLOCK_IN, SSM_BLOCK_STATE
    u0 = 2 * D_MODEL // WI
    tb = lambda i: nt - 1 - i
    xprev = lambda i: jnp.maximum(tb(i) * (tt // 8) - 1, 0)
    tn_dims = _DIMS["tn"]
    nt_dims = _DIMS["nt"]

    def body(dy_ref, u_ref, xr_ref, xi_ref, xpr_ref, xpi_ref, br_ref, bi_ref, cr_ref, ci_ref, pwr_ref, pwi_ref,
             cwr_ref, cwi_ref, d_ref,
             du_ref, dbr_ref, dbi_ref, dcr_ref, dci_ref, dar_ref, dai_ref, dd_ref,
             gr, gi, ext_r, ext_i, car_r, car_i):
        i = pl.program_id(1)

        @pl.when(i == 0)
        def _():
            car_r[...] = jnp.zeros_like(car_r)
            car_i[...] = jnp.zeros_like(car_i)
            dbr_ref[...] = jnp.zeros_like(dbr_ref)
            dbi_ref[...] = jnp.zeros_like(dbi_ref)
            dcr_ref[...] = jnp.zeros_like(dcr_ref)
            dci_ref[...] = jnp.zeros_like(dci_ref)
            dar_ref[...] = jnp.zeros_like(dar_ref)
            dai_ref[...] = jnp.zeros_like(dai_ref)
            dd_ref[...] = jnp.zeros_like(dd_ref)

        dy = dy_ref[...]
        dyb = dy.astype(BF16)
        u = u_ref[...]
        ub = u.astype(BF16)
        gr[...] = lax.dot_general(dyb, cr_ref[...].astype(BF16), nt_dims, preferred_element_type=F32)
        gi[...] = -lax.dot_general(dyb, ci_ref[...].astype(BF16), nt_dims, preferred_element_type=F32)
        first = tb(i) == 0
        ext_r[0:8, :] = jnp.where(first, 0.0, xpr_ref[...])
        ext_i[0:8, :] = jnp.where(first, 0.0, xpi_ref[...])
        ext_r[8:8 + tt, :] = xr_ref[...]
        ext_i[8:8 + tt, :] = xi_ref[...]
        rows = lax.broadcasted_iota(jnp.int32, (8, WS), 0)

        def lam_grad(j, g_r, g_i, carry):
            a_r, a_i = carry
            cur = pl.ds(pl.multiple_of(j * 8 + 8, 8), 8)
            prv = pl.ds(pl.multiple_of(j * 8, 8), 8)
            xc_r, xc_i = ext_r[cur, :], ext_i[cur, :]
            xl_r, xl_i = ext_r[prv, :], ext_i[prv, :]
            xp_r = jnp.where(rows == 0, jnp.broadcast_to(xl_r[7:8, :], (8, WS)), pltpu.roll(xc_r, 1, 0))
            xp_i = jnp.where(rows == 0, jnp.broadcast_to(xl_i[7:8, :], (8, WS)), pltpu.roll(xc_i, 1, 0))
            return (a_r + g_r * xp_r + g_i * xp_i, a_i + g_i * xp_r - g_r * xp_i)

        tile = _scan_block(gr, gi, gr, gi, car_r, car_i, pwr_ref, pwi_ref, cwr_ref, cwi_ref, tt // 8, True, extra=lam_grad)
        z8 = jnp.zeros((8, WS), F32)
        a_r, a_i = lax.fori_loop(0, tt // 8, tile, (z8, z8))
        dar_ref[...] += a_r
        dai_ref[...] += a_i
        grb = gr[...].astype(BF16)
        gib = gi[...].astype(BF16)
        dbr_ref[...] += lax.dot_general(ub, grb, tn_dims, preferred_element_type=F32)
        dbi_ref[...] += lax.dot_general(ub, gib, tn_dims, preferred_element_type=F32)
        dcr_ref[...] += lax.dot_general(xr_ref[...].astype(BF16), dyb, tn_dims, preferred_element_type=F32)
        dci_ref[...] -= lax.dot_general(xi_ref[...].astype(BF16), dyb, tn_dims, preferred_element_type=F32)
        du = (lax.dot_general(grb, br_ref[...].astype(BF16), nt_dims, preferred_element_type=F32)
              + lax.dot_general(gib, bi_ref[...].astype(BF16), nt_dims, preferred_element_type=F32)
              + d_ref[...] * dy)
        du_ref[...] = du.astype(BF16)
        dd_ref[...] += _rowsum8(dy * u)

    wspec = lambda shp: pl.BlockSpec((None,) + shp, lambda b, i: (b, 0, 0))
    return pl.pallas_call(
        body, grid=(SSM_BLOCKS, nt),
        in_specs=[pl.BlockSpec((tt, WI), lambda b, i: (tb(i), b)),
                  pl.BlockSpec((tt, WI), lambda b, i: (tb(i), u0 + b)),
                  pl.BlockSpec((tt, WS), lambda b, i: (tb(i), b)), pl.BlockSpec((tt, WS), lambda b, i: (tb(i), b)),
                  pl.BlockSpec((8, WS), lambda b, i: (xprev(i), b)), pl.BlockSpec((8, WS), lambda b, i: (xprev(i), b)),
                  wspec((WI, WS)), wspec((WI, WS)), wspec((WS, WI)), wspec((WS, WI)),
                  pl.BlockSpec((8, WS), lambda b, i: (0, b)), pl.BlockSpec((8, WS), lambda b, i: (0, b)),
                  pl.BlockSpec((8, WS), lambda b, i: (0, b)), pl.BlockSpec((8, WS), lambda b, i: (0, b)),
                  pl.BlockSpec((1, WI), lambda b, i: (0, b))],
        out_specs=[pl.BlockSpec((tt, WI), lambda b, i: (tb(i), b)),
                   wspec((WI, WS)), wspec((WI, WS)), wspec((WS, WI)), wspec((WS, WI)),
                   pl.BlockSpec((8, WS), lambda b, i: (0, b)), pl.BlockSpec((8, WS), lambda b, i: (0, b)),
                   pl.BlockSpec((8, WI), lambda b, i: (0, b))],
        out_shape=[jax.ShapeDtypeStruct((T, D_SSM), BF16),
                   jax.ShapeDtypeStruct((SSM_BLOCKS, WI, WS), F32), jax.ShapeDtypeStruct((SSM_BLOCKS, WI, WS), F32),
                   jax.ShapeDtypeStruct((SSM_BLOCKS, WS, WI), F32), jax.ShapeDtypeStruct((SSM_BLOCKS, WS, WI), F32),
                   jax.ShapeDtypeStruct((8, SSM_BLOCKS * WS), F32), jax.ShapeDtypeStruct((8, SSM_BLOCKS * WS), F32),
                   jax.ShapeDtypeStruct((8, D_SSM), F32)],
        scratch_shapes=[pltpu.VMEM((tt, WS), F32), pltpu.VMEM((tt, WS), F32),
                        pltpu.VMEM((tt + 8, WS), F32), pltpu.VMEM((tt + 8, WS), F32),
                        pltpu.VMEM((8, WS), F32), pltpu.VMEM((8, WS), F32)],
        compiler_params=_cparams(("arbitrary", "arbitrary")), name=name,
    )(dy, p, xr, xi, xr, xi, Br, Bi, Cr, Ci, pwc_r, pwc_i, cwc_r, cwc_i, dvec)


SSM_SEGS = 8


def seg_perm(a, tt):
    T, C = a.shape
    return a.reshape(T // tt, SSM_SEGS, tt // SSM_SEGS, C).transpose(0, 2, 1, 3).reshape(T, C)


def seg_unperm(a, tt):
    T, C = a.shape
    return a.reshape(T // tt, tt // SSM_SEGS, SSM_SEGS, C).transpose(0, 2, 1, 3).reshape(T, C)


def _ssm_tt(T):
    return _pick(T, [512, 256, 128])


def _seg_tables(ar_ref, ai_ref, conj, pb_r, pb_i, pw_r, pw_i, cw_r, cw_i, sl):
    W = ar_ref.shape[1]
    lr = jnp.broadcast_to(ar_ref[...], (8, W))
    li = jnp.broadcast_to(ai_ref[...], (8, W))
    if conj:
        li = -li

    def power(j, cur):
        cr, ci = cur
        pb_r[j] = cr
        pb_i[j] = ci
        return cr * lr - ci * li, cr * li + ci * lr

    lax.fori_loop(0, sl, power, (lr, li))
    br, bi = pb_r[sl - 1], pb_i[sl - 1]
    rows = lax.broadcasted_iota(jnp.int32, (8, W), 0)
    cr, ci = br, bi
    tr, ti = jnp.zeros((8, W), F32), jnp.zeros((8, W), F32)
    ur, ui = tr, ti
    for r in range(8):
        tr, ti = jnp.where(rows == r, cr, tr), jnp.where(rows == r, ci, ti)
        ur, ui = jnp.where(rows == 7 - r, cr, ur), jnp.where(rows == 7 - r, ci, ui)
        cr, ci = cr * br - ci * bi, cr * bi + ci * br
    pw_r[...] = tr
    pw_i[...] = ti
    cw_r[...] = ur
    cw_i[...] = ui


def ssm_seg_fwd(u, Br, Bi, Cr, Ci, ar, ai, dvec, *, name):
    T = u.shape[0]
    tt = _ssm_tt(T)
    nt = T // tt
    sl = tt // SSM_SEGS
    WI, WS = SSM_BLOCK_IN, SSM_BLOCK_STATE

    def body(u_ref, br_ref, bi_ref, cr_ref, ci_ref, ar_ref, ai_ref, d_ref, xr_ref, xi_ref, y_ref,
             bur, bui, pb_r, pb_i, pw_r, pw_i, cw_r, cw_i, end_r, end_i, car_r, car_i):
        i = pl.program_id(1)

        @pl.when(i == 0)
        def _():
            _seg_tables(ar_ref, ai_ref, False, pb_r, pb_i, pw_r, pw_i, cw_r, cw_i, sl)
            car_r[...] = jnp.zeros_like(car_r)
            car_i[...] = jnp.zeros_like(car_i)

        u = u_ref[...]
        ub = u.astype(BF16)
        bur[...] = jnp.dot(ub, br_ref[...].astype(BF16), preferred_element_type=F32)
        bui[...] = jnp.dot(ub, bi_ref[...].astype(BF16), preferred_element_type=F32)
        lr = jnp.broadcast_to(ar_ref[...], (8, WS))
        li = jnp.broadcast_to(ai_ref[...], (8, WS))

        def step(j, st):
            sr, si = st
            rw = pl.ds(pl.multiple_of(j * 8, 8), 8)
            nr = lr * sr - li * si + bur[rw, :]
            ni = lr * si + li * sr + bui[rw, :]
            xr_ref[rw, :] = nr
            xi_ref[rw, :] = ni
            return nr, ni

        z8 = jnp.zeros((8, WS), F32)
        end_r[...], end_i[...] = lax.fori_loop(0, sl, step, (z8, z8))
        old_r, old_i = car_r[...], car_i[...]
        _scan_block(end_r, end_i, end_r, end_i, car_r, car_i, pw_r, pw_i, pw_r, pw_i, 1, False)(0, 0)
        rows = lax.broadcasted_iota(jnp.int32, (8, WS), 0)
        s_r = jnp.where(rows == 0, old_r, pltpu.roll(end_r[...], 1, 0))
        s_i = jnp.where(rows == 0, old_i, pltpu.roll(end_i[...], 1, 0))

        def fix(j, c):
            rw = pl.ds(pl.multiple_of(j * 8, 8), 8)
            pr, pi = pb_r[j], pb_i[j]
            xr_ref[rw, :] = xr_ref[rw, :] + pr * s_r - pi * s_i
            xi_ref[rw, :] = xi_ref[rw, :] + pr * s_i + pi * s_r
            return c

        lax.fori_loop(0, sl, fix, 0)
        y = (jnp.dot(xr_ref[...].astype(BF16), cr_ref[...].astype(BF16), preferred_element_type=F32)
             - jnp.dot(xi_ref[...].astype(BF16), ci_ref[...].astype(BF16), preferred_element_type=F32)
             + d_ref[...] * u)
        y_ref[...] = y.astype(BF16)

    wspec = lambda shp: pl.BlockSpec((None,) + shp, lambda b, i: (b, 0, 0))
    vec = lambda w: pl.BlockSpec((1, w), lambda b, i: (0, b))
    tile8 = pltpu.VMEM((8, WS), F32)
    return pl.pallas_call(
        body, grid=(SSM_BLOCKS, nt),
        in_specs=[pl.BlockSpec((tt, WI), lambda b, i: (i, b)), wspec((WI, WS)), wspec((WI, WS)), wspec((WS, WI)),
                  wspec((WS, WI)), vec(WS), vec(WS), vec(WI)],
        out_specs=[pl.BlockSpec((tt, WS), lambda b, i: (i, b)), pl.BlockSpec((tt, WS), lambda b, i: (i, b)),
                   pl.BlockSpec((tt, WI), lambda b, i: (i, b))],
        out_shape=[jax.ShapeDtypeStruct((T, SSM_BLOCKS * WS), F32)] * 2 + [jax.ShapeDtypeStruct((T, D_SSM), BF16)],
        scratch_shapes=[pltpu.VMEM((tt, WS), F32), pltpu.VMEM((tt, WS), F32),
                        pltpu.VMEM((sl, 8, WS), F32), pltpu.VMEM((sl, 8, WS), F32)] + [tile8] * 8,
        compiler_params=_cparams(("arbitrary", "arbitrary")), name=name)(u, Br, Bi, Cr, Ci, ar, ai, dvec)


def ssm_seg_bwd(dy, u, xr, xi, Br, Bi, Cr, Ci, ar, ai, dvec, *, name):
    T = u.shape[0]
    tt = _ssm_tt(T)
    nt = T // tt
    sl = tt // SSM_SEGS
    WI, WS = SSM_BLOCK_IN, SSM_BLOCK_STATE
    tb = lambda i: nt - 1 - i
    xprev = lambda i: jnp.maximum(tb(i) * (tt // 8) - 1, 0)
    tn_dims = _DIMS["tn"]
    nt_dims = _DIMS["nt"]

    def body(dy_ref, u_ref, xr_ref, xi_ref, xpr_ref, xpi_ref, br_ref, bi_ref, cr_ref, ci_ref, ar_ref, ai_ref, d_ref,
             du_ref, dbr_ref, dbi_ref, dcr_ref, dci_ref, dar_ref, dai_ref, dd_ref,
             gr, gi, ext_r, ext_i, pb_r, pb_i, pw_r, pw_i, cw_r, cw_i, end_r, end_i, car_r, car_i):
        i = pl.program_id(1)

        @pl.when(i == 0)
        def _():
            _seg_tables(ar_ref, ai_ref, True, pb_r, pb_i, pw_r, pw_i, cw_r, cw_i, sl)
            car_r[...] = jnp.zeros_like(car_r)
            car_i[...] = jnp.zeros_like(car_i)
            dbr_ref[...] = jnp.zeros_like(dbr_ref)
            dbi_ref[...] = jnp.zeros_like(dbi_ref)
            dcr_ref[...] = jnp.zeros_like(dcr_ref)
            dci_ref[...] = jnp.zeros_like(dci_ref)
            dar_ref[...] = jnp.zeros_like(dar_ref)
            dai_ref[...] = jnp.zeros_like(dai_ref)
            dd_ref[...] = jnp.zeros_like(dd_ref)

        dy = dy_ref[...]
        dyb = dy.astype(BF16)
        u = u_ref[...]
        ub = u.astype(BF16)
        gr[...] = lax.dot_general(dyb, cr_ref[...].astype(BF16), nt_dims, preferred_element_type=F32)
        gi[...] = -lax.dot_general(dyb, ci_ref[...].astype(BF16), nt_dims, preferred_element_type=F32)
        lr = jnp.broadcast_to(ar_ref[...], (8, WS))
        li = -jnp.broadcast_to(ai_ref[...], (8, WS))
        rows = lax.broadcasted_iota(jnp.int32, (8, WS), 0)

        def step(jj, st):
            sr, si = st
            rw = pl.ds(pl.multiple_of((sl - 1 - jj) * 8, 8), 8)
            nr = lr * sr - li * si + gr[rw, :]
            ni = lr * si + li * sr + gi[rw, :]
            gr[rw, :] = nr
            gi[rw, :] = ni
            return nr, ni

        z8 = jnp.zeros((8, WS), F32)
        end_r[...], end_i[...] = lax.fori_loop(0, sl, step, (z8, z8))
        old_r, old_i = car_r[...], car_i[...]
        _scan_block(end_r, end_i, end_r, end_i, car_r, car_i, pw_r, pw_i, cw_r, cw_i, 1, True)(0, 0)
        s_r = jnp.where(rows == 7, old_r, pltpu.roll(end_r[...], 7, 0))
        s_i = jnp.where(rows == 7, old_i, pltpu.roll(end_i[...], 7, 0))
        first = tb(i) == 0
        last_r, last_i = xr_ref[tt - 8:tt, :], xi_ref[tt - 8:tt, :]
        pv_r = jnp.where(first, 0.0, xpr_ref[...])
        pv_i = jnp.where(first, 0.0, xpi_ref[...])
        ext_r[0:8, :] = jnp.where(rows == 0, jnp.broadcast_to(pv_r[7:8, :], (8, WS)), pltpu.roll(last_r, 1, 0))
        ext_i[0:8, :] = jnp.where(rows == 0, jnp.broadcast_to(pv_i[7:8, :], (8, WS)), pltpu.roll(last_i, 1, 0))
        ext_r[8:8 + tt, :] = xr_ref[...]
        ext_i[8:8 + tt, :] = xi_ref[...]

        def fix(j, acc):
            a_r, a_i = acc
            rw = pl.ds(pl.multiple_of(j * 8, 8), 8)
            pr, pi = pb_r[sl - 1 - j], pb_i[sl - 1 - j]
            g_r = gr[rw, :] + pr * s_r - pi * s_i
            g_i = gi[rw, :] + pr * s_i + pi * s_r
            gr[rw, :] = g_r
            gi[rw, :] = g_i
            xp_r, xp_i = ext_r[rw, :], ext_i[rw, :]
            return a_r + g_r * xp_r + g_i * xp_i, a_i + g_i * xp_r - g_r * xp_i

        a_r, a_i = lax.fori_loop(0, sl, fix, (z8, z8))
        dar_ref[...] += a_r
        dai_ref[...] += a_i
        grb = gr[...].astype(BF16)
        gib = gi[...].astype(BF16)
        dbr_ref[...] += lax.dot_general(ub, grb, tn_dims, preferred_element_type=F32)
        dbi_ref[...] += lax.dot_general(ub, gib, tn_dims, preferred_element_type=F32)
        dcr_ref[...] += lax.dot_general(xr_ref[...].astype(BF16), dyb, tn_dims, preferred_element_type=F32)
        dci_ref[...] -= lax.dot_general(xi_ref[...].astype(BF16), dyb, tn_dims, preferred_element_type=F32)
        du = (lax.dot_general(grb, br_ref[...].astype(BF16), nt_dims, preferred_element_type=F32)
              + lax.dot_general(gib, bi_ref[...].astype(BF16), nt_dims, preferred_element_type=F32)
              + d_ref[...] * dy)
        du_ref[...] = du.astype(BF16)
        dd_ref[...] += _rowsum8(dy * u)

    wspec = lambda shp: pl.BlockSpec((None,) + shp, lambda b, i: (b, 0, 0))
    vec = lambda w: pl.BlockSpec((1, w), lambda b, i: (0, b))
    tile8 = pltpu.VMEM((8, WS), F32)
    return pl.pallas_call(
        body, grid=(SSM_BLOCKS, nt),
        in_specs=[pl.BlockSpec((tt, WI), lambda b, i: (tb(i), b)), pl.BlockSpec((tt, WI), lambda b, i: (tb(i), b)),
                  pl.BlockSpec((tt, WS), lambda b, i: (tb(i), b)), pl.BlockSpec((tt, WS), lambda b, i: (tb(i), b)),
                  pl.BlockSpec((8, WS), lambda b, i: (xprev(i), b)), pl.BlockSpec((8, WS), lambda b, i: (xprev(i), b)),
                  wspec((WI, WS)), wspec((WI, WS)), wspec((WS, WI)), wspec((WS, WI)), vec(WS), vec(WS), vec(WI)],
        out_specs=[pl.BlockSpec((tt, WI), lambda b, i: (tb(i), b)),
                   wspec((WI, WS)), wspec((WI, WS)), wspec((WS, WI)), wspec((WS, WI)),
                   pl.BlockSpec((8, WS), lambda b, i: (0, b)), pl.BlockSpec((8, WS), lambda b, i: (0, b)),
                   pl.BlockSpec((8, WI), lambda b, i: (0, b))],
        out_shape=[jax.ShapeDtypeStruct((T, D_SSM), BF16),
                   jax.ShapeDtypeStruct((SSM_BLOCKS, WI, WS), F32), jax.ShapeDtypeStruct((SSM_BLOCKS, WI, WS), F32),
                   jax.ShapeDtypeStruct((SSM_BLOCKS, WS, WI), F32), jax.ShapeDtypeStruct((SSM_BLOCKS, WS, WI), F32),
                   jax.ShapeDtypeStruct((8, SSM_BLOCKS * WS), F32), jax.ShapeDtypeStruct((8, SSM_BLOCKS * WS), F32),
                   jax.ShapeDtypeStruct((8, D_SSM), F32)],
        scratch_shapes=[pltpu.VMEM((tt, WS), F32), pltpu.VMEM((tt, WS), F32),
                        pltpu.VMEM((tt + 8, WS), F32), pltpu.VMEM((tt + 8, WS), F32),
                        pltpu.VMEM((sl, 8, WS), F32), pltpu.VMEM((sl, 8, WS), F32)] + [tile8] * 8,
        compiler_params=_cparams(("arbitrary", "arbitrary")), name=name,
    )(dy, u, xr, xi, xr, xi, Br, Bi, Cr, Ci, ar, ai, dvec)


def _ssm_discretise(log_step, lam_re, lam_im, b_re, b_im):
    step = jnp.exp(log_step)[:, None]
    mag = jnp.exp(lam_re * step)
    ar = mag * jnp.cos(lam_im * step)
    ai = mag * jnp.sin(lam_im * step)
    den = lam_re * lam_re + lam_im * lam_im
    nr = ar - 1.0
    cr = (nr * lam_re + ai * lam_im) / den
    ci = (ai * lam_re - nr * lam_im) / den
    bbr = cr[..., None] * b_re - ci[..., None] * b_im
    bbi = cr[..., None] * b_im + ci[..., None] * b_re
    return ar, ai, bbr, bbi


def _blockdiag_in(bb):
    t = jnp.transpose(bb, (0, 2, 1)).reshape(SSM_BLOCKS, 8, SSM_GROUP, SSM_STATE)
    eye = jnp.eye(8, dtype=bb.dtype)
    return (t[:, :, :, None, :] * eye[None, :, None, :, None]).reshape(SSM_BLOCKS, SSM_BLOCK_IN, SSM_BLOCK_STATE)


def _blockdiag_out(cc):
    t = jnp.transpose(cc, (0, 2, 1)).reshape(SSM_BLOCKS, 8, SSM_STATE, SSM_GROUP)
    eye = jnp.eye(8, dtype=cc.dtype)
    return (t[:, :, :, None, :] * eye[None, :, None, :, None]).reshape(SSM_BLOCKS, SSM_BLOCK_STATE, SSM_BLOCK_IN)


def _diag_in(d):
    t = d.reshape(SSM_BLOCKS, 8, SSM_GROUP, 8, SSM_STATE)
    t = jnp.einsum("bghgp->bghp", t).reshape(SSM_GROUPS, SSM_GROUP, SSM_STATE)
    return jnp.transpose(t, (0, 2, 1))


def _diag_out(d):
    t = d.reshape(SSM_BLOCKS, 8, SSM_STATE, 8, SSM_GROUP)
    t = jnp.einsum("bgpgh->bgph", t).reshape(SSM_GROUPS, SSM_STATE, SSM_GROUP)
    return jnp.transpose(t, (0, 2, 1))


def _powers(ar, ai):
    rs, is_ = [ar], [ai]
    for _ in range(7):
        r, i = rs[-1], is_[-1]
        rs.append(r * ar - i * ai)
        is_.append(r * ai + i * ar)
    return jnp.stack(rs), jnp.stack(is_), jnp.stack(rs[::-1]), jnp.stack(is_[::-1])


def attn_fwd(q, kv, *, name):
    T, D = q.shape
    nm = kv.shape[0]
    tq = _pick(T, [512, 256, 128])
    scale = HEAD_DIM ** -0.5

    def body(q_ref, k_ref, v_ref, o_ref):
        for h in range(N_HEADS):
            sl = slice(h * HEAD_DIM, (h + 1) * HEAD_DIM)
            s = lax.dot_general(q_ref[:, sl], k_ref[:, sl].astype(BF16), _DIMS["nt"], preferred_element_type=F32) * scale
            e = jnp.exp(s - jnp.max(s, axis=-1, keepdims=True))
            pr = e / jnp.sum(e, axis=-1, keepdims=True)
            o_ref[:, sl] = jnp.dot(pr.astype(BF16), v_ref[:, sl].astype(BF16), preferred_element_type=F32).astype(BF16)

    return pl.pallas_call(
        body, grid=(T // tq,),
        in_specs=[_rows(tq, D), pl.BlockSpec((nm, D), lambda i: (0, 0)), pl.BlockSpec((nm, D), lambda i: (0, 1))],
        out_specs=_rows(tq, D), out_shape=jax.ShapeDtypeStruct((T, D), BF16),
        compiler_params=_cparams(("arbitrary",)), name=name)(q, kv, kv)


def attn_bwd(q, kv, do, *, name):
    T, D = q.shape
    nm = kv.shape[0]
    tq = _pick(T, [512, 256, 128])
    nt = T // tq
    scale = HEAD_DIM ** -0.5

    def body(q_ref, k_ref, v_ref, do_ref, dq_ref, dkv_ref):
        i = pl.program_id(0)

        @pl.when(i == 0)
        def _():
            dkv_ref[...] = jnp.zeros_like(dkv_ref)

        for h in range(N_HEADS):
            sl = slice(h * HEAD_DIM, (h + 1) * HEAD_DIM)
            slv = slice(D + h * HEAD_DIM, D + (h + 1) * HEAD_DIM)
            qh = q_ref[:, sl]
            kh = k_ref[:, sl].astype(BF16)
            vh = v_ref[:, sl].astype(BF16)
            doh = do_ref[:, sl].astype(BF16)
            s = lax.dot_general(qh, kh, _DIMS["nt"], preferred_element_type=F32) * scale
            e = jnp.exp(s - jnp.max(s, axis=-1, keepdims=True))
            pr = e / jnp.sum(e, axis=-1, keepdims=True)
            dp = lax.dot_general(doh, vh, _DIMS["nt"], preferred_element_type=F32)
            ds = (pr * (dp - jnp.sum(pr * dp, axis=-1, keepdims=True)) * scale).astype(BF16)
            dq_ref[:, sl] = jnp.dot(ds, kh, preferred_element_type=F32).astype(BF16)
            dkv_ref[:, sl] += lax.dot_general(ds, qh, _DIMS["tn"], preferred_element_type=F32)
            dkv_ref[:, slv] += lax.dot_general(pr.astype(BF16), doh, _DIMS["tn"], preferred_element_type=F32)

    return pl.pallas_call(
        body, grid=(nt,),
        in_specs=[_rows(tq, D), pl.BlockSpec((nm, D), lambda i: (0, 0)), pl.BlockSpec((nm, D), lambda i: (0, 1)), _rows(tq, D)],
        out_specs=[_rows(tq, D), _const((nm, 2 * D))],
        out_shape=[jax.ShapeDtypeStruct((T, D), BF16), jax.ShapeDtypeStruct((nm, 2 * D), F32)],
        compiler_params=_cparams(("arbitrary",)), name=name)(q, kv, kv, do)


def _adam_math(w, g, m, v):
    m = ADAM_B1 * m + (1.0 - ADAM_B1) * g
    v = ADAM_B2 * v + (1.0 - ADAM_B2) * (g * g)
    m_hat = m / (1.0 - ADAM_B1 ** ADAM_STEP)
    v_hat = v / (1.0 - ADAM_B2 ** ADAM_STEP)
    delta = -ADAM_LR * (m_hat / (jnp.sqrt(v_hat) + ADAM_EPS) + ADAM_WD * w)
    return delta, m, v


def adamw(w, m, v, g_arr, g_row0, *, name):
    R, C = w.shape
    tr = _pick(R, [512, 256, 128, 64, 32, 16, 8])
    assert g_row0 % tr == 0
    g0 = g_row0 // tr

    def body(w_ref, m_ref, v_ref, g_ref, go_ref, d_ref, mo_ref, vo_ref):
        g = g_ref[...]
        d, mn, vn = _adam_math(w_ref[...], g, m_ref[...], v_ref[...])
        go_ref[...] = g
        d_ref[...] = d
        mo_ref[...] = mn
        vo_ref[...] = vn

    sp = pl.BlockSpec((tr, C), lambda i: (i, 0))
    return pl.pallas_call(
        body, grid=(R // tr,), in_specs=[sp, sp, sp, pl.BlockSpec((tr, C), lambda i: (g0 + i, 0))],
        out_specs=[sp] * 4, out_shape=[jax.ShapeDtypeStruct((R, C), F32)] * 4,
        compiler_params=_cparams(("arbitrary",)), name=name)(w, m, v, g_arr)


def _place():
    x, y, c = lax.axis_index("x"), lax.axis_index("y"), lax.axis_index("c")
    chips = [(1 - x, y), (x, 1 - y), (1 - x, 1 - y)]
    return x, y, c, chips


ANY = pl.BlockSpec(memory_space=pl.ANY)


def allgather_weights(bufs, *, name):
    n = len(bufs)

    def body(*refs):
        o_refs = refs[n:2 * n]
        send_sems, recv_sems, fsend_sems, frecv_sems = refs[2 * n:]
        x, y, c, chips = _place()
        k_me = 2 * x + y
        sib = (x, y, 1 - c)
        halves = [b.shape[1] // 2 for b in bufs]

        def half(a, cc):
            return pl.ds(pl.multiple_of(cc * halves[a], 16), halves[a])

        sends = []
        for a in range(n):
            for r, (px, py) in enumerate(chips):
                cp = pltpu.make_async_remote_copy(
                    src_ref=o_refs[a].at[k_me, half(a, c)], dst_ref=o_refs[a].at[k_me, half(a, c)],
                    send_sem=send_sems.at[3 * a + r], recv_sem=recv_sems.at[3 * a + r],
                    device_id=(px, py, c), device_id_type=MESH)
                cp.start()
                sends.append(cp)
        passed = []
        for a in range(n):
            for r, (px, py) in enumerate(chips):
                win = o_refs[a].at[2 * px + py, half(a, c)]
                pltpu.make_async_remote_copy(
                    src_ref=win, dst_ref=win, send_sem=send_sems.at[3 * a + r], recv_sem=recv_sems.at[3 * a + r],
                    device_id=(px, py, c), device_id_type=MESH).wait_recv()
                cp = pltpu.make_async_remote_copy(
                    src_ref=win, dst_ref=win, send_sem=fsend_sems.at[3 * a + r], recv_sem=frecv_sems.at[3 * a + r],
                    device_id=sib, device_id_type=MESH)
                cp.start()
                passed.append(cp)
        for a in range(n):
            for r, (px, py) in enumerate(chips):
                win = o_refs[a].at[2 * px + py, half(a, 1 - c)]
                pltpu.make_async_remote_copy(
                    src_ref=win, dst_ref=win, send_sem=fsend_sems.at[3 * a + r], recv_sem=frecv_sems.at[3 * a + r],
                    device_id=sib, device_id_type=MESH).wait_recv()
        for cp in sends + passed:
            cp.wait_send()

    return pl.pallas_call(
        body, in_specs=[ANY] * n, out_specs=[ANY] * n,
        out_shape=[jax.ShapeDtypeStruct(b.shape, b.dtype) for b in bufs],
        scratch_shapes=[pltpu.SemaphoreType.DMA((3 * n,))] * 4,
        input_output_aliases={a: a for a in range(n)},
        name=name)(*bufs)


HBM_SPEC = pl.BlockSpec(memory_space=pltpu.HBM)
SEM_SPEC = pl.BlockSpec(memory_space=pltpu.SEMAPHORE)


def _hbm(a):
    return pltpu.with_memory_space_constraint(a, pltpu.HBM)


def gather_start(bufs, pieces, *, name):
    n = len(bufs)
    npc = len(pieces)

    def body(*refs):
        b_refs = refs[:n]
        send_sems, recv_sems = refs[n], refs[n + 1]
        x, y, c, chips = _place()
        k_me = 2 * x + y
        for q, (a, row0, rows) in enumerate(pieces):
            win = b_refs[a].at[k_me, pl.ds(row0, rows)]
            for r, (px, py) in enumerate(chips):
                pltpu.make_async_remote_copy(
                    src_ref=win, dst_ref=win, send_sem=send_sems.at[3 * q + r], recv_sem=recv_sems.at[3 * q + r],
                    device_id=(px, py, c), device_id_type=MESH).start()

    return pl.pallas_call(
        body, in_specs=[HBM_SPEC] * n, out_specs=[SEM_SPEC, SEM_SPEC] + [HBM_SPEC] * n,
        out_shape=[pltpu.SemaphoreType.DMA((3 * npc,)), pltpu.SemaphoreType.DMA((3 * npc,))]
        + [pltpu.HBM(b.shape, b.dtype) for b in bufs],
        input_output_aliases={a: 2 + a for a in range(n)},
        compiler_params=pltpu.CompilerParams(has_side_effects=pltpu.SideEffectType.DATAFLOW_SIDE_EFFECTING),
        name=name)(*[_hbm(b) for b in bufs])


def gather_wait(send_sems, recv_sems, bufs, which, after, *, name):
    n = len(bufs)

    def body(*refs):
        b_refs = refs[:n]
        send_sems, recv_sems = refs[n], refs[n + 1]
        x, y, c, chips = _place()
        k_me = 2 * x + y
        for a, row0, rows, q in which:
            for r, (px, py) in enumerate(chips):
                cp = pltpu.make_async_remote_copy(
                    src_ref=b_refs[a].at[k_me, pl.ds(row0, rows)], dst_ref=b_refs[a].at[2 * px + py, pl.ds(row0, rows)],
                    send_sem=send_sems.at[3 * q + r], recv_sem=recv_sems.at[3 * q + r],
                    device_id=(px, py, c), device_id_type=MESH)
                cp.wait_send()
                cp.wait_recv()

    return pl.pallas_call(
        body, in_specs=[HBM_SPEC] * n + [SEM_SPEC, SEM_SPEC, ANY], out_specs=[HBM_SPEC] * n,
        out_shape=[pltpu.HBM(b.shape, b.dtype) for b in bufs],
        input_output_aliases={a: a for a in range(n)},
        compiler_params=pltpu.CompilerParams(has_side_effects=pltpu.SideEffectType.DATAFLOW_SIDE_EFFECTING),
        name=name)(*bufs, send_sems, recv_sems, after)


def exchange_halves(grads, *, name):
    n = len(grads)

    def body(*refs):
        g_refs, l_refs = refs[:n], refs[n:2 * n]
        send_sems, recv_sems = refs[2 * n:]
        x, y, c, _ = _place()
        cps = []
        for a in range(n):
            h = grads[a].shape[1] // 2
            cp = pltpu.make_async_remote_copy(
                src_ref=g_refs[a].at[:, pl.ds(pl.multiple_of((1 - c) * h, 8), h)], dst_ref=l_refs[a],
                send_sem=send_sems.at[a], recv_sem=recv_sems.at[a], device_id=(x, y, 1 - c), device_id_type=MESH)
            cp.start()
            cps.append(cp)
        for cp in cps:
            cp.wait()

    return pl.pallas_call(
        body, in_specs=[ANY] * n, out_specs=[ANY] * n,
        out_shape=[jax.ShapeDtypeStruct((g.shape[0], g.shape[1] // 2, g.shape[2]), g.dtype) for g in grads],
        scratch_shapes=[pltpu.SemaphoreType.DMA((n,))] * 2,
        name=name)(*grads)


N_PEERS = N_DEV - 1


def _scatter_copies(p_refs, l_refs, send_sems, recv_sems):
    x, y, c, _ = _place()
    cps = []
    for a in range(len(p_refs)):
        h = p_refs[a].shape[1] // 2
        for fx, fy in ((0, 0), (1, 0), (0, 1), (1, 1)):
            for fc in (0, 1):
                if (fx, fy, fc) == (0, 0, 0):
                    continue
                slot = 2 * (fx + 2 * fy) + fc - 1
                px, py, pc = (1 - x if fx else x), (1 - y if fy else y), (1 - c if fc else c)
                cps.append(pltpu.make_async_remote_copy(
                    src_ref=p_refs[a].at[2 * px + py, pl.ds(pl.multiple_of(pc * h, 16), h)], dst_ref=l_refs[a].at[slot],
                    send_sem=send_sems.at[N_PEERS * a + slot], recv_sem=recv_sems.at[N_PEERS * a + slot],
                    device_id=(px, py, pc), device_id_type=MESH))
    return cps


def scatter_start(parts, *, name):
    n = len(parts)
    lands = [lax.empty((N_PEERS, p.shape[1] // 2, p.shape[2]), p.dtype) for p in parts]

    def body(*refs):
        for cp in _scatter_copies(refs[:n], refs[n:2 * n], refs[2 * n], refs[2 * n + 1]):
            cp.start()

    outs = pl.pallas_call(
        body, in_specs=[HBM_SPEC] * (2 * n), out_specs=[SEM_SPEC, SEM_SPEC] + [HBM_SPEC] * (2 * n),
        out_shape=[pltpu.SemaphoreType.DMA((N_PEERS * n,)), pltpu.SemaphoreType.DMA((N_PEERS * n,))]
        + [pltpu.HBM(a.shape, a.dtype) for a in parts + lands],
        input_output_aliases={a: 2 + a for a in range(2 * n)},
        compiler_params=pltpu.CompilerParams(has_side_effects=pltpu.SideEffectType.DATAFLOW_SIDE_EFFECTING),
        name=name)(*[_hbm(a) for a in parts + lands])
    return outs[0], outs[1], list(outs[2:2 + n]), list(outs[2 + n:])


def scatter_wait(rounds, after, *, name):
    sizes = [len(r[2]) for r in rounds]
    flat = [a for r in rounds for a in r[2] + r[3]]
    sems = [s for r in rounds for s in (r[0], r[1])]
    nflat = len(flat)

    def body(*refs):
        pos = 0
        for ri, n in enumerate(sizes):
            for cp in _scatter_copies(refs[pos:pos + n], refs[pos + n:pos + 2 * n], refs[nflat + 2 * ri], refs[nflat + 2 * ri + 1]):
                cp.wait_send()
                cp.wait_recv()
            pos += 2 * n

    outs = pl.pallas_call(
        body, in_specs=[HBM_SPEC] * nflat + [SEM_SPEC] * len(sems) + [ANY], out_specs=[HBM_SPEC] * nflat,
        out_shape=[pltpu.HBM(a.shape, a.dtype) for a in flat],
        input_output_aliases={a: a for a in range(nflat)},
        compiler_params=pltpu.CompilerParams(has_side_effects=pltpu.SideEffectType.DATAFLOW_SIDE_EFFECTING),
        name=name)(*flat, *sems, after)
    res, pos = [], 0
    for n in sizes:
        res.append((list(outs[pos:pos + n]), list(outs[pos + n:pos + 2 * n])))
        pos += 2 * n
    return res


def join_halves(fulls, *, name):
    n = len(fulls)

    def body(*refs):
        o_refs = refs[n:2 * n]
        send_sems, recv_sems = refs[2 * n:]
        x, y, c, _ = _place()
        cps = []
        for a in range(n):
            h = fulls[a].shape[0] // 2
            win = o_refs[a].at[pl.ds(pl.multiple_of(c * h, 8), h)]
            cp = pltpu.make_async_remote_copy(
                src_ref=win, dst_ref=win, send_sem=send_sems.at[a], recv_sem=recv_sems.at[a],
                device_id=(x, y, 1 - c), device_id_type=MESH)
            cp.start()
            cps.append(cp)
        for a in range(n):
            h = fulls[a].shape[0] // 2
            other = o_refs[a].at[pl.ds(pl.multiple_of((1 - c) * h, 8), h)]
            pltpu.make_async_remote_copy(
                src_ref=other, dst_ref=other, send_sem=send_sems.at[a], recv_sem=recv_sems.at[a],
                device_id=(x, y, 1 - c), device_id_type=MESH).wait_recv()
        for cp in cps:
            cp.wait_send()

    return pl.pallas_call(
        body, in_specs=[ANY] * n, out_specs=[ANY] * n,
        out_shape=[jax.ShapeDtypeStruct(f.shape, f.dtype) for f in fulls],
        scratch_shapes=[pltpu.SemaphoreType.DMA((n,))] * 2,
        input_output_aliases={a: a for a in range(n)},
        name=name)(*fulls)


def add_partials(part, land, kc, *, name):
    _, R, C = part.shape
    H = R // 2
    tr = _pick(H, [256, 128, 64, 32, 16])
    per = H // tr

    def body(kc_ref, p_ref, l_ref, o_ref):
        acc = p_ref[...].astype(F32)
        for s in range(N_PEERS):
            acc = acc + l_ref[s].astype(F32)
        o_ref[...] = acc

    return pl.pallas_call(
        body,
        grid_spec=pltpu.PrefetchScalarGridSpec(
            num_scalar_prefetch=1, grid=(per,),
            in_specs=[pl.BlockSpec((None, tr, C), lambda i, kc_ref: (kc_ref[0], kc_ref[1] * per + i, 0)),
                      pl.BlockSpec((N_PEERS, tr, C), lambda i, kc_ref: (0, i, 0))],
            out_specs=pl.BlockSpec((tr, C), lambda i, kc_ref: (kc_ref[1] * per + i, 0))),
        out_shape=jax.ShapeDtypeStruct((R, C), F32),
        compiler_params=_cparams(("arbitrary",)), name=name)(kc, part, land)


def allgather_sum(v, *, name):
    m_per, n = v.shape

    def body(x_ref, out_ref, sum_ref, send_sems, recv_sems, local_sem):
        x, y, c, chips = _place()
        me, sibling = (x, y, c), (x, y, 1 - c)

        def rows(px, py, pc):
            return out_ref.at[pl.ds(pl.multiple_of((4 * px + 2 * py + pc) * m_per, 8), m_per), :]

        def copy(k, block, to, src=None):
            return pltpu.make_async_remote_copy(
                src_ref=rows(*block) if src is None else src, dst_ref=rows(*block),
                send_sem=send_sems.at[k], recv_sem=recv_sems.at[k], device_id=to, device_id_type=MESH)

        mine = pltpu.make_async_copy(x_ref, rows(*me), local_sem)
        mine.start()
        first = [copy(0, me, sibling, src=x_ref)]
        first += [copy(1 + j, me, (*chip, c), src=x_ref) for j, chip in enumerate(chips)]
        for cp in first:
            cp.start()
        passed = [copy(4 + j, (*chip, c), sibling) for j, chip in enumerate(chips)]
        for j, chip in enumerate(chips):
            copy(1 + j, (*chip, c), me).wait_recv()
            passed[j].start()
        copy(0, sibling, me).wait_recv()
        for j, chip in enumerate(chips):
            copy(4 + j, (*chip, 1 - c), me).wait_recv()
        for cp in first + passed:
            cp.wait_send()
        mine.wait()
        acc = out_ref[0:m_per, :]
        for d in range(1, N_DEV):
            acc = acc + out_ref[d * m_per:(d + 1) * m_per, :]
        sum_ref[...] = acc

    vm = pl.BlockSpec(memory_space=pltpu.VMEM)
    return pl.pallas_call(
        body, in_specs=[vm], out_specs=[vm, vm],
        out_shape=[jax.ShapeDtypeStruct((N_DEV * m_per, n), v.dtype), jax.ShapeDtypeStruct((m_per, n), v.dtype)],
        scratch_shapes=[pltpu.SemaphoreType.DMA((7,)), pltpu.SemaphoreType.DMA((7,)), pltpu.SemaphoreType.DMA],
        compiler_params=pltpu.CompilerParams(vmem_limit_bytes=VMEM_LIMIT_BYTES), name=name)(v)


def allreduce_two_level(v, *, name):
    m, n = v.shape
    h = m // 2

    def body(x_ref, out_ref, sib_ref, chip_ref, sems_send, sems_recv):
        x, y, c, chips = _place()
        k_me = 2 * x + y
        sib = (x, y, 1 - c)
        mine = pl.ds(pl.multiple_of(c * h, 8), h)
        other = pl.ds(pl.multiple_of((1 - c) * h, 8), h)

        def copy(q, src, dst, to):
            return pltpu.make_async_remote_copy(src_ref=src, dst_ref=dst, send_sem=sems_send.at[q], recv_sem=sems_recv.at[q],
                                                device_id=to, device_id_type=MESH)

        first = copy(0, x_ref.at[other], sib_ref, sib)
        first.start()
        first.wait()
        chip_ref[k_me] = x_ref[mine, :] + sib_ref[...]
        sends = [copy(1 + r, chip_ref.at[k_me], chip_ref.at[k_me], (px, py, c)) for r, (px, py) in enumerate(chips)]
        for cp in sends:
            cp.start()
        for r, (px, py) in enumerate(chips):
            copy(1 + r, chip_ref.at[2 * px + py], chip_ref.at[2 * px + py], (px, py, c)).wait_recv()
        for cp in sends:
            cp.wait_send()
        total = ((chip_ref[0] + chip_ref[1]) + chip_ref[2]) + chip_ref[3]
        out_ref[mine, :] = total
        last = copy(4, out_ref.at[mine], out_ref.at[mine], sib)
        last.start()
        copy(4, out_ref.at[other], out_ref.at[other], sib).wait_recv()
        last.wait_send()

    vm = pl.BlockSpec(memory_space=pltpu.VMEM)
    return pl.pallas_call(
        body, in_specs=[vm], out_specs=vm, out_shape=jax.ShapeDtypeStruct((m, n), v.dtype),
        scratch_shapes=[pltpu.VMEM((h, n), v.dtype), pltpu.VMEM((N_CHIPS, h, n), v.dtype),
                        pltpu.SemaphoreType.DMA((5,)), pltpu.SemaphoreType.DMA((5,))],
        compiler_params=pltpu.CompilerParams(vmem_limit_bytes=VMEM_LIMIT_BYTES), name=name)(v)


PACK_W = D_MODEL


def _pack_rows(shape):
    return -(-math.prod(shape) // PACK_W)


def _pack(arrs):
    cols = []
    for a in arrs:
        f = a.reshape(-1)
        pad = (-f.shape[0]) % PACK_W
        cols.append((jnp.pad(f, (0, pad)) if pad else f).reshape(-1, PACK_W))
    out = jnp.concatenate(cols, axis=0)
    pad = (-out.shape[0]) % 16
    return jnp.pad(out, ((0, pad), (0, 0)))


def _unpack(buf, shapes):
    outs, r = [], 0
    for s in shapes:
        nel = math.prod(s)
        nr = _pack_rows(s)
        outs.append(buf[r:r + nr].reshape(-1)[:nel].reshape(s))
        r += nr
    return outs


GA_CONV_OUT, GA_MIX_OUT, GA_WQ, GA_WO, GA_DOWN, GA_UP, GA_ROWS = 0, 256, 512, 768, 1024, 2048, 3072
G1_DOWN, G1_UP, G1_ROWS = 0, 1024, 2048
G2_CONV_OUT, G2_MIX_OUT, G2_WQ, G2_WO, G2_ROWS = 0, 256, 512, 768, 1024


def kernel(x, mem, in_norm_g, in_norm_b, w_in, conv_dw, conv_db, conv_norm_g, conv_norm_b, w_conv_out, ssm_log_step, ssm_lambda_re, ssm_lambda_im, ssm_b_re, ssm_b_im, ssm_c_re, ssm_c_im, ssm_d, w_ssm_glu, w_mix_out, ln1_g, ln1_b, xa_wq, xa_wkv, xa_wo, ln2_g, ln2_b, mlp_w_up, mlp_w_down, ln3_g, ln3_b, loss_target, m_in_norm_g, m_in_norm_b, m_w_in, m_conv_dw, m_conv_db, m_conv_norm_g, m_conv_norm_b, m_w_conv_out, m_ssm_log_step, m_ssm_lambda_re, m_ssm_lambda_im, m_ssm_b_re, m_ssm_b_im, m_ssm_c_re, m_ssm_c_im, m_ssm_d, m_w_ssm_glu, m_w_mix_out, m_ln1_g, m_ln1_b, m_xa_wq, m_xa_wkv, m_xa_wo, m_ln2_g, m_ln2_b, m_mlp_w_up, m_mlp_w_down, m_ln3_g, m_ln3_b, v_in_norm_g, v_in_norm_b, v_w_in, v_conv_dw, v_conv_db, v_conv_norm_g, v_conv_norm_b, v_w_conv_out, v_ssm_log_step, v_ssm_lambda_re, v_ssm_lambda_im, v_ssm_b_re, v_ssm_b_im, v_ssm_c_re, v_ssm_c_im, v_ssm_d, v_w_ssm_glu, v_w_mix_out, v_ln1_g, v_ln1_b, v_xa_wq, v_xa_wkv, v_xa_wo, v_ln2_g, v_ln2_b, v_mlp_w_up, v_mlp_w_down, v_ln3_g, v_ln3_b):
    D = D_MODEL
    xs = x[0]
    T = xs.shape[0]
    mems = mem[0]
    NM = mems.shape[0]
    tgt = loss_target[0]
    my_c = lax.axis_index("c")
    k_me = 2 * lax.axis_index("x") + lax.axis_index("y")
    c_arr = jnp.reshape(my_c, (1,)).astype(jnp.int32)
    k_arr = jnp.reshape(k_me, (1,)).astype(jnp.int32)

    sh_a = jnp.concatenate([w_conv_out[0], w_mix_out[0], xa_wq[0], xa_wo[0], mlp_w_down[0], mlp_w_up[0]], axis=0).astype(BF16)
    def own_block(shard):
        buf = lax.empty((N_CHIPS,) + shard.shape, shard.dtype)
        return lax.dynamic_update_slice(buf, shard[None], (k_me, 0, 0))

    dw_pad = jnp.pad(conv_dw[0], ((0, CONV_HALO - CONV_K), (0, 0)))
    (GIN,) = allgather_weights([own_block(w_in[0].astype(BF16))], name="gather_w_in")
    ag_bufs = [own_block(sh_a), GIN] + [own_block(s) for s in (xa_wkv[0].astype(BF16), w_ssm_glu[0].astype(BF16), dw_pad)]
    ag_pieces = [(4, 0, CONV_HALO), (0, GA_CONV_OUT, 256), (3, 0, D_SSM), (0, GA_MIX_OUT, 256), (0, GA_WQ, 256),
                 (2, 0, D), (0, GA_WO, 256), (0, GA_UP, D), (0, GA_DOWN, D)]
    ag_send, ag_recv, GA, GIN, GKV, GGLU, GDW = gather_start(ag_bufs, ag_pieces, name="gather_start")

    def w_rowshard(row0):
        return dict(b_spec=((N_CHIPS, 256, D), lambda i, j, k: (0, row0 // 256, 0)), b_view=(D, D), tn=D, tk=D)

    _, h0, h0b = ln_fwd(xs, in_norm_g, in_norm_b, name="ln0_fwd")
    p = mm_nn(h0b, GIN, ((None, D, 1152), lambda i, j, k: (j, 0, 0)), D_IN, tn=1152, tk=D, name="mm_w_in")[0]
    GA, GGLU, GDW = gather_wait(
        ag_send, ag_recv, [GA, GGLU, GDW],
        [(2, 0, CONV_HALO, 0), (0, GA_CONV_OUT, 256, 1), (1, 0, D_SSM, 2), (0, GA_MIX_OUT, 256, 3)], p, name="gather_wait_mixer")
    dw_taps = jnp.transpose(GDW, (1, 0, 2)).reshape(CONV_HALO, D)
    c_pre, actb = conv_fwd(p, dw_taps, conv_db, conv_norm_g[0].reshape(1, D), conv_norm_b[0].reshape(1, D), name="conv_fwd")
    ya = mm_nn(actb, GA, N=D, name="mm_conv_out", **w_rowshard(GA_CONV_OUT))[0]

    lstep, lre, lim = ssm_log_step[0], ssm_lambda_re[0], ssm_lambda_im[0]
    bre, bim, cre, cim = ssm_b_re[0], ssm_b_im[0], ssm_c_re[0], ssm_c_im[0]
    (ar, ai, bbr, bbi), disc_vjp = jax.vjp(_ssm_discretise, lstep, lre, lim, bre, bim)
    Br, Bi = _blockdiag_in(bbr), _blockdiag_in(bbi)
    Cr, Ci = _blockdiag_out(cre), _blockdiag_out(cim)
    lam_r, lam_i = ar.reshape(1, -1), ai.reshape(1, -1)
    dvec = ssm_d[0].reshape(1, D_SSM)
    tt_ssm = _ssm_tt(T)
    u_seg = seg_perm(p[:, 2 * D:2 * D + D_SSM], tt_ssm)
    xr, xi, yssm_seg = ssm_seg_fwd(u_seg, Br, Bi, Cr, Ci, lam_r, lam_i, dvec, name="ssm_fwd")
    yssm = seg_unperm(yssm_seg, tt_ssm)
    z = mm_nn(yssm, GGLU, ((None, D_SSM, 512), lambda i, j, k: (j, 0, 0)), 2 * D, tn=512, tk=D_SSM, name="mm_ssm_glu")[0]
    mergedb = merge_fwd(p, ya, z, name="merge_fwd")
    tm_ln = _pick(T, [512, 256, 128])
    row_spec = ((1, D), lambda i, j, k: (0, 0))

    def ln_epilogue(acc, res, g, b):
        r = ALPHA * res + acc
        xhat, _ = _ln_stats(r)
        h = xhat * g + b
        return r, h, h

    def mm_ln(a, row0, res, g, b, name):
        return mm_nn(a, GA, N=D, tm=tm_ln, extras=(res, g.reshape(1, D), b.reshape(1, D)),
                     extra_specs=[_mn(tm_ln, D), row_spec, row_spec], epilogue=ln_epilogue, out_dtypes=[F32, F32, BF16],
                     name=name, **w_rowshard(row0))

    r1, h1, h1b = mm_ln(mergedb, GA_MIX_OUT, h0, ln1_g[0], ln1_b[0], "mm_mix_out_ln1")
    GA, GKV = gather_wait(ag_send, ag_recv, [GA, GKV], [(0, GA_WQ, 256, 4), (1, 0, D, 5), (0, GA_WO, 256, 6)], r1,
                          name="gather_wait_attn")

    qb = mm_nn(h1b, GA, N=D, out_dtype=BF16, name="mm_wq", **w_rowshard(GA_WQ))[0]
    kv = mm_nn(mems, GKV, ((None, D, 512), lambda i, j, k: (j, 0, 0)), 2 * D, tn=512, tk=D, name="mm_wkv")[0]
    ob = attn_fwd(qb, kv, name="attn_fwd")
    r2, h2, h2b = mm_ln(ob, GA_WO, h1, ln2_g[0], ln2_b[0], "mm_wo_ln2")
    (GA,) = gather_wait(ag_send, ag_recv, [GA], [(0, GA_UP, D, 7), (0, GA_DOWN, D, 8)], r2, name="gather_wait_mlp")

    def relu2(acc):
        zr = jnp.maximum(acc, 0.0)
        return (zr * zr,)

    zzb = mm_nn(h2b, GA, ((None, D, D), lambda i, j, k: (j, GA_UP // D, 0)), D_FF, tn=D, tk=D,
                out_dtype=BF16, epilogue=relu2, name="mm_up")[0]
    ff = mm_nn(zzb, GA, ((N_CHIPS, D, D), lambda i, j, k: (0, GA_DOWN // D, 0)), D, tm=_pick(T, [512, 256, 128]), tn=D, tk=D_FF,
               b_view=(D_FF, D), name="mm_down")[0]
    dr3, dr3b, dg3, db3, sq = ln_loss_bwd(ff, h2, ln3_g[0], ln3_b[0], tgt, name="ln3_loss_bwd")

    def rs_begin(grads, rnd):
        return scatter_start(grads, name=f"rs{rnd}_scatter_start")

    g1_shape = jax.ShapeDtypeStruct((N_CHIPS, G1_ROWS, D), BF16)
    g2_shape = jax.ShapeDtypeStruct((N_CHIPS, G2_ROWS, D), BF16)
    dzpreb = mm_nt(dr3b, GA, ((None, D, D), lambda i, j, k: (j, GA_DOWN // D, 0)), D_FF, tn=D, tk=D, out_dtype=BF16,
                   extras=(zzb,), epilogue=lambda acc, zz: (acc * (2.0 * jnp.sqrt(zz.astype(F32))),), name="mm_down_t")[0]
    G1g = mm_tn(zzb, dr3b, tm=D, tn=D, tk=T, out_spec=((None, D, D), lambda i, j, k: (i, G1_DOWN // D, 0)),
                out_shape=g1_shape, name="mm_down_g")
    G1g = mm_tn(h2b, dzpreb, tm=D, tn=D, tk=T, out_spec=((None, D, D), lambda i, j, k: (j, G1_UP // D, 0)),
                out_shape=g1_shape, out_buf=G1g, name="mm_up_g")
    round1 = rs_begin([G1g], 1)
    dh2 = mm_nt(dzpreb, GA, ((N_CHIPS, D, D), lambda i, j, k: (0, GA_UP // D, 0)), D, tm=_pick(T, [512, 256, 128]), tn=D,
                tk=D_FF, b_chunks=N_CHIPS, extras=(dr3,), epilogue=lambda acc, d: (acc + ALPHA * d,),
                after=(round1[2][0],), name="mm_up_t")[0]
    dr2, dr2b, dg2, db2 = ln_bwd(r2, dh2, ln2_g[0], name="ln2_bwd")

    def g_rowshard(row0, out_buf):
        return dict(tm=D, tn=D, out_spec=((N_CHIPS, 256, D), lambda i, j, k: (0, row0 // 256, 0)), out_shape=g2_shape,
                    out_buf=out_buf)

    dob = mm_nt(dr2b, GA, N=D, out_dtype=BF16, name="mm_wo_t", **w_rowshard(GA_WO))[0]
    G2g = mm_tn(ob, dr2b, name="mm_wo_g", **g_rowshard(G2_WO, None))
    dqb, dkv = attn_bwd(qb, kv, dob, name="attn_bwd")
    G2g = mm_tn(h1b, dqb, name="mm_wq_g", **g_rowshard(G2_WQ, G2g))
    GKVg = mm_tn(mems, dkv, tm=D, tn=512, tk=NM, out_spec=((None, D, 512), lambda i, j, k: (j, 0, 0)),
                 out_shape=jax.ShapeDtypeStruct((N_CHIPS, D, 512), BF16), name="mm_wkv_g")
    dh1 = mm_nt(dqb, GA, N=D, extras=(dr2,), epilogue=lambda acc, d: (acc + ALPHA * d,), name="mm_wq_t",
                **w_rowshard(GA_WQ))[0]
    dr1, dr1b, dg1, db1 = ln_bwd(r1, dh1, ln1_g[0], name="ln1_bwd")

    dmerged = mm_nt(dr1b, GA, N=D, name="mm_mix_t", **w_rowshard(GA_MIX_OUT))[0]
    G2g = mm_tn(mergedb, dr1b, name="mm_mix_g", **g_rowshard(G2_MIX_OUT, G2g))
    dyab, dgatesb, dzb = merge_bwd(dmerged, p, ya, z, name="merge_bwd")
    GGLUg = mm_tn(yssm, dzb, tm=D_SSM, tn=512, out_spec=((None, D_SSM, 512), lambda i, j, k: (j, 0, 0)),
                  out_shape=jax.ShapeDtypeStruct((N_CHIPS, D_SSM, 512), BF16), name="mm_glu_g")
    dyssm = mm_nt(dzb, GGLU, ((N_CHIPS, D_SSM, 512), lambda i, j, k: (0, 0, 0)), D_SSM, tn=D_SSM, tk=2 * D, b_chunks=N_CHIPS,
                  name="mm_glu_t")[0]
    dub_seg, dBr, dBi, dCr, dCi, dar8, dai8, dd8 = ssm_seg_bwd(seg_perm(dyssm, tt_ssm), u_seg, xr, xi, Br, Bi, Cr, Ci,
                                                               lam_r, lam_i, dvec, name="ssm_bwd")
    dub = seg_unperm(dub_seg, tt_ssm)
    dar = jnp.sum(dar8, axis=0).reshape(SSM_GROUPS, SSM_STATE)
    dai = jnp.sum(dai8, axis=0).reshape(SSM_GROUPS, SSM_STATE)
    g_lstep, g_lre, g_lim, g_bre, g_bim = disc_vjp((dar, dai, _diag_in(dBr), _diag_in(dBi)))
    g_cre, g_cim = _diag_out(dCr), _diag_out(dCi)
    g_d = jnp.sum(dd8, axis=0).reshape(1, D_SSM)

    dact = mm_nt(dyab, GA, N=D, name="mm_conv_out_t", **w_rowshard(GA_CONV_OUT))[0]
    G2g = mm_tn(actb, dyab, name="mm_conv_out_g", **g_rowshard(G2_CONV_OUT, G2g))
    round2 = rs_begin([G2g, GKVg, GGLUg], 2)
    dc, dng, dnb, ddb = conv_bwd_norm(dact, c_pre, conv_norm_g[0].reshape(1, D), conv_norm_b[0].reshape(1, D),
                                      round2[2][0], name="conv_bwd_norm")
    dpb, ddw = conv_bwd_taps(dc, p, dw_taps, dub, dgatesb, name="conv_bwd_taps")
    GINg = mm_tn(h0b, dpb, tm=D, tn=1152, tk=T, out_spec=((None, D, 1152), lambda i, j, k: (j, 0, 0)),
                 out_shape=jax.ShapeDtypeStruct((N_CHIPS, D, 1152), BF16), name="mm_w_in_g")
    round3 = rs_begin([GINg], 3)
    dh0 = mm_nt(dpb, GIN, ((N_CHIPS, D, 1152), lambda i, j, k: (0, 0, 0)), D, tm=_pick(T, [512, 256, 128]), tn=D, tk=D_IN,
                b_chunks=N_CHIPS, extras=(dr1,), epilogue=lambda acc, d: (acc + ALPHA * d,), after=(round3[2][0],),
                name="mm_w_in_t")[0]
    gx, _, dg0, db0 = ln_bwd(xs, dh0, in_norm_g, name="ln0_bwd")

    kc_arr = jnp.concatenate([k_arr, c_arr])
    landed = scatter_wait([round1, round2, round3], gx, name="rs_scatter_wait")
    tags = ["mlp", "sq", "kv", "glu", "in"]
    pairs = [(pt, l2) for parts, lands2 in landed for pt, l2 in zip(parts, lands2)]
    halves = [add_partials(pt, l2, kc_arr, name="rs_add_partials_" + t) for (pt, l2), t in zip(pairs, tags)]
    g1, g2, gKV, gGLU, gIN = join_halves(halves, name="rs_join_halves")

    small_names = ["in_norm_g", "in_norm_b", "conv_db", "conv_norm_g", "conv_norm_b", "ssm_log_step", "ssm_lambda_re",
                   "ssm_lambda_im", "ssm_b_re", "ssm_b_im", "ssm_c_re", "ssm_c_im", "ssm_d", "ln1_g", "ln1_b",
                   "ln2_g", "ln2_b", "ln3_g", "ln3_b"]
    small_w = [in_norm_g, in_norm_b, conv_db, conv_norm_g, conv_norm_b, ssm_log_step, ssm_lambda_re, ssm_lambda_im,
               ssm_b_re, ssm_b_im, ssm_c_re, ssm_c_im, ssm_d, ln1_g, ln1_b, ln2_g, ln2_b, ln3_g, ln3_b]
    small_m = [m_in_norm_g, m_in_norm_b, m_conv_db, m_conv_norm_g, m_conv_norm_b, m_ssm_log_step, m_ssm_lambda_re,
               m_ssm_lambda_im, m_ssm_b_re, m_ssm_b_im, m_ssm_c_re, m_ssm_c_im, m_ssm_d, m_ln1_g, m_ln1_b, m_ln2_g,
               m_ln2_b, m_ln3_g, m_ln3_b]
    small_v = [v_in_norm_g, v_in_norm_b, v_conv_db, v_conv_norm_g, v_conv_norm_b, v_ssm_log_step, v_ssm_lambda_re,
               v_ssm_lambda_im, v_ssm_b_re, v_ssm_b_im, v_ssm_c_re, v_ssm_c_im, v_ssm_d, v_ln1_g, v_ln1_b, v_ln2_g,
               v_ln2_b, v_ln3_g, v_ln3_b]
    small_g = [dg0, db0, ddb, dng, dnb, g_lstep, g_lre, g_lim, g_bre, g_bim, g_cre, g_cim, g_d, dg1, db1, dg2, db2, dg3, db3]
    small_shapes = [w.shape for w in small_w]
    n_small_rows = _pack(small_w).shape[0]
    packed_g = _pack(small_g + [ddw, sq])
    summed = allreduce_two_level(packed_g, name="allreduce_small")
    small_rows = sum(_pack_rows(s) for s in small_shapes)
    dw_rows = _pack_rows((CONV_HALO, D))
    loss = 0.5 * summed[small_rows + dw_rows, 0] / D
    ddw_full = summed[small_rows:small_rows + dw_rows].reshape(CONV_HALO, D)
    g_dw = lax.dynamic_slice_in_dim(ddw_full, k_me * (D // N_CHIPS), D // N_CHIPS, axis=1)
    gs_packed = jnp.pad(summed[:small_rows], ((0, n_small_rows - small_rows), (0, 0)))

    res = {}

    def upd(nm, w, m, v, g_arr, row0=0):
        shp = w.shape
        w2, m2, v2 = (a.reshape(-1, shp[-1]) for a in (w, m, v))
        outs = adamw(w2, m2, v2, g_arr, row0, name="adamw_" + nm)
        res[nm] = tuple(o.reshape(shp) for o in outs)

    upd("w_conv_out", w_conv_out, m_w_conv_out, v_w_conv_out, g2, G2_CONV_OUT)
    upd("w_mix_out", w_mix_out, m_w_mix_out, v_w_mix_out, g2, G2_MIX_OUT)
    upd("xa_wq", xa_wq, m_xa_wq, v_xa_wq, g2, G2_WQ)
    upd("xa_wo", xa_wo, m_xa_wo, v_xa_wo, g2, G2_WO)
    upd("mlp_w_down", mlp_w_down, m_mlp_w_down, v_mlp_w_down, g1, G1_DOWN)
    upd("mlp_w_up", mlp_w_up, m_mlp_w_up, v_mlp_w_up, g1, G1_UP)
    upd("w_in", w_in, m_w_in, v_w_in, gIN)
    upd("xa_wkv", xa_wkv, m_xa_wkv, v_xa_wkv, gKV)
    upd("w_ssm_glu", w_ssm_glu, m_w_ssm_glu, v_w_ssm_glu, gGLU)
    pad_dw = lambda a: jnp.pad(a[0], ((0, CONV_HALO - CONV_K), (0, 0)))
    dw_outs = adamw(pad_dw(conv_dw), pad_dw(m_conv_dw), pad_dw(v_conv_dw), g_dw, 0, name="adamw_conv_dw")
    res["conv_dw"] = tuple(o[:CONV_K][None] for o in dw_outs)
    sm_outs = adamw(_pack(small_w), _pack(small_m), _pack(small_v), gs_packed, 0, name="adamw_small")
    sm_un = [_unpack(o, small_shapes) for o in sm_outs]
    for idx, nm in enumerate(small_names):
        res[nm] = tuple(sm_un[q][idx] for q in range(4))

    order = ["in_norm_g", "in_norm_b", "w_in", "conv_dw", "conv_db", "conv_norm_g", "conv_norm_b", "w_conv_out",
             "ssm_log_step", "ssm_lambda_re", "ssm_lambda_im", "ssm_b_re", "ssm_b_im", "ssm_c_re", "ssm_c_im", "ssm_d",
             "w_ssm_glu", "w_mix_out", "ln1_g", "ln1_b", "xa_wq", "xa_wkv", "xa_wo", "ln2_g", "ln2_b", "mlp_w_up",
             "mlp_w_down", "ln3_g", "ln3_b"]
    return (loss, gx[None], *[res[n][0] for n in order], *[res[n][1] for n in order],
            *[res[n][2] for n in order], *[res[n][3] for n in order])
```

```python
import functools
import math

import jax
import jax.numpy as jnp
from jax import lax
from jax.experimental import pallas as pl
from jax.experimental.pallas import tpu as pltpu

F32 = jnp.float32
BF16 = jnp.bfloat16
MESH = pl.DeviceIdType.MESH

D_MODEL = 1024
N_HEADS = 4
HEAD_DIM = D_MODEL // N_HEADS
CONV_K = 31
CONV_HALO = 32
D_SSM = 512
SSM_GROUPS = 32
SSM_GROUP = 16
SSM_STATE = 64
SSM_BLOCKS = 4
SSM_BLOCK_IN = D_SSM // SSM_BLOCKS
SSM_BLOCK_STATE = SSM_GROUPS * SSM_STATE // SSM_BLOCKS
D_FF = 4096
D_IN = 4608
LN_EPS = 1e-5
ALPHA = (2.0 * 1) ** 0.25
N_CHIPS = 4
N_DEV = 8
ADAM_LR, ADAM_B1, ADAM_B2, ADAM_EPS, ADAM_WD, ADAM_STEP = 0.001, 0.9, 0.999, 1e-08, 0.01, 10
VMEM_LIMIT_BYTES = 56 * 1024 * 1024


def _pick(dim, cands):
    for c in cands:
        if dim % c == 0:
            return c
    return dim


def _cparams(sem=None):
    return pltpu.CompilerParams(dimension_semantics=sem, vmem_limit_bytes=VMEM_LIMIT_BYTES)


def _sigmoid(x):
    return 1.0 / (1.0 + jnp.exp(-x))


_DIMS = {"nn": (((1,), (0,)), ((), ())), "nt": (((1,), (1,)), ((), ())), "tn": (((0,), (0,)), ((), ()))}


def matmul(a, b, *, mode, M, N, K, tm, tn, tk, a_spec, b_spec, out_specs, out_shapes, name,
           extras=(), extra_specs=(), epilogue=None, alias_buf=None, b_view=None, after=(), b_chunks=None):
    nk = K // tk
    ne = len(extras)
    no = len(out_shapes)
    na = (0 if alias_buf is None else 1) + len(after)
    dims = _DIMS[mode]

    def body(*refs):
        a_ref, b_ref = refs[0], refs[1]
        e_refs = refs[2:2 + ne]
        o_refs = refs[2 + ne + na:2 + ne + na + no]

        def finish(acc):
            outs = (acc,) if epilogue is None else epilogue(acc, *[r[...] for r in e_refs])
            for o, r in zip(outs, o_refs):
                r[...] = o.astype(r.dtype).reshape(r.shape)

        if b_chunks:
            kc = a_ref.shape[1] // b_chunks
            prod = None
            for q in range(b_chunks):
                part = lax.dot_general(a_ref[:, q * kc:(q + 1) * kc].astype(BF16), b_ref[q].astype(BF16), dims,
                                       preferred_element_type=F32)
                prod = part if prod is None else prod + part
        else:
            b_blk = b_ref[...] if b_view is None else b_ref[...].reshape(b_view)
            prod = lax.dot_general(a_ref[...].astype(BF16), b_blk.astype(BF16), dims, preferred_element_type=F32)
        if nk == 1:
            finish(prod)
        else:
            acc_ref = refs[-1]
            k = pl.program_id(2)

            @pl.when(k == 0)
            def _():
                acc_ref[...] = prod

            @pl.when(k > 0)
            def _():
                acc_ref[...] += prod

            @pl.when(k == nk - 1)
            def _():
                finish(acc_ref[...])

    in_specs = [pl.BlockSpec(*a_spec), pl.BlockSpec(*b_spec)] + [pl.BlockSpec(*s) for s in extra_specs]
    ins = [a, b, *extras]
    if alias_buf is not None:
        in_specs.append(pl.BlockSpec(memory_space=pl.ANY))
        ins.append(alias_buf)
    for dep in after:
        in_specs.append(pl.BlockSpec(memory_space=pl.ANY))
        ins.append(dep)
    res = pl.pallas_call(
        body,
        grid=(M // tm, N // tn, nk),
        in_specs=in_specs,
        out_specs=[pl.BlockSpec(*s) for s in out_specs],
        out_shape=out_shapes,
        scratch_shapes=[] if nk == 1 else [pltpu.VMEM((tm, tn), F32)],
        input_output_aliases={2 + ne: 0} if alias_buf is not None else {},
        compiler_params=_cparams(("parallel", "parallel", "arbitrary")),
        name=name,
    )(*ins)
    return res


def _mn(tm, tn):
    return ((tm, tn), lambda i, j, k: (i, j))


def mm_nn(a, b_arr, b_spec, N, *, name, tm=None, tn, tk, out_dtype=F32, extras=(), epilogue=None, out_dtypes=None,
          b_view=None, extra_specs=None):
    M, K = a.shape
    tm = tm or _pick(M, [1024, 512, 256, 128])
    dts = out_dtypes or [out_dtype]
    return matmul(a, b_arr, mode="nn", M=M, N=N, K=K, tm=tm, tn=tn, tk=tk,
                  a_spec=((tm, tk), lambda i, j, k: (i, k)), b_spec=b_spec, b_view=b_view,
                  out_specs=[_mn(tm, tn)] * len(dts), out_shapes=[jax.ShapeDtypeStruct((M, N), d) for d in dts],
                  extras=extras, extra_specs=extra_specs or [_mn(tm, tn)] * len(extras), epilogue=epilogue, name=name)


def mm_nt(a, b_arr, b_spec, N, *, name, tm=None, tn, tk, out_dtype=F32, extras=(), epilogue=None, out_dtypes=None,
          b_view=None, after=(), b_chunks=None):
    M, K = a.shape
    tm = tm or _pick(M, [1024, 512, 256, 128])
    dts = out_dtypes or [out_dtype]
    return matmul(a, b_arr, mode="nt", M=M, N=N, K=K, tm=tm, tn=tn, tk=tk, after=after, b_chunks=b_chunks,
                  a_spec=((tm, tk), lambda i, j, k: (i, k)), b_spec=b_spec, b_view=b_view,
                  out_specs=[_mn(tm, tn)] * len(dts), out_shapes=[jax.ShapeDtypeStruct((M, N), d) for d in dts],
                  extras=extras, extra_specs=[_mn(tm, tn)] * len(extras), epilogue=epilogue, name=name)


def mm_tn(a, b, *, name, tm, tn, tk=None, out_spec, out_shape, out_buf=None):
    K, M = a.shape
    N = b.shape[1]
    tk = tk or _pick(K, [2048, 1024, 512, 256, 128])
    return matmul(a, b, mode="tn", M=M, N=N, K=K, tm=tm, tn=tn, tk=tk,
                  a_spec=((tk, tm), lambda i, j, k: (k, i)), b_spec=((tk, tn), lambda i, j, k: (k, j)),
                  out_specs=[out_spec], out_shapes=[out_shape], alias_buf=out_buf, name=name)[0]


def _rows(tc, w, cb=0):
    return pl.BlockSpec((tc, w), lambda i: (i, cb))


def _const(shape):
    return pl.BlockSpec(shape, lambda i: tuple([0] * len(shape)))


def _ln_stats(r):
    mu = jnp.mean(r, axis=-1, keepdims=True)
    xc = r - mu
    var = jnp.mean(xc * xc, axis=-1, keepdims=True)
    rstd = lax.rsqrt(var + LN_EPS)
    return xc * rstd, rstd


def _rowsum8(v):
    tc, w = v.shape
    return jnp.sum(v.reshape(tc // 8, 8, w), axis=0)


def ln_fwd(x, g, b, *, name, res=None):
    T, D = x.shape
    tc = _pick(T, [512, 256, 128])
    has_res = res is not None

    def body(*refs):
        if has_res:
            x_ref, res_ref, g_ref, b_ref, r_ref, h_ref, hb_ref = refs
            r = ALPHA * res_ref[...] + x_ref[...]
            r_ref[...] = r
        else:
            x_ref, g_ref, b_ref, h_ref, hb_ref = refs
            r = x_ref[...]
        xhat, _ = _ln_stats(r)
        y = xhat * g_ref[...] + b_ref[...]
        h_ref[...] = y
        hb_ref[...] = y.astype(BF16)

    ins = [x] + ([res] if has_res else []) + [g.reshape(1, D), b.reshape(1, D)]
    in_specs = [_rows(tc, D)] * (2 if has_res else 1) + [_const((1, D))] * 2
    n_out = 3 if has_res else 2
    outs = pl.pallas_call(
        body, grid=(T // tc,), in_specs=in_specs, out_specs=[_rows(tc, D)] * n_out,
        out_shape=[jax.ShapeDtypeStruct((T, D), F32)] * (n_out - 1) + [jax.ShapeDtypeStruct((T, D), BF16)],
        compiler_params=_cparams(("arbitrary",)), name=name)(*ins)
    if has_res:
        return outs
    return (x,) + tuple(outs)


def ln_bwd(r, dy, g, *, name):
    T, D = r.shape
    tc = _pick(T, [512, 256, 128])
    nt = T // tc

    def body(r_ref, dy_ref, g_ref, dr_ref, drb_ref, dg_ref, db_ref, accg, accb):
        i = pl.program_id(0)

        @pl.when(i == 0)
        def _():
            accg[...] = jnp.zeros_like(accg)
            accb[...] = jnp.zeros_like(accb)

        xhat, rstd = _ln_stats(r_ref[...])
        dy = dy_ref[...]
        dxh = dy * g_ref[...]
        m1 = jnp.mean(dxh, axis=-1, keepdims=True)
        m2 = jnp.mean(dxh * xhat, axis=-1, keepdims=True)
        dr = rstd * (dxh - m1 - xhat * m2)
        dr_ref[...] = dr
        drb_ref[...] = dr.astype(BF16)
        accg[...] += _rowsum8(dy * xhat)
        accb[...] += _rowsum8(dy)

        @pl.when(i == nt - 1)
        def _():
            dg_ref[...] = jnp.sum(accg[...], axis=0, keepdims=True)
            db_ref[...] = jnp.sum(accb[...], axis=0, keepdims=True)

    return pl.pallas_call(
        body, grid=(nt,), in_specs=[_rows(tc, D), _rows(tc, D), _const((1, D))],
        out_specs=[_rows(tc, D), _rows(tc, D), _const((1, D)), _const((1, D))],
        out_shape=[jax.ShapeDtypeStruct((T, D), F32), jax.ShapeDtypeStruct((T, D), BF16),
                   jax.ShapeDtypeStruct((1, D), F32), jax.ShapeDtypeStruct((1, D), F32)],
        scratch_shapes=[pltpu.VMEM((8, D), F32), pltpu.VMEM((8, D), F32)],
        compiler_params=_cparams(("arbitrary",)), name=name)(r, dy, g.reshape(1, D))


def ln_loss_bwd(x, res, g, b, target, *, name):
    T, D = x.shape
    tc = _pick(T, [512, 256, 128])
    nt = T // tc

    def body(x_ref, res_ref, g_ref, b_ref, t_ref, dr_ref, drb_ref, dg_ref, db_ref, loss_ref, accg, accb, accl):
        i = pl.program_id(0)

        @pl.when(i == 0)
        def _():
            accg[...] = jnp.zeros_like(accg)
            accb[...] = jnp.zeros_like(accb)
            accl[...] = jnp.zeros_like(accl)

        r = ALPHA * res_ref[...] + x_ref[...]
        xhat, rstd = _ln_stats(r)
        e = xhat * g_ref[...] + b_ref[...] - t_ref[...]
        dy = e * (1.0 / D)
        dxh = dy * g_ref[...]
        m1 = jnp.mean(dxh, axis=-1, keepdims=True)
        m2 = jnp.mean(dxh * xhat, axis=-1, keepdims=True)
        dr = rstd * (dxh - m1 - xhat * m2)
        dr_ref[...] = dr
        drb_ref[...] = dr.astype(BF16)
        accg[...] += _rowsum8(dy * xhat)
        accb[...] += _rowsum8(dy)
        accl[...] += _rowsum8(e * e)

        @pl.when(i == nt - 1)
        def _():
            dg_ref[...] = jnp.sum(accg[...], axis=0, keepdims=True)
            db_ref[...] = jnp.sum(accb[...], axis=0, keepdims=True)
            s = jnp.sum(jnp.sum(accl[...], axis=0, keepdims=True), axis=1, keepdims=True)
            loss_ref[...] = jnp.broadcast_to(s, (1, 128))

    return pl.pallas_call(
        body, grid=(nt,), in_specs=[_rows(tc, D), _rows(tc, D), _const((1, D)), _const((1, D)), _rows(tc, D)],
        out_specs=[_rows(tc, D), _rows(tc, D), _const((1, D)), _const((1, D)), _const((1, 128))],
        out_shape=[jax.ShapeDtypeStruct((T, D), F32), jax.ShapeDtypeStruct((T, D), BF16),
                   jax.ShapeDtypeStruct((1, D), F32), jax.ShapeDtypeStruct((1, D), F32), jax.ShapeDtypeStruct((1, 128), F32)],
        scratch_shapes=[pltpu.VMEM((8, D), F32)] * 3,
        compiler_params=_cparams(("arbitrary",)), name=name)(x, res, g.reshape(1, D), b.reshape(1, D), target)


def _halo_prev(tc):
    per = tc // CONV_HALO
    return lambda i: jnp.maximum(i * per - 1, 0)


CONV_ROWS = 32
CONV_TAP_GROUP = 4
CONV_TILE_UNROLL = 4


def _fill_shifts(S, nrows):
    for b in range(1, 8):
        S[b, 0:nrows - 8, :] = S[0, b:b + nrows - 8, :]


def _tap_sum(S, w_ref, offs, r0, nrows):
    acc = None
    for k, o in enumerate(offs):
        a, b = divmod(o, 8)
        term = w_ref[k:k + 1, :] * S[b, pl.ds(pl.multiple_of(r0 + 8 * a, 8), nrows), :]
        acc = term if acc is None else acc + term
    return acc


def conv_fwd(p, dw, db, ng, nb, *, name):
    T = p.shape[0]
    D = D_MODEL
    tc = _pick(T, [256, 128])
    prev = _halo_prev(tc)
    off = CONV_HALO - (CONV_K - 1)
    offs = [off + k for k in range(CONV_K)]

    def body(val_ref, gate_ref, valp_ref, gatep_ref, dw_ref, db_ref, ng_ref, nb_ref, c_ref, act_ref, S):
        i = pl.program_id(0)
        u_prev = valp_ref[...] * _sigmoid(gatep_ref[...])
        S[0, 0:CONV_HALO, :] = jnp.where(i > 0, u_prev, 0.0)
        S[0, CONV_HALO:CONV_HALO + tc, :] = val_ref[...] * _sigmoid(gate_ref[...])
        _fill_shifts(S, CONV_HALO + tc)

        def rows(j, carry):
            r0 = pl.multiple_of(j * CONV_ROWS, CONV_ROWS)
            c_ref[pl.ds(r0, CONV_ROWS), :] = _tap_sum(S, dw_ref, offs, r0, CONV_ROWS) + db_ref[...]
            return carry

        lax.fori_loop(0, tc // CONV_ROWS, rows, 0)
        c = c_ref[...]
        xhat, _ = _ln_stats(c)
        cn = xhat * ng_ref[...] + nb_ref[...]
        act_ref[...] = (cn * _sigmoid(cn)).astype(BF16)

    return pl.pallas_call(
        body, grid=(T // tc,),
        in_specs=[_rows(tc, D, 0), _rows(tc, D, 1),
                  pl.BlockSpec((CONV_HALO, D), lambda i: (prev(i), 0)), pl.BlockSpec((CONV_HALO, D), lambda i: (prev(i), 1)),
                  _const((CONV_HALO, D)), _const((1, D)), _const((1, D)), _const((1, D))],
        out_specs=[_rows(tc, D), _rows(tc, D)],
        out_shape=[jax.ShapeDtypeStruct((T, D), F32), jax.ShapeDtypeStruct((T, D), BF16)],
        scratch_shapes=[pltpu.VMEM((8, CONV_HALO + tc, D), F32)],
        compiler_params=_cparams(("arbitrary",)), name=name)(p, p, p, p, dw, db, ng, nb)


def conv_bwd_norm(dact, c_pre, ng, nb, after, *, name):
    T, D = c_pre.shape
    tc = _pick(T, [512, 256, 128])
    nt = T // tc

    def body(da_ref, c_ref, ng_ref, nb_ref, after_ref, dc_ref, dng_ref, dnb_ref, ddb_ref, accg, accb, accd):
        i = pl.program_id(0)

        @pl.when(i == 0)
        def _():
            accg[...] = jnp.zeros_like(accg)
            accb[...] = jnp.zeros_like(accb)
            accd[...] = jnp.zeros_like(accd)

        xhat, rstd = _ln_stats(c_ref[...])
        cn = xhat * ng_ref[...] + nb_ref[...]
        s = _sigmoid(cn)
        dcn = da_ref[...] * (s * (1.0 + cn * (1.0 - s)))
        dxh = dcn * ng_ref[...]
        m1 = jnp.mean(dxh, axis=-1, keepdims=True)
        m2 = jnp.mean(dxh * xhat, axis=-1, keepdims=True)
        dc = rstd * (dxh - m1 - xhat * m2)
        dc_ref[...] = dc
        accg[...] += _rowsum8(dcn * xhat)
        accb[...] += _rowsum8(dcn)
        accd[...] += _rowsum8(dc)

        @pl.when(i == nt - 1)
        def _():
            dng_ref[...] = jnp.sum(accg[...], axis=0, keepdims=True)
            dnb_ref[...] = jnp.sum(accb[...], axis=0, keepdims=True)
            ddb_ref[...] = jnp.sum(accd[...], axis=0, keepdims=True)

    vec = jax.ShapeDtypeStruct((1, D), F32)
    return pl.pallas_call(
        body, grid=(nt,), in_specs=[_rows(tc, D), _rows(tc, D), _const((1, D)), _const((1, D)), ANY],
        out_specs=[_rows(tc, D), _const((1, D)), _const((1, D)), _const((1, D))],
        out_shape=[jax.ShapeDtypeStruct((T, D), F32), vec, vec, vec],
        scratch_shapes=[pltpu.VMEM((8, D), F32)] * 3,
        compiler_params=_cparams(("arbitrary",)), name=name)(dact, c_pre, ng, nb, after)


def conv_bwd_taps(dc, p, dw, du_ssm, dgates, *, name):
    T, D = dc.shape
    tc = _pick(T, [256, 128])
    nt = T // tc
    per = tc // CONV_HALO
    prev = _halo_prev(tc)
    last_halo = T // CONV_HALO - 1
    nxt = lambda i: jnp.minimum((i + 1) * per, last_halo)
    off = CONV_HALO - (CONV_K - 1)

    def body(dc_ref, dcn_ref, val_ref, gate_ref, valp_ref, gatep_ref, dw_ref, dus_ref, dg_ref, dvg_ref, ddw_ref,
             ext_u, ext_d, acc):
        i = pl.program_id(0)

        @pl.when(i == 0)
        def _():
            acc[...] = jnp.zeros_like(acc)

        dvg_ref[:, 2 * D:2 * D + D_SSM] = dus_ref[...]
        dvg_ref[:, 2 * D + D_SSM:D_IN] = dg_ref[...]

        u_prev = valp_ref[...] * _sigmoid(gatep_ref[...])
        ext_u[0, 0:CONV_HALO, :] = jnp.where(i > 0, u_prev, 0.0)
        ext_u[0, CONV_HALO:CONV_HALO + tc, :] = val_ref[...] * _sigmoid(gate_ref[...])
        ext_d[0, 0:tc, :] = dc_ref[...]
        ext_d[0, tc:tc + CONV_HALO, :] = jnp.where(i < nt - 1, dcn_ref[...], 0.0)
        _fill_shifts(ext_u, CONV_HALO + tc)
        _fill_shifts(ext_d, CONV_HALO + tc)

        def rows(j, carry):
            r0 = pl.multiple_of(j * CONV_ROWS, CONV_ROWS)
            sl = pl.ds(r0, CONV_ROWS)
            du = _tap_sum(ext_d, dw_ref, [CONV_K - 1 - k for k in range(CONV_K)], r0, CONV_ROWS)
            sg = _sigmoid(gate_ref[sl, :])
            dvg_ref[sl, 0:D] = (du * sg).astype(BF16)
            dvg_ref[sl, D:2 * D] = (du * val_ref[sl, :] * sg * (1.0 - sg)).astype(BF16)
            return carry

        lax.fori_loop(0, tc // CONV_ROWS, rows, 0)

        for k0 in range(0, CONV_K, CONV_TAP_GROUP):
            ks = list(range(k0, min(k0 + CONV_TAP_GROUP, CONV_K)))

            def taps(j, accs, ks=ks):
                out = list(accs)
                for t in range(CONV_TILE_UNROLL):
                    r0 = pl.multiple_of((j * CONV_TILE_UNROLL + t) * 8, 8)
                    dct = dc_ref[pl.ds(r0, 8), :]
                    for q, k in enumerate(ks):
                        a, b = divmod(off + k, 8)
                        out[q] = out[q] + dct * ext_u[b, pl.ds(pl.multiple_of(r0 + 8 * a, 8), 8), :]
                return tuple(out)

            accs = lax.fori_loop(0, tc // (8 * CONV_TILE_UNROLL), taps, tuple(jnp.zeros((8, D), F32) for _ in ks))
            for k, a_k in zip(ks, accs):
                acc[k] += a_k

        @pl.when(i == nt - 1)
        def _():
            ddw_ref[...] = jnp.zeros_like(ddw_ref)
            for k in range(CONV_K):
                ddw_ref[k:k + 1, :] = jnp.sum(acc[k], axis=0, keepdims=True)

    return pl.pallas_call(
        body, grid=(nt,),
        in_specs=[_rows(tc, D), pl.BlockSpec((CONV_HALO, D), lambda i: (nxt(i), 0)),
                  _rows(tc, D, 0), _rows(tc, D, 1),
                  pl.BlockSpec((CONV_HALO, D), lambda i: (prev(i), 0)), pl.BlockSpec((CONV_HALO, D), lambda i: (prev(i), 1)),
                  _const((CONV_HALO, D)), _rows(tc, D_SSM), _rows(tc, 2 * D)],
        out_specs=[_rows(tc, D_IN), _const((CONV_HALO, D))],
        out_shape=[jax.ShapeDtypeStruct((T, D_IN), BF16), jax.ShapeDtypeStruct((CONV_HALO, D), F32)],
        scratch_shapes=[pltpu.VMEM((8, CONV_HALO + tc, D), F32), pltpu.VMEM((8, CONV_HALO + tc, D), F32),
                        pltpu.VMEM((CONV_K, 8, D), F32)],
        compiler_params=_cparams(("arbitrary",)), name=name)(dc, dc, p, p, p, p, dw, du_ssm, dgates)


GATE_A0 = (2 * D_MODEL + D_SSM) // 512
GATE_B0 = GATE_A0 + 2


def merge_fwd(p, ya, z, *, name):
    T = p.shape[0]
    D = D_MODEL
    tc = _pick(T, [512, 256, 128])
    W = 512

    def body(ga_ref, gb_ref, ya_ref, z1_ref, z2_ref, o_ref):
        yb = z1_ref[...] * _sigmoid(z2_ref[...])
        o_ref[...] = (_sigmoid(ga_ref[...]) * ya_ref[...] + _sigmoid(gb_ref[...]) * yb).astype(BF16)

    return pl.pallas_call(
        body, grid=(T // tc, D // W),
        in_specs=[pl.BlockSpec((tc, W), lambda i, j: (i, GATE_A0 + j)), pl.BlockSpec((tc, W), lambda i, j: (i, GATE_B0 + j)),
                  pl.BlockSpec((tc, W), lambda i, j: (i, j)), pl.BlockSpec((tc, W), lambda i, j: (i, j)),
                  pl.BlockSpec((tc, W), lambda i, j: (i, D // W + j))],
        out_specs=pl.BlockSpec((tc, W), lambda i, j: (i, j)),
        out_shape=jax.ShapeDtypeStruct((T, D), BF16),
        compiler_params=_cparams(("arbitrary", "arbitrary")), name=name)(p, p, ya, z, z)


def merge_bwd(dm, p, ya, z, *, name):
    T = p.shape[0]
    D = D_MODEL
    tc = _pick(T, [256, 128])
    W = 512
    nb = D // W

    def body(dm_ref, ga0_ref, ga1_ref, gb0_ref, gb1_ref, ya_ref, z_ref, dya_ref, dg_ref, dz_ref):
        for j, (ga_ref, gb_ref) in enumerate(((ga0_ref, gb0_ref), (ga1_ref, gb1_ref))):
            c0 = slice(j * W, (j + 1) * W)
            c1 = slice(D + j * W, D + (j + 1) * W)
            dm = dm_ref[:, c0]
            sa = _sigmoid(ga_ref[...])
            sb = _sigmoid(gb_ref[...])
            s2 = _sigmoid(z_ref[:, c1])
            z1 = z_ref[:, c0]
            yb = z1 * s2
            dya_ref[:, c0] = (dm * sa).astype(BF16)
            dg_ref[:, c0] = (dm * ya_ref[:, c0] * sa * (1.0 - sa)).astype(BF16)
            dg_ref[:, c1] = (dm * yb * sb * (1.0 - sb)).astype(BF16)
            dyb = dm * sb
            dz_ref[:, c0] = (dyb * s2).astype(BF16)
            dz_ref[:, c1] = (dyb * z1 * s2 * (1.0 - s2)).astype(BF16)

    gate = lambda cb: pl.BlockSpec((tc, W), lambda i: (i, cb))
    return pl.pallas_call(
        body, grid=(T // tc,),
        in_specs=[_rows(tc, D), gate(GATE_A0), gate(GATE_A0 + 1), gate(GATE_B0), gate(GATE_B0 + 1), _rows(tc, D),
                  _rows(tc, 2 * D)],
        out_specs=[_rows(tc, D), _rows(tc, 2 * D), _rows(tc, 2 * D)],
        out_shape=[jax.ShapeDtypeStruct((T, D), BF16), jax.ShapeDtypeStruct((T, 2 * D), BF16),
                   jax.ShapeDtypeStruct((T, 2 * D), BF16)],
        compiler_params=_cparams(("arbitrary",)), name=name)(dm, p, p, p, p, ya, z)


def _scan_block(src_r, src_i, dst_r, dst_i, car_r, car_i, pw_r, pw_i, cw_r, cw_i, ntiles, reverse, extra=None):
    W = src_r.shape[1]
    rows = lax.broadcasted_iota(jnp.int32, (8, W), 0)
    steps = []
    for d, pr in ((1, 0), (2, 1), (4, 3)):
        valid = rows < 8 - d if reverse else rows >= d
        steps.append((d, jnp.where(valid, jnp.broadcast_to(pw_r[pr:pr + 1, :], (8, W)), 0.0),
                      jnp.where(valid, jnp.broadcast_to(pw_i[pr:pr + 1, :], (8, W)), 0.0)))
    cw_r, cw_i = cw_r[...], cw_i[...]

    def tile(jj, carry):
        j = ntiles - 1 - jj if reverse else jj
        sl = pl.ds(pl.multiple_of(j * 8, 8), 8)
        xr, xi = src_r[sl, :], src_i[sl, :]
        for d, lr, li in steps:
            sr = pltpu.roll(xr, 8 - d if reverse else d, 0)
            si = pltpu.roll(xi, 8 - d if reverse else d, 0)
            xr, xi = xr + lr * sr - li * si, xi + lr * si + li * sr
        cr, ci = car_r[...], car_i[...]
        xr, xi = xr + cw_r * cr - cw_i * ci, xi + cw_r * ci + cw_i * cr
        dst_r[sl, :] = xr
        dst_i[sl, :] = xi
        edge = 0 if reverse else 7
        car_r[...] = jnp.broadcast_to(xr[edge:edge + 1, :], (8, W))
        car_i[...] = jnp.broadcast_to(xi[edge:edge + 1, :], (8, W))
        if extra is not None:
            carry = extra(j, xr, xi, carry)
        return carry

    return tile


def ssm_fwd(p, Br, Bi, Cr, Ci, pw_r, pw_i, dvec, *, name):
    T = p.shape[0]
    tt = _pick(T, [512, 256, 128])
    nt = T // tt
    WI, WS = SSM_BLOCK_IN, SSM_BLOCK_STATE
    u0 = 2 * D_MODEL // WI

    def body(u_ref, br_ref, bi_ref, cr_ref, ci_ref, pwr_ref, pwi_ref, d_ref, xr_ref, xi_ref, y_ref, bur, bui, car_r, car_i):
        i = pl.program_id(1)

        @pl.when(i == 0)
        def _():
            car_r[...] = jnp.zeros_like(car_r)
            car_i[...] = jnp.zeros_like(car_i)

        u = u_ref[...]
        ub = u.astype(BF16)
        bur[...] = jnp.dot(ub, br_ref[...].astype(BF16), preferred_element_type=F32)
        bui[...] = jnp.dot(ub, bi_ref[...].astype(BF16), preferred_element_type=F32)
        tile = _scan_block(bur, bui, xr_ref, xi_ref, car_r, car_i, pwr_ref, pwi_ref, pwr_ref, pwi_ref, tt // 8, False)
        lax.fori_loop(0, tt // 8, tile, 0)
        y = (jnp.dot(xr_ref[...].astype(BF16), cr_ref[...].astype(BF16), preferred_element_type=F32)
             - jnp.dot(xi_ref[...].astype(BF16), ci_ref[...].astype(BF16), preferred_element_type=F32)
             + d_ref[...] * u)
        y_ref[...] = y.astype(BF16)

    return pl.pallas_call(
        body, grid=(SSM_BLOCKS, nt),
        in_specs=[pl.BlockSpec((tt, WI), lambda b, i: (i, u0 + b)),
                  pl.BlockSpec((None, WI, WS), lambda b, i: (b, 0, 0)), pl.BlockSpec((None, WI, WS), lambda b, i: (b, 0, 0)),
                  pl.BlockSpec((None, WS, WI), lambda b, i: (b, 0, 0)), pl.BlockSpec((None, WS, WI), lambda b, i: (b, 0, 0)),
                  pl.BlockSpec((8, WS), lambda b, i: (0, b)), pl.BlockSpec((8, WS), lambda b, i: (0, b)),
                  pl.BlockSpec((1, WI), lambda b, i: (0, b))],
        out_specs=[pl.BlockSpec((tt, WS), lambda b, i: (i, b)), pl.BlockSpec((tt, WS), lambda b, i: (i, b)),
                   pl.BlockSpec((tt, WI), lambda b, i: (i, b))],
        out_shape=[jax.ShapeDtypeStruct((T, SSM_BLOCKS * WS), F32)] * 2 + [jax.ShapeDtypeStruct((T, D_SSM), BF16)],
        scratch_shapes=[pltpu.VMEM((tt, WS), F32), pltpu.VMEM((tt, WS), F32), pltpu.VMEM((8, WS), F32), pltpu.VMEM((8, WS), F32)],
        compiler_params=_cparams(("arbitrary", "arbitrary")), name=name)(p, Br, Bi, Cr, Ci, pw_r, pw_i, dvec)


def ssm_bwd(dy, p, xr, xi, Br, Bi, Cr, Ci, pwc_r, pwc_i, cwc_r, cwc_i, dvec, *, name):
    T = p.shape[0]
    tt = _pick(T, [512, 256, 128])
    nt = T // tt
    WI, WS = SSM_BLOCK_IN, SSM_BLOCK_STATE
    u0 = 2 * D_MODEL // WI
    tb = lambda i: nt - 1 - i
    xprev = lambda i: jnp.maximum(tb(i) * (tt // 8) - 1, 0)
    tn_dims = _DIMS["tn"]
    nt_dims = _DIMS["nt"]

    def body(dy_ref, u_ref, xr_ref, xi_ref, xpr_ref, xpi_ref, br_ref, bi_ref, cr_ref, ci_ref, pwr_ref, pwi_ref,
             cwr_ref, cwi_ref, d_ref,
             du_ref, dbr_ref, dbi_ref, dcr_ref, dci_ref, dar_ref, dai_ref, dd_ref,
             gr, gi, ext_r, ext_i, car_r, car_i):
        i = pl.program_id(1)

        @pl.when(i == 0)
        def _():
            car_r[...] = jnp.zeros_like(car_r)
            car_i[...] = jnp.zeros_like(car_i)
            dbr_ref[...] = jnp.zeros_like(dbr_ref)
            dbi_ref[...] = jnp.zeros_like(dbi_ref)
            dcr_ref[...] = jnp.zeros_like(dcr_ref)
            dci_ref[...] = jnp.zeros_like(dci_ref)
            dar_ref[...] = jnp.zeros_like(dar_ref)
            dai_ref[...] = jnp.zeros_like(dai_ref)
            dd_ref[...] = jnp.zeros_like(dd_ref)

        dy = dy_ref[...]
        dyb = dy.astype(BF16)
        u = u_ref[...]
        ub = u.astype(BF16)
        gr[...] = lax.dot_general(dyb, cr_ref[...].astype(BF16), nt_dims, preferred_element_type=F32)
        gi[...] = -lax.dot_general(dyb, ci_ref[...].astype(BF16), nt_dims, preferred_element_type=F32)
        first = tb(i) == 0
        ext_r[0:8, :] = jnp.where(first, 0.0, xpr_ref[...])
        ext_i[0:8, :] = jnp.where(first, 0.0, xpi_ref[...])
        ext_r[8:8 + tt, :] = xr_ref[...]
        ext_i[8:8 + tt, :] = xi_ref[...]
        rows = lax.broadcasted_iota(jnp.int32, (8, WS), 0)

        def lam_grad(j, g_r, g_i, carry):
            a_r, a_i = carry
            cur = pl.ds(pl.multiple_of(j * 8 + 8, 8), 8)
            prv = pl.ds(pl.multiple_of(j * 8, 8), 8)
            xc_r, xc_i = ext_r[cur, :], ext_i[cur, :]
            xl_r, xl_i = ext_r[prv, :], ext_i[prv, :]
            xp_r = jnp.where(rows == 0, jnp.broadcast_to(xl_r[7:8, :], (8, WS)), pltpu.roll(xc_r, 1, 0))
            xp_i = jnp.where(rows == 0, jnp.broadcast_to(xl_i[7:8, :], (8, WS)), pltpu.roll(xc_i, 1, 0))
            return (a_r + g_r * xp_r + g_i * xp_i, a_i + g_i * xp_r - g_r * xp_i)

        tile = _scan_block(gr, gi, gr, gi, car_r, car_i, pwr_ref, pwi_ref, cwr_ref, cwi_ref, tt // 8, True, extra=lam_grad)
        z8 = jnp.zeros((8, WS), F32)
        a_r, a_i = lax.fori_loop(0, tt // 8, tile, (z8, z8))
        dar_ref[...] += a_r
        dai_ref[...] += a_i
        grb = gr[...].astype(BF16)
        gib = gi[...].astype(BF16)
        dbr_ref[...] += lax.dot_general(ub, grb, tn_dims, preferred_element_type=F32)
        dbi_ref[...] += lax.dot_general(ub, gib, tn_dims, preferred_element_type=F32)
        dcr_ref[...] += lax.dot_general(xr_ref[...].astype(BF16), dyb, tn_dims, preferred_element_type=F32)
        dci_ref[...] -= lax.dot_general(xi_ref[...].astype(BF16), dyb, tn_dims, preferred_element_type=F32)
        du = (lax.dot_general(grb, br_ref[...].astype(BF16), nt_dims, preferred_element_type=F32)
              + lax.dot_general(gib, bi_ref[...].astype(BF16), nt_dims, preferred_element_type=F32)
              + d_ref[...] * dy)
        du_ref[...] = du.astype(BF16)
        dd_ref[...] += _rowsum8(dy * u)

    wspec = lambda shp: pl.BlockSpec((None,) + shp, lambda b, i: (b, 0, 0))
    return pl.pallas_call(
        body, grid=(SSM_BLOCKS, nt),
        in_specs=[pl.BlockSpec((tt, WI), lambda b, i: (tb(i), b)),
                  pl.BlockSpec((tt, WI), lambda b, i: (tb(i), u0 + b)),
                  pl.BlockSpec((tt, WS), lambda b, i: (tb(i), b)), pl.BlockSpec((tt, WS), lambda b, i: (tb(i), b)),
                  pl.BlockSpec((8, WS), lambda b, i: (xprev(i), b)), pl.BlockSpec((8, WS), lambda b, i: (xprev(i), b)),
                  wspec((WI, WS)), wspec((WI, WS)), wspec((WS, WI)), wspec((WS, WI)),
                  pl.BlockSpec((8, WS), lambda b, i: (0, b)), pl.BlockSpec((8, WS), lambda b, i: (0, b)),
                  pl.BlockSpec((8, WS), lambda b, i: (0, b)), pl.BlockSpec((8, WS), lambda b, i: (0, b)),
                  pl.BlockSpec((1, WI), lambda b, i: (0, b))],
        out_specs=[pl.BlockSpec((tt, WI), lambda b, i: (tb(i), b)),
                   wspec((WI, WS)), wspec((WI, WS)), wspec((WS, WI)), wspec((WS, WI)),
                   pl.BlockSpec((8, WS), lambda b, i: (0, b)), pl.BlockSpec((8, WS), lambda b, i: (0, b)),
                   pl.BlockSpec((8, WI), lambda b, i: (0, b))],
        out_shape=[jax.ShapeDtypeStruct((T, D_SSM), BF16),
                   jax.ShapeDtypeStruct((SSM_BLOCKS, WI, WS), F32), jax.ShapeDtypeStruct((SSM_BLOCKS, WI, WS), F32),
                   jax.ShapeDtypeStruct((SSM_BLOCKS, WS, WI), F32), jax.ShapeDtypeStruct((SSM_BLOCKS, WS, WI), F32),
                   jax.ShapeDtypeStruct((8, SSM_BLOCKS * WS), F32), jax.ShapeDtypeStruct((8, SSM_BLOCKS * WS), F32),
                   jax.ShapeDtypeStruct((8, D_SSM), F32)],
        scratch_shapes=[pltpu.VMEM((tt, WS), F32), pltpu.VMEM((tt, WS), F32),
                        pltpu.VMEM((tt + 8, WS), F32), pltpu.VMEM((tt + 8, WS), F32),
                        pltpu.VMEM((8, WS), F32), pltpu.VMEM((8, WS), F32)],
        compiler_params=_cparams(("arbitrary", "arbitrary")), name=name,
    )(dy, p, xr, xi, xr, xi, Br, Bi, Cr, Ci, pwc_r, pwc_i, cwc_r, cwc_i, dvec)


SSM_SEGS = 8


def seg_perm(a, tt):
    T, C = a.shape
    return a.reshape(T // tt, SSM_SEGS, tt // SSM_SEGS, C).transpose(0, 2, 1, 3).reshape(T, C)


def seg_unperm(a, tt):
    T, C = a.shape
    return a.reshape(T // tt, tt // SSM_SEGS, SSM_SEGS, C).transpose(0, 2, 1, 3).reshape(T, C)


def _ssm_tt(T):
    return _pick(T, [512, 256, 128])


def _seg_gather(src_ref, dst_ref, sl):
    for j in range(sl):
        dst_ref[8 * j:8 * j + 8, :] = src_ref[pl.ds(j, SSM_SEGS, stride=sl), :]


def _seg_scatter(val, dst_ref, sl):
    for j in range(sl):
        dst_ref[pl.ds(j, SSM_SEGS, stride=sl), :] = val[8 * j:8 * j + 8, :]


def _seg_tables(ar_ref, ai_ref, conj, pb_r, pb_i, pw_r, pw_i, cw_r, cw_i, sl):
    W = ar_ref.shape[1]
    lr = jnp.broadcast_to(ar_ref[...], (8, W))
    li = jnp.broadcast_to(ai_ref[...], (8, W))
    if conj:
        li = -li

    def power(j, cur):
        cr, ci = cur
        pb_r[j] = cr
        pb_i[j] = ci
        return cr * lr - ci * li, cr * li + ci * lr

    lax.fori_loop(0, sl, power, (lr, li))
    br, bi = pb_r[sl - 1], pb_i[sl - 1]
    rows = lax.broadcasted_iota(jnp.int32, (8, W), 0)
    cr, ci = br, bi
    tr, ti = jnp.zeros((8, W), F32), jnp.zeros((8, W), F32)
    ur, ui = tr, ti
    for r in range(8):
        tr, ti = jnp.where(rows == r, cr, tr), jnp.where(rows == r, ci, ti)
        ur, ui = jnp.where(rows == 7 - r, cr, ur), jnp.where(rows == 7 - r, ci, ui)
        cr, ci = cr * br - ci * bi, cr * bi + ci * br
    pw_r[...] = tr
    pw_i[...] = ti
    cw_r[...] = ur
    cw_i[...] = ui


def ssm_seg_fwd(p, Br, Bi, Cr, Ci, ar, ai, dvec, *, name):
    T = p.shape[0]
    tt = _ssm_tt(T)
    nt = T // tt
    sl = tt // SSM_SEGS
    WI, WS = SSM_BLOCK_IN, SSM_BLOCK_STATE
    u0 = 2 * D_MODEL // WI

    def body(u_ref, br_ref, bi_ref, cr_ref, ci_ref, ar_ref, ai_ref, d_ref, xr_ref, xi_ref, y_ref,
             bur, bui, useg, ynat, pb_r, pb_i, pw_r, pw_i, cw_r, cw_i, end_r, end_i, car_r, car_i):
        i = pl.program_id(1)

        @pl.when(i == 0)
        def _():
            _seg_tables(ar_ref, ai_ref, False, pb_r, pb_i, pw_r, pw_i, cw_r, cw_i, sl)
            car_r[...] = jnp.zeros_like(car_r)
            car_i[...] = jnp.zeros_like(car_i)

        _seg_gather(u_ref, useg, sl)
        u = useg[...]
        ub = u.astype(BF16)
        bur[...] = jnp.dot(ub, br_ref[...].astype(BF16), preferred_element_type=F32)
        bui[...] = jnp.dot(ub, bi_ref[...].astype(BF16), preferred_element_type=F32)
        lr = jnp.broadcast_to(ar_ref[...], (8, WS))
        li = jnp.broadcast_to(ai_ref[...], (8, WS))

        def step(j, st):
            sr, si = st
            rw = pl.ds(pl.multiple_of(j * 8, 8), 8)
            nr = lr * sr - li * si + bur[rw, :]
            ni = lr * si + li * sr + bui[rw, :]
            xr_ref[rw, :] = nr
            xi_ref[rw, :] = ni
            return nr, ni

        z8 = jnp.zeros((8, WS), F32)
        end_r[...], end_i[...] = lax.fori_loop(0, sl, step, (z8, z8))
        old_r, old_i = car_r[...], car_i[...]
        _scan_block(end_r, end_i, end_r, end_i, car_r, car_i, pw_r, pw_i, pw_r, pw_i, 1, False)(0, 0)
        rows = lax.broadcasted_iota(jnp.int32, (8, WS), 0)
        s_r = jnp.where(rows == 0, old_r, pltpu.roll(end_r[...], 1, 0))
        s_i = jnp.where(rows == 0, old_i, pltpu.roll(end_i[...], 1, 0))

        def fix(j, c):
            rw = pl.ds(pl.multiple_of(j * 8, 8), 8)
            pr, pi = pb_r[j], pb_i[j]
            xr_ref[rw, :] = xr_ref[rw, :] + pr * s_r - pi * s_i
            xi_ref[rw, :] = xi_ref[rw, :] + pr * s_i + pi * s_r
            return c

        lax.fori_loop(0, sl, fix, 0)
        y = (jnp.dot(xr_ref[...].astype(BF16), cr_ref[...].astype(BF16), preferred_element_type=F32)
             - jnp.dot(xi_ref[...].astype(BF16), ci_ref[...].astype(BF16), preferred_element_type=F32)
             + d_ref[...] * u)
        _seg_scatter(y, ynat, sl)
        y_ref[...] = ynat[...].astype(BF16)

    wspec = lambda shp: pl.BlockSpec((None,) + shp, lambda b, i: (b, 0, 0))
    vec = lambda w: pl.BlockSpec((1, w), lambda b, i: (0, b))
    tile8 = pltpu.VMEM((8, WS), F32)
    return pl.pallas_call(
        body, grid=(SSM_BLOCKS, nt),
        in_specs=[pl.BlockSpec((tt, WI), lambda b, i: (i, u0 + b)), wspec((WI, WS)), wspec((WI, WS)), wspec((WS, WI)),
                  wspec((WS, WI)), vec(WS), vec(WS), vec(WI)],
        out_specs=[pl.BlockSpec((tt, WS), lambda b, i: (i, b)), pl.BlockSpec((tt, WS), lambda b, i: (i, b)),
                   pl.BlockSpec((tt, WI), lambda b, i: (i, b))],
        out_shape=[jax.ShapeDtypeStruct((T, SSM_BLOCKS * WS), F32)] * 2 + [jax.ShapeDtypeStruct((T, D_SSM), BF16)],
        scratch_shapes=[pltpu.VMEM((tt, WS), F32), pltpu.VMEM((tt, WS), F32),
                        pltpu.VMEM((tt, WI), F32), pltpu.VMEM((tt, WI), F32),
                        pltpu.VMEM((sl, 8, WS), F32), pltpu.VMEM((sl, 8, WS), F32)] + [tile8] * 8,
        compiler_params=_cparams(("arbitrary", "arbitrary")), name=name)(p, Br, Bi, Cr, Ci, ar, ai, dvec)


def ssm_seg_bwd(dy, u, xr, xi, Br, Bi, Cr, Ci, ar, ai, dvec, *, name):
    T = u.shape[0]
    tt = _ssm_tt(T)
    nt = T // tt
    sl = tt // SSM_SEGS
    WI, WS = SSM_BLOCK_IN, SSM_BLOCK_STATE
    u0 = 2 * D_MODEL // WI
    tb = lambda i: nt - 1 - i
    xprev = lambda i: jnp.maximum(tb(i) * (tt // 8) - 1, 0)
    tn_dims = _DIMS["tn"]
    nt_dims = _DIMS["nt"]

    def body(dyn_ref, un_ref, xr_ref, xi_ref, xpr_ref, xpi_ref, br_ref, bi_ref, cr_ref, ci_ref, ar_ref, ai_ref, d_ref,
             du_ref, dbr_ref, dbi_ref, dcr_ref, dci_ref, dar_ref, dai_ref, dd_ref,
             gr, gi, ext_r, ext_i, dy_ref, u_ref, dunat, pb_r, pb_i, pw_r, pw_i, cw_r, cw_i, end_r, end_i, car_r, car_i):
        _seg_gather(dyn_ref, dy_ref, sl)
        _seg_gather(un_ref, u_ref, sl)
        i = pl.program_id(1)

        @pl.when(i == 0)
        def _():
            _seg_tables(ar_ref, ai_ref, True, pb_r, pb_i, pw_r, pw_i, cw_r, cw_i, sl)
            car_r[...] = jnp.zeros_like(car_r)
            car_i[...] = jnp.zeros_like(car_i)
            dbr_ref[...] = jnp.zeros_like(dbr_ref)
            dbi_ref[...] = jnp.zeros_like(dbi_ref)
            dcr_ref[...] = jnp.zeros_like(dcr_ref)
            dci_ref[...] = jnp.zeros_like(dci_ref)
            dar_ref[...] = jnp.zeros_like(dar_ref)
            dai_ref[...] = jnp.zeros_like(dai_ref)
            dd_ref[...] = jnp.zeros_like(dd_ref)

        dy = dy_ref[...]
        dyb = dy.astype(BF16)
        u = u_ref[...]
        ub = u.astype(BF16)
        gr[...] = lax.dot_general(dyb, cr_ref[...].astype(BF16), nt_dims, preferred_element_type=F32)
        gi[...] = -lax.dot_general(dyb, ci_ref[...].astype(BF16), nt_dims, preferred_element_type=F32)
        lr = jnp.broadcast_to(ar_ref[...], (8, WS))
        li = -jnp.broadcast_to(ai_ref[...], (8, WS))
        rows = lax.broadcasted_iota(jnp.int32, (8, WS), 0)

        def step(jj, st):
            sr, si = st
            rw = pl.ds(pl.multiple_of((sl - 1 - jj) * 8, 8), 8)
            nr = lr * sr - li * si + gr[rw, :]
            ni = lr * si + li * sr + gi[rw, :]
            gr[rw, :] = nr
            gi[rw, :] = ni
            return nr, ni

        z8 = jnp.zeros((8, WS), F32)
        end_r[...], end_i[...] = lax.fori_loop(0, sl, step, (z8, z8))
        old_r, old_i = car_r[...], car_i[...]
        _scan_block(end_r, end_i, end_r, end_i, car_r, car_i, pw_r, pw_i, cw_r, cw_i, 1, True)(0, 0)
        s_r = jnp.where(rows == 7, old_r, pltpu.roll(end_r[...], 7, 0))
        s_i = jnp.where(rows == 7, old_i, pltpu.roll(end_i[...], 7, 0))
        first = tb(i) == 0
        last_r, last_i = xr_ref[tt - 8:tt, :], xi_ref[tt - 8:tt, :]
        pv_r = jnp.where(first, 0.0, xpr_ref[...])
        pv_i = jnp.where(first, 0.0, xpi_ref[...])
        ext_r[0:8, :] = jnp.where(rows == 0, jnp.broadcast_to(pv_r[7:8, :], (8, WS)), pltpu.roll(last_r, 1, 0))
        ext_i[0:8, :] = jnp.where(rows == 0, jnp.broadcast_to(pv_i[7:8, :], (8, WS)), pltpu.roll(last_i, 1, 0))
        ext_r[8:8 + tt, :] = xr_ref[...]
        ext_i[8:8 + tt, :] = xi_ref[...]

        def fix(j, acc):
            a_r, a_i = acc
            rw = pl.ds(pl.multiple_of(j * 8, 8), 8)
            pr, pi = pb_r[sl - 1 - j], pb_i[sl - 1 - j]
            g_r = gr[rw, :] + pr * s_r - pi * s_i
            g_i = gi[rw, :] + pr * s_i + pi * s_r
            gr[rw, :] = g_r
            gi[rw, :] = g_i
            xp_r, xp_i = ext_r[rw, :], ext_i[rw, :]
            return a_r + g_r * xp_r + g_i * xp_i, a_i + g_i * xp_r - g_r * xp_i

        a_r, a_i = lax.fori_loop(0, sl, fix, (z8, z8))
        dar_ref[...] += a_r
        dai_ref[...] += a_i
        grb = gr[...].astype(BF16)
        gib = gi[...].astype(BF16)
        dbr_ref[...] += lax.dot_general(ub, grb, tn_dims, preferred_element_type=F32)
        dbi_ref[...] += lax.dot_general(ub, gib, tn_dims, preferred_element_type=F32)
        dcr_ref[...] += lax.dot_general(xr_ref[...].astype(BF16), dyb, tn_dims, preferred_element_type=F32)
        dci_ref[...] -= lax.dot_general(xi_ref[...].astype(BF16), dyb, tn_dims, preferred_element_type=F32)
        du = (lax.dot_general(grb, br_ref[...].astype(BF16), nt_dims, preferred_element_type=F32)
              + lax.dot_general(gib, bi_ref[...].astype(BF16), nt_dims, preferred_element_type=F32)
              + d_ref[...] * dy)
        _seg_scatter(du, dunat, sl)
        du_ref[...] = dunat[...].astype(BF16)
        dd_ref[...] += _rowsum8(dy * u)

    wspec = lambda shp: pl.BlockSpec((None,) + shp, lambda b, i: (b, 0, 0))
    vec = lambda w: pl.BlockSpec((1, w), lambda b, i: (0, b))
    tile8 = pltpu.VMEM((8, WS), F32)
    return pl.pallas_call(
        body, grid=(SSM_BLOCKS, nt),
        in_specs=[pl.BlockSpec((tt, WI), lambda b, i: (tb(i), b)), pl.BlockSpec((tt, WI), lambda b, i: (tb(i), u0 + b)),
                  pl.BlockSpec((tt, WS), lambda b, i: (tb(i), b)), pl.BlockSpec((tt, WS), lambda b, i: (tb(i), b)),
                  pl.BlockSpec((8, WS), lambda b, i: (xprev(i), b)), pl.BlockSpec((8, WS), lambda b, i: (xprev(i), b)),
                  wspec((WI, WS)), wspec((WI, WS)), wspec((WS, WI)), wspec((WS, WI)), vec(WS), vec(WS), vec(WI)],
        out_specs=[pl.BlockSpec((tt, WI), lambda b, i: (tb(i), b)),
                   wspec((WI, WS)), wspec((WI, WS)), wspec((WS, WI)), wspec((WS, WI)),
                   pl.BlockSpec((8, WS), lambda b, i: (0, b)), pl.BlockSpec((8, WS), lambda b, i: (0, b)),
                   pl.BlockSpec((8, WI), lambda b, i: (0, b))],
        out_shape=[jax.ShapeDtypeStruct((T, D_SSM), BF16),
                   jax.ShapeDtypeStruct((SSM_BLOCKS, WI, WS), F32), jax.ShapeDtypeStruct((SSM_BLOCKS, WI, WS), F32),
                   jax.ShapeDtypeStruct((SSM_BLOCKS, WS, WI), F32), jax.ShapeDtypeStruct((SSM_BLOCKS, WS, WI), F32),
                   jax.ShapeDtypeStruct((8, SSM_BLOCKS * WS), F32), jax.ShapeDtypeStruct((8, SSM_BLOCKS * WS), F32),
                   jax.ShapeDtypeStruct((8, D_SSM), F32)],
        scratch_shapes=[pltpu.VMEM((tt, WS), F32), pltpu.VMEM((tt, WS), F32),
                        pltpu.VMEM((tt + 8, WS), F32), pltpu.VMEM((tt + 8, WS), F32),
                        pltpu.VMEM((tt, WI), F32), pltpu.VMEM((tt, WI), F32), pltpu.VMEM((tt, WI), F32),
                        pltpu.VMEM((sl, 8, WS), F32), pltpu.VMEM((sl, 8, WS), F32)] + [tile8] * 8,
        compiler_params=_cparams(("arbitrary", "arbitrary")), name=name,
    )(dy, u, xr, xi, xr, xi, Br, Bi, Cr, Ci, ar, ai, dvec)


def _ssm_discretise(log_step, lam_re, lam_im, b_re, b_im):
    step = jnp.exp(log_step)[:, None]
    mag = jnp.exp(lam_re * step)
    ar = mag * jnp.cos(lam_im * step)
    ai = mag * jnp.sin(lam_im * step)
    den = lam_re * lam_re + lam_im * lam_im
    nr = ar - 1.0
    cr = (nr * lam_re + ai * lam_im) / den
    ci = (ai * lam_re - nr * lam_im) / den
    bbr = cr[..., None] * b_re - ci[..., None] * b_im
    bbi = cr[..., None] * b_im + ci[..., None] * b_re
    return ar, ai, bbr, bbi


def _blockdiag_in(bb):
    t = jnp.transpose(bb, (0, 2, 1)).reshape(SSM_BLOCKS, 8, SSM_GROUP, SSM_STATE)
    eye = jnp.eye(8, dtype=bb.dtype)
    return (t[:, :, :, None, :] * eye[None, :, None, :, None]).reshape(SSM_BLOCKS, SSM_BLOCK_IN, SSM_BLOCK_STATE)


def _blockdiag_out(cc):
    t = jnp.transpose(cc, (0, 2, 1)).reshape(SSM_BLOCKS, 8, SSM_STATE, SSM_GROUP)
    eye = jnp.eye(8, dtype=cc.dtype)
    return (t[:, :, :, None, :] * eye[None, :, None, :, None]).reshape(SSM_BLOCKS, SSM_BLOCK_STATE, SSM_BLOCK_IN)


def _diag_in(d):
    t = d.reshape(SSM_BLOCKS, 8, SSM_GROUP, 8, SSM_STATE)
    t = jnp.einsum("bghgp->bghp", t).reshape(SSM_GROUPS, SSM_GROUP, SSM_STATE)
    return jnp.transpose(t, (0, 2, 1))


def _diag_out(d):
    t = d.reshape(SSM_BLOCKS, 8, SSM_STATE, 8, SSM_GROUP)
    t = jnp.einsum("bgpgh->bgph", t).reshape(SSM_GROUPS, SSM_STATE, SSM_GROUP)
    return jnp.transpose(t, (0, 2, 1))


def _powers(ar, ai):
    rs, is_ = [ar], [ai]
    for _ in range(7):
        r, i = rs[-1], is_[-1]
        rs.append(r * ar - i * ai)
        is_.append(r * ai + i * ar)
    return jnp.stack(rs), jnp.stack(is_), jnp.stack(rs[::-1]), jnp.stack(is_[::-1])


def attn_fwd(q, kv, *, name):
    T, D = q.shape
    nm = kv.shape[0]
    tq = _pick(T, [512, 256, 128])
    scale = HEAD_DIM ** -0.5

    def body(q_ref, k_ref, v_ref, o_ref):
        for h in range(N_HEADS):
            sl = slice(h * HEAD_DIM, (h + 1) * HEAD_DIM)
            s = lax.dot_general(q_ref[:, sl], k_ref[:, sl].astype(BF16), _DIMS["nt"], preferred_element_type=F32) * scale
            e = jnp.exp(s - jnp.max(s, axis=-1, keepdims=True))
            pr = e / jnp.sum(e, axis=-1, keepdims=True)
            o_ref[:, sl] = jnp.dot(pr.astype(BF16), v_ref[:, sl].astype(BF16), preferred_element_type=F32).astype(BF16)

    return pl.pallas_call(
        body, grid=(T // tq,),
        in_specs=[_rows(tq, D), pl.BlockSpec((nm, D), lambda i: (0, 0)), pl.BlockSpec((nm, D), lambda i: (0, 1))],
        out_specs=_rows(tq, D), out_shape=jax.ShapeDtypeStruct((T, D), BF16),
        compiler_params=_cparams(("arbitrary",)), name=name)(q, kv, kv)


def attn_bwd(q, kv, do, *, name):
    T, D = q.shape
    nm = kv.shape[0]
    tq = _pick(T, [512, 256, 128])
    nt = T // tq
    scale = HEAD_DIM ** -0.5

    def body(q_ref, k_ref, v_ref, do_ref, dq_ref, dkv_ref):
        i = pl.program_id(0)

        @pl.when(i == 0)
        def _():
            dkv_ref[...] = jnp.zeros_like(dkv_ref)

        for h in range(N_HEADS):
            sl = slice(h * HEAD_DIM, (h + 1) * HEAD_DIM)
            slv = slice(D + h * HEAD_DIM, D + (h + 1) * HEAD_DIM)
            qh = q_ref[:, sl]
            kh = k_ref[:, sl].astype(BF16)
            vh = v_ref[:, sl].astype(BF16)
            doh = do_ref[:, sl].astype(BF16)
            s = lax.dot_general(qh, kh, _DIMS["nt"], preferred_element_type=F32) * scale
            e = jnp.exp(s - jnp.max(s, axis=-1, keepdims=True))
            pr = e / jnp.sum(e, axis=-1, keepdims=True)
            dp = lax.dot_general(doh, vh, _DIMS["nt"], preferred_element_type=F32)
            ds = (pr * (dp - jnp.sum(pr * dp, axis=-1, keepdims=True)) * scale).astype(BF16)
            dq_ref[:, sl] = jnp.dot(ds, kh, preferred_element_type=F32).astype(BF16)
            dkv_ref[:, sl] += lax.dot_general(ds, qh, _DIMS["tn"], preferred_element_type=F32)
            dkv_ref[:, slv] += lax.dot_general(pr.astype(BF16), doh, _DIMS["tn"], preferred_element_type=F32)

    return pl.pallas_call(
        body, grid=(nt,),
        in_specs=[_rows(tq, D), pl.BlockSpec((nm, D), lambda i: (0, 0)), pl.BlockSpec((nm, D), lambda i: (0, 1)), _rows(tq, D)],
        out_specs=[_rows(tq, D), _const((nm, 2 * D))],
        out_shape=[jax.ShapeDtypeStruct((T, D), BF16), jax.ShapeDtypeStruct((nm, 2 * D), F32)],
        compiler_params=_cparams(("arbitrary",)), name=name)(q, kv, kv, do)


def _adam_math(w, g, m, v):
    m = ADAM_B1 * m + (1.0 - ADAM_B1) * g
    v = ADAM_B2 * v + (1.0 - ADAM_B2) * (g * g)
    m_hat = m / (1.0 - ADAM_B1 ** ADAM_STEP)
    v_hat = v / (1.0 - ADAM_B2 ** ADAM_STEP)
    delta = -ADAM_LR * (m_hat / (jnp.sqrt(v_hat) + ADAM_EPS) + ADAM_WD * w)
    return delta, m, v


def adamw(w, m, v, g_arr, g_row0, *, name):
    R, C = w.shape
    tr = _pick(R, [512, 256, 128, 64, 32, 16, 8])
    assert g_row0 % tr == 0
    g0 = g_row0 // tr

    def body(w_ref, m_ref, v_ref, g_ref, go_ref, d_ref, mo_ref, vo_ref):
        g = g_ref[...]
        d, mn, vn = _adam_math(w_ref[...], g, m_ref[...], v_ref[...])
        go_ref[...] = g
        d_ref[...] = d
        mo_ref[...] = mn
        vo_ref[...] = vn

    sp = pl.BlockSpec((tr, C), lambda i: (i, 0))
    return pl.pallas_call(
        body, grid=(R // tr,), in_specs=[sp, sp, sp, pl.BlockSpec((tr, C), lambda i: (g0 + i, 0))],
        out_specs=[sp] * 4, out_shape=[jax.ShapeDtypeStruct((R, C), F32)] * 4,
        compiler_params=_cparams(("arbitrary",)), name=name)(w, m, v, g_arr)


def _place():
    x, y, c = lax.axis_index("x"), lax.axis_index("y"), lax.axis_index("c")
    chips = [(1 - x, y), (x, 1 - y), (1 - x, 1 - y)]
    return x, y, c, chips


ANY = pl.BlockSpec(memory_space=pl.ANY)


def allgather_weights(bufs, *, name):
    n = len(bufs)

    def body(*refs):
        o_refs = refs[n:2 * n]
        send_sems, recv_sems, fsend_sems, frecv_sems = refs[2 * n:]
        x, y, c, chips = _place()
        k_me = 2 * x + y
        sib = (x, y, 1 - c)
        halves = [b.shape[1] // 2 for b in bufs]

        def half(a, cc):
            return pl.ds(pl.multiple_of(cc * halves[a], 16), halves[a])

        sends = []
        for a in range(n):
            for r, (px, py) in enumerate(chips):
                cp = pltpu.make_async_remote_copy(
                    src_ref=o_refs[a].at[k_me, half(a, c)], dst_ref=o_refs[a].at[k_me, half(a, c)],
                    send_sem=send_sems.at[3 * a + r], recv_sem=recv_sems.at[3 * a + r],
                    device_id=(px, py, c), device_id_type=MESH)
                cp.start()
                sends.append(cp)
        passed = []
        for a in range(n):
            for r, (px, py) in enumerate(chips):
                win = o_refs[a].at[2 * px + py, half(a, c)]
                pltpu.make_async_remote_copy(
                    src_ref=win, dst_ref=win, send_sem=send_sems.at[3 * a + r], recv_sem=recv_sems.at[3 * a + r],
                    device_id=(px, py, c), device_id_type=MESH).wait_recv()
                cp = pltpu.make_async_remote_copy(
                    src_ref=win, dst_ref=win, send_sem=fsend_sems.at[3 * a + r], recv_sem=frecv_sems.at[3 * a + r],
                    device_id=sib, device_id_type=MESH)
                cp.start()
                passed.append(cp)
        for a in range(n):
            for r, (px, py) in enumerate(chips):
                win = o_refs[a].at[2 * px + py, half(a, 1 - c)]
                pltpu.make_async_remote_copy(
                    src_ref=win, dst_ref=win, send_sem=fsend_sems.at[3 * a + r], recv_sem=frecv_sems.at[3 * a + r],
                    device_id=sib, device_id_type=MESH).wait_recv()
        for cp in sends + passed:
            cp.wait_send()

    return pl.pallas_call(
        body, in_specs=[ANY] * n, out_specs=[ANY] * n,
        out_shape=[jax.ShapeDtypeStruct(b.shape, b.dtype) for b in bufs],
        scratch_shapes=[pltpu.SemaphoreType.DMA((3 * n,))] * 4,
        input_output_aliases={a: a for a in range(n)},
        name=name)(*bufs)


HBM_SPEC = pl.BlockSpec(memory_space=pltpu.HBM)
SEM_SPEC = pl.BlockSpec(memory_space=pltpu.SEMAPHORE)


def _hbm(a):
    return pltpu.with_memory_space_constraint(a, pltpu.HBM)


def gather_start(bufs, pieces, *, name):
    n = len(bufs)
    npc = len(pieces)

    def body(*refs):
        b_refs = refs[:n]
        send_sems, recv_sems = refs[n], refs[n + 1]
        x, y, c, chips = _place()
        k_me = 2 * x + y
        for q, (a, row0, rows) in enumerate(pieces):
            win = b_refs[a].at[k_me, pl.ds(row0, rows)]
            for r, (px, py) in enumerate(chips):
                pltpu.make_async_remote_copy(
                    src_ref=win, dst_ref=win, send_sem=send_sems.at[3 * q + r], recv_sem=recv_sems.at[3 * q + r],
                    device_id=(px, py, c), device_id_type=MESH).start()

    return pl.pallas_call(
        body, in_specs=[HBM_SPEC] * n, out_specs=[SEM_SPEC, SEM_SPEC] + [HBM_SPEC] * n,
        out_shape=[pltpu.SemaphoreType.DMA((3 * npc,)), pltpu.SemaphoreType.DMA((3 * npc,))]
        + [pltpu.HBM(b.shape, b.dtype) for b in bufs],
        input_output_aliases={a: 2 + a for a in range(n)},
        compiler_params=pltpu.CompilerParams(has_side_effects=pltpu.SideEffectType.DATAFLOW_SIDE_EFFECTING),
        name=name)(*[_hbm(b) for b in bufs])


def gather_wait(send_sems, recv_sems, bufs, which, after, *, name):
    n = len(bufs)

    def body(*refs):
        b_refs = refs[:n]
        send_sems, recv_sems = refs[n], refs[n + 1]
        x, y, c, chips = _place()
        k_me = 2 * x + y
        for a, row0, rows, q in which:
            for r, (px, py) in enumerate(chips):
                cp = pltpu.make_async_remote_copy(
                    src_ref=b_refs[a].at[k_me, pl.ds(row0, rows)], dst_ref=b_refs[a].at[2 * px + py, pl.ds(row0, rows)],
                    send_sem=send_sems.at[3 * q + r], recv_sem=recv_sems.at[3 * q + r],
                    device_id=(px, py, c), device_id_type=MESH)
                cp.wait_send()
                cp.wait_recv()

    return pl.pallas_call(
        body, in_specs=[HBM_SPEC] * n + [SEM_SPEC, SEM_SPEC, ANY], out_specs=[HBM_SPEC] * n,
        out_shape=[pltpu.HBM(b.shape, b.dtype) for b in bufs],
        input_output_aliases={a: a for a in range(n)},
        compiler_params=pltpu.CompilerParams(has_side_effects=pltpu.SideEffectType.DATAFLOW_SIDE_EFFECTING),
        name=name)(*bufs, send_sems, recv_sems, after)


def exchange_halves(grads, *, name):
    n = len(grads)

    def body(*refs):
        g_refs, l_refs = refs[:n], refs[n:2 * n]
        send_sems, recv_sems = refs[2 * n:]
        x, y, c, _ = _place()
        cps = []
        for a in range(n):
            h = grads[a].shape[1] // 2
            cp = pltpu.make_async_remote_copy(
                src_ref=g_refs[a].at[:, pl.ds(pl.multiple_of((1 - c) * h, 8), h)], dst_ref=l_refs[a],
                send_sem=send_sems.at[a], recv_sem=recv_sems.at[a], device_id=(x, y, 1 - c), device_id_type=MESH)
            cp.start()
            cps.append(cp)
        for cp in cps:
            cp.wait()

    return pl.pallas_call(
        body, in_specs=[ANY] * n, out_specs=[ANY] * n,
        out_shape=[jax.ShapeDtypeStruct((g.shape[0], g.shape[1] // 2, g.shape[2]), g.dtype) for g in grads],
        scratch_shapes=[pltpu.SemaphoreType.DMA((n,))] * 2,
        name=name)(*grads)


N_PEERS = N_DEV - 1


def _scatter_copies(p_refs, l_refs, send_sems, recv_sems):
    x, y, c, _ = _place()
    cps = []
    for a in range(len(p_refs)):
        h = p_refs[a].shape[1] // 2
        for fx, fy in ((0, 0), (1, 0), (0, 1), (1, 1)):
            for fc in (0, 1):
                if (fx, fy, fc) == (0, 0, 0):
                    continue
                slot = 2 * (fx + 2 * fy) + fc - 1
                px, py, pc = (1 - x if fx else x), (1 - y if fy else y), (1 - c if fc else c)
                cps.append(pltpu.make_async_remote_copy(
                    src_ref=p_refs[a].at[2 * px + py, pl.ds(pl.multiple_of(pc * h, 16), h)], dst_ref=l_refs[a].at[slot],
                    send_sem=send_sems.at[N_PEERS * a + slot], recv_sem=recv_sems.at[N_PEERS * a + slot],
                    device_id=(px, py, pc), device_id_type=MESH))
    return cps


def scatter_start(parts, *, name):
    n = len(parts)
    lands = [lax.empty((N_PEERS, p.shape[1] // 2, p.shape[2]), p.dtype) for p in parts]

    def body(*refs):
        for cp in _scatter_copies(refs[:n], refs[n:2 * n], refs[2 * n], refs[2 * n + 1]):
            cp.start()

    outs = pl.pallas_call(
        body, in_specs=[HBM_SPEC] * (2 * n), out_specs=[SEM_SPEC, SEM_SPEC] + [HBM_SPEC] * (2 * n),
        out_shape=[pltpu.SemaphoreType.DMA((N_PEERS * n,)), pltpu.SemaphoreType.DMA((N_PEERS * n,))]
        + [pltpu.HBM(a.shape, a.dtype) for a in parts + lands],
        input_output_aliases={a: 2 + a for a in range(2 * n)},
        compiler_params=pltpu.CompilerParams(has_side_effects=pltpu.SideEffectType.DATAFLOW_SIDE_EFFECTING),
        name=name)(*[_hbm(a) for a in parts + lands])
    return outs[0], outs[1], list(outs[2:2 + n]), list(outs[2 + n:])


def scatter_wait(rounds, after, *, name):
    sizes = [len(r[2]) for r in rounds]
    flat = [a for r in rounds for a in r[2] + r[3]]
    sems = [s for r in rounds for s in (r[0], r[1])]
    nflat = len(flat)

    def body(*refs):
        pos = 0
        for ri, n in enumerate(sizes):
            for cp in _scatter_copies(refs[pos:pos + n], refs[pos + n:pos + 2 * n], refs[nflat + 2 * ri], refs[nflat + 2 * ri + 1]):
                cp.wait_send()
                cp.wait_recv()
            pos += 2 * n

    outs = pl.pallas_call(
        body, in_specs=[HBM_SPEC] * nflat + [SEM_SPEC] * len(sems) + [ANY], out_specs=[HBM_SPEC] * nflat,
        out_shape=[pltpu.HBM(a.shape, a.dtype) for a in flat],
        input_output_aliases={a: a for a in range(nflat)},
        compiler_params=pltpu.CompilerParams(has_side_effects=pltpu.SideEffectType.DATAFLOW_SIDE_EFFECTING),
        name=name)(*flat, *sems, after)
    res, pos = [], 0
    for n in sizes:
        res.append((list(outs[pos:pos + n]), list(outs[pos + n:pos + 2 * n])))
        pos += 2 * n
    return res


def join_halves(fulls, *, name):
    n = len(fulls)

    def body(*refs):
        o_refs = refs[n:2 * n]
        send_sems, recv_sems = refs[2 * n:]
        x, y, c, _ = _place()
        cps = []
        for a in range(n):
            h = fulls[a].shape[0] // 2
            win = o_refs[a].at[pl.ds(pl.multiple_of(c * h, 8), h)]
            cp = pltpu.make_async_remote_copy(
                src_ref=win, dst_ref=win, send_sem=send_sems.at[a], recv_sem=recv_sems.at[a],
                device_id=(x, y, 1 - c), device_id_type=MESH)
            cp.start()
            cps.append(cp)
        for a in range(n):
            h = fulls[a].shape[0] // 2
            other = o_refs[a].at[pl.ds(pl.multiple_of((1 - c) * h, 8), h)]
            pltpu.make_async_remote_copy(
                src_ref=other, dst_ref=other, send_sem=send_sems.at[a], recv_sem=recv_sems.at[a],
                device_id=(x, y, 1 - c), device_id_type=MESH).wait_recv()
        for cp in cps:
            cp.wait_send()

    return pl.pallas_call(
        body, in_specs=[ANY] * n, out_specs=[ANY] * n,
        out_shape=[jax.ShapeDtypeStruct(f.shape, f.dtype) for f in fulls],
        scratch_shapes=[pltpu.SemaphoreType.DMA((n,))] * 2,
        input_output_aliases={a: a for a in range(n)},
        name=name)(*fulls)


def add_partials(part, land, kc, *, name):
    _, R, C = part.shape
    H = R // 2
    tr = _pick(H, [256, 128, 64, 32, 16])
    per = H // tr

    def body(kc_ref, p_ref, l_ref, o_ref):
        acc = p_ref[...].astype(F32)
        for s in range(N_PEERS):
            acc = acc + l_ref[s].astype(F32)
        o_ref[...] = acc

    return pl.pallas_call(
        body,
        grid_spec=pltpu.PrefetchScalarGridSpec(
            num_scalar_prefetch=1, grid=(per,),
            in_specs=[pl.BlockSpec((None, tr, C), lambda i, kc_ref: (kc_ref[0], kc_ref[1] * per + i, 0)),
                      pl.BlockSpec((N_PEERS, tr, C), lambda i, kc_ref: (0, i, 0))],
            out_specs=pl.BlockSpec((tr, C), lambda i, kc_ref: (kc_ref[1] * per + i, 0))),
        out_shape=jax.ShapeDtypeStruct((R, C), F32),
        compiler_params=_cparams(("arbitrary",)), name=name)(kc, part, land)


def allgather_sum(v, *, name):
    m_per, n = v.shape

    def body(x_ref, out_ref, sum_ref, send_sems, recv_sems, local_sem):
        x, y, c, chips = _place()
        me, sibling = (x, y, c), (x, y, 1 - c)

        def rows(px, py, pc):
            return out_ref.at[pl.ds(pl.multiple_of((4 * px + 2 * py + pc) * m_per, 8), m_per), :]

        def copy(k, block, to, src=None):
            return pltpu.make_async_remote_copy(
                src_ref=rows(*block) if src is None else src, dst_ref=rows(*block),
                send_sem=send_sems.at[k], recv_sem=recv_sems.at[k], device_id=to, device_id_type=MESH)

        mine = pltpu.make_async_copy(x_ref, rows(*me), local_sem)
        mine.start()
        first = [copy(0, me, sibling, src=x_ref)]
        first += [copy(1 + j, me, (*chip, c), src=x_ref) for j, chip in enumerate(chips)]
        for cp in first:
            cp.start()
        passed = [copy(4 + j, (*chip, c), sibling) for j, chip in enumerate(chips)]
        for j, chip in enumerate(chips):
            copy(1 + j, (*chip, c), me).wait_recv()
            passed[j].start()
        copy(0, sibling, me).wait_recv()
        for j, chip in enumerate(chips):
            copy(4 + j, (*chip, 1 - c), me).wait_recv()
        for cp in first + passed:
            cp.wait_send()
        mine.wait()
        acc = out_ref[0:m_per, :]
        for d in range(1, N_DEV):
            acc = acc + out_ref[d * m_per:(d + 1) * m_per, :]
        sum_ref[...] = acc

    vm = pl.BlockSpec(memory_space=pltpu.VMEM)
    return pl.pallas_call(
        body, in_specs=[vm], out_specs=[vm, vm],
        out_shape=[jax.ShapeDtypeStruct((N_DEV * m_per, n), v.dtype), jax.ShapeDtypeStruct((m_per, n), v.dtype)],
        scratch_shapes=[pltpu.SemaphoreType.DMA((7,)), pltpu.SemaphoreType.DMA((7,)), pltpu.SemaphoreType.DMA],
        compiler_params=pltpu.CompilerParams(vmem_limit_bytes=VMEM_LIMIT_BYTES), name=name)(v)


def allreduce_two_level(v, *, name):
    m, n = v.shape
    h = m // 2

    def body(x_ref, out_ref, sib_ref, chip_ref, sems_send, sems_recv):
        x, y, c, chips = _place()
        k_me = 2 * x + y
        sib = (x, y, 1 - c)
        mine = pl.ds(pl.multiple_of(c * h, 8), h)
        other = pl.ds(pl.multiple_of((1 - c) * h, 8), h)

        def copy(q, src, dst, to):
            return pltpu.make_async_remote_copy(src_ref=src, dst_ref=dst, send_sem=sems_send.at[q], recv_sem=sems_recv.at[q],
                                                device_id=to, device_id_type=MESH)

        first = copy(0, x_ref.at[other], sib_ref, sib)
        first.start()
        first.wait()
        chip_ref[k_me] = x_ref[mine, :] + sib_ref[...]
        sends = [copy(1 + r, chip_ref.at[k_me], chip_ref.at[k_me], (px, py, c)) for r, (px, py) in enumerate(chips)]
        for cp in sends:
            cp.start()
        for r, (px, py) in enumerate(chips):
            copy(1 + r, chip_ref.at[2 * px + py], chip_ref.at[2 * px + py], (px, py, c)).wait_recv()
        for cp in sends:
            cp.wait_send()
        total = ((chip_ref[0] + chip_ref[1]) + chip_ref[2]) + chip_ref[3]
        out_ref[mine, :] = total
        last = copy(4, out_ref.at[mine], out_ref.at[mine], sib)
        last.start()
        copy(4, out_ref.at[other], out_ref.at[other], sib).wait_recv()
        last.wait_send()

    vm = pl.BlockSpec(memory_space=pltpu.VMEM)
    return pl.pallas_call(
        body, in_specs=[vm], out_specs=vm, out_shape=jax.ShapeDtypeStruct((m, n), v.dtype),
        scratch_shapes=[pltpu.VMEM((h, n), v.dtype), pltpu.VMEM((N_CHIPS, h, n), v.dtype),
                        pltpu.SemaphoreType.DMA((5,)), pltpu.SemaphoreType.DMA((5,))],
        compiler_params=pltpu.CompilerParams(vmem_limit_bytes=VMEM_LIMIT_BYTES), name=name)(v)


PACK_W = D_MODEL


def _pack_rows(shape):
    return -(-math.prod(shape) // PACK_W)


def _pack(arrs):
    cols = []
    for a in arrs:
        f = a.reshape(-1)
        pad = (-f.shape[0]) % PACK_W
        cols.append((jnp.pad(f, (0, pad)) if pad else f).reshape(-1, PACK_W))
    out = jnp.concatenate(cols, axis=0)
    pad = (-out.shape[0]) % 16
    return jnp.pad(out, ((0, pad), (0, 0)))


def _unpack(buf, shapes):
    outs, r = [], 0
    for s in shapes:
        nel = math.prod(s)
        nr = _pack_rows(s)
        outs.append(buf[r:r + nr].reshape(-1)[:nel].reshape(s))
        r += nr
    return outs


GA_CONV_OUT, GA_MIX_OUT, GA_WQ, GA_WO, GA_DOWN, GA_UP, GA_ROWS = 0, 256, 512, 768, 1024, 2048, 3072
G1_DOWN, G1_UP, G1_ROWS = 0, 1024, 2048
G2_CONV_OUT, G2_MIX_OUT, G2_WQ, G2_WO, G2_ROWS = 0, 256, 512, 768, 1024


def kernel(x, mem, in_norm_g, in_norm_b, w_in, conv_dw, conv_db, conv_norm_g, conv_norm_b, w_conv_out, ssm_log_step, ssm_lambda_re, ssm_lambda_im, ssm_b_re, ssm_b_im, ssm_c_re, ssm_c_im, ssm_d, w_ssm_glu, w_mix_out, ln1_g, ln1_b, xa_wq, xa_wkv, xa_wo, ln2_g, ln2_b, mlp_w_up, mlp_w_down, ln3_g, ln3_b, loss_target, m_in_norm_g, m_in_norm_b, m_w_in, m_conv_dw, m_conv_db, m_conv_norm_g, m_conv_norm_b, m_w_conv_out, m_ssm_log_step, m_ssm_lambda_re, m_ssm_lambda_im, m_ssm_b_re, m_ssm_b_im, m_ssm_c_re, m_ssm_c_im, m_ssm_d, m_w_ssm_glu, m_w_mix_out, m_ln1_g, m_ln1_b, m_xa_wq, m_xa_wkv, m_xa_wo, m_ln2_g, m_ln2_b, m_mlp_w_up, m_mlp_w_down, m_ln3_g, m_ln3_b, v_in_norm_g, v_in_norm_b, v_w_in, v_conv_dw, v_conv_db, v_conv_norm_g, v_conv_norm_b, v_w_conv_out, v_ssm_log_step, v_ssm_lambda_re, v_ssm_lambda_im, v_ssm_b_re, v_ssm_b_im, v_ssm_c_re, v_ssm_c_im, v_ssm_d, v_w_ssm_glu, v_w_mix_out, v_ln1_g, v_ln1_b, v_xa_wq, v_xa_wkv, v_xa_wo, v_ln2_g, v_ln2_b, v_mlp_w_up, v_mlp_w_down, v_ln3_g, v_ln3_b):
    D = D_MODEL
    xs = x[0]
    T = xs.shape[0]
    mems = mem[0]
    NM = mems.shape[0]
    tgt = loss_target[0]
    my_c = lax.axis_index("c")
    k_me = 2 * lax.axis_index("x") + lax.axis_index("y")
    c_arr = jnp.reshape(my_c, (1,)).astype(jnp.int32)
    k_arr = jnp.reshape(k_me, (1,)).astype(jnp.int32)

    sh_a = jnp.concatenate([w_conv_out[0], w_mix_out[0], xa_wq[0], xa_wo[0], mlp_w_down[0], mlp_w_up[0]], axis=0).astype(BF16)
    def own_block(shard):
        buf = lax.empty((N_CHIPS,) + shard.shape, shard.dtype)
        return lax.dynamic_update_slice(buf, shard[None], (k_me, 0, 0))

    dw_pad = jnp.pad(conv_dw[0], ((0, CONV_HALO - CONV_K), (0, 0)))
    (GIN,) = allgather_weights([own_block(w_in[0].astype(BF16))], name="gather_w_in")
    ag_bufs = [own_block(sh_a), GIN] + [own_block(s) for s in (xa_wkv[0].astype(BF16), w_ssm_glu[0].astype(BF16), dw_pad)]
    ag_pieces = [(4, 0, CONV_HALO), (0, GA_CONV_OUT, 256), (3, 0, D_SSM), (0, GA_MIX_OUT, 256), (0, GA_WQ, 256),
                 (2, 0, D), (0, GA_WO, 256), (0, GA_UP, D), (0, GA_DOWN, D)]
    ag_send, ag_recv, GA, GIN, GKV, GGLU, GDW = gather_start(ag_bufs, ag_pieces, name="gather_start")

    def w_rowshard(row0):
        return dict(b_spec=((N_CHIPS, 256, D), lambda i, j, k: (0, row0 // 256, 0)), b_view=(D, D), tn=D, tk=D)

    _, h0, h0b = ln_fwd(xs, in_norm_g, in_norm_b, name="ln0_fwd")
    p = mm_nn(h0b, GIN, ((None, D, 1152), lambda i, j, k: (j, 0, 0)), D_IN, tn=1152, tk=D, name="mm_w_in")[0]
    GA, GGLU, GDW = gather_wait(
        ag_send, ag_recv, [GA, GGLU, GDW],
        [(2, 0, CONV_HALO, 0), (0, GA_CONV_OUT, 256, 1), (1, 0, D_SSM, 2), (0, GA_MIX_OUT, 256, 3)], p, name="gather_wait_mixer")
    dw_taps = jnp.transpose(GDW, (1, 0, 2)).reshape(CONV_HALO, D)
    c_pre, actb = conv_fwd(p, dw_taps, conv_db, conv_norm_g[0].reshape(1, D), conv_norm_b[0].reshape(1, D), name="conv_fwd")
    ya = mm_nn(actb, GA, N=D, name="mm_conv_out", **w_rowshard(GA_CONV_OUT))[0]

    lstep, lre, lim = ssm_log_step[0], ssm_lambda_re[0], ssm_lambda_im[0]
    bre, bim, cre, cim = ssm_b_re[0], ssm_b_im[0], ssm_c_re[0], ssm_c_im[0]
    (ar, ai, bbr, bbi), disc_vjp = jax.vjp(_ssm_discretise, lstep, lre, lim, bre, bim)
    Br, Bi = _blockdiag_in(bbr), _blockdiag_in(bbi)
    Cr, Ci = _blockdiag_out(cre), _blockdiag_out(cim)
    lam_r, lam_i = ar.reshape(1, -1), ai.reshape(1, -1)
    dvec = ssm_d[0].reshape(1, D_SSM)
    xr, xi, yssm = ssm_seg_fwd(p, Br, Bi, Cr, Ci, lam_r, lam_i, dvec, name="ssm_fwd")
    z = mm_nn(yssm, GGLU, ((None, D_SSM, 512), lambda i, j, k: (j, 0, 0)), 2 * D, tn=512, tk=D_SSM, name="mm_ssm_glu")[0]
    mergedb = merge_fwd(p, ya, z, name="merge_fwd")
    tm_ln = _pick(T, [512, 256, 128])
    row_spec = ((1, D), lambda i, j, k: (0, 0))

    def ln_epilogue(acc, res, g, b):
        r = ALPHA * res + acc
        xhat, _ = _ln_stats(r)
        h = xhat * g + b
        return r, h, h

    def mm_ln(a, row0, res, g, b, name):
        return mm_nn(a, GA, N=D, tm=tm_ln, extras=(res, g.reshape(1, D), b.reshape(1, D)),
                     extra_specs=[_mn(tm_ln, D), row_spec, row_spec], epilogue=ln_epilogue, out_dtypes=[F32, F32, BF16],
                     name=name, **w_rowshard(row0))

    r1, h1, h1b = mm_ln(mergedb, GA_MIX_OUT, h0, ln1_g[0], ln1_b[0], "mm_mix_out_ln1")
    GA, GKV = gather_wait(ag_send, ag_recv, [GA, GKV], [(0, GA_WQ, 256, 4), (1, 0, D, 5), (0, GA_WO, 256, 6)], r1,
                          name="gather_wait_attn")

    qb = mm_nn(h1b, GA, N=D, out_dtype=BF16, name="mm_wq", **w_rowshard(GA_WQ))[0]
    kv = mm_nn(mems, GKV, ((None, D, 512), lambda i, j, k: (j, 0, 0)), 2 * D, tn=512, tk=D, name="mm_wkv")[0]
    ob = attn_fwd(qb, kv, name="attn_fwd")
    r2, h2, h2b = mm_ln(ob, GA_WO, h1, ln2_g[0], ln2_b[0], "mm_wo_ln2")
    (GA,) = gather_wait(ag_send, ag_recv, [GA], [(0, GA_UP, D, 7), (0, GA_DOWN, D, 8)], r2, name="gather_wait_mlp")

    def relu2(acc):
        zr = jnp.maximum(acc, 0.0)
        return (zr * zr,)

    zzb = mm_nn(h2b, GA, ((None, D, D), lambda i, j, k: (j, GA_UP // D, 0)), D_FF, tn=D, tk=D,
                out_dtype=BF16, epilogue=relu2, name="mm_up")[0]
    ff = mm_nn(zzb, GA, ((N_CHIPS, D, D), lambda i, j, k: (0, GA_DOWN // D, 0)), D, tm=_pick(T, [512, 256, 128]), tn=D, tk=D_FF,
               b_view=(D_FF, D), name="mm_down")[0]
    dr3, dr3b, dg3, db3, sq = ln_loss_bwd(ff, h2, ln3_g[0], ln3_b[0], tgt, name="ln3_loss_bwd")

    def rs_begin(grads, rnd):
        return scatter_start(grads, name=f"rs{rnd}_scatter_start")

    g1_shape = jax.ShapeDtypeStruct((N_CHIPS, G1_ROWS, D), BF16)
    g2_shape = jax.ShapeDtypeStruct((N_CHIPS, G2_ROWS, D), BF16)
    dzpreb = mm_nt(dr3b, GA, ((None, D, D), lambda i, j, k: (j, GA_DOWN // D, 0)), D_FF, tn=D, tk=D, out_dtype=BF16,
                   extras=(zzb,), epilogue=lambda acc, zz: (acc * (2.0 * jnp.sqrt(zz.astype(F32))),), name="mm_down_t")[0]
    G1g = mm_tn(zzb, dr3b, tm=D, tn=D, tk=T, out_spec=((None, D, D), lambda i, j, k: (i, G1_DOWN // D, 0)),
                out_shape=g1_shape, name="mm_down_g")
    G1g = mm_tn(h2b, dzpreb, tm=D, tn=D, tk=T, out_spec=((None, D, D), lambda i, j, k: (j, G1_UP // D, 0)),
                out_shape=g1_shape, out_buf=G1g, name="mm_up_g")
    round1 = rs_begin([G1g], 1)
    dh2 = mm_nt(dzpreb, GA, ((N_CHIPS, D, D), lambda i, j, k: (0, GA_UP // D, 0)), D, tm=_pick(T, [512, 256, 128]), tn=D,
                tk=D_FF, b_chunks=N_CHIPS, extras=(dr3,), epilogue=lambda acc, d: (acc + ALPHA * d,),
                after=(round1[2][0],), name="mm_up_t")[0]
    dr2, dr2b, dg2, db2 = ln_bwd(r2, dh2, ln2_g[0], name="ln2_bwd")

    def g_rowshard(row0, out_buf):
        return dict(tm=D, tn=D, out_spec=((N_CHIPS, 256, D), lambda i, j, k: (0, row0 // 256, 0)), out_shape=g2_shape,
                    out_buf=out_buf)

    dob = mm_nt(dr2b, GA, N=D, out_dtype=BF16, name="mm_wo_t", **w_rowshard(GA_WO))[0]
    G2g = mm_tn(ob, dr2b, name="mm_wo_g", **g_rowshard(G2_WO, None))
    dqb, dkv = attn_bwd(qb, kv, dob, name="attn_bwd")
    G2g = mm_tn(h1b, dqb, name="mm_wq_g", **g_rowshard(G2_WQ, G2g))
    GKVg = mm_tn(mems, dkv, tm=D, tn=512, tk=NM, out_spec=((None, D, 512), lambda i, j, k: (j, 0, 0)),
                 out_shape=jax.ShapeDtypeStruct((N_CHIPS, D, 512), BF16), name="mm_wkv_g")
    dh1 = mm_nt(dqb, GA, N=D, extras=(dr2,), epilogue=lambda acc, d: (acc + ALPHA * d,), name="mm_wq_t",
                **w_rowshard(GA_WQ))[0]
    dr1, dr1b, dg1, db1 = ln_bwd(r1, dh1, ln1_g[0], name="ln1_bwd")

    dmerged = mm_nt(dr1b, GA, N=D, name="mm_mix_t", **w_rowshard(GA_MIX_OUT))[0]
    G2g = mm_tn(mergedb, dr1b, name="mm_mix_g", **g_rowshard(G2_MIX_OUT, G2g))
    dyab, dgatesb, dzb = merge_bwd(dmerged, p, ya, z, name="merge_bwd")
    GGLUg = mm_tn(yssm, dzb, tm=D_SSM, tn=512, out_spec=((None, D_SSM, 512), lambda i, j, k: (j, 0, 0)),
                  out_shape=jax.ShapeDtypeStruct((N_CHIPS, D_SSM, 512), BF16), name="mm_glu_g")
    dyssm = mm_nt(dzb, GGLU, ((N_CHIPS, D_SSM, 512), lambda i, j, k: (0, 0, 0)), D_SSM, tn=D_SSM, tk=2 * D, b_chunks=N_CHIPS,
                  name="mm_glu_t")[0]
    dub, dBr, dBi, dCr, dCi, dar8, dai8, dd8 = ssm_seg_bwd(dyssm, p, xr, xi, Br, Bi, Cr, Ci, lam_r, lam_i, dvec,
                                                           name="ssm_bwd")
    dar = jnp.sum(dar8, axis=0).reshape(SSM_GROUPS, SSM_STATE)
    dai = jnp.sum(dai8, axis=0).reshape(SSM_GROUPS, SSM_STATE)
    g_lstep, g_lre, g_lim, g_bre, g_bim = disc_vjp((dar, dai, _diag_in(dBr), _diag_in(dBi)))
    g_cre, g_cim = _diag_out(dCr), _diag_out(dCi)
    g_d = jnp.sum(dd8, axis=0).reshape(1, D_SSM)

    dact = mm_nt(dyab, GA, N=D, name="mm_conv_out_t", **w_rowshard(GA_CONV_OUT))[0]
    G2g = mm_tn(actb, dyab, name="mm_conv_out_g", **g_rowshard(G2_CONV_OUT, G2g))
    round2 = rs_begin([G2g, GKVg, GGLUg], 2)
    dc, dng, dnb, ddb = conv_bwd_norm(dact, c_pre, conv_norm_g[0].reshape(1, D), conv_norm_b[0].reshape(1, D),
                                      round2[2][0], name="conv_bwd_norm")
    dpb, ddw = conv_bwd_taps(dc, p, dw_taps, dub, dgatesb, name="conv_bwd_taps")
    GINg = mm_tn(h0b, dpb, tm=D, tn=1152, tk=T, out_spec=((None, D, 1152), lambda i, j, k: (j, 0, 0)),
                 out_shape=jax.ShapeDtypeStruct((N_CHIPS, D, 1152), BF16), name="mm_w_in_g")
    round3 = rs_begin([GINg], 3)
    dh0 = mm_nt(dpb, GIN, ((N_CHIPS, D, 1152), lambda i, j, k: (0, 0, 0)), D, tm=_pick(T, [512, 256, 128]), tn=D, tk=D_IN,
                b_chunks=N_CHIPS, extras=(dr1,), epilogue=lambda acc, d: (acc + ALPHA * d,), after=(round3[2][0],),
                name="mm_w_in_t")[0]
    gx, _, dg0, db0 = ln_bwd(xs, dh0, in_norm_g, name="ln0_bwd")

    kc_arr = jnp.concatenate([k_arr, c_arr])
    landed = scatter_wait([round1, round2, round3], gx, name="rs_scatter_wait")
    tags = ["mlp", "sq", "kv", "glu", "in"]
    pairs = [(pt, l2) for parts, lands2 in landed for pt, l2 in zip(parts, lands2)]
    halves = [add_partials(pt, l2, kc_arr, name="rs_add_partials_" + t) for (pt, l2), t in zip(pairs, tags)]
    g1, g2, gKV, gGLU, gIN = join_halves(halves, name="rs_join_halves")

    small_names = ["in_norm_g", "in_norm_b", "conv_db", "conv_norm_g", "conv_norm_b", "ssm_log_step", "ssm_lambda_re",
                   "ssm_lambda_im", "ssm_b_re", "ssm_b_im", "ssm_c_re", "ssm_c_im", "ssm_d", "ln1_g", "ln1_b",
                   "ln2_g", "ln2_b", "ln3_g", "ln3_b"]
    small_w = [in_norm_g, in_norm_b, conv_db, conv_norm_g, conv_norm_b, ssm_log_step, ssm_lambda_re, ssm_lambda_im,
               ssm_b_re, ssm_b_im, ssm_c_re, ssm_c_im, ssm_d, ln1_g, ln1_b, ln2_g, ln2_b, ln3_g, ln3_b]
    small_m = [m_in_norm_g, m_in_norm_b, m_conv_db, m_conv_norm_g, m_conv_norm_b, m_ssm_log_step, m_ssm_lambda_re,
               m_ssm_lambda_im, m_ssm_b_re, m_ssm_b_im, m_ssm_c_re, m_ssm_c_im, m_ssm_d, m_ln1_g, m_ln1_b, m_ln2_g,
               m_ln2_b, m_ln3_g, m_ln3_b]
    small_v = [v_in_norm_g, v_in_norm_b, v_conv_db, v_conv_norm_g, v_conv_norm_b, v_ssm_log_step, v_ssm_lambda_re,
               v_ssm_lambda_im, v_ssm_b_re, v_ssm_b_im, v_ssm_c_re, v_ssm_c_im, v_ssm_d, v_ln1_g, v_ln1_b, v_ln2_g,
               v_ln2_b, v_ln3_g, v_ln3_b]
    small_g = [dg0, db0, ddb, dng, dnb, g_lstep, g_lre, g_lim, g_bre, g_bim, g_cre, g_cim, g_d, dg1, db1, dg2, db2, dg3, db3]
    small_shapes = [w.shape for w in small_w]
    n_small_rows = _pack(small_w).shape[0]
    packed_g = _pack(small_g + [ddw, sq])
    summed = allreduce_two_level(packed_g, name="allreduce_small")
    small_rows = sum(_pack_rows(s) for s in small_shapes)
    dw_rows = _pack_rows((CONV_HALO, D))
    loss = 0.5 * summed[small_rows + dw_rows, 0] / D
    ddw_full = summed[small_rows:small_rows + dw_rows].reshape(CONV_HALO, D)
    g_dw = lax.dynamic_slice_in_dim(ddw_full, k_me * (D // N_CHIPS), D // N_CHIPS, axis=1)
    gs_packed = jnp.pad(summed[:small_rows], ((0, n_small_rows - small_rows), (0, 0)))

    res = {}

    def upd(nm, w, m, v, g_arr, row0=0):
        shp = w.shape
        w2, m2, v2 = (a.reshape(-1, shp[-1]) for a in (w, m, v))
        outs = adamw(w2, m2, v2, g_arr, row0, name="adamw_" + nm)
        res[nm] = tuple(o.reshape(shp) for o in outs)

    upd("w_conv_out", w_conv_out, m_w_conv_out, v_w_conv_out, g2, G2_CONV_OUT)
    upd("w_mix_out", w_mix_out, m_w_mix_out, v_w_mix_out, g2, G2_MIX_OUT)
    upd("xa_wq", xa_wq, m_xa_wq, v_xa_wq, g2, G2_WQ)
    upd("xa_wo", xa_wo, m_xa_wo, v_xa_wo, g2, G2_WO)
    upd("mlp_w_down", mlp_w_down, m_mlp_w_down, v_mlp_w_down, g1, G1_DOWN)
    upd("mlp_w_up", mlp_w_up, m_mlp_w_up, v_mlp_w_up, g1, G1_UP)
    upd("w_in", w_in, m_w_in, v_w_in, gIN)
    upd("xa_wkv", xa_wkv, m_xa_wkv, v_xa_wkv, gKV)
    upd("w_ssm_glu", w_ssm_glu, m_w_ssm_glu, v_w_ssm_glu, gGLU)
    pad_dw = lambda a: jnp.pad(a[0], ((0, CONV_HALO - CONV_K), (0, 0)))
    dw_outs = adamw(pad_dw(conv_dw), pad_dw(m_conv_dw), pad_dw(v_conv_dw), g_dw, 0, name="adamw_conv_dw")
    res["conv_dw"] = tuple(o[:CONV_K][None] for o in dw_outs)
    sm_outs = adamw(_pack(small_w), _pack(small_m), _pack(small_v), gs_packed, 0, name="adamw_small")
    sm_un = [_unpack(o, small_shapes) for o in sm_outs]
    for idx, nm in enumerate(small_names):
        res[nm] = tuple(sm_un[q][idx] for q in range(4))

    order = ["in_norm_g", "in_norm_b", "w_in", "conv_dw", "conv_db", "conv_norm_g", "conv_norm_b", "w_conv_out",
             "ssm_log_step", "ssm_lambda_re", "ssm_lambda_im", "ssm_b_re", "ssm_b_im", "ssm_c_re", "ssm_c_im", "ssm_d",
             "w_ssm_glu", "w_mix_out", "ln1_g", "ln1_b", "xa_wq", "xa_wkv", "xa_wo", "ln2_g", "ln2_b", "mlp_w_up",
             "mlp_w_down", "ln3_g", "ln3_b"]
    return (loss, gx[None], *[res[n][0] for n in order], *[res[n][1] for n in order],
            *[res[n][2] for n in order], *[res[n][3] for n in order])
```

```python
import functools
import math

import jax
import jax.numpy as jnp
from jax import lax
from jax.experimental import pallas as pl
from jax.experimental.pallas import tpu as pltpu

F32 = jnp.float32
BF16 = jnp.bfloat16
MESH = pl.DeviceIdType.MESH

D_MODEL = 1024
N_HEADS = 4
HEAD_DIM = D_MODEL // N_HEADS
CONV_K = 31
CONV_HALO = 32
D_SSM = 512
SSM_GROUPS = 32
SSM_GROUP = 16
SSM_STATE = 64
SSM_BLOCKS = 4
SSM_BLOCK_IN = D_SSM // SSM_BLOCKS
SSM_BLOCK_STATE = SSM_GROUPS * SSM_STATE // SSM_BLOCKS
D_FF = 4096
D_IN = 4608
LN_EPS = 1e-5
ALPHA = (2.0 * 1) ** 0.25
N_CHIPS = 4
N_DEV = 8
ADAM_LR, ADAM_B1, ADAM_B2, ADAM_EPS, ADAM_WD, ADAM_STEP = 0.001, 0.9, 0.999, 1e-08, 0.01, 10
VMEM_LIMIT_BYTES = 56 * 1024 * 1024


def _pick(dim, cands):
    for c in cands:
        if dim % c == 0:
            return c
    return dim


def _cparams(sem=None):
    return pltpu.CompilerParams(dimension_semantics=sem, vmem_limit_bytes=VMEM_LIMIT_BYTES)


def _sigmoid(x):
    return 1.0 / (1.0 + jnp.exp(-x))


_DIMS = {"nn": (((1,), (0,)), ((), ())), "nt": (((1,), (1,)), ((), ())), "tn": (((0,), (0,)), ((), ()))}


def matmul(a, b, *, mode, M, N, K, tm, tn, tk, a_spec, b_spec, out_specs, out_shapes, name,
           extras=(), extra_specs=(), epilogue=None, alias_buf=None, b_view=None, after=(), b_chunks=None):
    nk = K // tk
    ne = len(extras)
    no = len(out_shapes)
    na = (0 if alias_buf is None else 1) + len(after)
    dims = _DIMS[mode]

    def body(*refs):
        a_ref, b_ref = refs[0], refs[1]
        e_refs = refs[2:2 + ne]
        o_refs = refs[2 + ne + na:2 + ne + na + no]

        def finish(acc):
            outs = (acc,) if epilogue is None else epilogue(acc, *[r[...] for r in e_refs])
            for o, r in zip(outs, o_refs):
                r[...] = o.astype(r.dtype).reshape(r.shape)

        if b_chunks:
            kc = a_ref.shape[1] // b_chunks
            prod = None
            for q in range(b_chunks):
                part = lax.dot_general(a_ref[:, q * kc:(q + 1) * kc].astype(BF16), b_ref[q].astype(BF16), dims,
                                       preferred_element_type=F32)
                prod = part if prod is None else prod + part
        else:
            b_blk = b_ref[...] if b_view is None else b_ref[...].reshape(b_view)
            prod = lax.dot_general(a_ref[...].astype(BF16), b_blk.astype(BF16), dims, preferred_element_type=F32)
        if nk == 1:
            finish(prod)
        else:
            acc_ref = refs[-1]
            k = pl.program_id(2)

            @pl.when(k == 0)
            def _():
                acc_ref[...] = prod

            @pl.when(k > 0)
            def _():
                acc_ref[...] += prod

            @pl.when(k == nk - 1)
            def _():
                finish(acc_ref[...])

    in_specs = [pl.BlockSpec(*a_spec), pl.BlockSpec(*b_spec)] + [pl.BlockSpec(*s) for s in extra_specs]
    ins = [a, b, *extras]
    if alias_buf is not None:
        in_specs.append(pl.BlockSpec(memory_space=pl.ANY))
        ins.append(alias_buf)
    for dep in after:
        in_specs.append(pl.BlockSpec(memory_space=pl.ANY))
        ins.append(dep)
    res = pl.pallas_call(
        body,
        grid=(M // tm, N // tn, nk),
        in_specs=in_specs,
        out_specs=[pl.BlockSpec(*s) for s in out_specs],
        out_shape=out_shapes,
        scratch_shapes=[] if nk == 1 else [pltpu.VMEM((tm, tn), F32)],
        input_output_aliases={2 + ne: 0} if alias_buf is not None else {},
        compiler_params=_cparams(("parallel", "parallel", "arbitrary")),
        name=name,
    )(*ins)
    return res


def _mn(tm, tn):
    return ((tm, tn), lambda i, j, k: (i, j))


def mm_nn(a, b_arr, b_spec, N, *, name, tm=None, tn, tk, out_dtype=F32, extras=(), epilogue=None, out_dtypes=None,
          b_view=None, extra_specs=None):
    M, K = a.shape
    tm = tm or _pick(M, [1024, 512, 256, 128])
    dts = out_dtypes or [out_dtype]
    return matmul(a, b_arr, mode="nn", M=M, N=N, K=K, tm=tm, tn=tn, tk=tk,
                  a_spec=((tm, tk), lambda i, j, k: (i, k)), b_spec=b_spec, b_view=b_view,
                  out_specs=[_mn(tm, tn)] * len(dts), out_shapes=[jax.ShapeDtypeStruct((M, N), d) for d in dts],
                  extras=extras, extra_specs=extra_specs or [_mn(tm, tn)] * len(extras), epilogue=epilogue, name=name)


def mm_nt(a, b_arr, b_spec, N, *, name, tm=None, tn, tk, out_dtype=F32, extras=(), epilogue=None, out_dtypes=None,
          b_view=None, after=(), b_chunks=None):
    M, K = a.shape
    tm = tm or _pick(M, [1024, 512, 256, 128])
    dts = out_dtypes or [out_dtype]
    return matmul(a, b_arr, mode="nt", M=M, N=N, K=K, tm=tm, tn=tn, tk=tk, after=after, b_chunks=b_chunks,
                  a_spec=((tm, tk), lambda i, j, k: (i, k)), b_spec=b_spec, b_view=b_view,
                  out_specs=[_mn(tm, tn)] * len(dts), out_shapes=[jax.ShapeDtypeStruct((M, N), d) for d in dts],
                  extras=extras, extra_specs=[_mn(tm, tn)] * len(extras), epilogue=epilogue, name=name)


def mm_tn(a, b, *, name, tm, tn, tk=None, out_spec, out_shape, out_buf=None):
    K, M = a.shape
    N = b.shape[1]
    tk = tk or _pick(K, [2048, 1024, 512, 256, 128])
    return matmul(a, b, mode="tn", M=M, N=N, K=K, tm=tm, tn=tn, tk=tk,
                  a_spec=((tk, tm), lambda i, j, k: (k, i)), b_spec=((tk, tn), lambda i, j, k: (k, j)),
                  out_specs=[out_spec], out_shapes=[out_shape], alias_buf=out_buf, name=name)[0]


def _rows(tc, w, cb=0):
    return pl.BlockSpec((tc, w), lambda i: (i, cb))


def _const(shape):
    return pl.BlockSpec(shape, lambda i: tuple([0] * len(shape)))


def _ln_stats(r):
    mu = jnp.mean(r, axis=-1, keepdims=True)
    xc = r - mu
    var = jnp.mean(xc * xc, axis=-1, keepdims=True)
    rstd = lax.rsqrt(var + LN_EPS)
    return xc * rstd, rstd


def _rowsum8(v):
    tc, w = v.shape
    return jnp.sum(v.reshape(tc // 8, 8, w), axis=0)


def ln_fwd(x, g, b, *, name, res=None):
    T, D = x.shape
    tc = _pick(T, [512, 256, 128])
    has_res = res is not None

    def body(*refs):
        if has_res:
            x_ref, res_ref, g_ref, b_ref, r_ref, h_ref, hb_ref = refs
            r = ALPHA * res_ref[...] + x_ref[...]
            r_ref[...] = r
        else:
            x_ref, g_ref, b_ref, h_ref, hb_ref = refs
            r = x_ref[...]
        xhat, _ = _ln_stats(r)
        y = xhat * g_ref[...] + b_ref[...]
        h_ref[...] = y
        hb_ref[...] = y.astype(BF16)

    ins = [x] + ([res] if has_res else []) + [g.reshape(1, D), b.reshape(1, D)]
    in_specs = [_rows(tc, D)] * (2 if has_res else 1) + [_const((1, D))] * 2
    n_out = 3 if has_res else 2
    outs = pl.pallas_call(
        body, grid=(T // tc,), in_specs=in_specs, out_specs=[_rows(tc, D)] * n_out,
        out_shape=[jax.ShapeDtypeStruct((T, D), F32)] * (n_out - 1) + [jax.ShapeDtypeStruct((T, D), BF16)],
        compiler_params=_cparams(("arbitrary",)), name=name)(*ins)
    if has_res:
        return outs
    return (x,) + tuple(outs)


def ln_bwd(r, dy, g, *, name):
    T, D = r.shape
    tc = _pick(T, [512, 256, 128])
    nt = T // tc

    def body(r_ref, dy_ref, g_ref, dr_ref, drb_ref, dg_ref, db_ref, accg, accb):
        i = pl.program_id(0)

        @pl.when(i == 0)
        def _():
            accg[...] = jnp.zeros_like(accg)
            accb[...] = jnp.zeros_like(accb)

        xhat, rstd = _ln_stats(r_ref[...])
        dy = dy_ref[...]
        dxh = dy * g_ref[...]
        m1 = jnp.mean(dxh, axis=-1, keepdims=True)
        m2 = jnp.mean(dxh * xhat, axis=-1, keepdims=True)
        dr = rstd * (dxh - m1 - xhat * m2)
        dr_ref[...] = dr
        drb_ref[...] = dr.astype(BF16)
        accg[...] += _rowsum8(dy * xhat)
        accb[...] += _rowsum8(dy)

        @pl.when(i == nt - 1)
        def _():
            dg_ref[...] = jnp.sum(accg[...], axis=0, keepdims=True)
            db_ref[...] = jnp.sum(accb[...], axis=0, keepdims=True)

    return pl.pallas_call(
        body, grid=(nt,), in_specs=[_rows(tc, D), _rows(tc, D), _const((1, D))],
        out_specs=[_rows(tc, D), _rows(tc, D), _const((1, D)), _const((1, D))],
        out_shape=[jax.ShapeDtypeStruct((T, D), F32), jax.ShapeDtypeStruct((T, D), BF16),
                   jax.ShapeDtypeStruct((1, D), F32), jax.ShapeDtypeStruct((1, D), F32)],
        scratch_shapes=[pltpu.VMEM((8, D), F32), pltpu.VMEM((8, D), F32)],
        compiler_params=_cparams(("arbitrary",)), name=name)(r, dy, g.reshape(1, D))


def ln_loss_bwd(x, res, g, b, target, *, name):
    T, D = x.shape
    tc = _pick(T, [512, 256, 128])
    nt = T // tc

    def body(x_ref, res_ref, g_ref, b_ref, t_ref, dr_ref, drb_ref, dg_ref, db_ref, loss_ref, accg, accb, accl):
        i = pl.program_id(0)

        @pl.when(i == 0)
        def _():
            accg[...] = jnp.zeros_like(accg)
            accb[...] = jnp.zeros_like(accb)
            accl[...] = jnp.zeros_like(accl)

        r = ALPHA * res_ref[...] + x_ref[...]
        xhat, rstd = _ln_stats(r)
        e = xhat * g_ref[...] + b_ref[...] - t_ref[...]
        dy = e * (1.0 / D)
        dxh = dy * g_ref[...]
        m1 = jnp.mean(dxh, axis=-1, keepdims=True)
        m2 = jnp.mean(dxh * xhat, axis=-1, keepdims=True)
        dr = rstd * (dxh - m1 - xhat * m2)
        dr_ref[...] = dr
        drb_ref[...] = dr.astype(BF16)
        accg[...] += _rowsum8(dy * xhat)
        accb[...] += _rowsum8(dy)
        accl[...] += _rowsum8(e * e)

        @pl.when(i == nt - 1)
        def _():
            dg_ref[...] = jnp.sum(accg[...], axis=0, keepdims=True)
            db_ref[...] = jnp.sum(accb[...], axis=0, keepdims=True)
            s = jnp.sum(jnp.sum(accl[...], axis=0, keepdims=True), axis=1, keepdims=True)
            loss_ref[...] = jnp.broadcast_to(s, (1, 128))

    return pl.pallas_call(
        body, grid=(nt,), in_specs=[_rows(tc, D), _rows(tc, D), _const((1, D)), _const((1, D)), _rows(tc, D)],
        out_specs=[_rows(tc, D), _rows(tc, D), _const((1, D)), _const((1, D)), _const((1, 128))],
        out_shape=[jax.ShapeDtypeStruct((T, D), F32), jax.ShapeDtypeStruct((T, D), BF16),
                   jax.ShapeDtypeStruct((1, D), F32), jax.ShapeDtypeStruct((1, D), F32), jax.ShapeDtypeStruct((1, 128), F32)],
        scratch_shapes=[pltpu.VMEM((8, D), F32)] * 3,
        compiler_params=_cparams(("arbitrary",)), name=name)(x, res, g.reshape(1, D), b.reshape(1, D), target)


def _halo_prev(tc):
    per = tc // CONV_HALO
    return lambda i: jnp.maximum(i * per - 1, 0)


CONV_ROWS = 32
CONV_TAP_GROUP = 4
CONV_TILE_UNROLL = 4


def _fill_shifts(S, nrows):
    for b in range(1, 8):
        S[b, 0:nrows - 8, :] = S[0, b:b + nrows - 8, :]


def _tap_sum(S, w_ref, offs, r0, nrows):
    acc = None
    for k, o in enumerate(offs):
        a, b = divmod(o, 8)
        term = w_ref[k:k + 1, :] * S[b, pl.ds(pl.multiple_of(r0 + 8 * a, 8), nrows), :]
        acc = term if acc is None else acc + term
    return acc


def conv_fwd(p, dw, db, ng, nb, *, name):
    T = p.shape[0]
    D = D_MODEL
    tc = _pick(T, [256, 128])
    prev = _halo_prev(tc)
    off = CONV_HALO - (CONV_K - 1)
    offs = [off + k for k in range(CONV_K)]

    def body(val_ref, gate_ref, valp_ref, gatep_ref, dw_ref, db_ref, ng_ref, nb_ref, c_ref, act_ref, S):
        i = pl.program_id(0)
        u_prev = valp_ref[...] * _sigmoid(gatep_ref[...])
        S[0, 0:CONV_HALO, :] = jnp.where(i > 0, u_prev, 0.0)
        S[0, CONV_HALO:CONV_HALO + tc, :] = val_ref[...] * _sigmoid(gate_ref[...])
        _fill_shifts(S, CONV_HALO + tc)

        def rows(j, carry):
            r0 = pl.multiple_of(j * CONV_ROWS, CONV_ROWS)
            c_ref[pl.ds(r0, CONV_ROWS), :] = _tap_sum(S, dw_ref, offs, r0, CONV_ROWS) + db_ref[...]
            return carry

        lax.fori_loop(0, tc // CONV_ROWS, rows, 0)
        c = c_ref[...]
        xhat, _ = _ln_stats(c)
        cn = xhat * ng_ref[...] + nb_ref[...]
        act_ref[...] = (cn * _sigmoid(cn)).astype(BF16)

    return pl.pallas_call(
        body, grid=(T // tc,),
        in_specs=[_rows(tc, D, 0), _rows(tc, D, 1),
                  pl.BlockSpec((CONV_HALO, D), lambda i: (prev(i), 0)), pl.BlockSpec((CONV_HALO, D), lambda i: (prev(i), 1)),
                  _const((CONV_HALO, D)), _const((1, D)), _const((1, D)), _const((1, D))],
        out_specs=[_rows(tc, D), _rows(tc, D)],
        out_shape=[jax.ShapeDtypeStruct((T, D), F32), jax.ShapeDtypeStruct((T, D), BF16)],
        scratch_shapes=[pltpu.VMEM((8, CONV_HALO + tc, D), F32)],
        compiler_params=_cparams(("arbitrary",)), name=name)(p, p, p, p, dw, db, ng, nb)


def conv_bwd_norm(dact, c_pre, ng, nb, after, *, name):
    T, D = c_pre.shape
    tc = _pick(T, [512, 256, 128])
    nt = T // tc

    def body(da_ref, c_ref, ng_ref, nb_ref, after_ref, dc_ref, dng_ref, dnb_ref, ddb_ref, accg, accb, accd):
        i = pl.program_id(0)

        @pl.when(i == 0)
        def _():
            accg[...] = jnp.zeros_like(accg)
            accb[...] = jnp.zeros_like(accb)
            accd[...] = jnp.zeros_like(accd)

        xhat, rstd = _ln_stats(c_ref[...])
        cn = xhat * ng_ref[...] + nb_ref[...]
        s = _sigmoid(cn)
        dcn = da_ref[...] * (s * (1.0 + cn * (1.0 - s)))
        dxh = dcn * ng_ref[...]
        m1 = jnp.mean(dxh, axis=-1, keepdims=True)
        m2 = jnp.mean(dxh * xhat, axis=-1, keepdims=True)
        dc = rstd * (dxh - m1 - xhat * m2)
        dc_ref[...] = dc
        accg[...] += _rowsum8(dcn * xhat)
        accb[...] += _rowsum8(dcn)
        accd[...] += _rowsum8(dc)

        @pl.when(i == nt - 1)
        def _():
            dng_ref[...] = jnp.sum(accg[...], axis=0, keepdims=True)
            dnb_ref[...] = jnp.sum(accb[...], axis=0, keepdims=True)
            ddb_ref[...] = jnp.sum(accd[...], axis=0, keepdims=True)

    vec = jax.ShapeDtypeStruct((1, D), F32)
    return pl.pallas_call(
        body, grid=(nt,), in_specs=[_rows(tc, D), _rows(tc, D), _const((1, D)), _const((1, D)), ANY],
        out_specs=[_rows(tc, D), _const((1, D)), _const((1, D)), _const((1, D))],
        out_shape=[jax.ShapeDtypeStruct((T, D), F32), vec, vec, vec],
        scratch_shapes=[pltpu.VMEM((8, D), F32)] * 3,
        compiler_params=_cparams(("arbitrary",)), name=name)(dact, c_pre, ng, nb, after)


def conv_bwd_taps(dc, p, dw, du_ssm, dgates, *, name):
    T, D = dc.shape
    tc = _pick(T, [256, 128])
    nt = T // tc
    per = tc // CONV_HALO
    prev = _halo_prev(tc)
    last_halo = T // CONV_HALO - 1
    nxt = lambda i: jnp.minimum((i + 1) * per, last_halo)
    off = CONV_HALO - (CONV_K - 1)

    def body(dc_ref, dcn_ref, val_ref, gate_ref, valp_ref, gatep_ref, dw_ref, dus_ref, dg_ref, dvg_ref, ddw_ref,
             ext_u, ext_d, acc):
        i = pl.program_id(0)

        @pl.when(i == 0)
        def _():
            acc[...] = jnp.zeros_like(acc)

        dvg_ref[:, 2 * D:2 * D + D_SSM] = dus_ref[...]
        dvg_ref[:, 2 * D + D_SSM:D_IN] = dg_ref[...]

        u_prev = valp_ref[...] * _sigmoid(gatep_ref[...])
        ext_u[0, 0:CONV_HALO, :] = jnp.where(i > 0, u_prev, 0.0)
        ext_u[0, CONV_HALO:CONV_HALO + tc, :] = val_ref[...] * _sigmoid(gate_ref[...])
        ext_d[0, 0:tc, :] = dc_ref[...]
        ext_d[0, tc:tc + CONV_HALO, :] = jnp.where(i < nt - 1, dcn_ref[...], 0.0)
        _fill_shifts(ext_u, CONV_HALO + tc)
        _fill_shifts(ext_d, CONV_HALO + tc)

        def rows(j, carry):
            r0 = pl.multiple_of(j * CONV_ROWS, CONV_ROWS)
            sl = pl.ds(r0, CONV_ROWS)
            du = _tap_sum(ext_d, dw_ref, [CONV_K - 1 - k for k in range(CONV_K)], r0, CONV_ROWS)
            sg = _sigmoid(gate_ref[sl, :])
            dvg_ref[sl, 0:D] = (du * sg).astype(BF16)
            dvg_ref[sl, D:2 * D] = (du * val_ref[sl, :] * sg * (1.0 - sg)).astype(BF16)
            return carry

        lax.fori_loop(0, tc // CONV_ROWS, rows, 0)

        for k0 in range(0, CONV_K, CONV_TAP_GROUP):
            ks = list(range(k0, min(k0 + CONV_TAP_GROUP, CONV_K)))

            def taps(j, accs, ks=ks):
                out = list(accs)
                for t in range(CONV_TILE_UNROLL):
                    r0 = pl.multiple_of((j * CONV_TILE_UNROLL + t) * 8, 8)
                    dct = dc_ref[pl.ds(r0, 8), :]
                    for q, k in enumerate(ks):
                        a, b = divmod(off + k, 8)
                        out[q] = out[q] + dct * ext_u[b, pl.ds(pl.multiple_of(r0 + 8 * a, 8), 8), :]
                return tuple(out)

            accs = lax.fori_loop(0, tc // (8 * CONV_TILE_UNROLL), taps, tuple(jnp.zeros((8, D), F32) for _ in ks))
            for k, a_k in zip(ks, accs):
                acc[k] += a_k

        @pl.when(i == nt - 1)
        def _():
            ddw_ref[...] = jnp.zeros_like(ddw_ref)
            for k in range(CONV_K):
                ddw_ref[k:k + 1, :] = jnp.sum(acc[k], axis=0, keepdims=True)

    return pl.pallas_call(
        body, grid=(nt,),
        in_specs=[_rows(tc, D), pl.BlockSpec((CONV_HALO, D), lambda i: (nxt(i), 0)),
                  _rows(tc, D, 0), _rows(tc, D, 1),
                  pl.BlockSpec((CONV_HALO, D), lambda i: (prev(i), 0)), pl.BlockSpec((CONV_HALO, D), lambda i: (prev(i), 1)),
                  _const((CONV_HALO, D)), _rows(tc, D_SSM), _rows(tc, 2 * D)],
        out_specs=[_rows(tc, D_IN), _const((CONV_HALO, D))],
        out_shape=[jax.ShapeDtypeStruct((T, D_IN), BF16), jax.ShapeDtypeStruct((CONV_HALO, D), F32)],
        scratch_shapes=[pltpu.VMEM((8, CONV_HALO + tc, D), F32), pltpu.VMEM((8, CONV_HALO + tc, D), F32),
                        pltpu.VMEM((CONV_K, 8, D), F32)],
        compiler_params=_cparams(("arbitrary",)), name=name)(dc, dc, p, p, p, p, dw, du_ssm, dgates)


GATE_A0 = (2 * D_MODEL + D_SSM) // 512
GATE_B0 = GATE_A0 + 2


def merge_fwd(p, ya, z, *, name):
    T = p.shape[0]
    D = D_MODEL
    tc = _pick(T, [512, 256, 128])
    W = 512

    def body(ga_ref, gb_ref, ya_ref, z1_ref, z2_ref, o_ref):
        yb = z1_ref[...].astype(F32) * _sigmoid(z2_ref[...].astype(F32))
        o_ref[...] = (_sigmoid(ga_ref[...]) * ya_ref[...].astype(F32) + _sigmoid(gb_ref[...]) * yb).astype(BF16)

    return pl.pallas_call(
        body, grid=(T // tc, D // W),
        in_specs=[pl.BlockSpec((tc, W), lambda i, j: (i, GATE_A0 + j)), pl.BlockSpec((tc, W), lambda i, j: (i, GATE_B0 + j)),
                  pl.BlockSpec((tc, W), lambda i, j: (i, j)), pl.BlockSpec((tc, W), lambda i, j: (i, j)),
                  pl.BlockSpec((tc, W), lambda i, j: (i, D // W + j))],
        out_specs=pl.BlockSpec((tc, W), lambda i, j: (i, j)),
        out_shape=jax.ShapeDtypeStruct((T, D), BF16),
        compiler_params=_cparams(("arbitrary", "arbitrary")), name=name)(p, p, ya, z, z)


def merge_bwd(dm, p, ya, z, *, name):
    T = p.shape[0]
    D = D_MODEL
    tc = _pick(T, [256, 128])
    W = 512
    nb = D // W

    def body(dm_ref, ga0_ref, ga1_ref, gb0_ref, gb1_ref, ya_ref, z_ref, dya_ref, dg_ref, dz_ref):
        for j, (ga_ref, gb_ref) in enumerate(((ga0_ref, gb0_ref), (ga1_ref, gb1_ref))):
            c0 = slice(j * W, (j + 1) * W)
            c1 = slice(D + j * W, D + (j + 1) * W)
            dm = dm_ref[:, c0].astype(F32)
            sa = _sigmoid(ga_ref[...])
            sb = _sigmoid(gb_ref[...])
            s2 = _sigmoid(z_ref[:, c1].astype(F32))
            z1 = z_ref[:, c0].astype(F32)
            yb = z1 * s2
            dya_ref[:, c0] = (dm * sa).astype(BF16)
            dg_ref[:, c0] = (dm * ya_ref[:, c0].astype(F32) * sa * (1.0 - sa)).astype(BF16)
            dg_ref[:, c1] = (dm * yb * sb * (1.0 - sb)).astype(BF16)
            dyb = dm * sb
            dz_ref[:, c0] = (dyb * s2).astype(BF16)
            dz_ref[:, c1] = (dyb * z1 * s2 * (1.0 - s2)).astype(BF16)

    gate = lambda cb: pl.BlockSpec((tc, W), lambda i: (i, cb))
    return pl.pallas_call(
        body, grid=(T // tc,),
        in_specs=[_rows(tc, D), gate(GATE_A0), gate(GATE_A0 + 1), gate(GATE_B0), gate(GATE_B0 + 1), _rows(tc, D),
                  _rows(tc, 2 * D)],
        out_specs=[_rows(tc, D), _rows(tc, 2 * D), _rows(tc, 2 * D)],
        out_shape=[jax.ShapeDtypeStruct((T, D), BF16), jax.ShapeDtypeStruct((T, 2 * D), BF16),
                   jax.ShapeDtypeStruct((T, 2 * D), BF16)],
        compiler_params=_cparams(("arbitrary",)), name=name)(dm, p, p, p, p, ya, z)


def _scan_block(src_r, src_i, dst_r, dst_i, car_r, car_i, pw_r, pw_i, cw_r, cw_i, ntiles, reverse, extra=None):
    W = src_r.shape[1]
    rows = lax.broadcasted_iota(jnp.int32, (8, W), 0)
    steps = []
    for d, pr in ((1, 0), (2, 1), (4, 3)):
        valid = rows < 8 - d if reverse else rows >= d
        steps.append((d, jnp.where(valid, jnp.broadcast_to(pw_r[pr:pr + 1, :], (8, W)), 0.0),
                      jnp.where(valid, jnp.broadcast_to(pw_i[pr:pr + 1, :], (8, W)), 0.0)))
    cw_r, cw_i = cw_r[...], cw_i[...]

    def tile(jj, carry):
        j = ntiles - 1 - jj if reverse else jj
        sl = pl.ds(pl.multiple_of(j * 8, 8), 8)
        xr, xi = src_r[sl, :], src_i[sl, :]
        for d, lr, li in steps:
            sr = pltpu.roll(xr, 8 - d if reverse else d, 0)
            si = pltpu.roll(xi, 8 - d if reverse else d, 0)
            xr, xi = xr + lr * sr - li * si, xi + lr * si + li * sr
        cr, ci = car_r[...], car_i[...]
        xr, xi = xr + cw_r * cr - cw_i * ci, xi + cw_r * ci + cw_i * cr
        dst_r[sl, :] = xr
        dst_i[sl, :] = xi
        edge = 0 if reverse else 7
        car_r[...] = jnp.broadcast_to(xr[edge:edge + 1, :], (8, W))
        car_i[...] = jnp.broadcast_to(xi[edge:edge + 1, :], (8, W))
        if extra is not None:
            carry = extra(j, xr, xi, carry)
        return carry

    return tile


def ssm_fwd(p, Br, Bi, Cr, Ci, pw_r, pw_i, dvec, *, name):
    T = p.shape[0]
    tt = _pick(T, [512, 256, 128])
    nt = T // tt
    WI, WS = SSM_BLOCK_IN, SSM_BLOCK_STATE
    u0 = 2 * D_MODEL // WI

    def body(u_ref, br_ref, bi_ref, cr_ref, ci_ref, pwr_ref, pwi_ref, d_ref, xr_ref, xi_ref, y_ref, bur, bui, car_r, car_i):
        i = pl.program_id(1)

        @pl.when(i == 0)
        def _():
            car_r[...] = jnp.zeros_like(car_r)
            car_i[...] = jnp.zeros_like(car_i)

        u = u_ref[...]
        ub = u.astype(BF16)
        bur[...] = jnp.dot(ub, br_ref[...].astype(BF16), preferred_element_type=F32)
        bui[...] = jnp.dot(ub, bi_ref[...].astype(BF16), preferred_element_type=F32)
        tile = _scan_block(bur, bui, xr_ref, xi_ref, car_r, car_i, pwr_ref, pwi_ref, pwr_ref, pwi_ref, tt // 8, False)
        lax.fori_loop(0, tt // 8, tile, 0)
        y = (jnp.dot(xr_ref[...].astype(BF16), cr_ref[...].astype(BF16), preferred_element_type=F32)
             - jnp.dot(xi_ref[...].astype(BF16), ci_ref[...].astype(BF16), preferred_element_type=F32)
             + d_ref[...] * u)
        y_ref[...] = y.astype(BF16)

    return pl.pallas_call(
        body, grid=(SSM_BLOCKS, nt),
        in_specs=[pl.BlockSpec((tt, WI), lambda b, i: (i, u0 + b)),
                  pl.BlockSpec((None, WI, WS), lambda b, i: (b, 0, 0)), pl.BlockSpec((None, WI, WS), lambda b, i: (b, 0, 0)),
                  pl.BlockSpec((None, WS, WI), lambda b, i: (b, 0, 0)), pl.BlockSpec((None, WS, WI), lambda b, i: (b, 0, 0)),
                  pl.BlockSpec((8, WS), lambda b, i: (0, b)), pl.BlockSpec((8, WS), lambda b, i: (0, b)),
                  pl.BlockSpec((1, WI), lambda b, i: (0, b))],
        out_specs=[pl.BlockSpec((tt, WS), lambda b, i: (i, b)), pl.BlockSpec((tt, WS), lambda b, i: (i, b)),
                   pl.BlockSpec((tt, WI), lambda b, i: (i, b))],
        out_shape=[jax.ShapeDtypeStruct((T, SSM_BLOCKS * WS), F32)] * 2 + [jax.ShapeDtypeStruct((T, D_SSM), BF16)],
        scratch_shapes=[pltpu.VMEM((tt, WS), F32), pltpu.VMEM((tt, WS), F32), pltpu.VMEM((8, WS), F32), pltpu.VMEM((8, WS), F32)],
        compiler_params=_cparams(("arbitrary", "arbitrary")), name=name)(p, Br, Bi, Cr, Ci, pw_r, pw_i, dvec)


def ssm_bwd(dy, p, xr, xi, Br, Bi, Cr, Ci, pwc_r, pwc_i, cwc_r, cwc_i, dvec, *, name):
    T = p.shape[0]
    tt = _pick(T, [512, 256, 128])
    nt = T // tt
    WI, WS = SSM_BLOCK_IN, SSM_BLOCK_STATE
    u0 = 2 * D_MODEL // WI
    tb = lambda i: nt - 1 - i
    xprev = lambda i: jnp.maximum(tb(i) * (tt // 8) - 1, 0)
    tn_dims = _DIMS["tn"]
    nt_dims = _DIMS["nt"]

    def body(dy_ref, u_ref, xr_ref, xi_ref, xpr_ref, xpi_ref, br_ref, bi_ref, cr_ref, ci_ref, pwr_ref, pwi_ref,
             cwr_ref, cwi_ref, d_ref,
             du_ref, dbr_ref, dbi_ref, dcr_ref, dci_ref, dar_ref, dai_ref, dd_ref,
             gr, gi, ext_r, ext_i, car_r, car_i):
        i = pl.program_id(1)

        @pl.when(i == 0)
        def _():
            car_r[...] = jnp.zeros_like(car_r)
            car_i[...] = jnp.zeros_like(car_i)
            dbr_ref[...] = jnp.zeros_like(dbr_ref)
            dbi_ref[...] = jnp.zeros_like(dbi_ref)
            dcr_ref[...] = jnp.zeros_like(dcr_ref)
            dci_ref[...] = jnp.zeros_like(dci_ref)
            dar_ref[...] = jnp.zeros_like(dar_ref)
            dai_ref[...] = jnp.zeros_like(dai_ref)
            dd_ref[...] = jnp.zeros_like(dd_ref)

        dy = dy_ref[...]
        dyb = dy.astype(BF16)
        u = u_ref[...]
        ub = u.astype(BF16)
        gr[...] = lax.dot_general(dyb, cr_ref[...].astype(BF16), nt_dims, preferred_element_type=F32)
        gi[...] = -lax.dot_general(dyb, ci_ref[...].astype(BF16), nt_dims, preferred_element_type=F32)
        first = tb(i) == 0
        ext_r[0:8, :] = jnp.where(first, 0.0, xpr_ref[...])
        ext_i[0:8, :] = jnp.where(first, 0.0, xpi_ref[...])
        ext_r[8:8 + tt, :] = xr_ref[...]
        ext_i[8:8 + tt, :] = xi_ref[...]
        rows = lax.broadcasted_iota(jnp.int32, (8, WS), 0)

        def lam_grad(j, g_r, g_i, carry):
            a_r, a_i = carry
            cur = pl.ds(pl.multiple_of(j * 8 + 8, 8), 8)
            prv = pl.ds(pl.multiple_of(j * 8, 8), 8)
            xc_r, xc_i = ext_r[cur, :], ext_i[cur, :]
            xl_r, xl_i = ext_r[prv, :], ext_i[prv, :]
            xp_r = jnp.where(rows == 0, jnp.broadcast_to(xl_r[7:8, :], (8, WS)), pltpu.roll(xc_r, 1, 0))
            xp_i = jnp.where(rows == 0, jnp.broadcast_to(xl_i[7:8, :], (8, WS)), pltpu.roll(xc_i, 1, 0))
            return (a_r + g_r * xp_r + g_i * xp_i, a_i + g_i * xp_r - g_r * xp_i)

        tile = _scan_block(gr, gi, gr, gi, car_r, car_i, pwr_ref, pwi_ref, cwr_ref, cwi_ref, tt // 8, True, extra=lam_grad)
        z8 = jnp.zeros((8, WS), F32)
        a_r, a_i = lax.fori_loop(0, tt // 8, tile, (z8, z8))
        dar_ref[...] += a_r
        dai_ref[...] += a_i
        grb = gr[...].astype(BF16)
        gib = gi[...].astype(BF16)
        dbr_ref[...] += lax.dot_general(ub, grb, tn_dims, preferred_element_type=F32)
        dbi_ref[...] += lax.dot_general(ub, gib, tn_dims, preferred_element_type=F32)
        dcr_ref[...] += lax.dot_general(xr_ref[...].astype(BF16), dyb, tn_dims, preferred_element_type=F32)
        dci_ref[...] -= lax.dot_general(xi_ref[...].astype(BF16), dyb, tn_dims, preferred_element_type=F32)
        du = (lax.dot_general(grb, br_ref[...].astype(BF16), nt_dims, preferred_element_type=F32)
              + lax.dot_general(gib, bi_ref[...].astype(BF16), nt_dims, preferred_element_type=F32)
              + d_ref[...] * dy)
        du_ref[...] = du.astype(BF16)
        dd_ref[...] += _rowsum8(dy * u)

    wspec = lambda shp: pl.BlockSpec((None,) + shp, lambda b, i: (b, 0, 0))
    return pl.pallas_call(
        body, grid=(SSM_BLOCKS, nt),
        in_specs=[pl.BlockSpec((tt, WI), lambda b, i: (tb(i), b)),
                  pl.BlockSpec((tt, WI), lambda b, i: (tb(i), u0 + b)),
                  pl.BlockSpec((tt, WS), lambda b, i: (tb(i), b)), pl.BlockSpec((tt, WS), lambda b, i: (tb(i), b)),
                  pl.BlockSpec((8, WS), lambda b, i: (xprev(i), b)), pl.BlockSpec((8, WS), lambda b, i: (xprev(i), b)),
                  wspec((WI, WS)), wspec((WI, WS)), wspec((WS, WI)), wspec((WS, WI)),
                  pl.BlockSpec((8, WS), lambda b, i: (0, b)), pl.BlockSpec((8, WS), lambda b, i: (0, b)),
                  pl.BlockSpec((8, WS), lambda b, i: (0, b)), pl.BlockSpec((8, WS), lambda b, i: (0, b)),
                  pl.BlockSpec((1, WI), lambda b, i: (0, b))],
        out_specs=[pl.BlockSpec((tt, WI), lambda b, i: (tb(i), b)),
                   wspec((WI, WS)), wspec((WI, WS)), wspec((WS, WI)), wspec((WS, WI)),
                   pl.BlockSpec((8, WS), lambda b, i: (0, b)), pl.BlockSpec((8, WS), lambda b, i: (0, b)),
                   pl.BlockSpec((8, WI), lambda b, i: (0, b))],
        out_shape=[jax.ShapeDtypeStruct((T, D_SSM), BF16),
                   jax.ShapeDtypeStruct((SSM_BLOCKS, WI, WS), F32), jax.ShapeDtypeStruct((SSM_BLOCKS, WI, WS), F32),
                   jax.ShapeDtypeStruct((SSM_BLOCKS, WS, WI), F32), jax.ShapeDtypeStruct((SSM_BLOCKS, WS, WI), F32),
                   jax.ShapeDtypeStruct((8, SSM_BLOCKS * WS), F32), jax.ShapeDtypeStruct((8, SSM_BLOCKS * WS), F32),
                   jax.ShapeDtypeStruct((8, D_SSM), F32)],
        scratch_shapes=[pltpu.VMEM((tt, WS), F32), pltpu.VMEM((tt, WS), F32),
                        pltpu.VMEM((tt + 8, WS), F32), pltpu.VMEM((tt + 8, WS), F32),
                        pltpu.VMEM((8, WS), F32), pltpu.VMEM((8, WS), F32)],
        compiler_params=_cparams(("arbitrary", "arbitrary")), name=name,
    )(dy, p, xr, xi, xr, xi, Br, Bi, Cr, Ci, pwc_r, pwc_i, cwc_r, cwc_i, dvec)


SSM_SEGS = 8


def seg_perm(a, tt):
    T, C = a.shape
    return a.reshape(T // tt, SSM_SEGS, tt // SSM_SEGS, C).transpose(0, 2, 1, 3).reshape(T, C)


def seg_unperm(a, tt):
    T, C = a.shape
    return a.reshape(T // tt, tt // SSM_SEGS, SSM_SEGS, C).transpose(0, 2, 1, 3).reshape(T, C)


def _ssm_tt(T):
    return _pick(T, [512, 256, 128])


def _seg_gather(src_ref, dst_ref, sl):
    for j in range(sl):
        dst_ref[8 * j:8 * j + 8, :] = src_ref[pl.ds(j, SSM_SEGS, stride=sl), :]


def _seg_scatter(val, dst_ref, sl):
    for j in range(sl):
        dst_ref[pl.ds(j, SSM_SEGS, stride=sl), :] = val[8 * j:8 * j + 8, :]


def _seg_tables(ar_ref, ai_ref, conj, pb_r, pb_i, pw_r, pw_i, cw_r, cw_i, sl):
    W = ar_ref.shape[1]
    lr = jnp.broadcast_to(ar_ref[...], (8, W))
    li = jnp.broadcast_to(ai_ref[...], (8, W))
    if conj:
        li = -li

    def power(j, cur):
        cr, ci = cur
        pb_r[j] = cr
        pb_i[j] = ci
        return cr * lr - ci * li, cr * li + ci * lr

    lax.fori_loop(0, sl, power, (lr, li))
    br, bi = pb_r[sl - 1], pb_i[sl - 1]
    rows = lax.broadcasted_iota(jnp.int32, (8, W), 0)
    cr, ci = br, bi
    tr, ti = jnp.zeros((8, W), F32), jnp.zeros((8, W), F32)
    ur, ui = tr, ti
    for r in range(8):
        tr, ti = jnp.where(rows == r, cr, tr), jnp.where(rows == r, ci, ti)
        ur, ui = jnp.where(rows == 7 - r, cr, ur), jnp.where(rows == 7 - r, ci, ui)
        cr, ci = cr * br - ci * bi, cr * bi + ci * br
    pw_r[...] = tr
    pw_i[...] = ti
    cw_r[...] = ur
    cw_i[...] = ui


def ssm_seg_fwd(p, Br, Bi, Cr, Ci, ar, ai, dvec, *, name):
    T = p.shape[0]
    tt = _ssm_tt(T)
    nt = T // tt
    sl = tt // SSM_SEGS
    WI, WS = SSM_BLOCK_IN, SSM_BLOCK_STATE
    u0 = 2 * D_MODEL // WI

    def body(u_ref, br_ref, bi_ref, cr_ref, ci_ref, ar_ref, ai_ref, d_ref, xr_ref, xi_ref, y_ref,
             bur, bui, useg, ynat, pb_r, pb_i, pw_r, pw_i, cw_r, cw_i, end_r, end_i, car_r, car_i):
        i = pl.program_id(1)

        @pl.when(i == 0)
        def _():
            _seg_tables(ar_ref, ai_ref, False, pb_r, pb_i, pw_r, pw_i, cw_r, cw_i, sl)
            car_r[...] = jnp.zeros_like(car_r)
            car_i[...] = jnp.zeros_like(car_i)

        _seg_gather(u_ref, useg, sl)
        u = useg[...]
        ub = u.astype(BF16)
        bur[...] = jnp.dot(ub, br_ref[...].astype(BF16), preferred_element_type=F32)
        bui[...] = jnp.dot(ub, bi_ref[...].astype(BF16), preferred_element_type=F32)
        lr = jnp.broadcast_to(ar_ref[...], (8, WS))
        li = jnp.broadcast_to(ai_ref[...], (8, WS))

        def step(j, st):
            sr, si = st
            rw = pl.ds(pl.multiple_of(j * 8, 8), 8)
            nr = lr * sr - li * si + bur[rw, :]
            ni = lr * si + li * sr + bui[rw, :]
            xr_ref[rw, :] = nr
            xi_ref[rw, :] = ni
            return nr, ni

        z8 = jnp.zeros((8, WS), F32)
        end_r[...], end_i[...] = lax.fori_loop(0, sl, step, (z8, z8))
        old_r, old_i = car_r[...], car_i[...]
        _scan_block(end_r, end_i, end_r, end_i, car_r, car_i, pw_r, pw_i, pw_r, pw_i, 1, False)(0, 0)
        rows = lax.broadcasted_iota(jnp.int32, (8, WS), 0)
        s_r = jnp.where(rows == 0, old_r, pltpu.roll(end_r[...], 1, 0))
        s_i = jnp.where(rows == 0, old_i, pltpu.roll(end_i[...], 1, 0))

        def fix(j, c):
            rw = pl.ds(pl.multiple_of(j * 8, 8), 8)
            pr, pi = pb_r[j], pb_i[j]
            xr_ref[rw, :] = xr_ref[rw, :] + pr * s_r - pi * s_i
            xi_ref[rw, :] = xi_ref[rw, :] + pr * s_i + pi * s_r
            return c

        lax.fori_loop(0, sl, fix, 0)
        y = (jnp.dot(xr_ref[...].astype(BF16), cr_ref[...].astype(BF16), preferred_element_type=F32)
             - jnp.dot(xi_ref[...].astype(BF16), ci_ref[...].astype(BF16), preferred_element_type=F32)
             + d_ref[...] * u)
        _seg_scatter(y, ynat, sl)
        y_ref[...] = ynat[...].astype(BF16)

    wspec = lambda shp: pl.BlockSpec((None,) + shp, lambda b, i: (b, 0, 0))
    vec = lambda w: pl.BlockSpec((1, w), lambda b, i: (0, b))
    tile8 = pltpu.VMEM((8, WS), F32)
    return pl.pallas_call(
        body, grid=(SSM_BLOCKS, nt),
        in_specs=[pl.BlockSpec((tt, WI), lambda b, i: (i, u0 + b)), wspec((WI, WS)), wspec((WI, WS)), wspec((WS, WI)),
                  wspec((WS, WI)), vec(WS), vec(WS), vec(WI)],
        out_specs=[pl.BlockSpec((tt, WS), lambda b, i: (i, b)), pl.BlockSpec((tt, WS), lambda b, i: (i, b)),
                   pl.BlockSpec((tt, WI), lambda b, i: (i, b))],
        out_shape=[jax.ShapeDtypeStruct((T, SSM_BLOCKS * WS), F32)] * 2 + [jax.ShapeDtypeStruct((T, D_SSM), BF16)],
        scratch_shapes=[pltpu.VMEM((tt, WS), F32), pltpu.VMEM((tt, WS), F32),
                        pltpu.VMEM((tt, WI), F32), pltpu.VMEM((tt, WI), F32),
                        pltpu.VMEM((sl, 8, WS), F32), pltpu.VMEM((sl, 8, WS), F32)] + [tile8] * 8,
        compiler_params=_cparams(("arbitrary", "arbitrary")), name=name)(p, Br, Bi, Cr, Ci, ar, ai, dvec)


def ssm_seg_bwd(dy, u, xr, xi, Br, Bi, Cr, Ci, ar, ai, dvec, *, name):
    T = u.shape[0]
    tt = _ssm_tt(T)
    nt = T // tt
    sl = tt // SSM_SEGS
    WI, WS = SSM_BLOCK_IN, SSM_BLOCK_STATE
    u0 = 2 * D_MODEL // WI
    tb = lambda i: nt - 1 - i
    xprev = lambda i: jnp.maximum(tb(i) * (tt // 8) - 1, 0)
    tn_dims = _DIMS["tn"]
    nt_dims = _DIMS["nt"]

    def body(dyn_ref, un_ref, xr_ref, xi_ref, xpr_ref, xpi_ref, br_ref, bi_ref, cr_ref, ci_ref, ar_ref, ai_ref, d_ref,
             du_ref, dbr_ref, dbi_ref, dcr_ref, dci_ref, dar_ref, dai_ref, dd_ref,
             gr, gi, ext_r, ext_i, dy_ref, u_ref, dunat, pb_r, pb_i, pw_r, pw_i, cw_r, cw_i, end_r, end_i, car_r, car_i):
        _seg_gather(dyn_ref, dy_ref, sl)
        _seg_gather(un_ref, u_ref, sl)
        i = pl.program_id(1)

        @pl.when(i == 0)
        def _():
            _seg_tables(ar_ref, ai_ref, True, pb_r, pb_i, pw_r, pw_i, cw_r, cw_i, sl)
            car_r[...] = jnp.zeros_like(car_r)
            car_i[...] = jnp.zeros_like(car_i)
            dbr_ref[...] = jnp.zeros_like(dbr_ref)
            dbi_ref[...] = jnp.zeros_like(dbi_ref)
            dcr_ref[...] = jnp.zeros_like(dcr_ref)
            dci_ref[...] = jnp.zeros_like(dci_ref)
            dar_ref[...] = jnp.zeros_like(dar_ref)
            dai_ref[...] = jnp.zeros_like(dai_ref)
            dd_ref[...] = jnp.zeros_like(dd_ref)

        dy = dy_ref[...]
        dyb = dy.astype(BF16)
        u = u_ref[...]
        ub = u.astype(BF16)
        gr[...] = lax.dot_general(dyb, cr_ref[...].astype(BF16), nt_dims, preferred_element_type=F32)
        gi[...] = -lax.dot_general(dyb, ci_ref[...].astype(BF16), nt_dims, preferred_element_type=F32)
        lr = jnp.broadcast_to(ar_ref[...], (8, WS))
        li = -jnp.broadcast_to(ai_ref[...], (8, WS))
        rows = lax.broadcasted_iota(jnp.int32, (8, WS), 0)

        def step(jj, st):
            sr, si = st
            rw = pl.ds(pl.multiple_of((sl - 1 - jj) * 8, 8), 8)
            nr = lr * sr - li * si + gr[rw, :]
            ni = lr * si + li * sr + gi[rw, :]
            gr[rw, :] = nr
            gi[rw, :] = ni
            return nr, ni

        z8 = jnp.zeros((8, WS), F32)
        end_r[...], end_i[...] = lax.fori_loop(0, sl, step, (z8, z8))
        old_r, old_i = car_r[...], car_i[...]
        _scan_block(end_r, end_i, end_r, end_i, car_r, car_i, pw_r, pw_i, cw_r, cw_i, 1, True)(0, 0)
        s_r = jnp.where(rows == 7, old_r, pltpu.roll(end_r[...], 7, 0))
        s_i = jnp.where(rows == 7, old_i, pltpu.roll(end_i[...], 7, 0))
        first = tb(i) == 0
        last_r, last_i = xr_ref[tt - 8:tt, :], xi_ref[tt - 8:tt, :]
        pv_r = jnp.where(first, 0.0, xpr_ref[...])
        pv_i = jnp.where(first, 0.0, xpi_ref[...])
        ext_r[0:8, :] = jnp.where(rows == 0, jnp.broadcast_to(pv_r[7:8, :], (8, WS)), pltpu.roll(last_r, 1, 0))
        ext_i[0:8, :] = jnp.where(rows == 0, jnp.broadcast_to(pv_i[7:8, :], (8, WS)), pltpu.roll(last_i, 1, 0))
        ext_r[8:8 + tt, :] = xr_ref[...]
        ext_i[8:8 + tt, :] = xi_ref[...]

        def fix(j, acc):
            a_r, a_i = acc
            rw = pl.ds(pl.multiple_of(j * 8, 8), 8)
            pr, pi = pb_r[sl - 1 - j], pb_i[sl - 1 - j]
            g_r = gr[rw, :] + pr * s_r - pi * s_i
            g_i = gi[rw, :] + pr * s_i + pi * s_r
            gr[rw, :] = g_r
            gi[rw, :] = g_i
            xp_r, xp_i = ext_r[rw, :], ext_i[rw, :]
            return a_r + g_r * xp_r + g_i * xp_i, a_i + g_i * xp_r - g_r * xp_i

        a_r, a_i = lax.fori_loop(0, sl, fix, (z8, z8))
        dar_ref[...] += a_r
        dai_ref[...] += a_i
        grb = gr[...].astype(BF16)
        gib = gi[...].astype(BF16)
        dbr_ref[...] += lax.dot_general(ub, grb, tn_dims, preferred_element_type=F32)
        dbi_ref[...] += lax.dot_general(ub, gib, tn_dims, preferred_element_type=F32)
        dcr_ref[...] += lax.dot_general(xr_ref[...].astype(BF16), dyb, tn_dims, preferred_element_type=F32)
        dci_ref[...] -= lax.dot_general(xi_ref[...].astype(BF16), dyb, tn_dims, preferred_element_type=F32)
        du = (lax.dot_general(grb, br_ref[...].astype(BF16), nt_dims, preferred_element_type=F32)
              + lax.dot_general(gib, bi_ref[...].astype(BF16), nt_dims, preferred_element_type=F32)
              + d_ref[...] * dy)
        _seg_scatter(du, dunat, sl)
        du_ref[...] = dunat[...].astype(BF16)
        dd_ref[...] += _rowsum8(dy * u)

    wspec = lambda shp: pl.BlockSpec((None,) + shp, lambda b, i: (b, 0, 0))
    vec = lambda w: pl.BlockSpec((1, w), lambda b, i: (0, b))
    tile8 = pltpu.VMEM((8, WS), F32)
    return pl.pallas_call(
        body, grid=(SSM_BLOCKS, nt),
        in_specs=[pl.BlockSpec((tt, WI), lambda b, i: (tb(i), b)), pl.BlockSpec((tt, WI), lambda b, i: (tb(i), u0 + b)),
                  pl.BlockSpec((tt, WS), lambda b, i: (tb(i), b)), pl.BlockSpec((tt, WS), lambda b, i: (tb(i), b)),
                  pl.BlockSpec((8, WS), lambda b, i: (xprev(i), b)), pl.BlockSpec((8, WS), lambda b, i: (xprev(i), b)),
                  wspec((WI, WS)), wspec((WI, WS)), wspec((WS, WI)), wspec((WS, WI)), vec(WS), vec(WS), vec(WI)],
        out_specs=[pl.BlockSpec((tt, WI), lambda b, i: (tb(i), b)),
                   wspec((WI, WS)), wspec((WI, WS)), wspec((WS, WI)), wspec((WS, WI)),
                   pl.BlockSpec((8, WS), lambda b, i: (0, b)), pl.BlockSpec((8, WS), lambda b, i: (0, b)),
                   pl.BlockSpec((8, WI), lambda b, i: (0, b))],
        out_shape=[jax.ShapeDtypeStruct((T, D_SSM), BF16),
                   jax.ShapeDtypeStruct((SSM_BLOCKS, WI, WS), F32), jax.ShapeDtypeStruct((SSM_BLOCKS, WI, WS), F32),
                   jax.ShapeDtypeStruct((SSM_BLOCKS, WS, WI), F32), jax.ShapeDtypeStruct((SSM_BLOCKS, WS, WI), F32),
                   jax.ShapeDtypeStruct((8, SSM_BLOCKS * WS), F32), jax.ShapeDtypeStruct((8, SSM_BLOCKS * WS), F32),
                   jax.ShapeDtypeStruct((8, D_SSM), F32)],
        scratch_shapes=[pltpu.VMEM((tt, WS), F32), pltpu.VMEM((tt, WS), F32),
                        pltpu.VMEM((tt + 8, WS), F32), pltpu.VMEM((tt + 8, WS), F32),
                        pltpu.VMEM((tt, WI), F32), pltpu.VMEM((tt, WI), F32), pltpu.VMEM((tt, WI), F32),
                        pltpu.VMEM((sl, 8, WS), F32), pltpu.VMEM((sl, 8, WS), F32)] + [tile8] * 8,
        compiler_params=_cparams(("arbitrary", "arbitrary")), name=name,
    )(dy, u, xr, xi, xr, xi, Br, Bi, Cr, Ci, ar, ai, dvec)


def _ssm_discretise(log_step, lam_re, lam_im, b_re, b_im):
    step = jnp.exp(log_step)[:, None]
    mag = jnp.exp(lam_re * step)
    ar = mag * jnp.cos(lam_im * step)
    ai = mag * jnp.sin(lam_im * step)
    den = lam_re * lam_re + lam_im * lam_im
    nr = ar - 1.0
    cr = (nr * lam_re + ai * lam_im) / den
    ci = (ai * lam_re - nr * lam_im) / den
    bbr = cr[..., None] * b_re - ci[..., None] * b_im
    bbi = cr[..., None] * b_im + ci[..., None] * b_re
    return ar, ai, bbr, bbi


def _blockdiag_in(bb):
    t = jnp.transpose(bb, (0, 2, 1)).reshape(SSM_BLOCKS, 8, SSM_GROUP, SSM_STATE)
    eye = jnp.eye(8, dtype=bb.dtype)
    return (t[:, :, :, None, :] * eye[None, :, None, :, None]).reshape(SSM_BLOCKS, SSM_BLOCK_IN, SSM_BLOCK_STATE)


def _blockdiag_out(cc):
    t = jnp.transpose(cc, (0, 2, 1)).reshape(SSM_BLOCKS, 8, SSM_STATE, SSM_GROUP)
    eye = jnp.eye(8, dtype=cc.dtype)
    return (t[:, :, :, None, :] * eye[None, :, None, :, None]).reshape(SSM_BLOCKS, SSM_BLOCK_STATE, SSM_BLOCK_IN)


def _diag_in(d):
    t = d.reshape(SSM_BLOCKS, 8, SSM_GROUP, 8, SSM_STATE)
    t = jnp.einsum("bghgp->bghp", t).reshape(SSM_GROUPS, SSM_GROUP, SSM_STATE)
    return jnp.transpose(t, (0, 2, 1))


def _diag_out(d):
    t = d.reshape(SSM_BLOCKS, 8, SSM_STATE, 8, SSM_GROUP)
    t = jnp.einsum("bgpgh->bgph", t).reshape(SSM_GROUPS, SSM_STATE, SSM_GROUP)
    return jnp.transpose(t, (0, 2, 1))


def _powers(ar, ai):
    rs, is_ = [ar], [ai]
    for _ in range(7):
        r, i = rs[-1], is_[-1]
        rs.append(r * ar - i * ai)
        is_.append(r * ai + i * ar)
    return jnp.stack(rs), jnp.stack(is_), jnp.stack(rs[::-1]), jnp.stack(is_[::-1])


def attn_fwd(q, kv, *, name):
    T, D = q.shape
    nm = kv.shape[0]
    tq = _pick(T, [512, 256, 128])
    scale = HEAD_DIM ** -0.5

    def body(q_ref, k_ref, v_ref, o_ref):
        for h in range(N_HEADS):
            sl = slice(h * HEAD_DIM, (h + 1) * HEAD_DIM)
            s = lax.dot_general(q_ref[:, sl], k_ref[:, sl].astype(BF16), _DIMS["nt"], preferred_element_type=F32) * scale
            e = jnp.exp(s - jnp.max(s, axis=-1, keepdims=True))
            pr = e / jnp.sum(e, axis=-1, keepdims=True)
            o_ref[:, sl] = jnp.dot(pr.astype(BF16), v_ref[:, sl].astype(BF16), preferred_element_type=F32).astype(BF16)

    return pl.pallas_call(
        body, grid=(T // tq,),
        in_specs=[_rows(tq, D), pl.BlockSpec((nm, D), lambda i: (0, 0)), pl.BlockSpec((nm, D), lambda i: (0, 1))],
        out_specs=_rows(tq, D), out_shape=jax.ShapeDtypeStruct((T, D), BF16),
        compiler_params=_cparams(("arbitrary",)), name=name)(q, kv, kv)


def attn_bwd(q, kv, do, *, name):
    T, D = q.shape
    nm = kv.shape[0]
    tq = _pick(T, [512, 256, 128])
    nt = T // tq
    scale = HEAD_DIM ** -0.5

    def body(q_ref, k_ref, v_ref, do_ref, dq_ref, dkv_ref):
        i = pl.program_id(0)

        @pl.when(i == 0)
        def _():
            dkv_ref[...] = jnp.zeros_like(dkv_ref)

        for h in range(N_HEADS):
            sl = slice(h * HEAD_DIM, (h + 1) * HEAD_DIM)
            slv = slice(D + h * HEAD_DIM, D + (h + 1) * HEAD_DIM)
            qh = q_ref[:, sl]
            kh = k_ref[:, sl].astype(BF16)
            vh = v_ref[:, sl].astype(BF16)
            doh = do_ref[:, sl].astype(BF16)
            s = lax.dot_general(qh, kh, _DIMS["nt"], preferred_element_type=F32) * scale
            e = jnp.exp(s - jnp.max(s, axis=-1, keepdims=True))
            pr = e / jnp.sum(e, axis=-1, keepdims=True)
            dp = lax.dot_general(doh, vh, _DIMS["nt"], preferred_element_type=F32)
            ds = (pr * (dp - jnp.sum(pr * dp, axis=-1, keepdims=True)) * scale).astype(BF16)
            dq_ref[:, sl] = jnp.dot(ds, kh, preferred_element_type=F32).astype(BF16)
            dkv_ref[:, sl] += lax.dot_general(ds, qh, _DIMS["tn"], preferred_element_type=F32)
            dkv_ref[:, slv] += lax.dot_general(pr.astype(BF16), doh, _DIMS["tn"], preferred_element_type=F32)

    return pl.pallas_call(
        body, grid=(nt,),
        in_specs=[_rows(tq, D), pl.BlockSpec((nm, D), lambda i: (0, 0)), pl.BlockSpec((nm, D), lambda i: (0, 1)), _rows(tq, D)],
        out_specs=[_rows(tq, D), _const((nm, 2 * D))],
        out_shape=[jax.ShapeDtypeStruct((T, D), BF16), jax.ShapeDtypeStruct((nm, 2 * D), F32)],
        compiler_params=_cparams(("arbitrary",)), name=name)(q, kv, kv, do)


def _adam_math(w, g, m, v):
    m = ADAM_B1 * m + (1.0 - ADAM_B1) * g
    v = ADAM_B2 * v + (1.0 - ADAM_B2) * (g * g)
    m_hat = m / (1.0 - ADAM_B1 ** ADAM_STEP)
    v_hat = v / (1.0 - ADAM_B2 ** ADAM_STEP)
    delta = -ADAM_LR * (m_hat / (jnp.sqrt(v_hat) + ADAM_EPS) + ADAM_WD * w)
    return delta, m, v


def adamw(w, m, v, g_arr, g_row0, *, name):
    R, C = w.shape
    tr = _pick(R, [512, 256, 128, 64, 32, 16, 8])
    assert g_row0 % tr == 0
    g0 = g_row0 // tr

    def body(w_ref, m_ref, v_ref, g_ref, go_ref, d_ref, mo_ref, vo_ref):
        g = g_ref[...]
        d, mn, vn = _adam_math(w_ref[...], g, m_ref[...], v_ref[...])
        go_ref[...] = g
        d_ref[...] = d
        mo_ref[...] = mn
        vo_ref[...] = vn

    sp = pl.BlockSpec((tr, C), lambda i: (i, 0))
    return pl.pallas_call(
        body, grid=(R // tr,), in_specs=[sp, sp, sp, pl.BlockSpec((tr, C), lambda i: (g0 + i, 0))],
        out_specs=[sp] * 4, out_shape=[jax.ShapeDtypeStruct((R, C), F32)] * 4,
        compiler_params=_cparams(("arbitrary",)), name=name)(w, m, v, g_arr)


def _place():
    x, y, c = lax.axis_index("x"), lax.axis_index("y"), lax.axis_index("c")
    chips = [(1 - x, y), (x, 1 - y), (1 - x, 1 - y)]
    return x, y, c, chips


ANY = pl.BlockSpec(memory_space=pl.ANY)


def allgather_weights(bufs, *, name):
    n = len(bufs)

    def body(*refs):
        o_refs = refs[n:2 * n]
        send_sems, recv_sems, fsend_sems, frecv_sems = refs[2 * n:]
        x, y, c, chips = _place()
        k_me = 2 * x + y
        sib = (x, y, 1 - c)
        halves = [b.shape[1] // 2 for b in bufs]

        def half(a, cc):
            return pl.ds(pl.multiple_of(cc * halves[a], 16), halves[a])

        sends = []
        for a in range(n):
            for r, (px, py) in enumerate(chips):
                cp = pltpu.make_async_remote_copy(
                    src_ref=o_refs[a].at[k_me, half(a, c)], dst_ref=o_refs[a].at[k_me, half(a, c)],
                    send_sem=send_sems.at[3 * a + r], recv_sem=recv_sems.at[3 * a + r],
                    device_id=(px, py, c), device_id_type=MESH)
                cp.start()
                sends.append(cp)
        passed = []
        for a in range(n):
            for r, (px, py) in enumerate(chips):
                win = o_refs[a].at[2 * px + py, half(a, c)]
                pltpu.make_async_remote_copy(
                    src_ref=win, dst_ref=win, send_sem=send_sems.at[3 * a + r], recv_sem=recv_sems.at[3 * a + r],
                    device_id=(px, py, c), device_id_type=MESH).wait_recv()
                cp = pltpu.make_async_remote_copy(
                    src_ref=win, dst_ref=win, send_sem=fsend_sems.at[3 * a + r], recv_sem=frecv_sems.at[3 * a + r],
                    device_id=sib, device_id_type=MESH)
                cp.start()
                passed.append(cp)
        for a in range(n):
            for r, (px, py) in enumerate(chips):
                win = o_refs[a].at[2 * px + py, half(a, 1 - c)]
                pltpu.make_async_remote_copy(
                    src_ref=win, dst_ref=win, send_sem=fsend_sems.at[3 * a + r], recv_sem=frecv_sems.at[3 * a + r],
                    device_id=sib, device_id_type=MESH).wait_recv()
        for cp in sends + passed:
            cp.wait_send()

    return pl.pallas_call(
        body, in_specs=[ANY] * n, out_specs=[ANY] * n,
        out_shape=[jax.ShapeDtypeStruct(b.shape, b.dtype) for b in bufs],
        scratch_shapes=[pltpu.SemaphoreType.DMA((3 * n,))] * 4,
        input_output_aliases={a: a for a in range(n)},
        name=name)(*bufs)


HBM_SPEC = pl.BlockSpec(memory_space=pltpu.HBM)
SEM_SPEC = pl.BlockSpec(memory_space=pltpu.SEMAPHORE)


def _hbm(a):
    return pltpu.with_memory_space_constraint(a, pltpu.HBM)


def gather_start(bufs, pieces, *, name):
    n = len(bufs)
    npc = len(pieces)

    def body(*refs):
        b_refs = refs[:n]
        send_sems, recv_sems = refs[n], refs[n + 1]
        x, y, c, chips = _place()
        k_me = 2 * x + y
        for q, (a, row0, rows) in enumerate(pieces):
            win = b_refs[a].at[k_me, pl.ds(row0, rows)]
            for r, (px, py) in enumerate(chips):
                pltpu.make_async_remote_copy(
                    src_ref=win, dst_ref=win, send_sem=send_sems.at[3 * q + r], recv_sem=recv_sems.at[3 * q + r],
                    device_id=(px, py, c), device_id_type=MESH).start()

    return pl.pallas_call(
        body, in_specs=[HBM_SPEC] * n, out_specs=[SEM_SPEC, SEM_SPEC] + [HBM_SPEC] * n,
        out_shape=[pltpu.SemaphoreType.DMA((3 * npc,)), pltpu.SemaphoreType.DMA((3 * npc,))]
        + [pltpu.HBM(b.shape, b.dtype) for b in bufs],
        input_output_aliases={a: 2 + a for a in range(n)},
        compiler_params=pltpu.CompilerParams(has_side_effects=pltpu.SideEffectType.DATAFLOW_SIDE_EFFECTING),
        name=name)(*[_hbm(b) for b in bufs])


def gather_wait(send_sems, recv_sems, bufs, which, after, *, name):
    n = len(bufs)

    def body(*refs):
        b_refs = refs[:n]
        send_sems, recv_sems = refs[n], refs[n + 1]
        x, y, c, chips = _place()
        k_me = 2 * x + y
        for a, row0, rows, q in which:
            for r, (px, py) in enumerate(chips):
                cp = pltpu.make_async_remote_copy(
                    src_ref=b_refs[a].at[k_me, pl.ds(row0, rows)], dst_ref=b_refs[a].at[2 * px + py, pl.ds(row0, rows)],
                    send_sem=send_sems.at[3 * q + r], recv_sem=recv_sems.at[3 * q + r],
                    device_id=(px, py, c), device_id_type=MESH)
                cp.wait_send()
                cp.wait_recv()

    return pl.pallas_call(
        body, in_specs=[HBM_SPEC] * n + [SEM_SPEC, SEM_SPEC, ANY], out_specs=[HBM_SPEC] * n,
        out_shape=[pltpu.HBM(b.shape, b.dtype) for b in bufs],
        input_output_aliases={a: a for a in range(n)},
        compiler_params=pltpu.CompilerParams(has_side_effects=pltpu.SideEffectType.DATAFLOW_SIDE_EFFECTING),
        name=name)(*bufs, send_sems, recv_sems, after)


def exchange_halves(grads, *, name):
    n = len(grads)

    def body(*refs):
        g_refs, l_refs = refs[:n], refs[n:2 * n]
        send_sems, recv_sems = refs[2 * n:]
        x, y, c, _ = _place()
        cps = []
        for a in range(n):
            h = grads[a].shape[1] // 2
            cp = pltpu.make_async_remote_copy(
                src_ref=g_refs[a].at[:, pl.ds(pl.multiple_of((1 - c) * h, 8), h)], dst_ref=l_refs[a],
                send_sem=send_sems.at[a], recv_sem=recv_sems.at[a], device_id=(x, y, 1 - c), device_id_type=MESH)
            cp.start()
            cps.append(cp)
        for cp in cps:
            cp.wait()

    return pl.pallas_call(
        body, in_specs=[ANY] * n, out_specs=[ANY] * n,
        out_shape=[jax.ShapeDtypeStruct((g.shape[0], g.shape[1] // 2, g.shape[2]), g.dtype) for g in grads],
        scratch_shapes=[pltpu.SemaphoreType.DMA((n,))] * 2,
        name=name)(*grads)


N_PEERS = N_DEV - 1


def _scatter_copies(p_refs, l_refs, send_sems, recv_sems):
    x, y, c, _ = _place()
    cps = []
    for a in range(len(p_refs)):
        h = p_refs[a].shape[1] // 2
        for fx, fy in ((0, 0), (1, 0), (0, 1), (1, 1)):
            for fc in (0, 1):
                if (fx, fy, fc) == (0, 0, 0):
                    continue
                slot = 2 * (fx + 2 * fy) + fc - 1
                px, py, pc = (1 - x if fx else x), (1 - y if fy else y), (1 - c if fc else c)
                cps.append(pltpu.make_async_remote_copy(
                    src_ref=p_refs[a].at[2 * px + py, pl.ds(pl.multiple_of(pc * h, 16), h)], dst_ref=l_refs[a].at[slot],
                    send_sem=send_sems.at[N_PEERS * a + slot], recv_sem=recv_sems.at[N_PEERS * a + slot],
                    device_id=(px, py, pc), device_id_type=MESH))
    return cps


def scatter_start(parts, *, name):
    n = len(parts)
    lands = [lax.empty((N_PEERS, p.shape[1] // 2, p.shape[2]), p.dtype) for p in parts]

    def body(*refs):
        for cp in _scatter_copies(refs[:n], refs[n:2 * n], refs[2 * n], refs[2 * n + 1]):
            cp.start()

    outs = pl.pallas_call(
        body, in_specs=[HBM_SPEC] * (2 * n), out_specs=[SEM_SPEC, SEM_SPEC] + [HBM_SPEC] * (2 * n),
        out_shape=[pltpu.SemaphoreType.DMA((N_PEERS * n,)), pltpu.SemaphoreType.DMA((N_PEERS * n,))]
        + [pltpu.HBM(a.shape, a.dtype) for a in parts + lands],
        input_output_aliases={a: 2 + a for a in range(2 * n)},
        compiler_params=pltpu.CompilerParams(has_side_effects=pltpu.SideEffectType.DATAFLOW_SIDE_EFFECTING),
        name=name)(*[_hbm(a) for a in parts + lands])
    return outs[0], outs[1], list(outs[2:2 + n]), list(outs[2 + n:])


def scatter_wait(rounds, after, *, name):
    sizes = [len(r[2]) for r in rounds]
    flat = [a for r in rounds for a in r[2] + r[3]]
    sems = [s for r in rounds for s in (r[0], r[1])]
    nflat = len(flat)

    def body(*refs):
        pos = 0
        for ri, n in enumerate(sizes):
            for cp in _scatter_copies(refs[pos:pos + n], refs[pos + n:pos + 2 * n], refs[nflat + 2 * ri], refs[nflat + 2 * ri + 1]):
                cp.wait_send()
                cp.wait_recv()
            pos += 2 * n

    outs = pl.pallas_call(
        body, in_specs=[HBM_SPEC] * nflat + [SEM_SPEC] * len(sems) + [ANY], out_specs=[HBM_SPEC] * nflat,
        out_shape=[pltpu.HBM(a.shape, a.dtype) for a in flat],
        input_output_aliases={a: a for a in range(nflat)},
        compiler_params=pltpu.CompilerParams(has_side_effects=pltpu.SideEffectType.DATAFLOW_SIDE_EFFECTING),
        name=name)(*flat, *sems, after)
    res, pos = [], 0
    for n in sizes:
        res.append((list(outs[pos:pos + n]), list(outs[pos + n:pos + 2 * n])))
        pos += 2 * n
    return res


def join_halves(fulls, *, name):
    n = len(fulls)

    def body(*refs):
        o_refs = refs[n:2 * n]
        send_sems, recv_sems = refs[2 * n:]
        x, y, c, _ = _place()
        cps = []
        for a in range(n):
            h = fulls[a].shape[0] // 2
            win = o_refs[a].at[pl.ds(pl.multiple_of(c * h, 8), h)]
            cp = pltpu.make_async_remote_copy(
                src_ref=win, dst_ref=win, send_sem=send_sems.at[a], recv_sem=recv_sems.at[a],
                device_id=(x, y, 1 - c), device_id_type=MESH)
            cp.start()
            cps.append(cp)
        for a in range(n):
            h = fulls[a].shape[0] // 2
            other = o_refs[a].at[pl.ds(pl.multiple_of((1 - c) * h, 8), h)]
            pltpu.make_async_remote_copy(
                src_ref=other, dst_ref=other, send_sem=send_sems.at[a], recv_sem=recv_sems.at[a],
                device_id=(x, y, 1 - c), device_id_type=MESH).wait_recv()
        for cp in cps:
            cp.wait_send()

    return pl.pallas_call(
        body, in_specs=[ANY] * n, out_specs=[ANY] * n,
        out_shape=[jax.ShapeDtypeStruct(f.shape, f.dtype) for f in fulls],
        scratch_shapes=[pltpu.SemaphoreType.DMA((n,))] * 2,
        input_output_aliases={a: a for a in range(n)},
        name=name)(*fulls)


def add_partials(part, land, kc, *, name):
    _, R, C = part.shape
    H = R // 2
    tr = _pick(H, [256, 128, 64, 32, 16])
    per = H // tr

    def body(kc_ref, p_ref, l_ref, o_ref):
        acc = p_ref[...].astype(F32)
        for s in range(N_PEERS):
            acc = acc + l_ref[s].astype(F32)
        o_ref[...] = acc

    return pl.pallas_call(
        body,
        grid_spec=pltpu.PrefetchScalarGridSpec(
            num_scalar_prefetch=1, grid=(per,),
            in_specs=[pl.BlockSpec((None, tr, C), lambda i, kc_ref: (kc_ref[0], kc_ref[1] * per + i, 0)),
                      pl.BlockSpec((N_PEERS, tr, C), lambda i, kc_ref: (0, i, 0))],
            out_specs=pl.BlockSpec((tr, C), lambda i, kc_ref: (kc_ref[1] * per + i, 0))),
        out_shape=jax.ShapeDtypeStruct((R, C), F32),
        compiler_params=_cparams(("arbitrary",)), name=name)(kc, part, land)


def allgather_sum(v, *, name):
    m_per, n = v.shape

    def body(x_ref, out_ref, sum_ref, send_sems, recv_sems, local_sem):
        x, y, c, chips = _place()
        me, sibling = (x, y, c), (x, y, 1 - c)

        def rows(px, py, pc):
            return out_ref.at[pl.ds(pl.multiple_of((4 * px + 2 * py + pc) * m_per, 8), m_per), :]

        def copy(k, block, to, src=None):
            return pltpu.make_async_remote_copy(
                src_ref=rows(*block) if src is None else src, dst_ref=rows(*block),
                send_sem=send_sems.at[k], recv_sem=recv_sems.at[k], device_id=to, device_id_type=MESH)

        mine = pltpu.make_async_copy(x_ref, rows(*me), local_sem)
        mine.start()
        first = [copy(0, me, sibling, src=x_ref)]
        first += [copy(1 + j, me, (*chip, c), src=x_ref) for j, chip in enumerate(chips)]
        for cp in first:
            cp.start()
        passed = [copy(4 + j, (*chip, c), sibling) for j, chip in enumerate(chips)]
        for j, chip in enumerate(chips):
            copy(1 + j, (*chip, c), me).wait_recv()
            passed[j].start()
        copy(0, sibling, me).wait_recv()
        for j, chip in enumerate(chips):
            copy(4 + j, (*chip, 1 - c), me).wait_recv()
        for cp in first + passed:
            cp.wait_send()
        mine.wait()
        acc = out_ref[0:m_per, :]
        for d in range(1, N_DEV):
            acc = acc + out_ref[d * m_per:(d + 1) * m_per, :]
        sum_ref[...] = acc

    vm = pl.BlockSpec(memory_space=pltpu.VMEM)
    return pl.pallas_call(
        body, in_specs=[vm], out_specs=[vm, vm],
        out_shape=[jax.ShapeDtypeStruct((N_DEV * m_per, n), v.dtype), jax.ShapeDtypeStruct((m_per, n), v.dtype)],
        scratch_shapes=[pltpu.SemaphoreType.DMA((7,)), pltpu.SemaphoreType.DMA((7,)), pltpu.SemaphoreType.DMA],
        compiler_params=pltpu.CompilerParams(vmem_limit_bytes=VMEM_LIMIT_BYTES), name=name)(v)


def allreduce_two_level(v, *, name):
    m, n = v.shape
    h = m // 2

    def body(x_ref, out_ref, sib_ref, chip_ref, sems_send, sems_recv):
        x, y, c, chips = _place()
        k_me = 2 * x + y
        sib = (x, y, 1 - c)
        mine = pl.ds(pl.multiple_of(c * h, 8), h)
        other = pl.ds(pl.multiple_of((1 - c) * h, 8), h)

        def copy(q, src, dst, to):
            return pltpu.make_async_remote_copy(src_ref=src, dst_ref=dst, send_sem=sems_send.at[q], recv_sem=sems_recv.at[q],
                                                device_id=to, device_id_type=MESH)

        first = copy(0, x_ref.at[other], sib_ref, sib)
        first.start()
        first.wait()
        chip_ref[k_me] = x_ref[mine, :] + sib_ref[...]
        sends = [copy(1 + r, chip_ref.at[k_me], chip_ref.at[k_me], (px, py, c)) for r, (px, py) in enumerate(chips)]
        for cp in sends:
            cp.start()
        for r, (px, py) in enumerate(chips):
            copy(1 + r, chip_ref.at[2 * px + py], chip_ref.at[2 * px + py], (px, py, c)).wait_recv()
        for cp in sends:
            cp.wait_send()
        total = ((chip_ref[0] + chip_ref[1]) + chip_ref[2]) + chip_ref[3]
        out_ref[mine, :] = total
        last = copy(4, out_ref.at[mine], out_ref.at[mine], sib)
        last.start()
        copy(4, out_ref.at[other], out_ref.at[other], sib).wait_recv()
        last.wait_send()

    vm = pl.BlockSpec(memory_space=pltpu.VMEM)
    return pl.pallas_call(
        body, in_specs=[vm], out_specs=vm, out_shape=jax.ShapeDtypeStruct((m, n), v.dtype),
        scratch_shapes=[pltpu.VMEM((h, n), v.dtype), pltpu.VMEM((N_CHIPS, h, n), v.dtype),
                        pltpu.SemaphoreType.DMA((5,)), pltpu.SemaphoreType.DMA((5,))],
        compiler_params=pltpu.CompilerParams(vmem_limit_bytes=VMEM_LIMIT_BYTES), name=name)(v)


PACK_W = D_MODEL


def _pack_rows(shape):
    return -(-math.prod(shape) // PACK_W)


def _pack(arrs):
    cols = []
    for a in arrs:
        f = a.reshape(-1)
        pad = (-f.shape[0]) % PACK_W
        cols.append((jnp.pad(f, (0, pad)) if pad else f).reshape(-1, PACK_W))
    out = jnp.concatenate(cols, axis=0)
    pad = (-out.shape[0]) % 16
    return jnp.pad(out, ((0, pad), (0, 0)))


def _unpack(buf, shapes):
    outs, r = [], 0
    for s in shapes:
        nel = math.prod(s)
        nr = _pack_rows(s)
        outs.append(buf[r:r + nr].reshape(-1)[:nel].reshape(s))
        r += nr
    return outs


GA_CONV_OUT, GA_MIX_OUT, GA_WQ, GA_WO, GA_DOWN, GA_UP, GA_ROWS = 0, 256, 512, 768, 1024, 2048, 3072
G1_DOWN, G1_UP, G1_ROWS = 0, 1024, 2048
G2_CONV_OUT, G2_MIX_OUT, G2_WQ, G2_WO, G2_ROWS = 0, 256, 512, 768, 1024


def kernel(x, mem, in_norm_g, in_norm_b, w_in, conv_dw, conv_db, conv_norm_g, conv_norm_b, w_conv_out, ssm_log_step, ssm_lambda_re, ssm_lambda_im, ssm_b_re, ssm_b_im, ssm_c_re, ssm_c_im, ssm_d, w_ssm_glu, w_mix_out, ln1_g, ln1_b, xa_wq, xa_wkv, xa_wo, ln2_g, ln2_b, mlp_w_up, mlp_w_down, ln3_g, ln3_b, loss_target, m_in_norm_g, m_in_norm_b, m_w_in, m_conv_dw, m_conv_db, m_conv_norm_g, m_conv_norm_b, m_w_conv_out, m_ssm_log_step, m_ssm_lambda_re, m_ssm_lambda_im, m_ssm_b_re, m_ssm_b_im, m_ssm_c_re, m_ssm_c_im, m_ssm_d, m_w_ssm_glu, m_w_mix_out, m_ln1_g, m_ln1_b, m_xa_wq, m_xa_wkv, m_xa_wo, m_ln2_g, m_ln2_b, m_mlp_w_up, m_mlp_w_down, m_ln3_g, m_ln3_b, v_in_norm_g, v_in_norm_b, v_w_in, v_conv_dw, v_conv_db, v_conv_norm_g, v_conv_norm_b, v_w_conv_out, v_ssm_log_step, v_ssm_lambda_re, v_ssm_lambda_im, v_ssm_b_re, v_ssm_b_im, v_ssm_c_re, v_ssm_c_im, v_ssm_d, v_w_ssm_glu, v_w_mix_out, v_ln1_g, v_ln1_b, v_xa_wq, v_xa_wkv, v_xa_wo, v_ln2_g, v_ln2_b, v_mlp_w_up, v_mlp_w_down, v_ln3_g, v_ln3_b):
    D = D_MODEL
    xs = x[0]
    T = xs.shape[0]
    mems = mem[0]
    NM = mems.shape[0]
    tgt = loss_target[0]
    my_c = lax.axis_index("c")
    k_me = 2 * lax.axis_index("x") + lax.axis_index("y")
    c_arr = jnp.reshape(my_c, (1,)).astype(jnp.int32)
    k_arr = jnp.reshape(k_me, (1,)).astype(jnp.int32)

    sh_a = jnp.concatenate([w_conv_out[0], w_mix_out[0], xa_wq[0], xa_wo[0], mlp_w_down[0], mlp_w_up[0]], axis=0).astype(BF16)
    def own_block(shard):
        buf = lax.empty((N_CHIPS,) + shard.shape, shard.dtype)
        return lax.dynamic_update_slice(buf, shard[None], (k_me, 0, 0))

    dw_pad = jnp.pad(conv_dw[0], ((0, CONV_HALO - CONV_K), (0, 0)))
    (GIN,) = allgather_weights([own_block(w_in[0].astype(BF16))], name="gather_w_in")
    ag_bufs = [own_block(sh_a), GIN] + [own_block(s) for s in (xa_wkv[0].astype(BF16), w_ssm_glu[0].astype(BF16), dw_pad)]
    ag_pieces = [(4, 0, CONV_HALO), (0, GA_CONV_OUT, 256), (3, 0, D_SSM), (0, GA_MIX_OUT, 256), (0, GA_WQ, 256),
                 (2, 0, D), (0, GA_WO, 256), (0, GA_UP, D), (0, GA_DOWN, D)]
    ag_send, ag_recv, GA, GIN, GKV, GGLU, GDW = gather_start(ag_bufs, ag_pieces, name="gather_start")

    def w_rowshard(row0):
        return dict(b_spec=((N_CHIPS, 256, D), lambda i, j, k: (0, row0 // 256, 0)), b_view=(D, D), tn=D, tk=D)

    _, h0, h0b = ln_fwd(xs, in_norm_g, in_norm_b, name="ln0_fwd")
    p = mm_nn(h0b, GIN, ((None, D, 1152), lambda i, j, k: (j, 0, 0)), D_IN, tn=1152, tk=D, name="mm_w_in")[0]
    GA, GGLU, GDW = gather_wait(
        ag_send, ag_recv, [GA, GGLU, GDW],
        [(2, 0, CONV_HALO, 0), (0, GA_CONV_OUT, 256, 1), (1, 0, D_SSM, 2), (0, GA_MIX_OUT, 256, 3)], p, name="gather_wait_mixer")
    dw_taps = jnp.transpose(GDW, (1, 0, 2)).reshape(CONV_HALO, D)
    c_pre, actb = conv_fwd(p, dw_taps, conv_db, conv_norm_g[0].reshape(1, D), conv_norm_b[0].reshape(1, D), name="conv_fwd")
    ya = mm_nn(actb, GA, N=D, out_dtype=BF16, name="mm_conv_out", **w_rowshard(GA_CONV_OUT))[0]

    lstep, lre, lim = ssm_log_step[0], ssm_lambda_re[0], ssm_lambda_im[0]
    bre, bim, cre, cim = ssm_b_re[0], ssm_b_im[0], ssm_c_re[0], ssm_c_im[0]
    (ar, ai, bbr, bbi), disc_vjp = jax.vjp(_ssm_discretise, lstep, lre, lim, bre, bim)
    Br, Bi = _blockdiag_in(bbr), _blockdiag_in(bbi)
    Cr, Ci = _blockdiag_out(cre), _blockdiag_out(cim)
    lam_r, lam_i = ar.reshape(1, -1), ai.reshape(1, -1)
    dvec = ssm_d[0].reshape(1, D_SSM)
    xr, xi, yssm = ssm_seg_fwd(p, Br, Bi, Cr, Ci, lam_r, lam_i, dvec, name="ssm_fwd")
    z = mm_nn(yssm, GGLU, ((None, D_SSM, 512), lambda i, j, k: (j, 0, 0)), 2 * D, tn=512, tk=D_SSM, out_dtype=BF16,
              name="mm_ssm_glu")[0]
    mergedb = merge_fwd(p, ya, z, name="merge_fwd")
    tm_ln = _pick(T, [512, 256, 128])
    row_spec = ((1, D), lambda i, j, k: (0, 0))

    def ln_epilogue(acc, res, g, b):
        r = ALPHA * res + acc
        xhat, _ = _ln_stats(r)
        h = xhat * g + b
        return r, h, h

    def mm_ln(a, row0, res, g, b, name):
        return mm_nn(a, GA, N=D, tm=tm_ln, extras=(res, g.reshape(1, D), b.reshape(1, D)),
                     extra_specs=[_mn(tm_ln, D), row_spec, row_spec], epilogue=ln_epilogue, out_dtypes=[F32, F32, BF16],
                     name=name, **w_rowshard(row0))

    r1, h1, h1b = mm_ln(mergedb, GA_MIX_OUT, h0, ln1_g[0], ln1_b[0], "mm_mix_out_ln1")
    GA, GKV = gather_wait(ag_send, ag_recv, [GA, GKV], [(0, GA_WQ, 256, 4), (1, 0, D, 5), (0, GA_WO, 256, 6)], r1,
                          name="gather_wait_attn")

    qb = mm_nn(h1b, GA, N=D, out_dtype=BF16, name="mm_wq", **w_rowshard(GA_WQ))[0]
    kv = mm_nn(mems, GKV, ((None, D, 512), lambda i, j, k: (j, 0, 0)), 2 * D, tn=512, tk=D, name="mm_wkv")[0]
    ob = attn_fwd(qb, kv, name="attn_fwd")
    r2, h2, h2b = mm_ln(ob, GA_WO, h1, ln2_g[0], ln2_b[0], "mm_wo_ln2")
    (GA,) = gather_wait(ag_send, ag_recv, [GA], [(0, GA_UP, D, 7), (0, GA_DOWN, D, 8)], r2, name="gather_wait_mlp")

    def relu2(acc):
        zr = jnp.maximum(acc, 0.0)
        return (zr * zr,)

    zzb = mm_nn(h2b, GA, ((None, D, D), lambda i, j, k: (j, GA_UP // D, 0)), D_FF, tn=D, tk=D,
                out_dtype=BF16, epilogue=relu2, name="mm_up")[0]
    ff = mm_nn(zzb, GA, ((N_CHIPS, D, D), lambda i, j, k: (0, GA_DOWN // D, 0)), D, tm=_pick(T, [512, 256, 128]), tn=D, tk=D_FF,
               b_view=(D_FF, D), name="mm_down")[0]
    dr3, dr3b, dg3, db3, sq = ln_loss_bwd(ff, h2, ln3_g[0], ln3_b[0], tgt, name="ln3_loss_bwd")

    def rs_begin(grads, rnd):
        return scatter_start(grads, name=f"rs{rnd}_scatter_start")

    g1_shape = jax.ShapeDtypeStruct((N_CHIPS, G1_ROWS, D), BF16)
    g2_shape = jax.ShapeDtypeStruct((N_CHIPS, G2_ROWS, D), BF16)
    dzpreb = mm_nt(dr3b, GA, ((None, D, D), lambda i, j, k: (j, GA_DOWN // D, 0)), D_FF, tn=D, tk=D, out_dtype=BF16,
                   extras=(zzb,), epilogue=lambda acc, zz: (acc * (2.0 * jnp.sqrt(zz.astype(F32))),), name="mm_down_t")[0]
    G1g = mm_tn(zzb, dr3b, tm=D, tn=D, tk=T, out_spec=((None, D, D), lambda i, j, k: (i, G1_DOWN // D, 0)),
                out_shape=g1_shape, name="mm_down_g")
    G1g = mm_tn(h2b, dzpreb, tm=D, tn=D, tk=T, out_spec=((None, D, D), lambda i, j, k: (j, G1_UP // D, 0)),
                out_shape=g1_shape, out_buf=G1g, name="mm_up_g")
    round1 = rs_begin([G1g], 1)
    dh2 = mm_nt(dzpreb, GA, ((N_CHIPS, D, D), lambda i, j, k: (0, GA_UP // D, 0)), D, tm=_pick(T, [512, 256, 128]), tn=D,
                tk=D_FF, b_chunks=N_CHIPS, extras=(dr3,), epilogue=lambda acc, d: (acc + ALPHA * d,),
                after=(round1[2][0],), name="mm_up_t")[0]
    dr2, dr2b, dg2, db2 = ln_bwd(r2, dh2, ln2_g[0], name="ln2_bwd")

    def g_rowshard(row0, out_buf):
        return dict(tm=D, tn=D, out_spec=((N_CHIPS, 256, D), lambda i, j, k: (0, row0 // 256, 0)), out_shape=g2_shape,
                    out_buf=out_buf)

    dob = mm_nt(dr2b, GA, N=D, out_dtype=BF16, name="mm_wo_t", **w_rowshard(GA_WO))[0]
    G2g = mm_tn(ob, dr2b, name="mm_wo_g", **g_rowshard(G2_WO, None))
    dqb, dkv = attn_bwd(qb, kv, dob, name="attn_bwd")
    G2g = mm_tn(h1b, dqb, name="mm_wq_g", **g_rowshard(G2_WQ, G2g))
    GKVg = mm_tn(mems, dkv, tm=D, tn=512, tk=NM, out_spec=((None, D, 512), lambda i, j, k: (j, 0, 0)),
                 out_shape=jax.ShapeDtypeStruct((N_CHIPS, D, 512), BF16), name="mm_wkv_g")
    dh1 = mm_nt(dqb, GA, N=D, extras=(dr2,), epilogue=lambda acc, d: (acc + ALPHA * d,), name="mm_wq_t",
                **w_rowshard(GA_WQ))[0]
    dr1, dr1b, dg1, db1 = ln_bwd(r1, dh1, ln1_g[0], name="ln1_bwd")

    dmerged = mm_nt(dr1b, GA, N=D, out_dtype=BF16, name="mm_mix_t", **w_rowshard(GA_MIX_OUT))[0]
    G2g = mm_tn(mergedb, dr1b, name="mm_mix_g", **g_rowshard(G2_MIX_OUT, G2g))
    dyab, dgatesb, dzb = merge_bwd(dmerged, p, ya, z, name="merge_bwd")
    GGLUg = mm_tn(yssm, dzb, tm=D_SSM, tn=512, out_spec=((None, D_SSM, 512), lambda i, j, k: (j, 0, 0)),
                  out_shape=jax.ShapeDtypeStruct((N_CHIPS, D_SSM, 512), BF16), name="mm_glu_g")
    dyssm = mm_nt(dzb, GGLU, ((N_CHIPS, D_SSM, 512), lambda i, j, k: (0, 0, 0)), D_SSM, tn=D_SSM, tk=2 * D, b_chunks=N_CHIPS,
                  name="mm_glu_t")[0]
    dub, dBr, dBi, dCr, dCi, dar8, dai8, dd8 = ssm_seg_bwd(dyssm, p, xr, xi, Br, Bi, Cr, Ci, lam_r, lam_i, dvec,
                                                           name="ssm_bwd")
    dar = jnp.sum(dar8, axis=0).reshape(SSM_GROUPS, SSM_STATE)
    dai = jnp.sum(dai8, axis=0).reshape(SSM_GROUPS, SSM_STATE)
    g_lstep, g_lre, g_lim, g_bre, g_bim = disc_vjp((dar, dai, _diag_in(dBr), _diag_in(dBi)))
    g_cre, g_cim = _diag_out(dCr), _diag_out(dCi)
    g_d = jnp.sum(dd8, axis=0).reshape(1, D_SSM)

    dact = mm_nt(dyab, GA, N=D, name="mm_conv_out_t", **w_rowshard(GA_CONV_OUT))[0]
    G2g = mm_tn(actb, dyab, name="mm_conv_out_g", **g_rowshard(G2_CONV_OUT, G2g))
    round2 = rs_begin([G2g, GKVg, GGLUg], 2)
    dc, dng, dnb, ddb = conv_bwd_norm(dact, c_pre, conv_norm_g[0].reshape(1, D), conv_norm_b[0].reshape(1, D),
                                      round2[2][0], name="conv_bwd_norm")
    dpb, ddw = conv_bwd_taps(dc, p, dw_taps, dub, dgatesb, name="conv_bwd_taps")
    GINg = mm_tn(h0b, dpb, tm=D, tn=1152, tk=T, out_spec=((None, D, 1152), lambda i, j, k: (j, 0, 0)),
                 out_shape=jax.ShapeDtypeStruct((N_CHIPS, D, 1152), BF16), name="mm_w_in_g")
    round3 = rs_begin([GINg], 3)
    dh0 = mm_nt(dpb, GIN, ((N_CHIPS, D, 1152), lambda i, j, k: (0, 0, 0)), D, tm=_pick(T, [512, 256, 128]), tn=D, tk=D_IN,
                b_chunks=N_CHIPS, extras=(dr1,), epilogue=lambda acc, d: (acc + ALPHA * d,), after=(round3[2][0],),
                name="mm_w_in_t")[0]
    gx, _, dg0, db0 = ln_bwd(xs, dh0, in_norm_g, name="ln0_bwd")

    kc_arr = jnp.concatenate([k_arr, c_arr])
    landed = scatter_wait([round1, round2, round3], gx, name="rs_scatter_wait")
    tags = ["mlp", "sq", "kv", "glu", "in"]
    pairs = [(pt, l2) for parts, lands2 in landed for pt, l2 in zip(parts, lands2)]
    halves = [add_partials(pt, l2, kc_arr, name="rs_add_partials_" + t) for (pt, l2), t in zip(pairs, tags)]
    g1, g2, gKV, gGLU, gIN = join_halves(halves, name="rs_join_halves")

    small_names = ["in_norm_g", "in_norm_b", "conv_db", "conv_norm_g", "conv_norm_b", "ssm_log_step", "ssm_lambda_re",
                   "ssm_lambda_im", "ssm_b_re", "ssm_b_im", "ssm_c_re", "ssm_c_im", "ssm_d", "ln1_g", "ln1_b",
                   "ln2_g", "ln2_b", "ln3_g", "ln3_b"]
    small_w = [in_norm_g, in_norm_b, conv_db, conv_norm_g, conv_norm_b, ssm_log_step, ssm_lambda_re, ssm_lambda_im,
               ssm_b_re, ssm_b_im, ssm_c_re, ssm_c_im, ssm_d, ln1_g, ln1_b, ln2_g, ln2_b, ln3_g, ln3_b]
    small_m = [m_in_norm_g, m_in_norm_b, m_conv_db, m_conv_norm_g, m_conv_norm_b, m_ssm_log_step, m_ssm_lambda_re,
               m_ssm_lambda_im, m_ssm_b_re, m_ssm_b_im, m_ssm_c_re, m_ssm_c_im, m_ssm_d, m_ln1_g, m_ln1_b, m_ln2_g,
               m_ln2_b, m_ln3_g, m_ln3_b]
    small_v = [v_in_norm_g, v_in_norm_b, v_conv_db, v_conv_norm_g, v_conv_norm_b, v_ssm_log_step, v_ssm_lambda_re,
               v_ssm_lambda_im, v_ssm_b_re, v_ssm_b_im, v_ssm_c_re, v_ssm_c_im, v_ssm_d, v_ln1_g, v_ln1_b, v_ln2_g,
               v_ln2_b, v_ln3_g, v_ln3_b]
    small_g = [dg0, db0, ddb, dng, dnb, g_lstep, g_lre, g_lim, g_bre, g_bim, g_cre, g_cim, g_d, dg1, db1, dg2, db2, dg3, db3]
    small_shapes = [w.shape for w in small_w]
    n_small_rows = _pack(small_w).shape[0]
    packed_g = _pack(small_g + [ddw, sq])
    summed = allreduce_two_level(packed_g, name="allreduce_small")
    small_rows = sum(_pack_rows(s) for s in small_shapes)
    dw_rows = _pack_rows((CONV_HALO, D))
    loss = 0.5 * summed[small_rows + dw_rows, 0] / D
    ddw_full = summed[small_rows:small_rows + dw_rows].reshape(CONV_HALO, D)
    g_dw = lax.dynamic_slice_in_dim(ddw_full, k_me * (D // N_CHIPS), D // N_CHIPS, axis=1)
    gs_packed = jnp.pad(summed[:small_rows], ((0, n_small_rows - small_rows), (0, 0)))

    res = {}

    def upd(nm, w, m, v, g_arr, row0=0):
        shp = w.shape
        w2, m2, v2 = (a.reshape(-1, shp[-1]) for a in (w, m, v))
        outs = adamw(w2, m2, v2, g_arr, row0, name="adamw_" + nm)
        res[nm] = tuple(o.reshape(shp) for o in outs)

    upd("w_conv_out", w_conv_out, m_w_conv_out, v_w_conv_out, g2, G2_CONV_OUT)
    upd("w_mix_out", w_mix_out, m_w_mix_out, v_w_mix_out, g2, G2_MIX_OUT)
    upd("xa_wq", xa_wq, m_xa_wq, v_xa_wq, g2, G2_WQ)
    upd("xa_wo", xa_wo, m_xa_wo, v_xa_wo, g2, G2_WO)
    upd("mlp_w_down", mlp_w_down, m_mlp_w_down, v_mlp_w_down, g1, G1_DOWN)
    upd("mlp_w_up", mlp_w_up, m_mlp_w_up, v_mlp_w_up, g1, G1_UP)
    upd("w_in", w_in, m_w_in, v_w_in, gIN)
    upd("xa_wkv", xa_wkv, m_xa_wkv, v_xa_wkv, gKV)
    upd("w_ssm_glu", w_ssm_glu, m_w_ssm_glu, v_w_ssm_glu, gGLU)
    pad_dw = lambda a: jnp.pad(a[0], ((0, CONV_HALO - CONV_K), (0, 0)))
    dw_outs = adamw(pad_dw(conv_dw), pad_dw(m_conv_dw), pad_dw(v_conv_dw), g_dw, 0, name="adamw_conv_dw")
    res["conv_dw"] = tuple(o[:CONV_K][None] for o in dw_outs)
    sm_outs = adamw(_pack(small_w), _pack(small_m), _pack(small_v), gs_packed, 0, name="adamw_small")
    sm_un = [_unpack(o, small_shapes) for o in sm_outs]
    for idx, nm in enumerate(small_names):
        res[nm] = tuple(sm_un[q][idx] for q in range(4))

    order = ["in_norm_g", "in_norm_b", "w_in", "conv_dw", "conv_db", "conv_norm_g", "conv_norm_b", "w_conv_out",
             "ssm_log_step", "ssm_lambda_re", "ssm_lambda_im", "ssm_b_re", "ssm_b_im", "ssm_c_re", "ssm_c_im", "ssm_d",
             "w_ssm_glu", "w_mix_out", "ln1_g", "ln1_b", "xa_wq", "xa_wkv", "xa_wo", "ln2_g", "ln2_b", "mlp_w_up",
             "mlp_w_down", "ln3_g", "ln3_b"]
    return (loss, gx[None], *[res[n][0] for n in order], *[res[n][1] for n in order],
            *[res[n][2] for n in order], *[res[n][3] for n in order])
```

```python
import functools
import math

import jax
import jax.numpy as jnp
from jax import lax
from jax.experimental import pallas as pl
from jax.experimental.pallas import tpu as pltpu

F32 = jnp.float32
BF16 = jnp.bfloat16
MESH = pl.DeviceIdType.MESH

D_MODEL = 1024
N_HEADS = 4
HEAD_DIM = D_MODEL // N_HEADS
CONV_K = 31
CONV_HALO = 32
D_SSM = 512
SSM_GROUPS = 32
SSM_GROUP = 16
SSM_STATE = 64
SSM_BLOCKS = 4
SSM_BLOCK_IN = D_SSM // SSM_BLOCKS
SSM_BLOCK_STATE = SSM_GROUPS * SSM_STATE // SSM_BLOCKS
D_FF = 4096
D_IN = 4608
LN_EPS = 1e-5
ALPHA = (2.0 * 1) ** 0.25
N_CHIPS = 4
N_DEV = 8
ADAM_LR, ADAM_B1, ADAM_B2, ADAM_EPS, ADAM_WD, ADAM_STEP = 0.001, 0.9, 0.999, 1e-08, 0.01, 10
VMEM_LIMIT_BYTES = 56 * 1024 * 1024


def _pick(dim, cands):
    for c in cands:
        if dim % c == 0:
            return c
    return dim


def _cparams(sem=None):
    return pltpu.CompilerParams(dimension_semantics=sem, vmem_limit_bytes=VMEM_LIMIT_BYTES)


def _sigmoid(x):
    return 1.0 / (1.0 + jnp.exp(-x))


_DIMS = {"nn": (((1,), (0,)), ((), ())), "nt": (((1,), (1,)), ((), ())), "tn": (((0,), (0,)), ((), ()))}


def matmul(a, b, *, mode, M, N, K, tm, tn, tk, a_spec, b_spec, out_specs, out_shapes, name,
           extras=(), extra_specs=(), epilogue=None, alias_buf=None, b_view=None, after=(), b_chunks=None):
    nk = K // tk
    ne = len(extras)
    no = len(out_shapes)
    na = (0 if alias_buf is None else 1) + len(after)
    dims = _DIMS[mode]

    def body(*refs):
        a_ref, b_ref = refs[0], refs[1]
        e_refs = refs[2:2 + ne]
        o_refs = refs[2 + ne + na:2 + ne + na + no]

        def finish(acc):
            outs = (acc,) if epilogue is None else epilogue(acc, *[r[...] for r in e_refs])
            for o, r in zip(outs, o_refs):
                r[...] = o.astype(r.dtype).reshape(r.shape)

        if b_chunks:
            kc = a_ref.shape[1] // b_chunks
            prod = None
            for q in range(b_chunks):
                part = lax.dot_general(a_ref[:, q * kc:(q + 1) * kc].astype(BF16), b_ref[q].astype(BF16), dims,
                                       preferred_element_type=F32)
                prod = part if prod is None else prod + part
        else:
            b_blk = b_ref[...] if b_view is None else b_ref[...].reshape(b_view)
            prod = lax.dot_general(a_ref[...].astype(BF16), b_blk.astype(BF16), dims, preferred_element_type=F32)
        if nk == 1:
            finish(prod)
        else:
            acc_ref = refs[-1]
            k = pl.program_id(2)

            @pl.when(k == 0)
            def _():
                acc_ref[...] = prod

            @pl.when(k > 0)
            def _():
                acc_ref[...] += prod

            @pl.when(k == nk - 1)
            def _():
                finish(acc_ref[...])

    in_specs = [pl.BlockSpec(*a_spec), pl.BlockSpec(*b_spec)] + [pl.BlockSpec(*s) for s in extra_specs]
    ins = [a, b, *extras]
    if alias_buf is not None:
        in_specs.append(pl.BlockSpec(memory_space=pl.ANY))
        ins.append(alias_buf)
    for dep in after:
        in_specs.append(pl.BlockSpec(memory_space=pl.ANY))
        ins.append(dep)
    res = pl.pallas_call(
        body,
        grid=(M // tm, N // tn, nk),
        in_specs=in_specs,
        out_specs=[pl.BlockSpec(*s) for s in out_specs],
        out_shape=out_shapes,
        scratch_shapes=[] if nk == 1 else [pltpu.VMEM((tm, tn), F32)],
        input_output_aliases={2 + ne: 0} if alias_buf is not None else {},
        compiler_params=_cparams(("parallel", "parallel", "arbitrary")),
        name=name,
    )(*ins)
    return res


def _mn(tm, tn):
    return ((tm, tn), lambda i, j, k: (i, j))


def mm_nn(a, b_arr, b_spec, N, *, name, tm=None, tn, tk, out_dtype=F32, extras=(), epilogue=None, out_dtypes=None,
          b_view=None, extra_specs=None):
    M, K = a.shape
    tm = tm or _pick(M, [1024, 512, 256, 128])
    dts = out_dtypes or [out_dtype]
    return matmul(a, b_arr, mode="nn", M=M, N=N, K=K, tm=tm, tn=tn, tk=tk,
                  a_spec=((tm, tk), lambda i, j, k: (i, k)), b_spec=b_spec, b_view=b_view,
                  out_specs=[_mn(tm, tn)] * len(dts), out_shapes=[jax.ShapeDtypeStruct((M, N), d) for d in dts],
                  extras=extras, extra_specs=extra_specs or [_mn(tm, tn)] * len(extras), epilogue=epilogue, name=name)


def mm_nt(a, b_arr, b_spec, N, *, name, tm=None, tn, tk, out_dtype=F32, extras=(), epilogue=None, out_dtypes=None,
          b_view=None, after=(), b_chunks=None):
    M, K = a.shape
    tm = tm or _pick(M, [1024, 512, 256, 128])
    dts = out_dtypes or [out_dtype]
    return matmul(a, b_arr, mode="nt", M=M, N=N, K=K, tm=tm, tn=tn, tk=tk, after=after, b_chunks=b_chunks,
                  a_spec=((tm, tk), lambda i, j, k: (i, k)), b_spec=b_spec, b_view=b_view,
                  out_specs=[_mn(tm, tn)] * len(dts), out_shapes=[jax.ShapeDtypeStruct((M, N), d) for d in dts],
                  extras=extras, extra_specs=[_mn(tm, tn)] * len(extras), epilogue=epilogue, name=name)


def mm_tn(a, b, *, name, tm, tn, tk=None, out_spec, out_shape, out_buf=None):
    K, M = a.shape
    N = b.shape[1]
    tk = tk or _pick(K, [2048, 1024, 512, 256, 128])
    return matmul(a, b, mode="tn", M=M, N=N, K=K, tm=tm, tn=tn, tk=tk,
                  a_spec=((tk, tm), lambda i, j, k: (k, i)), b_spec=((tk, tn), lambda i, j, k: (k, j)),
                  out_specs=[out_spec], out_shapes=[out_shape], alias_buf=out_buf, name=name)[0]


def _rows(tc, w, cb=0):
    return pl.BlockSpec((tc, w), lambda i: (i, cb))


def _const(shape):
    return pl.BlockSpec(shape, lambda i: tuple([0] * len(shape)))


def _ln_stats(r):
    mu = jnp.mean(r, axis=-1, keepdims=True)
    xc = r - mu
    var = jnp.mean(xc * xc, axis=-1, keepdims=True)
    rstd = lax.rsqrt(var + LN_EPS)
    return xc * rstd, rstd


def _rowsum8(v):
    tc, w = v.shape
    return jnp.sum(v.reshape(tc // 8, 8, w), axis=0)


def ln_fwd(x, g, b, *, name, res=None):
    T, D = x.shape
    tc = _pick(T, [1024, 512, 256, 128])
    has_res = res is not None

    def body(*refs):
        if has_res:
            x_ref, res_ref, g_ref, b_ref, r_ref, h_ref, hb_ref = refs
            r = ALPHA * res_ref[...] + x_ref[...]
            r_ref[...] = r
        else:
            x_ref, g_ref, b_ref, h_ref, hb_ref = refs
            r = x_ref[...]
        xhat, _ = _ln_stats(r)
        y = xhat * g_ref[...] + b_ref[...]
        h_ref[...] = y
        hb_ref[...] = y.astype(BF16)

    ins = [x] + ([res] if has_res else []) + [g.reshape(1, D), b.reshape(1, D)]
    in_specs = [_rows(tc, D)] * (2 if has_res else 1) + [_const((1, D))] * 2
    n_out = 3 if has_res else 2
    outs = pl.pallas_call(
        body, grid=(T // tc,), in_specs=in_specs, out_specs=[_rows(tc, D)] * n_out,
        out_shape=[jax.ShapeDtypeStruct((T, D), F32)] * (n_out - 1) + [jax.ShapeDtypeStruct((T, D), BF16)],
        compiler_params=_cparams(("arbitrary",)), name=name)(*ins)
    if has_res:
        return outs
    return (x,) + tuple(outs)


def ln_bwd(r, dy, g, *, name):
    T, D = r.shape
    tc = _pick(T, [1024, 512, 256, 128])
    nt = T // tc

    def body(r_ref, dy_ref, g_ref, dr_ref, drb_ref, dg_ref, db_ref, accg, accb):
        i = pl.program_id(0)

        @pl.when(i == 0)
        def _():
            accg[...] = jnp.zeros_like(accg)
            accb[...] = jnp.zeros_like(accb)

        xhat, rstd = _ln_stats(r_ref[...])
        dy = dy_ref[...]
        dxh = dy * g_ref[...]
        m1 = jnp.mean(dxh, axis=-1, keepdims=True)
        m2 = jnp.mean(dxh * xhat, axis=-1, keepdims=True)
        dr = rstd * (dxh - m1 - xhat * m2)
        dr_ref[...] = dr
        drb_ref[...] = dr.astype(BF16)
        accg[...] += _rowsum8(dy * xhat)
        accb[...] += _rowsum8(dy)

        @pl.when(i == nt - 1)
        def _():
            dg_ref[...] = jnp.sum(accg[...], axis=0, keepdims=True)
            db_ref[...] = jnp.sum(accb[...], axis=0, keepdims=True)

    return pl.pallas_call(
        body, grid=(nt,), in_specs=[_rows(tc, D), _rows(tc, D), _const((1, D))],
        out_specs=[_rows(tc, D), _rows(tc, D), _const((1, D)), _const((1, D))],
        out_shape=[jax.ShapeDtypeStruct((T, D), F32), jax.ShapeDtypeStruct((T, D), BF16),
                   jax.ShapeDtypeStruct((1, D), F32), jax.ShapeDtypeStruct((1, D), F32)],
        scratch_shapes=[pltpu.VMEM((8, D), F32), pltpu.VMEM((8, D), F32)],
        compiler_params=_cparams(("arbitrary",)), name=name)(r, dy, g.reshape(1, D))


def ln_loss_bwd(x, res, g, b, target, *, name):
    T, D = x.shape
    tc = _pick(T, [1024, 512, 256, 128])
    nt = T // tc

    def body(x_ref, res_ref, g_ref, b_ref, t_ref, dr_ref, drb_ref, dg_ref, db_ref, loss_ref, accg, accb, accl):
        i = pl.program_id(0)

        @pl.when(i == 0)
        def _():
            accg[...] = jnp.zeros_like(accg)
            accb[...] = jnp.zeros_like(accb)
            accl[...] = jnp.zeros_like(accl)

        r = ALPHA * res_ref[...] + x_ref[...]
        xhat, rstd = _ln_stats(r)
        e = xhat * g_ref[...] + b_ref[...] - t_ref[...]
        dy = e * (1.0 / D)
        dxh = dy * g_ref[...]
        m1 = jnp.mean(dxh, axis=-1, keepdims=True)
        m2 = jnp.mean(dxh * xhat, axis=-1, keepdims=True)
        dr = rstd * (dxh - m1 - xhat * m2)
        dr_ref[...] = dr
        drb_ref[...] = dr.astype(BF16)
        accg[...] += _rowsum8(dy * xhat)
        accb[...] += _rowsum8(dy)
        accl[...] += _rowsum8(e * e)

        @pl.when(i == nt - 1)
        def _():
            dg_ref[...] = jnp.sum(accg[...], axis=0, keepdims=True)
            db_ref[...] = jnp.sum(accb[...], axis=0, keepdims=True)
            s = jnp.sum(jnp.sum(accl[...], axis=0, keepdims=True), axis=1, keepdims=True)
            loss_ref[...] = jnp.broadcast_to(s, (1, 128))

    return pl.pallas_call(
        body, grid=(nt,), in_specs=[_rows(tc, D), _rows(tc, D), _const((1, D)), _const((1, D)), _rows(tc, D)],
        out_specs=[_rows(tc, D), _rows(tc, D), _const((1, D)), _const((1, D)), _const((1, 128))],
        out_shape=[jax.ShapeDtypeStruct((T, D), F32), jax.ShapeDtypeStruct((T, D), BF16),
                   jax.ShapeDtypeStruct((1, D), F32), jax.ShapeDtypeStruct((1, D), F32), jax.ShapeDtypeStruct((1, 128), F32)],
        scratch_shapes=[pltpu.VMEM((8, D), F32)] * 3,
        compiler_params=_cparams(("arbitrary",)), name=name)(x, res, g.reshape(1, D), b.reshape(1, D), target)


def _halo_prev(tc):
    per = tc // CONV_HALO
    return lambda i: jnp.maximum(i * per - 1, 0)


CONV_ROWS = 32
CONV_TAP_GROUP = 4
CONV_TILE_UNROLL = 4


def _fill_shifts(S, nrows):
    for b in range(1, 8):
        S[b, 0:nrows - 8, :] = S[0, b:b + nrows - 8, :]


def _tap_sum(S, w_ref, offs, r0, nrows):
    acc = None
    for k, o in enumerate(offs):
        a, b = divmod(o, 8)
        term = w_ref[k:k + 1, :] * S[b, pl.ds(pl.multiple_of(r0 + 8 * a, 8), nrows), :]
        acc = term if acc is None else acc + term
    return acc


def conv_fwd(p, dw, db, ng, nb, *, name):
    T = p.shape[0]
    D = D_MODEL
    tc = _pick(T, [256, 128])
    prev = _halo_prev(tc)
    off = CONV_HALO - (CONV_K - 1)
    offs = [off + k for k in range(CONV_K)]

    def body(val_ref, gate_ref, valp_ref, gatep_ref, dw_ref, db_ref, ng_ref, nb_ref, c_ref, act_ref, S):
        i = pl.program_id(0)
        u_prev = valp_ref[...] * _sigmoid(gatep_ref[...])
        S[0, 0:CONV_HALO, :] = jnp.where(i > 0, u_prev, 0.0)
        S[0, CONV_HALO:CONV_HALO + tc, :] = val_ref[...] * _sigmoid(gate_ref[...])
        _fill_shifts(S, CONV_HALO + tc)

        def rows(j, carry):
            r0 = pl.multiple_of(j * CONV_ROWS, CONV_ROWS)
            c_ref[pl.ds(r0, CONV_ROWS), :] = _tap_sum(S, dw_ref, offs, r0, CONV_ROWS) + db_ref[...]
            return carry

        lax.fori_loop(0, tc // CONV_ROWS, rows, 0)
        c = c_ref[...]
        xhat, _ = _ln_stats(c)
        cn = xhat * ng_ref[...] + nb_ref[...]
        act_ref[...] = (cn * _sigmoid(cn)).astype(BF16)

    return pl.pallas_call(
        body, grid=(T // tc,),
        in_specs=[_rows(tc, D, 0), _rows(tc, D, 1),
                  pl.BlockSpec((CONV_HALO, D), lambda i: (prev(i), 0)), pl.BlockSpec((CONV_HALO, D), lambda i: (prev(i), 1)),
                  _const((CONV_HALO, D)), _const((1, D)), _const((1, D)), _const((1, D))],
        out_specs=[_rows(tc, D), _rows(tc, D)],
        out_shape=[jax.ShapeDtypeStruct((T, D), F32), jax.ShapeDtypeStruct((T, D), BF16)],
        scratch_shapes=[pltpu.VMEM((8, CONV_HALO + tc, D), F32)],
        compiler_params=_cparams(("arbitrary",)), name=name)(p, p, p, p, dw, db, ng, nb)


def conv_bwd_norm(dact, c_pre, ng, nb, after, *, name):
    T, D = c_pre.shape
    tc = _pick(T, [1024, 512, 256, 128])
    nt = T // tc

    def body(da_ref, c_ref, ng_ref, nb_ref, after_ref, dc_ref, dng_ref, dnb_ref, ddb_ref, accg, accb, accd):
        i = pl.program_id(0)

        @pl.when(i == 0)
        def _():
            accg[...] = jnp.zeros_like(accg)
            accb[...] = jnp.zeros_like(accb)
            accd[...] = jnp.zeros_like(accd)

        xhat, rstd = _ln_stats(c_ref[...])
        cn = xhat * ng_ref[...] + nb_ref[...]
        s = _sigmoid(cn)
        dcn = da_ref[...] * (s * (1.0 + cn * (1.0 - s)))
        dxh = dcn * ng_ref[...]
        m1 = jnp.mean(dxh, axis=-1, keepdims=True)
        m2 = jnp.mean(dxh * xhat, axis=-1, keepdims=True)
        dc = rstd * (dxh - m1 - xhat * m2)
        dc_ref[...] = dc
        accg[...] += _rowsum8(dcn * xhat)
        accb[...] += _rowsum8(dcn)
        accd[...] += _rowsum8(dc)

        @pl.when(i == nt - 1)
        def _():
            dng_ref[...] = jnp.sum(accg[...], axis=0, keepdims=True)
            dnb_ref[...] = jnp.sum(accb[...], axis=0, keepdims=True)
            ddb_ref[...] = jnp.sum(accd[...], axis=0, keepdims=True)

    vec = jax.ShapeDtypeStruct((1, D), F32)
    return pl.pallas_call(
        body, grid=(nt,), in_specs=[_rows(tc, D), _rows(tc, D), _const((1, D)), _const((1, D)), ANY],
        out_specs=[_rows(tc, D), _const((1, D)), _const((1, D)), _const((1, D))],
        out_shape=[jax.ShapeDtypeStruct((T, D), F32), vec, vec, vec],
        scratch_shapes=[pltpu.VMEM((8, D), F32)] * 3,
        compiler_params=_cparams(("arbitrary",)), name=name)(dact, c_pre, ng, nb, after)


def conv_bwd_taps(dc, p, dw, du_ssm, dgates, *, name):
    T, D = dc.shape
    tc = _pick(T, [256, 128])
    nt = T // tc
    per = tc // CONV_HALO
    prev = _halo_prev(tc)
    last_halo = T // CONV_HALO - 1
    nxt = lambda i: jnp.minimum((i + 1) * per, last_halo)
    off = CONV_HALO - (CONV_K - 1)

    def body(dc_ref, dcn_ref, val_ref, gate_ref, valp_ref, gatep_ref, dw_ref, dus_ref, dg_ref, dvg_ref, ddw_ref,
             ext_u, ext_d, acc):
        i = pl.program_id(0)

        @pl.when(i == 0)
        def _():
            acc[...] = jnp.zeros_like(acc)

        dvg_ref[:, 2 * D:2 * D + D_SSM] = dus_ref[...]
        dvg_ref[:, 2 * D + D_SSM:D_IN] = dg_ref[...]

        u_prev = valp_ref[...] * _sigmoid(gatep_ref[...])
        ext_u[0, 0:CONV_HALO, :] = jnp.where(i > 0, u_prev, 0.0)
        ext_u[0, CONV_HALO:CONV_HALO + tc, :] = val_ref[...] * _sigmoid(gate_ref[...])
        ext_d[0, 0:tc, :] = dc_ref[...]
        ext_d[0, tc:tc + CONV_HALO, :] = jnp.where(i < nt - 1, dcn_ref[...], 0.0)
        _fill_shifts(ext_u, CONV_HALO + tc)
        _fill_shifts(ext_d, CONV_HALO + tc)

        def rows(j, carry):
            r0 = pl.multiple_of(j * CONV_ROWS, CONV_ROWS)
            sl = pl.ds(r0, CONV_ROWS)
            du = _tap_sum(ext_d, dw_ref, [CONV_K - 1 - k for k in range(CONV_K)], r0, CONV_ROWS)
            sg = _sigmoid(gate_ref[sl, :])
            dvg_ref[sl, 0:D] = (du * sg).astype(BF16)
            dvg_ref[sl, D:2 * D] = (du * val_ref[sl, :] * sg * (1.0 - sg)).astype(BF16)
            return carry

        lax.fori_loop(0, tc // CONV_ROWS, rows, 0)

        for k0 in range(0, CONV_K, CONV_TAP_GROUP):
            ks = list(range(k0, min(k0 + CONV_TAP_GROUP, CONV_K)))

            def taps(j, accs, ks=ks):
                out = list(accs)
                for t in range(CONV_TILE_UNROLL):
                    r0 = pl.multiple_of((j * CONV_TILE_UNROLL + t) * 8, 8)
                    dct = dc_ref[pl.ds(r0, 8), :]
                    for q, k in enumerate(ks):
                        a, b = divmod(off + k, 8)
                        out[q] = out[q] + dct * ext_u[b, pl.ds(pl.multiple_of(r0 + 8 * a, 8), 8), :]
                return tuple(out)

            accs = lax.fori_loop(0, tc // (8 * CONV_TILE_UNROLL), taps, tuple(jnp.zeros((8, D), F32) for _ in ks))
            for k, a_k in zip(ks, accs):
                acc[k] += a_k

        @pl.when(i == nt - 1)
        def _():
            ddw_ref[...] = jnp.zeros_like(ddw_ref)
            for k in range(CONV_K):
                ddw_ref[k:k + 1, :] = jnp.sum(acc[k], axis=0, keepdims=True)

    return pl.pallas_call(
        body, grid=(nt,),
        in_specs=[_rows(tc, D), pl.BlockSpec((CONV_HALO, D), lambda i: (nxt(i), 0)),
                  _rows(tc, D, 0), _rows(tc, D, 1),
                  pl.BlockSpec((CONV_HALO, D), lambda i: (prev(i), 0)), pl.BlockSpec((CONV_HALO, D), lambda i: (prev(i), 1)),
                  _const((CONV_HALO, D)), _rows(tc, D_SSM), _rows(tc, 2 * D)],
        out_specs=[_rows(tc, D_IN), _const((CONV_HALO, D))],
        out_shape=[jax.ShapeDtypeStruct((T, D_IN), BF16), jax.ShapeDtypeStruct((CONV_HALO, D), F32)],
        scratch_shapes=[pltpu.VMEM((8, CONV_HALO + tc, D), F32), pltpu.VMEM((8, CONV_HALO + tc, D), F32),
                        pltpu.VMEM((CONV_K, 8, D), F32)],
        compiler_params=_cparams(("arbitrary",)), name=name)(dc, dc, p, p, p, p, dw, du_ssm, dgates)


GATE_A0 = (2 * D_MODEL + D_SSM) // 512
GATE_B0 = GATE_A0 + 2


def merge_fwd(p, ya, z, *, name):
    T = p.shape[0]
    D = D_MODEL
    tc = _pick(T, [1024, 512, 256, 128])
    W = 512

    def body(ga_ref, gb_ref, ya_ref, z1_ref, z2_ref, o_ref):
        yb = z1_ref[...].astype(F32) * _sigmoid(z2_ref[...].astype(F32))
        o_ref[...] = (_sigmoid(ga_ref[...]) * ya_ref[...].astype(F32) + _sigmoid(gb_ref[...]) * yb).astype(BF16)

    return pl.pallas_call(
        body, grid=(T // tc, D // W),
        in_specs=[pl.BlockSpec((tc, W), lambda i, j: (i, GATE_A0 + j)), pl.BlockSpec((tc, W), lambda i, j: (i, GATE_B0 + j)),
                  pl.BlockSpec((tc, W), lambda i, j: (i, j)), pl.BlockSpec((tc, W), lambda i, j: (i, j)),
                  pl.BlockSpec((tc, W), lambda i, j: (i, D // W + j))],
        out_specs=pl.BlockSpec((tc, W), lambda i, j: (i, j)),
        out_shape=jax.ShapeDtypeStruct((T, D), BF16),
        compiler_params=_cparams(("arbitrary", "arbitrary")), name=name)(p, p, ya, z, z)


def merge_bwd(dm, p, ya, z, *, name):
    T = p.shape[0]
    D = D_MODEL
    tc = _pick(T, [512, 256, 128])
    W = 512
    nb = D // W

    def body(dm_ref, ga0_ref, ga1_ref, gb0_ref, gb1_ref, ya_ref, z_ref, dya_ref, dg_ref, dz_ref):
        for j, (ga_ref, gb_ref) in enumerate(((ga0_ref, gb0_ref), (ga1_ref, gb1_ref))):
            c0 = slice(j * W, (j + 1) * W)
            c1 = slice(D + j * W, D + (j + 1) * W)
            dm = dm_ref[:, c0].astype(F32)
            sa = _sigmoid(ga_ref[...])
            sb = _sigmoid(gb_ref[...])
            s2 = _sigmoid(z_ref[:, c1].astype(F32))
            z1 = z_ref[:, c0].astype(F32)
            yb = z1 * s2
            dya_ref[:, c0] = (dm * sa).astype(BF16)
            dg_ref[:, c0] = (dm * ya_ref[:, c0].astype(F32) * sa * (1.0 - sa)).astype(BF16)
            dg_ref[:, c1] = (dm * yb * sb * (1.0 - sb)).astype(BF16)
            dyb = dm * sb
            dz_ref[:, c0] = (dyb * s2).astype(BF16)
            dz_ref[:, c1] = (dyb * z1 * s2 * (1.0 - s2)).astype(BF16)

    gate = lambda cb: pl.BlockSpec((tc, W), lambda i: (i, cb))
    return pl.pallas_call(
        body, grid=(T // tc,),
        in_specs=[_rows(tc, D), gate(GATE_A0), gate(GATE_A0 + 1), gate(GATE_B0), gate(GATE_B0 + 1), _rows(tc, D),
                  _rows(tc, 2 * D)],
        out_specs=[_rows(tc, D), _rows(tc, 2 * D), _rows(tc, 2 * D)],
        out_shape=[jax.ShapeDtypeStruct((T, D), BF16), jax.ShapeDtypeStruct((T, 2 * D), BF16),
                   jax.ShapeDtypeStruct((T, 2 * D), BF16)],
        compiler_params=_cparams(("arbitrary",)), name=name)(dm, p, p, p, p, ya, z)


def _scan_block(src_r, src_i, dst_r, dst_i, car_r, car_i, pw_r, pw_i, cw_r, cw_i, ntiles, reverse, extra=None):
    W = src_r.shape[1]
    rows = lax.broadcasted_iota(jnp.int32, (8, W), 0)
    steps = []
    for d, pr in ((1, 0), (2, 1), (4, 3)):
        valid = rows < 8 - d if reverse else rows >= d
        steps.append((d, jnp.where(valid, jnp.broadcast_to(pw_r[pr:pr + 1, :], (8, W)), 0.0),
                      jnp.where(valid, jnp.broadcast_to(pw_i[pr:pr + 1, :], (8, W)), 0.0)))
    cw_r, cw_i = cw_r[...], cw_i[...]

    def tile(jj, carry):
        j = ntiles - 1 - jj if reverse else jj
        sl = pl.ds(pl.multiple_of(j * 8, 8), 8)
        xr, xi = src_r[sl, :], src_i[sl, :]
        for d, lr, li in steps:
            sr = pltpu.roll(xr, 8 - d if reverse else d, 0)
            si = pltpu.roll(xi, 8 - d if reverse else d, 0)
            xr, xi = xr + lr * sr - li * si, xi + lr * si + li * sr
        cr, ci = car_r[...], car_i[...]
        xr, xi = xr + cw_r * cr - cw_i * ci, xi + cw_r * ci + cw_i * cr
        dst_r[sl, :] = xr
        dst_i[sl, :] = xi
        edge = 0 if reverse else 7
        car_r[...] = jnp.broadcast_to(xr[edge:edge + 1, :], (8, W))
        car_i[...] = jnp.broadcast_to(xi[edge:edge + 1, :], (8, W))
        if extra is not None:
            carry = extra(j, xr, xi, carry)
        return carry

    return tile


def ssm_fwd(p, Br, Bi, Cr, Ci, pw_r, pw_i, dvec, *, name):
    T = p.shape[0]
    tt = _pick(T, [512, 256, 128])
    nt = T // tt
    WI, WS = SSM_BLOCK_IN, SSM_BLOCK_STATE
    u0 = 2 * D_MODEL // WI

    def body(u_ref, br_ref, bi_ref, cr_ref, ci_ref, pwr_ref, pwi_ref, d_ref, xr_ref, xi_ref, y_ref, bur, bui, car_r, car_i):
        i = pl.program_id(1)

        @pl.when(i == 0)
        def _():
            car_r[...] = jnp.zeros_like(car_r)
            car_i[...] = jnp.zeros_like(car_i)

        u = u_ref[...]
        ub = u.astype(BF16)
        bur[...] = jnp.dot(ub, br_ref[...].astype(BF16), preferred_element_type=F32)
        bui[...] = jnp.dot(ub, bi_ref[...].astype(BF16), preferred_element_type=F32)
        tile = _scan_block(bur, bui, xr_ref, xi_ref, car_r, car_i, pwr_ref, pwi_ref, pwr_ref, pwi_ref, tt // 8, False)
        lax.fori_loop(0, tt // 8, tile, 0)
        y = (jnp.dot(xr_ref[...].astype(BF16), cr_ref[...].astype(BF16), preferred_element_type=F32)
             - jnp.dot(xi_ref[...].astype(BF16), ci_ref[...].astype(BF16), preferred_element_type=F32)
             + d_ref[...] * u)
        y_ref[...] = y.astype(BF16)

    return pl.pallas_call(
        body, grid=(SSM_BLOCKS, nt),
        in_specs=[pl.BlockSpec((tt, WI), lambda b, i: (i, u0 + b)),
                  pl.BlockSpec((None, WI, WS), lambda b, i: (b, 0, 0)), pl.BlockSpec((None, WI, WS), lambda b, i: (b, 0, 0)),
                  pl.BlockSpec((None, WS, WI), lambda b, i: (b, 0, 0)), pl.BlockSpec((None, WS, WI), lambda b, i: (b, 0, 0)),
                  pl.BlockSpec((8, WS), lambda b, i: (0, b)), pl.BlockSpec((8, WS), lambda b, i: (0, b)),
                  pl.BlockSpec((1, WI), lambda b, i: (0, b))],
        out_specs=[pl.BlockSpec((tt, WS), lambda b, i: (i, b)), pl.BlockSpec((tt, WS), lambda b, i: (i, b)),
                   pl.BlockSpec((tt, WI), lambda b, i: (i, b))],
        out_shape=[jax.ShapeDtypeStruct((T, SSM_BLOCKS * WS), F32)] * 2 + [jax.ShapeDtypeStruct((T, D_SSM), BF16)],
        scratch_shapes=[pltpu.VMEM((tt, WS), F32), pltpu.VMEM((tt, WS), F32), pltpu.VMEM((8, WS), F32), pltpu.VMEM((8, WS), F32)],
        compiler_params=_cparams(("arbitrary", "arbitrary")), name=name)(p, Br, Bi, Cr, Ci, pw_r, pw_i, dvec)


def ssm_bwd(dy, p, xr, xi, Br, Bi, Cr, Ci, pwc_r, pwc_i, cwc_r, cwc_i, dvec, *, name):
    T = p.shape[0]
    tt = _pick(T, [512, 256, 128])
    nt = T // tt
    WI, WS = SSM_BLOCK_IN, SSM_BLOCK_STATE
    u0 = 2 * D_MODEL // WI
    tb = lambda i: nt - 1 - i
    xprev = lambda i: jnp.maximum(tb(i) * (tt // 8) - 1, 0)
    tn_dims = _DIMS["tn"]
    nt_dims = _DIMS["nt"]

    def body(dy_ref, u_ref, xr_ref, xi_ref, xpr_ref, xpi_ref, br_ref, bi_ref, cr_ref, ci_ref, pwr_ref, pwi_ref,
             cwr_ref, cwi_ref, d_ref,
             du_ref, dbr_ref, dbi_ref, dcr_ref, dci_ref, dar_ref, dai_ref, dd_ref,
             gr, gi, ext_r, ext_i, car_r, car_i):
        i = pl.program_id(1)

        @pl.when(i == 0)
        def _():
            car_r[...] = jnp.zeros_like(car_r)
            car_i[...] = jnp.zeros_like(car_i)
            dbr_ref[...] = jnp.zeros_like(dbr_ref)
            dbi_ref[...] = jnp.zeros_like(dbi_ref)
            dcr_ref[...] = jnp.zeros_like(dcr_ref)
            dci_ref[...] = jnp.zeros_like(dci_ref)
            dar_ref[...] = jnp.zeros_like(dar_ref)
            dai_ref[...] = jnp.zeros_like(dai_ref)
            dd_ref[...] = jnp.zeros_like(dd_ref)

        dy = dy_ref[...]
        dyb = dy.astype(BF16)
        u = u_ref[...]
        ub = u.astype(BF16)
        gr[...] = lax.dot_general(dyb, cr_ref[...].astype(BF16), nt_dims, preferred_element_type=F32)
        gi[...] = -lax.dot_general(dyb, ci_ref[...].astype(BF16), nt_dims, preferred_element_type=F32)
        first = tb(i) == 0
        ext_r[0:8, :] = jnp.where(first, 0.0, xpr_ref[...])
        ext_i[0:8, :] = jnp.where(first, 0.0, xpi_ref[...])
        ext_r[8:8 + tt, :] = xr_ref[...]
        ext_i[8:8 + tt, :] = xi_ref[...]
        rows = lax.broadcasted_iota(jnp.int32, (8, WS), 0)

        def lam_grad(j, g_r, g_i, carry):
            a_r, a_i = carry
            cur = pl.ds(pl.multiple_of(j * 8 + 8, 8), 8)
            prv = pl.ds(pl.multiple_of(j * 8, 8), 8)
            xc_r, xc_i = ext_r[cur, :], ext_i[cur, :]
            xl_r, xl_i = ext_r[prv, :], ext_i[prv, :]
            xp_r = jnp.where(rows == 0, jnp.broadcast_to(xl_r[7:8, :], (8, WS)), pltpu.roll(xc_r, 1, 0))
            xp_i = jnp.where(rows == 0, jnp.broadcast_to(xl_i[7:8, :], (8, WS)), pltpu.roll(xc_i, 1, 0))
            return (a_r + g_r * xp_r + g_i * xp_i, a_i + g_i * xp_r - g_r * xp_i)

        tile = _scan_block(gr, gi, gr, gi, car_r, car_i, pwr_ref, pwi_ref, cwr_ref, cwi_ref, tt // 8, True, extra=lam_grad)
        z8 = jnp.zeros((8, WS), F32)
        a_r, a_i = lax.fori_loop(0, tt // 8, tile, (z8, z8))
        dar_ref[...] += a_r
        dai_ref[...] += a_i
        grb = gr[...].astype(BF16)
        gib = gi[...].astype(BF16)
        dbr_ref[...] += lax.dot_general(ub, grb, tn_dims, preferred_element_type=F32)
        dbi_ref[...] += lax.dot_general(ub, gib, tn_dims, preferred_element_type=F32)
        dcr_ref[...] += lax.dot_general(xr_ref[...].astype(BF16), dyb, tn_dims, preferred_element_type=F32)
        dci_ref[...] -= lax.dot_general(xi_ref[...].astype(BF16), dyb, tn_dims, preferred_element_type=F32)
        du = (lax.dot_general(grb, br_ref[...].astype(BF16), nt_dims, preferred_element_type=F32)
              + lax.dot_general(gib, bi_ref[...].astype(BF16), nt_dims, preferred_element_type=F32)
              + d_ref[...] * dy)
        du_ref[...] = du.astype(BF16)
        dd_ref[...] += _rowsum8(dy * u)

    wspec = lambda shp: pl.BlockSpec((None,) + shp, lambda b, i: (b, 0, 0))
    return pl.pallas_call(
        body, grid=(SSM_BLOCKS, nt),
        in_specs=[pl.BlockSpec((tt, WI), lambda b, i: (tb(i), b)),
                  pl.BlockSpec((tt, WI), lambda b, i: (tb(i), u0 + b)),
                  pl.BlockSpec((tt, WS), lambda b, i: (tb(i), b)), pl.BlockSpec((tt, WS), lambda b, i: (tb(i), b)),
                  pl.BlockSpec((8, WS), lambda b, i: (xprev(i), b)), pl.BlockSpec((8, WS), lambda b, i: (xprev(i), b)),
                  wspec((WI, WS)), wspec((WI, WS)), wspec((WS, WI)), wspec((WS, WI)),
                  pl.BlockSpec((8, WS), lambda b, i: (0, b)), pl.BlockSpec((8, WS), lambda b, i: (0, b)),
                  pl.BlockSpec((8, WS), lambda b, i: (0, b)), pl.BlockSpec((8, WS), lambda b, i: (0, b)),
                  pl.BlockSpec((1, WI), lambda b, i: (0, b))],
        out_specs=[pl.BlockSpec((tt, WI), lambda b, i: (tb(i), b)),
                   wspec((WI, WS)), wspec((WI, WS)), wspec((WS, WI)), wspec((WS, WI)),
                   pl.BlockSpec((8, WS), lambda b, i: (0, b)), pl.BlockSpec((8, WS), lambda b, i: (0, b)),
                   pl.BlockSpec((8, WI), lambda b, i: (0, b))],
        out_shape=[jax.ShapeDtypeStruct((T, D_SSM), BF16),
                   jax.ShapeDtypeStruct((SSM_BLOCKS, WI, WS), F32), jax.ShapeDtypeStruct((SSM_BLOCKS, WI, WS), F32),
                   jax.ShapeDtypeStruct((SSM_BLOCKS, WS, WI), F32), jax.ShapeDtypeStruct((SSM_BLOCKS, WS, WI), F32),
                   jax.ShapeDtypeStruct((8, SSM_BLOCKS * WS), F32), jax.ShapeDtypeStruct((8, SSM_BLOCKS * WS), F32),
                   jax.ShapeDtypeStruct((8, D_SSM), F32)],
        scratch_shapes=[pltpu.VMEM((tt, WS), F32), pltpu.VMEM((tt, WS), F32),
                        pltpu.VMEM((tt + 8, WS), F32), pltpu.VMEM((tt + 8, WS), F32),
                        pltpu.VMEM((8, WS), F32), pltpu.VMEM((8, WS), F32)],
        compiler_params=_cparams(("arbitrary", "arbitrary")), name=name,
    )(dy, p, xr, xi, xr, xi, Br, Bi, Cr, Ci, pwc_r, pwc_i, cwc_r, cwc_i, dvec)


SSM_SEGS = 8


def seg_perm(a, tt):
    T, C = a.shape
    return a.reshape(T // tt, SSM_SEGS, tt // SSM_SEGS, C).transpose(0, 2, 1, 3).reshape(T, C)


def seg_unperm(a, tt):
    T, C = a.shape
    return a.reshape(T // tt, tt // SSM_SEGS, SSM_SEGS, C).transpose(0, 2, 1, 3).reshape(T, C)


def _ssm_tt(T):
    return _pick(T, [512, 256, 128])


def _seg_gather(src_ref, dst_ref, sl):
    for j in range(sl):
        dst_ref[8 * j:8 * j + 8, :] = src_ref[pl.ds(j, SSM_SEGS, stride=sl), :]


def _seg_scatter(val, dst_ref, sl):
    for j in range(sl):
        dst_ref[pl.ds(j, SSM_SEGS, stride=sl), :] = val[8 * j:8 * j + 8, :]


def _seg_tables(ar_ref, ai_ref, conj, pb_r, pb_i, pw_r, pw_i, cw_r, cw_i, sl):
    W = ar_ref.shape[1]
    lr = jnp.broadcast_to(ar_ref[...], (8, W))
    li = jnp.broadcast_to(ai_ref[...], (8, W))
    if conj:
        li = -li

    def power(j, cur):
        cr, ci = cur
        pb_r[j] = cr
        pb_i[j] = ci
        return cr * lr - ci * li, cr * li + ci * lr

    lax.fori_loop(0, sl, power, (lr, li))
    br, bi = pb_r[sl - 1], pb_i[sl - 1]
    rows = lax.broadcasted_iota(jnp.int32, (8, W), 0)
    cr, ci = br, bi
    tr, ti = jnp.zeros((8, W), F32), jnp.zeros((8, W), F32)
    ur, ui = tr, ti
    for r in range(8):
        tr, ti = jnp.where(rows == r, cr, tr), jnp.where(rows == r, ci, ti)
        ur, ui = jnp.where(rows == 7 - r, cr, ur), jnp.where(rows == 7 - r, ci, ui)
        cr, ci = cr * br - ci * bi, cr * bi + ci * br
    pw_r[...] = tr
    pw_i[...] = ti
    cw_r[...] = ur
    cw_i[...] = ui


def ssm_seg_fwd(p, Br, Bi, Cr, Ci, ar, ai, dvec, *, name):
    T = p.shape[0]
    tt = _ssm_tt(T)
    nt = T // tt
    sl = tt // SSM_SEGS
    WI, WS = SSM_BLOCK_IN, SSM_BLOCK_STATE
    u0 = 2 * D_MODEL // WI

    def body(u_ref, br_ref, bi_ref, cr_ref, ci_ref, ar_ref, ai_ref, d_ref, xr_ref, xi_ref, y_ref,
             bur, bui, useg, ynat, pb_r, pb_i, pw_r, pw_i, cw_r, cw_i, end_r, end_i, car_r, car_i):
        i = pl.program_id(1)

        @pl.when(i == 0)
        def _():
            _seg_tables(ar_ref, ai_ref, False, pb_r, pb_i, pw_r, pw_i, cw_r, cw_i, sl)
            car_r[...] = jnp.zeros_like(car_r)
            car_i[...] = jnp.zeros_like(car_i)

        _seg_gather(u_ref, useg, sl)
        u = useg[...]
        ub = u.astype(BF16)
        bur[...] = jnp.dot(ub, br_ref[...].astype(BF16), preferred_element_type=F32)
        bui[...] = jnp.dot(ub, bi_ref[...].astype(BF16), preferred_element_type=F32)
        lr = jnp.broadcast_to(ar_ref[...], (8, WS))
        li = jnp.broadcast_to(ai_ref[...], (8, WS))

        def step(j, st):
            sr, si = st
            rw = pl.ds(pl.multiple_of(j * 8, 8), 8)
            nr = lr * sr - li * si + bur[rw, :]
            ni = lr * si + li * sr + bui[rw, :]
            xr_ref[rw, :] = nr
            xi_ref[rw, :] = ni
            return nr, ni

        z8 = jnp.zeros((8, WS), F32)
        end_r[...], end_i[...] = lax.fori_loop(0, sl, step, (z8, z8))
        old_r, old_i = car_r[...], car_i[...]
        _scan_block(end_r, end_i, end_r, end_i, car_r, car_i, pw_r, pw_i, pw_r, pw_i, 1, False)(0, 0)
        rows = lax.broadcasted_iota(jnp.int32, (8, WS), 0)
        s_r = jnp.where(rows == 0, old_r, pltpu.roll(end_r[...], 1, 0))
        s_i = jnp.where(rows == 0, old_i, pltpu.roll(end_i[...], 1, 0))

        def fix(j, c):
            rw = pl.ds(pl.multiple_of(j * 8, 8), 8)
            pr, pi = pb_r[j], pb_i[j]
            xr_ref[rw, :] = xr_ref[rw, :] + pr * s_r - pi * s_i
            xi_ref[rw, :] = xi_ref[rw, :] + pr * s_i + pi * s_r
            return c

        lax.fori_loop(0, sl, fix, 0)
        y = (jnp.dot(xr_ref[...].astype(BF16), cr_ref[...].astype(BF16), preferred_element_type=F32)
             - jnp.dot(xi_ref[...].astype(BF16), ci_ref[...].astype(BF16), preferred_element_type=F32)
             + d_ref[...] * u)
        _seg_scatter(y, ynat, sl)
        y_ref[...] = ynat[...].astype(BF16)

    wspec = lambda shp: pl.BlockSpec((None,) + shp, lambda b, i: (b, 0, 0))
    vec = lambda w: pl.BlockSpec((1, w), lambda b, i: (0, b))
    tile8 = pltpu.VMEM((8, WS), F32)
    return pl.pallas_call(
        body, grid=(SSM_BLOCKS, nt),
        in_specs=[pl.BlockSpec((tt, WI), lambda b, i: (i, u0 + b)), wspec((WI, WS)), wspec((WI, WS)), wspec((WS, WI)),
                  wspec((WS, WI)), vec(WS), vec(WS), vec(WI)],
        out_specs=[pl.BlockSpec((tt, WS), lambda b, i: (i, b)), pl.BlockSpec((tt, WS), lambda b, i: (i, b)),
                   pl.BlockSpec((tt, WI), lambda b, i: (i, b))],
        out_shape=[jax.ShapeDtypeStruct((T, SSM_BLOCKS * WS), F32)] * 2 + [jax.ShapeDtypeStruct((T, D_SSM), BF16)],
        scratch_shapes=[pltpu.VMEM((tt, WS), F32), pltpu.VMEM((tt, WS), F32),
                        pltpu.VMEM((tt, WI), F32), pltpu.VMEM((tt, WI), F32),
                        pltpu.VMEM((sl, 8, WS), F32), pltpu.VMEM((sl, 8, WS), F32)] + [tile8] * 8,
        compiler_params=_cparams(("arbitrary", "arbitrary")), name=name)(p, Br, Bi, Cr, Ci, ar, ai, dvec)


def ssm_seg_bwd(dy, u, xr, xi, Br, Bi, Cr, Ci, ar, ai, dvec, *, name):
    T = u.shape[0]
    tt = _ssm_tt(T)
    nt = T // tt
    sl = tt // SSM_SEGS
    WI, WS = SSM_BLOCK_IN, SSM_BLOCK_STATE
    u0 = 2 * D_MODEL // WI
    tb = lambda i: nt - 1 - i
    xprev = lambda i: jnp.maximum(tb(i) * (tt // 8) - 1, 0)
    tn_dims = _DIMS["tn"]
    nt_dims = _DIMS["nt"]

    def body(dyn_ref, un_ref, xr_ref, xi_ref, xpr_ref, xpi_ref, br_ref, bi_ref, cr_ref, ci_ref, ar_ref, ai_ref, d_ref,
             du_ref, dbr_ref, dbi_ref, dcr_ref, dci_ref, dar_ref, dai_ref, dd_ref,
             gr, gi, ext_r, ext_i, dy_ref, u_ref, dunat, pb_r, pb_i, pw_r, pw_i, cw_r, cw_i, end_r, end_i, car_r, car_i):
        _seg_gather(dyn_ref, dy_ref, sl)
        _seg_gather(un_ref, u_ref, sl)
        i = pl.program_id(1)

        @pl.when(i == 0)
        def _():
            _seg_tables(ar_ref, ai_ref, True, pb_r, pb_i, pw_r, pw_i, cw_r, cw_i, sl)
            car_r[...] = jnp.zeros_like(car_r)
            car_i[...] = jnp.zeros_like(car_i)
            dbr_ref[...] = jnp.zeros_like(dbr_ref)
            dbi_ref[...] = jnp.zeros_like(dbi_ref)
            dcr_ref[...] = jnp.zeros_like(dcr_ref)
            dci_ref[...] = jnp.zeros_like(dci_ref)
            dar_ref[...] = jnp.zeros_like(dar_ref)
            dai_ref[...] = jnp.zeros_like(dai_ref)
            dd_ref[...] = jnp.zeros_like(dd_ref)

        dy = dy_ref[...]
        dyb = dy.astype(BF16)
        u = u_ref[...]
        ub = u.astype(BF16)
        gr[...] = lax.dot_general(dyb, cr_ref[...].astype(BF16), nt_dims, preferred_element_type=F32)
        gi[...] = -lax.dot_general(dyb, ci_ref[...].astype(BF16), nt_dims, preferred_element_type=F32)
        lr = jnp.broadcast_to(ar_ref[...], (8, WS))
        li = -jnp.broadcast_to(ai_ref[...], (8, WS))
        rows = lax.broadcasted_iota(jnp.int32, (8, WS), 0)

        def step(jj, st):
            sr, si = st
            rw = pl.ds(pl.multiple_of((sl - 1 - jj) * 8, 8), 8)
            nr = lr * sr - li * si + gr[rw, :]
            ni = lr * si + li * sr + gi[rw, :]
            gr[rw, :] = nr
            gi[rw, :] = ni
            return nr, ni

        z8 = jnp.zeros((8, WS), F32)
        end_r[...], end_i[...] = lax.fori_loop(0, sl, step, (z8, z8))
        old_r, old_i = car_r[...], car_i[...]
        _scan_block(end_r, end_i, end_r, end_i, car_r, car_i, pw_r, pw_i, cw_r, cw_i, 1, True)(0, 0)
        s_r = jnp.where(rows == 7, old_r, pltpu.roll(end_r[...], 7, 0))
        s_i = jnp.where(rows == 7, old_i, pltpu.roll(end_i[...], 7, 0))
        first = tb(i) == 0
        last_r, last_i = xr_ref[tt - 8:tt, :], xi_ref[tt - 8:tt, :]
        pv_r = jnp.where(first, 0.0, xpr_ref[...])
        pv_i = jnp.where(first, 0.0, xpi_ref[...])
        ext_r[0:8, :] = jnp.where(rows == 0, jnp.broadcast_to(pv_r[7:8, :], (8, WS)), pltpu.roll(last_r, 1, 0))
        ext_i[0:8, :] = jnp.where(rows == 0, jnp.broadcast_to(pv_i[7:8, :], (8, WS)), pltpu.roll(last_i, 1, 0))
        ext_r[8:8 + tt, :] = xr_ref[...]
        ext_i[8:8 + tt, :] = xi_ref[...]

        def fix(j, acc):
            a_r, a_i = acc
            rw = pl.ds(pl.multiple_of(j * 8, 8), 8)
            pr, pi = pb_r[sl - 1 - j], pb_i[sl - 1 - j]
            g_r = gr[rw, :] + pr * s_r - pi * s_i
            g_i = gi[rw, :] + pr * s_i + pi * s_r
            gr[rw, :] = g_r
            gi[rw, :] = g_i
            xp_r, xp_i = ext_r[rw, :], ext_i[rw, :]
            return a_r + g_r * xp_r + g_i * xp_i, a_i + g_i * xp_r - g_r * xp_i

        a_r, a_i = lax.fori_loop(0, sl, fix, (z8, z8))
        dar_ref[...] += a_r
        dai_ref[...] += a_i
        grb = gr[...].astype(BF16)
        gib = gi[...].astype(BF16)
        dbr_ref[...] += lax.dot_general(ub, grb, tn_dims, preferred_element_type=F32)
        dbi_ref[...] += lax.dot_general(ub, gib, tn_dims, preferred_element_type=F32)
        dcr_ref[...] += lax.dot_general(xr_ref[...].astype(BF16), dyb, tn_dims, preferred_element_type=F32)
        dci_ref[...] -= lax.dot_general(xi_ref[...].astype(BF16), dyb, tn_dims, preferred_element_type=F32)
        du = (lax.dot_general(grb, br_ref[...].astype(BF16), nt_dims, preferred_element_type=F32)
              + lax.dot_general(gib, bi_ref[...].astype(BF16), nt_dims, preferred_element_type=F32)
              + d_ref[...] * dy)
        _seg_scatter(du, dunat, sl)
        du_ref[...] = dunat[...].astype(BF16)
        dd_ref[...] += _rowsum8(dy * u)

    wspec = lambda shp: pl.BlockSpec((None,) + shp, lambda b, i: (b, 0, 0))
    vec = lambda w: pl.BlockSpec((1, w), lambda b, i: (0, b))
    tile8 = pltpu.VMEM((8, WS), F32)
    return pl.pallas_call(
        body, grid=(SSM_BLOCKS, nt),
        in_specs=[pl.BlockSpec((tt, WI), lambda b, i: (tb(i), b)), pl.BlockSpec((tt, WI), lambda b, i: (tb(i), u0 + b)),
                  pl.BlockSpec((tt, WS), lambda b, i: (tb(i), b)), pl.BlockSpec((tt, WS), lambda b, i: (tb(i), b)),
                  pl.BlockSpec((8, WS), lambda b, i: (xprev(i), b)), pl.BlockSpec((8, WS), lambda b, i: (xprev(i), b)),
                  wspec((WI, WS)), wspec((WI, WS)), wspec((WS, WI)), wspec((WS, WI)), vec(WS), vec(WS), vec(WI)],
        out_specs=[pl.BlockSpec((tt, WI), lambda b, i: (tb(i), b)),
                   wspec((WI, WS)), wspec((WI, WS)), wspec((WS, WI)), wspec((WS, WI)),
                   pl.BlockSpec((8, WS), lambda b, i: (0, b)), pl.BlockSpec((8, WS), lambda b, i: (0, b)),
                   pl.BlockSpec((8, WI), lambda b, i: (0, b))],
        out_shape=[jax.ShapeDtypeStruct((T, D_SSM), BF16),
                   jax.ShapeDtypeStruct((SSM_BLOCKS, WI, WS), F32), jax.ShapeDtypeStruct((SSM_BLOCKS, WI, WS), F32),
                   jax.ShapeDtypeStruct((SSM_BLOCKS, WS, WI), F32), jax.ShapeDtypeStruct((SSM_BLOCKS, WS, WI), F32),
                   jax.ShapeDtypeStruct((8, SSM_BLOCKS * WS), F32), jax.ShapeDtypeStruct((8, SSM_BLOCKS * WS), F32),
                   jax.ShapeDtypeStruct((8, D_SSM), F32)],
        scratch_shapes=[pltpu.VMEM((tt, WS), F32), pltpu.VMEM((tt, WS), F32),
                        pltpu.VMEM((tt + 8, WS), F32), pltpu.VMEM((tt + 8, WS), F32),
                        pltpu.VMEM((tt, WI), F32), pltpu.VMEM((tt, WI), F32), pltpu.VMEM((tt, WI), F32),
                        pltpu.VMEM((sl, 8, WS), F32), pltpu.VMEM((sl, 8, WS), F32)] + [tile8] * 8,
        compiler_params=_cparams(("arbitrary", "arbitrary")), name=name,
    )(dy, u, xr, xi, xr, xi, Br, Bi, Cr, Ci, ar, ai, dvec)


def _ssm_discretise(log_step, lam_re, lam_im, b_re, b_im):
    step = jnp.exp(log_step)[:, None]
    mag = jnp.exp(lam_re * step)
    ar = mag * jnp.cos(lam_im * step)
    ai = mag * jnp.sin(lam_im * step)
    den = lam_re * lam_re + lam_im * lam_im
    nr = ar - 1.0
    cr = (nr * lam_re + ai * lam_im) / den
    ci = (ai * lam_re - nr * lam_im) / den
    bbr = cr[..., None] * b_re - ci[..., None] * b_im
    bbi = cr[..., None] * b_im + ci[..., None] * b_re
    return ar, ai, bbr, bbi


def _blockdiag_in(bb):
    t = jnp.transpose(bb, (0, 2, 1)).reshape(SSM_BLOCKS, 8, SSM_GROUP, SSM_STATE)
    eye = jnp.eye(8, dtype=bb.dtype)
    return (t[:, :, :, None, :] * eye[None, :, None, :, None]).reshape(SSM_BLOCKS, SSM_BLOCK_IN, SSM_BLOCK_STATE)


def _blockdiag_out(cc):
    t = jnp.transpose(cc, (0, 2, 1)).reshape(SSM_BLOCKS, 8, SSM_STATE, SSM_GROUP)
    eye = jnp.eye(8, dtype=cc.dtype)
    return (t[:, :, :, None, :] * eye[None, :, None, :, None]).reshape(SSM_BLOCKS, SSM_BLOCK_STATE, SSM_BLOCK_IN)


def _diag_in(d):
    t = d.reshape(SSM_BLOCKS, 8, SSM_GROUP, 8, SSM_STATE)
    t = jnp.einsum("bghgp->bghp", t).reshape(SSM_GROUPS, SSM_GROUP, SSM_STATE)
    return jnp.transpose(t, (0, 2, 1))


def _diag_out(d):
    t = d.reshape(SSM_BLOCKS, 8, SSM_STATE, 8, SSM_GROUP)
    t = jnp.einsum("bgpgh->bgph", t).reshape(SSM_GROUPS, SSM_STATE, SSM_GROUP)
    return jnp.transpose(t, (0, 2, 1))


def _powers(ar, ai):
    rs, is_ = [ar], [ai]
    for _ in range(7):
        r, i = rs[-1], is_[-1]
        rs.append(r * ar - i * ai)
        is_.append(r * ai + i * ar)
    return jnp.stack(rs), jnp.stack(is_), jnp.stack(rs[::-1]), jnp.stack(is_[::-1])


def attn_fwd(q, kv, *, name):
    T, D = q.shape
    nm = kv.shape[0]
    tq = _pick(T, [512, 256, 128])
    scale = HEAD_DIM ** -0.5

    def body(q_ref, k_ref, v_ref, o_ref):
        for h in range(N_HEADS):
            sl = slice(h * HEAD_DIM, (h + 1) * HEAD_DIM)
            s = lax.dot_general(q_ref[:, sl], k_ref[:, sl].astype(BF16), _DIMS["nt"], preferred_element_type=F32) * scale
            e = jnp.exp(s - jnp.max(s, axis=-1, keepdims=True))
            pr = e / jnp.sum(e, axis=-1, keepdims=True)
            o_ref[:, sl] = jnp.dot(pr.astype(BF16), v_ref[:, sl].astype(BF16), preferred_element_type=F32).astype(BF16)

    return pl.pallas_call(
        body, grid=(T // tq,),
        in_specs=[_rows(tq, D), pl.BlockSpec((nm, D), lambda i: (0, 0)), pl.BlockSpec((nm, D), lambda i: (0, 1))],
        out_specs=_rows(tq, D), out_shape=jax.ShapeDtypeStruct((T, D), BF16),
        compiler_params=_cparams(("arbitrary",)), name=name)(q, kv, kv)


def attn_bwd(q, kv, do, *, name):
    T, D = q.shape
    nm = kv.shape[0]
    tq = _pick(T, [512, 256, 128])
    nt = T // tq
    scale = HEAD_DIM ** -0.5

    def body(q_ref, k_ref, v_ref, do_ref, dq_ref, dkv_ref):
        i = pl.program_id(0)

        @pl.when(i == 0)
        def _():
            dkv_ref[...] = jnp.zeros_like(dkv_ref)

        for h in range(N_HEADS):
            sl = slice(h * HEAD_DIM, (h + 1) * HEAD_DIM)
            slv = slice(D + h * HEAD_DIM, D + (h + 1) * HEAD_DIM)
            qh = q_ref[:, sl]
            kh = k_ref[:, sl].astype(BF16)
            vh = v_ref[:, sl].astype(BF16)
            doh = do_ref[:, sl].astype(BF16)
            s = lax.dot_general(qh, kh, _DIMS["nt"], preferred_element_type=F32) * scale
            e = jnp.exp(s - jnp.max(s, axis=-1, keepdims=True))
            pr = e / jnp.sum(e, axis=-1, keepdims=True)
            dp = lax.dot_general(doh, vh, _DIMS["nt"], preferred_element_type=F32)
            ds = (pr * (dp - jnp.sum(pr * dp, axis=-1, keepdims=True)) * scale).astype(BF16)
            dq_ref[:, sl] = jnp.dot(ds, kh, preferred_element_type=F32).astype(BF16)
            dkv_ref[:, sl] += lax.dot_general(ds, qh, _DIMS["tn"], preferred_element_type=F32)
            dkv_ref[:, slv] += lax.dot_general(pr.astype(BF16), doh, _DIMS["tn"], preferred_element_type=F32)

    return pl.pallas_call(
        body, grid=(nt,),
        in_specs=[_rows(tq, D), pl.BlockSpec((nm, D), lambda i: (0, 0)), pl.BlockSpec((nm, D), lambda i: (0, 1)), _rows(tq, D)],
        out_specs=[_rows(tq, D), _const((nm, 2 * D))],
        out_shape=[jax.ShapeDtypeStruct((T, D), BF16), jax.ShapeDtypeStruct((nm, 2 * D), F32)],
        compiler_params=_cparams(("arbitrary",)), name=name)(q, kv, kv, do)


def _adam_math(w, g, m, v):
    m = ADAM_B1 * m + (1.0 - ADAM_B1) * g
    v = ADAM_B2 * v + (1.0 - ADAM_B2) * (g * g)
    m_hat = m / (1.0 - ADAM_B1 ** ADAM_STEP)
    v_hat = v / (1.0 - ADAM_B2 ** ADAM_STEP)
    delta = -ADAM_LR * (m_hat / (jnp.sqrt(v_hat) + ADAM_EPS) + ADAM_WD * w)
    return delta, m, v


def adamw(w, m, v, g_arr, g_row0, *, name):
    R, C = w.shape
    tr = _pick(R, [512, 256, 128, 64, 32, 16, 8])
    assert g_row0 % tr == 0
    g0 = g_row0 // tr

    def body(w_ref, m_ref, v_ref, g_ref, go_ref, d_ref, mo_ref, vo_ref):
        g = g_ref[...]
        d, mn, vn = _adam_math(w_ref[...], g, m_ref[...], v_ref[...])
        go_ref[...] = g
        d_ref[...] = d
        mo_ref[...] = mn
        vo_ref[...] = vn

    sp = pl.BlockSpec((tr, C), lambda i: (i, 0))
    return pl.pallas_call(
        body, grid=(R // tr,), in_specs=[sp, sp, sp, pl.BlockSpec((tr, C), lambda i: (g0 + i, 0))],
        out_specs=[sp] * 4, out_shape=[jax.ShapeDtypeStruct((R, C), F32)] * 4,
        compiler_params=_cparams(("arbitrary",)), name=name)(w, m, v, g_arr)


def _place():
    x, y, c = lax.axis_index("x"), lax.axis_index("y"), lax.axis_index("c")
    chips = [(1 - x, y), (x, 1 - y), (1 - x, 1 - y)]
    return x, y, c, chips


ANY = pl.BlockSpec(memory_space=pl.ANY)


def allgather_weights(bufs, *, name):
    n = len(bufs)

    def body(*refs):
        o_refs = refs[n:2 * n]
        send_sems, recv_sems, fsend_sems, frecv_sems = refs[2 * n:]
        x, y, c, chips = _place()
        k_me = 2 * x + y
        sib = (x, y, 1 - c)
        halves = [b.shape[1] // 2 for b in bufs]

        def half(a, cc):
            return pl.ds(pl.multiple_of(cc * halves[a], 16), halves[a])

        sends = []
        for a in range(n):
            for r, (px, py) in enumerate(chips):
                cp = pltpu.make_async_remote_copy(
                    src_ref=o_refs[a].at[k_me, half(a, c)], dst_ref=o_refs[a].at[k_me, half(a, c)],
                    send_sem=send_sems.at[3 * a + r], recv_sem=recv_sems.at[3 * a + r],
                    device_id=(px, py, c), device_id_type=MESH)
                cp.start()
                sends.append(cp)
        passed = []
        for a in range(n):
            for r, (px, py) in enumerate(chips):
                win = o_refs[a].at[2 * px + py, half(a, c)]
                pltpu.make_async_remote_copy(
                    src_ref=win, dst_ref=win, send_sem=send_sems.at[3 * a + r], recv_sem=recv_sems.at[3 * a + r],
                    device_id=(px, py, c), device_id_type=MESH).wait_recv()
                cp = pltpu.make_async_remote_copy(
                    src_ref=win, dst_ref=win, send_sem=fsend_sems.at[3 * a + r], recv_sem=frecv_sems.at[3 * a + r],
                    device_id=sib, device_id_type=MESH)
                cp.start()
                passed.append(cp)
        for a in range(n):
            for r, (px, py) in enumerate(chips):
                win = o_refs[a].at[2 * px + py, half(a, 1 - c)]
                pltpu.make_async_remote_copy(
                    src_ref=win, dst_ref=win, send_sem=fsend_sems.at[3 * a + r], recv_sem=frecv_sems.at[3 * a + r],
                    device_id=sib, device_id_type=MESH).wait_recv()
        for cp in sends + passed:
            cp.wait_send()

    return pl.pallas_call(
        body, in_specs=[ANY] * n, out_specs=[ANY] * n,
        out_shape=[jax.ShapeDtypeStruct(b.shape, b.dtype) for b in bufs],
        scratch_shapes=[pltpu.SemaphoreType.DMA((3 * n,))] * 4,
        input_output_aliases={a: a for a in range(n)},
        name=name)(*bufs)


HBM_SPEC = pl.BlockSpec(memory_space=pltpu.HBM)
SEM_SPEC = pl.BlockSpec(memory_space=pltpu.SEMAPHORE)


def _hbm(a):
    return pltpu.with_memory_space_constraint(a, pltpu.HBM)


def gather_start(bufs, pieces, *, name):
    n = len(bufs)
    npc = len(pieces)

    def body(*refs):
        b_refs = refs[:n]
        send_sems, recv_sems = refs[n], refs[n + 1]
        x, y, c, chips = _place()
        k_me = 2 * x + y
        for q, (a, row0, rows) in enumerate(pieces):
            win = b_refs[a].at[k_me, pl.ds(row0, rows)]
            for r, (px, py) in enumerate(chips):
                pltpu.make_async_remote_copy(
                    src_ref=win, dst_ref=win, send_sem=send_sems.at[3 * q + r], recv_sem=recv_sems.at[3 * q + r],
                    device_id=(px, py, c), device_id_type=MESH).start()

    return pl.pallas_call(
        body, in_specs=[HBM_SPEC] * n, out_specs=[SEM_SPEC, SEM_SPEC] + [HBM_SPEC] * n,
        out_shape=[pltpu.SemaphoreType.DMA((3 * npc,)), pltpu.SemaphoreType.DMA((3 * npc,))]
        + [pltpu.HBM(b.shape, b.dtype) for b in bufs],
        input_output_aliases={a: 2 + a for a in range(n)},
        compiler_params=pltpu.CompilerParams(has_side_effects=pltpu.SideEffectType.DATAFLOW_SIDE_EFFECTING),
        name=name)(*[_hbm(b) for b in bufs])


def gather_wait(send_sems, recv_sems, bufs, which, after, *, name):
    n = len(bufs)

    def body(*refs):
        b_refs = refs[:n]
        send_sems, recv_sems = refs[n], refs[n + 1]
        x, y, c, chips = _place()
        k_me = 2 * x + y
        for a, row0, rows, q in which:
            for r, (px, py) in enumerate(chips):
                cp = pltpu.make_async_remote_copy(
                    src_ref=b_refs[a].at[k_me, pl.ds(row0, rows)], dst_ref=b_refs[a].at[2 * px + py, pl.ds(row0, rows)],
                    send_sem=send_sems.at[3 * q + r], recv_sem=recv_sems.at[3 * q + r],
                    device_id=(px, py, c), device_id_type=MESH)
                cp.wait_send()
                cp.wait_recv()

    return pl.pallas_call(
        body, in_specs=[HBM_SPEC] * n + [SEM_SPEC, SEM_SPEC, ANY], out_specs=[HBM_SPEC] * n,
        out_shape=[pltpu.HBM(b.shape, b.dtype) for b in bufs],
        input_output_aliases={a: a for a in range(n)},
        compiler_params=pltpu.CompilerParams(has_side_effects=pltpu.SideEffectType.DATAFLOW_SIDE_EFFECTING),
        name=name)(*bufs, send_sems, recv_sems, after)


def exchange_halves(grads, *, name):
    n = len(grads)

    def body(*refs):
        g_refs, l_refs = refs[:n], refs[n:2 * n]
        send_sems, recv_sems = refs[2 * n:]
        x, y, c, _ = _place()
        cps = []
        for a in range(n):
            h = grads[a].shape[1] // 2
            cp = pltpu.make_async_remote_copy(
                src_ref=g_refs[a].at[:, pl.ds(pl.multiple_of((1 - c) * h, 8), h)], dst_ref=l_refs[a],
                send_sem=send_sems.at[a], recv_sem=recv_sems.at[a], device_id=(x, y, 1 - c), device_id_type=MESH)
            cp.start()
            cps.append(cp)
        for cp in cps:
            cp.wait()

    return pl.pallas_call(
        body, in_specs=[ANY] * n, out_specs=[ANY] * n,
        out_shape=[jax.ShapeDtypeStruct((g.shape[0], g.shape[1] // 2, g.shape[2]), g.dtype) for g in grads],
        scratch_shapes=[pltpu.SemaphoreType.DMA((n,))] * 2,
        name=name)(*grads)


N_PEERS = N_DEV - 1


def _scatter_copies(p_refs, l_refs, send_sems, recv_sems):
    x, y, c, _ = _place()
    cps = []
    for a in range(len(p_refs)):
        h = p_refs[a].shape[1] // 2
        for fx, fy in ((0, 0), (1, 0), (0, 1), (1, 1)):
            for fc in (0, 1):
                if (fx, fy, fc) == (0, 0, 0):
                    continue
                slot = 2 * (fx + 2 * fy) + fc - 1
                px, py, pc = (1 - x if fx else x), (1 - y if fy else y), (1 - c if fc else c)
                cps.append(pltpu.make_async_remote_copy(
                    src_ref=p_refs[a].at[2 * px + py, pl.ds(pl.multiple_of(pc * h, 16), h)], dst_ref=l_refs[a].at[slot],
                    send_sem=send_sems.at[N_PEERS * a + slot], recv_sem=recv_sems.at[N_PEERS * a + slot],
                    device_id=(px, py, pc), device_id_type=MESH))
    return cps


def scatter_start(parts, *, name):
    n = len(parts)
    lands = [lax.empty((N_PEERS, p.shape[1] // 2, p.shape[2]), p.dtype) for p in parts]

    def body(*refs):
        for cp in _scatter_copies(refs[:n], refs[n:2 * n], refs[2 * n], refs[2 * n + 1]):
            cp.start()

    outs = pl.pallas_call(
        body, in_specs=[HBM_SPEC] * (2 * n), out_specs=[SEM_SPEC, SEM_SPEC] + [HBM_SPEC] * (2 * n),
        out_shape=[pltpu.SemaphoreType.DMA((N_PEERS * n,)), pltpu.SemaphoreType.DMA((N_PEERS * n,))]
        + [pltpu.HBM(a.shape, a.dtype) for a in parts + lands],
        input_output_aliases={a: 2 + a for a in range(2 * n)},
        compiler_params=pltpu.CompilerParams(has_side_effects=pltpu.SideEffectType.DATAFLOW_SIDE_EFFECTING),
        name=name)(*[_hbm(a) for a in parts + lands])
    return outs[0], outs[1], list(outs[2:2 + n]), list(outs[2 + n:])


def scatter_wait(rounds, after, *, name):
    sizes = [len(r[2]) for r in rounds]
    flat = [a for r in rounds for a in r[2] + r[3]]
    sems = [s for r in rounds for s in (r[0], r[1])]
    nflat = len(flat)

    def body(*refs):
        pos = 0
        for ri, n in enumerate(sizes):
            for cp in _scatter_copies(refs[pos:pos + n], refs[pos + n:pos + 2 * n], refs[nflat + 2 * ri], refs[nflat + 2 * ri + 1]):
                cp.wait_send()
                cp.wait_recv()
            pos += 2 * n

    outs = pl.pallas_call(
        body, in_specs=[HBM_SPEC] * nflat + [SEM_SPEC] * len(sems) + [ANY], out_specs=[HBM_SPEC] * nflat,
        out_shape=[pltpu.HBM(a.shape, a.dtype) for a in flat],
        input_output_aliases={a: a for a in range(nflat)},
        compiler_params=pltpu.CompilerParams(has_side_effects=pltpu.SideEffectType.DATAFLOW_SIDE_EFFECTING),
        name=name)(*flat, *sems, after)
    res, pos = [], 0
    for n in sizes:
        res.append((list(outs[pos:pos + n]), list(outs[pos + n:pos + 2 * n])))
        pos += 2 * n
    return res


def join_halves(fulls, *, name):
    n = len(fulls)

    def body(*refs):
        o_refs = refs[n:2 * n]
        send_sems, recv_sems = refs[2 * n:]
        x, y, c, _ = _place()
        cps = []
        for a in range(n):
            h = fulls[a].shape[0] // 2
            win = o_refs[a].at[pl.ds(pl.multiple_of(c * h, 8), h)]
            cp = pltpu.make_async_remote_copy(
                src_ref=win, dst_ref=win, send_sem=send_sems.at[a], recv_sem=recv_sems.at[a],
                device_id=(x, y, 1 - c), device_id_type=MESH)
            cp.start()
            cps.append(cp)
        for a in range(n):
            h = fulls[a].shape[0] // 2
            other = o_refs[a].at[pl.ds(pl.multiple_of((1 - c) * h, 8), h)]
            pltpu.make_async_remote_copy(
                src_ref=other, dst_ref=other, send_sem=send_sems.at[a], recv_sem=recv_sems.at[a],
                device_id=(x, y, 1 - c), device_id_type=MESH).wait_recv()
        for cp in cps:
            cp.wait_send()

    return pl.pallas_call(
        body, in_specs=[ANY] * n, out_specs=[ANY] * n,
        out_shape=[jax.ShapeDtypeStruct(f.shape, f.dtype) for f in fulls],
        scratch_shapes=[pltpu.SemaphoreType.DMA((n,))] * 2,
        input_output_aliases={a: a for a in range(n)},
        name=name)(*fulls)


def add_partials(part, land, kc, *, name):
    _, R, C = part.shape
    H = R // 2
    tr = _pick(H, [256, 128, 64, 32, 16])
    per = H // tr

    def body(kc_ref, p_ref, l_ref, o_ref):
        acc = p_ref[...].astype(F32)
        for s in range(N_PEERS):
            acc = acc + l_ref[s].astype(F32)
        o_ref[...] = acc

    return pl.pallas_call(
        body,
        grid_spec=pltpu.PrefetchScalarGridSpec(
            num_scalar_prefetch=1, grid=(per,),
            in_specs=[pl.BlockSpec((None, tr, C), lambda i, kc_ref: (kc_ref[0], kc_ref[1] * per + i, 0)),
                      pl.BlockSpec((N_PEERS, tr, C), lambda i, kc_ref: (0, i, 0))],
            out_specs=pl.BlockSpec((tr, C), lambda i, kc_ref: (kc_ref[1] * per + i, 0))),
        out_shape=jax.ShapeDtypeStruct((R, C), F32),
        compiler_params=_cparams(("arbitrary",)), name=name)(kc, part, land)


def allgather_sum(v, *, name):
    m_per, n = v.shape

    def body(x_ref, out_ref, sum_ref, send_sems, recv_sems, local_sem):
        x, y, c, chips = _place()
        me, sibling = (x, y, c), (x, y, 1 - c)

        def rows(px, py, pc):
            return out_ref.at[pl.ds(pl.multiple_of((4 * px + 2 * py + pc) * m_per, 8), m_per), :]

        def copy(k, block, to, src=None):
            return pltpu.make_async_remote_copy(
                src_ref=rows(*block) if src is None else src, dst_ref=rows(*block),
                send_sem=send_sems.at[k], recv_sem=recv_sems.at[k], device_id=to, device_id_type=MESH)

        mine = pltpu.make_async_copy(x_ref, rows(*me), local_sem)
        mine.start()
        first = [copy(0, me, sibling, src=x_ref)]
        first += [copy(1 + j, me, (*chip, c), src=x_ref) for j, chip in enumerate(chips)]
        for cp in first:
            cp.start()
        passed = [copy(4 + j, (*chip, c), sibling) for j, chip in enumerate(chips)]
        for j, chip in enumerate(chips):
            copy(1 + j, (*chip, c), me).wait_recv()
            passed[j].start()
        copy(0, sibling, me).wait_recv()
        for j, chip in enumerate(chips):
            copy(4 + j, (*chip, 1 - c), me).wait_recv()
        for cp in first + passed:
            cp.wait_send()
        mine.wait()
        acc = out_ref[0:m_per, :]
        for d in range(1, N_DEV):
            acc = acc + out_ref[d * m_per:(d + 1) * m_per, :]
        sum_ref[...] = acc

    vm = pl.BlockSpec(memory_space=pltpu.VMEM)
    return pl.pallas_call(
        body, in_specs=[vm], out_specs=[vm, vm],
        out_shape=[jax.ShapeDtypeStruct((N_DEV * m_per, n), v.dtype), jax.ShapeDtypeStruct((m_per, n), v.dtype)],
        scratch_shapes=[pltpu.SemaphoreType.DMA((7,)), pltpu.SemaphoreType.DMA((7,)), pltpu.SemaphoreType.DMA],
        compiler_params=pltpu.CompilerParams(vmem_limit_bytes=VMEM_LIMIT_BYTES), name=name)(v)


def allreduce_two_level(v, *, name):
    m, n = v.shape
    h = m // 2

    def body(x_ref, out_ref, sib_ref, chip_ref, sems_send, sems_recv):
        x, y, c, chips = _place()
        k_me = 2 * x + y
        sib = (x, y, 1 - c)
        mine = pl.ds(pl.multiple_of(c * h, 8), h)
        other = pl.ds(pl.multiple_of((1 - c) * h, 8), h)

        def copy(q, src, dst, to):
            return pltpu.make_async_remote_copy(src_ref=src, dst_ref=dst, send_sem=sems_send.at[q], recv_sem=sems_recv.at[q],
                                                device_id=to, device_id_type=MESH)

        first = copy(0, x_ref.at[other], sib_ref, sib)
        first.start()
        first.wait()
        chip_ref[k_me] = x_ref[mine, :] + sib_ref[...]
        sends = [copy(1 + r, chip_ref.at[k_me], chip_ref.at[k_me], (px, py, c)) for r, (px, py) in enumerate(chips)]
        for cp in sends:
            cp.start()
        for r, (px, py) in enumerate(chips):
            copy(1 + r, chip_ref.at[2 * px + py], chip_ref.at[2 * px + py], (px, py, c)).wait_recv()
        for cp in sends:
            cp.wait_send()
        total = ((chip_ref[0] + chip_ref[1]) + chip_ref[2]) + chip_ref[3]
        out_ref[mine, :] = total
        last = copy(4, out_ref.at[mine], out_ref.at[mine], sib)
        last.start()
        copy(4, out_ref.at[other], out_ref.at[other], sib).wait_recv()
        last.wait_send()

    vm = pl.BlockSpec(memory_space=pltpu.VMEM)
    return pl.pallas_call(
        body, in_specs=[vm], out_specs=vm, out_shape=jax.ShapeDtypeStruct((m, n), v.dtype),
        scratch_shapes=[pltpu.VMEM((h, n), v.dtype), pltpu.VMEM((N_CHIPS, h, n), v.dtype),
                        pltpu.SemaphoreType.DMA((5,)), pltpu.SemaphoreType.DMA((5,))],
        compiler_params=pltpu.CompilerParams(vmem_limit_bytes=VMEM_LIMIT_BYTES), name=name)(v)


PACK_W = D_MODEL


def _pack_rows(shape):
    return -(-math.prod(shape) // PACK_W)


def _pack(arrs):
    cols = []
    for a in arrs:
        f = a.reshape(-1)
        pad = (-f.shape[0]) % PACK_W
        cols.append((jnp.pad(f, (0, pad)) if pad else f).reshape(-1, PACK_W))
    out = jnp.concatenate(cols, axis=0)
    pad = (-out.shape[0]) % 16
    return jnp.pad(out, ((0, pad), (0, 0)))


def _unpack(buf, shapes):
    outs, r = [], 0
    for s in shapes:
        nel = math.prod(s)
        nr = _pack_rows(s)
        outs.append(buf[r:r + nr].reshape(-1)[:nel].reshape(s))
        r += nr
    return outs


GA_CONV_OUT, GA_MIX_OUT, GA_WQ, GA_WO, GA_DOWN, GA_UP, GA_ROWS = 0, 256, 512, 768, 1024, 2048, 3072
G1_DOWN, G1_UP, G1_ROWS = 0, 1024, 2048
G2_CONV_OUT, G2_MIX_OUT, G2_WQ, G2_WO, G2_ROWS = 0, 256, 512, 768, 1024


def kernel(x, mem, in_norm_g, in_norm_b, w_in, conv_dw, conv_db, conv_norm_g, conv_norm_b, w_conv_out, ssm_log_step, ssm_lambda_re, ssm_lambda_im, ssm_b_re, ssm_b_im, ssm_c_re, ssm_c_im, ssm_d, w_ssm_glu, w_mix_out, ln1_g, ln1_b, xa_wq, xa_wkv, xa_wo, ln2_g, ln2_b, mlp_w_up, mlp_w_down, ln3_g, ln3_b, loss_target, m_in_norm_g, m_in_norm_b, m_w_in, m_conv_dw, m_conv_db, m_conv_norm_g, m_conv_norm_b, m_w_conv_out, m_ssm_log_step, m_ssm_lambda_re, m_ssm_lambda_im, m_ssm_b_re, m_ssm_b_im, m_ssm_c_re, m_ssm_c_im, m_ssm_d, m_w_ssm_glu, m_w_mix_out, m_ln1_g, m_ln1_b, m_xa_wq, m_xa_wkv, m_xa_wo, m_ln2_g, m_ln2_b, m_mlp_w_up, m_mlp_w_down, m_ln3_g, m_ln3_b, v_in_norm_g, v_in_norm_b, v_w_in, v_conv_dw, v_conv_db, v_conv_norm_g, v_conv_norm_b, v_w_conv_out, v_ssm_log_step, v_ssm_lambda_re, v_ssm_lambda_im, v_ssm_b_re, v_ssm_b_im, v_ssm_c_re, v_ssm_c_im, v_ssm_d, v_w_ssm_glu, v_w_mix_out, v_ln1_g, v_ln1_b, v_xa_wq, v_xa_wkv, v_xa_wo, v_ln2_g, v_ln2_b, v_mlp_w_up, v_mlp_w_down, v_ln3_g, v_ln3_b):
    D = D_MODEL
    xs = x[0]
    T = xs.shape[0]
    mems = mem[0]
    NM = mems.shape[0]
    tgt = loss_target[0]
    my_c = lax.axis_index("c")
    k_me = 2 * lax.axis_index("x") + lax.axis_index("y")
    c_arr = jnp.reshape(my_c, (1,)).astype(jnp.int32)
    k_arr = jnp.reshape(k_me, (1,)).astype(jnp.int32)

    sh_a = jnp.concatenate([w_conv_out[0], w_mix_out[0], xa_wq[0], xa_wo[0], mlp_w_down[0], mlp_w_up[0]], axis=0).astype(BF16)
    def own_block(shard):
        buf = lax.empty((N_CHIPS,) + shard.shape, shard.dtype)
        return lax.dynamic_update_slice(buf, shard[None], (k_me, 0, 0))

    dw_pad = jnp.pad(conv_dw[0], ((0, CONV_HALO - CONV_K), (0, 0)))
    (GIN,) = allgather_weights([own_block(w_in[0].astype(BF16))], name="gather_w_in")
    ag_bufs = [own_block(sh_a), GIN] + [own_block(s) for s in (xa_wkv[0].astype(BF16), w_ssm_glu[0].astype(BF16), dw_pad)]
    ag_pieces = [(4, 0, CONV_HALO), (0, GA_CONV_OUT, 256), (3, 0, D_SSM), (0, GA_MIX_OUT, 256), (0, GA_WQ, 256),
                 (2, 0, D), (0, GA_WO, 256), (0, GA_UP, D), (0, GA_DOWN, D)]
    ag_send, ag_recv, GA, GIN, GKV, GGLU, GDW = gather_start(ag_bufs, ag_pieces, name="gather_start")

    def w_rowshard(row0):
        return dict(b_spec=((N_CHIPS, 256, D), lambda i, j, k: (0, row0 // 256, 0)), b_view=(D, D), tn=D, tk=D)

    _, h0, h0b = ln_fwd(xs, in_norm_g, in_norm_b, name="ln0_fwd")
    p = mm_nn(h0b, GIN, ((None, D, 1152), lambda i, j, k: (j, 0, 0)), D_IN, tn=1152, tk=D, name="mm_w_in")[0]
    GA, GGLU, GDW = gather_wait(
        ag_send, ag_recv, [GA, GGLU, GDW],
        [(2, 0, CONV_HALO, 0), (0, GA_CONV_OUT, 256, 1), (1, 0, D_SSM, 2), (0, GA_MIX_OUT, 256, 3)], p, name="gather_wait_mixer")
    dw_taps = jnp.transpose(GDW, (1, 0, 2)).reshape(CONV_HALO, D)
    c_pre, actb = conv_fwd(p, dw_taps, conv_db, conv_norm_g[0].reshape(1, D), conv_norm_b[0].reshape(1, D), name="conv_fwd")
    ya = mm_nn(actb, GA, N=D, out_dtype=BF16, name="mm_conv_out", **w_rowshard(GA_CONV_OUT))[0]

    lstep, lre, lim = ssm_log_step[0], ssm_lambda_re[0], ssm_lambda_im[0]
    bre, bim, cre, cim = ssm_b_re[0], ssm_b_im[0], ssm_c_re[0], ssm_c_im[0]
    (ar, ai, bbr, bbi), disc_vjp = jax.vjp(_ssm_discretise, lstep, lre, lim, bre, bim)
    Br, Bi = _blockdiag_in(bbr), _blockdiag_in(bbi)
    Cr, Ci = _blockdiag_out(cre), _blockdiag_out(cim)
    lam_r, lam_i = ar.reshape(1, -1), ai.reshape(1, -1)
    dvec = ssm_d[0].reshape(1, D_SSM)
    xr, xi, yssm = ssm_seg_fwd(p, Br, Bi, Cr, Ci, lam_r, lam_i, dvec, name="ssm_fwd")
    z = mm_nn(yssm, GGLU, ((None, D_SSM, 512), lambda i, j, k: (j, 0, 0)), 2 * D, tn=512, tk=D_SSM, out_dtype=BF16,
              name="mm_ssm_glu")[0]
    mergedb = merge_fwd(p, ya, z, name="merge_fwd")
    tm_ln = _pick(T, [512, 256, 128])
    row_spec = ((1, D), lambda i, j, k: (0, 0))

    def ln_epilogue(acc, res, g, b):
        r = ALPHA * res + acc
        xhat, _ = _ln_stats(r)
        h = xhat * g + b
        return r, h, h

    def mm_ln(a, row0, res, g, b, name):
        return mm_nn(a, GA, N=D, tm=tm_ln, extras=(res, g.reshape(1, D), b.reshape(1, D)),
                     extra_specs=[_mn(tm_ln, D), row_spec, row_spec], epilogue=ln_epilogue, out_dtypes=[F32, F32, BF16],
                     name=name, **w_rowshard(row0))

    r1, h1, h1b = mm_ln(mergedb, GA_MIX_OUT, h0, ln1_g[0], ln1_b[0], "mm_mix_out_ln1")
    GA, GKV = gather_wait(ag_send, ag_recv, [GA, GKV], [(0, GA_WQ, 256, 4), (1, 0, D, 5), (0, GA_WO, 256, 6)], r1,
                          name="gather_wait_attn")

    qb = mm_nn(h1b, GA, N=D, out_dtype=BF16, name="mm_wq", **w_rowshard(GA_WQ))[0]
    kv = mm_nn(mems, GKV, ((None, D, 512), lambda i, j, k: (j, 0, 0)), 2 * D, tn=512, tk=D, name="mm_wkv")[0]
    ob = attn_fwd(qb, kv, name="attn_fwd")
    r2, h2, h2b = mm_ln(ob, GA_WO, h1, ln2_g[0], ln2_b[0], "mm_wo_ln2")
    (GA,) = gather_wait(ag_send, ag_recv, [GA], [(0, GA_UP, D, 7), (0, GA_DOWN, D, 8)], r2, name="gather_wait_mlp")

    def relu2(acc):
        zr = jnp.maximum(acc, 0.0)
        return (zr * zr,)

    zzb = mm_nn(h2b, GA, ((None, D, D), lambda i, j, k: (j, GA_UP // D, 0)), D_FF, tn=D, tk=D,
                out_dtype=BF16, epilogue=relu2, name="mm_up")[0]
    ff = mm_nn(zzb, GA, ((N_CHIPS, D, D), lambda i, j, k: (0, GA_DOWN // D, 0)), D, tm=_pick(T, [512, 256, 128]), tn=D, tk=D_FF,
               b_view=(D_FF, D), name="mm_down")[0]
    dr3, dr3b, dg3, db3, sq = ln_loss_bwd(ff, h2, ln3_g[0], ln3_b[0], tgt, name="ln3_loss_bwd")

    def rs_begin(grads, rnd):
        return scatter_start(grads, name=f"rs{rnd}_scatter_start")

    g1_shape = jax.ShapeDtypeStruct((N_CHIPS, G1_ROWS, D), BF16)
    g2_shape = jax.ShapeDtypeStruct((N_CHIPS, G2_ROWS, D), BF16)
    dzpreb = mm_nt(dr3b, GA, ((None, D, D), lambda i, j, k: (j, GA_DOWN // D, 0)), D_FF, tn=D, tk=D, out_dtype=BF16,
                   extras=(zzb,), epilogue=lambda acc, zz: (acc * (2.0 * jnp.sqrt(zz.astype(F32))),), name="mm_down_t")[0]
    G1g = mm_tn(zzb, dr3b, tm=D, tn=D, tk=T, out_spec=((None, D, D), lambda i, j, k: (i, G1_DOWN // D, 0)),
                out_shape=g1_shape, name="mm_down_g")
    G1g = mm_tn(h2b, dzpreb, tm=D, tn=D, tk=T, out_spec=((None, D, D), lambda i, j, k: (j, G1_UP // D, 0)),
                out_shape=g1_shape, out_buf=G1g, name="mm_up_g")
    round1 = rs_begin([G1g], 1)
    dh2 = mm_nt(dzpreb, GA, ((N_CHIPS, D, D), lambda i, j, k: (0, GA_UP // D, 0)), D, tm=_pick(T, [512, 256, 128]), tn=D,
                tk=D_FF, b_chunks=N_CHIPS, extras=(dr3,), epilogue=lambda acc, d: (acc + ALPHA * d,),
                after=(round1[2][0],), name="mm_up_t")[0]
    dr2, dr2b, dg2, db2 = ln_bwd(r2, dh2, ln2_g[0], name="ln2_bwd")

    def g_rowshard(row0, out_buf):
        return dict(tm=D, tn=D, out_spec=((N_CHIPS, 256, D), lambda i, j, k: (0, row0 // 256, 0)), out_shape=g2_shape,
                    out_buf=out_buf)

    dob = mm_nt(dr2b, GA, N=D, out_dtype=BF16, name="mm_wo_t", **w_rowshard(GA_WO))[0]
    G2g = mm_tn(ob, dr2b, name="mm_wo_g", **g_rowshard(G2_WO, None))
    dqb, dkv = attn_bwd(qb, kv, dob, name="attn_bwd")
    G2g = mm_tn(h1b, dqb, name="mm_wq_g", **g_rowshard(G2_WQ, G2g))
    GKVg = mm_tn(mems, dkv, tm=D, tn=512, tk=NM, out_spec=((None, D, 512), lambda i, j, k: (j, 0, 0)),
                 out_shape=jax.ShapeDtypeStruct((N_CHIPS, D, 512), BF16), name="mm_wkv_g")
    dh1 = mm_nt(dqb, GA, N=D, extras=(dr2,), epilogue=lambda acc, d: (acc + ALPHA * d,), name="mm_wq_t",
                **w_rowshard(GA_WQ))[0]
    dr1, dr1b, dg1, db1 = ln_bwd(r1, dh1, ln1_g[0], name="ln1_bwd")

    dmerged = mm_nt(dr1b, GA, N=D, out_dtype=BF16, name="mm_mix_t", **w_rowshard(GA_MIX_OUT))[0]
    G2g = mm_tn(mergedb, dr1b, name="mm_mix_g", **g_rowshard(G2_MIX_OUT, G2g))
    dyab, dgatesb, dzb = merge_bwd(dmerged, p, ya, z, name="merge_bwd")
    GGLUg = mm_tn(yssm, dzb, tm=D_SSM, tn=512, out_spec=((None, D_SSM, 512), lambda i, j, k: (j, 0, 0)),
                  out_shape=jax.ShapeDtypeStruct((N_CHIPS, D_SSM, 512), BF16), name="mm_glu_g")
    dyssm = mm_nt(dzb, GGLU, ((N_CHIPS, D_SSM, 512), lambda i, j, k: (0, 0, 0)), D_SSM, tn=D_SSM, tk=2 * D, b_chunks=N_CHIPS,
                  name="mm_glu_t")[0]
    dub, dBr, dBi, dCr, dCi, dar8, dai8, dd8 = ssm_seg_bwd(dyssm, p, xr, xi, Br, Bi, Cr, Ci, lam_r, lam_i, dvec,
                                                           name="ssm_bwd")
    dar = jnp.sum(dar8, axis=0).reshape(SSM_GROUPS, SSM_STATE)
    dai = jnp.sum(dai8, axis=0).reshape(SSM_GROUPS, SSM_STATE)
    g_lstep, g_lre, g_lim, g_bre, g_bim = disc_vjp((dar, dai, _diag_in(dBr), _diag_in(dBi)))
    g_cre, g_cim = _diag_out(dCr), _diag_out(dCi)
    g_d = jnp.sum(dd8, axis=0).reshape(1, D_SSM)

    dact = mm_nt(dyab, GA, N=D, name="mm_conv_out_t", **w_rowshard(GA_CONV_OUT))[0]
    G2g = mm_tn(actb, dyab, name="mm_conv_out_g", **g_rowshard(G2_CONV_OUT, G2g))
    round2 = rs_begin([G2g, GKVg, GGLUg], 2)
    dc, dng, dnb, ddb = conv_bwd_norm(dact, c_pre, conv_norm_g[0].reshape(1, D), conv_norm_b[0].reshape(1, D),
                                      round2[2][0], name="conv_bwd_norm")
    dpb, ddw = conv_bwd_taps(dc, p, dw_taps, dub, dgatesb, name="conv_bwd_taps")
    GINg = mm_tn(h0b, dpb, tm=D, tn=1152, tk=T, out_spec=((None, D, 1152), lambda i, j, k: (j, 0, 0)),
                 out_shape=jax.ShapeDtypeStruct((N_CHIPS, D, 1152), BF16), name="mm_w_in_g")
    round3 = rs_begin([GINg], 3)
    dh0 = mm_nt(dpb, GIN, ((N_CHIPS, D, 1152), lambda i, j, k: (0, 0, 0)), D, tm=_pick(T, [512, 256, 128]), tn=D, tk=D_IN,
                b_chunks=N_CHIPS, extras=(dr1,), epilogue=lambda acc, d: (acc + ALPHA * d,), after=(round3[2][0],),
                name="mm_w_in_t")[0]
    gx, _, dg0, db0 = ln_bwd(xs, dh0, in_norm_g, name="ln0_bwd")

    kc_arr = jnp.concatenate([k_arr, c_arr])
    landed = scatter_wait([round1, round2, round3], gx, name="rs_scatter_wait")
    tags = ["mlp", "sq", "kv", "glu", "in"]
    pairs = [(pt, l2) for parts, lands2 in landed for pt, l2 in zip(parts, lands2)]
    halves = [add_partials(pt, l2, kc_arr, name="rs_add_partials_" + t) for (pt, l2), t in zip(pairs, tags)]
    g1, g2, gKV, gGLU, gIN = join_halves(halves, name="rs_join_halves")

    small_names = ["in_norm_g", "in_norm_b", "conv_db", "conv_norm_g", "conv_norm_b", "ssm_log_step", "ssm_lambda_re",
                   "ssm_lambda_im", "ssm_b_re", "ssm_b_im", "ssm_c_re", "ssm_c_im", "ssm_d", "ln1_g", "ln1_b",
                   "ln2_g", "ln2_b", "ln3_g", "ln3_b"]
    small_w = [in_norm_g, in_norm_b, conv_db, conv_norm_g, conv_norm_b, ssm_log_step, ssm_lambda_re, ssm_lambda_im,
               ssm_b_re, ssm_b_im, ssm_c_re, ssm_c_im, ssm_d, ln1_g, ln1_b, ln2_g, ln2_b, ln3_g, ln3_b]
    small_m = [m_in_norm_g, m_in_norm_b, m_conv_db, m_conv_norm_g, m_conv_norm_b, m_ssm_log_step, m_ssm_lambda_re,
               m_ssm_lambda_im, m_ssm_b_re, m_ssm_b_im, m_ssm_c_re, m_ssm_c_im, m_ssm_d, m_ln1_g, m_ln1_b, m_ln2_g,
               m_ln2_b, m_ln3_g, m_ln3_b]
    small_v = [v_in_norm_g, v_in_norm_b, v_conv_db, v_conv_norm_g, v_conv_norm_b, v_ssm_log_step, v_ssm_lambda_re,
               v_ssm_lambda_im, v_ssm_b_re, v_ssm_b_im, v_ssm_c_re, v_ssm_c_im, v_ssm_d, v_ln1_g, v_ln1_b, v_ln2_g,
               v_ln2_b, v_ln3_g, v_ln3_b]
    small_g = [dg0, db0, ddb, dng, dnb, g_lstep, g_lre, g_lim, g_bre, g_bim, g_cre, g_cim, g_d, dg1, db1, dg2, db2, dg3, db3]
    small_shapes = [w.shape for w in small_w]
    n_small_rows = _pack(small_w).shape[0]
    packed_g = _pack(small_g + [ddw, sq])
    summed = allreduce_two_level(packed_g, name="allreduce_small")
    small_rows = sum(_pack_rows(s) for s in small_shapes)
    dw_rows = _pack_rows((CONV_HALO, D))
    loss = 0.5 * summed[small_rows + dw_rows, 0] / D
    ddw_full = summed[small_rows:small_rows + dw_rows].reshape(CONV_HALO, D)
    g_dw = lax.dynamic_slice_in_dim(ddw_full, k_me * (D // N_CHIPS), D // N_CHIPS, axis=1)
    gs_packed = jnp.pad(summed[:small_rows], ((0, n_small_rows - small_rows), (0, 0)))

    res = {}

    def upd(nm, w, m, v, g_arr, row0=0):
        shp = w.shape
        w2, m2, v2 = (a.reshape(-1, shp[-1]) for a in (w, m, v))
        outs = adamw(w2, m2, v2, g_arr, row0, name="adamw_" + nm)
        res[nm] = tuple(o.reshape(shp) for o in outs)

    upd("w_conv_out", w_conv_out, m_w_conv_out, v_w_conv_out, g2, G2_CONV_OUT)
    upd("w_mix_out", w_mix_out, m_w_mix_out, v_w_mix_out, g2, G2_MIX_OUT)
    upd("xa_wq", xa_wq, m_xa_wq, v_xa_wq, g2, G2_WQ)
    upd("xa_wo", xa_wo, m_xa_wo, v_xa_wo, g2, G2_WO)
    upd("mlp_w_down", mlp_w_down, m_mlp_w_down, v_mlp_w_down, g1, G1_DOWN)
    upd("mlp_w_up", mlp_w_up, m_mlp_w_up, v_mlp_w_up, g1, G1_UP)
    upd("w_in", w_in, m_w_in, v_w_in, gIN)
    upd("xa_wkv", xa_wkv, m_xa_wkv, v_xa_wkv, gKV)
    upd("w_ssm_glu", w_ssm_glu, m_w_ssm_glu, v_w_ssm_glu, gGLU)
    pad_dw = lambda a: jnp.pad(a[0], ((0, CONV_HALO - CONV_K), (0, 0)))
    dw_outs = adamw(pad_dw(conv_dw), pad_dw(m_conv_dw), pad_dw(v_conv_dw), g_dw, 0, name="adamw_conv_dw")
    res["conv_dw"] = tuple(o[:CONV_K][None] for o in dw_outs)
    sm_outs = adamw(_pack(small_w), _pack(small_m), _pack(small_v), gs_packed, 0, name="adamw_small")
    sm_un = [_unpack(o, small_shapes) for o in sm_outs]
    for idx, nm in enumerate(small_names):
        res[nm] = tuple(sm_un[q][idx] for q in range(4))

    order = ["in_norm_g", "in_norm_b", "w_in", "conv_dw", "conv_db", "conv_norm_g", "conv_norm_b", "w_conv_out",
             "ssm_log_step", "ssm_lambda_re", "ssm_lambda_im", "ssm_b_re", "ssm_b_im", "ssm_c_re", "ssm_c_im", "ssm_d",
             "w_ssm_glu", "w_mix_out", "ln1_g", "ln1_b", "xa_wq", "xa_wkv", "xa_wo", "ln2_g", "ln2_b", "mlp_w_up",
             "mlp_w_down", "ln3_g", "ln3_b"]
    return (loss, gx[None], *[res[n][0] for n in order], *[res[n][1] for n in order],
            *[res[n][2] for n in order], *[res[n][3] for n in order])
```

```python
import math

import jax
import jax.numpy as jnp
from jax import lax
from jax.experimental import pallas as pl
from jax.experimental.pallas import tpu as pltpu

F32 = jnp.float32
BF16 = jnp.bfloat16
MESH = pl.DeviceIdType.MESH

D_MODEL = 1024
N_HEADS = 4
HEAD_DIM = D_MODEL // N_HEADS
CONV_K = 31
CONV_HALO = 32
D_SSM = 512
SSM_GROUPS = 32
SSM_GROUP = 16
SSM_STATE = 64
SSM_BLOCKS = 4
SSM_BLOCK_IN = D_SSM // SSM_BLOCKS
SSM_BLOCK_STATE = SSM_GROUPS * SSM_STATE // SSM_BLOCKS
D_FF = 4096
D_IN = 4608
LN_EPS = 1e-5
ALPHA = (2.0 * 1) ** 0.25
N_CHIPS = 4
N_DEV = 8
ADAM_LR, ADAM_B1, ADAM_B2, ADAM_EPS, ADAM_WD, ADAM_STEP = 0.001, 0.9, 0.999, 1e-08, 0.01, 10
VMEM_LIMIT_BYTES = 56 * 1024 * 1024


def _pick(dim, cands):
    for c in cands:
        if dim % c == 0:
            return c
    return dim


def _cparams(sem=None):
    return pltpu.CompilerParams(dimension_semantics=sem, vmem_limit_bytes=VMEM_LIMIT_BYTES)


def _sigmoid(x):
    return 1.0 / (1.0 + jnp.exp(-x))


_DIMS = {"nn": (((1,), (0,)), ((), ())), "nt": (((1,), (1,)), ((), ())), "tn": (((0,), (0,)), ((), ()))}


def matmul(a, b, *, mode, M, N, K, tm, tn, tk, a_spec, b_spec, out_specs, out_shapes, name,
           extras=(), extra_specs=(), epilogue=None, alias_buf=None, b_view=None, after=(), b_chunks=None):
    nk = K // tk
    ne = len(extras)
    no = len(out_shapes)
    na = (0 if alias_buf is None else 1) + len(after)
    dims = _DIMS[mode]

    def body(*refs):
        a_ref, b_ref = refs[0], refs[1]
        e_refs = refs[2:2 + ne]
        o_refs = refs[2 + ne + na:2 + ne + na + no]

        def finish(acc):
            outs = (acc,) if epilogue is None else epilogue(acc, *[r[...] for r in e_refs])
            for o, r in zip(outs, o_refs):
                r[...] = o.astype(r.dtype).reshape(r.shape)

        if b_chunks:
            kc = a_ref.shape[1] // b_chunks
            prod = None
            for q in range(b_chunks):
                part = lax.dot_general(a_ref[:, q * kc:(q + 1) * kc].astype(BF16), b_ref[q].astype(BF16), dims,
                                       preferred_element_type=F32)
                prod = part if prod is None else prod + part
        else:
            b_blk = b_ref[...] if b_view is None else b_ref[...].reshape(b_view)
            prod = lax.dot_general(a_ref[...].astype(BF16), b_blk.astype(BF16), dims, preferred_element_type=F32)
        if nk == 1:
            finish(prod)
        else:
            acc_ref = refs[-1]
            k = pl.program_id(2)

            @pl.when(k == 0)
            def _():
                acc_ref[...] = prod

            @pl.when(k > 0)
            def _():
                acc_ref[...] += prod

            @pl.when(k == nk - 1)
            def _():
                finish(acc_ref[...])

    in_specs = [pl.BlockSpec(*a_spec), pl.BlockSpec(*b_spec)] + [pl.BlockSpec(*s) for s in extra_specs]
    ins = [a, b, *extras]
    if alias_buf is not None:
        in_specs.append(pl.BlockSpec(memory_space=pl.ANY))
        ins.append(alias_buf)
    for dep in after:
        in_specs.append(pl.BlockSpec(memory_space=pl.ANY))
        ins.append(dep)
    res = pl.pallas_call(
        body,
        grid=(M // tm, N // tn, nk),
        in_specs=in_specs,
        out_specs=[pl.BlockSpec(*s) for s in out_specs],
        out_shape=out_shapes,
        scratch_shapes=[] if nk == 1 else [pltpu.VMEM((tm, tn), F32)],
        input_output_aliases={2 + ne: 0} if alias_buf is not None else {},
        compiler_params=_cparams(("parallel", "parallel", "arbitrary")),
        name=name,
    )(*ins)
    return res


def _mn(tm, tn):
    return ((tm, tn), lambda i, j, k: (i, j))


def mm_nn(a, b_arr, b_spec, N, *, name, tm=None, tn, tk, out_dtype=F32, extras=(), epilogue=None, out_dtypes=None,
          b_view=None, extra_specs=None):
    M, K = a.shape
    tm = tm or _pick(M, [1024, 512, 256, 128])
    dts = out_dtypes or [out_dtype]
    return matmul(a, b_arr, mode="nn", M=M, N=N, K=K, tm=tm, tn=tn, tk=tk,
                  a_spec=((tm, tk), lambda i, j, k: (i, k)), b_spec=b_spec, b_view=b_view,
                  out_specs=[_mn(tm, tn)] * len(dts), out_shapes=[jax.ShapeDtypeStruct((M, N), d) for d in dts],
                  extras=extras, extra_specs=extra_specs or [_mn(tm, tn)] * len(extras), epilogue=epilogue, name=name)


def mm_nt(a, b_arr, b_spec, N, *, name, tm=None, tn, tk, out_dtype=F32, extras=(), epilogue=None, out_dtypes=None,
          b_view=None, after=(), b_chunks=None):
    M, K = a.shape
    tm = tm or _pick(M, [1024, 512, 256, 128])
    dts = out_dtypes or [out_dtype]
    return matmul(a, b_arr, mode="nt", M=M, N=N, K=K, tm=tm, tn=tn, tk=tk, after=after, b_chunks=b_chunks,
                  a_spec=((tm, tk), lambda i, j, k: (i, k)), b_spec=b_spec, b_view=b_view,
                  out_specs=[_mn(tm, tn)] * len(dts), out_shapes=[jax.ShapeDtypeStruct((M, N), d) for d in dts],
                  extras=extras, extra_specs=[_mn(tm, tn)] * len(extras), epilogue=epilogue, name=name)


def mm_tn(a, b, *, name, tm, tn, tk=None, out_spec, out_shape, out_buf=None):
    K, M = a.shape
    N = b.shape[1]
    tk = tk or _pick(K, [2048, 1024, 512, 256, 128])
    return matmul(a, b, mode="tn", M=M, N=N, K=K, tm=tm, tn=tn, tk=tk,
                  a_spec=((tk, tm), lambda i, j, k: (k, i)), b_spec=((tk, tn), lambda i, j, k: (k, j)),
                  out_specs=[out_spec], out_shapes=[out_shape], alias_buf=out_buf, name=name)[0]


def _rows(tc, w, cb=0):
    return pl.BlockSpec((tc, w), lambda i: (i, cb))


def _const(shape):
    return pl.BlockSpec(shape, lambda i: tuple([0] * len(shape)))


def _ln_stats(r):
    mu = jnp.mean(r, axis=-1, keepdims=True)
    xc = r - mu
    var = jnp.mean(xc * xc, axis=-1, keepdims=True)
    rstd = lax.rsqrt(var + LN_EPS)
    return xc * rstd, rstd


def _rowsum8(v):
    tc, w = v.shape
    return jnp.sum(v.reshape(tc // 8, 8, w), axis=0)


def ln_fwd(x, g, b, *, name):
    T, D = x.shape
    tc = _pick(T, [1024, 512, 256, 128])

    def body(x_ref, g_ref, b_ref, h_ref, hb_ref):
        xhat, _ = _ln_stats(x_ref[...])
        y = xhat * g_ref[...] + b_ref[...]
        h_ref[...] = y
        hb_ref[...] = y.astype(BF16)

    return pl.pallas_call(
        body, grid=(T // tc,), in_specs=[_rows(tc, D), _const((1, D)), _const((1, D))], out_specs=[_rows(tc, D)] * 2,
        out_shape=[jax.ShapeDtypeStruct((T, D), F32), jax.ShapeDtypeStruct((T, D), BF16)],
        compiler_params=_cparams(("arbitrary",)), name=name)(x, g.reshape(1, D), b.reshape(1, D))


def ln_bwd(r, dy, g, *, name, bf16_copy=True):
    T, D = r.shape
    tc = _pick(T, [1024, 512, 256, 128])
    nt = T // tc

    def body(r_ref, dy_ref, g_ref, dr_ref, *rest):
        drb_ref = rest[0] if bf16_copy else None
        dg_ref, db_ref, accg, accb = rest[-4:]
        i = pl.program_id(0)

        @pl.when(i == 0)
        def _():
            accg[...] = jnp.zeros_like(accg)
            accb[...] = jnp.zeros_like(accb)

        xhat, rstd = _ln_stats(r_ref[...])
        dy = dy_ref[...]
        dxh = dy * g_ref[...]
        m1 = jnp.mean(dxh, axis=-1, keepdims=True)
        m2 = jnp.mean(dxh * xhat, axis=-1, keepdims=True)
        dr = rstd * (dxh - m1 - xhat * m2)
        dr_ref[...] = dr
        if bf16_copy:
            drb_ref[...] = dr.astype(BF16)
        accg[...] += _rowsum8(dy * xhat)
        accb[...] += _rowsum8(dy)

        @pl.when(i == nt - 1)
        def _():
            dg_ref[...] = jnp.sum(accg[...], axis=0, keepdims=True)
            db_ref[...] = jnp.sum(accb[...], axis=0, keepdims=True)

    rows_out = [jax.ShapeDtypeStruct((T, D), F32)] + ([jax.ShapeDtypeStruct((T, D), BF16)] if bf16_copy else [])
    return pl.pallas_call(
        body, grid=(nt,), in_specs=[_rows(tc, D), _rows(tc, D), _const((1, D))],
        out_specs=[_rows(tc, D)] * len(rows_out) + [_const((1, D)), _const((1, D))],
        out_shape=rows_out + [jax.ShapeDtypeStruct((1, D), F32), jax.ShapeDtypeStruct((1, D), F32)],
        scratch_shapes=[pltpu.VMEM((8, D), F32), pltpu.VMEM((8, D), F32)],
        compiler_params=_cparams(("arbitrary",)), name=name)(r, dy, g.reshape(1, D))


def ln_loss_bwd(x, res, g, b, target, *, name):
    T, D = x.shape
    tc = _pick(T, [1024, 512, 256, 128])
    nt = T // tc

    def body(x_ref, res_ref, g_ref, b_ref, t_ref, dr_ref, drb_ref, dg_ref, db_ref, loss_ref, accg, accb, accl):
        i = pl.program_id(0)

        @pl.when(i == 0)
        def _():
            accg[...] = jnp.zeros_like(accg)
            accb[...] = jnp.zeros_like(accb)
            accl[...] = jnp.zeros_like(accl)

        r = ALPHA * res_ref[...] + x_ref[...]
        xhat, rstd = _ln_stats(r)
        e = xhat * g_ref[...] + b_ref[...] - t_ref[...]
        dy = e * (1.0 / D)
        dxh = dy * g_ref[...]
        m1 = jnp.mean(dxh, axis=-1, keepdims=True)
        m2 = jnp.mean(dxh * xhat, axis=-1, keepdims=True)
        dr = rstd * (dxh - m1 - xhat * m2)
        dr_ref[...] = dr
        drb_ref[...] = dr.astype(BF16)
        accg[...] += _rowsum8(dy * xhat)
        accb[...] += _rowsum8(dy)
        accl[...] += _rowsum8(e * e)

        @pl.when(i == nt - 1)
        def _():
            dg_ref[...] = jnp.sum(accg[...], axis=0, keepdims=True)
            db_ref[...] = jnp.sum(accb[...], axis=0, keepdims=True)
            s = jnp.sum(jnp.sum(accl[...], axis=0, keepdims=True), axis=1, keepdims=True)
            loss_ref[...] = jnp.broadcast_to(s, (1, 128))

    return pl.pallas_call(
        body, grid=(nt,), in_specs=[_rows(tc, D), _rows(tc, D), _const((1, D)), _const((1, D)), _rows(tc, D)],
        out_specs=[_rows(tc, D), _rows(tc, D), _const((1, D)), _const((1, D)), _const((1, 128))],
        out_shape=[jax.ShapeDtypeStruct((T, D), F32), jax.ShapeDtypeStruct((T, D), BF16),
                   jax.ShapeDtypeStruct((1, D), F32), jax.ShapeDtypeStruct((1, D), F32), jax.ShapeDtypeStruct((1, 128), F32)],
        scratch_shapes=[pltpu.VMEM((8, D), F32)] * 3,
        compiler_params=_cparams(("arbitrary",)), name=name)(x, res, g.reshape(1, D), b.reshape(1, D), target)


def _halo_prev(tc):
    per = tc // CONV_HALO
    return lambda i: jnp.maximum(i * per - 1, 0)


CONV_ROWS = 32
CONV_TAP_GROUP = 4
CONV_TILE_UNROLL = 4


def _fill_shifts(S, nrows):
    for b in range(1, 8):
        S[b, 0:nrows - 8, :] = S[0, b:b + nrows - 8, :]


def _tap_sum(S, w_ref, offs, r0, nrows):
    acc = None
    for k, o in enumerate(offs):
        a, b = divmod(o, 8)
        term = w_ref[k:k + 1, :] * S[b, pl.ds(pl.multiple_of(r0 + 8 * a, 8), nrows), :]
        acc = term if acc is None else acc + term
    return acc


def conv_fwd(p, dw, db, ng, nb, *, name):
    T = p.shape[0]
    D = D_MODEL
    tc = _pick(T, [256, 128])
    prev = _halo_prev(tc)
    off = CONV_HALO - (CONV_K - 1)
    offs = [off + k for k in range(CONV_K)]

    def body(val_ref, gate_ref, valp_ref, gatep_ref, dw_ref, db_ref, ng_ref, nb_ref, c_ref, act_ref, S):
        i = pl.program_id(0)
        u_prev = valp_ref[...] * _sigmoid(gatep_ref[...])
        S[0, 0:CONV_HALO, :] = jnp.where(i > 0, u_prev, 0.0)
        S[0, CONV_HALO:CONV_HALO + tc, :] = val_ref[...] * _sigmoid(gate_ref[...])
        _fill_shifts(S, CONV_HALO + tc)

        def rows(j, carry):
            r0 = pl.multiple_of(j * CONV_ROWS, CONV_ROWS)
            c_ref[pl.ds(r0, CONV_ROWS), :] = _tap_sum(S, dw_ref, offs, r0, CONV_ROWS) + db_ref[...]
            return carry

        lax.fori_loop(0, tc // CONV_ROWS, rows, 0)
        c = c_ref[...]
        xhat, _ = _ln_stats(c)
        cn = xhat * ng_ref[...] + nb_ref[...]
        act_ref[...] = (cn * _sigmoid(cn)).astype(BF16)

    return pl.pallas_call(
        body, grid=(T // tc,),
        in_specs=[_rows(tc, D, 0), _rows(tc, D, 1),
                  pl.BlockSpec((CONV_HALO, D), lambda i: (prev(i), 0)), pl.BlockSpec((CONV_HALO, D), lambda i: (prev(i), 1)),
                  _const((CONV_HALO, D)), _const((1, D)), _const((1, D)), _const((1, D))],
        out_specs=[_rows(tc, D), _rows(tc, D)],
        out_shape=[jax.ShapeDtypeStruct((T, D), F32), jax.ShapeDtypeStruct((T, D), BF16)],
        scratch_shapes=[pltpu.VMEM((8, CONV_HALO + tc, D), F32)],
        compiler_params=_cparams(("arbitrary",)), name=name)(p, p, p, p, dw, db, ng, nb)


def conv_bwd_norm(dact, c_pre, ng, nb, after, *, name):
    T, D = c_pre.shape
    tc = _pick(T, [1024, 512, 256, 128])
    nt = T // tc

    def body(da_ref, c_ref, ng_ref, nb_ref, after_ref, dc_ref, dng_ref, dnb_ref, ddb_ref, accg, accb, accd):
        i = pl.program_id(0)

        @pl.when(i == 0)
        def _():
            accg[...] = jnp.zeros_like(accg)
            accb[...] = jnp.zeros_like(accb)
            accd[...] = jnp.zeros_like(accd)

        xhat, rstd = _ln_stats(c_ref[...])
        cn = xhat * ng_ref[...] + nb_ref[...]
        s = _sigmoid(cn)
        dcn = da_ref[...] * (s * (1.0 + cn * (1.0 - s)))
        dxh = dcn * ng_ref[...]
        m1 = jnp.mean(dxh, axis=-1, keepdims=True)
        m2 = jnp.mean(dxh * xhat, axis=-1, keepdims=True)
        dc = rstd * (dxh - m1 - xhat * m2)
        dc_ref[...] = dc
        accg[...] += _rowsum8(dcn * xhat)
        accb[...] += _rowsum8(dcn)
        accd[...] += _rowsum8(dc)

        @pl.when(i == nt - 1)
        def _():
            dng_ref[...] = jnp.sum(accg[...], axis=0, keepdims=True)
            dnb_ref[...] = jnp.sum(accb[...], axis=0, keepdims=True)
            ddb_ref[...] = jnp.sum(accd[...], axis=0, keepdims=True)

    vec = jax.ShapeDtypeStruct((1, D), F32)
    return pl.pallas_call(
        body, grid=(nt,), in_specs=[_rows(tc, D), _rows(tc, D), _const((1, D)), _const((1, D)), ANY],
        out_specs=[_rows(tc, D), _const((1, D)), _const((1, D)), _const((1, D))],
        out_shape=[jax.ShapeDtypeStruct((T, D), F32), vec, vec, vec],
        scratch_shapes=[pltpu.VMEM((8, D), F32)] * 3,
        compiler_params=_cparams(("arbitrary",)), name=name)(dact, c_pre, ng, nb, after)


def conv_bwd_taps(dc, p, dw, du_ssm, dgates, *, name):
    T, D = dc.shape
    tc = _pick(T, [256, 128])
    nt = T // tc
    per = tc // CONV_HALO
    prev = _halo_prev(tc)
    last_halo = T // CONV_HALO - 1
    nxt = lambda i: jnp.minimum((i + 1) * per, last_halo)
    off = CONV_HALO - (CONV_K - 1)

    def body(dc_ref, dcn_ref, val_ref, gate_ref, valp_ref, gatep_ref, dw_ref, dus_ref, dg_ref, dvg_ref, ddw_ref,
             ext_u, ext_d, acc):
        i = pl.program_id(0)

        @pl.when(i == 0)
        def _():
            acc[...] = jnp.zeros_like(acc)

        dvg_ref[:, 2 * D:2 * D + D_SSM] = dus_ref[...]
        dvg_ref[:, 2 * D + D_SSM:D_IN] = dg_ref[...]

        u_prev = valp_ref[...] * _sigmoid(gatep_ref[...])
        ext_u[0, 0:CONV_HALO, :] = jnp.where(i > 0, u_prev, 0.0)
        ext_u[0, CONV_HALO:CONV_HALO + tc, :] = val_ref[...] * _sigmoid(gate_ref[...])
        ext_d[0, 0:tc, :] = dc_ref[...]
        ext_d[0, tc:tc + CONV_HALO, :] = jnp.where(i < nt - 1, dcn_ref[...], 0.0)
        _fill_shifts(ext_u, CONV_HALO + tc)
        _fill_shifts(ext_d, CONV_HALO + tc)

        def rows(j, carry):
            r0 = pl.multiple_of(j * CONV_ROWS, CONV_ROWS)
            sl = pl.ds(r0, CONV_ROWS)
            du = _tap_sum(ext_d, dw_ref, [CONV_K - 1 - k for k in range(CONV_K)], r0, CONV_ROWS)
            sg = _sigmoid(gate_ref[sl, :])
            dvg_ref[sl, 0:D] = (du * sg).astype(BF16)
            dvg_ref[sl, D:2 * D] = (du * val_ref[sl, :] * sg * (1.0 - sg)).astype(BF16)
            return carry

        lax.fori_loop(0, tc // CONV_ROWS, rows, 0)

        for k0 in range(0, CONV_K, CONV_TAP_GROUP):
            ks = list(range(k0, min(k0 + CONV_TAP_GROUP, CONV_K)))

            def taps(j, accs, ks=ks):
                out = list(accs)
                for t in range(CONV_TILE_UNROLL):
                    r0 = pl.multiple_of((j * CONV_TILE_UNROLL + t) * 8, 8)
                    dct = dc_ref[pl.ds(r0, 8), :]
                    for q, k in enumerate(ks):
                        a, b = divmod(off + k, 8)
                        out[q] = out[q] + dct * ext_u[b, pl.ds(pl.multiple_of(r0 + 8 * a, 8), 8), :]
                return tuple(out)

            accs = lax.fori_loop(0, tc // (8 * CONV_TILE_UNROLL), taps, tuple(jnp.zeros((8, D), F32) for _ in ks))
            for k, a_k in zip(ks, accs):
                acc[k] += a_k

        @pl.when(i == nt - 1)
        def _():
            ddw_ref[...] = jnp.zeros_like(ddw_ref)
            for k in range(CONV_K):
                ddw_ref[k:k + 1, :] = jnp.sum(acc[k], axis=0, keepdims=True)

    return pl.pallas_call(
        body, grid=(nt,),
        in_specs=[_rows(tc, D), pl.BlockSpec((CONV_HALO, D), lambda i: (nxt(i), 0)),
                  _rows(tc, D, 0), _rows(tc, D, 1),
                  pl.BlockSpec((CONV_HALO, D), lambda i: (prev(i), 0)), pl.BlockSpec((CONV_HALO, D), lambda i: (prev(i), 1)),
                  _const((CONV_HALO, D)), _rows(tc, D_SSM), _rows(tc, 2 * D)],
        out_specs=[_rows(tc, D_IN), _const((CONV_HALO, D))],
        out_shape=[jax.ShapeDtypeStruct((T, D_IN), BF16), jax.ShapeDtypeStruct((CONV_HALO, D), F32)],
        scratch_shapes=[pltpu.VMEM((8, CONV_HALO + tc, D), F32), pltpu.VMEM((8, CONV_HALO + tc, D), F32),
                        pltpu.VMEM((CONV_K, 8, D), F32)],
        compiler_params=_cparams(("arbitrary",)), name=name)(dc, dc, p, p, p, p, dw, du_ssm, dgates)


GATE_A0 = (2 * D_MODEL + D_SSM) // 512
GATE_B0 = GATE_A0 + 2


def merge_fwd(p, ya, z, *, name):
    T = p.shape[0]
    D = D_MODEL
    tc = _pick(T, [1024, 512, 256, 128])
    W = 512

    def body(ga_ref, gb_ref, ya_ref, z1_ref, z2_ref, o_ref):
        yb = z1_ref[...].astype(F32) * _sigmoid(z2_ref[...].astype(F32))
        o_ref[...] = (_sigmoid(ga_ref[...]) * ya_ref[...].astype(F32) + _sigmoid(gb_ref[...]) * yb).astype(BF16)

    return pl.pallas_call(
        body, grid=(T // tc, D // W),
        in_specs=[pl.BlockSpec((tc, W), lambda i, j: (i, GATE_A0 + j)), pl.BlockSpec((tc, W), lambda i, j: (i, GATE_B0 + j)),
                  pl.BlockSpec((tc, W), lambda i, j: (i, j)), pl.BlockSpec((tc, W), lambda i, j: (i, j)),
                  pl.BlockSpec((tc, W), lambda i, j: (i, D // W + j))],
        out_specs=pl.BlockSpec((tc, W), lambda i, j: (i, j)),
        out_shape=jax.ShapeDtypeStruct((T, D), BF16),
        compiler_params=_cparams(("arbitrary", "arbitrary")), name=name)(p, p, ya, z, z)


def merge_bwd(dm, p, ya, z, *, name):
    T = p.shape[0]
    D = D_MODEL
    tc = _pick(T, [512, 256, 128])
    W = 512
    nb = D // W

    def body(dm_ref, ga0_ref, ga1_ref, gb0_ref, gb1_ref, ya_ref, z_ref, dya_ref, dg_ref, dz_ref):
        for j, (ga_ref, gb_ref) in enumerate(((ga0_ref, gb0_ref), (ga1_ref, gb1_ref))):
            c0 = slice(j * W, (j + 1) * W)
            c1 = slice(D + j * W, D + (j + 1) * W)
            dm = dm_ref[:, c0].astype(F32)
            sa = _sigmoid(ga_ref[...])
            sb = _sigmoid(gb_ref[...])
            s2 = _sigmoid(z_ref[:, c1].astype(F32))
            z1 = z_ref[:, c0].astype(F32)
            yb = z1 * s2
            dya_ref[:, c0] = (dm * sa).astype(BF16)
            dg_ref[:, c0] = (dm * ya_ref[:, c0].astype(F32) * sa * (1.0 - sa)).astype(BF16)
            dg_ref[:, c1] = (dm * yb * sb * (1.0 - sb)).astype(BF16)
            dyb = dm * sb
            dz_ref[:, c0] = (dyb * s2).astype(BF16)
            dz_ref[:, c1] = (dyb * z1 * s2 * (1.0 - s2)).astype(BF16)

    gate = lambda cb: pl.BlockSpec((tc, W), lambda i: (i, cb))
    return pl.pallas_call(
        body, grid=(T // tc,),
        in_specs=[_rows(tc, D), gate(GATE_A0), gate(GATE_A0 + 1), gate(GATE_B0), gate(GATE_B0 + 1), _rows(tc, D),
                  _rows(tc, 2 * D)],
        out_specs=[_rows(tc, D), _rows(tc, 2 * D), _rows(tc, 2 * D)],
        out_shape=[jax.ShapeDtypeStruct((T, D), BF16), jax.ShapeDtypeStruct((T, 2 * D), BF16),
                   jax.ShapeDtypeStruct((T, 2 * D), BF16)],
        compiler_params=_cparams(("arbitrary",)), name=name)(dm, p, p, p, p, ya, z)


def _scan_block(src_r, src_i, dst_r, dst_i, car_r, car_i, pw_r, pw_i, cw_r, cw_i, ntiles, reverse):
    W = src_r.shape[1]
    rows = lax.broadcasted_iota(jnp.int32, (8, W), 0)
    steps = []
    for d, pr in ((1, 0), (2, 1), (4, 3)):
        valid = rows < 8 - d if reverse else rows >= d
        steps.append((d, jnp.where(valid, jnp.broadcast_to(pw_r[pr:pr + 1, :], (8, W)), 0.0),
                      jnp.where(valid, jnp.broadcast_to(pw_i[pr:pr + 1, :], (8, W)), 0.0)))
    cw_r, cw_i = cw_r[...], cw_i[...]

    def tile(jj, carry):
        j = ntiles - 1 - jj if reverse else jj
        sl = pl.ds(pl.multiple_of(j * 8, 8), 8)
        xr, xi = src_r[sl, :], src_i[sl, :]
        for d, lr, li in steps:
            sr = pltpu.roll(xr, 8 - d if reverse else d, 0)
            si = pltpu.roll(xi, 8 - d if reverse else d, 0)
            xr, xi = xr + lr * sr - li * si, xi + lr * si + li * sr
        cr, ci = car_r[...], car_i[...]
        xr, xi = xr + cw_r * cr - cw_i * ci, xi + cw_r * ci + cw_i * cr
        dst_r[sl, :] = xr
        dst_i[sl, :] = xi
        edge = 0 if reverse else 7
        car_r[...] = jnp.broadcast_to(xr[edge:edge + 1, :], (8, W))
        car_i[...] = jnp.broadcast_to(xi[edge:edge + 1, :], (8, W))
        return carry

    return tile


SSM_SEGS = 8


def _ssm_tt(T):
    return _pick(T, [512, 256, 128])


def _seg_gather(src_ref, dst_ref, sl):
    for j in range(sl):
        dst_ref[8 * j:8 * j + 8, :] = src_ref[pl.ds(j, SSM_SEGS, stride=sl), :]


def _seg_scatter(val, dst_ref, sl):
    for j in range(sl):
        dst_ref[pl.ds(j, SSM_SEGS, stride=sl), :] = val[8 * j:8 * j + 8, :]


def _seg_tables(ar_ref, ai_ref, conj, pb_r, pb_i, pw_r, pw_i, cw_r, cw_i, sl):
    W = ar_ref.shape[1]
    lr = jnp.broadcast_to(ar_ref[...], (8, W))
    li = jnp.broadcast_to(ai_ref[...], (8, W))
    if conj:
        li = -li

    def power(j, cur):
        cr, ci = cur
        pb_r[j] = cr
        pb_i[j] = ci
        return cr * lr - ci * li, cr * li + ci * lr

    lax.fori_loop(0, sl, power, (lr, li))
    br, bi = pb_r[sl - 1], pb_i[sl - 1]
    rows = lax.broadcasted_iota(jnp.int32, (8, W), 0)
    cr, ci = br, bi
    tr, ti = jnp.zeros((8, W), F32), jnp.zeros((8, W), F32)
    ur, ui = tr, ti
    for r in range(8):
        tr, ti = jnp.where(rows == r, cr, tr), jnp.where(rows == r, ci, ti)
        ur, ui = jnp.where(rows == 7 - r, cr, ur), jnp.where(rows == 7 - r, ci, ui)
        cr, ci = cr * br - ci * bi, cr * bi + ci * br
    pw_r[...] = tr
    pw_i[...] = ti
    cw_r[...] = ur
    cw_i[...] = ui


def ssm_seg_fwd(p, Br, Bi, Cr, Ci, ar, ai, dvec, *, name):
    T = p.shape[0]
    tt = _ssm_tt(T)
    nt = T // tt
    sl = tt // SSM_SEGS
    WI, WS = SSM_BLOCK_IN, SSM_BLOCK_STATE
    u0 = 2 * D_MODEL // WI

    def body(u_ref, br_ref, bi_ref, cr_ref, ci_ref, ar_ref, ai_ref, d_ref, xr_ref, xi_ref, y_ref,
             bur, bui, useg, ynat, pb_r, pb_i, pw_r, pw_i, cw_r, cw_i, end_r, end_i, car_r, car_i):
        i = pl.program_id(1)

        @pl.when(i == 0)
        def _():
            _seg_tables(ar_ref, ai_ref, False, pb_r, pb_i, pw_r, pw_i, cw_r, cw_i, sl)
            car_r[...] = jnp.zeros_like(car_r)
            car_i[...] = jnp.zeros_like(car_i)

        _seg_gather(u_ref, useg, sl)
        u = useg[...]
        ub = u.astype(BF16)
        bur[...] = jnp.dot(ub, br_ref[...].astype(BF16), preferred_element_type=F32)
        bui[...] = jnp.dot(ub, bi_ref[...].astype(BF16), preferred_element_type=F32)
        lr = jnp.broadcast_to(ar_ref[...], (8, WS))
        li = jnp.broadcast_to(ai_ref[...], (8, WS))

        def step(j, st):
            sr, si = st
            rw = pl.ds(pl.multiple_of(j * 8, 8), 8)
            nr = lr * sr - li * si + bur[rw, :]
            ni = lr * si + li * sr + bui[rw, :]
            xr_ref[rw, :] = nr
            xi_ref[rw, :] = ni
            return nr, ni

        z8 = jnp.zeros((8, WS), F32)
        end_r[...], end_i[...] = lax.fori_loop(0, sl, step, (z8, z8))
        old_r, old_i = car_r[...], car_i[...]
        _scan_block(end_r, end_i, end_r, end_i, car_r, car_i, pw_r, pw_i, pw_r, pw_i, 1, False)(0, 0)
        rows = lax.broadcasted_iota(jnp.int32, (8, WS), 0)
        s_r = jnp.where(rows == 0, old_r, pltpu.roll(end_r[...], 1, 0))
        s_i = jnp.where(rows == 0, old_i, pltpu.roll(end_i[...], 1, 0))

        def fix(j, c):
            rw = pl.ds(pl.multiple_of(j * 8, 8), 8)
            pr, pi = pb_r[j], pb_i[j]
            xr_ref[rw, :] = xr_ref[rw, :] + pr * s_r - pi * s_i
            xi_ref[rw, :] = xi_ref[rw, :] + pr * s_i + pi * s_r
            return c

        lax.fori_loop(0, sl, fix, 0)
        y = (jnp.dot(xr_ref[...].astype(BF16), cr_ref[...].astype(BF16), preferred_element_type=F32)
             - jnp.dot(xi_ref[...].astype(BF16), ci_ref[...].astype(BF16), preferred_element_type=F32)
             + d_ref[...] * u)
        _seg_scatter(y, ynat, sl)
        y_ref[...] = ynat[...].astype(BF16)

    wspec = lambda shp: pl.BlockSpec((None,) + shp, lambda b, i: (b, 0, 0))
    vec = lambda w: pl.BlockSpec((1, w), lambda b, i: (0, b))
    tile8 = pltpu.VMEM((8, WS), F32)
    return pl.pallas_call(
        body, grid=(SSM_BLOCKS, nt),
        in_specs=[pl.BlockSpec((tt, WI), lambda b, i: (i, u0 + b)), wspec((WI, WS)), wspec((WI, WS)), wspec((WS, WI)),
                  wspec((WS, WI)), vec(WS), vec(WS), vec(WI)],
        out_specs=[pl.BlockSpec((tt, WS), lambda b, i: (i, b)), pl.BlockSpec((tt, WS), lambda b, i: (i, b)),
                   pl.BlockSpec((tt, WI), lambda b, i: (i, b))],
        out_shape=[jax.ShapeDtypeStruct((T, SSM_BLOCKS * WS), F32)] * 2 + [jax.ShapeDtypeStruct((T, D_SSM), BF16)],
        scratch_shapes=[pltpu.VMEM((tt, WS), F32), pltpu.VMEM((tt, WS), F32),
                        pltpu.VMEM((tt, WI), F32), pltpu.VMEM((tt, WI), F32),
                        pltpu.VMEM((sl, 8, WS), F32), pltpu.VMEM((sl, 8, WS), F32)] + [tile8] * 8,
        compiler_params=_cparams(("arbitrary", "arbitrary")), name=name)(p, Br, Bi, Cr, Ci, ar, ai, dvec)


def ssm_seg_bwd(dy, u, xr, xi, Br, Bi, Cr, Ci, ar, ai, dvec, *, name):
    T = u.shape[0]
    tt = _ssm_tt(T)
    nt = T // tt
    sl = tt // SSM_SEGS
    WI, WS = SSM_BLOCK_IN, SSM_BLOCK_STATE
    u0 = 2 * D_MODEL // WI
    tb = lambda i: nt - 1 - i
    xprev = lambda i: jnp.maximum(tb(i) * (tt // 8) - 1, 0)
    tn_dims = _DIMS["tn"]
    nt_dims = _DIMS["nt"]

    def body(dyn_ref, un_ref, xr_ref, xi_ref, xpr_ref, xpi_ref, br_ref, bi_ref, cr_ref, ci_ref, ar_ref, ai_ref, d_ref,
             du_ref, dbr_ref, dbi_ref, dcr_ref, dci_ref, dar_ref, dai_ref, dd_ref,
             gr, gi, ext_r, ext_i, dy_ref, u_ref, dunat, pb_r, pb_i, pw_r, pw_i, cw_r, cw_i, end_r, end_i, car_r, car_i):
        _seg_gather(dyn_ref, dy_ref, sl)
        _seg_gather(un_ref, u_ref, sl)
        i = pl.program_id(1)

        @pl.when(i == 0)
        def _():
            _seg_tables(ar_ref, ai_ref, True, pb_r, pb_i, pw_r, pw_i, cw_r, cw_i, sl)
            car_r[...] = jnp.zeros_like(car_r)
            car_i[...] = jnp.zeros_like(car_i)
            dbr_ref[...] = jnp.zeros_like(dbr_ref)
            dbi_ref[...] = jnp.zeros_like(dbi_ref)
            dcr_ref[...] = jnp.zeros_like(dcr_ref)
            dci_ref[...] = jnp.zeros_like(dci_ref)
            dar_ref[...] = jnp.zeros_like(dar_ref)
            dai_ref[...] = jnp.zeros_like(dai_ref)
            dd_ref[...] = jnp.zeros_like(dd_ref)

        dy = dy_ref[...]
        dyb = dy.astype(BF16)
        u = u_ref[...]
        ub = u.astype(BF16)
        gr[...] = lax.dot_general(dyb, cr_ref[...].astype(BF16), nt_dims, preferred_element_type=F32)
        gi[...] = -lax.dot_general(dyb, ci_ref[...].astype(BF16), nt_dims, preferred_element_type=F32)
        lr = jnp.broadcast_to(ar_ref[...], (8, WS))
        li = -jnp.broadcast_to(ai_ref[...], (8, WS))
        rows = lax.broadcasted_iota(jnp.int32, (8, WS), 0)

        def step(jj, st):
            sr, si = st
            rw = pl.ds(pl.multiple_of((sl - 1 - jj) * 8, 8), 8)
            nr = lr * sr - li * si + gr[rw, :]
            ni = lr * si + li * sr + gi[rw, :]
            gr[rw, :] = nr
            gi[rw, :] = ni
            return nr, ni

        z8 = jnp.zeros((8, WS), F32)
        end_r[...], end_i[...] = lax.fori_loop(0, sl, step, (z8, z8))
        old_r, old_i = car_r[...], car_i[...]
        _scan_block(end_r, end_i, end_r, end_i, car_r, car_i, pw_r, pw_i, cw_r, cw_i, 1, True)(0, 0)
        s_r = jnp.where(rows == 7, old_r, pltpu.roll(end_r[...], 7, 0))
        s_i = jnp.where(rows == 7, old_i, pltpu.roll(end_i[...], 7, 0))
        first = tb(i) == 0
        last_r, last_i = xr_ref[tt - 8:tt, :], xi_ref[tt - 8:tt, :]
        pv_r = jnp.where(first, 0.0, xpr_ref[...])
        pv_i = jnp.where(first, 0.0, xpi_ref[...])
        ext_r[0:8, :] = jnp.where(rows == 0, jnp.broadcast_to(pv_r[7:8, :], (8, WS)), pltpu.roll(last_r, 1, 0))
        ext_i[0:8, :] = jnp.where(rows == 0, jnp.broadcast_to(pv_i[7:8, :], (8, WS)), pltpu.roll(last_i, 1, 0))
        ext_r[8:8 + tt, :] = xr_ref[...]
        ext_i[8:8 + tt, :] = xi_ref[...]

        def fix(j, acc):
            a_r, a_i = acc
            rw = pl.ds(pl.multiple_of(j * 8, 8), 8)
            pr, pi = pb_r[sl - 1 - j], pb_i[sl - 1 - j]
            g_r = gr[rw, :] + pr * s_r - pi * s_i
            g_i = gi[rw, :] + pr * s_i + pi * s_r
            gr[rw, :] = g_r
            gi[rw, :] = g_i
            xp_r, xp_i = ext_r[rw, :], ext_i[rw, :]
            return a_r + g_r * xp_r + g_i * xp_i, a_i + g_i * xp_r - g_r * xp_i

        a_r, a_i = lax.fori_loop(0, sl, fix, (z8, z8))
        dar_ref[...] += a_r
        dai_ref[...] += a_i
        grb = gr[...].astype(BF16)
        gib = gi[...].astype(BF16)
        dbr_ref[...] += lax.dot_general(ub, grb, tn_dims, preferred_element_type=F32)
        dbi_ref[...] += lax.dot_general(ub, gib, tn_dims, preferred_element_type=F32)
        dcr_ref[...] += lax.dot_general(xr_ref[...].astype(BF16), dyb, tn_dims, preferred_element_type=F32)
        dci_ref[...] -= lax.dot_general(xi_ref[...].astype(BF16), dyb, tn_dims, preferred_element_type=F32)
        du = (lax.dot_general(grb, br_ref[...].astype(BF16), nt_dims, preferred_element_type=F32)
              + lax.dot_general(gib, bi_ref[...].astype(BF16), nt_dims, preferred_element_type=F32)
              + d_ref[...] * dy)
        _seg_scatter(du, dunat, sl)
        du_ref[...] = dunat[...].astype(BF16)
        dd_ref[...] += _rowsum8(dy * u)

    wspec = lambda shp: pl.BlockSpec((None,) + shp, lambda b, i: (b, 0, 0))
    vec = lambda w: pl.BlockSpec((1, w), lambda b, i: (0, b))
    tile8 = pltpu.VMEM((8, WS), F32)
    return pl.pallas_call(
        body, grid=(SSM_BLOCKS, nt),
        in_specs=[pl.BlockSpec((tt, WI), lambda b, i: (tb(i), b)), pl.BlockSpec((tt, WI), lambda b, i: (tb(i), u0 + b)),
                  pl.BlockSpec((tt, WS), lambda b, i: (tb(i), b)), pl.BlockSpec((tt, WS), lambda b, i: (tb(i), b)),
                  pl.BlockSpec((8, WS), lambda b, i: (xprev(i), b)), pl.BlockSpec((8, WS), lambda b, i: (xprev(i), b)),
                  wspec((WI, WS)), wspec((WI, WS)), wspec((WS, WI)), wspec((WS, WI)), vec(WS), vec(WS), vec(WI)],
        out_specs=[pl.BlockSpec((tt, WI), lambda b, i: (tb(i), b)),
                   wspec((WI, WS)), wspec((WI, WS)), wspec((WS, WI)), wspec((WS, WI)),
                   pl.BlockSpec((8, WS), lambda b, i: (0, b)), pl.BlockSpec((8, WS), lambda b, i: (0, b)),
                   pl.BlockSpec((8, WI), lambda b, i: (0, b))],
        out_shape=[jax.ShapeDtypeStruct((T, D_SSM), BF16),
                   jax.ShapeDtypeStruct((SSM_BLOCKS, WI, WS), F32), jax.ShapeDtypeStruct((SSM_BLOCKS, WI, WS), F32),
                   jax.ShapeDtypeStruct((SSM_BLOCKS, WS, WI), F32), jax.ShapeDtypeStruct((SSM_BLOCKS, WS, WI), F32),
                   jax.ShapeDtypeStruct((8, SSM_BLOCKS * WS), F32), jax.ShapeDtypeStruct((8, SSM_BLOCKS * WS), F32),
                   jax.ShapeDtypeStruct((8, D_SSM), F32)],
        scratch_shapes=[pltpu.VMEM((tt, WS), F32), pltpu.VMEM((tt, WS), F32),
                        pltpu.VMEM((tt + 8, WS), F32), pltpu.VMEM((tt + 8, WS), F32),
                        pltpu.VMEM((tt, WI), F32), pltpu.VMEM((tt, WI), F32), pltpu.VMEM((tt, WI), F32),
                        pltpu.VMEM((sl, 8, WS), F32), pltpu.VMEM((sl, 8, WS), F32)] + [tile8] * 8,
        compiler_params=_cparams(("arbitrary", "arbitrary")), name=name,
    )(dy, u, xr, xi, xr, xi, Br, Bi, Cr, Ci, ar, ai, dvec)


def _ssm_discretise(log_step, lam_re, lam_im, b_re, b_im):
    step = jnp.exp(log_step)[:, None]
    mag = jnp.exp(lam_re * step)
    ar = mag * jnp.cos(lam_im * step)
    ai = mag * jnp.sin(lam_im * step)
    den = lam_re * lam_re + lam_im * lam_im
    nr = ar - 1.0
    cr = (nr * lam_re + ai * lam_im) / den
    ci = (ai * lam_re - nr * lam_im) / den
    bbr = cr[..., None] * b_re - ci[..., None] * b_im
    bbi = cr[..., None] * b_im + ci[..., None] * b_re
    return ar, ai, bbr, bbi


def _blockdiag_in(bb):
    t = jnp.transpose(bb, (0, 2, 1)).reshape(SSM_BLOCKS, 8, SSM_GROUP, SSM_STATE)
    eye = jnp.eye(8, dtype=bb.dtype)
    return (t[:, :, :, None, :] * eye[None, :, None, :, None]).reshape(SSM_BLOCKS, SSM_BLOCK_IN, SSM_BLOCK_STATE)


def _blockdiag_out(cc):
    t = jnp.transpose(cc, (0, 2, 1)).reshape(SSM_BLOCKS, 8, SSM_STATE, SSM_GROUP)
    eye = jnp.eye(8, dtype=cc.dtype)
    return (t[:, :, :, None, :] * eye[None, :, None, :, None]).reshape(SSM_BLOCKS, SSM_BLOCK_STATE, SSM_BLOCK_IN)


def _diag_in(d):
    t = d.reshape(SSM_BLOCKS, 8, SSM_GROUP, 8, SSM_STATE)
    t = jnp.einsum("bghgp->bghp", t).reshape(SSM_GROUPS, SSM_GROUP, SSM_STATE)
    return jnp.transpose(t, (0, 2, 1))


def _diag_out(d):
    t = d.reshape(SSM_BLOCKS, 8, SSM_STATE, 8, SSM_GROUP)
    t = jnp.einsum("bgpgh->bgph", t).reshape(SSM_GROUPS, SSM_STATE, SSM_GROUP)
    return jnp.transpose(t, (0, 2, 1))


def attn_fwd(q, kv, *, name):
    T, D = q.shape
    nm = kv.shape[0]
    tq = _pick(T, [512, 256, 128])
    scale = HEAD_DIM ** -0.5

    def body(q_ref, k_ref, v_ref, o_ref):
        for h in range(N_HEADS):
            sl = slice(h * HEAD_DIM, (h + 1) * HEAD_DIM)
            s = lax.dot_general(q_ref[:, sl], k_ref[:, sl].astype(BF16), _DIMS["nt"], preferred_element_type=F32) * scale
            e = jnp.exp(s - jnp.max(s, axis=-1, keepdims=True))
            pr = e / jnp.sum(e, axis=-1, keepdims=True)
            o_ref[:, sl] = jnp.dot(pr.astype(BF16), v_ref[:, sl].astype(BF16), preferred_element_type=F32).astype(BF16)

    return pl.pallas_call(
        body, grid=(T // tq,),
        in_specs=[_rows(tq, D), pl.BlockSpec((nm, D), lambda i: (0, 0)), pl.BlockSpec((nm, D), lambda i: (0, 1))],
        out_specs=_rows(tq, D), out_shape=jax.ShapeDtypeStruct((T, D), BF16),
        compiler_params=_cparams(("arbitrary",)), name=name)(q, kv, kv)


def attn_bwd(q, kv, do, *, name):
    T, D = q.shape
    nm = kv.shape[0]
    tq = _pick(T, [512, 256, 128])
    nt = T // tq
    scale = HEAD_DIM ** -0.5

    def body(q_ref, k_ref, v_ref, do_ref, dq_ref, dkv_ref):
        i = pl.program_id(0)

        @pl.when(i == 0)
        def _():
            dkv_ref[...] = jnp.zeros_like(dkv_ref)

        for h in range(N_HEADS):
            sl = slice(h * HEAD_DIM, (h + 1) * HEAD_DIM)
            slv = slice(D + h * HEAD_DIM, D + (h + 1) * HEAD_DIM)
            qh = q_ref[:, sl]
            kh = k_ref[:, sl].astype(BF16)
            vh = v_ref[:, sl].astype(BF16)
            doh = do_ref[:, sl].astype(BF16)
            s = lax.dot_general(qh, kh, _DIMS["nt"], preferred_element_type=F32) * scale
            e = jnp.exp(s - jnp.max(s, axis=-1, keepdims=True))
            pr = e / jnp.sum(e, axis=-1, keepdims=True)
            dp = lax.dot_general(doh, vh, _DIMS["nt"], preferred_element_type=F32)
            ds = (pr * (dp - jnp.sum(pr * dp, axis=-1, keepdims=True)) * scale).astype(BF16)
            dq_ref[:, sl] = jnp.dot(ds, kh, preferred_element_type=F32).astype(BF16)
            dkv_ref[:, sl] += lax.dot_general(ds, qh, _DIMS["tn"], preferred_element_type=F32)
            dkv_ref[:, slv] += lax.dot_general(pr.astype(BF16), doh, _DIMS["tn"], preferred_element_type=F32)

    return pl.pallas_call(
        body, grid=(nt,),
        in_specs=[_rows(tq, D), pl.BlockSpec((nm, D), lambda i: (0, 0)), pl.BlockSpec((nm, D), lambda i: (0, 1)), _rows(tq, D)],
        out_specs=[_rows(tq, D), _const((nm, 2 * D))],
        out_shape=[jax.ShapeDtypeStruct((T, D), BF16), jax.ShapeDtypeStruct((nm, 2 * D), F32)],
        compiler_params=_cparams(("arbitrary",)), name=name)(q, kv, kv, do)


def _adam_math(w, g, m, v):
    m = ADAM_B1 * m + (1.0 - ADAM_B1) * g
    v = ADAM_B2 * v + (1.0 - ADAM_B2) * (g * g)
    m_hat = m / (1.0 - ADAM_B1 ** ADAM_STEP)
    v_hat = v / (1.0 - ADAM_B2 ** ADAM_STEP)
    delta = -ADAM_LR * (m_hat / (jnp.sqrt(v_hat) + ADAM_EPS) + ADAM_WD * w)
    return delta, m, v


def adamw(w, m, v, g_arr, g_row0, *, name):
    R, C = w.shape
    tr = _pick(R, [512, 256, 128, 64, 32, 16, 8])
    assert g_row0 % tr == 0
    g0 = g_row0 // tr

    def body(w_ref, m_ref, v_ref, g_ref, go_ref, d_ref, mo_ref, vo_ref):
        g = g_ref[...]
        d, mn, vn = _adam_math(w_ref[...], g, m_ref[...], v_ref[...])
        go_ref[...] = g
        d_ref[...] = d
        mo_ref[...] = mn
        vo_ref[...] = vn

    sp = pl.BlockSpec((tr, C), lambda i: (i, 0))
    return pl.pallas_call(
        body, grid=(R // tr,), in_specs=[sp, sp, sp, pl.BlockSpec((tr, C), lambda i: (g0 + i, 0))],
        out_specs=[sp] * 4, out_shape=[jax.ShapeDtypeStruct((R, C), F32)] * 4,
        compiler_params=_cparams(("arbitrary",)), name=name)(w, m, v, g_arr)


def _place():
    x, y, c = lax.axis_index("x"), lax.axis_index("y"), lax.axis_index("c")
    chips = [(1 - x, y), (x, 1 - y), (1 - x, 1 - y)]
    return x, y, c, chips


ANY = pl.BlockSpec(memory_space=pl.ANY)


def allgather_weights(bufs, *, name):
    n = len(bufs)

    def body(*refs):
        o_refs = refs[n:2 * n]
        send_sems, recv_sems, fsend_sems, frecv_sems = refs[2 * n:]
        x, y, c, chips = _place()
        k_me = 2 * x + y
        sib = (x, y, 1 - c)
        halves = [b.shape[1] // 2 for b in bufs]

        def half(a, cc):
            return pl.ds(pl.multiple_of(cc * halves[a], 16), halves[a])

        sends = []
        for a in range(n):
            for r, (px, py) in enumerate(chips):
                cp = pltpu.make_async_remote_copy(
                    src_ref=o_refs[a].at[k_me, half(a, c)], dst_ref=o_refs[a].at[k_me, half(a, c)],
                    send_sem=send_sems.at[3 * a + r], recv_sem=recv_sems.at[3 * a + r],
                    device_id=(px, py, c), device_id_type=MESH)
                cp.start()
                sends.append(cp)
        passed = []
        for a in range(n):
            for r, (px, py) in enumerate(chips):
                win = o_refs[a].at[2 * px + py, half(a, c)]
                pltpu.make_async_remote_copy(
                    src_ref=win, dst_ref=win, send_sem=send_sems.at[3 * a + r], recv_sem=recv_sems.at[3 * a + r],
                    device_id=(px, py, c), device_id_type=MESH).wait_recv()
                cp = pltpu.make_async_remote_copy(
                    src_ref=win, dst_ref=win, send_sem=fsend_sems.at[3 * a + r], recv_sem=frecv_sems.at[3 * a + r],
                    device_id=sib, device_id_type=MESH)
                cp.start()
                passed.append(cp)
        for a in range(n):
            for r, (px, py) in enumerate(chips):
                win = o_refs[a].at[2 * px + py, half(a, 1 - c)]
                pltpu.make_async_remote_copy(
                    src_ref=win, dst_ref=win, send_sem=fsend_sems.at[3 * a + r], recv_sem=frecv_sems.at[3 * a + r],
                    device_id=sib, device_id_type=MESH).wait_recv()
        for cp in sends + passed:
            cp.wait_send()

    return pl.pallas_call(
        body, in_specs=[ANY] * n, out_specs=[ANY] * n,
        out_shape=[jax.ShapeDtypeStruct(b.shape, b.dtype) for b in bufs],
        scratch_shapes=[pltpu.SemaphoreType.DMA((3 * n,))] * 4,
        input_output_aliases={a: a for a in range(n)},
        name=name)(*bufs)


HBM_SPEC = pl.BlockSpec(memory_space=pltpu.HBM)
SEM_SPEC = pl.BlockSpec(memory_space=pltpu.SEMAPHORE)


def _hbm(a):
    return pltpu.with_memory_space_constraint(a, pltpu.HBM)


def gather_start(bufs, pieces, *, name):
    n = len(bufs)
    npc = len(pieces)

    def body(*refs):
        b_refs = refs[:n]
        send_sems, recv_sems = refs[n], refs[n + 1]
        x, y, c, chips = _place()
        k_me = 2 * x + y
        for q, (a, row0, rows) in enumerate(pieces):
            win = b_refs[a].at[k_me, pl.ds(row0, rows)]
            for r, (px, py) in enumerate(chips):
                pltpu.make_async_remote_copy(
                    src_ref=win, dst_ref=win, send_sem=send_sems.at[3 * q + r], recv_sem=recv_sems.at[3 * q + r],
                    device_id=(px, py, c), device_id_type=MESH).start()

    return pl.pallas_call(
        body, in_specs=[HBM_SPEC] * n, out_specs=[SEM_SPEC, SEM_SPEC] + [HBM_SPEC] * n,
        out_shape=[pltpu.SemaphoreType.DMA((3 * npc,)), pltpu.SemaphoreType.DMA((3 * npc,))]
        + [pltpu.HBM(b.shape, b.dtype) for b in bufs],
        input_output_aliases={a: 2 + a for a in range(n)},
        compiler_params=pltpu.CompilerParams(has_side_effects=pltpu.SideEffectType.DATAFLOW_SIDE_EFFECTING),
        name=name)(*[_hbm(b) for b in bufs])


def gather_wait(send_sems, recv_sems, bufs, which, after, *, name):
    n = len(bufs)

    def body(*refs):
        b_refs = refs[:n]
        send_sems, recv_sems = refs[n], refs[n + 1]
        x, y, c, chips = _place()
        k_me = 2 * x + y
        for a, row0, rows, q in which:
            for r, (px, py) in enumerate(chips):
                cp = pltpu.make_async_remote_copy(
                    src_ref=b_refs[a].at[k_me, pl.ds(row0, rows)], dst_ref=b_refs[a].at[2 * px + py, pl.ds(row0, rows)],
                    send_sem=send_sems.at[3 * q + r], recv_sem=recv_sems.at[3 * q + r],
                    device_id=(px, py, c), device_id_type=MESH)
                cp.wait_send()
                cp.wait_recv()

    return pl.pallas_call(
        body, in_specs=[HBM_SPEC] * n + [SEM_SPEC, SEM_SPEC, ANY], out_specs=[HBM_SPEC] * n,
        out_shape=[pltpu.HBM(b.shape, b.dtype) for b in bufs],
        input_output_aliases={a: a for a in range(n)},
        compiler_params=pltpu.CompilerParams(has_side_effects=pltpu.SideEffectType.DATAFLOW_SIDE_EFFECTING),
        name=name)(*bufs, send_sems, recv_sems, after)


N_PEERS = N_DEV - 1


def _scatter_copies(p_refs, l_refs, send_sems, recv_sems):
    x, y, c, _ = _place()
    cps = []
    for a in range(len(p_refs)):
        h = p_refs[a].shape[1] // 2
        for fx, fy in ((0, 0), (1, 0), (0, 1), (1, 1)):
            for fc in (0, 1):
                if (fx, fy, fc) == (0, 0, 0):
                    continue
                slot = 2 * (fx + 2 * fy) + fc - 1
                px, py, pc = (1 - x if fx else x), (1 - y if fy else y), (1 - c if fc else c)
                cps.append(pltpu.make_async_remote_copy(
                    src_ref=p_refs[a].at[2 * px + py, pl.ds(pl.multiple_of(pc * h, 16), h)], dst_ref=l_refs[a].at[slot],
                    send_sem=send_sems.at[N_PEERS * a + slot], recv_sem=recv_sems.at[N_PEERS * a + slot],
                    device_id=(px, py, pc), device_id_type=MESH))
    return cps


def scatter_start(parts, *, name):
    n = len(parts)
    lands = [lax.empty((N_PEERS, p.shape[1] // 2, p.shape[2]), p.dtype) for p in parts]

    def body(*refs):
        for cp in _scatter_copies(refs[:n], refs[n:2 * n], refs[2 * n], refs[2 * n + 1]):
            cp.start()

    outs = pl.pallas_call(
        body, in_specs=[HBM_SPEC] * (2 * n), out_specs=[SEM_SPEC, SEM_SPEC] + [HBM_SPEC] * (2 * n),
        out_shape=[pltpu.SemaphoreType.DMA((N_PEERS * n,)), pltpu.SemaphoreType.DMA((N_PEERS * n,))]
        + [pltpu.HBM(a.shape, a.dtype) for a in parts + lands],
        input_output_aliases={a: 2 + a for a in range(2 * n)},
        compiler_params=pltpu.CompilerParams(has_side_effects=pltpu.SideEffectType.DATAFLOW_SIDE_EFFECTING),
        name=name)(*[_hbm(a) for a in parts + lands])
    return outs[0], outs[1], list(outs[2:2 + n]), list(outs[2 + n:])


def scatter_wait(rounds, after, *, name):
    sizes = [len(r[2]) for r in rounds]
    flat = [a for r in rounds for a in r[2] + r[3]]
    sems = [s for r in rounds for s in (r[0], r[1])]
    nflat = len(flat)

    def body(*refs):
        pos = 0
        for ri, n in enumerate(sizes):
            for cp in _scatter_copies(refs[pos:pos + n], refs[pos + n:pos + 2 * n], refs[nflat + 2 * ri], refs[nflat + 2 * ri + 1]):
                cp.wait_send()
                cp.wait_recv()
            pos += 2 * n

    outs = pl.pallas_call(
        body, in_specs=[HBM_SPEC] * nflat + [SEM_SPEC] * len(sems) + [ANY], out_specs=[HBM_SPEC] * nflat,
        out_shape=[pltpu.HBM(a.shape, a.dtype) for a in flat],
        input_output_aliases={a: a for a in range(nflat)},
        compiler_params=pltpu.CompilerParams(has_side_effects=pltpu.SideEffectType.DATAFLOW_SIDE_EFFECTING),
        name=name)(*flat, *sems, after)
    res, pos = [], 0
    for n in sizes:
        res.append((list(outs[pos:pos + n]), list(outs[pos + n:pos + 2 * n])))
        pos += 2 * n
    return res


def join_halves(fulls, *, name):
    n = len(fulls)

    def body(*refs):
        o_refs = refs[n:2 * n]
        send_sems, recv_sems = refs[2 * n:]
        x, y, c, _ = _place()
        cps = []
        for a in range(n):
            h = fulls[a].shape[0] // 2
            win = o_refs[a].at[pl.ds(pl.multiple_of(c * h, 8), h)]
            cp = pltpu.make_async_remote_copy(
                src_ref=win, dst_ref=win, send_sem=send_sems.at[a], recv_sem=recv_sems.at[a],
                device_id=(x, y, 1 - c), device_id_type=MESH)
            cp.start()
            cps.append(cp)
        for a in range(n):
            h = fulls[a].shape[0] // 2
            other = o_refs[a].at[pl.ds(pl.multiple_of((1 - c) * h, 8), h)]
            pltpu.make_async_remote_copy(
                src_ref=other, dst_ref=other, send_sem=send_sems.at[a], recv_sem=recv_sems.at[a],
                device_id=(x, y, 1 - c), device_id_type=MESH).wait_recv()
        for cp in cps:
            cp.wait_send()

    return pl.pallas_call(
        body, in_specs=[ANY] * n, out_specs=[ANY] * n,
        out_shape=[jax.ShapeDtypeStruct(f.shape, f.dtype) for f in fulls],
        scratch_shapes=[pltpu.SemaphoreType.DMA((n,))] * 2,
        input_output_aliases={a: a for a in range(n)},
        name=name)(*fulls)


def add_partials(part, land, kc, *, name):
    _, R, C = part.shape
    H = R // 2
    tr = _pick(H, [256, 128, 64, 32, 16])
    per = H // tr

    def body(kc_ref, p_ref, l_ref, o_ref):
        acc = p_ref[...].astype(F32)
        for s in range(N_PEERS):
            acc = acc + l_ref[s].astype(F32)
        o_ref[...] = acc

    return pl.pallas_call(
        body,
        grid_spec=pltpu.PrefetchScalarGridSpec(
            num_scalar_prefetch=1, grid=(per,),
            in_specs=[pl.BlockSpec((None, tr, C), lambda i, kc_ref: (kc_ref[0], kc_ref[1] * per + i, 0)),
                      pl.BlockSpec((N_PEERS, tr, C), lambda i, kc_ref: (0, i, 0))],
            out_specs=pl.BlockSpec((tr, C), lambda i, kc_ref: (kc_ref[1] * per + i, 0))),
        out_shape=jax.ShapeDtypeStruct((R, C), F32),
        compiler_params=_cparams(("arbitrary",)), name=name)(kc, part, land)


def allreduce_two_level(v, *, name):
    m, n = v.shape
    h = m // 2

    def body(x_ref, out_ref, sib_ref, chip_ref, sems_send, sems_recv):
        x, y, c, chips = _place()
        k_me = 2 * x + y
        sib = (x, y, 1 - c)
        mine = pl.ds(pl.multiple_of(c * h, 8), h)
        other = pl.ds(pl.multiple_of((1 - c) * h, 8), h)

        def copy(q, src, dst, to):
            return pltpu.make_async_remote_copy(src_ref=src, dst_ref=dst, send_sem=sems_send.at[q], recv_sem=sems_recv.at[q],
                                                device_id=to, device_id_type=MESH)

        first = copy(0, x_ref.at[other], sib_ref, sib)
        first.start()
        first.wait()
        chip_ref[k_me] = x_ref[mine, :] + sib_ref[...]
        sends = [copy(1 + r, chip_ref.at[k_me], chip_ref.at[k_me], (px, py, c)) for r, (px, py) in enumerate(chips)]
        for cp in sends:
            cp.start()
        for r, (px, py) in enumerate(chips):
            copy(1 + r, chip_ref.at[2 * px + py], chip_ref.at[2 * px + py], (px, py, c)).wait_recv()
        for cp in sends:
            cp.wait_send()
        total = ((chip_ref[0] + chip_ref[1]) + chip_ref[2]) + chip_ref[3]
        out_ref[mine, :] = total
        last = copy(4, out_ref.at[mine], out_ref.at[mine], sib)
        last.start()
        copy(4, out_ref.at[other], out_ref.at[other], sib).wait_recv()
        last.wait_send()

    vm = pl.BlockSpec(memory_space=pltpu.VMEM)
    return pl.pallas_call(
        body, in_specs=[vm], out_specs=vm, out_shape=jax.ShapeDtypeStruct((m, n), v.dtype),
        scratch_shapes=[pltpu.VMEM((h, n), v.dtype), pltpu.VMEM((N_CHIPS, h, n), v.dtype),
                        pltpu.SemaphoreType.DMA((5,)), pltpu.SemaphoreType.DMA((5,))],
        compiler_params=pltpu.CompilerParams(vmem_limit_bytes=VMEM_LIMIT_BYTES), name=name)(v)


PACK_W = D_MODEL


def _pack_rows(shape):
    return -(-math.prod(shape) // PACK_W)


def _pack(arrs):
    cols = []
    for a in arrs:
        f = a.reshape(-1)
        pad = (-f.shape[0]) % PACK_W
        cols.append((jnp.pad(f, (0, pad)) if pad else f).reshape(-1, PACK_W))
    out = jnp.concatenate(cols, axis=0)
    pad = (-out.shape[0]) % 16
    return jnp.pad(out, ((0, pad), (0, 0)))


def _unpack(buf, shapes):
    outs, r = [], 0
    for s in shapes:
        nel = math.prod(s)
        nr = _pack_rows(s)
        outs.append(buf[r:r + nr].reshape(-1)[:nel].reshape(s))
        r += nr
    return outs


GA_CONV_OUT, GA_MIX_OUT, GA_WQ, GA_WO, GA_DOWN, GA_UP, GA_ROWS = 0, 256, 512, 768, 1024, 2048, 3072
G1_DOWN, G1_UP, G1_ROWS = 0, 1024, 2048
G2_CONV_OUT, G2_MIX_OUT, G2_WQ, G2_WO, G2_ROWS = 0, 256, 512, 768, 1024


def kernel(x, mem, in_norm_g, in_norm_b, w_in, conv_dw, conv_db, conv_norm_g, conv_norm_b, w_conv_out, ssm_log_step, ssm_lambda_re, ssm_lambda_im, ssm_b_re, ssm_b_im, ssm_c_re, ssm_c_im, ssm_d, w_ssm_glu, w_mix_out, ln1_g, ln1_b, xa_wq, xa_wkv, xa_wo, ln2_g, ln2_b, mlp_w_up, mlp_w_down, ln3_g, ln3_b, loss_target, m_in_norm_g, m_in_norm_b, m_w_in, m_conv_dw, m_conv_db, m_conv_norm_g, m_conv_norm_b, m_w_conv_out, m_ssm_log_step, m_ssm_lambda_re, m_ssm_lambda_im, m_ssm_b_re, m_ssm_b_im, m_ssm_c_re, m_ssm_c_im, m_ssm_d, m_w_ssm_glu, m_w_mix_out, m_ln1_g, m_ln1_b, m_xa_wq, m_xa_wkv, m_xa_wo, m_ln2_g, m_ln2_b, m_mlp_w_up, m_mlp_w_down, m_ln3_g, m_ln3_b, v_in_norm_g, v_in_norm_b, v_w_in, v_conv_dw, v_conv_db, v_conv_norm_g, v_conv_norm_b, v_w_conv_out, v_ssm_log_step, v_ssm_lambda_re, v_ssm_lambda_im, v_ssm_b_re, v_ssm_b_im, v_ssm_c_re, v_ssm_c_im, v_ssm_d, v_w_ssm_glu, v_w_mix_out, v_ln1_g, v_ln1_b, v_xa_wq, v_xa_wkv, v_xa_wo, v_ln2_g, v_ln2_b, v_mlp_w_up, v_mlp_w_down, v_ln3_g, v_ln3_b):
    D = D_MODEL
    xs = x[0]
    T = xs.shape[0]
    mems = mem[0]
    NM = mems.shape[0]
    tgt = loss_target[0]
    my_c = lax.axis_index("c")
    k_me = 2 * lax.axis_index("x") + lax.axis_index("y")
    c_arr = jnp.reshape(my_c, (1,)).astype(jnp.int32)
    k_arr = jnp.reshape(k_me, (1,)).astype(jnp.int32)

    sh_a = jnp.concatenate([w_conv_out[0], w_mix_out[0], xa_wq[0], xa_wo[0], mlp_w_down[0], mlp_w_up[0]], axis=0).astype(BF16)
    def own_block(shard):
        buf = lax.empty((N_CHIPS,) + shard.shape, shard.dtype)
        return lax.dynamic_update_slice(buf, shard[None], (k_me, 0, 0))

    dw_pad = jnp.pad(conv_dw[0], ((0, CONV_HALO - CONV_K), (0, 0)))
    (GIN,) = allgather_weights([own_block(w_in[0].astype(BF16))], name="gather_w_in")
    ag_bufs = [own_block(sh_a), GIN] + [own_block(s) for s in (xa_wkv[0].astype(BF16), w_ssm_glu[0].astype(BF16), dw_pad)]
    ag_pieces = [(4, 0, CONV_HALO), (0, GA_CONV_OUT, 256), (3, 0, D_SSM), (0, GA_MIX_OUT, 256), (0, GA_WQ, 256),
                 (2, 0, D), (0, GA_WO, 256), (0, GA_UP, D), (0, GA_DOWN, D)]
    ag_send, ag_recv, GA, GIN, GKV, GGLU, GDW = gather_start(ag_bufs, ag_pieces, name="gather_start")

    def w_rowshard(row0):
        return dict(b_spec=((N_CHIPS, 256, D), lambda i, j, k: (0, row0 // 256, 0)), b_view=(D, D), tn=D, tk=D)

    h0, h0b = ln_fwd(xs, in_norm_g, in_norm_b, name="ln0_fwd")
    p = mm_nn(h0b, GIN, ((None, D, 1152), lambda i, j, k: (j, 0, 0)), D_IN, tn=1152, tk=D, name="mm_w_in")[0]
    GA, GGLU, GDW = gather_wait(
        ag_send, ag_recv, [GA, GGLU, GDW],
        [(2, 0, CONV_HALO, 0), (0, GA_CONV_OUT, 256, 1), (1, 0, D_SSM, 2), (0, GA_MIX_OUT, 256, 3)], p, name="gather_wait_mixer")
    dw_taps = jnp.transpose(GDW, (1, 0, 2)).reshape(CONV_HALO, D)
    c_pre, actb = conv_fwd(p, dw_taps, conv_db, conv_norm_g[0].reshape(1, D), conv_norm_b[0].reshape(1, D), name="conv_fwd")
    ya = mm_nn(actb, GA, N=D, out_dtype=BF16, name="mm_conv_out", **w_rowshard(GA_CONV_OUT))[0]

    lstep, lre, lim = ssm_log_step[0], ssm_lambda_re[0], ssm_lambda_im[0]
    bre, bim, cre, cim = ssm_b_re[0], ssm_b_im[0], ssm_c_re[0], ssm_c_im[0]
    (ar, ai, bbr, bbi), disc_vjp = jax.vjp(_ssm_discretise, lstep, lre, lim, bre, bim)
    Br, Bi = _blockdiag_in(bbr), _blockdiag_in(bbi)
    Cr, Ci = _blockdiag_out(cre), _blockdiag_out(cim)
    lam_r, lam_i = ar.reshape(1, -1), ai.reshape(1, -1)
    dvec = ssm_d[0].reshape(1, D_SSM)
    xr, xi, yssm = ssm_seg_fwd(p, Br, Bi, Cr, Ci, lam_r, lam_i, dvec, name="ssm_fwd")
    z = mm_nn(yssm, GGLU, ((None, D_SSM, 512), lambda i, j, k: (j, 0, 0)), 2 * D, tn=512, tk=D_SSM, out_dtype=BF16,
              name="mm_ssm_glu")[0]
    mergedb = merge_fwd(p, ya, z, name="merge_fwd")
    tm_ln = _pick(T, [512, 256, 128])
    row_spec = ((1, D), lambda i, j, k: (0, 0))

    def ln_epilogue(acc, res, g, b):
        r = ALPHA * res + acc
        xhat, _ = _ln_stats(r)
        h = xhat * g + b
        return r, h, h

    def mm_ln(a, row0, res, g, b, name):
        return mm_nn(a, GA, N=D, tm=tm_ln, extras=(res, g.reshape(1, D), b.reshape(1, D)),
                     extra_specs=[_mn(tm_ln, D), row_spec, row_spec], epilogue=ln_epilogue, out_dtypes=[F32, F32, BF16],
                     name=name, **w_rowshard(row0))

    r1, h1, h1b = mm_ln(mergedb, GA_MIX_OUT, h0, ln1_g[0], ln1_b[0], "mm_mix_out_ln1")
    GA, GKV = gather_wait(ag_send, ag_recv, [GA, GKV], [(0, GA_WQ, 256, 4), (1, 0, D, 5), (0, GA_WO, 256, 6)], r1,
                          name="gather_wait_attn")

    qb = mm_nn(h1b, GA, N=D, out_dtype=BF16, name="mm_wq", **w_rowshard(GA_WQ))[0]
    kv = mm_nn(mems, GKV, ((None, D, 512), lambda i, j, k: (j, 0, 0)), 2 * D, tn=512, tk=D, name="mm_wkv")[0]
    ob = attn_fwd(qb, kv, name="attn_fwd")
    r2, h2, h2b = mm_ln(ob, GA_WO, h1, ln2_g[0], ln2_b[0], "mm_wo_ln2")
    (GA,) = gather_wait(ag_send, ag_recv, [GA], [(0, GA_UP, D, 7), (0, GA_DOWN, D, 8)], r2, name="gather_wait_mlp")

    def relu2(acc):
        zr = jnp.maximum(acc, 0.0)
        return (zr * zr,)

    zzb = mm_nn(h2b, GA, ((None, D, D), lambda i, j, k: (j, GA_UP // D, 0)), D_FF, tn=D, tk=D,
                out_dtype=BF16, epilogue=relu2, name="mm_up")[0]
    ff = mm_nn(zzb, GA, ((N_CHIPS, D, D), lambda i, j, k: (0, GA_DOWN // D, 0)), D, tm=_pick(T, [512, 256, 128]), tn=D, tk=D_FF,
               b_view=(D_FF, D), name="mm_down")[0]
    dr3, dr3b, dg3, db3, sq = ln_loss_bwd(ff, h2, ln3_g[0], ln3_b[0], tgt, name="ln3_loss_bwd")

    def rs_begin(grads, rnd):
        return scatter_start(grads, name=f"rs{rnd}_scatter_start")

    g1_shape = jax.ShapeDtypeStruct((N_CHIPS, G1_ROWS, D), BF16)
    g2_shape = jax.ShapeDtypeStruct((N_CHIPS, G2_ROWS, D), BF16)
    dzpreb = mm_nt(dr3b, GA, ((None, D, D), lambda i, j, k: (j, GA_DOWN // D, 0)), D_FF, tn=D, tk=D, out_dtype=BF16,
                   extras=(zzb,), epilogue=lambda acc, zz: (acc * (2.0 * jnp.sqrt(zz.astype(F32))),), name="mm_down_t")[0]
    G1g = mm_tn(zzb, dr3b, tm=D, tn=D, tk=T, out_spec=((None, D, D), lambda i, j, k: (i, G1_DOWN // D, 0)),
                out_shape=g1_shape, name="mm_down_g")
    G1g = mm_tn(h2b, dzpreb, tm=D, tn=D, tk=T, out_spec=((None, D, D), lambda i, j, k: (j, G1_UP // D, 0)),
                out_shape=g1_shape, out_buf=G1g, name="mm_up_g")
    round1 = rs_begin([G1g], 1)
    dh2 = mm_nt(dzpreb, GA, ((N_CHIPS, D, D), lambda i, j, k: (0, GA_UP // D, 0)), D, tm=_pick(T, [512, 256, 128]), tn=D,
                tk=D_FF, b_chunks=N_CHIPS, extras=(dr3,), epilogue=lambda acc, d: (acc + ALPHA * d,),
                after=(round1[2][0],), name="mm_up_t")[0]
    dr2, dr2b, dg2, db2 = ln_bwd(r2, dh2, ln2_g[0], name="ln2_bwd")

    def g_rowshard(row0, out_buf):
        return dict(tm=D, tn=D, out_spec=((N_CHIPS, 256, D), lambda i, j, k: (0, row0 // 256, 0)), out_shape=g2_shape,
                    out_buf=out_buf)

    dob = mm_nt(dr2b, GA, N=D, out_dtype=BF16, name="mm_wo_t", **w_rowshard(GA_WO))[0]
    G2g = mm_tn(ob, dr2b, name="mm_wo_g", **g_rowshard(G2_WO, None))
    dqb, dkv = attn_bwd(qb, kv, dob, name="attn_bwd")
    G2g = mm_tn(h1b, dqb, name="mm_wq_g", **g_rowshard(G2_WQ, G2g))
    GKVg = mm_tn(mems, dkv, tm=D, tn=512, tk=NM, out_spec=((None, D, 512), lambda i, j, k: (j, 0, 0)),
                 out_shape=jax.ShapeDtypeStruct((N_CHIPS, D, 512), BF16), name="mm_wkv_g")
    dh1 = mm_nt(dqb, GA, N=D, extras=(dr2,), epilogue=lambda acc, d: (acc + ALPHA * d,), name="mm_wq_t",
                **w_rowshard(GA_WQ))[0]
    dr1, dr1b, dg1, db1 = ln_bwd(r1, dh1, ln1_g[0], name="ln1_bwd")

    dmerged = mm_nt(dr1b, GA, N=D, out_dtype=BF16, name="mm_mix_t", **w_rowshard(GA_MIX_OUT))[0]
    G2g = mm_tn(mergedb, dr1b, name="mm_mix_g", **g_rowshard(G2_MIX_OUT, G2g))
    dyab, dgatesb, dzb = merge_bwd(dmerged, p, ya, z, name="merge_bwd")
    GGLUg = mm_tn(yssm, dzb, tm=D_SSM, tn=512, out_spec=((None, D_SSM, 512), lambda i, j, k: (j, 0, 0)),
                  out_shape=jax.ShapeDtypeStruct((N_CHIPS, D_SSM, 512), BF16), name="mm_glu_g")
    dyssm = mm_nt(dzb, GGLU, ((N_CHIPS, D_SSM, 512), lambda i, j, k: (0, 0, 0)), D_SSM, tn=D_SSM, tk=2 * D, b_chunks=N_CHIPS,
                  name="mm_glu_t")[0]
    dub, dBr, dBi, dCr, dCi, dar8, dai8, dd8 = ssm_seg_bwd(dyssm, p, xr, xi, Br, Bi, Cr, Ci, lam_r, lam_i, dvec,
                                                           name="ssm_bwd")
    dar = jnp.sum(dar8, axis=0).reshape(SSM_GROUPS, SSM_STATE)
    dai = jnp.sum(dai8, axis=0).reshape(SSM_GROUPS, SSM_STATE)
    g_lstep, g_lre, g_lim, g_bre, g_bim = disc_vjp((dar, dai, _diag_in(dBr), _diag_in(dBi)))
    g_cre, g_cim = _diag_out(dCr), _diag_out(dCi)
    g_d = jnp.sum(dd8, axis=0).reshape(1, D_SSM)

    dact = mm_nt(dyab, GA, N=D, name="mm_conv_out_t", **w_rowshard(GA_CONV_OUT))[0]
    G2g = mm_tn(actb, dyab, name="mm_conv_out_g", **g_rowshard(G2_CONV_OUT, G2g))
    round2 = rs_begin([G2g, GKVg, GGLUg], 2)
    dc, dng, dnb, ddb = conv_bwd_norm(dact, c_pre, conv_norm_g[0].reshape(1, D), conv_norm_b[0].reshape(1, D),
                                      round2[2][0], name="conv_bwd_norm")
    dpb, ddw = conv_bwd_taps(dc, p, dw_taps, dub, dgatesb, name="conv_bwd_taps")
    GINg = mm_tn(h0b, dpb, tm=D, tn=1152, tk=T, out_spec=((None, D, 1152), lambda i, j, k: (j, 0, 0)),
                 out_shape=jax.ShapeDtypeStruct((N_CHIPS, D, 1152), BF16), name="mm_w_in_g")
    round3 = rs_begin([GINg], 3)
    dh0 = mm_nt(dpb, GIN, ((N_CHIPS, D, 1152), lambda i, j, k: (0, 0, 0)), D, tm=_pick(T, [512, 256, 128]), tn=D, tk=D_IN,
                b_chunks=N_CHIPS, extras=(dr1,), epilogue=lambda acc, d: (acc + ALPHA * d,), after=(round3[2][0],),
                name="mm_w_in_t")[0]
    gx, dg0, db0 = ln_bwd(xs, dh0, in_norm_g, bf16_copy=False, name="ln0_bwd")

    kc_arr = jnp.concatenate([k_arr, c_arr])
    landed = scatter_wait([round1, round2, round3], gx, name="rs_scatter_wait")
    tags = ["mlp", "sq", "kv", "glu", "in"]
    pairs = [(pt, l2) for parts, lands2 in landed for pt, l2 in zip(parts, lands2)]
    halves = [add_partials(pt, l2, kc_arr, name="rs_add_partials_" + t) for (pt, l2), t in zip(pairs, tags)]
    g1, g2, gKV, gGLU, gIN = join_halves(halves, name="rs_join_halves")

    small_names = ["in_norm_g", "in_norm_b", "conv_db", "conv_norm_g", "conv_norm_b", "ssm_log_step", "ssm_lambda_re",
                   "ssm_lambda_im", "ssm_b_re", "ssm_b_im", "ssm_c_re", "ssm_c_im", "ssm_d", "ln1_g", "ln1_b",
                   "ln2_g", "ln2_b", "ln3_g", "ln3_b"]
    small_w = [in_norm_g, in_norm_b, conv_db, conv_norm_g, conv_norm_b, ssm_log_step, ssm_lambda_re, ssm_lambda_im,
               ssm_b_re, ssm_b_im, ssm_c_re, ssm_c_im, ssm_d, ln1_g, ln1_b, ln2_g, ln2_b, ln3_g, ln3_b]
    small_m = [m_in_norm_g, m_in_norm_b, m_conv_db, m_conv_norm_g, m_conv_norm_b, m_ssm_log_step, m_ssm_lambda_re,
               m_ssm_lambda_im, m_ssm_b_re, m_ssm_b_im, m_ssm_c_re, m_ssm_c_im, m_ssm_d, m_ln1_g, m_ln1_b, m_ln2_g,
               m_ln2_b, m_ln3_g, m_ln3_b]
    small_v = [v_in_norm_g, v_in_norm_b, v_conv_db, v_conv_norm_g, v_conv_norm_b, v_ssm_log_step, v_ssm_lambda_re,
               v_ssm_lambda_im, v_ssm_b_re, v_ssm_b_im, v_ssm_c_re, v_ssm_c_im, v_ssm_d, v_ln1_g, v_ln1_b, v_ln2_g,
               v_ln2_b, v_ln3_g, v_ln3_b]
    small_g = [dg0, db0, ddb, dng, dnb, g_lstep, g_lre, g_lim, g_bre, g_bim, g_cre, g_cim, g_d, dg1, db1, dg2, db2, dg3, db3]
    small_shapes = [w.shape for w in small_w]
    n_small_rows = _pack(small_w).shape[0]
    packed_g = _pack(small_g + [ddw, sq])
    summed = allreduce_two_level(packed_g, name="allreduce_small")
    small_rows = sum(_pack_rows(s) for s in small_shapes)
    dw_rows = _pack_rows((CONV_HALO, D))
    loss = 0.5 * summed[small_rows + dw_rows, 0] / D
    ddw_full = summed[small_rows:small_rows + dw_rows].reshape(CONV_HALO, D)
    g_dw = lax.dynamic_slice_in_dim(ddw_full, k_me * (D // N_CHIPS), D // N_CHIPS, axis=1)
    gs_packed = jnp.pad(summed[:small_rows], ((0, n_small_rows - small_rows), (0, 0)))

    res = {}

    def upd(nm, w, m, v, g_arr, row0=0):
        shp = w.shape
        w2, m2, v2 = (a.reshape(-1, shp[-1]) for a in (w, m, v))
        outs = adamw(w2, m2, v2, g_arr, row0, name="adamw_" + nm)
        res[nm] = tuple(o.reshape(shp) for o in outs)

    upd("w_conv_out", w_conv_out, m_w_conv_out, v_w_conv_out, g2, G2_CONV_OUT)
    upd("w_mix_out", w_mix_out, m_w_mix_out, v_w_mix_out, g2, G2_MIX_OUT)
    upd("xa_wq", xa_wq, m_xa_wq, v_xa_wq, g2, G2_WQ)
    upd("xa_wo", xa_wo, m_xa_wo, v_xa_wo, g2, G2_WO)
    upd("mlp_w_down", mlp_w_down, m_mlp_w_down, v_mlp_w_down, g1, G1_DOWN)
    upd("mlp_w_up", mlp_w_up, m_mlp_w_up, v_mlp_w_up, g1, G1_UP)
    upd("w_in", w_in, m_w_in, v_w_in, gIN)
    upd("xa_wkv", xa_wkv, m_xa_wkv, v_xa_wkv, gKV)
    upd("w_ssm_glu", w_ssm_glu, m_w_ssm_glu, v_w_ssm_glu, gGLU)
    pad_dw = lambda a: jnp.pad(a[0], ((0, CONV_HALO - CONV_K), (0, 0)))
    dw_outs = adamw(pad_dw(conv_dw), pad_dw(m_conv_dw), pad_dw(v_conv_dw), g_dw, 0, name="adamw_conv_dw")
    res["conv_dw"] = tuple(o[:CONV_K][None] for o in dw_outs)
    sm_outs = adamw(_pack(small_w), _pack(small_m), _pack(small_v), gs_packed, 0, name="adamw_small")
    sm_un = [_unpack(o, small_shapes) for o in sm_outs]
    for idx, nm in enumerate(small_names):
        res[nm] = tuple(sm_un[q][idx] for q in range(4))

    order = ["in_norm_g", "in_norm_b", "w_in", "conv_dw", "conv_db", "conv_norm_g", "conv_norm_b", "w_conv_out",
             "ssm_log_step", "ssm_lambda_re", "ssm_lambda_im", "ssm_b_re", "ssm_b_im", "ssm_c_re", "ssm_c_im", "ssm_d",
             "w_ssm_glu", "w_mix_out", "ln1_g", "ln1_b", "xa_wq", "xa_wkv", "xa_wo", "ln2_g", "ln2_b", "mlp_w_up",
             "mlp_w_down", "ln3_g", "ln3_b"]
    return (loss, gx[None], *[res[n][0] for n in order], *[res[n][1] for n in order],
            *[res[n][2] for n in order], *[res[n][3] for n in order])
```
